```python
import math
import jax, jax.numpy as jnp
from jax import lax
import numpy as np

D_MODEL = 2048
BATCH = 16
SEQ = 256
DEPTH = 2
DEC_BATCH = 2
DEC_SEQ = 4096
PAST_LEN = 512

GRID_W = 64
EPS = 1e-6
ROPE_BASE = 10000.0
Q_BLOCK = 128
N_MIXERS = 4
GROUP_W = D_MODEL // N_MIXERS
D_MIX = N_MIXERS * GROUP_W
N_MOD = 6

S5_CH = GROUP_W
S5_GROUP_CH = 16
S5_GROUPS = S5_CH // S5_GROUP_CH
S5_STATE = 64
MLA_HEADS = 4
MLA_NOPE = 128
MLA_ROPE = 64
MLA_V = GROUP_W // MLA_HEADS
MLA_Q_RANK = GROUP_W
MLA_KV_RANK = GROUP_W // 2
MLA_SCALE = (MLA_NOPE + MLA_ROPE) ** -0.5
LRU_W = GROUP_W
LRU_BLOCKS = 8
LRU_BLOCK = LRU_W // LRU_BLOCKS
LRU_CONV = 4
LRU_C = 8.0
RET_HEADS = 4
RET_DK = GROUP_W // RET_HEADS
RET_DV = GROUP_W // RET_HEADS
RET_CHUNK = 128
N_EXPERTS = 16
N_EXPERT_GROUPS = 4
EXPERTS_PER_GROUP = N_EXPERTS // N_EXPERT_GROUPS
TOP_K = 2
D_EXPERT = D_MODEL // 4

PART_SIZES = (S5_CH, MLA_Q_RANK, MLA_KV_RANK, MLA_ROPE, LRU_W, LRU_W,
              RET_HEADS * RET_DK, RET_HEADS * RET_DK, RET_HEADS * RET_DV, RET_HEADS * RET_DV)
D_IN = S5_CH + MLA_Q_RANK + MLA_KV_RANK + MLA_ROPE + 2 * LRU_W + 2 * RET_HEADS * RET_DK + 2 * RET_HEADS * RET_DV

kernel_name = 'hybrid_diffusion_prefix_trunk_step'


def _rmsnorm(x, g=None):
    xf = x.astype(jnp.float32)
    y = xf * lax.rsqrt(jnp.mean(xf * xf, axis=-1, keepdims=True) + EPS)
    if g is not None:
        y = y * g.astype(jnp.float32)
    return y.astype(x.dtype)


def _grid_angles(t_len, rot_dim):
    rows = t_len // GRID_W
    row = jnp.repeat(jnp.arange(rows, dtype=jnp.float32), GRID_W)
    col = jnp.tile(jnp.arange(GRID_W, dtype=jnp.float32), rows)
    n_freq = rot_dim // 4
    inv = ROPE_BASE ** (-jnp.arange(n_freq, dtype=jnp.float32) / n_freq)
    return row[:, None] * inv[None], col[:, None] * inv[None]


def _rotate(x, ang):
    n = x.shape[-1] // 2
    cos = jnp.cos(ang)[None, :, None, :]
    sin = jnp.sin(ang)[None, :, None, :]
    x1, x2 = x[..., :n], x[..., n:]
    return jnp.concatenate([x1 * cos - x2 * sin, x2 * cos + x1 * sin], axis=-1)


def _axial_rope(x):
    t_len, rot = x.shape[1], x.shape[-1]
    ang_r, ang_c = _grid_angles(t_len, rot)
    xf = x.astype(jnp.float32)
    half = rot // 2
    out = jnp.concatenate([_rotate(xf[..., :half], ang_r), _rotate(xf[..., half:], ang_c)], axis=-1)
    return out.astype(x.dtype)


def _blocked_attention(q, k, v, scale):
    bn, tq, nh, dq = q.shape
    nb = tq // Q_BLOCK
    qb = q.reshape(bn, nb, Q_BLOCK, nh, dq).transpose(1, 0, 2, 3, 4)

    def _one(q_blk):
        s = jnp.einsum('bqhd,bkhd->bhqk', q_blk, k, preferred_element_type=jnp.float32) * scale
        p = jax.nn.softmax(s, axis=-1)
        return jnp.einsum('bhqk,bkhv->bqhv', p.astype(v.dtype), v)

    out = lax.map(_one, qb)
    return out.transpose(1, 0, 2, 3, 4).reshape(bn, tq, nh, v.shape[-1])


def _linear_combine(e1, e2):
    a1, b1 = e1
    a2, b2 = e2
    return a2 * a1, a2 * b1 + b2


def _s5_mixer(u, p, h0):
    f32 = jnp.float32
    bn, t_len, _ = u.shape
    ug = u.astype(f32).reshape(bn, t_len, S5_GROUPS, S5_GROUP_CH)
    b_mat = lax.complex(p['s5_b_re'].astype(f32), p['s5_b_im'].astype(f32))
    bu = jnp.einsum('gpc,btgc->btgp', b_mat, ug.astype(jnp.complex64))
    hs, finals = [], []
    for d in range(2):
        lam = lax.complex(p['s5_a_re'][d].astype(f32), p['s5_a_im'][d].astype(f32))
        dt = jnp.exp(p['s5_log_dt'][d].astype(f32))[:, None]
        a_bar = jnp.exp(lam * dt)
        b_in = ((a_bar - 1.0) / lam) * bu
        a_in = jnp.broadcast_to(a_bar, b_in.shape)
        if d == 1:
            a_in, b_in = jnp.flip(a_in, 1), jnp.flip(b_in, 1)
        a_cum, h = lax.associative_scan(_linear_combine, (a_in, b_in), axis=1)
        if h0 is not None:
            h0c = lax.complex(h0[:, d, ..., 0].astype(f32), h0[:, d, ..., 1].astype(f32))
            h = h + a_cum * h0c[:, None]
        else:
            finals.append(h[:, -1])
        if d == 1:
            h = jnp.flip(h, 1)
        hs.append(h)
    c_mat = lax.complex(p['s5_c_re'].astype(f32), p['s5_c_im'].astype(f32))
    y = jnp.real(jnp.einsum('gcp,btgp->btgc', c_mat, hs[0] + hs[1])).reshape(bn, t_len, S5_CH)
    y = y + p['s5_d'].astype(f32) * u.astype(f32)
    y = jax.nn.gelu(y)
    y = y * jax.nn.sigmoid(y @ p['s5_w_glu'].astype(f32))
    fin = None
    if h0 is None:
        fs = jnp.stack(finals, axis=1)
        fin = jnp.stack([jnp.real(fs), jnp.imag(fs)], axis=-1).astype(u.dtype)
    return y.astype(u.dtype), fin


def _rglru_mixer(xb, gb, p, h0):
    f32 = jnp.float32
    bn, t_len, w = xb.shape
    left = LRU_CONV // 2
    conv = lax.conv_general_dilated(xb.astype(f32), p['lru_conv_w'].astype(f32)[:, None, :],
                                    window_strides=(1,), padding=[(left, LRU_CONV - 1 - left)],
                                    dimension_numbers=('NWC', 'WIO', 'NWC'), feature_group_count=w)
    xc = conv + p['lru_conv_b'].astype(f32)
    xblk = xc.reshape(bn, t_len, LRU_BLOCKS, LRU_BLOCK)
    hs, finals = [], []
    for d in range(2):
        r = jax.nn.sigmoid(jnp.einsum('btnk,nkj->btnj', xblk, p['lru_w_a'][d].astype(f32)).reshape(bn, t_len, w)
                           + p['lru_b_a'][d].astype(f32))
        i = jax.nn.sigmoid(jnp.einsum('btnk,nkj->btnj', xblk, p['lru_w_x'][d].astype(f32)).reshape(bn, t_len, w)
                           + p['lru_b_x'][d].astype(f32))
        log_a = -LRU_C * r * jax.nn.softplus(-p['lru_lambda'][d].astype(f32))
        a = jnp.exp(log_a)
        b = jnp.sqrt(-jnp.expm1(2.0 * log_a)) * (i * xc)
        if d == 1:
            a, b = jnp.flip(a, 1), jnp.flip(b, 1)
        a_cum, h = lax.associative_scan(_linear_combine, (a, b), axis=1)
        if h0 is not None:
            h = h + a_cum * h0[:, d].astype(f32)[:, None]
        else:
            finals.append(h[:, -1])
        if d == 1:
            h = jnp.flip(h, 1)
        hs.append(h)
    y = (hs[0] + hs[1]) * jax.nn.gelu(gb.astype(f32))
    fin = jnp.stack(finals, axis=1).astype(xb.dtype) if h0 is None else None
    return y.astype(xb.dtype), fin


def _chunk_retention(q, k, v, s0, inclusive):
    f32 = jnp.float32
    bn, t_len, nh, _ = q.shape
    n_chunks = t_len // RET_CHUNK
    log_g = jnp.log1p(-jnp.exp2(-5.0 - jnp.arange(nh, dtype=f32)))
    idx = jnp.arange(RET_CHUNK, dtype=f32)
    diff = idx[:, None] - idx[None, :]
    mask = (diff >= 0) if inclusive else (diff > 0)
    decay = jnp.where(mask[None], jnp.exp(jnp.maximum(diff, 0.0)[None] * log_g[:, None, None]), 0.0)
    xi = jnp.exp((idx + 1.0)[:, None] * log_g[None])
    zeta = jnp.exp((RET_CHUNK - 1.0 - idx)[:, None] * log_g[None])
    g_chunk = jnp.exp(RET_CHUNK * log_g)

    def to_chunks(a):
        return a.reshape(bn, n_chunks, RET_CHUNK, nh, a.shape[-1]).transpose(1, 0, 2, 3, 4)

    def step(s, qkv):
        qc, kc, vc = qkv
        scores = jnp.einsum('bqhd,bkhd->bhqk', qc, kc) * decay
        inner = jnp.einsum('bhqk,bkhv->bqhv', scores, vc)
        cross = jnp.einsum('bqhd,bhdv->bqhv', qc, s) * xi[None, :, :, None]
        s_new = g_chunk[None, :, None, None] * s + jnp.einsum('bkhd,bkhv->bhdv', kc * zeta[None, :, :, None], vc)
        return s_new, inner + cross

    s_fin, out = lax.scan(step, s0, (to_chunks(q), to_chunks(k), to_chunks(v)))
    return out.transpose(1, 0, 2, 3, 4).reshape(bn, t_len, nh, v.shape[-1]), s_fin


def _retention_mixer(q, k, v, g, s0):
    f32 = jnp.float32
    bn, t_len, _ = q.shape
    q = q.reshape(bn, t_len, RET_HEADS, RET_DK)
    k = k.reshape(bn, t_len, RET_HEADS, RET_DK)
    v = v.reshape(bn, t_len, RET_HEADS, RET_DV)
    if s0 is not None:
        q, k = _axial_rope(q), _axial_rope(k)
        s0 = s0.astype(f32)
    else:
        s0 = jnp.zeros((bn, 2, RET_HEADS, RET_DK, RET_DV), f32)
    qf, kf, vf = q.astype(f32), k.astype(f32) * (RET_DK ** -0.5), v.astype(f32)
    o_f, s_f = _chunk_retention(qf, kf, vf, s0[:, 0], True)
    o_b, s_b = _chunk_retention(jnp.flip(qf, 1), jnp.flip(kf, 1), jnp.flip(vf, 1), s0[:, 1], False)
    o = _rmsnorm(o_f + jnp.flip(o_b, 1)).reshape(bn, t_len, RET_HEADS * RET_DV)
    o = o * jax.nn.silu(g.astype(f32))
    return o.astype(g.dtype), jnp.stack([s_f, s_b], axis=1).astype(g.dtype)


def _mla_expand(c_kv, w_ukv):
    bn, t_len, _ = c_kv.shape
    kv = (c_kv @ w_ukv).reshape(bn, t_len, MLA_HEADS, MLA_NOPE + MLA_V)
    return kv[..., :MLA_NOPE], kv[..., MLA_NOPE:]


def _mla_mixer(q_lat, kv_lat, k_rope, p, ckv_ctx, kpe_ctx):
    latent = ckv_ctx is not None
    bn, t_len, _ = q_lat.shape
    q = (_rmsnorm(q_lat, p['mla_q_norm_g']) @ p['mla_w_uq']).reshape(bn, t_len, MLA_HEADS, MLA_NOPE + MLA_ROPE)
    q_nope, q_pe = q[..., :MLA_NOPE], q[..., MLA_NOPE:]
    c_kv = _rmsnorm(kv_lat, p['mla_kv_norm_g'])
    k_pe = k_rope[:, :, None, :]
    if latent:
        q_pe, k_pe = _axial_rope(q_pe), _axial_rope(k_pe)
    k_nope, v = _mla_expand(c_kv, p['mla_w_ukv'])
    k = jnp.concatenate([k_nope, jnp.broadcast_to(k_pe, (bn, t_len, MLA_HEADS, MLA_ROPE))], axis=-1)
    if latent:
        t_ctx = ckv_ctx.shape[1]
        kn_c, v_c = _mla_expand(ckv_ctx, p['mla_w_ukv'])
        k_c = jnp.concatenate([kn_c, jnp.broadcast_to(kpe_ctx[:, :, None, :], (bn, t_ctx, MLA_HEADS, MLA_ROPE))], axis=-1)
        k = jnp.concatenate([k_c, k], axis=1)
        v = jnp.concatenate([v_c, v], axis=1)
    out = _blocked_attention(jnp.concatenate([q_nope, q_pe], axis=-1), k, v, MLA_SCALE)
    return out.reshape(bn, t_len, MLA_HEADS * MLA_V), c_kv


def _moe(h, router_w, router_b, p):
    f32 = jnp.float32
    bn, t_len, d = h.shape
    t = h.reshape(bn * t_len, d)
    scores = jax.nn.softmax((t @ router_w).astype(f32), axis=-1)
    sel = scores + router_b.astype(f32)
    grp = sel.reshape(-1, N_EXPERT_GROUPS, EXPERTS_PER_GROUP)
    grp_score = lax.top_k(grp, TOP_K)[0].sum(-1)
    g_idx = jnp.argmax(grp_score, axis=-1)
    in_grp = jnp.take_along_axis(grp, g_idx[:, None, None], axis=1)[:, 0]
    _, local = lax.top_k(in_grp, TOP_K)
    e_idx = g_idx[:, None] * EXPERTS_PER_GROUP + local
    w = jnp.take_along_axis(scores, e_idx, axis=-1)
    w = w / jnp.sum(w, axis=-1, keepdims=True)
    combine = jnp.sum(jax.nn.one_hot(e_idx, N_EXPERTS, dtype=f32) * w[..., None], axis=1)
    gate = jnp.einsum('nd,edf->nef', t, p['moe_w_gate'])
    up = jnp.einsum('nd,edf->nef', t, p['moe_w_up'])
    act = jax.nn.silu(gate) * up * combine.astype(t.dtype)[..., None]
    y = jnp.einsum('nef,efd->nd', act, p['moe_w_down'])
    return y.reshape(bn, t_len, d)


def _layer(x, mod, lp, router_w, router_b, ctx):
    bn, t_len, d = x.shape
    mod = mod.reshape(mod.shape[0], 1, N_MOD, d)
    sh1, sc1, g1, sh2, sc2, g2 = (mod[:, :, j] for j in range(N_MOD))
    h = _rmsnorm(x, lp['norm1_g']) * (1.0 + sc1) + sh1
    proj = h @ lp['w_in']
    cuts = [int(cut) for cut in np.cumsum(PART_SIZES)[:-1]]
    s5_u, mla_q, mla_kv, mla_kr, lru_x, lru_g, ret_q, ret_k, ret_v, ret_g = jnp.split(proj, cuts, axis=-1)
    if ctx is None:
        ckv_c = kpe_c = s5_h0 = lru_h0 = ret_s0 = None
    else:
        ckv_c, kpe_c, s5_h0, lru_h0, ret_s0 = ctx
    y_s5, s5_fin = _s5_mixer(s5_u, lp, s5_h0)
    y_mla, c_kv = _mla_mixer(mla_q, mla_kv, mla_kr, lp, ckv_c, kpe_c)
    y_lru, lru_fin = _rglru_mixer(lru_x, lru_g, lp, lru_h0)
    y_ret, ret_fin = _retention_mixer(ret_q, ret_k, ret_v, ret_g, ret_s0)
    mix = jnp.concatenate([y_s5, y_mla, y_lru, y_ret], axis=-1) @ lp['w_out']
    x = x + g1 * mix
    h2 = _rmsnorm(x, lp['norm2_g']) * (1.0 + sc2) + sh2
    x = x + g2 * _moe(h2, router_w, router_b, lp)
    return x, (c_kv, mla_kr, s5_fin, lru_fin, ret_fin)


def setup_inputs(seed: int = 0) -> dict:
    key = jax.random.key(seed)
    ks = iter(jax.random.split(key, 64))
    f32 = jnp.float32

    def nrm(shape, scale):
        return scale * jax.random.normal(next(ks), shape, f32)

    d = D_MODEL
    inp = {}
    inp['x_prompt'] = nrm((BATCH, SEQ, d), 1.0)
    inp['x_sample'] = nrm((DEC_BATCH, DEC_SEQ, d), 1.0)
    inp['c'] = nrm((DEC_BATCH, d), 1.0)
    inp['cache_mla_ckv'] = nrm((DEC_BATCH, DEPTH, PAST_LEN, MLA_KV_RANK), 1.0)
    inp['cache_mla_kpe'] = nrm((DEC_BATCH, DEPTH, PAST_LEN, MLA_ROPE), 1.0)
    inp['state_s5'] = nrm((DEC_BATCH, DEPTH, 2, S5_GROUPS, S5_STATE, 2), 0.3)
    inp['state_lru'] = nrm((DEC_BATCH, DEPTH, 2, LRU_W), 0.5)
    inp['state_ret'] = nrm((DEC_BATCH, DEPTH, 2, RET_HEADS, RET_DK, RET_DV), 0.3)
    inp['c_ctx'] = nrm((d,), 1.0)
    inp['w_ada'] = nrm((DEPTH, d, N_MOD * d), 0.5 * d ** -0.5)
    inp['b_ada'] = nrm((DEPTH, N_MOD * d), 0.02)
    inp['norm1_g'] = 1.0 + nrm((DEPTH, d), 0.02)
    inp['norm2_g'] = 1.0 + nrm((DEPTH, d), 0.02)
    inp['w_in'] = nrm((DEPTH, d, D_IN), d ** -0.5)
    inp['w_out'] = nrm((DEPTH, D_MIX, d), D_MIX ** -0.5)
    inp['s5_a_re'] = -0.5 + nrm((DEPTH, 2, S5_GROUPS, S5_STATE), 0.01)
    inp['s5_a_im'] = math.pi * jnp.arange(S5_STATE, dtype=f32) + nrm((DEPTH, 2, S5_GROUPS, S5_STATE), 0.01)
    inp['s5_log_dt'] = jax.random.uniform(next(ks), (DEPTH, 2, S5_GROUPS), f32, math.log(1e-3), math.log(1e-1))
    inp['s5_b_re'] = nrm((DEPTH, S5_GROUPS, S5_STATE, S5_GROUP_CH), (2 * S5_GROUP_CH) ** -0.5)
    inp['s5_b_im'] = nrm((DEPTH, S5_GROUPS, S5_STATE, S5_GROUP_CH), (2 * S5_GROUP_CH) ** -0.5)
    inp['s5_c_re'] = nrm((DEPTH, S5_GROUPS, S5_GROUP_CH, S5_STATE), S5_STATE ** -0.5)
    inp['s5_c_im'] = nrm((DEPTH, S5_GROUPS, S5_GROUP_CH, S5_STATE), S5_STATE ** -0.5)
    inp['s5_d'] = nrm((DEPTH, S5_CH), 1.0)
    inp['s5_w_glu'] = nrm((DEPTH, S5_CH, S5_CH), S5_CH ** -0.5)
    inp['mla_q_norm_g'] = 1.0 + nrm((DEPTH, MLA_Q_RANK), 0.02)
    inp['mla_w_uq'] = nrm((DEPTH, MLA_Q_RANK, MLA_HEADS * (MLA_NOPE + MLA_ROPE)), MLA_Q_RANK ** -0.5)
    inp['mla_kv_norm_g'] = 1.0 + nrm((DEPTH, MLA_KV_RANK), 0.02)
    inp['mla_w_ukv'] = nrm((DEPTH, MLA_KV_RANK, MLA_HEADS * (MLA_NOPE + MLA_V)), MLA_KV_RANK ** -0.5)
    inp['lru_conv_w'] = nrm((DEPTH, LRU_CONV, LRU_W), LRU_CONV ** -0.5)
    inp['lru_conv_b'] = nrm((DEPTH, LRU_W), 0.02)
    inp['lru_w_a'] = nrm((DEPTH, 2, LRU_BLOCKS, LRU_BLOCK, LRU_BLOCK), LRU_BLOCK ** -0.5)
    inp['lru_b_a'] = nrm((DEPTH, 2, LRU_W), 0.02)
    inp['lru_w_x'] = nrm((DEPTH, 2, LRU_BLOCKS, LRU_BLOCK, LRU_BLOCK), LRU_BLOCK ** -0.5)
    inp['lru_b_x'] = nrm((DEPTH, 2, LRU_W), 0.02)
    s = jax.random.uniform(next(ks), (DEPTH, 2, LRU_W), f32, 0.9, 0.999) ** (1.0 / LRU_C)
    inp['lru_lambda'] = jnp.log(s) - jnp.log1p(-s)
    inp['router_w'] = nrm((d, N_EXPERTS), d ** -0.5)
    inp['router_b'] = nrm((N_EXPERTS,), 0.01)
    inp['moe_w_gate'] = nrm((DEPTH, N_EXPERTS, d, D_EXPERT), d ** -0.5)
    inp['moe_w_up'] = nrm((DEPTH, N_EXPERTS, d, D_EXPERT), d ** -0.5)
    inp['moe_w_down'] = nrm((DEPTH, N_EXPERTS, D_EXPERT, d), D_EXPERT ** -0.5)
    inp['final_norm_g'] = 1.0 + nrm((d,), 0.02)
    return inp


def reference(x_prompt, x_sample, c, cache_mla_ckv, cache_mla_kpe, state_s5, state_lru, state_ret,
              c_ctx, w_ada, b_ada, norm1_g, norm2_g, w_in, w_out,
              s5_a_re, s5_a_im, s5_log_dt, s5_b_re, s5_b_im, s5_c_re, s5_c_im, s5_d, s5_w_glu,
              mla_q_norm_g, mla_w_uq, mla_kv_norm_g, mla_w_ukv,
              lru_conv_w, lru_conv_b, lru_w_a, lru_b_a, lru_w_x, lru_b_x, lru_lambda,
              router_w, router_b, moe_w_gate, moe_w_up, moe_w_down, final_norm_g):
    stacked = dict(w_ada=w_ada, b_ada=b_ada, norm1_g=norm1_g, norm2_g=norm2_g, w_in=w_in, w_out=w_out,
                   s5_a_re=s5_a_re, s5_a_im=s5_a_im, s5_log_dt=s5_log_dt, s5_b_re=s5_b_re, s5_b_im=s5_b_im,
                   s5_c_re=s5_c_re, s5_c_im=s5_c_im, s5_d=s5_d, s5_w_glu=s5_w_glu,
                   mla_q_norm_g=mla_q_norm_g, mla_w_uq=mla_w_uq, mla_kv_norm_g=mla_kv_norm_g, mla_w_ukv=mla_w_ukv,
                   lru_conv_w=lru_conv_w, lru_conv_b=lru_conv_b, lru_w_a=lru_w_a, lru_b_a=lru_b_a,
                   lru_w_x=lru_w_x, lru_b_x=lru_b_x, lru_lambda=lru_lambda,
                   moe_w_gate=moe_w_gate, moe_w_up=moe_w_up, moe_w_down=moe_w_down)

    x = x_prompt
    layer_states = []
    for l in range(DEPTH):
        lp = {name: arr[l] for name, arr in stacked.items()}
        mod = jax.nn.silu(c_ctx)[None] @ lp['w_ada'] + lp['b_ada']
        x, st = _layer(x, mod, lp, router_w, router_b, None)
        layer_states.append(st)
    y_prompt = _rmsnorm(x, final_norm_g)
    new_cache_mla_ckv = jnp.stack([st[0] for st in layer_states], axis=1)
    new_cache_mla_kpe = jnp.stack([st[1] for st in layer_states], axis=1)
    new_state_s5 = jnp.stack([st[2] for st in layer_states], axis=1)
    new_state_lru = jnp.stack([st[3] for st in layer_states], axis=1)
    new_state_ret = jnp.stack([st[4] for st in layer_states], axis=1)

    x = x_sample
    for l in range(DEPTH):
        lp = {name: arr[l] for name, arr in stacked.items()}
        mod = jax.nn.silu(c) @ lp['w_ada'] + lp['b_ada']
        ctx = (cache_mla_ckv[:, l], cache_mla_kpe[:, l], state_s5[:, l], state_lru[:, l], state_ret[:, l])
        x, _ = _layer(x, mod, lp, router_w, router_b, ctx)
    y_sample = _rmsnorm(x, final_norm_g)

    return (y_prompt, y_sample, new_cache_mla_ckv, new_cache_mla_kpe, new_state_s5, new_state_lru, new_state_ret)
```

```python
import math
import functools
import numpy as np
import jax
import jax.numpy as jnp
from jax import lax
from jax.experimental import pallas as pl
from jax.experimental.pallas import tpu as pltpu

D_MODEL = 2048
BATCH = 16
SEQ = 256
DEPTH = 2
DEC_BATCH = 2
DEC_SEQ = 4096
PAST_LEN = 512
GRID_W = 64
EPS = 1e-6
ROPE_BASE = 10000.0
Q_BLOCK = 128
N_MIXERS = 4
GROUP_W = D_MODEL // N_MIXERS
D_MIX = N_MIXERS * GROUP_W
N_MOD = 6
S5_CH = GROUP_W
S5_GROUP_CH = 16
S5_GROUPS = S5_CH // S5_GROUP_CH
S5_STATE = 64
MLA_HEADS = 4
MLA_NOPE = 128
MLA_ROPE = 64
MLA_V = GROUP_W // MLA_HEADS
MLA_Q_RANK = GROUP_W
MLA_KV_RANK = GROUP_W // 2
MLA_SCALE = (MLA_NOPE + MLA_ROPE) ** -0.5
LRU_W = GROUP_W
LRU_BLOCKS = 8
LRU_BLOCK = LRU_W // LRU_BLOCKS
LRU_CONV = 4
LRU_C = 8.0
RET_HEADS = 4
RET_DK = GROUP_W // RET_HEADS
RET_DV = GROUP_W // RET_HEADS
RET_CHUNK = 128
N_EXPERTS = 16
N_EXPERT_GROUPS = 4
EXPERTS_PER_GROUP = N_EXPERTS // N_EXPERT_GROUPS
TOP_K = 2
D_EXPERT = D_MODEL // 4
PART_SIZES = (S5_CH, MLA_Q_RANK, MLA_KV_RANK, MLA_ROPE, LRU_W, LRU_W,
              RET_HEADS * RET_DK, RET_HEADS * RET_DK, RET_HEADS * RET_DV, RET_HEADS * RET_DV)
D_IN = sum(PART_SIZES)

V7X_VMEM_LIMIT = 48 * 1024 * 1024


def _mm_kernel(x_ref, w_ref, o_ref):
    o_ref[...] = jnp.dot(x_ref[...].astype(jnp.bfloat16), w_ref[...].astype(jnp.bfloat16),
                         preferred_element_type=jnp.float32).astype(o_ref.dtype)


def _matmul(x, w, tm=512, tn=512, out_dtype=jnp.float32):
    m, k = x.shape
    _, n = w.shape
    tm = min(tm, m)
    tn = min(tn, n)
    if n % tn:
        tn = 256
    assert m % tm == 0 and n % tn == 0
    return pl.pallas_call(
        _mm_kernel,
        grid=(n // tn, m // tm),
        in_specs=[pl.BlockSpec((tm, k), lambda j, i: (i, 0)),
                  pl.BlockSpec((k, tn), lambda j, i: (0, j))],
        out_specs=pl.BlockSpec((tm, tn), lambda j, i: (i, j)),
        out_shape=jax.ShapeDtypeStruct((m, n), out_dtype),
        compiler_params=pltpu.CompilerParams(
            dimension_semantics=("arbitrary", "arbitrary"), vmem_limit_bytes=V7X_VMEM_LIMIT),
        name="matmul",
    )(x, w)


def _mm3(x, w, **kw):
    b, t, k = x.shape
    return _matmul(x.reshape(b * t, k), w, **kw).reshape(b, t, w.shape[1])


def _rmsnorm(x, g=None):
    xf = x.astype(jnp.float32)
    y = xf * lax.rsqrt(jnp.mean(xf * xf, axis=-1, keepdims=True) + EPS)
    if g is not None:
        y = y * g.astype(jnp.float32)
    return y.astype(x.dtype)


def _grid_angles(t_len, rot_dim):
    rows = t_len // GRID_W
    row = jnp.repeat(jnp.arange(rows, dtype=jnp.float32), GRID_W)
    col = jnp.tile(jnp.arange(GRID_W, dtype=jnp.float32), rows)
    n_freq = rot_dim // 4
    inv = ROPE_BASE ** (-jnp.arange(n_freq, dtype=jnp.float32) / n_freq)
    return row[:, None] * inv[None], col[:, None] * inv[None]


def _rotate(x, ang):
    n = x.shape[-1] // 2
    cos = jnp.cos(ang)[None, :, None, :]
    sin = jnp.sin(ang)[None, :, None, :]
    x1, x2 = x[..., :n], x[..., n:]
    return jnp.concatenate([x1 * cos - x2 * sin, x2 * cos + x1 * sin], axis=-1)


def _axial_rope(x):
    t_len, rot = x.shape[1], x.shape[-1]
    ang_r, ang_c = _grid_angles(t_len, rot)
    xf = x.astype(jnp.float32)
    half = rot // 2
    out = jnp.concatenate([_rotate(xf[..., :half], ang_r), _rotate(xf[..., half:], ang_c)], axis=-1)
    return out.astype(x.dtype)


def _blocked_attention(q, k, v, scale):
    bn, tq, nh, dq = q.shape
    nb = tq // Q_BLOCK
    qb = q.reshape(bn, nb, Q_BLOCK, nh, dq).transpose(1, 0, 2, 3, 4)

    def _one(q_blk):
        s = jnp.einsum('bqhd,bkhd->bhqk', q_blk, k, preferred_element_type=jnp.float32) * scale
        p = jax.nn.softmax(s, axis=-1)
        return jnp.einsum('bhqk,bkhv->bqhv', p.astype(v.dtype), v)

    out = lax.map(_one, qb)
    return out.transpose(1, 0, 2, 3, 4).reshape(bn, tq, nh, v.shape[-1])


def _linear_combine(e1, e2):
    a1, b1 = e1
    a2, b2 = e2
    return a2 * a1, a2 * b1 + b2


def _s5_mixer(u, p, h0):
    f32 = jnp.float32
    bn, t_len, _ = u.shape
    ug = u.astype(f32).reshape(bn, t_len, S5_GROUPS, S5_GROUP_CH)
    b_mat = lax.complex(p['s5_b_re'].astype(f32), p['s5_b_im'].astype(f32))
    bu = jnp.einsum('gpc,btgc->btgp', b_mat, ug.astype(jnp.complex64))
    hs, finals = [], []
    for d in range(2):
        lam = lax.complex(p['s5_a_re'][d].astype(f32), p['s5_a_im'][d].astype(f32))
        dt = jnp.exp(p['s5_log_dt'][d].astype(f32))[:, None]
        a_bar = jnp.exp(lam * dt)
        b_in = ((a_bar - 1.0) / lam) * bu
        a_in = jnp.broadcast_to(a_bar, b_in.shape)
        if d == 1:
            a_in, b_in = jnp.flip(a_in, 1), jnp.flip(b_in, 1)
        a_cum, h = lax.associative_scan(_linear_combine, (a_in, b_in), axis=1)
        if h0 is not None:
            h0c = lax.complex(h0[:, d, ..., 0].astype(f32), h0[:, d, ..., 1].astype(f32))
            h = h + a_cum * h0c[:, None]
        else:
            finals.append(h[:, -1])
        if d == 1:
            h = jnp.flip(h, 1)
        hs.append(h)
    c_mat = lax.complex(p['s5_c_re'].astype(f32), p['s5_c_im'].astype(f32))
    y = jnp.real(jnp.einsum('gcp,btgp->btgc', c_mat, hs[0] + hs[1])).reshape(bn, t_len, S5_CH)
    y = y + p['s5_d'].astype(f32) * u.astype(f32)
    y = jax.nn.gelu(y)
    y = y * jax.nn.sigmoid(y @ p['s5_w_glu'].astype(f32))
    fin = None
    if h0 is None:
        fs = jnp.stack(finals, axis=1)
        fin = jnp.stack([jnp.real(fs), jnp.imag(fs)], axis=-1).astype(u.dtype)
    return y.astype(u.dtype), fin


def _rglru_mixer(xb, gb, p, h0):
    f32 = jnp.float32
    bn, t_len, w = xb.shape
    left = LRU_CONV // 2
    conv = lax.conv_general_dilated(xb.astype(f32), p['lru_conv_w'].astype(f32)[:, None, :],
                                    window_strides=(1,), padding=[(left, LRU_CONV - 1 - left)],
                                    dimension_numbers=('NWC', 'WIO', 'NWC'), feature_group_count=w)
    xc = conv + p['lru_conv_b'].astype(f32)
    xblk = xc.reshape(bn, t_len, LRU_BLOCKS, LRU_BLOCK)
    hs, finals = [], []
    for d in range(2):
        r = jax.nn.sigmoid(jnp.einsum('btnk,nkj->btnj', xblk, p['lru_w_a'][d].astype(f32)).reshape(bn, t_len, w)
                           + p['lru_b_a'][d].astype(f32))
        i = jax.nn.sigmoid(jnp.einsum('btnk,nkj->btnj', xblk, p['lru_w_x'][d].astype(f32)).reshape(bn, t_len, w)
                           + p['lru_b_x'][d].astype(f32))
        log_a = -LRU_C * r * jax.nn.softplus(-p['lru_lambda'][d].astype(f32))
        a = jnp.exp(log_a)
        b = jnp.sqrt(-jnp.expm1(2.0 * log_a)) * (i * xc)
        if d == 1:
            a, b = jnp.flip(a, 1), jnp.flip(b, 1)
        a_cum, h = lax.associative_scan(_linear_combine, (a, b), axis=1)
        if h0 is not None:
            h = h + a_cum * h0[:, d].astype(f32)[:, None]
        else:
            finals.append(h[:, -1])
        if d == 1:
            h = jnp.flip(h, 1)
        hs.append(h)
    y = (hs[0] + hs[1]) * jax.nn.gelu(gb.astype(f32))
    fin = jnp.stack(finals, axis=1).astype(xb.dtype) if h0 is None else None
    return y.astype(xb.dtype), fin


def _chunk_retention(q, k, v, s0, inclusive):
    f32 = jnp.float32
    bn, t_len, nh, _ = q.shape
    n_chunks = t_len // RET_CHUNK
    log_g = jnp.log1p(-jnp.exp2(-5.0 - jnp.arange(nh, dtype=f32)))
    idx = jnp.arange(RET_CHUNK, dtype=f32)
    diff = idx[:, None] - idx[None, :]
    mask = (diff >= 0) if inclusive else (diff > 0)
    decay = jnp.where(mask[None], jnp.exp(jnp.maximum(diff, 0.0)[None] * log_g[:, None, None]), 0.0)
    xi = jnp.exp((idx + 1.0)[:, None] * log_g[None])
    zeta = jnp.exp((RET_CHUNK - 1.0 - idx)[:, None] * log_g[None])
    g_chunk = jnp.exp(RET_CHUNK * log_g)

    def to_chunks(a):
        return a.reshape(bn, n_chunks, RET_CHUNK, nh, a.shape[-1]).transpose(1, 0, 2, 3, 4)

    def step(s, qkv):
        qc, kc, vc = qkv
        scores = jnp.einsum('bqhd,bkhd->bhqk', qc, kc) * decay
        inner = jnp.einsum('bhqk,bkhv->bqhv', scores, vc)
        cross = jnp.einsum('bqhd,bhdv->bqhv', qc, s) * xi[None, :, :, None]
        s_new = g_chunk[None, :, None, None] * s + jnp.einsum('bkhd,bkhv->bhdv', kc * zeta[None, :, :, None], vc)
        return s_new, inner + cross

    s_fin, out = lax.scan(step, s0, (to_chunks(q), to_chunks(k), to_chunks(v)))
    return out.transpose(1, 0, 2, 3, 4).reshape(bn, t_len, nh, v.shape[-1]), s_fin


def _retention_mixer(q, k, v, g, s0):
    f32 = jnp.float32
    bn, t_len, _ = q.shape
    q = q.reshape(bn, t_len, RET_HEADS, RET_DK)
    k = k.reshape(bn, t_len, RET_HEADS, RET_DK)
    v = v.reshape(bn, t_len, RET_HEADS, RET_DV)
    if s0 is not None:
        q, k = _axial_rope(q), _axial_rope(k)
        s0 = s0.astype(f32)
    else:
        s0 = jnp.zeros((bn, 2, RET_HEADS, RET_DK, RET_DV), f32)
    qf, kf, vf = q.astype(f32), k.astype(f32) * (RET_DK ** -0.5), v.astype(f32)
    o_f, s_f = _chunk_retention(qf, kf, vf, s0[:, 0], True)
    o_b, s_b = _chunk_retention(jnp.flip(qf, 1), jnp.flip(kf, 1), jnp.flip(vf, 1), s0[:, 1], False)
    o = _rmsnorm(o_f + jnp.flip(o_b, 1)).reshape(bn, t_len, RET_HEADS * RET_DV)
    o = o * jax.nn.silu(g.astype(f32))
    return o.astype(g.dtype), jnp.stack([s_f, s_b], axis=1).astype(g.dtype)


def _mla_expand(c_kv, w_ukv):
    bn, t_len, _ = c_kv.shape
    kv = _mm3(c_kv, w_ukv).reshape(bn, t_len, MLA_HEADS, MLA_NOPE + MLA_V)
    return kv[..., :MLA_NOPE], kv[..., MLA_NOPE:]


def _mla_mixer(q_lat, kv_lat, k_rope, p, ckv_ctx, kpe_ctx):
    latent = ckv_ctx is not None
    bn, t_len, _ = q_lat.shape
    q = _mm3(_rmsnorm(q_lat, p['mla_q_norm_g']), p['mla_w_uq']).reshape(bn, t_len, MLA_HEADS, MLA_NOPE + MLA_ROPE)
    q_nope, q_pe = q[..., :MLA_NOPE], q[..., MLA_NOPE:]
    c_kv = _rmsnorm(kv_lat, p['mla_kv_norm_g'])
    k_pe = k_rope[:, :, None, :]
    if latent:
        q_pe, k_pe = _axial_rope(q_pe), _axial_rope(k_pe)
    k_nope, v = _mla_expand(c_kv, p['mla_w_ukv'])
    k = jnp.concatenate([k_nope, jnp.broadcast_to(k_pe, (bn, t_len, MLA_HEADS, MLA_ROPE))], axis=-1)
    if latent:
        t_ctx = ckv_ctx.shape[1]
        kn_c, v_c = _mla_expand(ckv_ctx, p['mla_w_ukv'])
        k_c = jnp.concatenate([kn_c, jnp.broadcast_to(kpe_ctx[:, :, None, :], (bn, t_ctx, MLA_HEADS, MLA_ROPE))], axis=-1)
        k = jnp.concatenate([k_c, k], axis=1)
        v = jnp.concatenate([v_c, v], axis=1)
    out = _blocked_attention(jnp.concatenate([q_nope, q_pe], axis=-1), k, v, MLA_SCALE)
    return out.reshape(bn, t_len, MLA_HEADS * MLA_V), c_kv


def _moe(h, router_w, router_b, p):
    f32 = jnp.float32
    bn, t_len, d = h.shape
    t = h.reshape(bn * t_len, d)
    scores = jax.nn.softmax((t @ router_w).astype(f32), axis=-1)
    sel = scores + router_b.astype(f32)
    grp = sel.reshape(-1, N_EXPERT_GROUPS, EXPERTS_PER_GROUP)
    grp_score = lax.top_k(grp, TOP_K)[0].sum(-1)
    g_idx = jnp.argmax(grp_score, axis=-1)
    in_grp = jnp.take_along_axis(grp, g_idx[:, None, None], axis=1)[:, 0]
    _, local = lax.top_k(in_grp, TOP_K)
    e_idx = g_idx[:, None] * EXPERTS_PER_GROUP + local
    w = jnp.take_along_axis(scores, e_idx, axis=-1)
    w = w / jnp.sum(w, axis=-1, keepdims=True)
    combine = jnp.sum(jax.nn.one_hot(e_idx, N_EXPERTS, dtype=f32) * w[..., None], axis=1)
    gate = jnp.einsum('nd,edf->nef', t, p['moe_w_gate'])
    up = jnp.einsum('nd,edf->nef', t, p['moe_w_up'])
    act = jax.nn.silu(gate) * up * combine.astype(t.dtype)[..., None]
    y = jnp.einsum('nef,efd->nd', act, p['moe_w_down'])
    return y.reshape(bn, t_len, d)


def _layer(x, mod, lp, router_w, router_b, ctx):
    bn, t_len, d = x.shape
    mod = mod.reshape(mod.shape[0], 1, N_MOD, d)
    sh1, sc1, g1, sh2, sc2, g2 = (mod[:, :, j] for j in range(N_MOD))
    h = _rmsnorm(x, lp['norm1_g']) * (1.0 + sc1) + sh1
    proj = _mm3(h, lp['w_in_pad'], tn=640)[..., :D_IN]
    cuts = [int(cut) for cut in np.cumsum(PART_SIZES)[:-1]]
    s5_u, mla_q, mla_kv, mla_kr, lru_x, lru_g, ret_q, ret_k, ret_v, ret_g = jnp.split(proj, cuts, axis=-1)
    if ctx is None:
        ckv_c = kpe_c = s5_h0 = lru_h0 = ret_s0 = None
    else:
        ckv_c, kpe_c, s5_h0, lru_h0, ret_s0 = ctx
    y_s5, s5_fin = _s5_mixer(s5_u, lp, s5_h0)
    y_mla, c_kv = _mla_mixer(mla_q, mla_kv, mla_kr, lp, ckv_c, kpe_c)
    y_lru, lru_fin = _rglru_mixer(lru_x, lru_g, lp, lru_h0)
    y_ret, ret_fin = _retention_mixer(ret_q, ret_k, ret_v, ret_g, ret_s0)
    mix = _mm3(jnp.concatenate([y_s5, y_mla, y_lru, y_ret], axis=-1), lp['w_out'])
    x = x + g1 * mix
    h2 = _rmsnorm(x, lp['norm2_g']) * (1.0 + sc2) + sh2
    x = x + g2 * _moe(h2, router_w, router_b, lp)
    return x, (c_kv, mla_kr, s5_fin, lru_fin, ret_fin)


def kernel(x_prompt, x_sample, c, cache_mla_ckv, cache_mla_kpe, state_s5, state_lru, state_ret,
           c_ctx, w_ada, b_ada, norm1_g, norm2_g, w_in, w_out,
           s5_a_re, s5_a_im, s5_log_dt, s5_b_re, s5_b_im, s5_c_re, s5_c_im, s5_d, s5_w_glu,
           mla_q_norm_g, mla_w_uq, mla_kv_norm_g, mla_w_ukv,
           lru_conv_w, lru_conv_b, lru_w_a, lru_b_a, lru_w_x, lru_b_x, lru_lambda,
           router_w, router_b, moe_w_gate, moe_w_up, moe_w_down, final_norm_g):
    w_in_pad = jnp.pad(w_in, ((0, 0), (0, 0), (0, 4480 - D_IN)))
    stacked = dict(w_ada=w_ada, b_ada=b_ada, norm1_g=norm1_g, norm2_g=norm2_g, w_in_pad=w_in_pad, w_out=w_out,
                   s5_a_re=s5_a_re, s5_a_im=s5_a_im, s5_log_dt=s5_log_dt, s5_b_re=s5_b_re, s5_b_im=s5_b_im,
                   s5_c_re=s5_c_re, s5_c_im=s5_c_im, s5_d=s5_d, s5_w_glu=s5_w_glu,
                   mla_q_norm_g=mla_q_norm_g, mla_w_uq=mla_w_uq, mla_kv_norm_g=mla_kv_norm_g, mla_w_ukv=mla_w_ukv,
                   lru_conv_w=lru_conv_w, lru_conv_b=lru_conv_b, lru_w_a=lru_w_a, lru_b_a=lru_b_a,
                   lru_w_x=lru_w_x, lru_b_x=lru_b_x, lru_lambda=lru_lambda,
                   moe_w_gate=moe_w_gate, moe_w_up=moe_w_up, moe_w_down=moe_w_down)

    x = x_prompt
    layer_states = []
    for l in range(DEPTH):
        lp = {name: arr[l] for name, arr in stacked.items()}
        mod = jax.nn.silu(c_ctx)[None] @ lp['w_ada'] + lp['b_ada']
        x, st = _layer(x, mod, lp, router_w, router_b, None)
        layer_states.append(st)
    y_prompt = _rmsnorm(x, final_norm_g)
    new_cache_mla_ckv = jnp.stack([st[0] for st in layer_states], axis=1)
    new_cache_mla_kpe = jnp.stack([st[1] for st in layer_states], axis=1)
    new_state_s5 = jnp.stack([st[2] for st in layer_states], axis=1)
    new_state_lru = jnp.stack([st[3] for st in layer_states], axis=1)
    new_state_ret = jnp.stack([st[4] for st in layer_states], axis=1)

    x = x_sample
    for l in range(DEPTH):
        lp = {name: arr[l] for name, arr in stacked.items()}
        mod = jax.nn.silu(c) @ lp['w_ada'] + lp['b_ada']
        ctx = (cache_mla_ckv[:, l], cache_mla_kpe[:, l], state_s5[:, l], state_lru[:, l], state_ret[:, l])
        x, _ = _layer(x, mod, lp, router_w, router_b, ctx)
    y_sample = _rmsnorm(x, final_norm_g)

    return (y_prompt, y_sample, new_cache_mla_ckv, new_cache_mla_kpe, new_state_s5, new_state_lru, new_state_ret)
```

```python
import math
import functools
import numpy as np
import jax
import jax.numpy as jnp
from jax import lax
from jax.experimental import pallas as pl
from jax.experimental.pallas import tpu as pltpu

D_MODEL = 2048
BATCH = 16
SEQ = 256
DEPTH = 2
DEC_BATCH = 2
DEC_SEQ = 4096
PAST_LEN = 512
GRID_W = 64
EPS = 1e-6
ROPE_BASE = 10000.0
Q_BLOCK = 128
N_MIXERS = 4
GROUP_W = D_MODEL // N_MIXERS
D_MIX = N_MIXERS * GROUP_W
N_MOD = 6
S5_CH = GROUP_W
S5_GROUP_CH = 16
S5_GROUPS = S5_CH // S5_GROUP_CH
S5_STATE = 64
MLA_HEADS = 4
MLA_NOPE = 128
MLA_ROPE = 64
MLA_V = GROUP_W // MLA_HEADS
MLA_Q_RANK = GROUP_W
MLA_KV_RANK = GROUP_W // 2
MLA_SCALE = (MLA_NOPE + MLA_ROPE) ** -0.5
LRU_W = GROUP_W
LRU_BLOCKS = 8
LRU_BLOCK = LRU_W // LRU_BLOCKS
LRU_CONV = 4
LRU_C = 8.0
RET_HEADS = 4
RET_DK = GROUP_W // RET_HEADS
RET_DV = GROUP_W // RET_HEADS
RET_CHUNK = 128
N_EXPERTS = 16
N_EXPERT_GROUPS = 4
EXPERTS_PER_GROUP = N_EXPERTS // N_EXPERT_GROUPS
TOP_K = 2
D_EXPERT = D_MODEL // 4
PART_SIZES = (S5_CH, MLA_Q_RANK, MLA_KV_RANK, MLA_ROPE, LRU_W, LRU_W,
              RET_HEADS * RET_DK, RET_HEADS * RET_DK, RET_HEADS * RET_DV, RET_HEADS * RET_DV)
D_IN = sum(PART_SIZES)

V7X_VMEM_LIMIT = 48 * 1024 * 1024


def _mm_kernel(x_ref, w_ref, o_ref):
    o_ref[...] = jnp.dot(x_ref[...].astype(jnp.bfloat16), w_ref[...].astype(jnp.bfloat16),
                         preferred_element_type=jnp.float32).astype(o_ref.dtype)


def _matmul(x, w, tm=512, tn=512, out_dtype=jnp.float32):
    m, k = x.shape
    _, n = w.shape
    tm = min(tm, m)
    tn = min(tn, n)
    if n % tn:
        tn = 256
    assert m % tm == 0 and n % tn == 0
    return pl.pallas_call(
        _mm_kernel,
        grid=(n // tn, m // tm),
        in_specs=[pl.BlockSpec((tm, k), lambda j, i: (i, 0)),
                  pl.BlockSpec((k, tn), lambda j, i: (0, j))],
        out_specs=pl.BlockSpec((tm, tn), lambda j, i: (i, j)),
        out_shape=jax.ShapeDtypeStruct((m, n), out_dtype),
        compiler_params=pltpu.CompilerParams(
            dimension_semantics=("arbitrary", "arbitrary"), vmem_limit_bytes=V7X_VMEM_LIMIT),
        name="matmul",
    )(x, w)


def _mm3(x, w, **kw):
    b, t, k = x.shape
    return _matmul(x.reshape(b * t, k), w, **kw).reshape(b, t, w.shape[1])


S5_L = 32
S5_PAIRS = S5_GROUPS // 2
S5_Q = 4
S5_LANES = S5_GROUPS * S5_STATE


def _s5_tables(a_re, a_im, log_dt, b_re, b_im, c_re, c_im):
    f32 = jnp.float32
    L, G, P, C = S5_L, S5_GROUPS, S5_STATE, S5_GROUP_CH
    dt = jnp.exp(log_dt.astype(f32))[..., None]
    zr, zi = a_re * dt, a_im * dt
    ab_r, ab_i = jnp.exp(zr) * jnp.cos(zi), jnp.exp(zr) * jnp.sin(zi)
    den = a_re * a_re + a_im * a_im
    nr, ni = ab_r - 1.0, ab_i
    be_r = (nr * a_re + ni * a_im) / den
    be_i = (ni * a_re - nr * a_im) / den
    bp_r = be_r[..., None] * b_re[None] - be_i[..., None] * b_im[None]
    bp_i = be_r[..., None] * b_im[None] + be_i[..., None] * b_re[None]
    tau = jnp.arange(L + 1, dtype=f32)[:, None, None, None]
    pw_r = jnp.exp(zr[None] * tau) * jnp.cos(zi[None] * tau)
    pw_i = jnp.exp(zr[None] * tau) * jnp.sin(zi[None] * tau)
    e_r = pw_r[..., None] * bp_r[None] - pw_i[..., None] * bp_i[None]
    e_i = pw_r[..., None] * bp_i[None] + pw_i[..., None] * bp_r[None]
    hp = lax.Precision.HIGHEST
    kk = (jnp.einsum('gcp,tdgpk->tdgkc', c_re, e_r, precision=hp)
          - jnp.einsum('gcp,tdgpk->tdgkc', c_im, e_i, precision=hp))
    idx = jnp.arange(L)
    diff = idx[None, :] - idx[:, None]
    kf = kk[:L, 0][jnp.maximum(diff, 0)]
    kb = kk[:L, 1][jnp.maximum(-diff, 0)]
    toe = (jnp.where((diff >= 0)[:, :, None, None, None], kf, 0.0)
           + jnp.where((diff <= 0)[:, :, None, None, None], kb, 0.0))
    toe = toe.transpose(2, 0, 3, 1, 4).reshape(G, L * C, L * C)

    wsf_r, wsf_i = e_r[:L, 0][::-1], e_i[:L, 0][::-1]
    wsb_r, wsb_i = e_r[:L, 1], e_i[:L, 1]
    ws = jnp.stack([wsf_r, wsf_i, wsb_r, wsb_i], axis=0)
    ws = ws.transpose(2, 1, 4, 0, 3).reshape(S5_PAIRS, 2, L * C, S5_Q, 1, P)
    eye2 = jnp.eye(2, dtype=f32)[None, :, None, None, :, None]
    ws_pair = (ws * eye2).reshape(S5_PAIRS, 2 * L * C, S5_Q * 2 * P)

    pf_r, pf_i = pw_r[1:L + 1, 0], pw_i[1:L + 1, 0]
    pb_r, pb_i = pw_r[1:L + 1, 1][::-1], pw_i[1:L + 1, 1][::-1]

    def _m(p_r, p_i):
        m_r = c_re[None] * p_r[:, :, None, :] - c_im[None] * p_i[:, :, None, :]
        m_i = c_re[None] * p_i[:, :, None, :] + c_im[None] * p_r[:, :, None, :]
        return m_r, -m_i

    wo = jnp.stack(_m(pf_r, pf_i) + _m(pb_r, pb_i), axis=0)
    wo = wo.transpose(2, 0, 4, 1, 3).reshape(S5_PAIRS, 2, S5_Q, P, 1, L * C)
    wo = wo.transpose(0, 2, 1, 3, 4, 5)
    eye2o = jnp.eye(2, dtype=f32)[None, None, :, None, :, None]
    wo_pair = (wo * eye2o).reshape(S5_PAIRS, S5_Q * 2 * P, 2 * L * C)

    a_l = jnp.stack([pw_r[L, 0], pw_i[L, 0], pw_r[L, 1], pw_i[L, 1]], axis=0).reshape(S5_Q, 1, S5_LANES)
    bf = jnp.bfloat16
    return toe.astype(bf), ws_pair.astype(bf), wo_pair.astype(bf), a_l


def _s5a_kernel(x_ref, t_ref, ws_ref, y1_ref, s_ref):
    x0, x1 = x_ref[0], x_ref[1]
    y1_ref[0] = jnp.dot(x0, t_ref[0], preferred_element_type=jnp.float32)
    y1_ref[1] = jnp.dot(x1, t_ref[1], preferred_element_type=jnp.float32)
    s = jnp.dot(jnp.concatenate([x0, x1], axis=1), ws_ref[0], preferred_element_type=jnp.float32)
    for q in range(S5_Q):
        s_ref[q] = s[:, q * 128:(q + 1) * 128]


def _s5a(x, toe, ws_pair):
    g, r, w = x.shape
    return pl.pallas_call(
        _s5a_kernel,
        grid=(S5_PAIRS,),
        in_specs=[pl.BlockSpec((2, r, w), lambda i: (i, 0, 0)),
                  pl.BlockSpec((2, w, w), lambda i: (i, 0, 0)),
                  pl.BlockSpec((1, 2 * w, S5_Q * 128), lambda i: (i, 0, 0))],
        out_specs=[pl.BlockSpec((2, r, w), lambda i: (i, 0, 0)),
                   pl.BlockSpec((S5_Q, r, 128), lambda i: (0, 0, i))],
        out_shape=[jax.ShapeDtypeStruct((g, r, w), jnp.float32),
                   jax.ShapeDtypeStruct((S5_Q, r, S5_LANES), jnp.float32)],
        compiler_params=pltpu.CompilerParams(dimension_semantics=("arbitrary",),
                                             vmem_limit_bytes=V7X_VMEM_LIMIT),
        name="s5_chunk_local",
    )(x, toe, ws_pair)


def _s5b_kernel(s_ref, a_ref, h0_ref, hp_ref, fin_ref, *, nc):
    def run(qr, qi, order_fwd):
        ar, ai = a_ref[qr], a_ref[qi]

        def body(i, carry):
            hr, hi = carry
            k = i if order_fwd else nc - 1 - i
            hp_ref[qr, pl.ds(k, 1), :] = hr
            hp_ref[qi, pl.ds(k, 1), :] = hi
            nr = ar * hr - ai * hi + s_ref[qr, pl.ds(k, 1), :]
            ni = ar * hi + ai * hr + s_ref[qi, pl.ds(k, 1), :]
            return nr, ni

        hr, hi = lax.fori_loop(0, nc, body, (h0_ref[qr], h0_ref[qi]))
        fin_ref[qr] = hr
        fin_ref[qi] = hi

    run(0, 1, True)
    run(2, 3, False)


S5_SCAN_LANES = 1024


def _s5b(s, a_l, h0):
    _, nc, w = s.shape
    tl = S5_SCAN_LANES
    vec = pl.BlockSpec((S5_Q, 1, tl), lambda i: (0, 0, i))
    seq = pl.BlockSpec((S5_Q, nc, tl), lambda i: (0, 0, i))
    return pl.pallas_call(
        functools.partial(_s5b_kernel, nc=nc),
        grid=(w // tl,),
        in_specs=[seq, vec, vec],
        out_specs=[seq, vec],
        out_shape=[jax.ShapeDtypeStruct((S5_Q, nc, w), jnp.float32),
                   jax.ShapeDtypeStruct((S5_Q, 1, w), jnp.float32)],
        compiler_params=pltpu.CompilerParams(dimension_semantics=("arbitrary",),
                                             vmem_limit_bytes=V7X_VMEM_LIMIT),
        name="s5_chunk_scan",
    )(s, a_l, h0)


def _s5c_kernel(h_ref, wo_ref, y1_ref, y_ref):
    hcat = jnp.concatenate([h_ref[q] for q in range(S5_Q)], axis=1).astype(jnp.bfloat16)
    y2 = jnp.dot(hcat, wo_ref[0], preferred_element_type=jnp.float32)
    w = y1_ref.shape[-1]
    y_ref[0] = (y1_ref[0] + y2[:, :w]).astype(y_ref.dtype)
    y_ref[1] = (y1_ref[1] + y2[:, w:]).astype(y_ref.dtype)


def _s5c(hprev, wo_pair, y1):
    g, r, w = y1.shape
    return pl.pallas_call(
        _s5c_kernel,
        grid=(S5_PAIRS,),
        in_specs=[pl.BlockSpec((S5_Q, r, 128), lambda i: (0, 0, i)),
                  pl.BlockSpec((1, S5_Q * 128, 2 * w), lambda i: (i, 0, 0)),
                  pl.BlockSpec((2, r, w), lambda i: (i, 0, 0))],
        out_specs=pl.BlockSpec((2, r, w), lambda i: (i, 0, 0)),
        out_shape=jax.ShapeDtypeStruct((g, r, w), jnp.float32),
        compiler_params=pltpu.CompilerParams(dimension_semantics=("arbitrary",),
                                             vmem_limit_bytes=V7X_VMEM_LIMIT),
        name="s5_state_to_out",
    )(hprev, wo_pair, y1)


def _s5_to_chunks(u):
    b, t, _ = u.shape
    nc = t // S5_L
    x = u.reshape(b, nc, S5_L, S5_GROUPS, S5_GROUP_CH).transpose(3, 1, 0, 2, 4)
    return x.reshape(S5_GROUPS, nc * b, S5_L * S5_GROUP_CH)


def _s5_from_chunks(y, b, t):
    nc = t // S5_L
    y = y.reshape(S5_GROUPS, nc, b, S5_L, S5_GROUP_CH).transpose(2, 1, 3, 0, 4)
    return y.reshape(b, t, S5_CH)


def _s5_state_planes(h0):
    b = h0.shape[0]
    return h0.transpose(1, 4, 0, 2, 3).reshape(S5_Q, b, S5_LANES)


def _s5_core(us, h0s, tables):
    toe, ws_pair, wo_pair, a_l = tables
    xs = [_s5_to_chunks(u.astype(jnp.bfloat16)) for u in us]
    rows = [x.shape[1] for x in xs]
    y1, s = _s5a(jnp.concatenate(xs, axis=1), toe, ws_pair)
    hps, fins = [], []
    off = 0
    for u, h0, r in zip(us, h0s, rows):
        b, t, _ = u.shape
        nc = t // S5_L
        if h0 is None:
            h0p = jnp.zeros((S5_Q, 1, b * S5_LANES), jnp.float32)
        else:
            h0p = _s5_state_planes(h0.astype(jnp.float32)).reshape(S5_Q, 1, b * S5_LANES)
        hp, fin = _s5b(s[:, off:off + r].reshape(S5_Q, nc, b * S5_LANES), jnp.tile(a_l, (1, 1, b)), h0p)
        hps.append(hp.reshape(S5_Q, r, S5_LANES))
        fins.append(fin.reshape(2, 2, b, S5_GROUPS, S5_STATE).transpose(2, 0, 3, 4, 1))
        off += r
    y = _s5c(jnp.concatenate(hps, axis=1), wo_pair, y1)
    outs, off = [], 0
    for u, r in zip(us, rows):
        b, t, _ = u.shape
        outs.append(_s5_from_chunks(y[:, off:off + r], b, t))
        off += r
    return outs, fins


LRU_TC = 256
LRU_HALO = 8


def _sigmoid(x):
    return 1.0 / (1.0 + jnp.exp(-x))


def _lru_kernel(xp_ref, cw_ref, cb_ref, wg_ref, bg_ref, sp_ref, h0_ref, out_ref, fin_ref,
                a_s, b_s, hf_s, *, t_len, tc):
    f32 = jnp.float32
    nt = t_len // tc
    w = LRU_W

    def gates(r0, d):
        slab = xp_ref[0, pl.ds(r0, tc + 2 * LRU_HALO), :].astype(f32)
        o = LRU_HALO - LRU_CONV // 2
        xc = cb_ref[...] + sum(cw_ref[k:k + 1, :] * slab[o + k:o + k + tc] for k in range(LRU_CONV))
        g = jnp.dot(xc.astype(jnp.bfloat16), wg_ref[d], preferred_element_type=f32) + bg_ref[d]
        r = _sigmoid(g[:, :w])
        i = _sigmoid(g[:, w:])
        log_a = -sp_ref[d] * r
        a_s[...] = jnp.exp(log_a)
        b_s[...] = jnp.sqrt(1.0 - jnp.exp(2.0 * log_a)) * (i * xc)

    def fwd_chunk(c, h):
        r0 = pl.multiple_of(c * tc, tc)
        gates(r0, 0)

        def rows(i, h):
            for j in range(8):
                t = i * 8 + j
                h = a_s[pl.ds(t, 1), :] * h + b_s[pl.ds(t, 1), :]
                hf_s[pl.ds(r0 + t, 1), :] = h
            return h

        return lax.fori_loop(0, tc // 8, rows, h)

    h = lax.fori_loop(0, nt, fwd_chunk, h0_ref[0, 0:1, :])
    fin_ref[0, 0:1, :] = h

    def bwd_chunk(ci, h):
        r0 = pl.multiple_of((nt - 1 - ci) * tc, tc)
        gates(r0, 1)

        def rows(i, h):
            for j in range(8):
                t = tc - 1 - (i * 8 + j)
                h = a_s[pl.ds(t, 1), :] * h + b_s[pl.ds(t, 1), :]
                out_ref[0, pl.ds(r0 + t, 1), :] = hf_s[pl.ds(r0 + t, 1), :] + h
            return h

        return lax.fori_loop(0, tc // 8, rows, h)

    h = lax.fori_loop(0, nt, bwd_chunk, h0_ref[0, 1:2, :])
    fin_ref[0, 1:2, :] = h


def _block_diag(wb):
    n, k, j = wb.shape
    return (wb[:, :, None, :] * jnp.eye(n, dtype=wb.dtype)[:, None, :, None]).reshape(n * k, n * j)


def _lru_tables(conv_w, conv_b, w_a, b_a, w_x, b_x, lam):
    wg = jnp.stack([jnp.concatenate([_block_diag(w_a[d]), _block_diag(w_x[d])], axis=1) for d in range(2)])
    bg = jnp.concatenate([b_a, b_x], axis=-1)[:, None, :]
    sp = (LRU_C * jax.nn.softplus(-lam.astype(jnp.float32)))[:, None, :]
    return conv_w, conv_b[None, :], wg.astype(jnp.bfloat16), bg, sp


def _lru_core(x, h0, tables):
    b, t, w = x.shape
    tc = min(LRU_TC, t)
    cw, cb, wg, bg, sp = tables
    xp = jnp.pad(x.astype(jnp.bfloat16), ((0, 0), (LRU_HALO, LRU_HALO), (0, 0)))
    full = lambda shape: pl.BlockSpec(shape, lambda i: (0,) * len(shape))
    return pl.pallas_call(
        functools.partial(_lru_kernel, t_len=t, tc=tc),
        grid=(b,),
        in_specs=[pl.BlockSpec((1, t + 2 * LRU_HALO, w), lambda i: (i, 0, 0)),
                  full(cw.shape), full(cb.shape), full(wg.shape), full(bg.shape), full(sp.shape),
                  pl.BlockSpec((1, 2, w), lambda i: (i, 0, 0))],
        out_specs=[pl.BlockSpec((1, t, w), lambda i: (i, 0, 0)),
                   pl.BlockSpec((1, 2, w), lambda i: (i, 0, 0))],
        out_shape=[jax.ShapeDtypeStruct((b, t, w), jnp.float32),
                   jax.ShapeDtypeStruct((b, 2, w), jnp.float32)],
        scratch_shapes=[pltpu.VMEM((tc, w), jnp.float32), pltpu.VMEM((tc, w), jnp.float32),
                        pltpu.VMEM((t, w), jnp.float32)],
        compiler_params=pltpu.CompilerParams(dimension_semantics=("arbitrary",),
                                             vmem_limit_bytes=V7X_VMEM_LIMIT),
        name="rglru",
    )(xp, cw, cb, wg, bg, sp, h0)


def _rmsnorm(x, g=None):
    xf = x.astype(jnp.float32)
    y = xf * lax.rsqrt(jnp.mean(xf * xf, axis=-1, keepdims=True) + EPS)
    if g is not None:
        y = y * g.astype(jnp.float32)
    return y.astype(x.dtype)


def _grid_angles(t_len, rot_dim):
    rows = t_len // GRID_W
    row = jnp.repeat(jnp.arange(rows, dtype=jnp.float32), GRID_W)
    col = jnp.tile(jnp.arange(GRID_W, dtype=jnp.float32), rows)
    n_freq = rot_dim // 4
    inv = ROPE_BASE ** (-jnp.arange(n_freq, dtype=jnp.float32) / n_freq)
    return row[:, None] * inv[None], col[:, None] * inv[None]


def _rotate(x, ang):
    n = x.shape[-1] // 2
    cos = jnp.cos(ang)[None, :, None, :]
    sin = jnp.sin(ang)[None, :, None, :]
    x1, x2 = x[..., :n], x[..., n:]
    return jnp.concatenate([x1 * cos - x2 * sin, x2 * cos + x1 * sin], axis=-1)


def _axial_rope(x):
    t_len, rot = x.shape[1], x.shape[-1]
    ang_r, ang_c = _grid_angles(t_len, rot)
    xf = x.astype(jnp.float32)
    half = rot // 2
    out = jnp.concatenate([_rotate(xf[..., :half], ang_r), _rotate(xf[..., half:], ang_c)], axis=-1)
    return out.astype(x.dtype)


def _blocked_attention(q, k, v, scale):
    bn, tq, nh, dq = q.shape
    nb = tq // Q_BLOCK
    qb = q.reshape(bn, nb, Q_BLOCK, nh, dq).transpose(1, 0, 2, 3, 4)

    def _one(q_blk):
        s = jnp.einsum('bqhd,bkhd->bhqk', q_blk, k, preferred_element_type=jnp.float32) * scale
        p = jax.nn.softmax(s, axis=-1)
        return jnp.einsum('bhqk,bkhv->bqhv', p.astype(v.dtype), v)

    out = lax.map(_one, qb)
    return out.transpose(1, 0, 2, 3, 4).reshape(bn, tq, nh, v.shape[-1])


def _linear_combine(e1, e2):
    a1, b1 = e1
    a2, b2 = e2
    return a2 * a1, a2 * b1 + b2


def _s5_mixer(u, p, h0):
    f32 = jnp.float32
    bn, t_len, _ = u.shape
    ug = u.astype(f32).reshape(bn, t_len, S5_GROUPS, S5_GROUP_CH)
    b_mat = lax.complex(p['s5_b_re'].astype(f32), p['s5_b_im'].astype(f32))
    bu = jnp.einsum('gpc,btgc->btgp', b_mat, ug.astype(jnp.complex64))
    hs, finals = [], []
    for d in range(2):
        lam = lax.complex(p['s5_a_re'][d].astype(f32), p['s5_a_im'][d].astype(f32))
        dt = jnp.exp(p['s5_log_dt'][d].astype(f32))[:, None]
        a_bar = jnp.exp(lam * dt)
        b_in = ((a_bar - 1.0) / lam) * bu
        a_in = jnp.broadcast_to(a_bar, b_in.shape)
        if d == 1:
            a_in, b_in = jnp.flip(a_in, 1), jnp.flip(b_in, 1)
        a_cum, h = lax.associative_scan(_linear_combine, (a_in, b_in), axis=1)
        if h0 is not None:
            h0c = lax.complex(h0[:, d, ..., 0].astype(f32), h0[:, d, ..., 1].astype(f32))
            h = h + a_cum * h0c[:, None]
        else:
            finals.append(h[:, -1])
        if d == 1:
            h = jnp.flip(h, 1)
        hs.append(h)
    c_mat = lax.complex(p['s5_c_re'].astype(f32), p['s5_c_im'].astype(f32))
    y = jnp.real(jnp.einsum('gcp,btgp->btgc', c_mat, hs[0] + hs[1])).reshape(bn, t_len, S5_CH)
    y = y + p['s5_d'].astype(f32) * u.astype(f32)
    y = jax.nn.gelu(y)
    y = y * jax.nn.sigmoid(y @ p['s5_w_glu'].astype(f32))
    fin = None
    if h0 is None:
        fs = jnp.stack(finals, axis=1)
        fin = jnp.stack([jnp.real(fs), jnp.imag(fs)], axis=-1).astype(u.dtype)
    return y.astype(u.dtype), fin


def _rglru_mixer(xb, gb, p, h0):
    f32 = jnp.float32
    bn, t_len, w = xb.shape
    left = LRU_CONV // 2
    conv = lax.conv_general_dilated(xb.astype(f32), p['lru_conv_w'].astype(f32)[:, None, :],
                                    window_strides=(1,), padding=[(left, LRU_CONV - 1 - left)],
                                    dimension_numbers=('NWC', 'WIO', 'NWC'), feature_group_count=w)
    xc = conv + p['lru_conv_b'].astype(f32)
    xblk = xc.reshape(bn, t_len, LRU_BLOCKS, LRU_BLOCK)
    hs, finals = [], []
    for d in range(2):
        r = jax.nn.sigmoid(jnp.einsum('btnk,nkj->btnj', xblk, p['lru_w_a'][d].astype(f32)).reshape(bn, t_len, w)
                           + p['lru_b_a'][d].astype(f32))
        i = jax.nn.sigmoid(jnp.einsum('btnk,nkj->btnj', xblk, p['lru_w_x'][d].astype(f32)).reshape(bn, t_len, w)
                           + p['lru_b_x'][d].astype(f32))
        log_a = -LRU_C * r * jax.nn.softplus(-p['lru_lambda'][d].astype(f32))
        a = jnp.exp(log_a)
        b = jnp.sqrt(-jnp.expm1(2.0 * log_a)) * (i * xc)
        if d == 1:
            a, b = jnp.flip(a, 1), jnp.flip(b, 1)
        a_cum, h = lax.associative_scan(_linear_combine, (a, b), axis=1)
        if h0 is not None:
            h = h + a_cum * h0[:, d].astype(f32)[:, None]
        else:
            finals.append(h[:, -1])
        if d == 1:
            h = jnp.flip(h, 1)
        hs.append(h)
    y = (hs[0] + hs[1]) * jax.nn.gelu(gb.astype(f32))
    fin = jnp.stack(finals, axis=1).astype(xb.dtype) if h0 is None else None
    return y.astype(xb.dtype), fin


def _chunk_retention(q, k, v, s0, inclusive):
    f32 = jnp.float32
    bn, t_len, nh, _ = q.shape
    n_chunks = t_len // RET_CHUNK
    log_g = jnp.log1p(-jnp.exp2(-5.0 - jnp.arange(nh, dtype=f32)))
    idx = jnp.arange(RET_CHUNK, dtype=f32)
    diff = idx[:, None] - idx[None, :]
    mask = (diff >= 0) if inclusive else (diff > 0)
    decay = jnp.where(mask[None], jnp.exp(jnp.maximum(diff, 0.0)[None] * log_g[:, None, None]), 0.0)
    xi = jnp.exp((idx + 1.0)[:, None] * log_g[None])
    zeta = jnp.exp((RET_CHUNK - 1.0 - idx)[:, None] * log_g[None])
    g_chunk = jnp.exp(RET_CHUNK * log_g)

    def to_chunks(a):
        return a.reshape(bn, n_chunks, RET_CHUNK, nh, a.shape[-1]).transpose(1, 0, 2, 3, 4)

    def step(s, qkv):
        qc, kc, vc = qkv
        scores = jnp.einsum('bqhd,bkhd->bhqk', qc, kc) * decay
        inner = jnp.einsum('bhqk,bkhv->bqhv', scores, vc)
        cross = jnp.einsum('bqhd,bhdv->bqhv', qc, s) * xi[None, :, :, None]
        s_new = g_chunk[None, :, None, None] * s + jnp.einsum('bkhd,bkhv->bhdv', kc * zeta[None, :, :, None], vc)
        return s_new, inner + cross

    s_fin, out = lax.scan(step, s0, (to_chunks(q), to_chunks(k), to_chunks(v)))
    return out.transpose(1, 0, 2, 3, 4).reshape(bn, t_len, nh, v.shape[-1]), s_fin


def _retention_mixer(q, k, v, g, s0):
    f32 = jnp.float32
    bn, t_len, _ = q.shape
    q = q.reshape(bn, t_len, RET_HEADS, RET_DK)
    k = k.reshape(bn, t_len, RET_HEADS, RET_DK)
    v = v.reshape(bn, t_len, RET_HEADS, RET_DV)
    if s0 is not None:
        q, k = _axial_rope(q), _axial_rope(k)
        s0 = s0.astype(f32)
    else:
        s0 = jnp.zeros((bn, 2, RET_HEADS, RET_DK, RET_DV), f32)
    qf, kf, vf = q.astype(f32), k.astype(f32) * (RET_DK ** -0.5), v.astype(f32)
    o_f, s_f = _chunk_retention(qf, kf, vf, s0[:, 0], True)
    o_b, s_b = _chunk_retention(jnp.flip(qf, 1), jnp.flip(kf, 1), jnp.flip(vf, 1), s0[:, 1], False)
    o = _rmsnorm(o_f + jnp.flip(o_b, 1)).reshape(bn, t_len, RET_HEADS * RET_DV)
    o = o * jax.nn.silu(g.astype(f32))
    return o.astype(g.dtype), jnp.stack([s_f, s_b], axis=1).astype(g.dtype)


def _mla_expand(c_kv, w_ukv):
    bn, t_len, _ = c_kv.shape
    kv = _mm3(c_kv, w_ukv).reshape(bn, t_len, MLA_HEADS, MLA_NOPE + MLA_V)
    return kv[..., :MLA_NOPE], kv[..., MLA_NOPE:]


def _mla_mixer(q_lat, kv_lat, k_rope, p, ckv_ctx, kpe_ctx):
    latent = ckv_ctx is not None
    bn, t_len, _ = q_lat.shape
    q = _mm3(_rmsnorm(q_lat, p['mla_q_norm_g']), p['mla_w_uq']).reshape(bn, t_len, MLA_HEADS, MLA_NOPE + MLA_ROPE)
    q_nope, q_pe = q[..., :MLA_NOPE], q[..., MLA_NOPE:]
    c_kv = _rmsnorm(kv_lat, p['mla_kv_norm_g'])
    k_pe = k_rope[:, :, None, :]
    if latent:
        q_pe, k_pe = _axial_rope(q_pe), _axial_rope(k_pe)
    k_nope, v = _mla_expand(c_kv, p['mla_w_ukv'])
    k = jnp.concatenate([k_nope, jnp.broadcast_to(k_pe, (bn, t_len, MLA_HEADS, MLA_ROPE))], axis=-1)
    if latent:
        t_ctx = ckv_ctx.shape[1]
        kn_c, v_c = _mla_expand(ckv_ctx, p['mla_w_ukv'])
        k_c = jnp.concatenate([kn_c, jnp.broadcast_to(kpe_ctx[:, :, None, :], (bn, t_ctx, MLA_HEADS, MLA_ROPE))], axis=-1)
        k = jnp.concatenate([k_c, k], axis=1)
        v = jnp.concatenate([v_c, v], axis=1)
    out = _blocked_attention(jnp.concatenate([q_nope, q_pe], axis=-1), k, v, MLA_SCALE)
    return out.reshape(bn, t_len, MLA_HEADS * MLA_V), c_kv


def _moe(h, router_w, router_b, p):
    f32 = jnp.float32
    bn, t_len, d = h.shape
    t = h.reshape(bn * t_len, d)
    scores = jax.nn.softmax((t @ router_w).astype(f32), axis=-1)
    sel = scores + router_b.astype(f32)
    grp = sel.reshape(-1, N_EXPERT_GROUPS, EXPERTS_PER_GROUP)
    grp_score = lax.top_k(grp, TOP_K)[0].sum(-1)
    g_idx = jnp.argmax(grp_score, axis=-1)
    in_grp = jnp.take_along_axis(grp, g_idx[:, None, None], axis=1)[:, 0]
    _, local = lax.top_k(in_grp, TOP_K)
    e_idx = g_idx[:, None] * EXPERTS_PER_GROUP + local
    w = jnp.take_along_axis(scores, e_idx, axis=-1)
    w = w / jnp.sum(w, axis=-1, keepdims=True)
    combine = jnp.sum(jax.nn.one_hot(e_idx, N_EXPERTS, dtype=f32) * w[..., None], axis=1)
    gate = jnp.einsum('nd,edf->nef', t, p['moe_w_gate'])
    up = jnp.einsum('nd,edf->nef', t, p['moe_w_up'])
    act = jax.nn.silu(gate) * up * combine.astype(t.dtype)[..., None]
    y = jnp.einsum('nef,efd->nd', act, p['moe_w_down'])
    return y.reshape(bn, t_len, d)


def _s5_post(y_lin, u, lp):
    f32 = jnp.float32
    y = y_lin + lp['s5_d'].astype(f32) * u.astype(f32)
    y = jax.nn.gelu(y)
    return y * jax.nn.sigmoid(_mm3(y, lp['s5_w_glu']))


def _layer(xs, mods, lp, router_w, router_b, ctx):
    ckv_c, kpe_c, s5_h0, lru_h0, ret_s0 = ctx
    cuts = [int(cut) for cut in np.cumsum(PART_SIZES)[:-1]]
    parts, gates = [], []
    for x, mod in zip(xs, mods):
        mod = mod.reshape(mod.shape[0], 1, N_MOD, D_MODEL)
        sh1, sc1, g1, sh2, sc2, g2 = (mod[:, :, j] for j in range(N_MOD))
        h = _rmsnorm(x, lp['norm1_g']) * (1.0 + sc1) + sh1
        proj = _mm3(h, lp['w_in_pad'], tn=640)[..., :D_IN]
        parts.append(jnp.split(proj, cuts, axis=-1))
        gates.append((g1, sh2, sc2, g2))

    s5_lin, s5_fins = _s5_core([parts[0][0], parts[1][0]], [None, s5_h0], lp['s5_tables'])
    out_xs, states = [], None
    for si, (x, part) in enumerate(zip(xs, parts)):
        s5_u, mla_q, mla_kv, mla_kr, lru_x, lru_g, ret_q, ret_k, ret_v, ret_g = part
        latent = si == 1
        y_s5 = _s5_post(s5_lin[si], s5_u, lp)
        y_mla, c_kv = _mla_mixer(mla_q, mla_kv, mla_kr, lp, ckv_c if latent else None, kpe_c if latent else None)
        h0 = lru_h0 if latent else jnp.zeros((x.shape[0], 2, LRU_W), jnp.float32)
        hsum, lru_fin = _lru_core(lru_x, h0, lp['lru_tables'])
        y_lru = hsum * jax.nn.gelu(lru_g)
        y_ret, ret_fin = _retention_mixer(ret_q, ret_k, ret_v, ret_g, ret_s0 if latent else None)
        mix = _mm3(jnp.concatenate([y_s5, y_mla, y_lru, y_ret], axis=-1), lp['w_out'])
        g1, sh2, sc2, g2 = gates[si]
        x = x + g1 * mix
        h2 = _rmsnorm(x, lp['norm2_g']) * (1.0 + sc2) + sh2
        x = x + g2 * _moe(h2, router_w, router_b, lp)
        out_xs.append(x)
        if not latent:
            states = (c_kv, mla_kr, s5_fins[0], lru_fin, ret_fin)
    return out_xs, states


def kernel(x_prompt, x_sample, c, cache_mla_ckv, cache_mla_kpe, state_s5, state_lru, state_ret,
           c_ctx, w_ada, b_ada, norm1_g, norm2_g, w_in, w_out,
           s5_a_re, s5_a_im, s5_log_dt, s5_b_re, s5_b_im, s5_c_re, s5_c_im, s5_d, s5_w_glu,
           mla_q_norm_g, mla_w_uq, mla_kv_norm_g, mla_w_ukv,
           lru_conv_w, lru_conv_b, lru_w_a, lru_b_a, lru_w_x, lru_b_x, lru_lambda,
           router_w, router_b, moe_w_gate, moe_w_up, moe_w_down, final_norm_g):
    w_in_pad = jnp.pad(w_in, ((0, 0), (0, 0), (0, 4480 - D_IN)))
    stacked = dict(w_ada=w_ada, b_ada=b_ada, norm1_g=norm1_g, norm2_g=norm2_g, w_in_pad=w_in_pad, w_out=w_out,
                   s5_d=s5_d, s5_w_glu=s5_w_glu,
                   mla_q_norm_g=mla_q_norm_g, mla_w_uq=mla_w_uq, mla_kv_norm_g=mla_kv_norm_g, mla_w_ukv=mla_w_ukv,
                   moe_w_gate=moe_w_gate, moe_w_up=moe_w_up, moe_w_down=moe_w_down)

    xs = [x_prompt, x_sample]
    layer_states = []
    for l in range(DEPTH):
        lp = {name: arr[l] for name, arr in stacked.items()}
        lp['s5_tables'] = _s5_tables(s5_a_re[l], s5_a_im[l], s5_log_dt[l], s5_b_re[l], s5_b_im[l],
                                     s5_c_re[l], s5_c_im[l])
        lp['lru_tables'] = _lru_tables(lru_conv_w[l], lru_conv_b[l], lru_w_a[l], lru_b_a[l],
                                       lru_w_x[l], lru_b_x[l], lru_lambda[l])
        mods = [jax.nn.silu(c_ctx)[None] @ lp['w_ada'] + lp['b_ada'],
                jax.nn.silu(c) @ lp['w_ada'] + lp['b_ada']]
        ctx = (cache_mla_ckv[:, l], cache_mla_kpe[:, l], state_s5[:, l], state_lru[:, l], state_ret[:, l])
        xs, st = _layer(xs, mods, lp, router_w, router_b, ctx)
        layer_states.append(st)
    y_prompt = _rmsnorm(xs[0], final_norm_g)
    y_sample = _rmsnorm(xs[1], final_norm_g)
    new_cache_mla_ckv = jnp.stack([st[0] for st in layer_states], axis=1)
    new_cache_mla_kpe = jnp.stack([st[1] for st in layer_states], axis=1)
    new_state_s5 = jnp.stack([st[2] for st in layer_states], axis=1)
    new_state_lru = jnp.stack([st[3] for st in layer_states], axis=1)
    new_state_ret = jnp.stack([st[4] for st in layer_states], axis=1)

    return (y_prompt, y_sample, new_cache_mla_ckv, new_cache_mla_kpe, new_state_s5, new_state_lru, new_state_ret)
```

```python
import math
import functools
import numpy as np
import jax
import jax.numpy as jnp
from jax import lax
from jax.experimental import pallas as pl
from jax.experimental.pallas import tpu as pltpu

D_MODEL = 2048
BATCH = 16
SEQ = 256
DEPTH = 2
DEC_BATCH = 2
DEC_SEQ = 4096
PAST_LEN = 512
GRID_W = 64
EPS = 1e-6
ROPE_BASE = 10000.0
Q_BLOCK = 128
N_MIXERS = 4
GROUP_W = D_MODEL // N_MIXERS
D_MIX = N_MIXERS * GROUP_W
N_MOD = 6
S5_CH = GROUP_W
S5_GROUP_CH = 16
S5_GROUPS = S5_CH // S5_GROUP_CH
S5_STATE = 64
MLA_HEADS = 4
MLA_NOPE = 128
MLA_ROPE = 64
MLA_V = GROUP_W // MLA_HEADS
MLA_Q_RANK = GROUP_W
MLA_KV_RANK = GROUP_W // 2
MLA_SCALE = (MLA_NOPE + MLA_ROPE) ** -0.5
LRU_W = GROUP_W
LRU_BLOCKS = 8
LRU_BLOCK = LRU_W // LRU_BLOCKS
LRU_CONV = 4
LRU_C = 8.0
RET_HEADS = 4
RET_DK = GROUP_W // RET_HEADS
RET_DV = GROUP_W // RET_HEADS
RET_CHUNK = 128
N_EXPERTS = 16
N_EXPERT_GROUPS = 4
EXPERTS_PER_GROUP = N_EXPERTS // N_EXPERT_GROUPS
TOP_K = 2
D_EXPERT = D_MODEL // 4
PART_SIZES = (S5_CH, MLA_Q_RANK, MLA_KV_RANK, MLA_ROPE, LRU_W, LRU_W,
              RET_HEADS * RET_DK, RET_HEADS * RET_DK, RET_HEADS * RET_DV, RET_HEADS * RET_DV)
D_IN = sum(PART_SIZES)

V7X_VMEM_LIMIT = 48 * 1024 * 1024


def _mm_kernel(x_ref, w_ref, o_ref):
    o_ref[...] = jnp.dot(x_ref[...].astype(jnp.bfloat16), w_ref[...].astype(jnp.bfloat16),
                         preferred_element_type=jnp.float32).astype(o_ref.dtype)


def _matmul(x, w, tm=512, tn=512, out_dtype=jnp.float32):
    m, k = x.shape
    _, n = w.shape
    tm = min(tm, m)
    tn = min(tn, n)
    if n % tn:
        tn = 256
    assert m % tm == 0 and n % tn == 0
    return pl.pallas_call(
        _mm_kernel,
        grid=(n // tn, m // tm),
        in_specs=[pl.BlockSpec((tm, k), lambda j, i: (i, 0)),
                  pl.BlockSpec((k, tn), lambda j, i: (0, j))],
        out_specs=pl.BlockSpec((tm, tn), lambda j, i: (i, j)),
        out_shape=jax.ShapeDtypeStruct((m, n), out_dtype),
        compiler_params=pltpu.CompilerParams(
            dimension_semantics=("arbitrary", "arbitrary"), vmem_limit_bytes=V7X_VMEM_LIMIT),
        name="matmul",
    )(x, w)


def _mm3(x, w, **kw):
    b, t, k = x.shape
    return _matmul(x.reshape(b * t, k), w, **kw).reshape(b, t, w.shape[1])


S5_L = 32
S5_PAIRS = S5_GROUPS // 2
S5_Q = 4
S5_LANES = S5_GROUPS * S5_STATE


def _s5_tables(a_re, a_im, log_dt, b_re, b_im, c_re, c_im):
    f32 = jnp.float32
    L, G, P, C = S5_L, S5_GROUPS, S5_STATE, S5_GROUP_CH
    dt = jnp.exp(log_dt.astype(f32))[..., None]
    zr, zi = a_re * dt, a_im * dt
    ab_r, ab_i = jnp.exp(zr) * jnp.cos(zi), jnp.exp(zr) * jnp.sin(zi)
    den = a_re * a_re + a_im * a_im
    nr, ni = ab_r - 1.0, ab_i
    be_r = (nr * a_re + ni * a_im) / den
    be_i = (ni * a_re - nr * a_im) / den
    bp_r = be_r[..., None] * b_re[None] - be_i[..., None] * b_im[None]
    bp_i = be_r[..., None] * b_im[None] + be_i[..., None] * b_re[None]
    tau = jnp.arange(L + 1, dtype=f32)[:, None, None, None]
    pw_r = jnp.exp(zr[None] * tau) * jnp.cos(zi[None] * tau)
    pw_i = jnp.exp(zr[None] * tau) * jnp.sin(zi[None] * tau)
    e_r = pw_r[..., None] * bp_r[None] - pw_i[..., None] * bp_i[None]
    e_i = pw_r[..., None] * bp_i[None] + pw_i[..., None] * bp_r[None]
    hp = lax.Precision.HIGHEST
    kk = (jnp.einsum('gcp,tdgpk->tdgkc', c_re, e_r, precision=hp)
          - jnp.einsum('gcp,tdgpk->tdgkc', c_im, e_i, precision=hp))
    idx = jnp.arange(L)
    diff = idx[None, :] - idx[:, None]
    kf = kk[:L, 0][jnp.maximum(diff, 0)]
    kb = kk[:L, 1][jnp.maximum(-diff, 0)]
    toe = (jnp.where((diff >= 0)[:, :, None, None, None], kf, 0.0)
           + jnp.where((diff <= 0)[:, :, None, None, None], kb, 0.0))
    toe = toe.transpose(2, 0, 3, 1, 4).reshape(G, L * C, L * C)

    wsf_r, wsf_i = e_r[:L, 0][::-1], e_i[:L, 0][::-1]
    wsb_r, wsb_i = e_r[:L, 1], e_i[:L, 1]
    ws = jnp.stack([wsf_r, wsf_i, wsb_r, wsb_i], axis=0)
    ws = ws.transpose(2, 1, 4, 0, 3).reshape(S5_PAIRS, 2, L * C, S5_Q, 1, P)
    eye2 = jnp.eye(2, dtype=f32)[None, :, None, None, :, None]
    ws_pair = (ws * eye2).reshape(S5_PAIRS, 2 * L * C, S5_Q * 2 * P)

    pf_r, pf_i = pw_r[1:L + 1, 0], pw_i[1:L + 1, 0]
    pb_r, pb_i = pw_r[1:L + 1, 1][::-1], pw_i[1:L + 1, 1][::-1]

    def _m(p_r, p_i):
        m_r = c_re[None] * p_r[:, :, None, :] - c_im[None] * p_i[:, :, None, :]
        m_i = c_re[None] * p_i[:, :, None, :] + c_im[None] * p_r[:, :, None, :]
        return m_r, -m_i

    wo = jnp.stack(_m(pf_r, pf_i) + _m(pb_r, pb_i), axis=0)
    wo = wo.transpose(2, 0, 4, 1, 3).reshape(S5_PAIRS, 2, S5_Q, P, 1, L * C)
    wo = wo.transpose(0, 2, 1, 3, 4, 5)
    eye2o = jnp.eye(2, dtype=f32)[None, None, :, None, :, None]
    wo_pair = (wo * eye2o).reshape(S5_PAIRS, S5_Q * 2 * P, 2 * L * C)

    a_l = jnp.stack([pw_r[L, 0], pw_i[L, 0], pw_r[L, 1], pw_i[L, 1]], axis=0).reshape(S5_Q, 1, S5_LANES)
    bf = jnp.bfloat16
    return toe.astype(bf), ws_pair.astype(bf), wo_pair.astype(bf), a_l


def _s5a_kernel(x_ref, t_ref, ws_ref, y1_ref, s_ref):
    x0, x1 = x_ref[0], x_ref[1]
    y1_ref[0] = jnp.dot(x0, t_ref[0], preferred_element_type=jnp.float32)
    y1_ref[1] = jnp.dot(x1, t_ref[1], preferred_element_type=jnp.float32)
    s = jnp.dot(jnp.concatenate([x0, x1], axis=1), ws_ref[0], preferred_element_type=jnp.float32)
    for q in range(S5_Q):
        s_ref[q] = s[:, q * 128:(q + 1) * 128]


def _s5a(x, toe, ws_pair):
    g, r, w = x.shape
    return pl.pallas_call(
        _s5a_kernel,
        grid=(S5_PAIRS,),
        in_specs=[pl.BlockSpec((2, r, w), lambda i: (i, 0, 0)),
                  pl.BlockSpec((2, w, w), lambda i: (i, 0, 0)),
                  pl.BlockSpec((1, 2 * w, S5_Q * 128), lambda i: (i, 0, 0))],
        out_specs=[pl.BlockSpec((2, r, w), lambda i: (i, 0, 0)),
                   pl.BlockSpec((S5_Q, r, 128), lambda i: (0, 0, i))],
        out_shape=[jax.ShapeDtypeStruct((g, r, w), jnp.float32),
                   jax.ShapeDtypeStruct((S5_Q, r, S5_LANES), jnp.float32)],
        compiler_params=pltpu.CompilerParams(dimension_semantics=("arbitrary",),
                                             vmem_limit_bytes=V7X_VMEM_LIMIT),
        name="s5_chunk_local",
    )(x, toe, ws_pair)


def _s5b_kernel(s_ref, a_ref, h0_ref, hp_ref, fin_ref, *, nc):
    def run(qr, qi, order_fwd):
        ar, ai = a_ref[qr], a_ref[qi]

        def body(i, carry):
            hr, hi = carry
            k = i if order_fwd else nc - 1 - i
            hp_ref[qr, pl.ds(k, 1), :] = hr
            hp_ref[qi, pl.ds(k, 1), :] = hi
            nr = ar * hr - ai * hi + s_ref[qr, pl.ds(k, 1), :]
            ni = ar * hi + ai * hr + s_ref[qi, pl.ds(k, 1), :]
            return nr, ni

        hr, hi = lax.fori_loop(0, nc, body, (h0_ref[qr], h0_ref[qi]))
        fin_ref[qr] = hr
        fin_ref[qi] = hi

    run(0, 1, True)
    run(2, 3, False)


S5_SCAN_LANES = 1024


def _s5b(s, a_l, h0):
    _, nc, w = s.shape
    tl = S5_SCAN_LANES
    vec = pl.BlockSpec((S5_Q, 1, tl), lambda i: (0, 0, i))
    seq = pl.BlockSpec((S5_Q, nc, tl), lambda i: (0, 0, i))
    return pl.pallas_call(
        functools.partial(_s5b_kernel, nc=nc),
        grid=(w // tl,),
        in_specs=[seq, vec, vec],
        out_specs=[seq, vec],
        out_shape=[jax.ShapeDtypeStruct((S5_Q, nc, w), jnp.float32),
                   jax.ShapeDtypeStruct((S5_Q, 1, w), jnp.float32)],
        compiler_params=pltpu.CompilerParams(dimension_semantics=("arbitrary",),
                                             vmem_limit_bytes=V7X_VMEM_LIMIT),
        name="s5_chunk_scan",
    )(s, a_l, h0)


def _s5c_kernel(h_ref, wo_ref, y1_ref, y_ref):
    hcat = jnp.concatenate([h_ref[q] for q in range(S5_Q)], axis=1).astype(jnp.bfloat16)
    y2 = jnp.dot(hcat, wo_ref[0], preferred_element_type=jnp.float32)
    w = y1_ref.shape[-1]
    y_ref[0] = (y1_ref[0] + y2[:, :w]).astype(y_ref.dtype)
    y_ref[1] = (y1_ref[1] + y2[:, w:]).astype(y_ref.dtype)


def _s5c(hprev, wo_pair, y1):
    g, r, w = y1.shape
    return pl.pallas_call(
        _s5c_kernel,
        grid=(S5_PAIRS,),
        in_specs=[pl.BlockSpec((S5_Q, r, 128), lambda i: (0, 0, i)),
                  pl.BlockSpec((1, S5_Q * 128, 2 * w), lambda i: (i, 0, 0)),
                  pl.BlockSpec((2, r, w), lambda i: (i, 0, 0))],
        out_specs=pl.BlockSpec((2, r, w), lambda i: (i, 0, 0)),
        out_shape=jax.ShapeDtypeStruct((g, r, w), jnp.float32),
        compiler_params=pltpu.CompilerParams(dimension_semantics=("arbitrary",),
                                             vmem_limit_bytes=V7X_VMEM_LIMIT),
        name="s5_state_to_out",
    )(hprev, wo_pair, y1)


def _s5_to_chunks(u):
    b, t, _ = u.shape
    nc = t // S5_L
    x = u.reshape(b, nc, S5_L, S5_GROUPS, S5_GROUP_CH).transpose(3, 1, 0, 2, 4)
    return x.reshape(S5_GROUPS, nc * b, S5_L * S5_GROUP_CH)


def _s5_from_chunks(y, b, t):
    nc = t // S5_L
    y = y.reshape(S5_GROUPS, nc, b, S5_L, S5_GROUP_CH).transpose(2, 1, 3, 0, 4)
    return y.reshape(b, t, S5_CH)


def _s5_state_planes(h0):
    b = h0.shape[0]
    return h0.transpose(1, 4, 0, 2, 3).reshape(S5_Q, b, S5_LANES)


def _s5_core(us, h0s, tables):
    toe, ws_pair, wo_pair, a_l = tables
    xs = [_s5_to_chunks(u.astype(jnp.bfloat16)) for u in us]
    rows = [x.shape[1] for x in xs]
    y1, s = _s5a(jnp.concatenate(xs, axis=1), toe, ws_pair)
    hps, fins = [], []
    off = 0
    for u, h0, r in zip(us, h0s, rows):
        b, t, _ = u.shape
        nc = t // S5_L
        if h0 is None:
            h0p = jnp.zeros((S5_Q, 1, b * S5_LANES), jnp.float32)
        else:
            h0p = _s5_state_planes(h0.astype(jnp.float32)).reshape(S5_Q, 1, b * S5_LANES)
        hp, fin = _s5b(s[:, off:off + r].reshape(S5_Q, nc, b * S5_LANES), jnp.tile(a_l, (1, 1, b)), h0p)
        hps.append(hp.reshape(S5_Q, r, S5_LANES))
        fins.append(fin.reshape(2, 2, b, S5_GROUPS, S5_STATE).transpose(2, 0, 3, 4, 1))
        off += r
    y = _s5c(jnp.concatenate(hps, axis=1), wo_pair, y1)
    outs, off = [], 0
    for u, r in zip(us, rows):
        b, t, _ = u.shape
        outs.append(_s5_from_chunks(y[:, off:off + r], b, t))
        off += r
    return outs, fins


LRU_TC = 256
LRU_HALO = 8


def _sigmoid(x):
    return 1.0 / (1.0 + jnp.exp(-x))


def _lru_kernel(xp_ref, cw_ref, cb_ref, wg_ref, bg_ref, sp_ref, h0_ref, out_ref, fin_ref,
                a_s, b_s, hf_s, *, t_len, tc):
    f32 = jnp.float32
    nt = t_len // tc
    w = LRU_W

    def gates(r0, d):
        slab = xp_ref[0, pl.ds(r0, tc + 2 * LRU_HALO), :].astype(f32)
        o = LRU_HALO - LRU_CONV // 2
        xc = cb_ref[...] + sum(cw_ref[k:k + 1, :] * slab[o + k:o + k + tc] for k in range(LRU_CONV))
        g = jnp.dot(xc.astype(jnp.bfloat16), wg_ref[d], preferred_element_type=f32) + bg_ref[d]
        r = _sigmoid(g[:, :w])
        i = _sigmoid(g[:, w:])
        log_a = -sp_ref[d] * r
        a_s[...] = jnp.exp(log_a)
        b_s[...] = jnp.sqrt(1.0 - jnp.exp(2.0 * log_a)) * (i * xc)

    def fwd_chunk(c, h):
        r0 = pl.multiple_of(c * tc, tc)
        gates(r0, 0)

        def rows(i, h):
            for j in range(8):
                t = i * 8 + j
                h = a_s[pl.ds(t, 1), :] * h + b_s[pl.ds(t, 1), :]
                hf_s[pl.ds(r0 + t, 1), :] = h
            return h

        return lax.fori_loop(0, tc // 8, rows, h)

    h = lax.fori_loop(0, nt, fwd_chunk, h0_ref[0, 0:1, :])
    fin_ref[0, 0:1, :] = h

    def bwd_chunk(ci, h):
        r0 = pl.multiple_of((nt - 1 - ci) * tc, tc)
        gates(r0, 1)

        def rows(i, h):
            for j in range(8):
                t = tc - 1 - (i * 8 + j)
                h = a_s[pl.ds(t, 1), :] * h + b_s[pl.ds(t, 1), :]
                out_ref[0, pl.ds(r0 + t, 1), :] = hf_s[pl.ds(r0 + t, 1), :] + h
            return h

        return lax.fori_loop(0, tc // 8, rows, h)

    h = lax.fori_loop(0, nt, bwd_chunk, h0_ref[0, 1:2, :])
    fin_ref[0, 1:2, :] = h


def _block_diag(wb):
    n, k, j = wb.shape
    return (wb[:, :, None, :] * jnp.eye(n, dtype=wb.dtype)[:, None, :, None]).reshape(n * k, n * j)


def _lru_tables(conv_w, conv_b, w_a, b_a, w_x, b_x, lam):
    wg = jnp.stack([jnp.concatenate([_block_diag(w_a[d]), _block_diag(w_x[d])], axis=1) for d in range(2)])
    bg = jnp.concatenate([b_a, b_x], axis=-1)[:, None, :]
    sp = (LRU_C * jax.nn.softplus(-lam.astype(jnp.float32)))[:, None, :]
    return conv_w, conv_b[None, :], wg.astype(jnp.bfloat16), bg, sp


def _lru_core(x, h0, tables):
    b, t, w = x.shape
    tc = min(LRU_TC, t)
    cw, cb, wg, bg, sp = tables
    xp = jnp.pad(x.astype(jnp.bfloat16), ((0, 0), (LRU_HALO, LRU_HALO), (0, 0)))
    full = lambda shape: pl.BlockSpec(shape, lambda i: (0,) * len(shape))
    return pl.pallas_call(
        functools.partial(_lru_kernel, t_len=t, tc=tc),
        grid=(b,),
        in_specs=[pl.BlockSpec((1, t + 2 * LRU_HALO, w), lambda i: (i, 0, 0)),
                  full(cw.shape), full(cb.shape), full(wg.shape), full(bg.shape), full(sp.shape),
                  pl.BlockSpec((1, 2, w), lambda i: (i, 0, 0))],
        out_specs=[pl.BlockSpec((1, t, w), lambda i: (i, 0, 0)),
                   pl.BlockSpec((1, 2, w), lambda i: (i, 0, 0))],
        out_shape=[jax.ShapeDtypeStruct((b, t, w), jnp.float32),
                   jax.ShapeDtypeStruct((b, 2, w), jnp.float32)],
        scratch_shapes=[pltpu.VMEM((tc, w), jnp.float32), pltpu.VMEM((tc, w), jnp.float32),
                        pltpu.VMEM((t, w), jnp.float32)],
        compiler_params=pltpu.CompilerParams(dimension_semantics=("arbitrary",),
                                             vmem_limit_bytes=V7X_VMEM_LIMIT),
        name="rglru",
    )(xp, cw, cb, wg, bg, sp, h0)


ATT_TQ = 256
ATT_DQ = 256


def _attn_kernel(q_ref, k_ref, v_ref, o_ref):
    s = lax.dot_general(q_ref[0], k_ref[0], (((1,), (1,)), ((), ())), preferred_element_type=jnp.float32)
    m = jnp.max(s, axis=-1, keepdims=True)
    p = jnp.exp(s - m)
    l = jnp.sum(p, axis=-1, keepdims=True)
    o = jnp.dot(p.astype(jnp.bfloat16), v_ref[0], preferred_element_type=jnp.float32)
    o_ref[0] = (o / l).astype(o_ref.dtype)


def _attention(q, k, v):
    b, tq, _ = q.shape
    tk = k.shape[1]
    tb = min(ATT_TQ, tq)
    return pl.pallas_call(
        _attn_kernel,
        grid=(b, MLA_HEADS, tq // tb),
        in_specs=[pl.BlockSpec((1, tb, ATT_DQ), lambda bi, h, i: (bi, i, h)),
                  pl.BlockSpec((1, tk, ATT_DQ), lambda bi, h, i: (bi, 0, h)),
                  pl.BlockSpec((1, tk, MLA_V), lambda bi, h, i: (bi, 0, h))],
        out_specs=pl.BlockSpec((1, tb, MLA_V), lambda bi, h, i: (bi, i, h)),
        out_shape=jax.ShapeDtypeStruct((b, tq, MLA_HEADS * MLA_V), jnp.bfloat16),
        compiler_params=pltpu.CompilerParams(dimension_semantics=("arbitrary",) * 3,
                                             vmem_limit_bytes=V7X_VMEM_LIMIT),
        name="mla_attention",
    )(q, k, v)


def _ret_tables():
    f32 = jnp.float32
    c = RET_CHUNK
    log_g = jnp.log1p(-jnp.exp2(-5.0 - jnp.arange(RET_HEADS, dtype=f32)))[:, None, None]
    idx = jnp.arange(c, dtype=f32)
    dec = jnp.exp(jnp.abs(idx[:, None] - idx[None, :])[None] * log_g)
    row = lambda e: jnp.broadcast_to(jnp.exp(e[None, :, None] * log_g), (RET_HEADS, c, c))
    return jnp.stack([dec, row(idx + 1.0), row(c - idx), row(c - 1.0 - idx), row(idx)], axis=0)


def _ret_kernel(q_ref, k_ref, v_ref, g_ref, tab_ref, s0_ref, o_ref, fin_ref, acc_s, *, nc):
    f32, bf = jnp.float32, jnp.bfloat16
    c = RET_CHUNK
    dec, xif, xib, zf, zb = (tab_ref[i, 0] for i in range(5))
    g_chunk = xif[c - 1:c, :]

    def chunk(k0):
        sl = pl.ds(pl.multiple_of(k0 * c, c), c)
        return sl, q_ref[0, sl, :], k_ref[0, sl, :].astype(f32) * (RET_DK ** -0.5), v_ref[0, sl, :]

    def state_update(s, kc, z, vc):
        kz = (kc * z).astype(bf)
        return g_chunk * s + lax.dot_general(kz, vc, (((0,), (0,)), ((), ())), preferred_element_type=f32)

    def fwd(i, s):
        sl, qc, kc, vc = chunk(i)
        sc = lax.dot_general(qc, kc.astype(bf), (((1,), (1,)), ((), ())), preferred_element_type=f32) * dec
        inner = jnp.dot(sc.astype(bf), vc, preferred_element_type=f32)
        cross = jnp.dot(qc, s.astype(bf), preferred_element_type=f32) * xif
        acc_s[sl, :] = inner + cross
        return state_update(s, kc, zf, vc)

    fin_ref[0, 0, 0] = lax.fori_loop(0, nc, fwd, s0_ref[0, 0, 0])

    def bwd(i, s):
        sl, qc, kc, vc = chunk(nc - 1 - i)
        o = acc_s[sl, :] + jnp.dot(qc, s.astype(bf), preferred_element_type=f32) * xib
        o = o * lax.rsqrt(jnp.mean(o * o, axis=-1, keepdims=True) + EPS)
        g = g_ref[0, sl, :].astype(f32)
        o_ref[0, sl, :] = (o * (g * _sigmoid(g))).astype(o_ref.dtype)
        return state_update(s, kc, zb, vc)

    fin_ref[0, 1, 0] = lax.fori_loop(0, nc, bwd, s0_ref[0, 1, 0])


def _retention(q, k, v, g, s0, tables):
    b, t, _ = q.shape
    nc = t // RET_CHUNK
    seq = pl.BlockSpec((1, t, RET_DK), lambda bi, h: (bi, 0, h))
    st = pl.BlockSpec((1, 2, 1, RET_DK, RET_DV), lambda bi, h: (bi, 0, h, 0, 0))
    return pl.pallas_call(
        functools.partial(_ret_kernel, nc=nc),
        grid=(b, RET_HEADS),
        in_specs=[seq, seq, seq, seq,
                  pl.BlockSpec((5, 1, RET_CHUNK, RET_CHUNK), lambda bi, h: (0, h, 0, 0)), st],
        out_specs=[seq, st],
        out_shape=[jax.ShapeDtypeStruct((b, t, RET_HEADS * RET_DV), jnp.bfloat16),
                   jax.ShapeDtypeStruct((b, 2, RET_HEADS, RET_DK, RET_DV), jnp.float32)],
        scratch_shapes=[pltpu.VMEM((t, RET_DV), jnp.float32)],
        compiler_params=pltpu.CompilerParams(dimension_semantics=("arbitrary", "arbitrary"),
                                             vmem_limit_bytes=V7X_VMEM_LIMIT),
        name="retention",
    )(q, k, v, g, tables, s0)


MOE_TM = 256
ROUTER_PAD = 128


def _split_bf16(x):
    hi = x.astype(jnp.bfloat16)
    return hi, (x - hi.astype(jnp.float32)).astype(jnp.bfloat16)


def _router_kernel(h_ref, whi_ref, wlo_ref, rb_ref, idx_ref, wgt_ref):
    f32 = jnp.float32
    h = h_ref[...].astype(f32)
    hi, lo = _split_bf16(h)
    dn = (((1,), (1,)), ((), ()))
    lt = (lax.dot_general(whi_ref[...], hi, dn, preferred_element_type=f32)
          + lax.dot_general(whi_ref[...], lo, dn, preferred_element_type=f32)
          + lax.dot_general(wlo_ref[...], hi, dn, preferred_element_type=f32))[:N_EXPERTS]
    m = jnp.max(lt, axis=0, keepdims=True)
    e = jnp.exp(lt - m)
    sc = e / jnp.sum(e, axis=0, keepdims=True)
    sel = sc + rb_ref[...][:N_EXPERTS, 0:1]
    rows = lambda a: [a[j:j + 1, :] for j in range(N_EXPERTS)]
    sel_r, sc_r = rows(sel), rows(sc)
    epg = EXPERTS_PER_GROUP

    def top2sum(a, b, c, d):
        h1, l1, h2, l2 = jnp.maximum(a, b), jnp.minimum(a, b), jnp.maximum(c, d), jnp.minimum(c, d)
        return jnp.maximum(h1, h2) + jnp.maximum(jnp.minimum(h1, h2), jnp.maximum(l1, l2))

    gs = [top2sum(*sel_r[g * epg:(g + 1) * epg]) for g in range(N_EXPERT_GROUPS)]
    best, gi = gs[0], jnp.zeros_like(gs[0], dtype=jnp.int32)
    for g in range(1, N_EXPERT_GROUPS):
        upd = gs[g] > best
        gi = jnp.where(upd, g, gi)
        best = jnp.where(upd, gs[g], best)

    def pick(r, j):
        out = r[j]
        for g in range(1, N_EXPERT_GROUPS):
            out = jnp.where(gi == g, r[g * epg + j], out)
        return out

    v = [pick(sel_r, j) for j in range(epg)]
    s = [pick(sc_r, j) for j in range(epg)]

    def argmax_first(vals):
        bv, bi = vals[0], jnp.zeros_like(gi)
        for j in range(1, epg):
            upd = vals[j] > bv
            bi = jnp.where(upd, j, bi)
            bv = jnp.where(upd, vals[j], bv)
        return bi

    i1 = argmax_first(v)
    neg = jnp.float32(-jnp.inf)
    i2 = argmax_first([jnp.where(i1 == j, neg, v[j]) for j in range(epg)])
    take = lambda i: sum(jnp.where(i == j, s[j], 0.0) for j in range(epg))
    w1, w2 = take(i1), take(i2)
    tot = w1 + w2
    idx_ref[...] = jnp.concatenate([gi * epg + i1, gi * epg + i2], axis=0)
    wgt_ref[...] = jnp.concatenate([w1 / tot, w2 / tot], axis=0)


def _router(h2, router_w, router_b):
    n, d = h2.shape
    wt = jnp.zeros((ROUTER_PAD, d), jnp.float32).at[:N_EXPERTS].set(router_w.T.astype(jnp.float32))
    whi, wlo = _split_bf16(wt)
    rb = jnp.zeros((ROUTER_PAD, 128), jnp.float32).at[:N_EXPERTS].set(router_b.astype(jnp.float32)[:, None])
    tm = MOE_TM
    full = lambda a: pl.BlockSpec(a.shape, lambda i: (0, 0))
    return pl.pallas_call(
        _router_kernel,
        grid=(n // tm,),
        in_specs=[pl.BlockSpec((tm, d), lambda i: (i, 0)), full(whi), full(wlo), full(rb)],
        out_specs=[pl.BlockSpec((TOP_K, tm), lambda i: (0, i)), pl.BlockSpec((TOP_K, tm), lambda i: (0, i))],
        out_shape=[jax.ShapeDtypeStruct((TOP_K, n), jnp.int32), jax.ShapeDtypeStruct((TOP_K, n), jnp.float32)],
        compiler_params=pltpu.CompilerParams(dimension_semantics=("arbitrary",),
                                             vmem_limit_bytes=V7X_VMEM_LIMIT),
        name="moe_router",
    )(h2, whi, wlo, rb)


def _expert_kernel(te_ref, nt_ref, x_ref, w_ref, wg_ref, wu_ref, wd_ref, y_ref):
    @pl.when(pl.program_id(0) < nt_ref[0])
    def _():
        x = x_ref[...]
        g = jnp.dot(x, wg_ref[0], preferred_element_type=jnp.float32)
        u = jnp.dot(x, wu_ref[0], preferred_element_type=jnp.float32)
        act = (g * _sigmoid(g)) * u * w_ref[...]
        y_ref[...] = jnp.dot(act.astype(jnp.bfloat16), wd_ref[0], preferred_element_type=jnp.float32
                             ).astype(y_ref.dtype)

    @pl.when(pl.program_id(0) >= nt_ref[0])
    def _():
        y_ref[...] = jnp.zeros_like(y_ref)


def _experts(xs, ws, tile_expert, n_tiles_used, wg, wu, wd):
    m, d = xs.shape
    tm = MOE_TM
    f = wg.shape[-1]
    return pl.pallas_call(
        _expert_kernel,
        grid_spec=pltpu.PrefetchScalarGridSpec(
            num_scalar_prefetch=2,
            grid=(m // tm,),
            in_specs=[pl.BlockSpec((tm, d), lambda i, te, nt: (i, 0)),
                      pl.BlockSpec((tm, 1), lambda i, te, nt: (i, 0)),
                      pl.BlockSpec((1, d, f), lambda i, te, nt: (te[i], 0, 0)),
                      pl.BlockSpec((1, d, f), lambda i, te, nt: (te[i], 0, 0)),
                      pl.BlockSpec((1, f, d), lambda i, te, nt: (te[i], 0, 0))],
            out_specs=pl.BlockSpec((tm, d), lambda i, te, nt: (i, 0))),
        out_shape=jax.ShapeDtypeStruct((m, d), jnp.bfloat16),
        compiler_params=pltpu.CompilerParams(dimension_semantics=("arbitrary",),
                                             vmem_limit_bytes=V7X_VMEM_LIMIT),
        name="moe_experts",
    )(tile_expert, n_tiles_used, xs, ws, wg, wu, wd)


def _moe_dispatch(idx):
    n = idx.shape[1]
    tm = MOE_TM
    m_pad = TOP_K * n + N_EXPERTS * tm
    e_flat = idx.reshape(-1)
    order = jnp.argsort(e_flat, stable=True)
    counts = jnp.zeros((N_EXPERTS,), jnp.int32).at[e_flat].add(1)
    padded = ((counts + tm - 1) // tm) * tm
    ends = jnp.cumsum(padded)
    starts_pad = ends - padded
    starts_raw = jnp.cumsum(counts) - counts
    e_sorted = e_flat[order]
    pos_sorted = starts_pad[e_sorted] + (jnp.arange(TOP_K * n, dtype=jnp.int32) - starts_raw[e_sorted])
    pos = jnp.zeros((TOP_K * n,), jnp.int32).at[order].set(pos_sorted)
    row_pair = jnp.zeros((m_pad,), jnp.int32).at[pos_sorted].set(order.astype(jnp.int32))
    row_valid = jnp.zeros((m_pad,), jnp.float32).at[pos_sorted].set(1.0)
    tile_start = jnp.arange(m_pad // tm, dtype=jnp.int32) * tm
    tile_expert = jnp.minimum(jnp.searchsorted(ends, tile_start, side='right'), N_EXPERTS - 1).astype(jnp.int32)
    n_tiles_used = (ends[-1] // tm).astype(jnp.int32).reshape(1)
    return pos.reshape(TOP_K, n), row_pair, row_valid, tile_expert, n_tiles_used


def _moe_sparse(h2, router_w, router_b, wg, wu, wd):
    n, d = h2.shape
    idx, wgt = _router(h2, router_w, router_b)
    pos, row_pair, row_valid, tile_expert, n_tiles_used = _moe_dispatch(idx)
    xs = jnp.take(h2.astype(jnp.bfloat16), row_pair % n, axis=0)
    ws = (jnp.take(wgt.reshape(-1), row_pair) * row_valid)[:, None]
    ys = _experts(xs, ws, tile_expert, n_tiles_used, wg, wu, wd)
    return (jnp.take(ys, pos[0], axis=0).astype(jnp.float32)
            + jnp.take(ys, pos[1], axis=0).astype(jnp.float32))


def _rmsnorm(x, g=None):
    xf = x.astype(jnp.float32)
    y = xf * lax.rsqrt(jnp.mean(xf * xf, axis=-1, keepdims=True) + EPS)
    if g is not None:
        y = y * g.astype(jnp.float32)
    return y.astype(x.dtype)


def _grid_angles(t_len, rot_dim):
    rows = t_len // GRID_W
    row = jnp.repeat(jnp.arange(rows, dtype=jnp.float32), GRID_W)
    col = jnp.tile(jnp.arange(GRID_W, dtype=jnp.float32), rows)
    n_freq = rot_dim // 4
    inv = ROPE_BASE ** (-jnp.arange(n_freq, dtype=jnp.float32) / n_freq)
    return row[:, None] * inv[None], col[:, None] * inv[None]


def _rotate(x, ang):
    n = x.shape[-1] // 2
    cos = jnp.cos(ang)[None, :, None, :]
    sin = jnp.sin(ang)[None, :, None, :]
    x1, x2 = x[..., :n], x[..., n:]
    return jnp.concatenate([x1 * cos - x2 * sin, x2 * cos + x1 * sin], axis=-1)


def _axial_rope(x):
    t_len, rot = x.shape[1], x.shape[-1]
    ang_r, ang_c = _grid_angles(t_len, rot)
    xf = x.astype(jnp.float32)
    half = rot // 2
    out = jnp.concatenate([_rotate(xf[..., :half], ang_r), _rotate(xf[..., half:], ang_c)], axis=-1)
    return out.astype(x.dtype)


def _blocked_attention(q, k, v, scale):
    bn, tq, nh, dq = q.shape
    nb = tq // Q_BLOCK
    qb = q.reshape(bn, nb, Q_BLOCK, nh, dq).transpose(1, 0, 2, 3, 4)

    def _one(q_blk):
        s = jnp.einsum('bqhd,bkhd->bhqk', q_blk, k, preferred_element_type=jnp.float32) * scale
        p = jax.nn.softmax(s, axis=-1)
        return jnp.einsum('bhqk,bkhv->bqhv', p.astype(v.dtype), v)

    out = lax.map(_one, qb)
    return out.transpose(1, 0, 2, 3, 4).reshape(bn, tq, nh, v.shape[-1])


def _linear_combine(e1, e2):
    a1, b1 = e1
    a2, b2 = e2
    return a2 * a1, a2 * b1 + b2


def _s5_mixer(u, p, h0):
    f32 = jnp.float32
    bn, t_len, _ = u.shape
    ug = u.astype(f32).reshape(bn, t_len, S5_GROUPS, S5_GROUP_CH)
    b_mat = lax.complex(p['s5_b_re'].astype(f32), p['s5_b_im'].astype(f32))
    bu = jnp.einsum('gpc,btgc->btgp', b_mat, ug.astype(jnp.complex64))
    hs, finals = [], []
    for d in range(2):
        lam = lax.complex(p['s5_a_re'][d].astype(f32), p['s5_a_im'][d].astype(f32))
        dt = jnp.exp(p['s5_log_dt'][d].astype(f32))[:, None]
        a_bar = jnp.exp(lam * dt)
        b_in = ((a_bar - 1.0) / lam) * bu
        a_in = jnp.broadcast_to(a_bar, b_in.shape)
        if d == 1:
            a_in, b_in = jnp.flip(a_in, 1), jnp.flip(b_in, 1)
        a_cum, h = lax.associative_scan(_linear_combine, (a_in, b_in), axis=1)
        if h0 is not None:
            h0c = lax.complex(h0[:, d, ..., 0].astype(f32), h0[:, d, ..., 1].astype(f32))
            h = h + a_cum * h0c[:, None]
        else:
            finals.append(h[:, -1])
        if d == 1:
            h = jnp.flip(h, 1)
        hs.append(h)
    c_mat = lax.complex(p['s5_c_re'].astype(f32), p['s5_c_im'].astype(f32))
    y = jnp.real(jnp.einsum('gcp,btgp->btgc', c_mat, hs[0] + hs[1])).reshape(bn, t_len, S5_CH)
    y = y + p['s5_d'].astype(f32) * u.astype(f32)
    y = jax.nn.gelu(y)
    y = y * jax.nn.sigmoid(y @ p['s5_w_glu'].astype(f32))
    fin = None
    if h0 is None:
        fs = jnp.stack(finals, axis=1)
        fin = jnp.stack([jnp.real(fs), jnp.imag(fs)], axis=-1).astype(u.dtype)
    return y.astype(u.dtype), fin


def _rglru_mixer(xb, gb, p, h0):
    f32 = jnp.float32
    bn, t_len, w = xb.shape
    left = LRU_CONV // 2
    conv = lax.conv_general_dilated(xb.astype(f32), p['lru_conv_w'].astype(f32)[:, None, :],
                                    window_strides=(1,), padding=[(left, LRU_CONV - 1 - left)],
                                    dimension_numbers=('NWC', 'WIO', 'NWC'), feature_group_count=w)
    xc = conv + p['lru_conv_b'].astype(f32)
    xblk = xc.reshape(bn, t_len, LRU_BLOCKS, LRU_BLOCK)
    hs, finals = [], []
    for d in range(2):
        r = jax.nn.sigmoid(jnp.einsum('btnk,nkj->btnj', xblk, p['lru_w_a'][d].astype(f32)).reshape(bn, t_len, w)
                           + p['lru_b_a'][d].astype(f32))
        i = jax.nn.sigmoid(jnp.einsum('btnk,nkj->btnj', xblk, p['lru_w_x'][d].astype(f32)).reshape(bn, t_len, w)
                           + p['lru_b_x'][d].astype(f32))
        log_a = -LRU_C * r * jax.nn.softplus(-p['lru_lambda'][d].astype(f32))
        a = jnp.exp(log_a)
        b = jnp.sqrt(-jnp.expm1(2.0 * log_a)) * (i * xc)
        if d == 1:
            a, b = jnp.flip(a, 1), jnp.flip(b, 1)
        a_cum, h = lax.associative_scan(_linear_combine, (a, b), axis=1)
        if h0 is not None:
            h = h + a_cum * h0[:, d].astype(f32)[:, None]
        else:
            finals.append(h[:, -1])
        if d == 1:
            h = jnp.flip(h, 1)
        hs.append(h)
    y = (hs[0] + hs[1]) * jax.nn.gelu(gb.astype(f32))
    fin = jnp.stack(finals, axis=1).astype(xb.dtype) if h0 is None else None
    return y.astype(xb.dtype), fin


def _chunk_retention(q, k, v, s0, inclusive):
    f32 = jnp.float32
    bn, t_len, nh, _ = q.shape
    n_chunks = t_len // RET_CHUNK
    log_g = jnp.log1p(-jnp.exp2(-5.0 - jnp.arange(nh, dtype=f32)))
    idx = jnp.arange(RET_CHUNK, dtype=f32)
    diff = idx[:, None] - idx[None, :]
    mask = (diff >= 0) if inclusive else (diff > 0)
    decay = jnp.where(mask[None], jnp.exp(jnp.maximum(diff, 0.0)[None] * log_g[:, None, None]), 0.0)
    xi = jnp.exp((idx + 1.0)[:, None] * log_g[None])
    zeta = jnp.exp((RET_CHUNK - 1.0 - idx)[:, None] * log_g[None])
    g_chunk = jnp.exp(RET_CHUNK * log_g)

    def to_chunks(a):
        return a.reshape(bn, n_chunks, RET_CHUNK, nh, a.shape[-1]).transpose(1, 0, 2, 3, 4)

    def step(s, qkv):
        qc, kc, vc = qkv
        scores = jnp.einsum('bqhd,bkhd->bhqk', qc, kc) * decay
        inner = jnp.einsum('bhqk,bkhv->bqhv', scores, vc)
        cross = jnp.einsum('bqhd,bhdv->bqhv', qc, s) * xi[None, :, :, None]
        s_new = g_chunk[None, :, None, None] * s + jnp.einsum('bkhd,bkhv->bhdv', kc * zeta[None, :, :, None], vc)
        return s_new, inner + cross

    s_fin, out = lax.scan(step, s0, (to_chunks(q), to_chunks(k), to_chunks(v)))
    return out.transpose(1, 0, 2, 3, 4).reshape(bn, t_len, nh, v.shape[-1]), s_fin


def _retention_mixer(q, k, v, g, s0):
    f32 = jnp.float32
    bn, t_len, _ = q.shape
    q = q.reshape(bn, t_len, RET_HEADS, RET_DK)
    k = k.reshape(bn, t_len, RET_HEADS, RET_DK)
    v = v.reshape(bn, t_len, RET_HEADS, RET_DV)
    if s0 is not None:
        q, k = _axial_rope(q), _axial_rope(k)
        s0 = s0.astype(f32)
    else:
        s0 = jnp.zeros((bn, 2, RET_HEADS, RET_DK, RET_DV), f32)
    qf, kf, vf = q.astype(f32), k.astype(f32) * (RET_DK ** -0.5), v.astype(f32)
    o_f, s_f = _chunk_retention(qf, kf, vf, s0[:, 0], True)
    o_b, s_b = _chunk_retention(jnp.flip(qf, 1), jnp.flip(kf, 1), jnp.flip(vf, 1), s0[:, 1], False)
    o = _rmsnorm(o_f + jnp.flip(o_b, 1)).reshape(bn, t_len, RET_HEADS * RET_DV)
    o = o * jax.nn.silu(g.astype(f32))
    return o.astype(g.dtype), jnp.stack([s_f, s_b], axis=1).astype(g.dtype)


def _mla_expand(c_kv, w_ukv):
    bn, t_len, _ = c_kv.shape
    kv = _mm3(c_kv, w_ukv).reshape(bn, t_len, MLA_HEADS, MLA_NOPE + MLA_V)
    return kv[..., :MLA_NOPE], kv[..., MLA_NOPE:]


def _mla_mixer(q_lat, kv_lat, k_rope, p, ckv_ctx, kpe_ctx):
    latent = ckv_ctx is not None
    bn, t_len, _ = q_lat.shape
    q = _mm3(_rmsnorm(q_lat, p['mla_q_norm_g']), p['mla_w_uq']).reshape(bn, t_len, MLA_HEADS, MLA_NOPE + MLA_ROPE)
    q_nope, q_pe = q[..., :MLA_NOPE], q[..., MLA_NOPE:]
    c_kv = _rmsnorm(kv_lat, p['mla_kv_norm_g'])
    k_pe = k_rope[:, :, None, :]
    if latent:
        q_pe, k_pe = _axial_rope(q_pe), _axial_rope(k_pe)
    k_nope, v = _mla_expand(c_kv, p['mla_w_ukv'])
    k = jnp.concatenate([k_nope, jnp.broadcast_to(k_pe, (bn, t_len, MLA_HEADS, MLA_ROPE))], axis=-1)
    if latent:
        t_ctx = ckv_ctx.shape[1]
        kn_c, v_c = _mla_expand(ckv_ctx, p['mla_w_ukv'])
        k_c = jnp.concatenate([kn_c, jnp.broadcast_to(kpe_ctx[:, :, None, :], (bn, t_ctx, MLA_HEADS, MLA_ROPE))], axis=-1)
        k = jnp.concatenate([k_c, k], axis=1)
        v = jnp.concatenate([v_c, v], axis=1)
    bf = jnp.bfloat16
    pad = lambda a: jnp.concatenate([a, jnp.zeros(a.shape[:-1] + (ATT_DQ - a.shape[-1],), a.dtype)],
                                    axis=-1).reshape(a.shape[0], a.shape[1], MLA_HEADS * ATT_DQ)
    qf = pad((jnp.concatenate([q_nope, q_pe], axis=-1) * MLA_SCALE).astype(bf))
    out = _attention(qf, pad(k.astype(bf)), v.astype(bf).reshape(bn, -1, MLA_HEADS * MLA_V))
    return out.astype(jnp.float32), c_kv


def _moe(h, router_w, router_b, p):
    f32 = jnp.float32
    bn, t_len, d = h.shape
    t = h.reshape(bn * t_len, d)
    scores = jax.nn.softmax((t @ router_w).astype(f32), axis=-1)
    sel = scores + router_b.astype(f32)
    grp = sel.reshape(-1, N_EXPERT_GROUPS, EXPERTS_PER_GROUP)
    grp_score = lax.top_k(grp, TOP_K)[0].sum(-1)
    g_idx = jnp.argmax(grp_score, axis=-1)
    in_grp = jnp.take_along_axis(grp, g_idx[:, None, None], axis=1)[:, 0]
    _, local = lax.top_k(in_grp, TOP_K)
    e_idx = g_idx[:, None] * EXPERTS_PER_GROUP + local
    w = jnp.take_along_axis(scores, e_idx, axis=-1)
    w = w / jnp.sum(w, axis=-1, keepdims=True)
    combine = jnp.sum(jax.nn.one_hot(e_idx, N_EXPERTS, dtype=f32) * w[..., None], axis=1)
    gate = jnp.einsum('nd,edf->nef', t, p['moe_w_gate'])
    up = jnp.einsum('nd,edf->nef', t, p['moe_w_up'])
    act = jax.nn.silu(gate) * up * combine.astype(t.dtype)[..., None]
    y = jnp.einsum('nef,efd->nd', act, p['moe_w_down'])
    return y.reshape(bn, t_len, d)


def _s5_post(y_lin, u, lp):
    f32 = jnp.float32
    y = y_lin + lp['s5_d'].astype(f32) * u.astype(f32)
    y = jax.nn.gelu(y)
    return y * jax.nn.sigmoid(_mm3(y, lp['s5_w_glu']))


def _layer(xs, mods, lp, router_w, router_b, ctx):
    ckv_c, kpe_c, s5_h0, lru_h0, ret_s0 = ctx
    cuts = [int(cut) for cut in np.cumsum(PART_SIZES)[:-1]]
    parts, gates = [], []
    for x, mod in zip(xs, mods):
        mod = mod.reshape(mod.shape[0], 1, N_MOD, D_MODEL)
        sh1, sc1, g1, sh2, sc2, g2 = (mod[:, :, j] for j in range(N_MOD))
        h = _rmsnorm(x, lp['norm1_g']) * (1.0 + sc1) + sh1
        proj = _mm3(h, lp['w_in_pad'], tn=640)[..., :D_IN]
        parts.append(jnp.split(proj, cuts, axis=-1))
        gates.append((g1, sh2, sc2, g2))

    s5_lin, s5_fins = _s5_core([parts[0][0], parts[1][0]], [None, s5_h0], lp['s5_tables'])
    out_xs, states = [], None
    for si, (x, part) in enumerate(zip(xs, parts)):
        s5_u, mla_q, mla_kv, mla_kr, lru_x, lru_g, ret_q, ret_k, ret_v, ret_g = part
        latent = si == 1
        y_s5 = _s5_post(s5_lin[si], s5_u, lp)
        y_mla, c_kv = _mla_mixer(mla_q, mla_kv, mla_kr, lp, ckv_c if latent else None, kpe_c if latent else None)
        h0 = lru_h0 if latent else jnp.zeros((x.shape[0], 2, LRU_W), jnp.float32)
        hsum, lru_fin = _lru_core(lru_x, h0, lp['lru_tables'])
        y_lru = hsum * jax.nn.gelu(lru_g)
        bn, t_len, _ = x.shape
        bf = jnp.bfloat16
        if latent:
            rope = lambda a: _axial_rope(a.reshape(bn, t_len, RET_HEADS, RET_DK)).reshape(bn, t_len, -1)
            ret_q, ret_k = rope(ret_q), rope(ret_k)
            s0 = ret_s0.astype(jnp.float32)
        else:
            s0 = jnp.zeros((bn, 2, RET_HEADS, RET_DK, RET_DV), jnp.float32)
        y_ret, ret_fin = _retention(ret_q.astype(bf), ret_k.astype(bf), ret_v.astype(bf), ret_g.astype(bf),
                                    s0, lp['ret_tables'])
        mix = _mm3(jnp.concatenate([y_s5, y_mla, y_lru, y_ret.astype(jnp.float32)], axis=-1), lp['w_out'])
        g1, sh2, sc2, g2 = gates[si]
        x = x + g1 * mix
        h2 = _rmsnorm(x, lp['norm2_g']) * (1.0 + sc2) + sh2
        moe = _moe_sparse(h2.reshape(bn * t_len, D_MODEL), router_w, router_b,
                          lp['moe_wg'], lp['moe_wu'], lp['moe_wd']).reshape(bn, t_len, D_MODEL)
        x = x + g2 * moe
        out_xs.append(x)
        if not latent:
            states = (c_kv, mla_kr, s5_fins[0], lru_fin, ret_fin)
    return out_xs, states


def kernel(x_prompt, x_sample, c, cache_mla_ckv, cache_mla_kpe, state_s5, state_lru, state_ret,
           c_ctx, w_ada, b_ada, norm1_g, norm2_g, w_in, w_out,
           s5_a_re, s5_a_im, s5_log_dt, s5_b_re, s5_b_im, s5_c_re, s5_c_im, s5_d, s5_w_glu,
           mla_q_norm_g, mla_w_uq, mla_kv_norm_g, mla_w_ukv,
           lru_conv_w, lru_conv_b, lru_w_a, lru_b_a, lru_w_x, lru_b_x, lru_lambda,
           router_w, router_b, moe_w_gate, moe_w_up, moe_w_down, final_norm_g):
    w_in_pad = jnp.pad(w_in, ((0, 0), (0, 0), (0, 4480 - D_IN)))
    stacked = dict(w_ada=w_ada, b_ada=b_ada, norm1_g=norm1_g, norm2_g=norm2_g, w_in_pad=w_in_pad, w_out=w_out,
                   s5_d=s5_d, s5_w_glu=s5_w_glu,
                   mla_q_norm_g=mla_q_norm_g, mla_w_uq=mla_w_uq, mla_kv_norm_g=mla_kv_norm_g, mla_w_ukv=mla_w_ukv,
                   moe_w_gate=moe_w_gate, moe_w_up=moe_w_up, moe_w_down=moe_w_down)

    xs = [x_prompt, x_sample]
    layer_states = []
    for l in range(DEPTH):
        lp = {name: arr[l] for name, arr in stacked.items()}
        lp['s5_tables'] = _s5_tables(s5_a_re[l], s5_a_im[l], s5_log_dt[l], s5_b_re[l], s5_b_im[l],
                                     s5_c_re[l], s5_c_im[l])
        lp['lru_tables'] = _lru_tables(lru_conv_w[l], lru_conv_b[l], lru_w_a[l], lru_b_a[l],
                                       lru_w_x[l], lru_b_x[l], lru_lambda[l])
        lp['ret_tables'] = _ret_tables()
        lp['moe_wg'] = moe_w_gate[l].astype(jnp.bfloat16)
        lp['moe_wu'] = moe_w_up[l].astype(jnp.bfloat16)
        lp['moe_wd'] = moe_w_down[l].astype(jnp.bfloat16)
        mods = [jax.nn.silu(c_ctx)[None] @ lp['w_ada'] + lp['b_ada'],
                jax.nn.silu(c) @ lp['w_ada'] + lp['b_ada']]
        ctx = (cache_mla_ckv[:, l], cache_mla_kpe[:, l], state_s5[:, l], state_lru[:, l], state_ret[:, l])
        xs, st = _layer(xs, mods, lp, router_w, router_b, ctx)
        layer_states.append(st)
    y_prompt = _rmsnorm(xs[0], final_norm_g)
    y_sample = _rmsnorm(xs[1], final_norm_g)
    new_cache_mla_ckv = jnp.stack([st[0] for st in layer_states], axis=1)
    new_cache_mla_kpe = jnp.stack([st[1] for st in layer_states], axis=1)
    new_state_s5 = jnp.stack([st[2] for st in layer_states], axis=1)
    new_state_lru = jnp.stack([st[3] for st in layer_states], axis=1)
    new_state_ret = jnp.stack([st[4] for st in layer_states], axis=1)

    return (y_prompt, y_sample, new_cache_mla_ckv, new_cache_mla_kpe, new_state_s5, new_state_lru, new_state_ret)
```

```python
import functools
import numpy as np
import jax
import jax.numpy as jnp
from jax import lax
from jax.experimental import pallas as pl
from jax.experimental.pallas import tpu as pltpu

D_MODEL = 2048
DEPTH = 2
GRID_W = 64
EPS = 1e-6
ROPE_BASE = 10000.0
N_MOD = 6
GROUP_W = 512
S5_CH = GROUP_W
S5_GROUP_CH = 16
S5_GROUPS = S5_CH // S5_GROUP_CH
S5_STATE = 64
MLA_HEADS = 4
MLA_NOPE = 128
MLA_ROPE = 64
MLA_V = 128
MLA_Q_RANK = GROUP_W
MLA_KV_RANK = GROUP_W // 2
MLA_SCALE = (MLA_NOPE + MLA_ROPE) ** -0.5
LRU_W = GROUP_W
LRU_CONV = 4
LRU_C = 8.0
RET_HEADS = 4
RET_DK = 128
RET_DV = 128
RET_CHUNK = 128
N_EXPERTS = 16
N_EXPERT_GROUPS = 4
EXPERTS_PER_GROUP = N_EXPERTS // N_EXPERT_GROUPS
TOP_K = 2
D_EXPERT = D_MODEL // 4

V7X_VMEM_LIMIT = 48 * 1024 * 1024
V7X_VMEM_LIMIT_BIG = 56 * 1024 * 1024

C_S5, C_MQ, C_LX, C_LG, C_RQ, C_RK, C_RV, C_RG, C_RQP, C_RKP, C_MKV, C_KR = (
    0, 512, 1024, 1536, 2048, 2560, 3072, 3584, 4096, 4608, 5120, 5376)
P_COLS = 5632
R_S5, R_MQ, R_MKV, R_KR, R_LX, R_LG, R_RQ, R_RK, R_RV, R_RG = (0, 512, 1024, 1280, 1344, 1856, 2368, 2880, 3392, 3904)


def _cparams(n_axes, limit=V7X_VMEM_LIMIT):
    return pltpu.CompilerParams(dimension_semantics=("arbitrary",) * n_axes, vmem_limit_bytes=limit)


def _sigmoid(x):
    return 1.0 / (1.0 + jnp.exp(-x))


def _gelu_tanh(x):
    return 0.5 * x * (1.0 + jnp.tanh(0.7978845608028654 * (x + 0.044715 * (x * x * x))))


def _rms(x):
    return x * lax.rsqrt(jnp.mean(x * x, axis=-1, keepdims=True) + EPS)


def _seq_of_tile(i, tm, n_ctx, dec_seq):
    r = i * tm
    return jnp.where(r < n_ctx, 0, 1 + (r - n_ctx) // dec_seq)


def _mm_kernel(x_ref, w_ref, o_ref):
    o_ref[...] = jnp.dot(x_ref[...].astype(jnp.bfloat16), w_ref[...].astype(jnp.bfloat16),
                         preferred_element_type=jnp.float32).astype(o_ref.dtype)


def _matmul(x, w, tm=512, tn=512, out_dtype=jnp.float32):
    m, k = x.shape
    _, n = w.shape
    tm, tn = min(tm, m), min(tn, n)
    return pl.pallas_call(
        _mm_kernel,
        grid=(n // tn, m // tm),
        in_specs=[pl.BlockSpec((tm, k), lambda j, i: (i, 0)), pl.BlockSpec((k, tn), lambda j, i: (0, j))],
        out_specs=pl.BlockSpec((tm, tn), lambda j, i: (i, j)),
        out_shape=jax.ShapeDtypeStruct((m, n), out_dtype),
        compiler_params=_cparams(2),
        name="matmul",
    )(x, w)


ADA_TN = 1024


def _ada_kernel(c_ref, w_ref, b_ref, o_ref):
    c = c_ref[...]
    s = (c * _sigmoid(c)).astype(jnp.bfloat16)
    o_ref[0] = jnp.dot(s, w_ref[0].astype(jnp.bfloat16), preferred_element_type=jnp.float32) + b_ref[0]


def _ada(cvec, w_ada, b_ada):
    depth, d, n = w_ada.shape
    return pl.pallas_call(
        _ada_kernel,
        grid=(depth, n // ADA_TN),
        in_specs=[pl.BlockSpec((8, d), lambda l, j: (0, 0)),
                  pl.BlockSpec((1, d, ADA_TN), lambda l, j: (l, 0, j)),
                  pl.BlockSpec((1, 1, ADA_TN), lambda l, j: (l, 0, j))],
        out_specs=pl.BlockSpec((1, 8, ADA_TN), lambda l, j: (l, 0, j)),
        out_shape=jax.ShapeDtypeStruct((depth, 8, n), jnp.float32),
        compiler_params=_cparams(2),
        name="adaln_mod",
    )(cvec, w_ada, b_ada.reshape(depth, 1, n))


IN_TM = 512
IN_TN = 512


def _inproj_kernel(x_ref, mod_ref, g_ref, w_ref, o_ref, h_s):
    @pl.when(pl.program_id(1) == 0)
    def _():
        h = _rms(x_ref[...]) * g_ref[...]
        h_s[...] = (h * (1.0 + mod_ref[0, 1:2, :]) + mod_ref[0, 0:1, :]).astype(h_s.dtype)

    o_ref[...] = jnp.dot(h_s[...], w_ref[...], preferred_element_type=jnp.float32).astype(o_ref.dtype)


def _inproj(x, mods, g, w, n_ctx, dec_seq):
    n, d = x.shape
    seq = functools.partial(_seq_of_tile, tm=IN_TM, n_ctx=n_ctx, dec_seq=dec_seq)
    return pl.pallas_call(
        _inproj_kernel,
        grid=(n // IN_TM, P_COLS // IN_TN),
        in_specs=[pl.BlockSpec((IN_TM, d), lambda i, j: (i, 0)),
                  pl.BlockSpec((1, N_MOD, d), lambda i, j: (seq(i), 0, 0)),
                  pl.BlockSpec((1, d), lambda i, j: (0, 0)),
                  pl.BlockSpec((d, IN_TN), lambda i, j: (0, j))],
        out_specs=pl.BlockSpec((IN_TM, IN_TN), lambda i, j: (i, j)),
        out_shape=jax.ShapeDtypeStruct((n, P_COLS), jnp.bfloat16),
        scratch_shapes=[pltpu.VMEM((IN_TM, d), jnp.bfloat16)],
        compiler_params=_cparams(2),
        name="norm_inproj",
    )(x, mods, g, w)


def _rot_partner_cols(w, n_heads, head_dim):
    q = head_dim // 4
    wr = w.reshape(w.shape[0], n_heads, 2, 2, q)
    return jnp.stack([-wr[:, :, :, 1], wr[:, :, :, 0]], axis=3).reshape(w.shape)


def _inproj_weights(w_in):
    part = lambda off, width: w_in[:, off:off + width]
    rq, rk, kr = part(R_RQ, 512), part(R_RK, 512), part(R_KR, MLA_ROPE)
    cols = [part(R_S5, 512), part(R_MQ, 512), part(R_LX, 512), part(R_LG, 512), rq, rk, part(R_RV, 512),
            part(R_RG, 512), _rot_partner_cols(rq, RET_HEADS, RET_DK), _rot_partner_cols(rk, RET_HEADS, RET_DK),
            part(R_MKV, MLA_KV_RANK), kr, _rot_partner_cols(kr, 1, MLA_ROPE)]
    w = jnp.concatenate(cols, axis=1)
    return jnp.pad(w, ((0, 0), (0, P_COLS - w.shape[1]))).astype(jnp.bfloat16)


def _rope_tables(t_len, rot_dim):
    rows = t_len // GRID_W
    row = jnp.repeat(jnp.arange(rows, dtype=jnp.float32), GRID_W)
    col = jnp.tile(jnp.arange(GRID_W, dtype=jnp.float32), rows)
    n_freq = rot_dim // 4
    inv = ROPE_BASE ** (-jnp.arange(n_freq, dtype=jnp.float32) / n_freq)
    ang = jnp.concatenate([row[:, None] * inv[None]] * 2 + [col[:, None] * inv[None]] * 2, axis=1)
    return jnp.cos(ang), jnp.sin(ang)


S5_L = 32
S5_PAIRS = S5_GROUPS // 2
S5_Q = 4
S5_LANES = S5_GROUPS * S5_STATE
S5_SCAN_LANES = 1024


def _s5_tables(a_re, a_im, log_dt, b_re, b_im, c_re, c_im):
    f32 = jnp.float32
    L, G, P, C = S5_L, S5_GROUPS, S5_STATE, S5_GROUP_CH
    dt = jnp.exp(log_dt.astype(f32))[..., None]
    zr, zi = a_re * dt, a_im * dt
    ab_r, ab_i = jnp.exp(zr) * jnp.cos(zi), jnp.exp(zr) * jnp.sin(zi)
    den = a_re * a_re + a_im * a_im
    nr, ni = ab_r - 1.0, ab_i
    be_r = (nr * a_re + ni * a_im) / den
    be_i = (ni * a_re - nr * a_im) / den
    bp_r = be_r[..., None] * b_re[None] - be_i[..., None] * b_im[None]
    bp_i = be_r[..., None] * b_im[None] + be_i[..., None] * b_re[None]
    tau = jnp.arange(L + 1, dtype=f32)[:, None, None, None]
    pw_r = jnp.exp(zr[None] * tau) * jnp.cos(zi[None] * tau)
    pw_i = jnp.exp(zr[None] * tau) * jnp.sin(zi[None] * tau)
    e_r = pw_r[..., None] * bp_r[None] - pw_i[..., None] * bp_i[None]
    e_i = pw_r[..., None] * bp_i[None] + pw_i[..., None] * bp_r[None]
    hp = lax.Precision.HIGHEST
    kk = (jnp.einsum('gcp,tdgpk->tdgkc', c_re, e_r, precision=hp)
          - jnp.einsum('gcp,tdgpk->tdgkc', c_im, e_i, precision=hp))
    idx = jnp.arange(L)
    diff = idx[None, :] - idx[:, None]
    kf = kk[:L, 0][jnp.maximum(diff, 0)]
    kb = kk[:L, 1][jnp.maximum(-diff, 0)]
    toe = (jnp.where((diff >= 0)[:, :, None, None, None], kf, 0.0)
           + jnp.where((diff <= 0)[:, :, None, None, None], kb, 0.0))
    toe = toe.transpose(2, 0, 3, 1, 4).reshape(G, L * C, L * C)

    wsf_r, wsf_i = e_r[:L, 0][::-1], e_i[:L, 0][::-1]
    wsb_r, wsb_i = e_r[:L, 1], e_i[:L, 1]
    ws = jnp.stack([wsf_r, wsf_i, wsb_r, wsb_i], axis=0)
    ws = ws.transpose(2, 1, 4, 0, 3).reshape(S5_PAIRS, 2, L * C, S5_Q, 1, P)
    eye2 = jnp.eye(2, dtype=f32)[None, :, None, None, :, None]
    ws_pair = (ws * eye2).reshape(S5_PAIRS, 2 * L * C, S5_Q * 2 * P)

    pf_r, pf_i = pw_r[1:L + 1, 0], pw_i[1:L + 1, 0]
    pb_r, pb_i = pw_r[1:L + 1, 1][::-1], pw_i[1:L + 1, 1][::-1]

    def _m(p_r, p_i):
        m_r = c_re[None] * p_r[:, :, None, :] - c_im[None] * p_i[:, :, None, :]
        m_i = c_re[None] * p_i[:, :, None, :] + c_im[None] * p_r[:, :, None, :]
        return m_r, -m_i

    wo = jnp.stack(_m(pf_r, pf_i) + _m(pb_r, pb_i), axis=0)
    wo = wo.transpose(2, 0, 4, 1, 3).reshape(S5_PAIRS, 2, S5_Q, P, 1, L * C)
    wo = wo.transpose(0, 2, 1, 3, 4, 5)
    eye2o = jnp.eye(2, dtype=f32)[None, None, :, None, :, None]
    wo_pair = (wo * eye2o).reshape(S5_PAIRS, S5_Q * 2 * P, 2 * L * C)

    a_l = jnp.stack([pw_r[L, 0], pw_i[L, 0], pw_r[L, 1], pw_i[L, 1]], axis=0).reshape(S5_Q, 1, S5_LANES)
    bf = jnp.bfloat16
    return toe.astype(bf), ws_pair.astype(bf), wo_pair.astype(bf), a_l


def _s5a_kernel(x_ref, t_ref, ws_ref, y1_ref, s_ref):
    x0, x1 = x_ref[0], x_ref[1]
    y1_ref[0] = jnp.dot(x0, t_ref[0], preferred_element_type=jnp.float32)
    y1_ref[1] = jnp.dot(x1, t_ref[1], preferred_element_type=jnp.float32)
    s = jnp.dot(jnp.concatenate([x0, x1], axis=1), ws_ref[0], preferred_element_type=jnp.float32)
    for q in range(S5_Q):
        s_ref[q] = s[:, q * 128:(q + 1) * 128]


def _s5a(x, toe, ws_pair):
    g, r, w = x.shape
    return pl.pallas_call(
        _s5a_kernel,
        grid=(S5_PAIRS,),
        in_specs=[pl.BlockSpec((2, r, w), lambda i: (i, 0, 0)),
                  pl.BlockSpec((2, w, w), lambda i: (i, 0, 0)),
                  pl.BlockSpec((1, 2 * w, S5_Q * 128), lambda i: (i, 0, 0))],
        out_specs=[pl.BlockSpec((2, r, w), lambda i: (i, 0, 0)),
                   pl.BlockSpec((S5_Q, r, 128), lambda i: (0, 0, i))],
        out_shape=[jax.ShapeDtypeStruct((g, r, w), jnp.float32),
                   jax.ShapeDtypeStruct((S5_Q, r, S5_LANES), jnp.float32)],
        compiler_params=_cparams(1),
        name="s5_chunk_local",
    )(x, toe, ws_pair)


def _s5b_kernel(s_ref, a_ref, h0_ref, hp_ref, fin_ref, *, nc):
    def run(qr, qi, order_fwd):
        ar, ai = a_ref[qr], a_ref[qi]

        def body(i, carry):
            hr, hi = carry
            k = i if order_fwd else nc - 1 - i
            hp_ref[qr, pl.ds(k, 1), :] = hr
            hp_ref[qi, pl.ds(k, 1), :] = hi
            nr = ar * hr - ai * hi + s_ref[qr, pl.ds(k, 1), :]
            ni = ar * hi + ai * hr + s_ref[qi, pl.ds(k, 1), :]
            return nr, ni

        hr, hi = lax.fori_loop(0, nc, body, (h0_ref[qr], h0_ref[qi]))
        fin_ref[qr] = hr
        fin_ref[qi] = hi

    run(0, 1, True)
    run(2, 3, False)


def _s5b(s, a_l, h0):
    _, nc, w = s.shape
    tl = S5_SCAN_LANES
    vec = pl.BlockSpec((S5_Q, 1, tl), lambda i: (0, 0, i))
    seq = pl.BlockSpec((S5_Q, nc, tl), lambda i: (0, 0, i))
    return pl.pallas_call(
        functools.partial(_s5b_kernel, nc=nc),
        grid=(w // tl,),
        in_specs=[seq, vec, vec],
        out_specs=[seq, vec],
        out_shape=[jax.ShapeDtypeStruct((S5_Q, nc, w), jnp.float32),
                   jax.ShapeDtypeStruct((S5_Q, 1, w), jnp.float32)],
        compiler_params=_cparams(1),
        name="s5_chunk_scan",
    )(s, a_l, h0)


def _s5c_kernel(h_ref, wo_ref, y1_ref, y_ref):
    hcat = jnp.concatenate([h_ref[q] for q in range(S5_Q)], axis=1).astype(jnp.bfloat16)
    y2 = jnp.dot(hcat, wo_ref[0], preferred_element_type=jnp.float32)
    w = y1_ref.shape[-1]
    y_ref[0] = (y1_ref[0] + y2[:, :w]).astype(y_ref.dtype)
    y_ref[1] = (y1_ref[1] + y2[:, w:]).astype(y_ref.dtype)


def _s5c(hprev, wo_pair, y1):
    g, r, w = y1.shape
    return pl.pallas_call(
        _s5c_kernel,
        grid=(S5_PAIRS,),
        in_specs=[pl.BlockSpec((S5_Q, r, 128), lambda i: (0, 0, i)),
                  pl.BlockSpec((1, S5_Q * 128, 2 * w), lambda i: (i, 0, 0)),
                  pl.BlockSpec((2, r, w), lambda i: (i, 0, 0))],
        out_specs=pl.BlockSpec((2, r, w), lambda i: (i, 0, 0)),
        out_shape=jax.ShapeDtypeStruct((g, r, w), jnp.float32),
        compiler_params=_cparams(1),
        name="s5_state_to_out",
    )(hprev, wo_pair, y1)


def _s5_to_chunks(u):
    b, t, _ = u.shape
    nc = t // S5_L
    x = u.reshape(b, nc, S5_L, S5_GROUPS, S5_GROUP_CH).transpose(3, 1, 0, 2, 4)
    return x.reshape(S5_GROUPS, nc * b, S5_L * S5_GROUP_CH)


def _s5_from_chunks(y, b, t):
    nc = t // S5_L
    y = y.reshape(S5_GROUPS, nc, b, S5_L, S5_GROUP_CH).transpose(2, 1, 3, 0, 4)
    return y.reshape(b * t, S5_CH)


def _s5_state_planes(h0):
    b = h0.shape[0]
    return h0.transpose(1, 4, 0, 2, 3).reshape(S5_Q, b, S5_LANES)


def _s5_core(us, h0s, tables):
    toe, ws_pair, wo_pair, a_l = tables
    xs = [_s5_to_chunks(u) for u in us]
    rows = [x.shape[1] for x in xs]
    y1, s = _s5a(jnp.concatenate(xs, axis=1), toe, ws_pair)
    hps, fins = [], []
    off = 0
    for u, h0, r in zip(us, h0s, rows):
        b, t, _ = u.shape
        nc = t // S5_L
        if h0 is None:
            h0p = jnp.zeros((S5_Q, 1, b * S5_LANES), jnp.float32)
        else:
            h0p = _s5_state_planes(h0.astype(jnp.float32)).reshape(S5_Q, 1, b * S5_LANES)
        hp, fin = _s5b(s[:, off:off + r].reshape(S5_Q, nc, b * S5_LANES), jnp.tile(a_l, (1, 1, b)), h0p)
        hps.append(hp.reshape(S5_Q, r, S5_LANES))
        fins.append(fin.reshape(2, 2, b, S5_GROUPS, S5_STATE).transpose(2, 0, 3, 4, 1))
        off += r
    y = _s5c(jnp.concatenate(hps, axis=1), wo_pair, y1)
    outs, off = [], 0
    for u, r in zip(us, rows):
        b, t, _ = u.shape
        outs.append(_s5_from_chunks(y[:, off:off + r], b, t))
        off += r
    return jnp.concatenate(outs, axis=0), fins


LRU_TC = 256
LRU_HALO = 16


def _lru_kernel(x_ref, gate_ref, cw_ref, cb_ref, wg_ref, bg_ref, sp_ref, h0_ref, out_ref, fin_ref,
                a_s, b_s, hf_s, *, t_len, tc):
    f32 = jnp.float32
    nt = t_len // tc
    w = LRU_W

    def gates(c, d):
        r0 = pl.multiple_of(c * tc, tc)
        lo = pl.multiple_of(jnp.maximum(r0 - LRU_HALO, 0), LRU_HALO)
        hi = pl.multiple_of(jnp.minimum(r0 + tc, t_len - LRU_HALO), LRU_HALO)
        prev = jnp.where(c > 0, x_ref[pl.ds(lo, LRU_HALO), :].astype(f32), 0.0)
        nxt = jnp.where(c < nt - 1, x_ref[pl.ds(hi, LRU_HALO), :].astype(f32), 0.0)
        slab = jnp.concatenate([prev, x_ref[pl.ds(r0, tc), :].astype(f32), nxt], axis=0)
        o = LRU_HALO - LRU_CONV // 2
        xc = cb_ref[...] + sum(cw_ref[k:k + 1, :] * slab[o + k:o + k + tc] for k in range(LRU_CONV))
        g = jnp.dot(xc.astype(jnp.bfloat16), wg_ref[d], preferred_element_type=f32) + bg_ref[d]
        r = _sigmoid(g[:, :w])
        i = _sigmoid(g[:, w:])
        log_a = -sp_ref[d] * r
        a_s[...] = jnp.exp(log_a)
        b_s[...] = jnp.sqrt(1.0 - jnp.exp(2.0 * log_a)) * (i * xc)
        return r0

    def fwd_chunk(c, h):
        r0 = gates(c, 0)

        def rows(i, h):
            for j in range(8):
                t = i * 8 + j
                h = a_s[pl.ds(t, 1), :] * h + b_s[pl.ds(t, 1), :]
                hf_s[pl.ds(r0 + t, 1), :] = h
            return h

        return lax.fori_loop(0, tc // 8, rows, h)

    h = lax.fori_loop(0, nt, fwd_chunk, h0_ref[0, 0:1, :])
    fin_ref[0, 0:1, :] = h

    def bwd_chunk(ci, h):
        r0 = gates(nt - 1 - ci, 1)

        def rows(i, h):
            for j in range(8):
                t = tc - 1 - (i * 8 + j)
                h = a_s[pl.ds(t, 1), :] * h + b_s[pl.ds(t, 1), :]
                hf_s[pl.ds(r0 + t, 1), :] = hf_s[pl.ds(r0 + t, 1), :] + h
            return h

        h = lax.fori_loop(0, tc // 8, rows, h)
        sl = pl.ds(r0, tc)
        out_ref[sl, :] = (hf_s[sl, :] * _gelu_tanh(gate_ref[sl, :].astype(f32))).astype(out_ref.dtype)
        return h

    h = lax.fori_loop(0, nt, bwd_chunk, h0_ref[0, 1:2, :])
    fin_ref[0, 1:2, :] = h


def _block_diag(wb):
    n, k, j = wb.shape
    return (wb[:, :, None, :] * jnp.eye(n, dtype=wb.dtype)[:, None, :, None]).reshape(n * k, n * j)


def _lru_tables(conv_w, conv_b, w_a, b_a, w_x, b_x, lam):
    wg = jnp.stack([jnp.concatenate([_block_diag(w_a[d]), _block_diag(w_x[d])], axis=1) for d in range(2)])
    bg = jnp.concatenate([b_a, b_x], axis=-1)[:, None, :]
    sp = (LRU_C * jax.nn.softplus(-lam.astype(jnp.float32)))[:, None, :]
    return conv_w, conv_b[None, :], wg.astype(jnp.bfloat16), bg, sp


def _lru(p, row0, b, t, h0, tables):
    w = LRU_W
    tc = min(LRU_TC, t)
    cw, cb, wg, bg, sp = tables
    full = lambda a: pl.BlockSpec(a.shape, lambda i: (0,) * a.ndim)
    rb = row0 // t
    return pl.pallas_call(
        functools.partial(_lru_kernel, t_len=t, tc=tc),
        grid=(b,),
        in_specs=[pl.BlockSpec((t, w), lambda i: (rb + i, C_LX // w)),
                  pl.BlockSpec((t, w), lambda i: (rb + i, C_LG // w)),
                  full(cw), full(cb), full(wg), full(bg), full(sp),
                  pl.BlockSpec((1, 2, w), lambda i: (i, 0, 0))],
        out_specs=[pl.BlockSpec((t, w), lambda i: (i, 0)),
                   pl.BlockSpec((1, 2, w), lambda i: (i, 0, 0))],
        out_shape=[jax.ShapeDtypeStruct((b * t, w), jnp.bfloat16),
                   jax.ShapeDtypeStruct((b, 2, w), jnp.float32)],
        scratch_shapes=[pltpu.VMEM((tc, w), jnp.float32), pltpu.VMEM((tc, w), jnp.float32),
                        pltpu.VMEM((t, w), jnp.float32)],
        compiler_params=_cparams(1),
        name="rglru",
    )(p, p, cw, cb, wg, bg, sp, h0)


MLA_TM = 512
ATT_DQ = 256


def _mla_prep_kernel(q_ref, kv_ref, kr_ref, tab_ref, gq_ref, gkv_ref, wuq_ref, wukv_ref,
                     qo_ref, ko_ref, vo_ref, ckv_ref):
    f32, bf = jnp.float32, jnp.bfloat16
    tab = tab_ref[...]

    def rope(blk):
        prod = blk * tab
        return prod + pltpu.roll(prod, 64, 1)

    qn = (_rms(q_ref[...].astype(f32)) * gq_ref[...]).astype(bf)
    qq = jnp.dot(qn, wuq_ref[...], preferred_element_type=f32)
    ckv = _rms(kv_ref[...].astype(f32)) * gkv_ref[...]
    ckv_ref[...] = ckv
    kk = jnp.dot(ckv.astype(bf), wukv_ref[...], preferred_element_type=f32)
    lane = lax.broadcasted_iota(jnp.int32, tab.shape, 1)
    kpe = jnp.where(lane < MLA_ROPE, rope(kr_ref[...].astype(f32)), 0.0).astype(bf)
    for h in range(MLA_HEADS):
        o = h * ATT_DQ
        qo_ref[:, o:o + 128] = (qq[:, o:o + 128] * MLA_SCALE).astype(bf)
        qo_ref[:, o + 128:o + 256] = (rope(qq[:, o + 128:o + 256]) * MLA_SCALE).astype(bf)
        ko_ref[:, o:o + 128] = kk[:, o:o + 128].astype(bf)
        ko_ref[:, o + 128:o + 256] = kpe
        vo_ref[:, h * MLA_V:(h + 1) * MLA_V] = kk[:, o + 128:o + 256].astype(bf)


def _mla_weights(w_uq, w_ukv):
    wq = w_uq.reshape(MLA_Q_RANK, MLA_HEADS, MLA_NOPE + MLA_ROPE)
    pe = wq[:, :, MLA_NOPE:].reshape(MLA_Q_RANK, MLA_HEADS * MLA_ROPE)
    pep = _rot_partner_cols(pe, MLA_HEADS, MLA_ROPE).reshape(MLA_Q_RANK, MLA_HEADS, MLA_ROPE)
    wq_ext = jnp.concatenate([wq, pep], axis=-1).reshape(MLA_Q_RANK, MLA_HEADS * ATT_DQ)
    return wq_ext.astype(jnp.bfloat16), w_ukv.astype(jnp.bfloat16)


def _mla_prep(p, tab, gq, gkv, wuq, wukv):
    n = p.shape[0]
    tm = MLA_TM
    full = lambda a: pl.BlockSpec(a.shape, lambda i: (0,) * a.ndim)
    row = lambda width, col: pl.BlockSpec((tm, width), lambda i: (i, col // width))
    return pl.pallas_call(
        _mla_prep_kernel,
        grid=(n // tm,),
        in_specs=[row(MLA_Q_RANK, C_MQ), row(MLA_KV_RANK, C_MKV), row(128, C_KR), row(128, 0),
                  full(gq), full(gkv), full(wuq), full(wukv)],
        out_specs=[row(MLA_HEADS * ATT_DQ, 0), row(MLA_HEADS * ATT_DQ, 0), row(MLA_HEADS * MLA_V, 0),
                   row(MLA_KV_RANK, 0)],
        out_shape=[jax.ShapeDtypeStruct((n, MLA_HEADS * ATT_DQ), jnp.bfloat16),
                   jax.ShapeDtypeStruct((n, MLA_HEADS * ATT_DQ), jnp.bfloat16),
                   jax.ShapeDtypeStruct((n, MLA_HEADS * MLA_V), jnp.bfloat16),
                   jax.ShapeDtypeStruct((n, MLA_KV_RANK), jnp.float32)],
        compiler_params=_cparams(1),
        name="mla_prep",
    )(p, p, p, tab, gq, gkv, wuq, wukv)


ATT_TQ = 256


def _attn_kernel(*refs, two):
    dn = (((1,), (1,)), ((), ()))
    f32 = jnp.float32
    if two:
        q_ref, k_ref, v_ref, k2_ref, v2_ref, o_ref = refs
    else:
        q_ref, k_ref, v_ref, o_ref = refs
    q = q_ref[...]
    s = lax.dot_general(q, k_ref[...], dn, preferred_element_type=f32)
    m = jnp.max(s, axis=-1, keepdims=True)
    if two:
        s2 = lax.dot_general(q, k2_ref[...], dn, preferred_element_type=f32)
        m = jnp.maximum(m, jnp.max(s2, axis=-1, keepdims=True))
    p = jnp.exp(s - m)
    l = jnp.sum(p, axis=-1, keepdims=True)
    o = jnp.dot(p.astype(jnp.bfloat16), v_ref[...], preferred_element_type=f32)
    if two:
        p2 = jnp.exp(s2 - m)
        l = l + jnp.sum(p2, axis=-1, keepdims=True)
        o = o + jnp.dot(p2.astype(jnp.bfloat16), v2_ref[...], preferred_element_type=f32)
    o_ref[...] = (o / l).astype(o_ref.dtype)


def _attention(q, k, v, row0, b, t, k2=None, v2=None):
    tb = min(ATT_TQ, t)
    nq = t // tb
    qb, kb = row0 // tb, row0 // t
    two = k2 is not None
    in_specs = [pl.BlockSpec((tb, ATT_DQ), lambda bi, h, i: (qb + bi * nq + i, h)),
                pl.BlockSpec((t, ATT_DQ), lambda bi, h, i: (kb + bi, h)),
                pl.BlockSpec((t, MLA_V), lambda bi, h, i: (kb + bi, h))]
    args = [q, k, v]
    if two:
        t2 = k2.shape[0] // b
        in_specs += [pl.BlockSpec((t2, ATT_DQ), lambda bi, h, i: (bi, h)),
                     pl.BlockSpec((t2, MLA_V), lambda bi, h, i: (bi, h))]
        args += [k2, v2]
    return pl.pallas_call(
        functools.partial(_attn_kernel, two=two),
        grid=(b, MLA_HEADS, nq),
        in_specs=in_specs,
        out_specs=pl.BlockSpec((tb, MLA_V), lambda bi, h, i: (bi * nq + i, h)),
        out_shape=jax.ShapeDtypeStruct((b * t, MLA_HEADS * MLA_V), jnp.bfloat16),
        compiler_params=_cparams(3),
        name="mla_attention",
    )(*args)


def _ret_tables():
    f32 = jnp.float32
    c = RET_CHUNK
    log_g = jnp.log1p(-jnp.exp2(-5.0 - jnp.arange(RET_HEADS, dtype=f32)))[:, None, None]
    idx = jnp.arange(c, dtype=f32)
    dec = jnp.exp(jnp.abs(idx[:, None] - idx[None, :])[None] * log_g)
    row = lambda e: jnp.broadcast_to(jnp.exp(e[None, :, None] * log_g), (RET_HEADS, c, c))
    return jnp.stack([dec, row(idx + 1.0), row(c - idx), row(c - 1.0 - idx), row(idx)], axis=0)


def _ret_kernel(*refs, nc, rope):
    f32, bf = jnp.float32, jnp.bfloat16
    if rope:
        q_ref, k_ref, v_ref, g_ref, qp_ref, kp_ref, cos_ref, sin_ref, tab_ref, s0_ref, o_ref, fin_ref, acc_s = refs
    else:
        q_ref, k_ref, v_ref, g_ref, tab_ref, s0_ref, o_ref, fin_ref, acc_s = refs
    c = RET_CHUNK
    dec, xif, xib, zf, zb = (tab_ref[i, 0] for i in range(5))
    g_chunk = xif[c - 1:c, :]

    def chunk(k0):
        sl = pl.ds(pl.multiple_of(k0 * c, c), c)
        qc, kc = q_ref[sl, :].astype(f32), k_ref[sl, :].astype(f32)
        if rope:
            cos, sin = cos_ref[sl, :], sin_ref[sl, :]
            qc = qc * cos + qp_ref[sl, :].astype(f32) * sin
            kc = kc * cos + kp_ref[sl, :].astype(f32) * sin
        return sl, qc.astype(bf), kc * (RET_DK ** -0.5), v_ref[sl, :]

    def state_update(s, kc, z, vc):
        kz = (kc * z).astype(bf)
        return g_chunk * s + lax.dot_general(kz, vc, (((0,), (0,)), ((), ())), preferred_element_type=f32)

    def fwd(i, s):
        sl, qc, kc, vc = chunk(i)
        sc = lax.dot_general(qc, kc.astype(bf), (((1,), (1,)), ((), ())), preferred_element_type=f32) * dec
        inner = jnp.dot(sc.astype(bf), vc, preferred_element_type=f32)
        cross = jnp.dot(qc, s.astype(bf), preferred_element_type=f32) * xif
        acc_s[sl, :] = inner + cross
        return state_update(s, kc, zf, vc)

    fin_ref[0, 0, 0] = lax.fori_loop(0, nc, fwd, s0_ref[0, 0, 0])

    def bwd(i, s):
        sl, qc, kc, vc = chunk(nc - 1 - i)
        o = _rms(acc_s[sl, :] + jnp.dot(qc, s.astype(bf), preferred_element_type=f32) * xib)
        g = g_ref[sl, :].astype(f32)
        o_ref[sl, :] = (o * (g * _sigmoid(g))).astype(o_ref.dtype)
        return state_update(s, kc, zb, vc)

    fin_ref[0, 1, 0] = lax.fori_loop(0, nc, bwd, s0_ref[0, 1, 0])


def _retention(p, row0, b, t, s0, tables, rope_tabs=None):
    nc = t // RET_CHUNK
    rb = row0 // t
    rope = rope_tabs is not None
    col = lambda c0: pl.BlockSpec((t, RET_DK), lambda bi, h: (rb + bi, c0 // RET_DK + h))
    st = pl.BlockSpec((1, 2, 1, RET_DK, RET_DV), lambda bi, h: (bi, 0, h, 0, 0))
    in_specs = [col(C_RQ), col(C_RK), col(C_RV), col(C_RG)]
    args = [p, p, p, p]
    if rope:
        tab = pl.BlockSpec((t, RET_DK), lambda bi, h: (0, 0))
        in_specs += [col(C_RQP), col(C_RKP), tab, tab]
        args += [p, p, rope_tabs[0], rope_tabs[1]]
    in_specs += [pl.BlockSpec((5, 1, RET_CHUNK, RET_CHUNK), lambda bi, h: (0, h, 0, 0)), st]
    args += [tables, s0]
    return pl.pallas_call(
        functools.partial(_ret_kernel, nc=nc, rope=rope),
        grid=(b, RET_HEADS),
        in_specs=in_specs,
        out_specs=[pl.BlockSpec((t, RET_DV), lambda bi, h: (bi, h)), st],
        out_shape=[jax.ShapeDtypeStruct((b * t, RET_HEADS * RET_DV), jnp.bfloat16),
                   jax.ShapeDtypeStruct((b, 2, RET_HEADS, RET_DK, RET_DV), jnp.float32)],
        scratch_shapes=[pltpu.VMEM((t, RET_DV), jnp.float32)],
        compiler_params=_cparams(2),
        name="retention",
    )(*args)


OUT_TM = 256
ROUTER_PAD = 128


def _split_bf16(x):
    hi = x.astype(jnp.bfloat16)
    return hi, (x - hi.astype(jnp.float32)).astype(jnp.bfloat16)


def _route(h, whi_ref, wlo_ref, rb_ref):
    f32 = jnp.float32
    hi, lo = _split_bf16(h)
    dn = (((1,), (1,)), ((), ()))
    lt = (lax.dot_general(whi_ref[...], hi, dn, preferred_element_type=f32)
          + lax.dot_general(whi_ref[...], lo, dn, preferred_element_type=f32)
          + lax.dot_general(wlo_ref[...], hi, dn, preferred_element_type=f32))[:N_EXPERTS]
    m = jnp.max(lt, axis=0, keepdims=True)
    e = jnp.exp(lt - m)
    sc = e / jnp.sum(e, axis=0, keepdims=True)
    sel = sc + rb_ref[...][:N_EXPERTS, 0:1]
    rows = lambda a: [a[j:j + 1, :] for j in range(N_EXPERTS)]
    sel_r, sc_r = rows(sel), rows(sc)
    epg = EXPERTS_PER_GROUP

    def top2sum(a, b, c, d):
        h1, l1, h2, l2 = jnp.maximum(a, b), jnp.minimum(a, b), jnp.maximum(c, d), jnp.minimum(c, d)
        return jnp.maximum(h1, h2) + jnp.maximum(jnp.minimum(h1, h2), jnp.maximum(l1, l2))

    gs = [top2sum(*sel_r[g * epg:(g + 1) * epg]) for g in range(N_EXPERT_GROUPS)]
    best, gi = gs[0], jnp.zeros_like(gs[0], dtype=jnp.int32)
    for g in range(1, N_EXPERT_GROUPS):
        upd = gs[g] > best
        gi = jnp.where(upd, g, gi)
        best = jnp.where(upd, gs[g], best)

    def pick(r, j):
        out = r[j]
        for g in range(1, N_EXPERT_GROUPS):
            out = jnp.where(gi == g, r[g * epg + j], out)
        return out

    v = [pick(sel_r, j) for j in range(epg)]
    s = [pick(sc_r, j) for j in range(epg)]

    def argmax_first(vals):
        bv, bi = vals[0], jnp.zeros_like(gi)
        for j in range(1, epg):
            upd = vals[j] > bv
            bi = jnp.where(upd, j, bi)
            bv = jnp.where(upd, vals[j], bv)
        return bi

    i1 = argmax_first(v)
    neg = jnp.float32(-jnp.inf)
    i2 = argmax_first([jnp.where(i1 == j, neg, v[j]) for j in range(epg)])
    take = lambda i: sum(jnp.where(i == j, s[j], 0.0) for j in range(epg))
    w1, w2 = take(i1), take(i2)
    tot = w1 + w2
    return (jnp.concatenate([gi * epg + i1, gi * epg + i2], axis=0),
            jnp.concatenate([w1 / tot, w2 / tot], axis=0))


def _mixout_kernel(ylin_ref, u_ref, ymla_ref, ylru_ref, yret_ref, x_ref, mod_ref, d_ref, wglu_ref, wout_ref,
                   g2n_ref, whi_ref, wlo_ref, rb_ref, xo_ref, h2_ref, idx_ref, wgt_ref):
    f32, bf = jnp.float32, jnp.bfloat16
    w = GROUP_W
    y = _gelu_tanh(ylin_ref[...] + d_ref[...] * u_ref[...].astype(f32))
    y5 = (y * _sigmoid(jnp.dot(y.astype(bf), wglu_ref[...], preferred_element_type=f32))).astype(bf)
    mix = (jnp.dot(y5, wout_ref[0:w, :], preferred_element_type=f32)
           + jnp.dot(ymla_ref[...], wout_ref[w:2 * w, :], preferred_element_type=f32)
           + jnp.dot(ylru_ref[...], wout_ref[2 * w:3 * w, :], preferred_element_type=f32)
           + jnp.dot(yret_ref[...], wout_ref[3 * w:4 * w, :], preferred_element_type=f32))
    x = x_ref[...] + mod_ref[0, 2:3, :] * mix
    xo_ref[...] = x
    h2 = _rms(x) * g2n_ref[...] * (1.0 + mod_ref[0, 4:5, :]) + mod_ref[0, 3:4, :]
    h2_ref[...] = h2.astype(bf)
    idx, wgt = _route(h2, whi_ref, wlo_ref, rb_ref)
    idx_ref[...] = idx
    wgt_ref[...] = wgt


def _router_weights(router_w, router_b):
    d = router_w.shape[0]
    wt = jnp.zeros((ROUTER_PAD, d), jnp.float32).at[:N_EXPERTS].set(router_w.T.astype(jnp.float32))
    whi, wlo = _split_bf16(wt)
    rb = jnp.zeros((ROUTER_PAD, 128), jnp.float32).at[:N_EXPERTS].set(router_b.astype(jnp.float32)[:, None])
    return whi, wlo, rb


def _mixout(ylin, p, ymla, ylru, yret, x, mods, s5_d, wglu, wout, g2n, rw, n_ctx, dec_seq):
    n, d = x.shape
    tm = OUT_TM
    w = GROUP_W
    seq = functools.partial(_seq_of_tile, tm=tm, n_ctx=n_ctx, dec_seq=dec_seq)
    full = lambda a: pl.BlockSpec(a.shape, lambda i: (0,) * a.ndim)
    row = lambda width: pl.BlockSpec((tm, width), lambda i: (i, 0))
    lanes = pl.BlockSpec((TOP_K, tm), lambda i: (0, i))
    whi, wlo, rb = rw
    return pl.pallas_call(
        _mixout_kernel,
        grid=(n // tm,),
        in_specs=[row(w), row(w), row(w), row(w), row(w), row(d),
                  pl.BlockSpec((1, N_MOD, d), lambda i: (seq(i), 0, 0)),
                  full(s5_d), full(wglu), full(wout), full(g2n), full(whi), full(wlo), full(rb)],
        out_specs=[row(d), row(d), lanes, lanes],
        out_shape=[jax.ShapeDtypeStruct((n, d), jnp.float32), jax.ShapeDtypeStruct((n, d), jnp.bfloat16),
                   jax.ShapeDtypeStruct((TOP_K, n), jnp.int32), jax.ShapeDtypeStruct((TOP_K, n), jnp.float32)],
        compiler_params=_cparams(1),
        name="mix_out_norm_route",
    )(ylin, p, ymla, ylru, yret, x, mods, s5_d, wglu, wout, g2n, whi, wlo, rb)


MOE_TM = 256


def _expert_kernel(te_ref, nt_ref, x_ref, w_ref, wg_ref, wu_ref, wd_ref, y_ref, wg_s, wu_s, wd_s):
    i = pl.program_id(0)
    bf = jnp.bfloat16

    @pl.when((i == 0) | (te_ref[i] != te_ref[jnp.maximum(i - 1, 0)]))
    def _():
        wg_s[...] = wg_ref[0].astype(bf)
        wu_s[...] = wu_ref[0].astype(bf)
        wd_s[...] = wd_ref[0].astype(bf)

    @pl.when(i < nt_ref[0])
    def _():
        x = x_ref[...]
        g = jnp.dot(x, wg_s[...], preferred_element_type=jnp.float32)
        u = jnp.dot(x, wu_s[...], preferred_element_type=jnp.float32)
        act = (g * _sigmoid(g)) * u * w_ref[...]
        y_ref[...] = jnp.dot(act.astype(bf), wd_s[...], preferred_element_type=jnp.float32).astype(y_ref.dtype)

    @pl.when(i >= nt_ref[0])
    def _():
        y_ref[...] = jnp.zeros_like(y_ref)


def _experts(xs, ws, tile_expert, n_tiles_used, wg, wu, wd):
    m, d = xs.shape
    tm = MOE_TM
    f = wg.shape[-1]
    return pl.pallas_call(
        _expert_kernel,
        grid_spec=pltpu.PrefetchScalarGridSpec(
            num_scalar_prefetch=2,
            grid=(m // tm,),
            in_specs=[pl.BlockSpec((tm, d), lambda i, te, nt: (i, 0)),
                      pl.BlockSpec((tm, 1), lambda i, te, nt: (i, 0)),
                      pl.BlockSpec((1, d, f), lambda i, te, nt: (te[i], 0, 0)),
                      pl.BlockSpec((1, d, f), lambda i, te, nt: (te[i], 0, 0)),
                      pl.BlockSpec((1, f, d), lambda i, te, nt: (te[i], 0, 0))],
            out_specs=pl.BlockSpec((tm, d), lambda i, te, nt: (i, 0)),
            scratch_shapes=[pltpu.VMEM((d, f), jnp.bfloat16), pltpu.VMEM((d, f), jnp.bfloat16),
                            pltpu.VMEM((f, d), jnp.bfloat16)]),
        out_shape=jax.ShapeDtypeStruct((m, d), jnp.bfloat16),
        compiler_params=_cparams(1, V7X_VMEM_LIMIT_BIG),
        name="moe_experts",
    )(tile_expert, n_tiles_used, xs, ws, wg, wu, wd)


def _moe_dispatch(idx):
    n = idx.shape[1]
    tm = MOE_TM
    m_pad = TOP_K * n + N_EXPERTS * tm
    e_flat = idx.reshape(-1)
    order = jnp.argsort(e_flat, stable=True)
    counts = jnp.zeros((N_EXPERTS,), jnp.int32).at[e_flat].add(1)
    padded = ((counts + tm - 1) // tm) * tm
    ends = jnp.cumsum(padded)
    starts_pad = ends - padded
    starts_raw = jnp.cumsum(counts) - counts
    e_sorted = e_flat[order]
    pos_sorted = starts_pad[e_sorted] + (jnp.arange(TOP_K * n, dtype=jnp.int32) - starts_raw[e_sorted])
    pos = jnp.zeros((TOP_K * n,), jnp.int32).at[order].set(pos_sorted)
    row_pair = jnp.zeros((m_pad,), jnp.int32).at[pos_sorted].set(order.astype(jnp.int32))
    row_valid = jnp.zeros((m_pad,), jnp.float32).at[pos_sorted].set(1.0)
    tile_start = jnp.arange(m_pad // tm, dtype=jnp.int32) * tm
    tile_expert = jnp.minimum(jnp.searchsorted(ends, tile_start, side='right'), N_EXPERTS - 1).astype(jnp.int32)
    n_tiles_used = (ends[-1] // tm).astype(jnp.int32).reshape(1)
    return pos.reshape(TOP_K, n), row_pair, row_valid, tile_expert, n_tiles_used


def _moe_rows(h2, idx, wgt, wg, wu, wd):
    n = h2.shape[0]
    pos, row_pair, row_valid, tile_expert, n_tiles_used = _moe_dispatch(idx)
    xs = jnp.take(h2, row_pair % n, axis=0)
    ws = (jnp.take(wgt.reshape(-1), row_pair) * row_valid)[:, None]
    ys = _experts(xs, ws, tile_expert, n_tiles_used, wg, wu, wd)
    return jnp.take(ys, pos[0], axis=0), jnp.take(ys, pos[1], axis=0)


RES_TM = 512


def _resid_kernel(x_ref, ya_ref, yb_ref, mod_ref, gf_ref, xo_ref, yo_ref):
    f32 = jnp.float32
    x = x_ref[...] + mod_ref[0, 5:6, :] * (ya_ref[...].astype(f32) + yb_ref[...].astype(f32))
    xo_ref[...] = x
    yo_ref[...] = _rms(x) * gf_ref[...]


def _resid(x, ya, yb, mods, gf, n_ctx, dec_seq):
    n, d = x.shape
    tm = RES_TM
    seq = functools.partial(_seq_of_tile, tm=tm, n_ctx=n_ctx, dec_seq=dec_seq)
    row = pl.BlockSpec((tm, d), lambda i: (i, 0))
    return pl.pallas_call(
        _resid_kernel,
        grid=(n // tm,),
        in_specs=[row, row, row, pl.BlockSpec((1, N_MOD, d), lambda i: (seq(i), 0, 0)),
                  pl.BlockSpec((1, d), lambda i: (0, 0))],
        out_specs=[row, row],
        out_shape=[jax.ShapeDtypeStruct((n, d), jnp.float32), jax.ShapeDtypeStruct((n, d), jnp.float32)],
        compiler_params=_cparams(1),
        name="moe_residual_norm",
    )(x, ya, yb, mods, gf)


def kernel(x_prompt, x_sample, c, cache_mla_ckv, cache_mla_kpe, state_s5, state_lru, state_ret,
           c_ctx, w_ada, b_ada, norm1_g, norm2_g, w_in, w_out,
           s5_a_re, s5_a_im, s5_log_dt, s5_b_re, s5_b_im, s5_c_re, s5_c_im, s5_d, s5_w_glu,
           mla_q_norm_g, mla_w_uq, mla_kv_norm_g, mla_w_ukv,
           lru_conv_w, lru_conv_b, lru_w_a, lru_b_a, lru_w_x, lru_b_x, lru_lambda,
           router_w, router_b, moe_w_gate, moe_w_up, moe_w_down, final_norm_g):
    f32, bf = jnp.float32, jnp.bfloat16
    batch, seq_len, d = x_prompt.shape
    dec_batch, dec_seq, _ = x_sample.shape
    past = cache_mla_ckv.shape[2]
    n_ctx, n_lat = batch * seq_len, dec_batch * dec_seq
    depth = w_in.shape[0]

    x = jnp.concatenate([x_prompt.reshape(n_ctx, d), x_sample.reshape(n_lat, d)], axis=0)
    cvec = jnp.zeros((8, d), f32).at[0].set(c_ctx).at[1:1 + dec_batch].set(c)
    mods_all = _ada(cvec, w_ada, b_ada).reshape(depth, 8, N_MOD, d)

    cos64, sin64 = _rope_tables(dec_seq, MLA_ROPE)
    lat_tab = jnp.tile(jnp.concatenate([cos64, sin64], axis=1), (dec_batch, 1))
    ctx_tab = jnp.concatenate([jnp.ones((n_ctx, MLA_ROPE), f32), jnp.zeros((n_ctx, MLA_ROPE), f32)], axis=1)
    mla_tab = jnp.concatenate([ctx_tab, lat_tab], axis=0)
    ret_rope = _rope_tables(dec_seq, RET_DK)
    ret_tables = _ret_tables()
    rw = _router_weights(router_w, router_b)
    gf = final_norm_g[None, :]

    states = []
    y_final = None
    for l in range(depth):
        mods = mods_all[l]
        s5_tab = _s5_tables(s5_a_re[l], s5_a_im[l], s5_log_dt[l], s5_b_re[l], s5_b_im[l], s5_c_re[l], s5_c_im[l])
        lru_tab = _lru_tables(lru_conv_w[l], lru_conv_b[l], lru_w_a[l], lru_b_a[l], lru_w_x[l], lru_b_x[l],
                              lru_lambda[l])
        wuq, wukv = _mla_weights(mla_w_uq[l], mla_w_ukv[l])

        p = _inproj(x, mods, norm1_g[l][None, :], _inproj_weights(w_in[l]), n_ctx, dec_seq)

        u_ctx = p[:n_ctx, C_S5:C_S5 + S5_CH].reshape(batch, seq_len, S5_CH)
        u_lat = p[n_ctx:, C_S5:C_S5 + S5_CH].reshape(dec_batch, dec_seq, S5_CH)
        ylin, s5_fins = _s5_core([u_ctx, u_lat], [None, state_s5[:, l]], s5_tab)

        qo, ko, vo, ckv = _mla_prep(p, mla_tab, mla_q_norm_g[l][None, :], mla_kv_norm_g[l][None, :], wuq, wukv)
        ckv_c = cache_mla_ckv[:, l].reshape(dec_batch * past, MLA_KV_RANK)
        kv_c = _matmul(ckv_c, wukv, out_dtype=bf).reshape(dec_batch * past, MLA_HEADS, 2, 128)
        kpe_c = jnp.broadcast_to(cache_mla_kpe[:, l].reshape(dec_batch * past, 1, MLA_ROPE).astype(bf),
                                 (dec_batch * past, MLA_HEADS, MLA_ROPE))
        k_c = jnp.concatenate([kv_c[:, :, 0], kpe_c, jnp.zeros_like(kpe_c)], axis=-1).reshape(
            dec_batch * past, MLA_HEADS * ATT_DQ)
        v_c = kv_c[:, :, 1].reshape(dec_batch * past, MLA_HEADS * MLA_V)
        ymla = jnp.concatenate([_attention(qo, ko, vo, 0, batch, seq_len),
                                _attention(qo, ko, vo, n_ctx, dec_batch, dec_seq, k_c, v_c)], axis=0)

        ylru_c, lru_fin = _lru(p, 0, batch, seq_len, jnp.zeros((batch, 2, LRU_W), f32), lru_tab)
        ylru_l, _ = _lru(p, n_ctx, dec_batch, dec_seq, state_lru[:, l].astype(f32), lru_tab)
        ylru = jnp.concatenate([ylru_c, ylru_l], axis=0)

        yret_c, ret_fin = _retention(p, 0, batch, seq_len,
                                     jnp.zeros((batch, 2, RET_HEADS, RET_DK, RET_DV), f32), ret_tables)
        yret_l, _ = _retention(p, n_ctx, dec_batch, dec_seq, state_ret[:, l].astype(f32), ret_tables, ret_rope)
        yret = jnp.concatenate([yret_c, yret_l], axis=0)

        x_mid, h2, idx, wgt = _mixout(ylin, p, ymla, ylru, yret, x, mods, s5_d[l][None, :],
                                      s5_w_glu[l].astype(bf), w_out[l].astype(bf), norm2_g[l][None, :], rw,
                                      n_ctx, dec_seq)
        ya, yb = _moe_rows(h2, idx, wgt, moe_w_gate[l], moe_w_up[l], moe_w_down[l])
        x, y_final = _resid(x_mid, ya, yb, mods, gf, n_ctx, dec_seq)

        states.append((ckv[:n_ctx].reshape(batch, seq_len, MLA_KV_RANK),
                       p[:n_ctx, C_KR:C_KR + MLA_ROPE].astype(f32).reshape(batch, seq_len, MLA_ROPE),
                       s5_fins[0], lru_fin, ret_fin))

    y_prompt = y_final[:n_ctx].reshape(batch, seq_len, d)
    y_sample = y_final[n_ctx:].reshape(dec_batch, dec_seq, d)
    new_cache_mla_ckv = jnp.stack([st[0] for st in states], axis=1)
    new_cache_mla_kpe = jnp.stack([st[1] for st in states], axis=1)
    new_state_s5 = jnp.stack([st[2] for st in states], axis=1)
    new_state_lru = jnp.stack([st[3] for st in states], axis=1)
    new_state_ret = jnp.stack([st[4] for st in states], axis=1)
    return (y_prompt, y_sample, new_cache_mla_ckv, new_cache_mla_kpe, new_state_s5, new_state_lru, new_state_ret)
```

```python
import functools
import numpy as np
import jax
import jax.numpy as jnp
from jax import lax
from jax.experimental import pallas as pl
from jax.experimental.pallas import tpu as pltpu

D_MODEL = 2048
DEPTH = 2
GRID_W = 64
EPS = 1e-6
ROPE_BASE = 10000.0
N_MOD = 6
GROUP_W = 512
S5_CH = GROUP_W
S5_GROUP_CH = 16
S5_GROUPS = S5_CH // S5_GROUP_CH
S5_STATE = 64
MLA_HEADS = 4
MLA_NOPE = 128
MLA_ROPE = 64
MLA_V = 128
MLA_Q_RANK = GROUP_W
MLA_KV_RANK = GROUP_W // 2
MLA_SCALE = (MLA_NOPE + MLA_ROPE) ** -0.5
LRU_W = GROUP_W
LRU_CONV = 4
LRU_C = 8.0
RET_HEADS = 4
RET_DK = 128
RET_DV = 128
RET_CHUNK = 128
N_EXPERTS = 16
N_EXPERT_GROUPS = 4
EXPERTS_PER_GROUP = N_EXPERTS // N_EXPERT_GROUPS
TOP_K = 2
D_EXPERT = D_MODEL // 4

V7X_VMEM_LIMIT = 48 * 1024 * 1024
V7X_VMEM_LIMIT_BIG = 56 * 1024 * 1024

C_S5, C_MQ, C_LX, C_LG, C_RQ, C_RK, C_RV, C_RG, C_RQP, C_RKP, C_MKV, C_KR = (
    0, 512, 1024, 1536, 2048, 2560, 3072, 3584, 4096, 4608, 5120, 5376)
P_COLS = 5632
R_S5, R_MQ, R_MKV, R_KR, R_LX, R_LG, R_RQ, R_RK, R_RV, R_RG = (0, 512, 1024, 1280, 1344, 1856, 2368, 2880, 3392, 3904)


def _cparams(n_axes, limit=V7X_VMEM_LIMIT):
    return pltpu.CompilerParams(dimension_semantics=("arbitrary",) * n_axes, vmem_limit_bytes=limit)


def _sigmoid(x):
    return 1.0 / (1.0 + jnp.exp(-x))


def _gelu_tanh(x):
    return 0.5 * x * (1.0 + jnp.tanh(0.7978845608028654 * (x + 0.044715 * (x * x * x))))


def _rms(x):
    return x * lax.rsqrt(jnp.mean(x * x, axis=-1, keepdims=True) + EPS)


def _seq_of_tile(i, tm, n_ctx, dec_seq):
    r = i * tm
    return jnp.where(r < n_ctx, 0, 1 + (r - n_ctx) // dec_seq)


def _mm_kernel(x_ref, w_ref, o_ref):
    o_ref[...] = jnp.dot(x_ref[...].astype(jnp.bfloat16), w_ref[...].astype(jnp.bfloat16),
                         preferred_element_type=jnp.float32).astype(o_ref.dtype)


def _matmul(x, w, tm=512, tn=512, out_dtype=jnp.float32):
    m, k = x.shape
    _, n = w.shape
    tm, tn = min(tm, m), min(tn, n)
    return pl.pallas_call(
        _mm_kernel,
        grid=(n // tn, m // tm),
        in_specs=[pl.BlockSpec((tm, k), lambda j, i: (i, 0)), pl.BlockSpec((k, tn), lambda j, i: (0, j))],
        out_specs=pl.BlockSpec((tm, tn), lambda j, i: (i, j)),
        out_shape=jax.ShapeDtypeStruct((m, n), out_dtype),
        compiler_params=_cparams(2),
        name="matmul",
    )(x, w)


ADA_TN = 1024


def _ada_kernel(c_ref, w_ref, b_ref, o_ref):
    c = c_ref[...]
    s = (c * _sigmoid(c)).astype(jnp.bfloat16)
    o_ref[0] = jnp.dot(s, w_ref[0].astype(jnp.bfloat16), preferred_element_type=jnp.float32) + b_ref[0]


def _ada(cvec, w_ada, b_ada):
    depth, d, n = w_ada.shape
    return pl.pallas_call(
        _ada_kernel,
        grid=(depth, n // ADA_TN),
        in_specs=[pl.BlockSpec((8, d), lambda l, j: (0, 0)),
                  pl.BlockSpec((1, d, ADA_TN), lambda l, j: (l, 0, j)),
                  pl.BlockSpec((1, 1, ADA_TN), lambda l, j: (l, 0, j))],
        out_specs=pl.BlockSpec((1, 8, ADA_TN), lambda l, j: (l, 0, j)),
        out_shape=jax.ShapeDtypeStruct((depth, 8, n), jnp.float32),
        compiler_params=_cparams(2),
        name="adaln_mod",
    )(cvec, w_ada, b_ada.reshape(depth, 1, n))


IN_TM = 512
IN_TN = 512


def _inproj_kernel(x_ref, mod_ref, g_ref, w_ref, o_ref, h_s):
    @pl.when(pl.program_id(1) == 0)
    def _():
        h = _rms(x_ref[...]) * g_ref[...]
        h_s[...] = (h * (1.0 + mod_ref[0, 1:2, :]) + mod_ref[0, 0:1, :]).astype(h_s.dtype)

    o_ref[...] = jnp.dot(h_s[...], w_ref[...], preferred_element_type=jnp.float32).astype(o_ref.dtype)


def _inproj(x, mods, g, w, n_ctx, dec_seq):
    n, d = x.shape
    seq = functools.partial(_seq_of_tile, tm=IN_TM, n_ctx=n_ctx, dec_seq=dec_seq)
    return pl.pallas_call(
        _inproj_kernel,
        grid=(n // IN_TM, P_COLS // IN_TN),
        in_specs=[pl.BlockSpec((IN_TM, d), lambda i, j: (i, 0)),
                  pl.BlockSpec((1, N_MOD, d), lambda i, j: (seq(i), 0, 0)),
                  pl.BlockSpec((1, d), lambda i, j: (0, 0)),
                  pl.BlockSpec((d, IN_TN), lambda i, j: (0, j))],
        out_specs=pl.BlockSpec((IN_TM, IN_TN), lambda i, j: (i, j)),
        out_shape=jax.ShapeDtypeStruct((n, P_COLS), jnp.bfloat16),
        scratch_shapes=[pltpu.VMEM((IN_TM, d), jnp.bfloat16)],
        compiler_params=_cparams(2),
        name="norm_inproj",
    )(x, mods, g, w)


def _rot_partner_cols(w, n_heads, head_dim):
    q = head_dim // 4
    wr = w.reshape(w.shape[0], n_heads, 2, 2, q)
    return jnp.stack([-wr[:, :, :, 1], wr[:, :, :, 0]], axis=3).reshape(w.shape)


def _inproj_weights(w_in):
    part = lambda off, width: w_in[:, off:off + width]
    rq, rk, kr = part(R_RQ, 512), part(R_RK, 512), part(R_KR, MLA_ROPE)
    cols = [part(R_S5, 512), part(R_MQ, 512), part(R_LX, 512), part(R_LG, 512), rq, rk, part(R_RV, 512),
            part(R_RG, 512), _rot_partner_cols(rq, RET_HEADS, RET_DK), _rot_partner_cols(rk, RET_HEADS, RET_DK),
            part(R_MKV, MLA_KV_RANK), kr, _rot_partner_cols(kr, 1, MLA_ROPE)]
    w = jnp.concatenate(cols, axis=1)
    return jnp.pad(w, ((0, 0), (0, P_COLS - w.shape[1]))).astype(jnp.bfloat16)


def _rope_tables(t_len, rot_dim):
    rows = t_len // GRID_W
    row = jnp.repeat(jnp.arange(rows, dtype=jnp.float32), GRID_W)
    col = jnp.tile(jnp.arange(GRID_W, dtype=jnp.float32), rows)
    n_freq = rot_dim // 4
    inv = ROPE_BASE ** (-jnp.arange(n_freq, dtype=jnp.float32) / n_freq)
    ang = jnp.concatenate([row[:, None] * inv[None]] * 2 + [col[:, None] * inv[None]] * 2, axis=1)
    return jnp.cos(ang), jnp.sin(ang)


S5_L = 32
S5_PAIRS = S5_GROUPS // 2
S5_Q = 4
S5_LANES = S5_GROUPS * S5_STATE
S5_SCAN_LANES = 1024


def _s5_tables(a_re, a_im, log_dt, b_re, b_im, c_re, c_im):
    f32 = jnp.float32
    L, G, P, C = S5_L, S5_GROUPS, S5_STATE, S5_GROUP_CH
    dt = jnp.exp(log_dt.astype(f32))[..., None]
    zr, zi = a_re * dt, a_im * dt
    ab_r, ab_i = jnp.exp(zr) * jnp.cos(zi), jnp.exp(zr) * jnp.sin(zi)
    den = a_re * a_re + a_im * a_im
    nr, ni = ab_r - 1.0, ab_i
    be_r = (nr * a_re + ni * a_im) / den
    be_i = (ni * a_re - nr * a_im) / den
    bp_r = be_r[..., None] * b_re[None] - be_i[..., None] * b_im[None]
    bp_i = be_r[..., None] * b_im[None] + be_i[..., None] * b_re[None]
    tau = jnp.arange(L + 1, dtype=f32)[:, None, None, None]
    pw_r = jnp.exp(zr[None] * tau) * jnp.cos(zi[None] * tau)
    pw_i = jnp.exp(zr[None] * tau) * jnp.sin(zi[None] * tau)
    e_r = pw_r[..., None] * bp_r[None] - pw_i[..., None] * bp_i[None]
    e_i = pw_r[..., None] * bp_i[None] + pw_i[..., None] * bp_r[None]
    hp = lax.Precision.HIGHEST
    kk = (jnp.einsum('gcp,tdgpk->tdgkc', c_re, e_r, precision=hp)
          - jnp.einsum('gcp,tdgpk->tdgkc', c_im, e_i, precision=hp))
    idx = jnp.arange(L)
    diff = idx[None, :] - idx[:, None]
    kf = kk[:L, 0][jnp.maximum(diff, 0)]
    kb = kk[:L, 1][jnp.maximum(-diff, 0)]
    toe = (jnp.where((diff >= 0)[:, :, None, None, None], kf, 0.0)
           + jnp.where((diff <= 0)[:, :, None, None, None], kb, 0.0))
    toe = toe.transpose(2, 0, 3, 1, 4).reshape(G, L * C, L * C)

    wsf_r, wsf_i = e_r[:L, 0][::-1], e_i[:L, 0][::-1]
    wsb_r, wsb_i = e_r[:L, 1], e_i[:L, 1]
    ws = jnp.stack([wsf_r, wsf_i, wsb_r, wsb_i], axis=0)
    ws = ws.transpose(2, 1, 4, 0, 3).reshape(S5_PAIRS, 2, L * C, S5_Q, 1, P)
    eye2 = jnp.eye(2, dtype=f32)[None, :, None, None, :, None]
    ws_pair = (ws * eye2).reshape(S5_PAIRS, 2 * L * C, S5_Q * 2 * P)

    pf_r, pf_i = pw_r[1:L + 1, 0], pw_i[1:L + 1, 0]
    pb_r, pb_i = pw_r[1:L + 1, 1][::-1], pw_i[1:L + 1, 1][::-1]

    def _m(p_r, p_i):
        m_r = c_re[None] * p_r[:, :, None, :] - c_im[None] * p_i[:, :, None, :]
        m_i = c_re[None] * p_i[:, :, None, :] + c_im[None] * p_r[:, :, None, :]
        return m_r, -m_i

    wo = jnp.stack(_m(pf_r, pf_i) + _m(pb_r, pb_i), axis=0)
    wo = wo.transpose(2, 0, 4, 1, 3).reshape(S5_PAIRS, 2, S5_Q, P, 1, L * C)
    wo = wo.transpose(0, 2, 1, 3, 4, 5)
    eye2o = jnp.eye(2, dtype=f32)[None, None, :, None, :, None]
    wo_pair = (wo * eye2o).reshape(S5_PAIRS, S5_Q * 2 * P, 2 * L * C)

    a_l = jnp.stack([pw_r[L, 0], pw_i[L, 0], pw_r[L, 1], pw_i[L, 1]], axis=0).reshape(S5_Q, 1, S5_LANES)
    bf = jnp.bfloat16
    return toe.astype(bf), ws_pair.astype(bf), wo_pair.astype(bf), a_l


def _s5a_kernel(x_ref, t_ref, ws_ref, y1_ref, s_ref):
    x0, x1 = x_ref[0], x_ref[1]
    y1_ref[0] = jnp.dot(x0, t_ref[0], preferred_element_type=jnp.float32)
    y1_ref[1] = jnp.dot(x1, t_ref[1], preferred_element_type=jnp.float32)
    s = jnp.dot(jnp.concatenate([x0, x1], axis=1), ws_ref[0], preferred_element_type=jnp.float32)
    for q in range(S5_Q):
        s_ref[q] = s[:, q * 128:(q + 1) * 128]


def _s5a(x, toe, ws_pair):
    g, r, w = x.shape
    return pl.pallas_call(
        _s5a_kernel,
        grid=(S5_PAIRS,),
        in_specs=[pl.BlockSpec((2, r, w), lambda i: (i, 0, 0)),
                  pl.BlockSpec((2, w, w), lambda i: (i, 0, 0)),
                  pl.BlockSpec((1, 2 * w, S5_Q * 128), lambda i: (i, 0, 0))],
        out_specs=[pl.BlockSpec((2, r, w), lambda i: (i, 0, 0)),
                   pl.BlockSpec((S5_Q, r, 128), lambda i: (0, 0, i))],
        out_shape=[jax.ShapeDtypeStruct((g, r, w), jnp.float32),
                   jax.ShapeDtypeStruct((S5_Q, r, S5_LANES), jnp.float32)],
        compiler_params=_cparams(1),
        name="s5_chunk_local",
    )(x, toe, ws_pair)


def _s5b_kernel(s_ref, a_ref, h0_ref, hp_ref, fin_ref, *, nc):
    def run(qr, qi, order_fwd):
        ar, ai = a_ref[qr], a_ref[qi]

        def body(i, carry):
            hr, hi = carry
            k = i if order_fwd else nc - 1 - i
            hp_ref[qr, pl.ds(k, 1), :] = hr
            hp_ref[qi, pl.ds(k, 1), :] = hi
            nr = ar * hr - ai * hi + s_ref[qr, pl.ds(k, 1), :]
            ni = ar * hi + ai * hr + s_ref[qi, pl.ds(k, 1), :]
            return nr, ni

        hr, hi = lax.fori_loop(0, nc, body, (h0_ref[qr], h0_ref[qi]))
        fin_ref[qr] = hr
        fin_ref[qi] = hi

    run(0, 1, True)
    run(2, 3, False)


def _s5b(s, a_l, h0):
    _, nc, w = s.shape
    tl = S5_SCAN_LANES
    vec = pl.BlockSpec((S5_Q, 1, tl), lambda i: (0, 0, i))
    seq = pl.BlockSpec((S5_Q, nc, tl), lambda i: (0, 0, i))
    return pl.pallas_call(
        functools.partial(_s5b_kernel, nc=nc),
        grid=(w // tl,),
        in_specs=[seq, vec, vec],
        out_specs=[seq, vec],
        out_shape=[jax.ShapeDtypeStruct((S5_Q, nc, w), jnp.float32),
                   jax.ShapeDtypeStruct((S5_Q, 1, w), jnp.float32)],
        compiler_params=_cparams(1),
        name="s5_chunk_scan",
    )(s, a_l, h0)


def _s5c_kernel(h_ref, wo_ref, y1_ref, y_ref):
    hcat = jnp.concatenate([h_ref[q] for q in range(S5_Q)], axis=1).astype(jnp.bfloat16)
    y2 = jnp.dot(hcat, wo_ref[0], preferred_element_type=jnp.float32)
    w = y1_ref.shape[-1]
    y_ref[0] = (y1_ref[0] + y2[:, :w]).astype(y_ref.dtype)
    y_ref[1] = (y1_ref[1] + y2[:, w:]).astype(y_ref.dtype)


def _s5c(hprev, wo_pair, y1):
    g, r, w = y1.shape
    return pl.pallas_call(
        _s5c_kernel,
        grid=(S5_PAIRS,),
        in_specs=[pl.BlockSpec((S5_Q, r, 128), lambda i: (0, 0, i)),
                  pl.BlockSpec((1, S5_Q * 128, 2 * w), lambda i: (i, 0, 0)),
                  pl.BlockSpec((2, r, w), lambda i: (i, 0, 0))],
        out_specs=pl.BlockSpec((2, r, w), lambda i: (i, 0, 0)),
        out_shape=jax.ShapeDtypeStruct((g, r, w), jnp.float32),
        compiler_params=_cparams(1),
        name="s5_state_to_out",
    )(hprev, wo_pair, y1)


def _s5_to_chunks(u):
    b, t, _ = u.shape
    nc = t // S5_L
    x = u.reshape(b, nc, S5_L, S5_GROUPS, S5_GROUP_CH).transpose(3, 1, 0, 2, 4)
    return x.reshape(S5_GROUPS, nc * b, S5_L * S5_GROUP_CH)


def _s5_from_chunks(y, b, t):
    nc = t // S5_L
    y = y.reshape(S5_GROUPS, nc, b, S5_L, S5_GROUP_CH).transpose(2, 1, 3, 0, 4)
    return y.reshape(b * t, S5_CH)


def _s5_state_planes(h0):
    b = h0.shape[0]
    return h0.transpose(1, 4, 0, 2, 3).reshape(S5_Q, b, S5_LANES)


def _s5_core(us, h0s, tables):
    toe, ws_pair, wo_pair, a_l = tables
    xs = [_s5_to_chunks(u) for u in us]
    rows = [x.shape[1] for x in xs]
    y1, s = _s5a(jnp.concatenate(xs, axis=1), toe, ws_pair)
    hps, fins = [], []
    off = 0
    for u, h0, r in zip(us, h0s, rows):
        b, t, _ = u.shape
        nc = t // S5_L
        if h0 is None:
            h0p = jnp.zeros((S5_Q, 1, b * S5_LANES), jnp.float32)
        else:
            h0p = _s5_state_planes(h0.astype(jnp.float32)).reshape(S5_Q, 1, b * S5_LANES)
        hp, fin = _s5b(s[:, off:off + r].reshape(S5_Q, nc, b * S5_LANES), jnp.tile(a_l, (1, 1, b)), h0p)
        hps.append(hp.reshape(S5_Q, r, S5_LANES))
        fins.append(fin.reshape(2, 2, b, S5_GROUPS, S5_STATE).transpose(2, 0, 3, 4, 1))
        off += r
    y = _s5c(jnp.concatenate(hps, axis=1), wo_pair, y1)
    outs, off = [], 0
    for u, r in zip(us, rows):
        b, t, _ = u.shape
        outs.append(_s5_from_chunks(y[:, off:off + r], b, t))
        off += r
    return jnp.concatenate(outs, axis=0), fins


LRU_TC = 256
LRU_HALO = 16


def _lru_kernel(x_ref, gate_ref, cw_ref, cb_ref, wg_ref, bg_ref, sp_ref, h0_ref, out_ref, fin_ref,
                a_s, b_s, hf_s, *, t_len, tc):
    f32 = jnp.float32
    nt = t_len // tc
    w = LRU_W

    def gates(c, d):
        r0 = pl.multiple_of(c * tc, tc)
        lo = pl.multiple_of(jnp.maximum(r0 - LRU_HALO, 0), LRU_HALO)
        hi = pl.multiple_of(jnp.minimum(r0 + tc, t_len - LRU_HALO), LRU_HALO)
        prev = jnp.where(c > 0, x_ref[pl.ds(lo, LRU_HALO), :].astype(f32), 0.0)
        nxt = jnp.where(c < nt - 1, x_ref[pl.ds(hi, LRU_HALO), :].astype(f32), 0.0)
        slab = jnp.concatenate([prev, x_ref[pl.ds(r0, tc), :].astype(f32), nxt], axis=0)
        o = LRU_HALO - LRU_CONV // 2
        xc = cb_ref[...] + sum(cw_ref[k:k + 1, :] * slab[o + k:o + k + tc] for k in range(LRU_CONV))
        g = jnp.dot(xc.astype(jnp.bfloat16), wg_ref[d], preferred_element_type=f32) + bg_ref[d]
        r = _sigmoid(g[:, :w])
        i = _sigmoid(g[:, w:])
        log_a = -sp_ref[d] * r
        a_s[...] = jnp.exp(log_a)
        b_s[...] = jnp.sqrt(1.0 - jnp.exp(2.0 * log_a)) * (i * xc)
        return r0

    def fwd_chunk(c, h):
        r0 = gates(c, 0)

        def rows(i, h):
            for j in range(8):
                t = i * 8 + j
                h = a_s[pl.ds(t, 1), :] * h + b_s[pl.ds(t, 1), :]
                hf_s[pl.ds(r0 + t, 1), :] = h
            return h

        return lax.fori_loop(0, tc // 8, rows, h)

    h = lax.fori_loop(0, nt, fwd_chunk, h0_ref[0, 0:1, :])
    fin_ref[0, 0:1, :] = h

    def bwd_chunk(ci, h):
        r0 = gates(nt - 1 - ci, 1)

        def rows(i, h):
            for j in range(8):
                t = tc - 1 - (i * 8 + j)
                h = a_s[pl.ds(t, 1), :] * h + b_s[pl.ds(t, 1), :]
                hf_s[pl.ds(r0 + t, 1), :] = hf_s[pl.ds(r0 + t, 1), :] + h
            return h

        h = lax.fori_loop(0, tc // 8, rows, h)
        sl = pl.ds(r0, tc)
        out_ref[sl, :] = (hf_s[sl, :] * _gelu_tanh(gate_ref[sl, :].astype(f32))).astype(out_ref.dtype)
        return h

    h = lax.fori_loop(0, nt, bwd_chunk, h0_ref[0, 1:2, :])
    fin_ref[0, 1:2, :] = h


def _block_diag(wb):
    n, k, j = wb.shape
    return (wb[:, :, None, :] * jnp.eye(n, dtype=wb.dtype)[:, None, :, None]).reshape(n * k, n * j)


def _lru_tables(conv_w, conv_b, w_a, b_a, w_x, b_x, lam):
    wg = jnp.stack([jnp.concatenate([_block_diag(w_a[d]), _block_diag(w_x[d])], axis=1) for d in range(2)])
    bg = jnp.concatenate([b_a, b_x], axis=-1)[:, None, :]
    sp = (LRU_C * jax.nn.softplus(-lam.astype(jnp.float32)))[:, None, :]
    return conv_w, conv_b[None, :], wg.astype(jnp.bfloat16), bg, sp


def _lru(p, row0, b, t, h0, tables):
    w = LRU_W
    tc = min(LRU_TC, t)
    cw, cb, wg, bg, sp = tables
    full = lambda a: pl.BlockSpec(a.shape, lambda i: (0,) * a.ndim)
    rb = row0 // t
    return pl.pallas_call(
        functools.partial(_lru_kernel, t_len=t, tc=tc),
        grid=(b,),
        in_specs=[pl.BlockSpec((t, w), lambda i: (rb + i, C_LX // w)),
                  pl.BlockSpec((t, w), lambda i: (rb + i, C_LG // w)),
                  full(cw), full(cb), full(wg), full(bg), full(sp),
                  pl.BlockSpec((1, 2, w), lambda i: (i, 0, 0))],
        out_specs=[pl.BlockSpec((t, w), lambda i: (i, 0)),
                   pl.BlockSpec((1, 2, w), lambda i: (i, 0, 0))],
        out_shape=[jax.ShapeDtypeStruct((b * t, w), jnp.bfloat16),
                   jax.ShapeDtypeStruct((b, 2, w), jnp.float32)],
        scratch_shapes=[pltpu.VMEM((tc, w), jnp.float32), pltpu.VMEM((tc, w), jnp.float32),
                        pltpu.VMEM((t, w), jnp.float32)],
        compiler_params=_cparams(1),
        name="rglru",
    )(p, p, cw, cb, wg, bg, sp, h0)


MLA_TM = 512
ATT_DQ = 256


def _mla_prep_kernel(q_ref, kv_ref, kr_ref, tab_ref, gq_ref, gkv_ref, wuq_ref, wukv_ref,
                     qo_ref, ko_ref, vo_ref, ckv_ref):
    f32, bf = jnp.float32, jnp.bfloat16
    tab = tab_ref[...]

    def rope(blk):
        prod = blk * tab
        return prod + pltpu.roll(prod, 64, 1)

    qn = (_rms(q_ref[...].astype(f32)) * gq_ref[...]).astype(bf)
    qq = jnp.dot(qn, wuq_ref[...], preferred_element_type=f32)
    ckv = _rms(kv_ref[...].astype(f32)) * gkv_ref[...]
    ckv_ref[...] = ckv
    kk = jnp.dot(ckv.astype(bf), wukv_ref[...], preferred_element_type=f32)
    lane = lax.broadcasted_iota(jnp.int32, tab.shape, 1)
    kpe = jnp.where(lane < MLA_ROPE, rope(kr_ref[...].astype(f32)), 0.0).astype(bf)
    for h in range(MLA_HEADS):
        o = h * ATT_DQ
        qo_ref[:, o:o + 128] = (qq[:, o:o + 128] * MLA_SCALE).astype(bf)
        qo_ref[:, o + 128:o + 256] = (rope(qq[:, o + 128:o + 256]) * MLA_SCALE).astype(bf)
        ko_ref[:, o:o + 128] = kk[:, o:o + 128].astype(bf)
        ko_ref[:, o + 128:o + 256] = kpe
        vo_ref[:, h * MLA_V:(h + 1) * MLA_V] = kk[:, o + 128:o + 256].astype(bf)


def _mla_weights(w_uq, w_ukv):
    wq = w_uq.reshape(MLA_Q_RANK, MLA_HEADS, MLA_NOPE + MLA_ROPE)
    pe = wq[:, :, MLA_NOPE:].reshape(MLA_Q_RANK, MLA_HEADS * MLA_ROPE)
    pep = _rot_partner_cols(pe, MLA_HEADS, MLA_ROPE).reshape(MLA_Q_RANK, MLA_HEADS, MLA_ROPE)
    wq_ext = jnp.concatenate([wq, pep], axis=-1).reshape(MLA_Q_RANK, MLA_HEADS * ATT_DQ)
    return wq_ext.astype(jnp.bfloat16), w_ukv.astype(jnp.bfloat16)


def _mla_prep(p, tab, gq, gkv, wuq, wukv):
    n = p.shape[0]
    tm = MLA_TM
    full = lambda a: pl.BlockSpec(a.shape, lambda i: (0,) * a.ndim)
    row = lambda width, col: pl.BlockSpec((tm, width), lambda i: (i, col // width))
    return pl.pallas_call(
        _mla_prep_kernel,
        grid=(n // tm,),
        in_specs=[row(MLA_Q_RANK, C_MQ), row(MLA_KV_RANK, C_MKV), row(128, C_KR), row(128, 0),
                  full(gq), full(gkv), full(wuq), full(wukv)],
        out_specs=[row(MLA_HEADS * ATT_DQ, 0), row(MLA_HEADS * ATT_DQ, 0), row(MLA_HEADS * MLA_V, 0),
                   row(MLA_KV_RANK, 0)],
        out_shape=[jax.ShapeDtypeStruct((n, MLA_HEADS * ATT_DQ), jnp.bfloat16),
                   jax.ShapeDtypeStruct((n, MLA_HEADS * ATT_DQ), jnp.bfloat16),
                   jax.ShapeDtypeStruct((n, MLA_HEADS * MLA_V), jnp.bfloat16),
                   jax.ShapeDtypeStruct((n, MLA_KV_RANK), jnp.float32)],
        compiler_params=_cparams(1),
        name="mla_prep",
    )(p, p, p, tab, gq, gkv, wuq, wukv)


ATT_TQ = 256


def _attn_kernel(*refs, two):
    dn = (((1,), (1,)), ((), ()))
    f32 = jnp.float32
    if two:
        q_ref, k_ref, v_ref, k2_ref, v2_ref, o_ref = refs
    else:
        q_ref, k_ref, v_ref, o_ref = refs
    q = q_ref[...]
    s = lax.dot_general(q, k_ref[...], dn, preferred_element_type=f32)
    m = jnp.max(s, axis=-1, keepdims=True)
    if two:
        s2 = lax.dot_general(q, k2_ref[...], dn, preferred_element_type=f32)
        m = jnp.maximum(m, jnp.max(s2, axis=-1, keepdims=True))
    p = jnp.exp(s - m)
    l = jnp.sum(p, axis=-1, keepdims=True)
    o = jnp.dot(p.astype(jnp.bfloat16), v_ref[...], preferred_element_type=f32)
    if two:
        p2 = jnp.exp(s2 - m)
        l = l + jnp.sum(p2, axis=-1, keepdims=True)
        o = o + jnp.dot(p2.astype(jnp.bfloat16), v2_ref[...], preferred_element_type=f32)
    o_ref[...] = (o / l).astype(o_ref.dtype)


def _attention(q, k, v, row0, b, t, k2=None, v2=None):
    tb = min(ATT_TQ, t)
    nq = t // tb
    qb, kb = row0 // tb, row0 // t
    two = k2 is not None
    in_specs = [pl.BlockSpec((tb, ATT_DQ), lambda bi, h, i: (qb + bi * nq + i, h)),
                pl.BlockSpec((t, ATT_DQ), lambda bi, h, i: (kb + bi, h)),
                pl.BlockSpec((t, MLA_V), lambda bi, h, i: (kb + bi, h))]
    args = [q, k, v]
    if two:
        t2 = k2.shape[0] // b
        in_specs += [pl.BlockSpec((t2, ATT_DQ), lambda bi, h, i: (bi, h)),
                     pl.BlockSpec((t2, MLA_V), lambda bi, h, i: (bi, h))]
        args += [k2, v2]
    return pl.pallas_call(
        functools.partial(_attn_kernel, two=two),
        grid=(b, MLA_HEADS, nq),
        in_specs=in_specs,
        out_specs=pl.BlockSpec((tb, MLA_V), lambda bi, h, i: (bi * nq + i, h)),
        out_shape=jax.ShapeDtypeStruct((b * t, MLA_HEADS * MLA_V), jnp.bfloat16),
        compiler_params=_cparams(3),
        name="mla_attention",
    )(*args)


def _ret_tables():
    f32 = jnp.float32
    c = RET_CHUNK
    log_g = jnp.log1p(-jnp.exp2(-5.0 - jnp.arange(RET_HEADS, dtype=f32)))[:, None, None]
    idx = jnp.arange(c, dtype=f32)
    dec = jnp.exp(jnp.abs(idx[:, None] - idx[None, :])[None] * log_g)
    row = lambda e: jnp.broadcast_to(jnp.exp(e[None, :, None] * log_g), (RET_HEADS, c, c))
    return jnp.stack([dec, row(idx + 1.0), row(c - idx), row(c - 1.0 - idx), row(idx)], axis=0)


def _ret_kernel(*refs, nc, rope):
    f32, bf = jnp.float32, jnp.bfloat16
    if rope:
        q_ref, k_ref, v_ref, g_ref, qp_ref, kp_ref, cos_ref, sin_ref, tab_ref, s0_ref, o_ref, fin_ref, acc_s = refs
    else:
        q_ref, k_ref, v_ref, g_ref, tab_ref, s0_ref, o_ref, fin_ref, acc_s = refs
    c = RET_CHUNK
    dec, xif, xib, zf, zb = (tab_ref[i, 0] for i in range(5))
    g_chunk = xif[c - 1:c, :]

    def chunk(k0):
        sl = pl.ds(pl.multiple_of(k0 * c, c), c)
        qc, kc = q_ref[sl, :].astype(f32), k_ref[sl, :].astype(f32)
        if rope:
            cos, sin = cos_ref[sl, :], sin_ref[sl, :]
            qc = qc * cos + qp_ref[sl, :].astype(f32) * sin
            kc = kc * cos + kp_ref[sl, :].astype(f32) * sin
        return sl, qc.astype(bf), kc * (RET_DK ** -0.5), v_ref[sl, :]

    def state_update(s, kc, z, vc):
        kz = (kc * z).astype(bf)
        return g_chunk * s + lax.dot_general(kz, vc, (((0,), (0,)), ((), ())), preferred_element_type=f32)

    def fwd(i, s):
        sl, qc, kc, vc = chunk(i)
        sc = lax.dot_general(qc, kc.astype(bf), (((1,), (1,)), ((), ())), preferred_element_type=f32) * dec
        inner = jnp.dot(sc.astype(bf), vc, preferred_element_type=f32)
        cross = jnp.dot(qc, s.astype(bf), preferred_element_type=f32) * xif
        acc_s[sl, :] = inner + cross
        return state_update(s, kc, zf, vc)

    fin_ref[0, 0, 0] = lax.fori_loop(0, nc, fwd, s0_ref[0, 0, 0])

    def bwd(i, s):
        sl, qc, kc, vc = chunk(nc - 1 - i)
        o = _rms(acc_s[sl, :] + jnp.dot(qc, s.astype(bf), preferred_element_type=f32) * xib)
        g = g_ref[sl, :].astype(f32)
        o_ref[sl, :] = (o * (g * _sigmoid(g))).astype(o_ref.dtype)
        return state_update(s, kc, zb, vc)

    fin_ref[0, 1, 0] = lax.fori_loop(0, nc, bwd, s0_ref[0, 1, 0])


def _retention(p, row0, b, t, s0, tables, rope_tabs=None):
    nc = t // RET_CHUNK
    rb = row0 // t
    rope = rope_tabs is not None
    col = lambda c0: pl.BlockSpec((t, RET_DK), lambda bi, h: (rb + bi, c0 // RET_DK + h))
    st = pl.BlockSpec((1, 2, 1, RET_DK, RET_DV), lambda bi, h: (bi, 0, h, 0, 0))
    in_specs = [col(C_RQ), col(C_RK), col(C_RV), col(C_RG)]
    args = [p, p, p, p]
    if rope:
        tab = pl.BlockSpec((t, RET_DK), lambda bi, h: (0, 0))
        in_specs += [col(C_RQP), col(C_RKP), tab, tab]
        args += [p, p, rope_tabs[0], rope_tabs[1]]
    in_specs += [pl.BlockSpec((5, 1, RET_CHUNK, RET_CHUNK), lambda bi, h: (0, h, 0, 0)), st]
    args += [tables, s0]
    return pl.pallas_call(
        functools.partial(_ret_kernel, nc=nc, rope=rope),
        grid=(b, RET_HEADS),
        in_specs=in_specs,
        out_specs=[pl.BlockSpec((t, RET_DV), lambda bi, h: (bi, h)), st],
        out_shape=[jax.ShapeDtypeStruct((b * t, RET_HEADS * RET_DV), jnp.bfloat16),
                   jax.ShapeDtypeStruct((b, 2, RET_HEADS, RET_DK, RET_DV), jnp.float32)],
        scratch_shapes=[pltpu.VMEM((t, RET_DV), jnp.float32)],
        compiler_params=_cparams(2),
        name="retention",
    )(*args)


OUT_TM = 256
ROUTER_PAD = 128


def _split_bf16(x):
    hi = x.astype(jnp.bfloat16)
    return hi, (x - hi.astype(jnp.float32)).astype(jnp.bfloat16)


def _route(h, whi_ref, wlo_ref, rb_ref):
    f32 = jnp.float32
    hi, lo = _split_bf16(h)
    dn = (((1,), (1,)), ((), ()))
    lt = (lax.dot_general(whi_ref[...], hi, dn, preferred_element_type=f32)
          + lax.dot_general(whi_ref[...], lo, dn, preferred_element_type=f32)
          + lax.dot_general(wlo_ref[...], hi, dn, preferred_element_type=f32))[:N_EXPERTS]
    m = jnp.max(lt, axis=0, keepdims=True)
    e = jnp.exp(lt - m)
    sc = e / jnp.sum(e, axis=0, keepdims=True)
    sel = sc + rb_ref[...][:N_EXPERTS, 0:1]
    rows = lambda a: [a[j:j + 1, :] for j in range(N_EXPERTS)]
    sel_r, sc_r = rows(sel), rows(sc)
    epg = EXPERTS_PER_GROUP

    def top2sum(a, b, c, d):
        h1, l1, h2, l2 = jnp.maximum(a, b), jnp.minimum(a, b), jnp.maximum(c, d), jnp.minimum(c, d)
        return jnp.maximum(h1, h2) + jnp.maximum(jnp.minimum(h1, h2), jnp.maximum(l1, l2))

    gs = [top2sum(*sel_r[g * epg:(g + 1) * epg]) for g in range(N_EXPERT_GROUPS)]
    best, gi = gs[0], jnp.zeros_like(gs[0], dtype=jnp.int32)
    for g in range(1, N_EXPERT_GROUPS):
        upd = gs[g] > best
        gi = jnp.where(upd, g, gi)
        best = jnp.where(upd, gs[g], best)

    def pick(r, j):
        out = r[j]
        for g in range(1, N_EXPERT_GROUPS):
            out = jnp.where(gi == g, r[g * epg + j], out)
        return out

    v = [pick(sel_r, j) for j in range(epg)]
    s = [pick(sc_r, j) for j in range(epg)]

    def argmax_first(vals):
        bv, bi = vals[0], jnp.zeros_like(gi)
        for j in range(1, epg):
            upd = vals[j] > bv
            bi = jnp.where(upd, j, bi)
            bv = jnp.where(upd, vals[j], bv)
        return bi

    i1 = argmax_first(v)
    neg = jnp.float32(-jnp.inf)
    i2 = argmax_first([jnp.where(i1 == j, neg, v[j]) for j in range(epg)])
    take = lambda i: sum(jnp.where(i == j, s[j], 0.0) for j in range(epg))
    w1, w2 = take(i1), take(i2)
    tot = w1 + w2
    return (jnp.concatenate([gi * epg + i1, gi * epg + i2], axis=0),
            jnp.concatenate([w1 / tot, w2 / tot], axis=0))


def _mixout_kernel(ylin_ref, u_ref, ymla_ref, ylru_ref, yret_ref, x_ref, mod_ref, d_ref, wglu_ref, wout_ref,
                   g2n_ref, whi_ref, wlo_ref, rb_ref, xo_ref, h2_ref, idx_ref, wgt_ref):
    f32, bf = jnp.float32, jnp.bfloat16
    w = GROUP_W
    y = _gelu_tanh(ylin_ref[...] + d_ref[...] * u_ref[...].astype(f32))
    y5 = (y * _sigmoid(jnp.dot(y.astype(bf), wglu_ref[...], preferred_element_type=f32))).astype(bf)
    mix = (jnp.dot(y5, wout_ref[0:w, :], preferred_element_type=f32)
           + jnp.dot(ymla_ref[...], wout_ref[w:2 * w, :], preferred_element_type=f32)
           + jnp.dot(ylru_ref[...], wout_ref[2 * w:3 * w, :], preferred_element_type=f32)
           + jnp.dot(yret_ref[...], wout_ref[3 * w:4 * w, :], preferred_element_type=f32))
    x = x_ref[...] + mod_ref[0, 2:3, :] * mix
    xo_ref[...] = x
    h2 = _rms(x) * g2n_ref[...] * (1.0 + mod_ref[0, 4:5, :]) + mod_ref[0, 3:4, :]
    h2_ref[...] = h2.astype(bf)
    idx, wgt = _route(h2, whi_ref, wlo_ref, rb_ref)
    idx_ref[...] = idx
    wgt_ref[...] = wgt


def _router_weights(router_w, router_b):
    d = router_w.shape[0]
    wt = jnp.zeros((ROUTER_PAD, d), jnp.float32).at[:N_EXPERTS].set(router_w.T.astype(jnp.float32))
    whi, wlo = _split_bf16(wt)
    rb = jnp.zeros((ROUTER_PAD, 128), jnp.float32).at[:N_EXPERTS].set(router_b.astype(jnp.float32)[:, None])
    return whi, wlo, rb


def _mixout(ylin, p, ymla, ylru, yret, x, mods, s5_d, wglu, wout, g2n, rw, n_ctx, dec_seq):
    n, d = x.shape
    tm = OUT_TM
    w = GROUP_W
    seq = functools.partial(_seq_of_tile, tm=tm, n_ctx=n_ctx, dec_seq=dec_seq)
    full = lambda a: pl.BlockSpec(a.shape, lambda i: (0,) * a.ndim)
    row = lambda width: pl.BlockSpec((tm, width), lambda i: (i, 0))
    lanes = pl.BlockSpec((TOP_K, tm), lambda i: (0, i))
    whi, wlo, rb = rw
    return pl.pallas_call(
        _mixout_kernel,
        grid=(n // tm,),
        in_specs=[row(w), row(w), row(w), row(w), row(w), row(d),
                  pl.BlockSpec((1, N_MOD, d), lambda i: (seq(i), 0, 0)),
                  full(s5_d), full(wglu), full(wout), full(g2n), full(whi), full(wlo), full(rb)],
        out_specs=[row(d), row(d), lanes, lanes],
        out_shape=[jax.ShapeDtypeStruct((n, d), jnp.float32), jax.ShapeDtypeStruct((n, d), jnp.bfloat16),
                   jax.ShapeDtypeStruct((TOP_K, n), jnp.int32), jax.ShapeDtypeStruct((TOP_K, n), jnp.float32)],
        compiler_params=_cparams(1),
        name="mix_out_norm_route",
    )(ylin, p, ymla, ylru, yret, x, mods, s5_d, wglu, wout, g2n, whi, wlo, rb)


MOE_TM = 256


def _pack_halves(x):
    w = x.shape[1] // 2
    bits = lambda a: lax.bitcast_convert_type(a.astype(jnp.bfloat16).astype(jnp.float32), jnp.int32)
    return bits(x[:, w:]) | lax.shift_right_logical(bits(x[:, :w]), 16)


def _unpack_halves(u):
    lo = lax.bitcast_convert_type(lax.shift_left(u, 16), jnp.float32)
    hi = lax.bitcast_convert_type(u & jnp.int32(-65536), jnp.float32)
    return jnp.concatenate([lo, hi], axis=1)


def _expert_kernel(te_ref, nt_ref, nv_ref, x_ref, w_ref, dst_ref, wg_ref, wu_ref, wd_ref, y_hbm,
                   wg_s, wu_s, wd_s, ybuf, sem):
    i = pl.program_id(0)
    bf = jnp.bfloat16
    nt = nt_ref[0]

    def scatter_done(n_rows):
        n8 = pl.multiple_of((n_rows // 8) * 8, 8)

        @pl.when(n8 > 0)
        def _():
            pltpu.make_async_copy(ybuf.at[pl.ds(0, n8), :], y_hbm.at[pl.ds(0, n8), :], sem).wait()

        def one(r, carry):
            pltpu.make_async_copy(ybuf.at[pl.ds(0, 1), :], y_hbm.at[pl.ds(0, 1), :], sem).wait()
            return carry

        lax.fori_loop(0, n_rows - n8, one, 0)

    @pl.when((i == 0) | (te_ref[i] != te_ref[jnp.maximum(i - 1, 0)]))
    def _():
        wg_s[...] = wg_ref[0].astype(bf)
        wu_s[...] = wu_ref[0].astype(bf)
        wd_s[...] = wd_ref[0].astype(bf)

    @pl.when(i < nt)
    def _():
        x = x_ref[...]
        g = jnp.dot(x, wg_s[...], preferred_element_type=jnp.float32)
        u = jnp.dot(x, wu_s[...], preferred_element_type=jnp.float32)
        act = ((g * _sigmoid(g)) * u * w_ref[...]).astype(bf)
        y = jnp.dot(act, wd_s[...], preferred_element_type=jnp.float32)

        @pl.when(i > 0)
        def _():
            scatter_done(nv_ref[jnp.maximum(i - 1, 0)])

        ybuf[...] = _pack_halves(y)

        def row(r, carry):
            pltpu.make_async_copy(ybuf.at[pl.ds(r, 1), :], y_hbm.at[pl.ds(dst_ref[0, 0, r], 1), :], sem).start()
            return carry

        lax.fori_loop(0, nv_ref[i], row, 0)

        @pl.when(i == nt - 1)
        def _():
            scatter_done(nv_ref[i])


def _experts(xs, ws, dst, tile_expert, n_tiles_used, n_valid, wg, wu, wd, n_out_rows):
    m, d = xs.shape
    tm = MOE_TM
    f = wg.shape[-1]
    return pl.pallas_call(
        _expert_kernel,
        grid_spec=pltpu.PrefetchScalarGridSpec(
            num_scalar_prefetch=3,
            grid=(m // tm,),
            in_specs=[pl.BlockSpec((tm, d), lambda i, te, nt, nv: (i, 0)),
                      pl.BlockSpec((tm, 1), lambda i, te, nt, nv: (i, 0)),
                      pl.BlockSpec((1, 1, tm), lambda i, te, nt, nv: (i, 0, 0), memory_space=pltpu.SMEM),
                      pl.BlockSpec((1, d, f), lambda i, te, nt, nv: (te[i], 0, 0)),
                      pl.BlockSpec((1, d, f), lambda i, te, nt, nv: (te[i], 0, 0)),
                      pl.BlockSpec((1, f, d), lambda i, te, nt, nv: (te[i], 0, 0))],
            out_specs=pl.BlockSpec(memory_space=pl.ANY),
            scratch_shapes=[pltpu.VMEM((d, f), jnp.bfloat16), pltpu.VMEM((d, f), jnp.bfloat16),
                            pltpu.VMEM((f, d), jnp.bfloat16), pltpu.VMEM((tm, d // 2), jnp.int32),
                            pltpu.SemaphoreType.DMA(())]),
        out_shape=jax.ShapeDtypeStruct((n_out_rows, d // 2), jnp.int32),
        compiler_params=pltpu.CompilerParams(dimension_semantics=("arbitrary",),
                                             vmem_limit_bytes=V7X_VMEM_LIMIT_BIG, disable_bounds_checks=True),
        name="moe_experts",
    )(tile_expert, n_tiles_used, n_valid, xs, ws, dst, wg, wu, wd)


def _moe_dispatch(idx, wgt):
    n = idx.shape[1]
    tm = MOE_TM
    n_pairs = TOP_K * n
    m_pad = n_pairs + N_EXPERTS * tm
    e_flat = idx.reshape(-1)
    order = jnp.argsort(e_flat, stable=True).astype(jnp.int32)
    e_sorted = jnp.take(e_flat, order)
    bounds = jnp.searchsorted(e_sorted, jnp.arange(N_EXPERTS + 1, dtype=jnp.int32), side='left').astype(jnp.int32)
    starts_raw, counts = bounds[:-1], bounds[1:] - bounds[:-1]
    padded = ((counts + tm - 1) // tm) * tm
    ends = jnp.cumsum(padded)
    starts_pad = ends - padded
    tile_start = jnp.arange(m_pad // tm, dtype=jnp.int32) * tm
    tile_expert = jnp.minimum(jnp.searchsorted(ends, tile_start, side='right'), N_EXPERTS - 1).astype(jnp.int32)
    r = jnp.arange(m_pad, dtype=jnp.int32)
    te_r = jnp.repeat(tile_expert, tm)
    off = r - jnp.take(starts_pad, te_r)
    valid = (off >= 0) & (off < jnp.take(counts, te_r))
    pair = jnp.take(order, jnp.clip(jnp.take(starts_raw, te_r) + off, 0, n_pairs - 1))
    tok = jnp.where(valid, pair % n, 0)
    dst = jnp.where(valid, pair, 0)
    ws = jnp.where(valid, jnp.take(wgt.reshape(-1), pair), 0.0)
    n_tiles_used = (ends[-1] // tm).astype(jnp.int32).reshape(1)
    n_valid = jnp.sum(valid.reshape(m_pad // tm, tm), axis=1).astype(jnp.int32)
    return tok, dst.reshape(m_pad // tm, 1, tm), ws[:, None], tile_expert, n_tiles_used, n_valid


def _moe_rows(h2, idx, wgt, wg, wu, wd):
    n = h2.shape[0]
    tok, dst, ws, tile_expert, n_tiles_used, n_valid = _moe_dispatch(idx, wgt)
    xs = jnp.take(h2, tok, axis=0)
    return _experts(xs, ws, dst, tile_expert, n_tiles_used, n_valid, wg, wu, wd, TOP_K * n)


RES_TM = 512


def _resid_kernel(x_ref, ya_ref, yb_ref, mod_ref, gf_ref, o_ref, *, final):
    x = x_ref[...] + mod_ref[0, 5:6, :] * (_unpack_halves(ya_ref[...]) + _unpack_halves(yb_ref[...]))
    o_ref[...] = _rms(x) * gf_ref[...] if final else x


def _resid(x, y2, mods, gf, n_ctx, dec_seq, final):
    n, d = x.shape
    tm = RES_TM
    seq = functools.partial(_seq_of_tile, tm=tm, n_ctx=n_ctx, dec_seq=dec_seq)
    row = pl.BlockSpec((tm, d), lambda i: (i, 0))
    return pl.pallas_call(
        functools.partial(_resid_kernel, final=final),
        grid=(n // tm,),
        in_specs=[row, pl.BlockSpec((tm, d // 2), lambda i: (i, 0)),
                  pl.BlockSpec((tm, d // 2), lambda i: (n // tm + i, 0)),
                  pl.BlockSpec((1, N_MOD, d), lambda i: (seq(i), 0, 0)),
                  pl.BlockSpec((1, d), lambda i: (0, 0))],
        out_specs=row,
        out_shape=jax.ShapeDtypeStruct((n, d), jnp.float32),
        compiler_params=_cparams(1),
        name="moe_residual_norm",
    )(x, y2, y2, mods, gf)


def kernel(x_prompt, x_sample, c, cache_mla_ckv, cache_mla_kpe, state_s5, state_lru, state_ret,
           c_ctx, w_ada, b_ada, norm1_g, norm2_g, w_in, w_out,
           s5_a_re, s5_a_im, s5_log_dt, s5_b_re, s5_b_im, s5_c_re, s5_c_im, s5_d, s5_w_glu,
           mla_q_norm_g, mla_w_uq, mla_kv_norm_g, mla_w_ukv,
           lru_conv_w, lru_conv_b, lru_w_a, lru_b_a, lru_w_x, lru_b_x, lru_lambda,
           router_w, router_b, moe_w_gate, moe_w_up, moe_w_down, final_norm_g):
    f32, bf = jnp.float32, jnp.bfloat16
    batch, seq_len, d = x_prompt.shape
    dec_batch, dec_seq, _ = x_sample.shape
    past = cache_mla_ckv.shape[2]
    n_ctx, n_lat = batch * seq_len, dec_batch * dec_seq
    depth = w_in.shape[0]

    x = jnp.concatenate([x_prompt.reshape(n_ctx, d), x_sample.reshape(n_lat, d)], axis=0)
    cvec = jnp.zeros((8, d), f32).at[0].set(c_ctx).at[1:1 + dec_batch].set(c)
    mods_all = _ada(cvec, w_ada, b_ada).reshape(depth, 8, N_MOD, d)

    cos64, sin64 = _rope_tables(dec_seq, MLA_ROPE)
    lat_tab = jnp.tile(jnp.concatenate([cos64, sin64], axis=1), (dec_batch, 1))
    ctx_tab = jnp.concatenate([jnp.ones((n_ctx, MLA_ROPE), f32), jnp.zeros((n_ctx, MLA_ROPE), f32)], axis=1)
    mla_tab = jnp.concatenate([ctx_tab, lat_tab], axis=0)
    ret_rope = _rope_tables(dec_seq, RET_DK)
    ret_tables = _ret_tables()
    rw = _router_weights(router_w, router_b)
    gf = final_norm_g[None, :]

    states = []
    for l in range(depth):
        mods = mods_all[l]
        s5_tab = _s5_tables(s5_a_re[l], s5_a_im[l], s5_log_dt[l], s5_b_re[l], s5_b_im[l], s5_c_re[l], s5_c_im[l])
        lru_tab = _lru_tables(lru_conv_w[l], lru_conv_b[l], lru_w_a[l], lru_b_a[l], lru_w_x[l], lru_b_x[l],
                              lru_lambda[l])
        wuq, wukv = _mla_weights(mla_w_uq[l], mla_w_ukv[l])

        p = _inproj(x, mods, norm1_g[l][None, :], _inproj_weights(w_in[l]), n_ctx, dec_seq)

        u_ctx = p[:n_ctx, C_S5:C_S5 + S5_CH].reshape(batch, seq_len, S5_CH)
        u_lat = p[n_ctx:, C_S5:C_S5 + S5_CH].reshape(dec_batch, dec_seq, S5_CH)
        ylin, s5_fins = _s5_core([u_ctx, u_lat], [None, state_s5[:, l]], s5_tab)

        qo, ko, vo, ckv = _mla_prep(p, mla_tab, mla_q_norm_g[l][None, :], mla_kv_norm_g[l][None, :], wuq, wukv)
        ckv_c = cache_mla_ckv[:, l].reshape(dec_batch * past, MLA_KV_RANK)
        kv_c = _matmul(ckv_c, wukv, out_dtype=bf).reshape(dec_batch * past, MLA_HEADS, 2, 128)
        kpe_c = jnp.broadcast_to(cache_mla_kpe[:, l].reshape(dec_batch * past, 1, MLA_ROPE).astype(bf),
                                 (dec_batch * past, MLA_HEADS, MLA_ROPE))
        k_c = jnp.concatenate([kv_c[:, :, 0], kpe_c, jnp.zeros_like(kpe_c)], axis=-1).reshape(
            dec_batch * past, MLA_HEADS * ATT_DQ)
        v_c = kv_c[:, :, 1].reshape(dec_batch * past, MLA_HEADS * MLA_V)
        ymla = jnp.concatenate([_attention(qo, ko, vo, 0, batch, seq_len),
                                _attention(qo, ko, vo, n_ctx, dec_batch, dec_seq, k_c, v_c)], axis=0)

        ylru_c, lru_fin = _lru(p, 0, batch, seq_len, jnp.zeros((batch, 2, LRU_W), f32), lru_tab)
        ylru_l, _ = _lru(p, n_ctx, dec_batch, dec_seq, state_lru[:, l].astype(f32), lru_tab)
        ylru = jnp.concatenate([ylru_c, ylru_l], axis=0)

        yret_c, ret_fin = _retention(p, 0, batch, seq_len,
                                     jnp.zeros((batch, 2, RET_HEADS, RET_DK, RET_DV), f32), ret_tables)
        yret_l, _ = _retention(p, n_ctx, dec_batch, dec_seq, state_ret[:, l].astype(f32), ret_tables, ret_rope)
        yret = jnp.concatenate([yret_c, yret_l], axis=0)

        x_mid, h2, idx, wgt = _mixout(ylin, p, ymla, ylru, yret, x, mods, s5_d[l][None, :],
                                      s5_w_glu[l].astype(bf), w_out[l].astype(bf), norm2_g[l][None, :], rw,
                                      n_ctx, dec_seq)
        y2 = _moe_rows(h2, idx, wgt, moe_w_gate[l], moe_w_up[l], moe_w_down[l])
        x = _resid(x_mid, y2, mods, gf, n_ctx, dec_seq, final=(l == depth - 1))

        states.append((ckv[:n_ctx].reshape(batch, seq_len, MLA_KV_RANK),
                       p[:n_ctx, C_KR:C_KR + MLA_ROPE].astype(f32).reshape(batch, seq_len, MLA_ROPE),
                       s5_fins[0], lru_fin, ret_fin))

    y_prompt = x[:n_ctx].reshape(batch, seq_len, d)
    y_sample = x[n_ctx:].reshape(dec_batch, dec_seq, d)
    new_cache_mla_ckv = jnp.stack([st[0] for st in states], axis=1)
    new_cache_mla_kpe = jnp.stack([st[1] for st in states], axis=1)
    new_state_s5 = jnp.stack([st[2] for st in states], axis=1)
    new_state_lru = jnp.stack([st[3] for st in states], axis=1)
    new_state_ret = jnp.stack([st[4] for st in states], axis=1)
    return (y_prompt, y_sample, new_cache_mla_ckv, new_cache_mla_kpe, new_state_s5, new_state_lru, new_state_ret)
```

```python
import functools
import numpy as np
import jax
import jax.numpy as jnp
from jax import lax
from jax.experimental import pallas as pl
from jax.experimental.pallas import tpu as pltpu

D_MODEL = 2048
DEPTH = 2
GRID_W = 64
EPS = 1e-6
ROPE_BASE = 10000.0
N_MOD = 6
GROUP_W = 512
S5_CH = GROUP_W
S5_GROUP_CH = 16
S5_GROUPS = S5_CH // S5_GROUP_CH
S5_STATE = 64
MLA_HEADS = 4
MLA_NOPE = 128
MLA_ROPE = 64
MLA_V = 128
MLA_Q_RANK = GROUP_W
MLA_KV_RANK = GROUP_W // 2
MLA_SCALE = (MLA_NOPE + MLA_ROPE) ** -0.5
LRU_W = GROUP_W
LRU_CONV = 4
LRU_C = 8.0
RET_HEADS = 4
RET_DK = 128
RET_DV = 128
RET_CHUNK = 128
N_EXPERTS = 16
N_EXPERT_GROUPS = 4
EXPERTS_PER_GROUP = N_EXPERTS // N_EXPERT_GROUPS
TOP_K = 2
D_EXPERT = D_MODEL // 4

V7X_VMEM_LIMIT = 48 * 1024 * 1024
V7X_VMEM_LIMIT_BIG = 56 * 1024 * 1024

C_S5, C_MQ, C_LX, C_LG, C_RQ, C_RK, C_RV, C_RG, C_RQP, C_RKP, C_MKV, C_KR = (
    0, 512, 1024, 1536, 2048, 2560, 3072, 3584, 4096, 4608, 5120, 5376)
P_COLS = 5632
R_S5, R_MQ, R_MKV, R_KR, R_LX, R_LG, R_RQ, R_RK, R_RV, R_RG = (0, 512, 1024, 1280, 1344, 1856, 2368, 2880, 3392, 3904)


def _cparams(n_axes, limit=V7X_VMEM_LIMIT):
    return pltpu.CompilerParams(dimension_semantics=("arbitrary",) * n_axes, vmem_limit_bytes=limit)


def _sigmoid(x):
    return 1.0 / (1.0 + jnp.exp(-x))


def _gelu_tanh(x):
    return 0.5 * x * (1.0 + jnp.tanh(0.7978845608028654 * (x + 0.044715 * (x * x * x))))


def _rms(x):
    return x * lax.rsqrt(jnp.mean(x * x, axis=-1, keepdims=True) + EPS)


def _seq_of_tile(i, tm, n_ctx, dec_seq):
    r = i * tm
    return jnp.where(r < n_ctx, 0, 1 + (r - n_ctx) // dec_seq)


def _mm_kernel(x_ref, w_ref, o_ref):
    o_ref[...] = jnp.dot(x_ref[...].astype(jnp.bfloat16), w_ref[...].astype(jnp.bfloat16),
                         preferred_element_type=jnp.float32).astype(o_ref.dtype)


def _matmul(x, w, tm=512, tn=512, out_dtype=jnp.float32):
    m, k = x.shape
    _, n = w.shape
    tm, tn = min(tm, m), min(tn, n)
    return pl.pallas_call(
        _mm_kernel,
        grid=(n // tn, m // tm),
        in_specs=[pl.BlockSpec((tm, k), lambda j, i: (i, 0)), pl.BlockSpec((k, tn), lambda j, i: (0, j))],
        out_specs=pl.BlockSpec((tm, tn), lambda j, i: (i, j)),
        out_shape=jax.ShapeDtypeStruct((m, n), out_dtype),
        compiler_params=_cparams(2),
        name="matmul",
    )(x, w)


ADA_TN = 1024


def _ada_kernel(c_ref, w_ref, b_ref, o_ref):
    c = c_ref[...]
    s = (c * _sigmoid(c)).astype(jnp.bfloat16)
    o_ref[0] = jnp.dot(s, w_ref[0].astype(jnp.bfloat16), preferred_element_type=jnp.float32) + b_ref[0]


def _ada(cvec, w_ada, b_ada):
    depth, d, n = w_ada.shape
    return pl.pallas_call(
        _ada_kernel,
        grid=(depth, n // ADA_TN),
        in_specs=[pl.BlockSpec((8, d), lambda l, j: (0, 0)),
                  pl.BlockSpec((1, d, ADA_TN), lambda l, j: (l, 0, j)),
                  pl.BlockSpec((1, 1, ADA_TN), lambda l, j: (l, 0, j))],
        out_specs=pl.BlockSpec((1, 8, ADA_TN), lambda l, j: (l, 0, j)),
        out_shape=jax.ShapeDtypeStruct((depth, 8, n), jnp.float32),
        compiler_params=_cparams(2),
        name="adaln_mod",
    )(cvec, w_ada, b_ada.reshape(depth, 1, n))


IN_TM = 512
IN_TN = 512


def _inproj_kernel(x_ref, mod_ref, g_ref, w_ref, o_ref, h_s):
    @pl.when(pl.program_id(1) == 0)
    def _():
        h = _rms(x_ref[...]) * g_ref[...]
        h_s[...] = (h * (1.0 + mod_ref[0, 1:2, :]) + mod_ref[0, 0:1, :]).astype(h_s.dtype)

    o_ref[...] = jnp.dot(h_s[...], w_ref[...], preferred_element_type=jnp.float32).astype(o_ref.dtype)


def _inproj(x, mods, g, w, n_ctx, dec_seq):
    n, d = x.shape
    seq = functools.partial(_seq_of_tile, tm=IN_TM, n_ctx=n_ctx, dec_seq=dec_seq)
    return pl.pallas_call(
        _inproj_kernel,
        grid=(n // IN_TM, P_COLS // IN_TN),
        in_specs=[pl.BlockSpec((IN_TM, d), lambda i, j: (i, 0)),
                  pl.BlockSpec((1, N_MOD, d), lambda i, j: (seq(i), 0, 0)),
                  pl.BlockSpec((1, d), lambda i, j: (0, 0)),
                  pl.BlockSpec((d, IN_TN), lambda i, j: (0, j))],
        out_specs=pl.BlockSpec((IN_TM, IN_TN), lambda i, j: (i, j)),
        out_shape=jax.ShapeDtypeStruct((n, P_COLS), jnp.bfloat16),
        scratch_shapes=[pltpu.VMEM((IN_TM, d), jnp.bfloat16)],
        compiler_params=_cparams(2),
        name="norm_inproj",
    )(x, mods, g, w)


def _rot_partner_cols(w, n_heads, head_dim):
    q = head_dim // 4
    wr = w.reshape(w.shape[0], n_heads, 2, 2, q)
    return jnp.stack([-wr[:, :, :, 1], wr[:, :, :, 0]], axis=3).reshape(w.shape)


def _inproj_weights(w_in):
    part = lambda off, width: w_in[:, off:off + width]
    rq, rk, kr = part(R_RQ, 512), part(R_RK, 512), part(R_KR, MLA_ROPE)
    cols = [part(R_S5, 512), part(R_MQ, 512), part(R_LX, 512), part(R_LG, 512), rq, rk, part(R_RV, 512),
            part(R_RG, 512), _rot_partner_cols(rq, RET_HEADS, RET_DK), _rot_partner_cols(rk, RET_HEADS, RET_DK),
            part(R_MKV, MLA_KV_RANK), kr, _rot_partner_cols(kr, 1, MLA_ROPE)]
    w = jnp.concatenate(cols, axis=1)
    return jnp.pad(w, ((0, 0), (0, P_COLS - w.shape[1]))).astype(jnp.bfloat16)


def _rope_tables(t_len, rot_dim):
    rows = t_len // GRID_W
    row = jnp.repeat(jnp.arange(rows, dtype=jnp.float32), GRID_W)
    col = jnp.tile(jnp.arange(GRID_W, dtype=jnp.float32), rows)
    n_freq = rot_dim // 4
    inv = ROPE_BASE ** (-jnp.arange(n_freq, dtype=jnp.float32) / n_freq)
    ang = jnp.concatenate([row[:, None] * inv[None]] * 2 + [col[:, None] * inv[None]] * 2, axis=1)
    return jnp.cos(ang), jnp.sin(ang)


S5_L = 32
S5_PAIRS = S5_GROUPS // 2
S5_Q = 4
S5_LANES = S5_GROUPS * S5_STATE
S5_SCAN_LANES = 1024


def _s5_tables(a_re, a_im, log_dt, b_re, b_im, c_re, c_im):
    f32 = jnp.float32
    L, G, P, C = S5_L, S5_GROUPS, S5_STATE, S5_GROUP_CH
    hp = lax.Precision.HIGHEST
    cmul = lambda xr, xi, yr, yi: (xr * yr - xi * yi, xr * yi + xi * yr)
    dt = jnp.exp(log_dt.astype(f32))[..., None]
    zr, zi = a_re * dt, a_im * dt
    ab_r, ab_i = jnp.exp(zr) * jnp.cos(zi), jnp.exp(zr) * jnp.sin(zi)
    den = a_re * a_re + a_im * a_im
    nr, ni = ab_r - 1.0, ab_i
    be_r = (nr * a_re + ni * a_im) / den
    be_i = (ni * a_re - nr * a_im) / den
    bt_r, bt_i = b_re.transpose(0, 2, 1), b_im.transpose(0, 2, 1)
    bp_r, bp_i = cmul(be_r[:, :, None, :], be_i[:, :, None, :], bt_r[None], bt_i[None])
    tau = jnp.arange(L + 1, dtype=f32)[None, None, :, None]
    mag = jnp.exp(zr[:, :, None, :] * tau)
    pw_r, pw_i = mag * jnp.cos(zi[:, :, None, :] * tau), mag * jnp.sin(zi[:, :, None, :] * tau)

    pws = lambda d, rev: tuple(x[d][:, :L][:, ::-1] if rev else x[d][:, :L] for x in (pw_r, pw_i))
    coef = lambda d, rev: cmul(*(x[:, :, None, :] for x in pws(d, rev)), bp_r[d][:, None], bp_i[d][:, None])
    ws = jnp.stack(coef(0, True) + coef(1, False), axis=3).reshape(G, L * C, S5_Q * P)

    ct_r, ct_i = c_re.transpose(0, 2, 1), c_im.transpose(0, 2, 1)
    pt_r, pt_i = pw_r.transpose(0, 1, 3, 2), pw_i.transpose(0, 1, 3, 2)

    def cm(er, ei):
        m_r, m_i = cmul(ct_r[:, :, None, :], ct_i[:, :, None, :], er[..., None], ei[..., None])
        return m_r.reshape(G, P, L * C), m_i.reshape(G, P, L * C)

    mf_r, mf_i = cm(pt_r[0][:, :, 1:L + 1], pt_i[0][:, :, 1:L + 1])
    mb_r, mb_i = cm(pt_r[1][:, :, 1:L + 1][:, :, ::-1], pt_i[1][:, :, 1:L + 1][:, :, ::-1])
    wo = jnp.stack([mf_r, -mf_i, mb_r, -mb_i], axis=1).reshape(G, S5_Q * P, L * C)

    def impulse(d):
        m_r, m_i = cm(pt_r[d][:, :, :L], pt_i[d][:, :, :L])
        kk = (jnp.einsum('gkp,gpx->gkx', bp_r[d], m_r, precision=hp)
              - jnp.einsum('gkp,gpx->gkx', bp_i[d], m_i, precision=hp))
        return kk.reshape(G, C, L, C)

    kf, kb = impulse(0), impulse(1)
    wide = jnp.concatenate([kb[:, :, :0:-1], kf[:, :, :1] + kb[:, :, :1], kf[:, :, 1:]], axis=2)
    wide = wide.reshape(G, C, (2 * L - 1) * C)
    toe = jnp.stack([wide[:, :, (L - 1 - s) * C:(L - 1 - s) * C + L * C] for s in range(L)], axis=1)
    toe = toe.reshape(G, L * C, L * C)

    a_l = jnp.stack([pw_r[0, :, L], pw_i[0, :, L], pw_r[1, :, L], pw_i[1, :, L]], axis=0).reshape(S5_Q, 1, S5_LANES)
    bf = jnp.bfloat16
    return toe.astype(bf), ws.astype(bf), wo.astype(bf), a_l


def _s5a_kernel(x_ref, t_ref, ws_ref, y1_ref, s_ref):
    ss = []
    for j in range(2):
        x = x_ref[j]
        y1_ref[j] = jnp.dot(x, t_ref[j], preferred_element_type=jnp.float32)
        ss.append(jnp.dot(x, ws_ref[j], preferred_element_type=jnp.float32))
    p = S5_STATE
    for q in range(S5_Q):
        s_ref[q] = jnp.concatenate([ss[0][:, q * p:(q + 1) * p], ss[1][:, q * p:(q + 1) * p]], axis=1)


def _s5a(x, toe, ws):
    g, r, w = x.shape
    return pl.pallas_call(
        _s5a_kernel,
        grid=(S5_PAIRS,),
        in_specs=[pl.BlockSpec((2, r, w), lambda i: (i, 0, 0)),
                  pl.BlockSpec((2, w, w), lambda i: (i, 0, 0)),
                  pl.BlockSpec((2, w, S5_Q * S5_STATE), lambda i: (i, 0, 0))],
        out_specs=[pl.BlockSpec((2, r, w), lambda i: (i, 0, 0)),
                   pl.BlockSpec((S5_Q, r, 128), lambda i: (0, 0, i))],
        out_shape=[jax.ShapeDtypeStruct((g, r, w), jnp.float32),
                   jax.ShapeDtypeStruct((S5_Q, r, S5_LANES), jnp.float32)],
        compiler_params=_cparams(1),
        name="s5_chunk_local",
    )(x, toe, ws)


def _s5b_kernel(s_ref, a_ref, h0_ref, hp_ref, fin_ref, *, nc):
    def run(qr, qi, order_fwd):
        ar, ai = a_ref[qr], a_ref[qi]

        def body(i, carry):
            hr, hi = carry
            k = i if order_fwd else nc - 1 - i
            hp_ref[qr, pl.ds(k, 1), :] = hr
            hp_ref[qi, pl.ds(k, 1), :] = hi
            nr = ar * hr - ai * hi + s_ref[qr, pl.ds(k, 1), :]
            ni = ar * hi + ai * hr + s_ref[qi, pl.ds(k, 1), :]
            return nr, ni

        hr, hi = lax.fori_loop(0, nc, body, (h0_ref[qr], h0_ref[qi]))
        fin_ref[qr] = hr
        fin_ref[qi] = hi

    run(0, 1, True)
    run(2, 3, False)


def _s5b(s, a_l, h0):
    _, nc, w = s.shape
    tl = S5_SCAN_LANES
    vec = pl.BlockSpec((S5_Q, 1, tl), lambda i: (0, 0, i))
    seq = pl.BlockSpec((S5_Q, nc, tl), lambda i: (0, 0, i))
    return pl.pallas_call(
        functools.partial(_s5b_kernel, nc=nc),
        grid=(w // tl,),
        in_specs=[seq, vec, vec],
        out_specs=[seq, vec],
        out_shape=[jax.ShapeDtypeStruct((S5_Q, nc, w), jnp.float32),
                   jax.ShapeDtypeStruct((S5_Q, 1, w), jnp.float32)],
        compiler_params=_cparams(1),
        name="s5_chunk_scan",
    )(s, a_l, h0)


def _s5c_kernel(h_ref, wo_ref, y1_ref, y_ref):
    p = S5_STATE
    for j in range(2):
        hcat = jnp.concatenate([h_ref[q][:, j * p:(j + 1) * p] for q in range(S5_Q)], axis=1)
        y2 = jnp.dot(hcat.astype(jnp.bfloat16), wo_ref[j], preferred_element_type=jnp.float32)
        y_ref[j] = (y1_ref[j] + y2).astype(y_ref.dtype)


def _s5c(hprev, wo, y1):
    g, r, w = y1.shape
    return pl.pallas_call(
        _s5c_kernel,
        grid=(S5_PAIRS,),
        in_specs=[pl.BlockSpec((S5_Q, r, 128), lambda i: (0, 0, i)),
                  pl.BlockSpec((2, S5_Q * S5_STATE, w), lambda i: (i, 0, 0)),
                  pl.BlockSpec((2, r, w), lambda i: (i, 0, 0))],
        out_specs=pl.BlockSpec((2, r, w), lambda i: (i, 0, 0)),
        out_shape=jax.ShapeDtypeStruct((g, r, w), jnp.float32),
        compiler_params=_cparams(1),
        name="s5_state_to_out",
    )(hprev, wo, y1)


def _s5_to_chunks(u):
    b, t, _ = u.shape
    nc = t // S5_L
    x = u.reshape(b, nc, S5_L, S5_GROUPS, S5_GROUP_CH).transpose(3, 1, 0, 2, 4)
    return x.reshape(S5_GROUPS, nc * b, S5_L * S5_GROUP_CH)


def _s5_from_chunks(y, b, t):
    nc = t // S5_L
    y = y.reshape(S5_GROUPS, nc, b, S5_L, S5_GROUP_CH).transpose(2, 1, 3, 0, 4)
    return y.reshape(b * t, S5_CH)


def _s5_state_planes(h0):
    b = h0.shape[0]
    return h0.transpose(1, 4, 0, 2, 3).reshape(S5_Q, b, S5_LANES)


def _s5_core(us, h0s, tables):
    toe, ws_pair, wo_pair, a_l = tables
    xs = [_s5_to_chunks(u) for u in us]
    rows = [x.shape[1] for x in xs]
    y1, s = _s5a(jnp.concatenate(xs, axis=1), toe, ws_pair)
    hps, fins = [], []
    off = 0
    for u, h0, r in zip(us, h0s, rows):
        b, t, _ = u.shape
        nc = t // S5_L
        if h0 is None:
            h0p = jnp.zeros((S5_Q, 1, b * S5_LANES), jnp.float32)
        else:
            h0p = _s5_state_planes(h0.astype(jnp.float32)).reshape(S5_Q, 1, b * S5_LANES)
        hp, fin = _s5b(s[:, off:off + r].reshape(S5_Q, nc, b * S5_LANES), jnp.tile(a_l, (1, 1, b)), h0p)
        hps.append(hp.reshape(S5_Q, r, S5_LANES))
        fins.append(fin.reshape(2, 2, b, S5_GROUPS, S5_STATE).transpose(2, 0, 3, 4, 1))
        off += r
    y = _s5c(jnp.concatenate(hps, axis=1), wo_pair, y1)
    outs, off = [], 0
    for u, r in zip(us, rows):
        b, t, _ = u.shape
        outs.append(_s5_from_chunks(y[:, off:off + r], b, t))
        off += r
    return jnp.concatenate(outs, axis=0), fins


LRU_TC = 256
LRU_HALO = 16


def _lru_kernel(x_ref, gate_ref, cw_ref, cb_ref, wg_ref, bg_ref, sp_ref, h0_ref, out_ref, fin_ref,
                a_s, b_s, hf_s, *, t_len, tc):
    f32 = jnp.float32
    nt = t_len // tc
    w = LRU_W

    def gates(c, d):
        r0 = pl.multiple_of(c * tc, tc)
        lo = pl.multiple_of(jnp.maximum(r0 - LRU_HALO, 0), LRU_HALO)
        hi = pl.multiple_of(jnp.minimum(r0 + tc, t_len - LRU_HALO), LRU_HALO)
        prev = jnp.where(c > 0, x_ref[pl.ds(lo, LRU_HALO), :].astype(f32), 0.0)
        nxt = jnp.where(c < nt - 1, x_ref[pl.ds(hi, LRU_HALO), :].astype(f32), 0.0)
        slab = jnp.concatenate([prev, x_ref[pl.ds(r0, tc), :].astype(f32), nxt], axis=0)
        o = LRU_HALO - LRU_CONV // 2
        xc = cb_ref[...] + sum(cw_ref[k:k + 1, :] * slab[o + k:o + k + tc] for k in range(LRU_CONV))
        g = jnp.dot(xc.astype(jnp.bfloat16), wg_ref[d], preferred_element_type=f32) + bg_ref[d]
        r = _sigmoid(g[:, :w])
        i = _sigmoid(g[:, w:])
        log_a = -sp_ref[d] * r
        a_s[...] = jnp.exp(log_a)
        b_s[...] = jnp.sqrt(1.0 - jnp.exp(2.0 * log_a)) * (i * xc)
        return r0

    def fwd_chunk(c, h):
        r0 = gates(c, 0)

        def rows(i, h):
            for j in range(8):
                t = i * 8 + j
                h = a_s[pl.ds(t, 1), :] * h + b_s[pl.ds(t, 1), :]
                hf_s[pl.ds(r0 + t, 1), :] = h
            return h

        return lax.fori_loop(0, tc // 8, rows, h)

    h = lax.fori_loop(0, nt, fwd_chunk, h0_ref[0, 0:1, :])
    fin_ref[0, 0:1, :] = h

    def bwd_chunk(ci, h):
        r0 = gates(nt - 1 - ci, 1)

        def rows(i, h):
            for j in range(8):
                t = tc - 1 - (i * 8 + j)
                h = a_s[pl.ds(t, 1), :] * h + b_s[pl.ds(t, 1), :]
                hf_s[pl.ds(r0 + t, 1), :] = hf_s[pl.ds(r0 + t, 1), :] + h
            return h

        h = lax.fori_loop(0, tc // 8, rows, h)
        sl = pl.ds(r0, tc)
        out_ref[sl, :] = (hf_s[sl, :] * _gelu_tanh(gate_ref[sl, :].astype(f32))).astype(out_ref.dtype)
        return h

    h = lax.fori_loop(0, nt, bwd_chunk, h0_ref[0, 1:2, :])
    fin_ref[0, 1:2, :] = h


def _block_diag(wb):
    n, k, j = wb.shape
    return (wb[:, :, None, :] * jnp.eye(n, dtype=wb.dtype)[:, None, :, None]).reshape(n * k, n * j)


def _lru_tables(conv_w, conv_b, w_a, b_a, w_x, b_x, lam):
    wg = jnp.stack([jnp.concatenate([_block_diag(w_a[d]), _block_diag(w_x[d])], axis=1) for d in range(2)])
    bg = jnp.concatenate([b_a, b_x], axis=-1)[:, None, :]
    sp = (LRU_C * jax.nn.softplus(-lam.astype(jnp.float32)))[:, None, :]
    return conv_w, conv_b[None, :], wg.astype(jnp.bfloat16), bg, sp


def _lru(p, row0, b, t, h0, tables):
    w = LRU_W
    tc = min(LRU_TC, t)
    cw, cb, wg, bg, sp = tables
    full = lambda a: pl.BlockSpec(a.shape, lambda i: (0,) * a.ndim)
    rb = row0 // t
    return pl.pallas_call(
        functools.partial(_lru_kernel, t_len=t, tc=tc),
        grid=(b,),
        in_specs=[pl.BlockSpec((t, w), lambda i: (rb + i, C_LX // w)),
                  pl.BlockSpec((t, w), lambda i: (rb + i, C_LG // w)),
                  full(cw), full(cb), full(wg), full(bg), full(sp),
                  pl.BlockSpec((1, 2, w), lambda i: (i, 0, 0))],
        out_specs=[pl.BlockSpec((t, w), lambda i: (i, 0)),
                   pl.BlockSpec((1, 2, w), lambda i: (i, 0, 0))],
        out_shape=[jax.ShapeDtypeStruct((b * t, w), jnp.bfloat16),
                   jax.ShapeDtypeStruct((b, 2, w), jnp.float32)],
        scratch_shapes=[pltpu.VMEM((tc, w), jnp.float32), pltpu.VMEM((tc, w), jnp.float32),
                        pltpu.VMEM((t, w), jnp.float32)],
        compiler_params=_cparams(1),
        name="rglru",
    )(p, p, cw, cb, wg, bg, sp, h0)


MLA_TM = 512
ATT_DQ = 256


def _mla_prep_kernel(q_ref, kv_ref, kr_ref, tab_ref, gq_ref, gkv_ref, wuq_ref, wukv_ref,
                     qo_ref, ko_ref, vo_ref, ckv_ref):
    f32, bf = jnp.float32, jnp.bfloat16
    tab = tab_ref[...]

    def rope(blk):
        prod = blk * tab
        return prod + pltpu.roll(prod, 64, 1)

    qn = (_rms(q_ref[...].astype(f32)) * gq_ref[...]).astype(bf)
    qq = jnp.dot(qn, wuq_ref[...], preferred_element_type=f32)
    ckv = _rms(kv_ref[...].astype(f32)) * gkv_ref[...]
    ckv_ref[...] = ckv
    kk = jnp.dot(ckv.astype(bf), wukv_ref[...], preferred_element_type=f32)
    lane = lax.broadcasted_iota(jnp.int32, tab.shape, 1)
    kpe = jnp.where(lane < MLA_ROPE, rope(kr_ref[...].astype(f32)), 0.0).astype(bf)
    for h in range(MLA_HEADS):
        o = h * ATT_DQ
        qo_ref[:, o:o + 128] = (qq[:, o:o + 128] * MLA_SCALE).astype(bf)
        qo_ref[:, o + 128:o + 256] = (rope(qq[:, o + 128:o + 256]) * MLA_SCALE).astype(bf)
        ko_ref[:, o:o + 128] = kk[:, o:o + 128].astype(bf)
        ko_ref[:, o + 128:o + 256] = kpe
        vo_ref[:, h * MLA_V:(h + 1) * MLA_V] = kk[:, o + 128:o + 256].astype(bf)


def _mla_weights(w_uq, w_ukv):
    wq = w_uq.reshape(MLA_Q_RANK, MLA_HEADS, MLA_NOPE + MLA_ROPE)
    pe = wq[:, :, MLA_NOPE:].reshape(MLA_Q_RANK, MLA_HEADS * MLA_ROPE)
    pep = _rot_partner_cols(pe, MLA_HEADS, MLA_ROPE).reshape(MLA_Q_RANK, MLA_HEADS, MLA_ROPE)
    wq_ext = jnp.concatenate([wq, pep], axis=-1).reshape(MLA_Q_RANK, MLA_HEADS * ATT_DQ)
    return wq_ext.astype(jnp.bfloat16), w_ukv.astype(jnp.bfloat16)


def _mla_prep(p, tab, gq, gkv, wuq, wukv):
    n = p.shape[0]
    tm = MLA_TM
    full = lambda a: pl.BlockSpec(a.shape, lambda i: (0,) * a.ndim)
    row = lambda width, col: pl.BlockSpec((tm, width), lambda i: (i, col // width))
    return pl.pallas_call(
        _mla_prep_kernel,
        grid=(n // tm,),
        in_specs=[row(MLA_Q_RANK, C_MQ), row(MLA_KV_RANK, C_MKV), row(128, C_KR), row(128, 0),
                  full(gq), full(gkv), full(wuq), full(wukv)],
        out_specs=[row(MLA_HEADS * ATT_DQ, 0), row(MLA_HEADS * ATT_DQ, 0), row(MLA_HEADS * MLA_V, 0),
                   row(MLA_KV_RANK, 0)],
        out_shape=[jax.ShapeDtypeStruct((n, MLA_HEADS * ATT_DQ), jnp.bfloat16),
                   jax.ShapeDtypeStruct((n, MLA_HEADS * ATT_DQ), jnp.bfloat16),
                   jax.ShapeDtypeStruct((n, MLA_HEADS * MLA_V), jnp.bfloat16),
                   jax.ShapeDtypeStruct((n, MLA_KV_RANK), jnp.float32)],
        compiler_params=_cparams(1),
        name="mla_prep",
    )(p, p, p, tab, gq, gkv, wuq, wukv)


ATT_TQ = 256


def _attn_kernel(*refs, two):
    dn = (((1,), (1,)), ((), ()))
    f32 = jnp.float32
    if two:
        q_ref, k_ref, v_ref, k2_ref, v2_ref, o_ref = refs
    else:
        q_ref, k_ref, v_ref, o_ref = refs
    q = q_ref[...]
    s = lax.dot_general(q, k_ref[...], dn, preferred_element_type=f32)
    m = jnp.max(s, axis=-1, keepdims=True)
    if two:
        s2 = lax.dot_general(q, k2_ref[...], dn, preferred_element_type=f32)
        m = jnp.maximum(m, jnp.max(s2, axis=-1, keepdims=True))
    p = jnp.exp(s - m)
    l = jnp.sum(p, axis=-1, keepdims=True)
    o = jnp.dot(p.astype(jnp.bfloat16), v_ref[...], preferred_element_type=f32)
    if two:
        p2 = jnp.exp(s2 - m)
        l = l + jnp.sum(p2, axis=-1, keepdims=True)
        o = o + jnp.dot(p2.astype(jnp.bfloat16), v2_ref[...], preferred_element_type=f32)
    o_ref[...] = (o / l).astype(o_ref.dtype)


def _attention(q, k, v, row0, b, t, k2=None, v2=None):
    tb = min(ATT_TQ, t)
    nq = t // tb
    qb, kb = row0 // tb, row0 // t
    two = k2 is not None
    in_specs = [pl.BlockSpec((tb, ATT_DQ), lambda bi, h, i: (qb + bi * nq + i, h)),
                pl.BlockSpec((t, ATT_DQ), lambda bi, h, i: (kb + bi, h)),
                pl.BlockSpec((t, MLA_V), lambda bi, h, i: (kb + bi, h))]
    args = [q, k, v]
    if two:
        t2 = k2.shape[0] // b
        in_specs += [pl.BlockSpec((t2, ATT_DQ), lambda bi, h, i: (bi, h)),
                     pl.BlockSpec((t2, MLA_V), lambda bi, h, i: (bi, h))]
        args += [k2, v2]
    return pl.pallas_call(
        functools.partial(_attn_kernel, two=two),
        grid=(b, MLA_HEADS, nq),
        in_specs=in_specs,
        out_specs=pl.BlockSpec((tb, MLA_V), lambda bi, h, i: (bi * nq + i, h)),
        out_shape=jax.ShapeDtypeStruct((b * t, MLA_HEADS * MLA_V), jnp.bfloat16),
        compiler_params=_cparams(3),
        name="mla_attention",
    )(*args)


def _ret_tables():
    f32 = jnp.float32
    c = RET_CHUNK
    log_g = jnp.log1p(-jnp.exp2(-5.0 - jnp.arange(RET_HEADS, dtype=f32)))[:, None, None]
    idx = jnp.arange(c, dtype=f32)
    dec = jnp.exp(jnp.abs(idx[:, None] - idx[None, :])[None] * log_g)
    row = lambda e: jnp.broadcast_to(jnp.exp(e[None, :, None] * log_g), (RET_HEADS, c, c))
    return jnp.stack([dec, row(idx + 1.0), row(c - idx), row(c - 1.0 - idx), row(idx)], axis=0)


def _ret_kernel(*refs, nc, rope):
    f32, bf = jnp.float32, jnp.bfloat16
    if rope:
        q_ref, k_ref, v_ref, g_ref, qp_ref, kp_ref, cos_ref, sin_ref, tab_ref, s0_ref, o_ref, fin_ref, acc_s = refs
    else:
        q_ref, k_ref, v_ref, g_ref, tab_ref, s0_ref, o_ref, fin_ref, acc_s = refs
    c = RET_CHUNK
    dec, xif, xib, zf, zb = (tab_ref[i, 0] for i in range(5))
    g_chunk = xif[c - 1:c, :]

    def chunk(k0):
        sl = pl.ds(pl.multiple_of(k0 * c, c), c)
        qc, kc = q_ref[sl, :].astype(f32), k_ref[sl, :].astype(f32)
        if rope:
            cos, sin = cos_ref[sl, :], sin_ref[sl, :]
            qc = qc * cos + qp_ref[sl, :].astype(f32) * sin
            kc = kc * cos + kp_ref[sl, :].astype(f32) * sin
        return sl, qc.astype(bf), kc * (RET_DK ** -0.5), v_ref[sl, :]

    def state_update(s, kc, z, vc):
        kz = (kc * z).astype(bf)
        return g_chunk * s + lax.dot_general(kz, vc, (((0,), (0,)), ((), ())), preferred_element_type=f32)

    def fwd(i, s):
        sl, qc, kc, vc = chunk(i)
        sc = lax.dot_general(qc, kc.astype(bf), (((1,), (1,)), ((), ())), preferred_element_type=f32) * dec
        inner = jnp.dot(sc.astype(bf), vc, preferred_element_type=f32)
        cross = jnp.dot(qc, s.astype(bf), preferred_element_type=f32) * xif
        acc_s[sl, :] = inner + cross
        return state_update(s, kc, zf, vc)

    fin_ref[0, 0, 0] = lax.fori_loop(0, nc, fwd, s0_ref[0, 0, 0])

    def bwd(i, s):
        sl, qc, kc, vc = chunk(nc - 1 - i)
        o = _rms(acc_s[sl, :] + jnp.dot(qc, s.astype(bf), preferred_element_type=f32) * xib)
        g = g_ref[sl, :].astype(f32)
        o_ref[sl, :] = (o * (g * _sigmoid(g))).astype(o_ref.dtype)
        return state_update(s, kc, zb, vc)

    fin_ref[0, 1, 0] = lax.fori_loop(0, nc, bwd, s0_ref[0, 1, 0])


def _retention(p, row0, b, t, s0, tables, rope_tabs=None):
    nc = t // RET_CHUNK
    rb = row0 // t
    rope = rope_tabs is not None
    col = lambda c0: pl.BlockSpec((t, RET_DK), lambda bi, h: (rb + bi, c0 // RET_DK + h))
    st = pl.BlockSpec((1, 2, 1, RET_DK, RET_DV), lambda bi, h: (bi, 0, h, 0, 0))
    in_specs = [col(C_RQ), col(C_RK), col(C_RV), col(C_RG)]
    args = [p, p, p, p]
    if rope:
        tab = pl.BlockSpec((t, RET_DK), lambda bi, h: (0, 0))
        in_specs += [col(C_RQP), col(C_RKP), tab, tab]
        args += [p, p, rope_tabs[0], rope_tabs[1]]
    in_specs += [pl.BlockSpec((5, 1, RET_CHUNK, RET_CHUNK), lambda bi, h: (0, h, 0, 0)), st]
    args += [tables, s0]
    return pl.pallas_call(
        functools.partial(_ret_kernel, nc=nc, rope=rope),
        grid=(b, RET_HEADS),
        in_specs=in_specs,
        out_specs=[pl.BlockSpec((t, RET_DV), lambda bi, h: (bi, h)), st],
        out_shape=[jax.ShapeDtypeStruct((b * t, RET_HEADS * RET_DV), jnp.bfloat16),
                   jax.ShapeDtypeStruct((b, 2, RET_HEADS, RET_DK, RET_DV), jnp.float32)],
        scratch_shapes=[pltpu.VMEM((t, RET_DV), jnp.float32)],
        compiler_params=_cparams(2),
        name="retention",
    )(*args)


OUT_TM = 256
ROUTER_PAD = 128


def _pack_halves(x):
    w = x.shape[1] // 2
    bits = lambda a: lax.bitcast_convert_type(a.astype(jnp.bfloat16).astype(jnp.float32), jnp.int32)
    return bits(x[:, w:]) | lax.shift_right_logical(bits(x[:, :w]), 16)


def _unpack_halves(u):
    lo = lax.bitcast_convert_type(lax.shift_left(u, 16), jnp.float32)
    hi = lax.bitcast_convert_type(u & jnp.int32(-65536), jnp.float32)
    return jnp.concatenate([lo, hi], axis=1)


def _split_bf16(x):
    hi = x.astype(jnp.bfloat16)
    return hi, (x - hi.astype(jnp.float32)).astype(jnp.bfloat16)


def _route(h, whi_ref, wlo_ref, rb_ref):
    f32 = jnp.float32
    hi, lo = _split_bf16(h)
    dn = (((1,), (1,)), ((), ()))
    lt = (lax.dot_general(whi_ref[...], hi, dn, preferred_element_type=f32)
          + lax.dot_general(whi_ref[...], lo, dn, preferred_element_type=f32)
          + lax.dot_general(wlo_ref[...], hi, dn, preferred_element_type=f32))[:N_EXPERTS]
    m = jnp.max(lt, axis=0, keepdims=True)
    e = jnp.exp(lt - m)
    sc = e / jnp.sum(e, axis=0, keepdims=True)
    sel = sc + rb_ref[...][:N_EXPERTS, 0:1]
    rows = lambda a: [a[j:j + 1, :] for j in range(N_EXPERTS)]
    sel_r, sc_r = rows(sel), rows(sc)
    epg = EXPERTS_PER_GROUP

    def top2sum(a, b, c, d):
        h1, l1, h2, l2 = jnp.maximum(a, b), jnp.minimum(a, b), jnp.maximum(c, d), jnp.minimum(c, d)
        return jnp.maximum(h1, h2) + jnp.maximum(jnp.minimum(h1, h2), jnp.maximum(l1, l2))

    gs = [top2sum(*sel_r[g * epg:(g + 1) * epg]) for g in range(N_EXPERT_GROUPS)]
    best, gi = gs[0], jnp.zeros_like(gs[0], dtype=jnp.int32)
    for g in range(1, N_EXPERT_GROUPS):
        upd = gs[g] > best
        gi = jnp.where(upd, g, gi)
        best = jnp.where(upd, gs[g], best)

    def pick(r, j):
        out = r[j]
        for g in range(1, N_EXPERT_GROUPS):
            out = jnp.where(gi == g, r[g * epg + j], out)
        return out

    v = [pick(sel_r, j) for j in range(epg)]
    s = [pick(sc_r, j) for j in range(epg)]

    def argmax_first(vals):
        bv, bi = vals[0], jnp.zeros_like(gi)
        for j in range(1, epg):
            upd = vals[j] > bv
            bi = jnp.where(upd, j, bi)
            bv = jnp.where(upd, vals[j], bv)
        return bi

    i1 = argmax_first(v)
    neg = jnp.float32(-jnp.inf)
    i2 = argmax_first([jnp.where(i1 == j, neg, v[j]) for j in range(epg)])
    take = lambda i: sum(jnp.where(i == j, s[j], 0.0) for j in range(epg))
    w1, w2 = take(i1), take(i2)
    tot = w1 + w2
    return (jnp.concatenate([gi * epg + i1, gi * epg + i2], axis=0),
            jnp.concatenate([w1 / tot, w2 / tot], axis=0))


def _mixout_kernel(ylin_ref, u_ref, ymla_ref, ylru_ref, yret_ref, x_ref, mod_ref, d_ref, wglu_ref, wout_ref,
                   g2n_ref, whi_ref, wlo_ref, rb_ref, xo_ref, h2_ref, idx_ref, wgt_ref):
    f32, bf = jnp.float32, jnp.bfloat16
    w = GROUP_W
    y = _gelu_tanh(ylin_ref[...] + d_ref[...] * u_ref[...].astype(f32))
    y5 = (y * _sigmoid(jnp.dot(y.astype(bf), wglu_ref[...], preferred_element_type=f32))).astype(bf)
    mix = (jnp.dot(y5, wout_ref[0:w, :], preferred_element_type=f32)
           + jnp.dot(ymla_ref[...], wout_ref[w:2 * w, :], preferred_element_type=f32)
           + jnp.dot(ylru_ref[...], wout_ref[2 * w:3 * w, :], preferred_element_type=f32)
           + jnp.dot(yret_ref[...], wout_ref[3 * w:4 * w, :], preferred_element_type=f32))
    x = x_ref[...] + mod_ref[0, 2:3, :] * mix
    xo_ref[...] = x
    h2 = _rms(x) * g2n_ref[...] * (1.0 + mod_ref[0, 4:5, :]) + mod_ref[0, 3:4, :]
    h2_ref[...] = _pack_halves(h2)
    idx, wgt = _route(h2, whi_ref, wlo_ref, rb_ref)
    idx_ref[...] = idx
    wgt_ref[...] = wgt


def _router_weights(router_w, router_b):
    d = router_w.shape[0]
    wt = jnp.zeros((ROUTER_PAD, d), jnp.float32).at[:N_EXPERTS].set(router_w.T.astype(jnp.float32))
    whi, wlo = _split_bf16(wt)
    rb = jnp.zeros((ROUTER_PAD, 128), jnp.float32).at[:N_EXPERTS].set(router_b.astype(jnp.float32)[:, None])
    return whi, wlo, rb


def _mixout(ylin, p, ymla, ylru, yret, x, mods, s5_d, wglu, wout, g2n, rw, n_ctx, dec_seq):
    n, d = x.shape
    tm = OUT_TM
    w = GROUP_W
    seq = functools.partial(_seq_of_tile, tm=tm, n_ctx=n_ctx, dec_seq=dec_seq)
    full = lambda a: pl.BlockSpec(a.shape, lambda i: (0,) * a.ndim)
    row = lambda width: pl.BlockSpec((tm, width), lambda i: (i, 0))
    lanes = pl.BlockSpec((TOP_K, tm), lambda i: (0, i))
    whi, wlo, rb = rw
    return pl.pallas_call(
        _mixout_kernel,
        grid=(n // tm,),
        in_specs=[row(w), row(w), row(w), row(w), row(w), row(d),
                  pl.BlockSpec((1, N_MOD, d), lambda i: (seq(i), 0, 0)),
                  full(s5_d), full(wglu), full(wout), full(g2n), full(whi), full(wlo), full(rb)],
        out_specs=[row(d), row(d // 2), lanes, lanes],
        out_shape=[jax.ShapeDtypeStruct((n, d), jnp.float32), jax.ShapeDtypeStruct((n, d // 2), jnp.int32),
                   jax.ShapeDtypeStruct((TOP_K, n), jnp.int32), jax.ShapeDtypeStruct((TOP_K, n), jnp.float32)],
        compiler_params=_cparams(1),
        name="mix_out_norm_route",
    )(ylin, p, ymla, ylru, yret, x, mods, s5_d, wglu, wout, g2n, whi, wlo, rb)


MOE_TM = 256


def _expert_kernel(te_ref, nt_ref, nv_ref, tok0_ref, tok1_ref, w_ref, dst_ref, wg_ref, wu_ref, wd_ref, h_hbm,
                   y_hbm, wg_s, wu_s, wd_s, xbuf, ybuf, gsem, ssem):
    i = pl.program_id(0)
    bf = jnp.bfloat16
    tm = ybuf.shape[0]
    nt = nt_ref[0]
    slot = i % 2

    def gather_start(tok_ref, sl):
        def row(r, carry):
            pltpu.make_async_copy(h_hbm.at[pl.ds(tok_ref[0, 0, r], 1), :], xbuf.at[sl, pl.ds(r, 1), :],
                                  gsem.at[sl]).start()
            return carry

        lax.fori_loop(0, tm, row, 0, unroll=8)

    def gather_done(sl):
        pltpu.make_async_copy(h_hbm.at[pl.ds(0, tm), :], xbuf.at[sl], gsem.at[sl]).wait()

    def scatter_done(n_rows):
        n8 = pl.multiple_of((n_rows // 8) * 8, 8)

        @pl.when(n8 > 0)
        def _():
            pltpu.make_async_copy(ybuf.at[pl.ds(0, n8), :], y_hbm.at[pl.ds(0, n8), :], ssem).wait()

        def one(r, carry):
            pltpu.make_async_copy(ybuf.at[pl.ds(0, 1), :], y_hbm.at[pl.ds(0, 1), :], ssem).wait()
            return carry

        lax.fori_loop(0, n_rows - n8, one, 0)

    @pl.when(i < nt)
    def _():
        @pl.when(i == 0)
        def _():
            gather_start(tok0_ref, 0)

        @pl.when(i + 1 < nt)
        def _():
            gather_start(tok1_ref, 1 - slot)

        @pl.when((i == 0) | (te_ref[i] != te_ref[jnp.maximum(i - 1, 0)]))
        def _():
            wg_s[...] = wg_ref[0, 0].astype(bf)
            wu_s[...] = wu_ref[0, 0].astype(bf)
            wd_s[...] = wd_ref[0, 0].astype(bf)

        gather_done(slot)
        x = _unpack_halves(xbuf[slot]).astype(bf)
        g = jnp.dot(x, wg_s[...], preferred_element_type=jnp.float32)
        u = jnp.dot(x, wu_s[...], preferred_element_type=jnp.float32)
        act = ((g * _sigmoid(g)) * u * w_ref[...]).astype(bf)
        y = jnp.dot(act, wd_s[...], preferred_element_type=jnp.float32)

        @pl.when(i > 0)
        def _():
            scatter_done(nv_ref[jnp.maximum(i - 1, 0)])

        ybuf[...] = _pack_halves(y)

        def row(r, carry):
            pltpu.make_async_copy(ybuf.at[pl.ds(r, 1), :], y_hbm.at[pl.ds(dst_ref[0, 0, r], 1), :], ssem).start()
            return carry

        lax.fori_loop(0, nv_ref[i], row, 0)

        @pl.when(i == nt - 1)
        def _():
            scatter_done(nv_ref[i])


def _experts(h2p, layer, ws, tok, dst, tile_expert, n_tiles_used, n_valid, wg, wu, wd):
    n, dh = h2p.shape
    d = 2 * dh
    tm = MOE_TM
    n_tiles = tok.shape[0]
    f = wg.shape[-1]
    smem_row = lambda nxt: pl.BlockSpec(
        (1, 1, tm), lambda i, te, nt, nv: (jnp.minimum(i + nxt, n_tiles - 1), 0, 0), memory_space=pltpu.SMEM)
    wspec = lambda a, b: pl.BlockSpec((1, 1, a, b), lambda i, te, nt, nv: (layer, te[i], 0, 0))
    return pl.pallas_call(
        _expert_kernel,
        grid_spec=pltpu.PrefetchScalarGridSpec(
            num_scalar_prefetch=3,
            grid=(n_tiles,),
            in_specs=[smem_row(0), smem_row(1),
                      pl.BlockSpec((tm, 1), lambda i, te, nt, nv: (i, 0)),
                      smem_row(0),
                      wspec(d, f), wspec(d, f), wspec(f, d),
                      pl.BlockSpec(memory_space=pl.ANY)],
            out_specs=pl.BlockSpec(memory_space=pl.ANY),
            scratch_shapes=[pltpu.VMEM((d, f), jnp.bfloat16), pltpu.VMEM((d, f), jnp.bfloat16),
                            pltpu.VMEM((f, d), jnp.bfloat16), pltpu.VMEM((2, tm, dh), jnp.int32),
                            pltpu.VMEM((tm, dh), jnp.int32), pltpu.SemaphoreType.DMA((2,)),
                            pltpu.SemaphoreType.DMA(())]),
        out_shape=jax.ShapeDtypeStruct((TOP_K * n, dh), jnp.int32),
        compiler_params=pltpu.CompilerParams(dimension_semantics=("arbitrary",),
                                             vmem_limit_bytes=V7X_VMEM_LIMIT_BIG, disable_bounds_checks=True),
        name="moe_experts",
    )(tile_expert, n_tiles_used, n_valid, tok, tok, ws, dst, wg, wu, wd, h2p)


def _moe_dispatch(idx, wgt):
    n = idx.shape[1]
    tm = MOE_TM
    n_pairs = TOP_K * n
    m_pad = n_pairs + N_EXPERTS * tm
    n_tiles = m_pad // tm
    e_flat = idx.reshape(-1)
    order = jnp.argsort(e_flat, stable=True).astype(jnp.int32)
    experts = jnp.arange(N_EXPERTS, dtype=jnp.int32)
    counts = jnp.sum((e_flat[None, :] == experts[:, None]).astype(jnp.int32), axis=1)
    starts_raw = jnp.cumsum(counts) - counts
    padded = ((counts + tm - 1) // tm) * tm
    ends = jnp.cumsum(padded)
    starts_pad = ends - padded
    tile_start = jnp.arange(n_tiles, dtype=jnp.int32) * tm
    tile_expert = jnp.minimum(jnp.sum((tile_start[:, None] >= ends[None, :]).astype(jnp.int32), axis=1),
                              N_EXPERTS - 1)
    r = jnp.arange(m_pad, dtype=jnp.int32)
    te_r = jnp.repeat(tile_expert, tm)
    off = r - jnp.take(starts_pad, te_r)
    valid = (off >= 0) & (off < jnp.take(counts, te_r))
    pair = jnp.take(order, jnp.clip(jnp.take(starts_raw, te_r) + off, 0, n_pairs - 1))
    tok = jnp.where(valid, pair % n, 0).reshape(n_tiles, 1, tm)
    dst = jnp.where(valid, pair, 0).reshape(n_tiles, 1, tm)
    ws = jnp.where(valid, jnp.take(wgt.reshape(-1), pair), 0.0)[:, None]
    n_tiles_used = (ends[-1] // tm).astype(jnp.int32).reshape(1)
    n_valid = jnp.sum(valid.reshape(n_tiles, tm), axis=1).astype(jnp.int32)
    return tok, dst, ws, tile_expert, n_tiles_used, n_valid


def _moe_rows(h2p, layer, idx, wgt, wg, wu, wd):
    tok, dst, ws, tile_expert, n_tiles_used, n_valid = _moe_dispatch(idx, wgt)
    return _experts(h2p, layer, ws, tok, dst, tile_expert, n_tiles_used, n_valid, wg, wu, wd)


RES_TM = 512


def _resid_kernel(x_ref, ya_ref, yb_ref, mod_ref, gf_ref, o_ref, *, final):
    x = x_ref[...] + mod_ref[0, 5:6, :] * (_unpack_halves(ya_ref[...]) + _unpack_halves(yb_ref[...]))
    o_ref[...] = _rms(x) * gf_ref[...] if final else x


def _resid(x, y2, mods, gf, n_ctx, dec_seq, final):
    n, d = x.shape
    tm = RES_TM
    seq = functools.partial(_seq_of_tile, tm=tm, n_ctx=n_ctx, dec_seq=dec_seq)
    row = pl.BlockSpec((tm, d), lambda i: (i, 0))
    return pl.pallas_call(
        functools.partial(_resid_kernel, final=final),
        grid=(n // tm,),
        in_specs=[row, pl.BlockSpec((tm, d // 2), lambda i: (i, 0)),
                  pl.BlockSpec((tm, d // 2), lambda i: (n // tm + i, 0)),
                  pl.BlockSpec((1, N_MOD, d), lambda i: (seq(i), 0, 0)),
                  pl.BlockSpec((1, d), lambda i: (0, 0))],
        out_specs=row,
        out_shape=jax.ShapeDtypeStruct((n, d), jnp.float32),
        compiler_params=_cparams(1),
        name="moe_residual_norm",
    )(x, y2, y2, mods, gf)


def kernel(x_prompt, x_sample, c, cache_mla_ckv, cache_mla_kpe, state_s5, state_lru, state_ret,
           c_ctx, w_ada, b_ada, norm1_g, norm2_g, w_in, w_out,
           s5_a_re, s5_a_im, s5_log_dt, s5_b_re, s5_b_im, s5_c_re, s5_c_im, s5_d, s5_w_glu,
           mla_q_norm_g, mla_w_uq, mla_kv_norm_g, mla_w_ukv,
           lru_conv_w, lru_conv_b, lru_w_a, lru_b_a, lru_w_x, lru_b_x, lru_lambda,
           router_w, router_b, moe_w_gate, moe_w_up, moe_w_down, final_norm_g):
    f32, bf = jnp.float32, jnp.bfloat16
    batch, seq_len, d = x_prompt.shape
    dec_batch, dec_seq, _ = x_sample.shape
    past = cache_mla_ckv.shape[2]
    n_ctx, n_lat = batch * seq_len, dec_batch * dec_seq
    depth = w_in.shape[0]

    x = jnp.concatenate([x_prompt.reshape(n_ctx, d), x_sample.reshape(n_lat, d)], axis=0)
    cvec = jnp.zeros((8, d), f32).at[0].set(c_ctx).at[1:1 + dec_batch].set(c)
    mods_all = _ada(cvec, w_ada, b_ada).reshape(depth, 8, N_MOD, d)

    cos64, sin64 = _rope_tables(dec_seq, MLA_ROPE)
    lat_tab = jnp.tile(jnp.concatenate([cos64, sin64], axis=1), (dec_batch, 1))
    ctx_tab = jnp.concatenate([jnp.ones((n_ctx, MLA_ROPE), f32), jnp.zeros((n_ctx, MLA_ROPE), f32)], axis=1)
    mla_tab = jnp.concatenate([ctx_tab, lat_tab], axis=0)
    ret_rope = _rope_tables(dec_seq, RET_DK)
    ret_tables = _ret_tables()
    rw = _router_weights(router_w, router_b)
    gf = final_norm_g[None, :]

    states = []
    for l in range(depth):
        mods = mods_all[l]
        s5_tab = _s5_tables(s5_a_re[l], s5_a_im[l], s5_log_dt[l], s5_b_re[l], s5_b_im[l], s5_c_re[l], s5_c_im[l])
        lru_tab = _lru_tables(lru_conv_w[l], lru_conv_b[l], lru_w_a[l], lru_b_a[l], lru_w_x[l], lru_b_x[l],
                              lru_lambda[l])
        wuq, wukv = _mla_weights(mla_w_uq[l], mla_w_ukv[l])

        p = _inproj(x, mods, norm1_g[l][None, :], _inproj_weights(w_in[l]), n_ctx, dec_seq)

        u_ctx = p[:n_ctx, C_S5:C_S5 + S5_CH].reshape(batch, seq_len, S5_CH)
        u_lat = p[n_ctx:, C_S5:C_S5 + S5_CH].reshape(dec_batch, dec_seq, S5_CH)
        ylin, s5_fins = _s5_core([u_ctx, u_lat], [None, state_s5[:, l]], s5_tab)

        qo, ko, vo, ckv = _mla_prep(p, mla_tab, mla_q_norm_g[l][None, :], mla_kv_norm_g[l][None, :], wuq, wukv)
        ckv_c = cache_mla_ckv[:, l].reshape(dec_batch * past, MLA_KV_RANK)
        kv_c = _matmul(ckv_c, wukv, out_dtype=bf).reshape(dec_batch * past, MLA_HEADS, 2, 128)
        kpe_c = jnp.broadcast_to(cache_mla_kpe[:, l].reshape(dec_batch * past, 1, MLA_ROPE).astype(bf),
                                 (dec_batch * past, MLA_HEADS, MLA_ROPE))
        k_c = jnp.concatenate([kv_c[:, :, 0], kpe_c, jnp.zeros_like(kpe_c)], axis=-1).reshape(
            dec_batch * past, MLA_HEADS * ATT_DQ)
        v_c = kv_c[:, :, 1].reshape(dec_batch * past, MLA_HEADS * MLA_V)
        ymla = jnp.concatenate([_attention(qo, ko, vo, 0, batch, seq_len),
                                _attention(qo, ko, vo, n_ctx, dec_batch, dec_seq, k_c, v_c)], axis=0)

        ylru_c, lru_fin = _lru(p, 0, batch, seq_len, jnp.zeros((batch, 2, LRU_W), f32), lru_tab)
        ylru_l, _ = _lru(p, n_ctx, dec_batch, dec_seq, state_lru[:, l].astype(f32), lru_tab)
        ylru = jnp.concatenate([ylru_c, ylru_l], axis=0)

        yret_c, ret_fin = _retention(p, 0, batch, seq_len,
                                     jnp.zeros((batch, 2, RET_HEADS, RET_DK, RET_DV), f32), ret_tables)
        yret_l, _ = _retention(p, n_ctx, dec_batch, dec_seq, state_ret[:, l].astype(f32), ret_tables, ret_rope)
        yret = jnp.concatenate([yret_c, yret_l], axis=0)

        x_mid, h2, idx, wgt = _mixout(ylin, p, ymla, ylru, yret, x, mods, s5_d[l][None, :],
                                      s5_w_glu[l].astype(bf), w_out[l].astype(bf), norm2_g[l][None, :], rw,
                                      n_ctx, dec_seq)
        y2 = _moe_rows(h2, l, idx, wgt, moe_w_gate, moe_w_up, moe_w_down)
        x = _resid(x_mid, y2, mods, gf, n_ctx, dec_seq, final=(l == depth - 1))

        states.append((ckv[:n_ctx].reshape(batch, seq_len, MLA_KV_RANK),
                       p[:n_ctx, C_KR:C_KR + MLA_ROPE].astype(f32).reshape(batch, seq_len, MLA_ROPE),
                       s5_fins[0], lru_fin, ret_fin))

    y_prompt = x[:n_ctx].reshape(batch, seq_len, d)
    y_sample = x[n_ctx:].reshape(dec_batch, dec_seq, d)
    new_cache_mla_ckv = jnp.stack([st[0] for st in states], axis=1)
    new_cache_mla_kpe = jnp.stack([st[1] for st in states], axis=1)
    new_state_s5 = jnp.stack([st[2] for st in states], axis=1)
    new_state_lru = jnp.stack([st[3] for st in states], axis=1)
    new_state_ret = jnp.stack([st[4] for st in states], axis=1)
    return (y_prompt, y_sample, new_cache_mla_ckv, new_cache_mla_kpe, new_state_s5, new_state_lru, new_state_ret)
```

```python
import functools
import numpy as np
import jax
import jax.numpy as jnp
from jax import lax
from jax.experimental import pallas as pl
from jax.experimental.pallas import tpu as pltpu

D_MODEL = 2048
DEPTH = 2
GRID_W = 64
EPS = 1e-6
ROPE_BASE = 10000.0
N_MOD = 6
GROUP_W = 512
S5_CH = GROUP_W
S5_GROUP_CH = 16
S5_GROUPS = S5_CH // S5_GROUP_CH
S5_STATE = 64
MLA_HEADS = 4
MLA_NOPE = 128
MLA_ROPE = 64
MLA_V = 128
MLA_Q_RANK = GROUP_W
MLA_KV_RANK = GROUP_W // 2
MLA_SCALE = (MLA_NOPE + MLA_ROPE) ** -0.5
LRU_W = GROUP_W
LRU_CONV = 4
LRU_C = 8.0
RET_HEADS = 4
RET_DK = 128
RET_DV = 128
RET_CHUNK = 128
N_EXPERTS = 16
N_EXPERT_GROUPS = 4
EXPERTS_PER_GROUP = N_EXPERTS // N_EXPERT_GROUPS
TOP_K = 2
D_EXPERT = D_MODEL // 4

V7X_VMEM_LIMIT = 48 * 1024 * 1024
V7X_VMEM_LIMIT_BIG = 56 * 1024 * 1024

C_S5, C_MQ, C_LX, C_LG, C_RQ, C_RK, C_RV, C_RG, C_RQP, C_RKP, C_MKV, C_KR = (
    0, 512, 1024, 1536, 2048, 2560, 3072, 3584, 4096, 4608, 5120, 5376)
P_COLS = 5632
R_S5, R_MQ, R_MKV, R_KR, R_LX, R_LG, R_RQ, R_RK, R_RV, R_RG = (0, 512, 1024, 1280, 1344, 1856, 2368, 2880, 3392, 3904)


def _cparams(n_axes, limit=V7X_VMEM_LIMIT):
    return pltpu.CompilerParams(dimension_semantics=("arbitrary",) * n_axes, vmem_limit_bytes=limit)


def _sigmoid(x):
    return 1.0 / (1.0 + jnp.exp(-x))


def _gelu_tanh(x):
    return 0.5 * x * (1.0 + jnp.tanh(0.7978845608028654 * (x + 0.044715 * (x * x * x))))


def _rms(x):
    return x * lax.rsqrt(jnp.mean(x * x, axis=-1, keepdims=True) + EPS)


def _seq_of_tile(i, tm, n_ctx, dec_seq):
    assert n_ctx % tm == 0 and dec_seq % tm == 0, "a row tile must not straddle two sequences"
    r = i * tm
    return jnp.where(r < n_ctx, 0, 1 + (r - n_ctx) // dec_seq)


def _mm_kernel(x_ref, w_ref, o_ref):
    o_ref[...] = jnp.dot(x_ref[...].astype(jnp.bfloat16), w_ref[...].astype(jnp.bfloat16),
                         preferred_element_type=jnp.float32).astype(o_ref.dtype)


def _matmul(x, w, tm=512, tn=512, out_dtype=jnp.float32):
    m, k = x.shape
    _, n = w.shape
    tm, tn = min(tm, m), min(tn, n)
    return pl.pallas_call(
        _mm_kernel,
        grid=(n // tn, m // tm),
        in_specs=[pl.BlockSpec((tm, k), lambda j, i: (i, 0)), pl.BlockSpec((k, tn), lambda j, i: (0, j))],
        out_specs=pl.BlockSpec((tm, tn), lambda j, i: (i, j)),
        out_shape=jax.ShapeDtypeStruct((m, n), out_dtype),
        compiler_params=_cparams(2),
        name="matmul",
    )(x, w)


ADA_TN = 1024


def _ada_kernel(c_ref, w_ref, b_ref, o_ref):
    c = c_ref[...]
    s = (c * _sigmoid(c)).astype(jnp.bfloat16)
    o_ref[0] = jnp.dot(s, w_ref[0].astype(jnp.bfloat16), preferred_element_type=jnp.float32) + b_ref[0]


def _ada(cvec, w_ada, b_ada):
    depth, d, n = w_ada.shape
    return pl.pallas_call(
        _ada_kernel,
        grid=(depth, n // ADA_TN),
        in_specs=[pl.BlockSpec((8, d), lambda l, j: (0, 0)),
                  pl.BlockSpec((1, d, ADA_TN), lambda l, j: (l, 0, j)),
                  pl.BlockSpec((1, 1, ADA_TN), lambda l, j: (l, 0, j))],
        out_specs=pl.BlockSpec((1, 8, ADA_TN), lambda l, j: (l, 0, j)),
        out_shape=jax.ShapeDtypeStruct((depth, 8, n), jnp.float32),
        compiler_params=_cparams(2),
        name="adaln_mod",
    )(cvec, w_ada, b_ada.reshape(depth, 1, n))


IN_TM = 1024
IN_TN = 512


def _inproj_kernel(x_ref, mod_ref, g_ref, w_ref, o_ref, h_s):
    @pl.when(pl.program_id(1) == 0)
    def _():
        h = _rms(x_ref[...]) * g_ref[...]
        h_s[...] = (h * (1.0 + mod_ref[0, 1:2, :]) + mod_ref[0, 0:1, :]).astype(h_s.dtype)

    o_ref[...] = jnp.dot(h_s[...], w_ref[...], preferred_element_type=jnp.float32).astype(o_ref.dtype)


def _inproj(x, mods, g, w, n_ctx, dec_seq):
    n, d = x.shape
    seq = functools.partial(_seq_of_tile, tm=IN_TM, n_ctx=n_ctx, dec_seq=dec_seq)
    return pl.pallas_call(
        _inproj_kernel,
        grid=(n // IN_TM, P_COLS // IN_TN),
        in_specs=[pl.BlockSpec((IN_TM, d), lambda i, j: (i, 0)),
                  pl.BlockSpec((1, N_MOD, d), lambda i, j: (seq(i), 0, 0)),
                  pl.BlockSpec((1, d), lambda i, j: (0, 0)),
                  pl.BlockSpec((d, IN_TN), lambda i, j: (0, j))],
        out_specs=pl.BlockSpec((IN_TM, IN_TN), lambda i, j: (i, j)),
        out_shape=jax.ShapeDtypeStruct((n, P_COLS), jnp.bfloat16),
        scratch_shapes=[pltpu.VMEM((IN_TM, d), jnp.bfloat16)],
        compiler_params=_cparams(2),
        name="norm_inproj",
    )(x, mods, g, w)


def _rot_partner_cols(w, n_heads, head_dim):
    q = head_dim // 4
    wr = w.reshape(w.shape[0], n_heads, 2, 2, q)
    return jnp.stack([-wr[:, :, :, 1], wr[:, :, :, 0]], axis=3).reshape(w.shape)


def _inproj_weights(w_in):
    part = lambda off, width: w_in[:, off:off + width]
    rq, rk, kr = part(R_RQ, 512), part(R_RK, 512), part(R_KR, MLA_ROPE)
    cols = [part(R_S5, 512), part(R_MQ, 512), part(R_LX, 512), part(R_LG, 512), rq, rk, part(R_RV, 512),
            part(R_RG, 512), _rot_partner_cols(rq, RET_HEADS, RET_DK), _rot_partner_cols(rk, RET_HEADS, RET_DK),
            part(R_MKV, MLA_KV_RANK), kr, _rot_partner_cols(kr, 1, MLA_ROPE)]
    w = jnp.concatenate(cols, axis=1)
    return jnp.pad(w, ((0, 0), (0, P_COLS - w.shape[1]))).astype(jnp.bfloat16)


def _rope_tables(t_len, rot_dim):
    rows = t_len // GRID_W
    row = jnp.repeat(jnp.arange(rows, dtype=jnp.float32), GRID_W)
    col = jnp.tile(jnp.arange(GRID_W, dtype=jnp.float32), rows)
    n_freq = rot_dim // 4
    inv = ROPE_BASE ** (-jnp.arange(n_freq, dtype=jnp.float32) / n_freq)
    ang = jnp.concatenate([row[:, None] * inv[None]] * 2 + [col[:, None] * inv[None]] * 2, axis=1)
    return jnp.cos(ang), jnp.sin(ang)


S5_L = 32
S5_PAIRS = S5_GROUPS // 2
S5_Q = 4
S5_LANES = S5_GROUPS * S5_STATE
S5_SCAN_LANES = 1024


def _s5_tables(a_re, a_im, log_dt, b_re, b_im, c_re, c_im):
    f32 = jnp.float32
    L, G, P, C = S5_L, S5_GROUPS, S5_STATE, S5_GROUP_CH
    hp = lax.Precision.HIGHEST
    cmul = lambda xr, xi, yr, yi: (xr * yr - xi * yi, xr * yi + xi * yr)
    dt = jnp.exp(log_dt.astype(f32))[..., None]
    zr, zi = a_re * dt, a_im * dt
    ab_r, ab_i = jnp.exp(zr) * jnp.cos(zi), jnp.exp(zr) * jnp.sin(zi)
    den = a_re * a_re + a_im * a_im
    nr, ni = ab_r - 1.0, ab_i
    be_r = (nr * a_re + ni * a_im) / den
    be_i = (ni * a_re - nr * a_im) / den
    bt_r, bt_i = b_re.transpose(0, 2, 1), b_im.transpose(0, 2, 1)
    bp_r, bp_i = cmul(be_r[:, :, None, :], be_i[:, :, None, :], bt_r[None], bt_i[None])
    tau = jnp.arange(L + 1, dtype=f32)[None, None, :, None]
    mag = jnp.exp(zr[:, :, None, :] * tau)
    pw_r, pw_i = mag * jnp.cos(zi[:, :, None, :] * tau), mag * jnp.sin(zi[:, :, None, :] * tau)

    pws = lambda d, rev: tuple(x[d][:, :L][:, ::-1] if rev else x[d][:, :L] for x in (pw_r, pw_i))
    coef = lambda d, rev: cmul(*(x[:, :, None, :] for x in pws(d, rev)), bp_r[d][:, None], bp_i[d][:, None])
    ws = jnp.stack(coef(0, True) + coef(1, False), axis=3).reshape(G, L * C, S5_Q * P)

    ct_r, ct_i = c_re.transpose(0, 2, 1), c_im.transpose(0, 2, 1)
    pt_r, pt_i = pw_r.transpose(0, 1, 3, 2), pw_i.transpose(0, 1, 3, 2)

    def cm(er, ei):
        m_r, m_i = cmul(ct_r[:, :, None, :], ct_i[:, :, None, :], er[..., None], ei[..., None])
        return m_r.reshape(G, P, L * C), m_i.reshape(G, P, L * C)

    mf_r, mf_i = cm(pt_r[0][:, :, 1:L + 1], pt_i[0][:, :, 1:L + 1])
    mb_r, mb_i = cm(pt_r[1][:, :, 1:L + 1][:, :, ::-1], pt_i[1][:, :, 1:L + 1][:, :, ::-1])
    wo = jnp.stack([mf_r, -mf_i, mb_r, -mb_i], axis=1).reshape(G, S5_Q * P, L * C)

    def impulse(d):
        m_r, m_i = cm(pt_r[d][:, :, :L], pt_i[d][:, :, :L])
        kk = (jnp.einsum('gkp,gpx->gkx', bp_r[d], m_r, precision=hp)
              - jnp.einsum('gkp,gpx->gkx', bp_i[d], m_i, precision=hp))
        return kk.reshape(G, C, L, C)

    kf, kb = impulse(0), impulse(1)
    wide = jnp.concatenate([kb[:, :, :0:-1], kf[:, :, :1] + kb[:, :, :1], kf[:, :, 1:]], axis=2)
    wide = jnp.pad(wide.reshape(G, C, (2 * L - 1) * C), ((0, 0), (0, 0), (0, C)))

    a_l = jnp.stack([pw_r[0, :, L], pw_i[0, :, L], pw_r[1, :, L], pw_i[1, :, L]], axis=0).reshape(S5_Q, 1, S5_LANES)
    bf = jnp.bfloat16
    return _s5_toeplitz(wide), ws.astype(bf), wo.astype(bf), a_l


def _s5_toeplitz_kernel(w_ref, o_ref):
    w = w_ref[0]
    for s in range(S5_L):
        off = (S5_L - 1 - s) * S5_GROUP_CH
        o_ref[0, s * S5_GROUP_CH:(s + 1) * S5_GROUP_CH, :] = w[:, off:off + S5_L * S5_GROUP_CH].astype(o_ref.dtype)


def _s5_toeplitz(wide):
    g, c, wl = wide.shape
    n = S5_L * S5_GROUP_CH
    return pl.pallas_call(
        _s5_toeplitz_kernel,
        grid=(g,),
        in_specs=[pl.BlockSpec((1, c, wl), lambda i: (i, 0, 0))],
        out_specs=pl.BlockSpec((1, n, n), lambda i: (i, 0, 0)),
        out_shape=jax.ShapeDtypeStruct((g, n, n), jnp.bfloat16),
        compiler_params=_cparams(1),
        name="s5_toeplitz",
    )(wide)


def _s5a_kernel(x_ref, t_ref, ws_ref, y1_ref, s_ref):
    ss = []
    for j in range(2):
        x = x_ref[j]
        y1_ref[j] = jnp.dot(x, t_ref[j], preferred_element_type=jnp.float32)
        ss.append(jnp.dot(x, ws_ref[j], preferred_element_type=jnp.float32))
    p = S5_STATE
    for q in range(S5_Q):
        s_ref[q] = jnp.concatenate([ss[0][:, q * p:(q + 1) * p], ss[1][:, q * p:(q + 1) * p]], axis=1)


def _s5a(x, toe, ws):
    g, r, w = x.shape
    return pl.pallas_call(
        _s5a_kernel,
        grid=(S5_PAIRS,),
        in_specs=[pl.BlockSpec((2, r, w), lambda i: (i, 0, 0)),
                  pl.BlockSpec((2, w, w), lambda i: (i, 0, 0)),
                  pl.BlockSpec((2, w, S5_Q * S5_STATE), lambda i: (i, 0, 0))],
        out_specs=[pl.BlockSpec((2, r, w), lambda i: (i, 0, 0)),
                   pl.BlockSpec((S5_Q, r, 128), lambda i: (0, 0, i))],
        out_shape=[jax.ShapeDtypeStruct((g, r, w), jnp.float32),
                   jax.ShapeDtypeStruct((S5_Q, r, S5_LANES), jnp.float32)],
        compiler_params=_cparams(1),
        name="s5_chunk_local",
    )(x, toe, ws)


def _s5b_kernel(s_ref, a_ref, h0_ref, hp_ref, fin_ref, *, nc):
    def run(qr, qi, order_fwd):
        ar, ai = a_ref[qr], a_ref[qi]

        def body(i, carry):
            hr, hi = carry
            k = i if order_fwd else nc - 1 - i
            hp_ref[qr, pl.ds(k, 1), :] = hr
            hp_ref[qi, pl.ds(k, 1), :] = hi
            nr = ar * hr - ai * hi + s_ref[qr, pl.ds(k, 1), :]
            ni = ar * hi + ai * hr + s_ref[qi, pl.ds(k, 1), :]
            return nr, ni

        hr, hi = lax.fori_loop(0, nc, body, (h0_ref[qr], h0_ref[qi]))
        fin_ref[qr] = hr
        fin_ref[qi] = hi

    run(0, 1, True)
    run(2, 3, False)


def _s5b(s, a_l, h0):
    _, nc, w = s.shape
    tl = S5_SCAN_LANES
    vec = pl.BlockSpec((S5_Q, 1, tl), lambda i: (0, 0, i))
    seq = pl.BlockSpec((S5_Q, nc, tl), lambda i: (0, 0, i))
    return pl.pallas_call(
        functools.partial(_s5b_kernel, nc=nc),
        grid=(w // tl,),
        in_specs=[seq, vec, vec],
        out_specs=[seq, vec],
        out_shape=[jax.ShapeDtypeStruct((S5_Q, nc, w), jnp.float32),
                   jax.ShapeDtypeStruct((S5_Q, 1, w), jnp.float32)],
        compiler_params=_cparams(1),
        name="s5_chunk_scan",
    )(s, a_l, h0)


def _s5c_kernel(h_ref, wo_ref, y1_ref, y_ref):
    p = S5_STATE
    for j in range(2):
        hcat = jnp.concatenate([h_ref[q][:, j * p:(j + 1) * p] for q in range(S5_Q)], axis=1)
        y2 = jnp.dot(hcat.astype(jnp.bfloat16), wo_ref[j], preferred_element_type=jnp.float32)
        y_ref[j] = (y1_ref[j] + y2).astype(y_ref.dtype)


def _s5c(hprev, wo, y1):
    g, r, w = y1.shape
    return pl.pallas_call(
        _s5c_kernel,
        grid=(S5_PAIRS,),
        in_specs=[pl.BlockSpec((S5_Q, r, 128), lambda i: (0, 0, i)),
                  pl.BlockSpec((2, S5_Q * S5_STATE, w), lambda i: (i, 0, 0)),
                  pl.BlockSpec((2, r, w), lambda i: (i, 0, 0))],
        out_specs=pl.BlockSpec((2, r, w), lambda i: (i, 0, 0)),
        out_shape=jax.ShapeDtypeStruct((g, r, w), jnp.float32),
        compiler_params=_cparams(1),
        name="s5_state_to_out",
    )(hprev, wo, y1)


def _s5_to_chunks(u):
    b, t, _ = u.shape
    nc = t // S5_L
    x = u.reshape(b, nc, S5_L, S5_GROUPS, S5_GROUP_CH).transpose(3, 1, 0, 2, 4)
    return x.reshape(S5_GROUPS, nc * b, S5_L * S5_GROUP_CH)


def _s5_from_chunks(y, b, t):
    nc = t // S5_L
    y = y.reshape(S5_GROUPS, nc, b, S5_L, S5_GROUP_CH).transpose(2, 1, 3, 0, 4)
    return y.reshape(b * t, S5_CH)


def _s5_state_planes(h0):
    b = h0.shape[0]
    return h0.transpose(1, 4, 0, 2, 3).reshape(S5_Q, b, S5_LANES)


def _s5_core(us, h0s, tables):
    toe, ws_pair, wo_pair, a_l = tables
    xs = [_s5_to_chunks(u) for u in us]
    rows = [x.shape[1] for x in xs]
    y1, s = _s5a(jnp.concatenate(xs, axis=1), toe, ws_pair)
    hps, fins = [], []
    off = 0
    for u, h0, r in zip(us, h0s, rows):
        b, t, _ = u.shape
        nc = t // S5_L
        if h0 is None:
            h0p = jnp.zeros((S5_Q, 1, b * S5_LANES), jnp.float32)
        else:
            h0p = _s5_state_planes(h0.astype(jnp.float32)).reshape(S5_Q, 1, b * S5_LANES)
        hp, fin = _s5b(s[:, off:off + r].reshape(S5_Q, nc, b * S5_LANES), jnp.tile(a_l, (1, 1, b)), h0p)
        hps.append(hp.reshape(S5_Q, r, S5_LANES))
        fins.append(fin.reshape(2, 2, b, S5_GROUPS, S5_STATE).transpose(2, 0, 3, 4, 1))
        off += r
    y = _s5c(jnp.concatenate(hps, axis=1), wo_pair, y1)
    outs, off = [], 0
    for u, r in zip(us, rows):
        b, t, _ = u.shape
        outs.append(_s5_from_chunks(y[:, off:off + r], b, t))
        off += r
    return jnp.concatenate(outs, axis=0), fins


LRU_TC = 256
LRU_HALO = 16


def _lru_kernel(x_ref, gate_ref, cw_ref, cb_ref, wg_ref, bg_ref, sp_ref, h0_ref, out_ref, fin_ref,
                a_s, b_s, hf_s, *, t_len, tc):
    f32 = jnp.float32
    nt = t_len // tc
    w = LRU_W

    def gates(c, d):
        r0 = pl.multiple_of(c * tc, tc)
        lo = pl.multiple_of(jnp.maximum(r0 - LRU_HALO, 0), LRU_HALO)
        hi = pl.multiple_of(jnp.minimum(r0 + tc, t_len - LRU_HALO), LRU_HALO)
        prev = jnp.where(c > 0, x_ref[pl.ds(lo, LRU_HALO), :].astype(f32), 0.0)
        nxt = jnp.where(c < nt - 1, x_ref[pl.ds(hi, LRU_HALO), :].astype(f32), 0.0)
        slab = jnp.concatenate([prev, x_ref[pl.ds(r0, tc), :].astype(f32), nxt], axis=0)
        o = LRU_HALO - LRU_CONV // 2
        xc = cb_ref[...] + sum(cw_ref[k:k + 1, :] * slab[o + k:o + k + tc] for k in range(LRU_CONV))
        g = jnp.dot(xc.astype(jnp.bfloat16), wg_ref[d], preferred_element_type=f32) + bg_ref[d]
        r = _sigmoid(g[:, :w])
        i = _sigmoid(g[:, w:])
        log_a = -sp_ref[d] * r
        a_s[...] = jnp.exp(log_a)
        b_s[...] = jnp.sqrt(1.0 - jnp.exp(2.0 * log_a)) * (i * xc)
        return r0

    def fwd_chunk(c, h):
        r0 = gates(c, 0)

        def rows(i, h):
            for j in range(8):
                t = i * 8 + j
                h = a_s[pl.ds(t, 1), :] * h + b_s[pl.ds(t, 1), :]
                hf_s[pl.ds(r0 + t, 1), :] = h
            return h

        return lax.fori_loop(0, tc // 8, rows, h)

    h = lax.fori_loop(0, nt, fwd_chunk, h0_ref[0, 0:1, :])
    fin_ref[0, 0:1, :] = h

    def bwd_chunk(ci, h):
        r0 = gates(nt - 1 - ci, 1)

        def rows(i, h):
            for j in range(8):
                t = tc - 1 - (i * 8 + j)
                h = a_s[pl.ds(t, 1), :] * h + b_s[pl.ds(t, 1), :]
                hf_s[pl.ds(r0 + t, 1), :] = hf_s[pl.ds(r0 + t, 1), :] + h
            return h

        h = lax.fori_loop(0, tc // 8, rows, h)
        sl = pl.ds(r0, tc)
        out_ref[sl, :] = (hf_s[sl, :] * _gelu_tanh(gate_ref[sl, :].astype(f32))).astype(out_ref.dtype)
        return h

    h = lax.fori_loop(0, nt, bwd_chunk, h0_ref[0, 1:2, :])
    fin_ref[0, 1:2, :] = h


def _block_diag(wb):
    n, k, j = wb.shape
    return (wb[:, :, None, :] * jnp.eye(n, dtype=wb.dtype)[:, None, :, None]).reshape(n * k, n * j)


def _lru_tables(conv_w, conv_b, w_a, b_a, w_x, b_x, lam):
    wg = jnp.stack([jnp.concatenate([_block_diag(w_a[d]), _block_diag(w_x[d])], axis=1) for d in range(2)])
    bg = jnp.concatenate([b_a, b_x], axis=-1)[:, None, :]
    sp = (LRU_C * jax.nn.softplus(-lam.astype(jnp.float32)))[:, None, :]
    return conv_w, conv_b[None, :], wg.astype(jnp.bfloat16), bg, sp


def _lru(p, row0, b, t, h0, tables):
    w = LRU_W
    tc = min(LRU_TC, t)
    cw, cb, wg, bg, sp = tables
    full = lambda a: pl.BlockSpec(a.shape, lambda i: (0,) * a.ndim)
    rb = row0 // t
    return pl.pallas_call(
        functools.partial(_lru_kernel, t_len=t, tc=tc),
        grid=(b,),
        in_specs=[pl.BlockSpec((t, w), lambda i: (rb + i, C_LX // w)),
                  pl.BlockSpec((t, w), lambda i: (rb + i, C_LG // w)),
                  full(cw), full(cb), full(wg), full(bg), full(sp),
                  pl.BlockSpec((1, 2, w), lambda i: (i, 0, 0))],
        out_specs=[pl.BlockSpec((t, w), lambda i: (i, 0)),
                   pl.BlockSpec((1, 2, w), lambda i: (i, 0, 0))],
        out_shape=[jax.ShapeDtypeStruct((b * t, w), jnp.bfloat16),
                   jax.ShapeDtypeStruct((b, 2, w), jnp.float32)],
        scratch_shapes=[pltpu.VMEM((tc, w), jnp.float32), pltpu.VMEM((tc, w), jnp.float32),
                        pltpu.VMEM((t, w), jnp.float32)],
        compiler_params=_cparams(1),
        name="rglru",
    )(p, p, cw, cb, wg, bg, sp, h0)


MLA_TM = 512
ATT_DQ = 256


def _mla_prep_kernel(q_ref, kv_ref, kr_ref, tab_ref, gq_ref, gkv_ref, wuq_ref, wukv_ref,
                     qo_ref, ko_ref, vo_ref, ckv_ref):
    f32, bf = jnp.float32, jnp.bfloat16
    tab = tab_ref[...]

    def rope(blk):
        prod = blk * tab
        return prod + pltpu.roll(prod, 64, 1)

    qn = (_rms(q_ref[...].astype(f32)) * gq_ref[...]).astype(bf)
    qq = jnp.dot(qn, wuq_ref[...], preferred_element_type=f32)
    ckv = _rms(kv_ref[...].astype(f32)) * gkv_ref[...]
    ckv_ref[...] = ckv
    kk = jnp.dot(ckv.astype(bf), wukv_ref[...], preferred_element_type=f32)
    lane = lax.broadcasted_iota(jnp.int32, tab.shape, 1)
    kpe = jnp.where(lane < MLA_ROPE, rope(kr_ref[...].astype(f32)), 0.0).astype(bf)
    for h in range(MLA_HEADS):
        o = h * ATT_DQ
        qo_ref[:, o:o + 128] = (qq[:, o:o + 128] * MLA_SCALE).astype(bf)
        qo_ref[:, o + 128:o + 256] = (rope(qq[:, o + 128:o + 256]) * MLA_SCALE).astype(bf)
        ko_ref[:, o:o + 128] = kk[:, o:o + 128].astype(bf)
        ko_ref[:, o + 128:o + 256] = kpe
        vo_ref[:, h * MLA_V:(h + 1) * MLA_V] = kk[:, o + 128:o + 256].astype(bf)


def _mla_weights(w_uq, w_ukv):
    wq = w_uq.reshape(MLA_Q_RANK, MLA_HEADS, MLA_NOPE + MLA_ROPE)
    pe = wq[:, :, MLA_NOPE:].reshape(MLA_Q_RANK, MLA_HEADS * MLA_ROPE)
    pep = _rot_partner_cols(pe, MLA_HEADS, MLA_ROPE).reshape(MLA_Q_RANK, MLA_HEADS, MLA_ROPE)
    wq_ext = jnp.concatenate([wq, pep], axis=-1).reshape(MLA_Q_RANK, MLA_HEADS * ATT_DQ)
    return wq_ext.astype(jnp.bfloat16), w_ukv.astype(jnp.bfloat16)


def _mla_prep(p, tab, gq, gkv, wuq, wukv):
    n = p.shape[0]
    tm = MLA_TM
    full = lambda a: pl.BlockSpec(a.shape, lambda i: (0,) * a.ndim)
    row = lambda width, col: pl.BlockSpec((tm, width), lambda i: (i, col // width))
    return pl.pallas_call(
        _mla_prep_kernel,
        grid=(n // tm,),
        in_specs=[row(MLA_Q_RANK, C_MQ), row(MLA_KV_RANK, C_MKV), row(128, C_KR), row(128, 0),
                  full(gq), full(gkv), full(wuq), full(wukv)],
        out_specs=[row(MLA_HEADS * ATT_DQ, 0), row(MLA_HEADS * ATT_DQ, 0), row(MLA_HEADS * MLA_V, 0),
                   row(MLA_KV_RANK, 0)],
        out_shape=[jax.ShapeDtypeStruct((n, MLA_HEADS * ATT_DQ), jnp.bfloat16),
                   jax.ShapeDtypeStruct((n, MLA_HEADS * ATT_DQ), jnp.bfloat16),
                   jax.ShapeDtypeStruct((n, MLA_HEADS * MLA_V), jnp.bfloat16),
                   jax.ShapeDtypeStruct((n, MLA_KV_RANK), jnp.float32)],
        compiler_params=_cparams(1),
        name="mla_prep",
    )(p, p, p, tab, gq, gkv, wuq, wukv)


ATT_TQ = 256


def _attn_kernel(*refs, two):
    dn = (((1,), (1,)), ((), ()))
    f32 = jnp.float32
    if two:
        q_ref, k_ref, v_ref, k2_ref, v2_ref, o_ref = refs
    else:
        q_ref, k_ref, v_ref, o_ref = refs
    q = q_ref[...]
    s = lax.dot_general(q, k_ref[...], dn, preferred_element_type=f32)
    m = jnp.max(s, axis=-1, keepdims=True)
    if two:
        s2 = lax.dot_general(q, k2_ref[...], dn, preferred_element_type=f32)
        m = jnp.maximum(m, jnp.max(s2, axis=-1, keepdims=True))
    p = jnp.exp(s - m)
    l = jnp.sum(p, axis=-1, keepdims=True)
    o = jnp.dot(p.astype(jnp.bfloat16), v_ref[...], preferred_element_type=f32)
    if two:
        p2 = jnp.exp(s2 - m)
        l = l + jnp.sum(p2, axis=-1, keepdims=True)
        o = o + jnp.dot(p2.astype(jnp.bfloat16), v2_ref[...], preferred_element_type=f32)
    o_ref[...] = (o / l).astype(o_ref.dtype)


def _attention(q, k, v, row0, b, t, k2=None, v2=None):
    tb = min(ATT_TQ, t)
    nq = t // tb
    qb, kb = row0 // tb, row0 // t
    two = k2 is not None
    in_specs = [pl.BlockSpec((tb, ATT_DQ), lambda bi, h, i: (qb + bi * nq + i, h)),
                pl.BlockSpec((t, ATT_DQ), lambda bi, h, i: (kb + bi, h)),
                pl.BlockSpec((t, MLA_V), lambda bi, h, i: (kb + bi, h))]
    args = [q, k, v]
    if two:
        t2 = k2.shape[0] // b
        in_specs += [pl.BlockSpec((t2, ATT_DQ), lambda bi, h, i: (bi, h)),
                     pl.BlockSpec((t2, MLA_V), lambda bi, h, i: (bi, h))]
        args += [k2, v2]
    return pl.pallas_call(
        functools.partial(_attn_kernel, two=two),
        grid=(b, MLA_HEADS, nq),
        in_specs=in_specs,
        out_specs=pl.BlockSpec((tb, MLA_V), lambda bi, h, i: (bi * nq + i, h)),
        out_shape=jax.ShapeDtypeStruct((b * t, MLA_HEADS * MLA_V), jnp.bfloat16),
        compiler_params=_cparams(3),
        name="mla_attention",
    )(*args)


def _ret_tables():
    f32 = jnp.float32
    c = RET_CHUNK
    log_g = jnp.log1p(-jnp.exp2(-5.0 - jnp.arange(RET_HEADS, dtype=f32)))[:, None, None]
    idx = jnp.arange(c, dtype=f32)
    dec = jnp.exp(jnp.abs(idx[:, None] - idx[None, :])[None] * log_g)
    row = lambda e: jnp.broadcast_to(jnp.exp(e[None, :, None] * log_g), (RET_HEADS, c, c))
    return jnp.stack([dec, row(idx + 1.0), row(c - idx), row(c - 1.0 - idx), row(idx)], axis=0)


def _ret_kernel(*refs, nc, rope):
    f32, bf = jnp.float32, jnp.bfloat16
    if rope:
        q_ref, k_ref, v_ref, g_ref, qp_ref, kp_ref, cos_ref, sin_ref, tab_ref, s0_ref, o_ref, fin_ref, acc_s = refs
    else:
        q_ref, k_ref, v_ref, g_ref, tab_ref, s0_ref, o_ref, fin_ref, acc_s = refs
    c = RET_CHUNK
    dec, xif, xib, zf, zb = (tab_ref[i, 0] for i in range(5))
    g_chunk = xif[c - 1:c, :]

    def chunk(k0):
        sl = pl.ds(pl.multiple_of(k0 * c, c), c)
        qc, kc = q_ref[sl, :].astype(f32), k_ref[sl, :].astype(f32)
        if rope:
            cos, sin = cos_ref[sl, :], sin_ref[sl, :]
            qc = qc * cos + qp_ref[sl, :].astype(f32) * sin
            kc = kc * cos + kp_ref[sl, :].astype(f32) * sin
        return sl, qc.astype(bf), kc * (RET_DK ** -0.5), v_ref[sl, :]

    def state_update(s, kc, z, vc):
        kz = (kc * z).astype(bf)
        return g_chunk * s + lax.dot_general(kz, vc, (((0,), (0,)), ((), ())), preferred_element_type=f32)

    def fwd(i, s):
        sl, qc, kc, vc = chunk(i)
        sc = lax.dot_general(qc, kc.astype(bf), (((1,), (1,)), ((), ())), preferred_element_type=f32) * dec
        inner = jnp.dot(sc.astype(bf), vc, preferred_element_type=f32)
        cross = jnp.dot(qc, s.astype(bf), preferred_element_type=f32) * xif
        acc_s[sl, :] = inner + cross
        return state_update(s, kc, zf, vc)

    fin_ref[0, 0, 0] = lax.fori_loop(0, nc, fwd, s0_ref[0, 0, 0])

    def bwd(i, s):
        sl, qc, kc, vc = chunk(nc - 1 - i)
        o = _rms(acc_s[sl, :] + jnp.dot(qc, s.astype(bf), preferred_element_type=f32) * xib)
        g = g_ref[sl, :].astype(f32)
        o_ref[sl, :] = (o * (g * _sigmoid(g))).astype(o_ref.dtype)
        return state_update(s, kc, zb, vc)

    fin_ref[0, 1, 0] = lax.fori_loop(0, nc, bwd, s0_ref[0, 1, 0])


def _retention(p, row0, b, t, s0, tables, rope_tabs=None):
    nc = t // RET_CHUNK
    rb = row0 // t
    rope = rope_tabs is not None
    col = lambda c0: pl.BlockSpec((t, RET_DK), lambda bi, h: (rb + bi, c0 // RET_DK + h))
    st = pl.BlockSpec((1, 2, 1, RET_DK, RET_DV), lambda bi, h: (bi, 0, h, 0, 0))
    in_specs = [col(C_RQ), col(C_RK), col(C_RV), col(C_RG)]
    args = [p, p, p, p]
    if rope:
        tab = pl.BlockSpec((t, RET_DK), lambda bi, h: (0, 0))
        in_specs += [col(C_RQP), col(C_RKP), tab, tab]
        args += [p, p, rope_tabs[0], rope_tabs[1]]
    in_specs += [pl.BlockSpec((5, 1, RET_CHUNK, RET_CHUNK), lambda bi, h: (0, h, 0, 0)), st]
    args += [tables, s0]
    return pl.pallas_call(
        functools.partial(_ret_kernel, nc=nc, rope=rope),
        grid=(b, RET_HEADS),
        in_specs=in_specs,
        out_specs=[pl.BlockSpec((t, RET_DV), lambda bi, h: (bi, h)), st],
        out_shape=[jax.ShapeDtypeStruct((b * t, RET_HEADS * RET_DV), jnp.bfloat16),
                   jax.ShapeDtypeStruct((b, 2, RET_HEADS, RET_DK, RET_DV), jnp.float32)],
        scratch_shapes=[pltpu.VMEM((t, RET_DV), jnp.float32)],
        compiler_params=_cparams(2),
        name="retention",
    )(*args)


OUT_TM = 256
ROUTER_PAD = 128


def _pack_halves(x):
    w = x.shape[1] // 2
    bits = lambda a: lax.bitcast_convert_type(a.astype(jnp.bfloat16).astype(jnp.float32), jnp.int32)
    return bits(x[:, w:]) | lax.shift_right_logical(bits(x[:, :w]), 16)


def _unpack_halves(u):
    lo = lax.bitcast_convert_type(lax.shift_left(u, 16), jnp.float32)
    hi = lax.bitcast_convert_type(u & jnp.int32(-65536), jnp.float32)
    return jnp.concatenate([lo, hi], axis=1)


def _split_bf16(x):
    hi = x.astype(jnp.bfloat16)
    return hi, (x - hi.astype(jnp.float32)).astype(jnp.bfloat16)


def _route(h, whi_ref, wlo_ref, rb_ref):
    f32 = jnp.float32
    hi, lo = _split_bf16(h)
    dn = (((1,), (1,)), ((), ()))
    lt = (lax.dot_general(whi_ref[...], hi, dn, preferred_element_type=f32)
          + lax.dot_general(whi_ref[...], lo, dn, preferred_element_type=f32)
          + lax.dot_general(wlo_ref[...], hi, dn, preferred_element_type=f32))[:N_EXPERTS]
    m = jnp.max(lt, axis=0, keepdims=True)
    e = jnp.exp(lt - m)
    sc = e / jnp.sum(e, axis=0, keepdims=True)
    sel = sc + rb_ref[...][:N_EXPERTS, 0:1]
    rows = lambda a: [a[j:j + 1, :] for j in range(N_EXPERTS)]
    sel_r, sc_r = rows(sel), rows(sc)
    epg = EXPERTS_PER_GROUP

    def top2sum(a, b, c, d):
        h1, l1, h2, l2 = jnp.maximum(a, b), jnp.minimum(a, b), jnp.maximum(c, d), jnp.minimum(c, d)
        return jnp.maximum(h1, h2) + jnp.maximum(jnp.minimum(h1, h2), jnp.maximum(l1, l2))

    gs = [top2sum(*sel_r[g * epg:(g + 1) * epg]) for g in range(N_EXPERT_GROUPS)]
    best, gi = gs[0], jnp.zeros_like(gs[0], dtype=jnp.int32)
    for g in range(1, N_EXPERT_GROUPS):
        upd = gs[g] > best
        gi = jnp.where(upd, g, gi)
        best = jnp.where(upd, gs[g], best)

    def pick(r, j):
        out = r[j]
        for g in range(1, N_EXPERT_GROUPS):
            out = jnp.where(gi == g, r[g * epg + j], out)
        return out

    v = [pick(sel_r, j) for j in range(epg)]
    s = [pick(sc_r, j) for j in range(epg)]

    def argmax_first(vals):
        bv, bi = vals[0], jnp.zeros_like(gi)
        for j in range(1, epg):
            upd = vals[j] > bv
            bi = jnp.where(upd, j, bi)
            bv = jnp.where(upd, vals[j], bv)
        return bi

    i1 = argmax_first(v)
    neg = jnp.float32(-jnp.inf)
    i2 = argmax_first([jnp.where(i1 == j, neg, v[j]) for j in range(epg)])
    take = lambda i: sum(jnp.where(i == j, s[j], 0.0) for j in range(epg))
    w1, w2 = take(i1), take(i2)
    tot = w1 + w2
    return (jnp.concatenate([gi * epg + i1, gi * epg + i2], axis=0),
            jnp.concatenate([w1 / tot, w2 / tot], axis=0))


def _mixout_kernel(ylin_ref, u_ref, ymla_c, ymla_l, ylru_c, ylru_l, yret_c, yret_l, x_ref, mod_ref, d_ref,
                   wglu_ref, wout_ref, g2n_ref, whi_ref, wlo_ref, rb_ref, xo_ref, h2_ref, idx_ref, wgt_ref,
                   *, n_ctx_tiles):
    f32, bf = jnp.float32, jnp.bfloat16
    w = GROUP_W
    is_ctx = pl.program_id(0) < n_ctx_tiles
    pick = lambda c_ref, l_ref: jnp.where(is_ctx, c_ref[...], l_ref[...])
    y = _gelu_tanh(ylin_ref[...] + d_ref[...] * u_ref[...].astype(f32))
    y5 = (y * _sigmoid(jnp.dot(y.astype(bf), wglu_ref[...], preferred_element_type=f32))).astype(bf)
    mix = (jnp.dot(y5, wout_ref[0:w, :], preferred_element_type=f32)
           + jnp.dot(pick(ymla_c, ymla_l), wout_ref[w:2 * w, :], preferred_element_type=f32)
           + jnp.dot(pick(ylru_c, ylru_l), wout_ref[2 * w:3 * w, :], preferred_element_type=f32)
           + jnp.dot(pick(yret_c, yret_l), wout_ref[3 * w:4 * w, :], preferred_element_type=f32))
    x = x_ref[...] + mod_ref[0, 2:3, :] * mix
    xo_ref[...] = x
    h2 = _rms(x) * g2n_ref[...] * (1.0 + mod_ref[0, 4:5, :]) + mod_ref[0, 3:4, :]
    h2_ref[...] = _pack_halves(h2)
    idx, wgt = _route(h2, whi_ref, wlo_ref, rb_ref)
    idx_ref[...] = idx
    wgt_ref[...] = wgt


def _router_weights(router_w, router_b):
    d = router_w.shape[0]
    wt = jnp.zeros((ROUTER_PAD, d), jnp.float32).at[:N_EXPERTS].set(router_w.T.astype(jnp.float32))
    whi, wlo = _split_bf16(wt)
    rb = jnp.zeros((ROUTER_PAD, 128), jnp.float32).at[:N_EXPERTS].set(router_b.astype(jnp.float32)[:, None])
    return whi, wlo, rb


def _mixout(ylin, p, ymla, ylru, yret, x, mods, s5_d, wglu, wout, g2n, rw, n_ctx, dec_seq):
    n, d = x.shape
    tm = OUT_TM
    w = GROUP_W
    na = n_ctx // tm
    seq = functools.partial(_seq_of_tile, tm=tm, n_ctx=n_ctx, dec_seq=dec_seq)
    full = lambda a: pl.BlockSpec(a.shape, lambda i: (0,) * a.ndim)
    row = lambda width: pl.BlockSpec((tm, width), lambda i: (i, 0))
    ctx_row = pl.BlockSpec((tm, w), lambda i: (jnp.minimum(i, na - 1), 0))
    lat_row = pl.BlockSpec((tm, w), lambda i: (jnp.maximum(i - na, 0), 0))
    lanes = pl.BlockSpec((TOP_K, tm), lambda i: (0, i))
    whi, wlo, rb = rw
    return pl.pallas_call(
        functools.partial(_mixout_kernel, n_ctx_tiles=na),
        grid=(n // tm,),
        in_specs=[row(w), row(w), ctx_row, lat_row, ctx_row, lat_row, ctx_row, lat_row, row(d),
                  pl.BlockSpec((1, N_MOD, d), lambda i: (seq(i), 0, 0)),
                  full(s5_d), full(wglu), full(wout), full(g2n), full(whi), full(wlo), full(rb)],
        out_specs=[row(d), row(d // 2), lanes, lanes],
        out_shape=[jax.ShapeDtypeStruct((n, d), jnp.float32), jax.ShapeDtypeStruct((n, d // 2), jnp.int32),
                   jax.ShapeDtypeStruct((TOP_K, n), jnp.int32), jax.ShapeDtypeStruct((TOP_K, n), jnp.float32)],
        compiler_params=_cparams(1),
        name="mix_out_norm_route",
    )(ylin, p, *ymla, *ylru, *yret, x, mods, s5_d, wglu, wout, g2n, whi, wlo, rb)


MOE_TM = 256


def _expert_kernel(te_ref, nt_ref, nv_ref, tok0_ref, tok1_ref, w_ref, dst_ref, wg_ref, wu_ref, wd_ref, h_hbm,
                   y_hbm, wg_s, wu_s, wd_s, xbuf, ybuf, gsem, ssem):
    i = pl.program_id(0)
    bf = jnp.bfloat16
    tm = ybuf.shape[0]
    nt = nt_ref[0]
    slot = i % 2

    def gather_start(tok_ref, sl, unroll):
        def row(r, carry):
            pltpu.make_async_copy(h_hbm.at[pl.ds(tok_ref[0, 0, r], 1), :], xbuf.at[sl, pl.ds(r, 1), :],
                                  gsem.at[sl]).start()
            return carry

        lax.fori_loop(0, tm, row, 0, unroll=unroll)

    def gather_done(sl):
        pltpu.make_async_copy(h_hbm.at[pl.ds(0, tm), :], xbuf.at[sl], gsem.at[sl]).wait()

    def scatter_done(n_rows):
        n8 = pl.multiple_of((n_rows // 8) * 8, 8)

        @pl.when(n8 > 0)
        def _():
            pltpu.make_async_copy(ybuf.at[pl.ds(0, n8), :], y_hbm.at[pl.ds(0, n8), :], ssem).wait()

        def one(r, carry):
            pltpu.make_async_copy(ybuf.at[pl.ds(0, 1), :], y_hbm.at[pl.ds(0, 1), :], ssem).wait()
            return carry

        lax.fori_loop(0, n_rows - n8, one, 0)

    @pl.when(i < nt)
    def _():
        @pl.when(i == 0)
        def _():
            gather_start(tok0_ref, 0, 8)

        @pl.when((i == 0) | (te_ref[i] != te_ref[jnp.maximum(i - 1, 0)]))
        def _():
            wg_s[...] = wg_ref[0, 0].astype(bf)
            wu_s[...] = wu_ref[0, 0].astype(bf)
            wd_s[...] = wd_ref[0, 0].astype(bf)

        gather_done(slot)
        gather_start(tok1_ref, 1 - slot, True)
        x = _unpack_halves(xbuf[slot]).astype(bf)
        g = jnp.dot(x, wg_s[...], preferred_element_type=jnp.float32)
        u = jnp.dot(x, wu_s[...], preferred_element_type=jnp.float32)
        act = ((g * _sigmoid(g)) * u * w_ref[...]).astype(bf)
        y = jnp.dot(act, wd_s[...], preferred_element_type=jnp.float32)

        @pl.when(i > 0)
        def _():
            scatter_done(nv_ref[jnp.maximum(i - 1, 0)])

        ybuf[...] = _pack_halves(y)

        def row(r, carry):
            pltpu.make_async_copy(ybuf.at[pl.ds(r, 1), :], y_hbm.at[pl.ds(dst_ref[0, 0, r], 1), :], ssem).start()
            return carry

        lax.fori_loop(0, nv_ref[i], row, 0)

        @pl.when(i == nt - 1)
        def _():
            scatter_done(nv_ref[i])
            gather_done(1 - slot)


def _experts(h2p, layer, ws, tok, dst, tile_expert, n_tiles_used, n_valid, wg, wu, wd):
    n, dh = h2p.shape
    d = 2 * dh
    tm = MOE_TM
    n_tiles = tok.shape[0]
    f = wg.shape[-1]
    smem_row = lambda nxt: pl.BlockSpec(
        (1, 1, tm), lambda i, te, nt, nv: (jnp.minimum(i + nxt, n_tiles - 1), 0, 0), memory_space=pltpu.SMEM)
    wspec = lambda a, b: pl.BlockSpec((1, 1, a, b), lambda i, te, nt, nv: (layer, te[i], 0, 0))
    return pl.pallas_call(
        _expert_kernel,
        grid_spec=pltpu.PrefetchScalarGridSpec(
            num_scalar_prefetch=3,
            grid=(n_tiles,),
            in_specs=[smem_row(0), smem_row(1),
                      pl.BlockSpec((tm, 1), lambda i, te, nt, nv: (i, 0)),
                      smem_row(0),
                      wspec(d, f), wspec(d, f), wspec(f, d),
                      pl.BlockSpec(memory_space=pl.ANY)],
            out_specs=pl.BlockSpec(memory_space=pl.ANY),
            scratch_shapes=[pltpu.VMEM((d, f), jnp.bfloat16), pltpu.VMEM((d, f), jnp.bfloat16),
                            pltpu.VMEM((f, d), jnp.bfloat16), pltpu.VMEM((2, tm, dh), jnp.int32),
                            pltpu.VMEM((tm, dh), jnp.int32), pltpu.SemaphoreType.DMA((2,)),
                            pltpu.SemaphoreType.DMA(())]),
        out_shape=jax.ShapeDtypeStruct((TOP_K * n, dh), jnp.int32),
        compiler_params=pltpu.CompilerParams(dimension_semantics=("arbitrary",),
                                             vmem_limit_bytes=V7X_VMEM_LIMIT_BIG, disable_bounds_checks=True),
        name="moe_experts",
    )(tile_expert, n_tiles_used, n_valid, tok, tok, ws, dst, wg, wu, wd, h2p)


def _moe_dispatch(idx, wgt):
    n = idx.shape[1]
    tm = MOE_TM
    n_pairs = TOP_K * n
    m_pad = n_pairs + N_EXPERTS * tm
    n_tiles = m_pad // tm
    e_flat = idx.reshape(-1)
    order = jnp.argsort(e_flat, stable=True).astype(jnp.int32)
    experts = jnp.arange(N_EXPERTS, dtype=jnp.int32)
    counts = jnp.sum((e_flat[None, :] == experts[:, None]).astype(jnp.int32), axis=1)
    starts_raw = jnp.cumsum(counts) - counts
    padded = ((counts + tm - 1) // tm) * tm
    ends = jnp.cumsum(padded)
    starts_pad = ends - padded
    tile_start = jnp.arange(n_tiles, dtype=jnp.int32) * tm
    tile_expert = jnp.minimum(jnp.sum((tile_start[:, None] >= ends[None, :]).astype(jnp.int32), axis=1),
                              N_EXPERTS - 1)
    r = jnp.arange(m_pad, dtype=jnp.int32)
    te_r = jnp.repeat(tile_expert, tm)
    off = r - jnp.take(starts_pad, te_r)
    valid = (off >= 0) & (off < jnp.take(counts, te_r))
    pair = jnp.take(order, jnp.clip(jnp.take(starts_raw, te_r) + off, 0, n_pairs - 1))
    tok = jnp.where(valid, pair % n, 0).reshape(n_tiles, 1, tm)
    dst = jnp.where(valid, pair, 0).reshape(n_tiles, 1, tm)
    ws = jnp.where(valid, jnp.take(wgt.reshape(-1), pair), 0.0)[:, None]
    n_tiles_used = (ends[-1] // tm).astype(jnp.int32).reshape(1)
    n_valid = jnp.sum(valid.reshape(n_tiles, tm), axis=1).astype(jnp.int32)
    return tok, dst, ws, tile_expert, n_tiles_used, n_valid


def _moe_rows(h2p, layer, idx, wgt, wg, wu, wd):
    tok, dst, ws, tile_expert, n_tiles_used, n_valid = _moe_dispatch(idx, wgt)
    return _experts(h2p, layer, ws, tok, dst, tile_expert, n_tiles_used, n_valid, wg, wu, wd)


RES_TM = 512


def _resid_kernel(x_ref, ya_ref, yb_ref, mod_ref, gf_ref, o_ref, *, final):
    x = x_ref[...] + mod_ref[0, 5:6, :] * (_unpack_halves(ya_ref[...]) + _unpack_halves(yb_ref[...]))
    o_ref[...] = _rms(x) * gf_ref[...] if final else x


def _resid(x, y2, mods, gf, n_ctx, dec_seq, final):
    n, d = x.shape
    tm = RES_TM
    seq = functools.partial(_seq_of_tile, tm=tm, n_ctx=n_ctx, dec_seq=dec_seq)
    row = pl.BlockSpec((tm, d), lambda i: (i, 0))
    return pl.pallas_call(
        functools.partial(_resid_kernel, final=final),
        grid=(n // tm,),
        in_specs=[row, pl.BlockSpec((tm, d // 2), lambda i: (i, 0)),
                  pl.BlockSpec((tm, d // 2), lambda i: (n // tm + i, 0)),
                  pl.BlockSpec((1, N_MOD, d), lambda i: (seq(i), 0, 0)),
                  pl.BlockSpec((1, d), lambda i: (0, 0))],
        out_specs=row,
        out_shape=jax.ShapeDtypeStruct((n, d), jnp.float32),
        compiler_params=_cparams(1),
        name="moe_residual_norm",
    )(x, y2, y2, mods, gf)


def kernel(x_prompt, x_sample, c, cache_mla_ckv, cache_mla_kpe, state_s5, state_lru, state_ret,
           c_ctx, w_ada, b_ada, norm1_g, norm2_g, w_in, w_out,
           s5_a_re, s5_a_im, s5_log_dt, s5_b_re, s5_b_im, s5_c_re, s5_c_im, s5_d, s5_w_glu,
           mla_q_norm_g, mla_w_uq, mla_kv_norm_g, mla_w_ukv,
           lru_conv_w, lru_conv_b, lru_w_a, lru_b_a, lru_w_x, lru_b_x, lru_lambda,
           router_w, router_b, moe_w_gate, moe_w_up, moe_w_down, final_norm_g):
    f32, bf = jnp.float32, jnp.bfloat16
    batch, seq_len, d = x_prompt.shape
    dec_batch, dec_seq, _ = x_sample.shape
    past = cache_mla_ckv.shape[2]
    n_ctx, n_lat = batch * seq_len, dec_batch * dec_seq
    depth = w_in.shape[0]

    x = jnp.concatenate([x_prompt.reshape(n_ctx, d), x_sample.reshape(n_lat, d)], axis=0)
    cvec = jnp.zeros((8, d), f32).at[0].set(c_ctx).at[1:1 + dec_batch].set(c)
    mods_all = _ada(cvec, w_ada, b_ada).reshape(depth, 8, N_MOD, d)

    cos64, sin64 = _rope_tables(dec_seq, MLA_ROPE)
    lat_tab = jnp.tile(jnp.concatenate([cos64, sin64], axis=1), (dec_batch, 1))
    ctx_tab = jnp.concatenate([jnp.ones((n_ctx, MLA_ROPE), f32), jnp.zeros((n_ctx, MLA_ROPE), f32)], axis=1)
    mla_tab = jnp.concatenate([ctx_tab, lat_tab], axis=0)
    ret_rope = _rope_tables(dec_seq, RET_DK)
    ret_tables = _ret_tables()
    rw = _router_weights(router_w, router_b)
    gf = final_norm_g[None, :]

    states = []
    for l in range(depth):
        mods = mods_all[l]
        s5_tab = _s5_tables(s5_a_re[l], s5_a_im[l], s5_log_dt[l], s5_b_re[l], s5_b_im[l], s5_c_re[l], s5_c_im[l])
        lru_tab = _lru_tables(lru_conv_w[l], lru_conv_b[l], lru_w_a[l], lru_b_a[l], lru_w_x[l], lru_b_x[l],
                              lru_lambda[l])
        wuq, wukv = _mla_weights(mla_w_uq[l], mla_w_ukv[l])

        p = _inproj(x, mods, norm1_g[l][None, :], _inproj_weights(w_in[l]), n_ctx, dec_seq)

        u_ctx = p[:n_ctx, C_S5:C_S5 + S5_CH].reshape(batch, seq_len, S5_CH)
        u_lat = p[n_ctx:, C_S5:C_S5 + S5_CH].reshape(dec_batch, dec_seq, S5_CH)
        ylin, s5_fins = _s5_core([u_ctx, u_lat], [None, state_s5[:, l]], s5_tab)

        qo, ko, vo, ckv = _mla_prep(p, mla_tab, mla_q_norm_g[l][None, :], mla_kv_norm_g[l][None, :], wuq, wukv)
        ckv_c = cache_mla_ckv[:, l].reshape(dec_batch * past, MLA_KV_RANK)
        kv_c = _matmul(ckv_c, wukv, out_dtype=bf).reshape(dec_batch * past, MLA_HEADS, 2, 128)
        kpe_c = jnp.broadcast_to(cache_mla_kpe[:, l].reshape(dec_batch * past, 1, MLA_ROPE).astype(bf),
                                 (dec_batch * past, MLA_HEADS, MLA_ROPE))
        k_c = jnp.concatenate([kv_c[:, :, 0], kpe_c, jnp.zeros_like(kpe_c)], axis=-1).reshape(
            dec_batch * past, MLA_HEADS * ATT_DQ)
        v_c = kv_c[:, :, 1].reshape(dec_batch * past, MLA_HEADS * MLA_V)
        ymla = (_attention(qo, ko, vo, 0, batch, seq_len),
                _attention(qo, ko, vo, n_ctx, dec_batch, dec_seq, k_c, v_c))

        ylru_c, lru_fin = _lru(p, 0, batch, seq_len, jnp.zeros((batch, 2, LRU_W), f32), lru_tab)
        ylru_l, _ = _lru(p, n_ctx, dec_batch, dec_seq, state_lru[:, l].astype(f32), lru_tab)
        ylru = (ylru_c, ylru_l)

        yret_c, ret_fin = _retention(p, 0, batch, seq_len,
                                     jnp.zeros((batch, 2, RET_HEADS, RET_DK, RET_DV), f32), ret_tables)
        yret_l, _ = _retention(p, n_ctx, dec_batch, dec_seq, state_ret[:, l].astype(f32), ret_tables, ret_rope)
        yret = (yret_c, yret_l)

        x_mid, h2, idx, wgt = _mixout(ylin, p, ymla, ylru, yret, x, mods, s5_d[l][None, :],
                                      s5_w_glu[l].astype(bf), w_out[l].astype(bf), norm2_g[l][None, :], rw,
                                      n_ctx, dec_seq)
        y2 = _moe_rows(h2, l, idx, wgt, moe_w_gate, moe_w_up, moe_w_down)
        x = _resid(x_mid, y2, mods, gf, n_ctx, dec_seq, final=(l == depth - 1))

        states.append((ckv[:n_ctx].reshape(batch, seq_len, MLA_KV_RANK),
                       p[:n_ctx, C_KR:C_KR + MLA_ROPE].astype(f32).reshape(batch, seq_len, MLA_ROPE),
                       s5_fins[0], lru_fin, ret_fin))

    y_prompt = x[:n_ctx].reshape(batch, seq_len, d)
    y_sample = x[n_ctx:].reshape(dec_batch, dec_seq, d)
    new_cache_mla_ckv = jnp.stack([st[0] for st in states], axis=1)
    new_cache_mla_kpe = jnp.stack([st[1] for st in states], axis=1)
    new_state_s5 = jnp.stack([st[2] for st in states], axis=1)
    new_state_lru = jnp.stack([st[3] for st in states], axis=1)
    new_state_ret = jnp.stack([st[4] for st in states], axis=1)
    return (y_prompt, y_sample, new_cache_mla_ckv, new_cache_mla_kpe, new_state_s5, new_state_lru, new_state_ret)
```

```python
import functools
import numpy as np
import jax
import jax.numpy as jnp
from jax import lax
from jax.experimental import pallas as pl
from jax.experimental.pallas import tpu as pltpu

D_MODEL = 2048
DEPTH = 2
GRID_W = 64
EPS = 1e-6
ROPE_BASE = 10000.0
N_MOD = 6
GROUP_W = 512
S5_CH = GROUP_W
S5_GROUP_CH = 16
S5_GROUPS = S5_CH // S5_GROUP_CH
S5_STATE = 64
MLA_HEADS = 4
MLA_NOPE = 128
MLA_ROPE = 64
MLA_V = 128
MLA_Q_RANK = GROUP_W
MLA_KV_RANK = GROUP_W // 2
MLA_SCALE = (MLA_NOPE + MLA_ROPE) ** -0.5
LRU_W = GROUP_W
LRU_CONV = 4
LRU_C = 8.0
RET_HEADS = 4
RET_DK = 128
RET_DV = 128
RET_CHUNK = 128
N_EXPERTS = 16
N_EXPERT_GROUPS = 4
EXPERTS_PER_GROUP = N_EXPERTS // N_EXPERT_GROUPS
TOP_K = 2
D_EXPERT = D_MODEL // 4

V7X_VMEM_LIMIT = 48 * 1024 * 1024
V7X_VMEM_LIMIT_BIG = 56 * 1024 * 1024

C_S5, C_MQ, C_LX, C_LG, C_RQ, C_RK, C_RV, C_RG, C_RQP, C_RKP, C_MKV, C_KR = (
    0, 512, 1024, 1536, 2048, 2560, 3072, 3584, 4096, 4608, 5120, 5376)
P_COLS = 5632
R_S5, R_MQ, R_MKV, R_KR, R_LX, R_LG, R_RQ, R_RK, R_RV, R_RG = (0, 512, 1024, 1280, 1344, 1856, 2368, 2880, 3392, 3904)


def _cparams(n_axes, limit=V7X_VMEM_LIMIT):
    return pltpu.CompilerParams(dimension_semantics=("arbitrary",) * n_axes, vmem_limit_bytes=limit)


def _sigmoid(x):
    return 1.0 / (1.0 + jnp.exp(-x))


def _gelu_tanh(x):
    return 0.5 * x * (1.0 + jnp.tanh(0.7978845608028654 * (x + 0.044715 * (x * x * x))))


def _rms(x):
    return x * lax.rsqrt(jnp.mean(x * x, axis=-1, keepdims=True) + EPS)


def _seq_of_tile(i, tm, n_ctx, dec_seq):
    assert n_ctx % tm == 0 and dec_seq % tm == 0, "a row tile must not straddle two sequences"
    r = i * tm
    return jnp.where(r < n_ctx, 0, 1 + (r - n_ctx) // dec_seq)


def _mm_kernel(x_ref, w_ref, o_ref):
    o_ref[...] = jnp.dot(x_ref[...].astype(jnp.bfloat16), w_ref[...].astype(jnp.bfloat16),
                         preferred_element_type=jnp.float32).astype(o_ref.dtype)


def _matmul(x, w, tm=512, tn=512, out_dtype=jnp.float32):
    m, k = x.shape
    _, n = w.shape
    tm, tn = min(tm, m), min(tn, n)
    return pl.pallas_call(
        _mm_kernel,
        grid=(n // tn, m // tm),
        in_specs=[pl.BlockSpec((tm, k), lambda j, i: (i, 0)), pl.BlockSpec((k, tn), lambda j, i: (0, j))],
        out_specs=pl.BlockSpec((tm, tn), lambda j, i: (i, j)),
        out_shape=jax.ShapeDtypeStruct((m, n), out_dtype),
        compiler_params=_cparams(2),
        name="matmul",
    )(x, w)


ADA_TN = 1024


def _ada_kernel(c_ref, w_ref, b_ref, o_ref):
    c = c_ref[...]
    s = (c * _sigmoid(c)).astype(jnp.bfloat16)
    o_ref[0] = jnp.dot(s, w_ref[0].astype(jnp.bfloat16), preferred_element_type=jnp.float32) + b_ref[0]


def _ada(cvec, w_ada, b_ada):
    depth, d, n = w_ada.shape
    return pl.pallas_call(
        _ada_kernel,
        grid=(depth, n // ADA_TN),
        in_specs=[pl.BlockSpec((8, d), lambda l, j: (0, 0)),
                  pl.BlockSpec((1, d, ADA_TN), lambda l, j: (l, 0, j)),
                  pl.BlockSpec((1, 1, ADA_TN), lambda l, j: (l, 0, j))],
        out_specs=pl.BlockSpec((1, 8, ADA_TN), lambda l, j: (l, 0, j)),
        out_shape=jax.ShapeDtypeStruct((depth, 8, n), jnp.float32),
        compiler_params=_cparams(2),
        name="adaln_mod",
    )(cvec, w_ada, b_ada.reshape(depth, 1, n))


IN_TM = 1024
IN_TN = 512


def _inproj_kernel(x_ref, mod_ref, g_ref, w_ref, o_ref, h_s):
    @pl.when(pl.program_id(1) == 0)
    def _():
        h = _rms(x_ref[...]) * g_ref[...]
        h_s[...] = (h * (1.0 + mod_ref[0, 1:2, :]) + mod_ref[0, 0:1, :]).astype(h_s.dtype)

    o_ref[...] = jnp.dot(h_s[...], w_ref[...], preferred_element_type=jnp.float32).astype(o_ref.dtype)


def _inproj(x, mods, g, w, n_ctx, dec_seq):
    n, d = x.shape
    seq = functools.partial(_seq_of_tile, tm=IN_TM, n_ctx=n_ctx, dec_seq=dec_seq)
    return pl.pallas_call(
        _inproj_kernel,
        grid=(n // IN_TM, P_COLS // IN_TN),
        in_specs=[pl.BlockSpec((IN_TM, d), lambda i, j: (i, 0)),
                  pl.BlockSpec((1, N_MOD, d), lambda i, j: (seq(i), 0, 0)),
                  pl.BlockSpec((1, d), lambda i, j: (0, 0)),
                  pl.BlockSpec((d, IN_TN), lambda i, j: (0, j))],
        out_specs=pl.BlockSpec((IN_TM, IN_TN), lambda i, j: (i, j)),
        out_shape=jax.ShapeDtypeStruct((n, P_COLS), jnp.bfloat16),
        scratch_shapes=[pltpu.VMEM((IN_TM, d), jnp.bfloat16)],
        compiler_params=_cparams(2),
        name="norm_inproj",
    )(x, mods, g, w)


def _rot_partner_cols(w, n_heads, head_dim):
    q = head_dim // 4
    wr = w.reshape(w.shape[0], n_heads, 2, 2, q)
    return jnp.stack([-wr[:, :, :, 1], wr[:, :, :, 0]], axis=3).reshape(w.shape)


def _inproj_weights(w_in):
    part = lambda off, width: w_in[:, off:off + width]
    rq, rk, kr = part(R_RQ, 512), part(R_RK, 512), part(R_KR, MLA_ROPE)
    cols = [part(R_S5, 512), part(R_MQ, 512), part(R_LX, 512), part(R_LG, 512), rq, rk, part(R_RV, 512),
            part(R_RG, 512), _rot_partner_cols(rq, RET_HEADS, RET_DK), _rot_partner_cols(rk, RET_HEADS, RET_DK),
            part(R_MKV, MLA_KV_RANK), kr, _rot_partner_cols(kr, 1, MLA_ROPE)]
    w = jnp.concatenate(cols, axis=1)
    return jnp.pad(w, ((0, 0), (0, P_COLS - w.shape[1]))).astype(jnp.bfloat16)


def _rope_tables(t_len, rot_dim):
    rows = t_len // GRID_W
    row = jnp.repeat(jnp.arange(rows, dtype=jnp.float32), GRID_W)
    col = jnp.tile(jnp.arange(GRID_W, dtype=jnp.float32), rows)
    n_freq = rot_dim // 4
    inv = ROPE_BASE ** (-jnp.arange(n_freq, dtype=jnp.float32) / n_freq)
    ang = jnp.concatenate([row[:, None] * inv[None]] * 2 + [col[:, None] * inv[None]] * 2, axis=1)
    return jnp.cos(ang), jnp.sin(ang)


S5_L = 32
S5_PAIRS = S5_GROUPS // 2
S5_Q = 4
S5_LANES = S5_GROUPS * S5_STATE
S5_SCAN_LANES = 1024


def _s5_tables(a_re, a_im, log_dt, b_re, b_im, c_re, c_im):
    f32 = jnp.float32
    L, G, P, C = S5_L, S5_GROUPS, S5_STATE, S5_GROUP_CH
    hp = lax.Precision.HIGHEST
    cmul = lambda xr, xi, yr, yi: (xr * yr - xi * yi, xr * yi + xi * yr)
    dt = jnp.exp(log_dt.astype(f32))[..., None]
    zr, zi = a_re * dt, a_im * dt
    ab_r, ab_i = jnp.exp(zr) * jnp.cos(zi), jnp.exp(zr) * jnp.sin(zi)
    den = a_re * a_re + a_im * a_im
    nr, ni = ab_r - 1.0, ab_i
    be_r = (nr * a_re + ni * a_im) / den
    be_i = (ni * a_re - nr * a_im) / den
    bt_r, bt_i = b_re.transpose(0, 2, 1), b_im.transpose(0, 2, 1)
    bp_r, bp_i = cmul(be_r[:, :, None, :], be_i[:, :, None, :], bt_r[None], bt_i[None])
    tau = jnp.arange(L + 1, dtype=f32)[None, None, :, None]
    mag = jnp.exp(zr[:, :, None, :] * tau)
    pw_r, pw_i = mag * jnp.cos(zi[:, :, None, :] * tau), mag * jnp.sin(zi[:, :, None, :] * tau)

    pws = lambda d, rev: tuple(x[d][:, :L][:, ::-1] if rev else x[d][:, :L] for x in (pw_r, pw_i))
    coef = lambda d, rev: cmul(*(x[:, :, None, :] for x in pws(d, rev)), bp_r[d][:, None], bp_i[d][:, None])
    ws = jnp.stack(coef(0, True) + coef(1, False), axis=3).reshape(G, L * C, S5_Q * P)

    ct_r, ct_i = c_re.transpose(0, 2, 1), c_im.transpose(0, 2, 1)
    pt_r, pt_i = pw_r.transpose(0, 1, 3, 2), pw_i.transpose(0, 1, 3, 2)

    def cm(er, ei):
        m_r, m_i = cmul(ct_r[:, :, None, :], ct_i[:, :, None, :], er[..., None], ei[..., None])
        return m_r.reshape(G, P, L * C), m_i.reshape(G, P, L * C)

    mf_r, mf_i = cm(pt_r[0][:, :, 1:L + 1], pt_i[0][:, :, 1:L + 1])
    mb_r, mb_i = cm(pt_r[1][:, :, 1:L + 1][:, :, ::-1], pt_i[1][:, :, 1:L + 1][:, :, ::-1])
    wo = jnp.stack([mf_r, -mf_i, mb_r, -mb_i], axis=1).reshape(G, S5_Q * P, L * C)

    def impulse(d):
        m_r, m_i = cm(pt_r[d][:, :, :L], pt_i[d][:, :, :L])
        kk = (jnp.einsum('gkp,gpx->gkx', bp_r[d], m_r, precision=hp)
              - jnp.einsum('gkp,gpx->gkx', bp_i[d], m_i, precision=hp))
        return kk.reshape(G, C, L, C)

    kf, kb = impulse(0), impulse(1)
    wide = jnp.concatenate([kb[:, :, :0:-1], kf[:, :, :1] + kb[:, :, :1], kf[:, :, 1:]], axis=2)
    wide = jnp.pad(wide.reshape(G, C, (2 * L - 1) * C), ((0, 0), (0, 0), (0, C)))

    a_l = jnp.stack([pw_r[0, :, L], pw_i[0, :, L], pw_r[1, :, L], pw_i[1, :, L]], axis=0).reshape(S5_Q, 1, S5_LANES)
    bf = jnp.bfloat16
    return _s5_toeplitz(wide), ws.astype(bf), wo.astype(bf), a_l


def _s5_toeplitz_kernel(w_ref, o_ref):
    w = w_ref[0]
    for s in range(S5_L):
        off = (S5_L - 1 - s) * S5_GROUP_CH
        o_ref[0, s * S5_GROUP_CH:(s + 1) * S5_GROUP_CH, :] = w[:, off:off + S5_L * S5_GROUP_CH].astype(o_ref.dtype)


def _s5_toeplitz(wide):
    g, c, wl = wide.shape
    n = S5_L * S5_GROUP_CH
    return pl.pallas_call(
        _s5_toeplitz_kernel,
        grid=(g,),
        in_specs=[pl.BlockSpec((1, c, wl), lambda i: (i, 0, 0))],
        out_specs=pl.BlockSpec((1, n, n), lambda i: (i, 0, 0)),
        out_shape=jax.ShapeDtypeStruct((g, n, n), jnp.bfloat16),
        compiler_params=_cparams(1),
        name="s5_toeplitz",
    )(wide)


def _s5a_kernel(x_ref, t_ref, ws_ref, y1_ref, s_ref):
    ss = []
    for j in range(2):
        x = x_ref[j]
        y1_ref[j] = jnp.dot(x, t_ref[j], preferred_element_type=jnp.float32)
        ss.append(jnp.dot(x, ws_ref[j], preferred_element_type=jnp.float32))
    p = S5_STATE
    for q in range(S5_Q):
        s_ref[q] = jnp.concatenate([ss[0][:, q * p:(q + 1) * p], ss[1][:, q * p:(q + 1) * p]], axis=1)


def _s5a(x, toe, ws):
    g, r, w = x.shape
    return pl.pallas_call(
        _s5a_kernel,
        grid=(S5_PAIRS,),
        in_specs=[pl.BlockSpec((2, r, w), lambda i: (i, 0, 0)),
                  pl.BlockSpec((2, w, w), lambda i: (i, 0, 0)),
                  pl.BlockSpec((2, w, S5_Q * S5_STATE), lambda i: (i, 0, 0))],
        out_specs=[pl.BlockSpec((2, r, w), lambda i: (i, 0, 0)),
                   pl.BlockSpec((S5_Q, r, 128), lambda i: (0, 0, i))],
        out_shape=[jax.ShapeDtypeStruct((g, r, w), jnp.float32),
                   jax.ShapeDtypeStruct((S5_Q, r, S5_LANES), jnp.float32)],
        compiler_params=_cparams(1),
        name="s5_chunk_local",
    )(x, toe, ws)


def _s5b_kernel(s_ref, a_ref, h0_ref, hp_ref, fin_ref, *, nc):
    def run(qr, qi, order_fwd):
        ar, ai = a_ref[qr], a_ref[qi]

        def body(i, carry):
            hr, hi = carry
            k = i if order_fwd else nc - 1 - i
            hp_ref[qr, pl.ds(k, 1), :] = hr
            hp_ref[qi, pl.ds(k, 1), :] = hi
            nr = ar * hr - ai * hi + s_ref[qr, pl.ds(k, 1), :]
            ni = ar * hi + ai * hr + s_ref[qi, pl.ds(k, 1), :]
            return nr, ni

        hr, hi = lax.fori_loop(0, nc, body, (h0_ref[qr], h0_ref[qi]))
        fin_ref[qr] = hr
        fin_ref[qi] = hi

    run(0, 1, True)
    run(2, 3, False)


def _s5b(s, a_l, h0):
    _, nc, w = s.shape
    tl = S5_SCAN_LANES
    vec = pl.BlockSpec((S5_Q, 1, tl), lambda i: (0, 0, i))
    seq = pl.BlockSpec((S5_Q, nc, tl), lambda i: (0, 0, i))
    return pl.pallas_call(
        functools.partial(_s5b_kernel, nc=nc),
        grid=(w // tl,),
        in_specs=[seq, vec, vec],
        out_specs=[seq, vec],
        out_shape=[jax.ShapeDtypeStruct((S5_Q, nc, w), jnp.float32),
                   jax.ShapeDtypeStruct((S5_Q, 1, w), jnp.float32)],
        compiler_params=_cparams(1),
        name="s5_chunk_scan",
    )(s, a_l, h0)


def _s5c_kernel(h_ref, wo_ref, y1_ref, y_ref):
    p = S5_STATE
    for j in range(2):
        hcat = jnp.concatenate([h_ref[q][:, j * p:(j + 1) * p] for q in range(S5_Q)], axis=1)
        y2 = jnp.dot(hcat.astype(jnp.bfloat16), wo_ref[j], preferred_element_type=jnp.float32)
        y_ref[j] = (y1_ref[j] + y2).astype(y_ref.dtype)


def _s5c(hprev, wo, y1):
    g, r, w = y1.shape
    return pl.pallas_call(
        _s5c_kernel,
        grid=(S5_PAIRS,),
        in_specs=[pl.BlockSpec((S5_Q, r, 128), lambda i: (0, 0, i)),
                  pl.BlockSpec((2, S5_Q * S5_STATE, w), lambda i: (i, 0, 0)),
                  pl.BlockSpec((2, r, w), lambda i: (i, 0, 0))],
        out_specs=pl.BlockSpec((2, r, w), lambda i: (i, 0, 0)),
        out_shape=jax.ShapeDtypeStruct((g, r, w), jnp.bfloat16),
        compiler_params=_cparams(1),
        name="s5_state_to_out",
    )(hprev, wo, y1)


def _s5_to_chunks(u):
    b, t, _ = u.shape
    nc = t // S5_L
    x = u.reshape(b, nc, S5_L, S5_GROUPS, S5_GROUP_CH).transpose(3, 1, 0, 2, 4)
    return x.reshape(S5_GROUPS, nc * b, S5_L * S5_GROUP_CH)


def _s5_from_chunks(y, b, t):
    nc = t // S5_L
    y = y.reshape(S5_GROUPS, nc, b, S5_L, S5_GROUP_CH).transpose(2, 1, 3, 0, 4)
    return y.reshape(b * t, S5_CH)


def _s5_state_planes(h0):
    b = h0.shape[0]
    return h0.transpose(1, 4, 0, 2, 3).reshape(S5_Q, b, S5_LANES)


def _s5_core(us, h0s, tables):
    toe, ws_pair, wo_pair, a_l = tables
    xs = [_s5_to_chunks(u) for u in us]
    rows = [x.shape[1] for x in xs]
    y1, s = _s5a(jnp.concatenate(xs, axis=1), toe, ws_pair)
    hps, fins = [], []
    off = 0
    for u, h0, r in zip(us, h0s, rows):
        b, t, _ = u.shape
        nc = t // S5_L
        if h0 is None:
            h0p = jnp.zeros((S5_Q, 1, b * S5_LANES), jnp.float32)
        else:
            h0p = _s5_state_planes(h0.astype(jnp.float32)).reshape(S5_Q, 1, b * S5_LANES)
        hp, fin = _s5b(s[:, off:off + r].reshape(S5_Q, nc, b * S5_LANES), jnp.tile(a_l, (1, 1, b)), h0p)
        hps.append(hp.reshape(S5_Q, r, S5_LANES))
        fins.append(fin.reshape(2, 2, b, S5_GROUPS, S5_STATE).transpose(2, 0, 3, 4, 1))
        off += r
    y = _s5c(jnp.concatenate(hps, axis=1), wo_pair, y1)
    outs, off = [], 0
    for u, r in zip(us, rows):
        b, t, _ = u.shape
        outs.append(_s5_from_chunks(y[:, off:off + r], b, t))
        off += r
    return jnp.concatenate(outs, axis=0), fins


LRU_TC = 256
LRU_HALO = 16


def _lru_kernel(x_ref, gate_ref, cw_ref, cb_ref, wg_ref, bg_ref, sp_ref, h0_ref, out_ref, fin_ref,
                a_s, b_s, hf_s, *, t_len, tc):
    f32 = jnp.float32
    nt = t_len // tc
    w = LRU_W
    nb = tc // 8
    row = lax.broadcasted_iota(jnp.int32, (nb, 8, w), 1)

    def gates(c, d):
        r0 = pl.multiple_of(c * tc, tc)
        lo = pl.multiple_of(jnp.maximum(r0 - LRU_HALO, 0), LRU_HALO)
        hi = pl.multiple_of(jnp.minimum(r0 + tc, t_len - LRU_HALO), LRU_HALO)
        prev = jnp.where(c > 0, x_ref[pl.ds(lo, LRU_HALO), :].astype(f32), 0.0)
        nxt = jnp.where(c < nt - 1, x_ref[pl.ds(hi, LRU_HALO), :].astype(f32), 0.0)
        slab = jnp.concatenate([prev, x_ref[pl.ds(r0, tc), :].astype(f32), nxt], axis=0)
        o = LRU_HALO - LRU_CONV // 2
        xc = cb_ref[...] + sum(cw_ref[k:k + 1, :] * slab[o + k:o + k + tc] for k in range(LRU_CONV))
        g = jnp.dot(xc.astype(jnp.bfloat16), wg_ref[d], preferred_element_type=f32) + bg_ref[d]
        r = _sigmoid(g[:, :w])
        i = _sigmoid(g[:, w:])
        log_a = -sp_ref[d] * r
        a = jnp.exp(log_a).reshape(nb, 8, w)
        b = (jnp.sqrt(1.0 - jnp.exp(2.0 * log_a)) * (i * xc)).reshape(nb, 8, w)
        for sh in (1, 2, 4):
            keep, rot = (row >= sh, sh) if d == 0 else (row < 8 - sh, 8 - sh)
            ap = jnp.where(keep, pltpu.roll(a, rot, 1), 1.0)
            bp = jnp.where(keep, pltpu.roll(b, rot, 1), 0.0)
            b = a * bp + b
            a = a * ap
        a_s[...] = a
        b_s[...] = b
        return r0

    def fwd_chunk(c, h):
        r0 = gates(c, 0)

        def block(k, h):
            h8 = a_s[k] * h + b_s[k]
            hf_s[pl.ds(pl.multiple_of(r0 + k * 8, 8), 8), :] = h8
            return h8[7:8, :]

        return lax.fori_loop(0, nb, block, h, unroll=4)

    h = lax.fori_loop(0, nt, fwd_chunk, h0_ref[0, 0:1, :])
    fin_ref[0, 0:1, :] = h

    def bwd_chunk(ci, h):
        r0 = gates(nt - 1 - ci, 1)

        def block(i, h):
            k = nb - 1 - i
            h8 = a_s[k] * h + b_s[k]
            rows = pl.ds(pl.multiple_of(r0 + k * 8, 8), 8)
            hf_s[rows, :] = hf_s[rows, :] + h8
            return h8[0:1, :]

        h = lax.fori_loop(0, nb, block, h, unroll=4)
        sl = pl.ds(r0, tc)
        out_ref[sl, :] = (hf_s[sl, :] * _gelu_tanh(gate_ref[sl, :].astype(f32))).astype(out_ref.dtype)
        return h

    h = lax.fori_loop(0, nt, bwd_chunk, h0_ref[0, 1:2, :])
    fin_ref[0, 1:2, :] = h


def _block_diag(wb):
    n, k, j = wb.shape
    return (wb[:, :, None, :] * jnp.eye(n, dtype=wb.dtype)[:, None, :, None]).reshape(n * k, n * j)


def _lru_tables(conv_w, conv_b, w_a, b_a, w_x, b_x, lam):
    wg = jnp.stack([jnp.concatenate([_block_diag(w_a[d]), _block_diag(w_x[d])], axis=1) for d in range(2)])
    bg = jnp.concatenate([b_a, b_x], axis=-1)[:, None, :]
    sp = (LRU_C * jax.nn.softplus(-lam.astype(jnp.float32)))[:, None, :]
    return conv_w, conv_b[None, :], wg.astype(jnp.bfloat16), bg, sp


def _lru(p, row0, b, t, h0, tables):
    w = LRU_W
    tc = min(LRU_TC, t)
    cw, cb, wg, bg, sp = tables
    full = lambda a: pl.BlockSpec(a.shape, lambda i: (0,) * a.ndim)
    rb = row0 // t
    return pl.pallas_call(
        functools.partial(_lru_kernel, t_len=t, tc=tc),
        grid=(b,),
        in_specs=[pl.BlockSpec((t, w), lambda i: (rb + i, C_LX // w)),
                  pl.BlockSpec((t, w), lambda i: (rb + i, C_LG // w)),
                  full(cw), full(cb), full(wg), full(bg), full(sp),
                  pl.BlockSpec((1, 2, w), lambda i: (i, 0, 0))],
        out_specs=[pl.BlockSpec((t, w), lambda i: (i, 0)),
                   pl.BlockSpec((1, 2, w), lambda i: (i, 0, 0))],
        out_shape=[jax.ShapeDtypeStruct((b * t, w), jnp.bfloat16),
                   jax.ShapeDtypeStruct((b, 2, w), jnp.float32)],
        scratch_shapes=[pltpu.VMEM((tc // 8, 8, w), jnp.float32), pltpu.VMEM((tc // 8, 8, w), jnp.float32),
                        pltpu.VMEM((t, w), jnp.float32)],
        compiler_params=_cparams(1),
        name="rglru",
    )(p, p, cw, cb, wg, bg, sp, h0)


MLA_TM = 512
ATT_DQ = 256


def _mla_prep_kernel(q_ref, kv_ref, kr_ref, tab_ref, gq_ref, gkv_ref, wuq_ref, wukv_ref,
                     qo_ref, ko_ref, vo_ref, ckv_ref):
    f32, bf = jnp.float32, jnp.bfloat16
    tab = tab_ref[...]

    def rope(blk):
        prod = blk * tab
        return prod + pltpu.roll(prod, 64, 1)

    qn = (_rms(q_ref[...].astype(f32)) * gq_ref[...]).astype(bf)
    qq = jnp.dot(qn, wuq_ref[...], preferred_element_type=f32)
    ckv = _rms(kv_ref[...].astype(f32)) * gkv_ref[...]
    ckv_ref[...] = ckv
    kk = jnp.dot(ckv.astype(bf), wukv_ref[...], preferred_element_type=f32)
    lane = lax.broadcasted_iota(jnp.int32, tab.shape, 1)
    kpe = jnp.where(lane < MLA_ROPE, rope(kr_ref[...].astype(f32)), 0.0).astype(bf)
    for h in range(MLA_HEADS):
        o = h * ATT_DQ
        qo_ref[:, o:o + 128] = (qq[:, o:o + 128] * MLA_SCALE).astype(bf)
        qo_ref[:, o + 128:o + 256] = (rope(qq[:, o + 128:o + 256]) * MLA_SCALE).astype(bf)
        ko_ref[:, o:o + 128] = kk[:, o:o + 128].astype(bf)
        ko_ref[:, o + 128:o + 256] = kpe
        vo_ref[:, h * MLA_V:(h + 1) * MLA_V] = kk[:, o + 128:o + 256].astype(bf)


def _mla_weights(w_uq, w_ukv):
    wq = w_uq.reshape(MLA_Q_RANK, MLA_HEADS, MLA_NOPE + MLA_ROPE)
    pe = wq[:, :, MLA_NOPE:].reshape(MLA_Q_RANK, MLA_HEADS * MLA_ROPE)
    pep = _rot_partner_cols(pe, MLA_HEADS, MLA_ROPE).reshape(MLA_Q_RANK, MLA_HEADS, MLA_ROPE)
    wq_ext = jnp.concatenate([wq, pep], axis=-1).reshape(MLA_Q_RANK, MLA_HEADS * ATT_DQ)
    return wq_ext.astype(jnp.bfloat16), w_ukv.astype(jnp.bfloat16)


def _mla_prep(p, tab, gq, gkv, wuq, wukv):
    n = p.shape[0]
    tm = MLA_TM
    full = lambda a: pl.BlockSpec(a.shape, lambda i: (0,) * a.ndim)
    row = lambda width, col: pl.BlockSpec((tm, width), lambda i: (i, col // width))
    return pl.pallas_call(
        _mla_prep_kernel,
        grid=(n // tm,),
        in_specs=[row(MLA_Q_RANK, C_MQ), row(MLA_KV_RANK, C_MKV), row(128, C_KR), row(128, 0),
                  full(gq), full(gkv), full(wuq), full(wukv)],
        out_specs=[row(MLA_HEADS * ATT_DQ, 0), row(MLA_HEADS * ATT_DQ, 0), row(MLA_HEADS * MLA_V, 0),
                   row(MLA_KV_RANK, 0)],
        out_shape=[jax.ShapeDtypeStruct((n, MLA_HEADS * ATT_DQ), jnp.bfloat16),
                   jax.ShapeDtypeStruct((n, MLA_HEADS * ATT_DQ), jnp.bfloat16),
                   jax.ShapeDtypeStruct((n, MLA_HEADS * MLA_V), jnp.bfloat16),
                   jax.ShapeDtypeStruct((n, MLA_KV_RANK), jnp.float32)],
        compiler_params=_cparams(1),
        name="mla_prep",
    )(p, p, p, tab, gq, gkv, wuq, wukv)


ATT_TQ = 256


def _attn_kernel(*refs, two):
    dn = (((1,), (1,)), ((), ()))
    f32 = jnp.float32
    if two:
        q_ref, k_ref, v_ref, k2_ref, v2_ref, o_ref = refs
    else:
        q_ref, k_ref, v_ref, o_ref = refs
    q = q_ref[...]
    s = lax.dot_general(q, k_ref[...], dn, preferred_element_type=f32)
    m = jnp.max(s, axis=-1, keepdims=True)
    if two:
        s2 = lax.dot_general(q, k2_ref[...], dn, preferred_element_type=f32)
        m = jnp.maximum(m, jnp.max(s2, axis=-1, keepdims=True))
    p = jnp.exp(s - m)
    l = jnp.sum(p, axis=-1, keepdims=True)
    o = jnp.dot(p.astype(jnp.bfloat16), v_ref[...], preferred_element_type=f32)
    if two:
        p2 = jnp.exp(s2 - m)
        l = l + jnp.sum(p2, axis=-1, keepdims=True)
        o = o + jnp.dot(p2.astype(jnp.bfloat16), v2_ref[...], preferred_element_type=f32)
    o_ref[...] = (o / l).astype(o_ref.dtype)


def _attention(q, k, v, row0, b, t, k2=None, v2=None):
    tb = min(ATT_TQ, t)
    nq = t // tb
    qb, kb = row0 // tb, row0 // t
    two = k2 is not None
    in_specs = [pl.BlockSpec((tb, ATT_DQ), lambda bi, h, i: (qb + bi * nq + i, h)),
                pl.BlockSpec((t, ATT_DQ), lambda bi, h, i: (kb + bi, h)),
                pl.BlockSpec((t, MLA_V), lambda bi, h, i: (kb + bi, h))]
    args = [q, k, v]
    if two:
        t2 = k2.shape[0] // b
        in_specs += [pl.BlockSpec((t2, ATT_DQ), lambda bi, h, i: (bi, h)),
                     pl.BlockSpec((t2, MLA_V), lambda bi, h, i: (bi, h))]
        args += [k2, v2]
    return pl.pallas_call(
        functools.partial(_attn_kernel, two=two),
        grid=(b, MLA_HEADS, nq),
        in_specs=in_specs,
        out_specs=pl.BlockSpec((tb, MLA_V), lambda bi, h, i: (bi * nq + i, h)),
        out_shape=jax.ShapeDtypeStruct((b * t, MLA_HEADS * MLA_V), jnp.bfloat16),
        compiler_params=_cparams(3),
        name="mla_attention",
    )(*args)


def _ret_tables():
    f32 = jnp.float32
    c = RET_CHUNK
    log_g = jnp.log1p(-jnp.exp2(-5.0 - jnp.arange(RET_HEADS, dtype=f32)))[:, None, None]
    idx = jnp.arange(c, dtype=f32)
    dec = jnp.exp(jnp.abs(idx[:, None] - idx[None, :])[None] * log_g)
    row = lambda e: jnp.broadcast_to(jnp.exp(e[None, :, None] * log_g), (RET_HEADS, c, c))
    return jnp.stack([dec, row(idx + 1.0), row(c - idx), row(c - 1.0 - idx), row(idx)], axis=0)


RET_HPS = 2


def _ret_kernel(*refs, nc, rope):
    f32, bf = jnp.float32, jnp.bfloat16
    if rope:
        (q_ref, k_ref, v_ref, g_ref, qp_ref, kp_ref, cos_ref, sin_ref, tab_ref, s0_ref, o_ref, fin_ref,
         acc_s, st_s) = refs
    else:
        q_ref, k_ref, v_ref, g_ref, tab_ref, s0_ref, o_ref, fin_ref, acc_s, st_s = refs
    c, dk = RET_CHUNK, RET_DK
    nt_dims, tn_dims = (((1,), (1,)), ((), ())), (((0,), (0,)), ((), ()))

    def chunk(k0, hh):
        sl = pl.ds(pl.multiple_of(k0 * c, c), c)
        cols = slice(hh * dk, (hh + 1) * dk)
        qc, kc = q_ref[sl, cols].astype(f32), k_ref[sl, cols].astype(f32)
        if rope:
            cos, sin = cos_ref[sl, :], sin_ref[sl, :]
            qc = qc * cos + qp_ref[sl, cols].astype(f32) * sin
            kc = kc * cos + kp_ref[sl, cols].astype(f32) * sin
        return sl, cols, qc.astype(bf), kc * (RET_DK ** -0.5), v_ref[sl, cols]

    st_s[...] = s0_ref[0]
    acc_s[...] = jnp.zeros_like(acc_s)

    def body(i, carry):
        for hh in range(RET_HPS):
            dec, xif, xib, zf, zb = (tab_ref[t, hh] for t in range(5))
            g_chunk = xif[c - 1:c, :]
            sl, cols, qc, kc, vc = chunk(i, hh)
            s = st_s[0, hh]
            sc = lax.dot_general(qc, kc.astype(bf), nt_dims, preferred_element_type=f32) * dec
            acc_s[sl, cols] += (jnp.dot(sc.astype(bf), vc, preferred_element_type=f32)
                                + jnp.dot(qc, s.astype(bf), preferred_element_type=f32) * xif)
            st_s[0, hh] = g_chunk * s + lax.dot_general((kc * zf).astype(bf), vc, tn_dims,
                                                        preferred_element_type=f32)
            sl, cols, qc, kc, vc = chunk(nc - 1 - i, hh)
            s = st_s[1, hh]
            acc_s[sl, cols] += jnp.dot(qc, s.astype(bf), preferred_element_type=f32) * xib
            st_s[1, hh] = g_chunk * s + lax.dot_general((kc * zb).astype(bf), vc, tn_dims,
                                                        preferred_element_type=f32)
        return carry

    lax.fori_loop(0, nc, body, 0)
    fin_ref[0] = st_s[...]

    def finish(i, carry):
        sl = pl.ds(pl.multiple_of(i * c, c), c)
        for hh in range(RET_HPS):
            cols = slice(hh * dk, (hh + 1) * dk)
            g = g_ref[sl, cols].astype(f32)
            o_ref[sl, cols] = (_rms(acc_s[sl, cols]) * (g * _sigmoid(g))).astype(o_ref.dtype)
        return carry

    lax.fori_loop(0, nc, finish, 0)


def _retention(p, row0, b, t, s0, tables, rope_tabs=None):
    nc = t // RET_CHUNK
    rb = row0 // t
    rope = rope_tabs is not None
    wd = RET_HPS * RET_DK
    col = lambda c0: pl.BlockSpec((t, wd), lambda bi, h: (rb + bi, c0 // wd + h))
    st = pl.BlockSpec((1, 2, RET_HPS, RET_DK, RET_DV), lambda bi, h: (bi, 0, h, 0, 0))
    in_specs = [col(C_RQ), col(C_RK), col(C_RV), col(C_RG)]
    args = [p, p, p, p]
    if rope:
        tab = pl.BlockSpec((t, RET_DK), lambda bi, h: (0, 0))
        in_specs += [col(C_RQP), col(C_RKP), tab, tab]
        args += [p, p, rope_tabs[0], rope_tabs[1]]
    in_specs += [pl.BlockSpec((5, RET_HPS, RET_CHUNK, RET_CHUNK), lambda bi, h: (0, h, 0, 0)), st]
    args += [tables, s0]
    return pl.pallas_call(
        functools.partial(_ret_kernel, nc=nc, rope=rope),
        grid=(b, RET_HEADS // RET_HPS),
        in_specs=in_specs,
        out_specs=[pl.BlockSpec((t, wd), lambda bi, h: (bi, h)), st],
        out_shape=[jax.ShapeDtypeStruct((b * t, RET_HEADS * RET_DV), jnp.bfloat16),
                   jax.ShapeDtypeStruct((b, 2, RET_HEADS, RET_DK, RET_DV), jnp.float32)],
        scratch_shapes=[pltpu.VMEM((t, wd), jnp.float32), pltpu.VMEM((2, RET_HPS, RET_DK, RET_DV), jnp.float32)],
        compiler_params=_cparams(2, V7X_VMEM_LIMIT_BIG),
        name="retention",
    )(*args)


OUT_TM = 256
ROUTER_PAD = 128


def _pack_halves(x):
    w = x.shape[1] // 2
    bits = lambda a: lax.bitcast_convert_type(a.astype(jnp.bfloat16).astype(jnp.float32), jnp.int32)
    return bits(x[:, w:]) | lax.shift_right_logical(bits(x[:, :w]), 16)


def _unpack_halves(u):
    lo = lax.bitcast_convert_type(lax.shift_left(u, 16), jnp.float32)
    hi = lax.bitcast_convert_type(u & jnp.int32(-65536), jnp.float32)
    return jnp.concatenate([lo, hi], axis=1)


def _split_bf16(x):
    hi = x.astype(jnp.bfloat16)
    return hi, (x - hi.astype(jnp.float32)).astype(jnp.bfloat16)


def _route(h, whi_ref, wlo_ref, rb_ref):
    f32 = jnp.float32
    hi, lo = _split_bf16(h)
    dn = (((1,), (1,)), ((), ()))
    lt = (lax.dot_general(whi_ref[...], hi, dn, preferred_element_type=f32)
          + lax.dot_general(whi_ref[...], lo, dn, preferred_element_type=f32)
          + lax.dot_general(wlo_ref[...], hi, dn, preferred_element_type=f32))[:N_EXPERTS]
    m = jnp.max(lt, axis=0, keepdims=True)
    e = jnp.exp(lt - m)
    sc = e / jnp.sum(e, axis=0, keepdims=True)
    sel = sc + rb_ref[...][:N_EXPERTS, 0:1]
    rows = lambda a: [a[j:j + 1, :] for j in range(N_EXPERTS)]
    sel_r, sc_r = rows(sel), rows(sc)
    epg = EXPERTS_PER_GROUP

    def top2sum(a, b, c, d):
        h1, l1, h2, l2 = jnp.maximum(a, b), jnp.minimum(a, b), jnp.maximum(c, d), jnp.minimum(c, d)
        return jnp.maximum(h1, h2) + jnp.maximum(jnp.minimum(h1, h2), jnp.maximum(l1, l2))

    gs = [top2sum(*sel_r[g * epg:(g + 1) * epg]) for g in range(N_EXPERT_GROUPS)]
    best, gi = gs[0], jnp.zeros_like(gs[0], dtype=jnp.int32)
    for g in range(1, N_EXPERT_GROUPS):
        upd = gs[g] > best
        gi = jnp.where(upd, g, gi)
        best = jnp.where(upd, gs[g], best)

    def pick(r, j):
        out = r[j]
        for g in range(1, N_EXPERT_GROUPS):
            out = jnp.where(gi == g, r[g * epg + j], out)
        return out

    v = [pick(sel_r, j) for j in range(epg)]
    s = [pick(sc_r, j) for j in range(epg)]

    def argmax_first(vals):
        bv, bi = vals[0], jnp.zeros_like(gi)
        for j in range(1, epg):
            upd = vals[j] > bv
            bi = jnp.where(upd, j, bi)
            bv = jnp.where(upd, vals[j], bv)
        return bi

    i1 = argmax_first(v)
    neg = jnp.float32(-jnp.inf)
    i2 = argmax_first([jnp.where(i1 == j, neg, v[j]) for j in range(epg)])
    take = lambda i: sum(jnp.where(i == j, s[j], 0.0) for j in range(epg))
    w1, w2 = take(i1), take(i2)
    tot = w1 + w2
    return (jnp.concatenate([gi * epg + i1, gi * epg + i2], axis=0),
            jnp.concatenate([w1 / tot, w2 / tot], axis=0))


def _mixout_kernel(ylin_ref, u_ref, ymla_c, ymla_l, ylru_c, ylru_l, yret_c, yret_l, x_ref, mod_ref, d_ref,
                   wglu_ref, wout_ref, g2n_ref, whi_ref, wlo_ref, rb_ref, xo_ref, h2_ref, idx_ref, wgt_ref,
                   *, n_ctx_tiles):
    f32, bf = jnp.float32, jnp.bfloat16
    w = GROUP_W
    is_ctx = pl.program_id(0) < n_ctx_tiles
    pick = lambda c_ref, l_ref: jnp.where(is_ctx, c_ref[...], l_ref[...])
    y = _gelu_tanh(ylin_ref[...].astype(f32) + d_ref[...] * u_ref[...].astype(f32))
    y5 = (y * _sigmoid(jnp.dot(y.astype(bf), wglu_ref[...], preferred_element_type=f32))).astype(bf)
    mix = (jnp.dot(y5, wout_ref[0:w, :], preferred_element_type=f32)
           + jnp.dot(pick(ymla_c, ymla_l), wout_ref[w:2 * w, :], preferred_element_type=f32)
           + jnp.dot(pick(ylru_c, ylru_l), wout_ref[2 * w:3 * w, :], preferred_element_type=f32)
           + jnp.dot(pick(yret_c, yret_l), wout_ref[3 * w:4 * w, :], preferred_element_type=f32))
    x = x_ref[...] + mod_ref[0, 2:3, :] * mix
    xo_ref[...] = x
    h2 = _rms(x) * g2n_ref[...] * (1.0 + mod_ref[0, 4:5, :]) + mod_ref[0, 3:4, :]
    h2_ref[...] = _pack_halves(h2)
    idx, wgt = _route(h2, whi_ref, wlo_ref, rb_ref)
    idx_ref[...] = idx
    wgt_ref[...] = wgt


def _router_weights(router_w, router_b):
    d = router_w.shape[0]
    wt = jnp.zeros((ROUTER_PAD, d), jnp.float32).at[:N_EXPERTS].set(router_w.T.astype(jnp.float32))
    whi, wlo = _split_bf16(wt)
    rb = jnp.zeros((ROUTER_PAD, 128), jnp.float32).at[:N_EXPERTS].set(router_b.astype(jnp.float32)[:, None])
    return whi, wlo, rb


def _mixout(ylin, p, ymla, ylru, yret, x, mods, s5_d, wglu, wout, g2n, rw, n_ctx, dec_seq):
    n, d = x.shape
    tm = OUT_TM
    w = GROUP_W
    na = n_ctx // tm
    seq = functools.partial(_seq_of_tile, tm=tm, n_ctx=n_ctx, dec_seq=dec_seq)
    full = lambda a: pl.BlockSpec(a.shape, lambda i: (0,) * a.ndim)
    row = lambda width: pl.BlockSpec((tm, width), lambda i: (i, 0))
    ctx_row = pl.BlockSpec((tm, w), lambda i: (jnp.minimum(i, na - 1), 0))
    lat_row = pl.BlockSpec((tm, w), lambda i: (jnp.maximum(i - na, 0), 0))
    lanes = pl.BlockSpec((TOP_K, tm), lambda i: (0, i))
    whi, wlo, rb = rw
    return pl.pallas_call(
        functools.partial(_mixout_kernel, n_ctx_tiles=na),
        grid=(n // tm,),
        in_specs=[row(w), row(w), ctx_row, lat_row, ctx_row, lat_row, ctx_row, lat_row, row(d),
                  pl.BlockSpec((1, N_MOD, d), lambda i: (seq(i), 0, 0)),
                  full(s5_d), full(wglu), full(wout), full(g2n), full(whi), full(wlo), full(rb)],
        out_specs=[row(d), row(d // 2), lanes, lanes],
        out_shape=[jax.ShapeDtypeStruct((n, d), jnp.float32), jax.ShapeDtypeStruct((n, d // 2), jnp.int32),
                   jax.ShapeDtypeStruct((TOP_K, n), jnp.int32), jax.ShapeDtypeStruct((TOP_K, n), jnp.float32)],
        compiler_params=_cparams(1),
        name="mix_out_norm_route",
    )(ylin, p, *ymla, *ylru, *yret, x, mods, s5_d, wglu, wout, g2n, whi, wlo, rb)


MOE_TM = 256


def _expert_kernel(te_ref, nt_ref, nv_ref, tok0_ref, tok1_ref, w_ref, dst_ref, wg_ref, wu_ref, wd_ref, h_hbm,
                   y_hbm, wg_s, wu_s, wd_s, xbuf, ybuf, gsem, ssem):
    i = pl.program_id(0)
    bf = jnp.bfloat16
    tm = ybuf.shape[0]
    nt = nt_ref[0]
    slot = i % 2

    def gather_start(tok_ref, sl, unroll):
        def row(r, carry):
            pltpu.make_async_copy(h_hbm.at[pl.ds(tok_ref[0, 0, r], 1), :], xbuf.at[sl, pl.ds(r, 1), :],
                                  gsem.at[sl]).start()
            return carry

        lax.fori_loop(0, tm, row, 0, unroll=unroll)

    def gather_done(sl):
        pltpu.make_async_copy(h_hbm.at[pl.ds(0, tm), :], xbuf.at[sl], gsem.at[sl]).wait()

    def scatter_done(n_rows):
        n8 = pl.multiple_of((n_rows // 8) * 8, 8)

        @pl.when(n8 > 0)
        def _():
            pltpu.make_async_copy(ybuf.at[pl.ds(0, n8), :], y_hbm.at[pl.ds(0, n8), :], ssem).wait()

        def one(r, carry):
            pltpu.make_async_copy(ybuf.at[pl.ds(0, 1), :], y_hbm.at[pl.ds(0, 1), :], ssem).wait()
            return carry

        lax.fori_loop(0, n_rows - n8, one, 0)

    @pl.when(i < nt)
    def _():
        @pl.when(i == 0)
        def _():
            gather_start(tok0_ref, 0, 8)

        @pl.when((i == 0) | (te_ref[i] != te_ref[jnp.maximum(i - 1, 0)]))
        def _():
            wg_s[...] = wg_ref[0, 0].astype(bf)
            wu_s[...] = wu_ref[0, 0].astype(bf)
            wd_s[...] = wd_ref[0, 0].astype(bf)

        gather_done(slot)
        gather_start(tok1_ref, 1 - slot, True)
        x = _unpack_halves(xbuf[slot]).astype(bf)
        g = jnp.dot(x, wg_s[...], preferred_element_type=jnp.float32)
        u = jnp.dot(x, wu_s[...], preferred_element_type=jnp.float32)
        act = ((g * _sigmoid(g)) * u * w_ref[...]).astype(bf)
        y = jnp.dot(act, wd_s[...], preferred_element_type=jnp.float32)

        @pl.when(i > 0)
        def _():
            scatter_done(nv_ref[jnp.maximum(i - 1, 0)])

        ybuf[...] = _pack_halves(y)

        def row(r, carry):
            pltpu.make_async_copy(ybuf.at[pl.ds(r, 1), :], y_hbm.at[pl.ds(dst_ref[0, 0, r], 1), :], ssem).start()
            return carry

        lax.fori_loop(0, nv_ref[i], row, 0)

        @pl.when(i == nt - 1)
        def _():
            scatter_done(nv_ref[i])
            gather_done(1 - slot)


def _experts(h2p, layer, ws, tok, dst, tile_expert, n_tiles_used, n_valid, wg, wu, wd):
    n, dh = h2p.shape
    d = 2 * dh
    tm = MOE_TM
    n_tiles = tok.shape[0]
    f = wg.shape[-1]
    smem_row = lambda nxt: pl.BlockSpec(
        (1, 1, tm), lambda i, te, nt, nv: (jnp.minimum(i + nxt, n_tiles - 1), 0, 0), memory_space=pltpu.SMEM)
    wspec = lambda a, b: pl.BlockSpec((1, 1, a, b), lambda i, te, nt, nv: (layer, te[i], 0, 0))
    return pl.pallas_call(
        _expert_kernel,
        grid_spec=pltpu.PrefetchScalarGridSpec(
            num_scalar_prefetch=3,
            grid=(n_tiles,),
            in_specs=[smem_row(0), smem_row(1),
                      pl.BlockSpec((tm, 1), lambda i, te, nt, nv: (i, 0)),
                      smem_row(0),
                      wspec(d, f), wspec(d, f), wspec(f, d),
                      pl.BlockSpec(memory_space=pl.ANY)],
            out_specs=pl.BlockSpec(memory_space=pl.ANY),
            scratch_shapes=[pltpu.VMEM((d, f), jnp.bfloat16), pltpu.VMEM((d, f), jnp.bfloat16),
                            pltpu.VMEM((f, d), jnp.bfloat16), pltpu.VMEM((2, tm, dh), jnp.int32),
                            pltpu.VMEM((tm, dh), jnp.int32), pltpu.SemaphoreType.DMA((2,)),
                            pltpu.SemaphoreType.DMA(())]),
        out_shape=jax.ShapeDtypeStruct((TOP_K * n, dh), jnp.int32),
        compiler_params=pltpu.CompilerParams(dimension_semantics=("arbitrary",),
                                             vmem_limit_bytes=V7X_VMEM_LIMIT_BIG, disable_bounds_checks=True),
        name="moe_experts",
    )(tile_expert, n_tiles_used, n_valid, tok, tok, ws, dst, wg, wu, wd, h2p)


def _moe_dispatch(idx, wgt):
    n = idx.shape[1]
    tm = MOE_TM
    n_pairs = TOP_K * n
    m_pad = n_pairs + N_EXPERTS * tm
    n_tiles = m_pad // tm
    e_flat = idx.reshape(-1)
    order = jnp.argsort(e_flat, stable=True).astype(jnp.int32)
    experts = jnp.arange(N_EXPERTS, dtype=jnp.int32)
    counts = jnp.sum((e_flat[None, :] == experts[:, None]).astype(jnp.int32), axis=1)
    starts_raw = jnp.cumsum(counts) - counts
    padded = ((counts + tm - 1) // tm) * tm
    ends = jnp.cumsum(padded)
    starts_pad = ends - padded
    tile_start = jnp.arange(n_tiles, dtype=jnp.int32) * tm
    tile_expert = jnp.minimum(jnp.sum((tile_start[:, None] >= ends[None, :]).astype(jnp.int32), axis=1),
                              N_EXPERTS - 1)
    r = jnp.arange(m_pad, dtype=jnp.int32)
    te_r = jnp.repeat(tile_expert, tm)
    off = r - jnp.take(starts_pad, te_r)
    valid = (off >= 0) & (off < jnp.take(counts, te_r))
    pair = jnp.take(order, jnp.clip(jnp.take(starts_raw, te_r) + off, 0, n_pairs - 1))
    tok = jnp.where(valid, pair % n, 0).reshape(n_tiles, 1, tm)
    dst = jnp.where(valid, pair, 0).reshape(n_tiles, 1, tm)
    ws = jnp.where(valid, jnp.take(wgt.reshape(-1), pair), 0.0)[:, None]
    n_tiles_used = (ends[-1] // tm).astype(jnp.int32).reshape(1)
    n_valid = jnp.sum(valid.reshape(n_tiles, tm), axis=1).astype(jnp.int32)
    return tok, dst, ws, tile_expert, n_tiles_used, n_valid


def _moe_rows(h2p, layer, idx, wgt, wg, wu, wd):
    tok, dst, ws, tile_expert, n_tiles_used, n_valid = _moe_dispatch(idx, wgt)
    return _experts(h2p, layer, ws, tok, dst, tile_expert, n_tiles_used, n_valid, wg, wu, wd)


RES_TM = 512


def _resid_kernel(x_ref, ya_ref, yb_ref, mod_ref, gf_ref, *o_refs, final, n_ctx_tiles):
    x = x_ref[...] + mod_ref[0, 5:6, :] * (_unpack_halves(ya_ref[...]) + _unpack_halves(yb_ref[...]))
    if not final:
        o_refs[0][...] = x
    else:
        y = _rms(x) * gf_ref[...]
        i = pl.program_id(0)

        @pl.when(i < n_ctx_tiles)
        def _():
            o_refs[0][...] = y

        @pl.when(i >= n_ctx_tiles)
        def _():
            o_refs[1][...] = y


def _resid(x, y2, mods, gf, n_ctx, dec_seq, final):
    n, d = x.shape
    tm = RES_TM
    na = n_ctx // tm
    seq = functools.partial(_seq_of_tile, tm=tm, n_ctx=n_ctx, dec_seq=dec_seq)
    row = pl.BlockSpec((tm, d), lambda i: (i, 0))
    if final:
        out_specs = [pl.BlockSpec((tm, d), lambda i: (jnp.minimum(i, na - 1), 0)),
                     pl.BlockSpec((tm, d), lambda i: (jnp.maximum(i - na, 0), 0))]
        out_shape = [jax.ShapeDtypeStruct((n_ctx, d), jnp.float32), jax.ShapeDtypeStruct((n - n_ctx, d), jnp.float32)]
    else:
        out_specs, out_shape = row, jax.ShapeDtypeStruct((n, d), jnp.float32)
    return pl.pallas_call(
        functools.partial(_resid_kernel, final=final, n_ctx_tiles=na),
        grid=(n // tm,),
        in_specs=[row, pl.BlockSpec((tm, d // 2), lambda i: (i, 0)),
                  pl.BlockSpec((tm, d // 2), lambda i: (n // tm + i, 0)),
                  pl.BlockSpec((1, N_MOD, d), lambda i: (seq(i), 0, 0)),
                  pl.BlockSpec((1, d), lambda i: (0, 0))],
        out_specs=out_specs,
        out_shape=out_shape,
        compiler_params=_cparams(1),
        name="moe_residual_norm",
    )(x, y2, y2, mods, gf)


def kernel(x_prompt, x_sample, c, cache_mla_ckv, cache_mla_kpe, state_s5, state_lru, state_ret,
           c_ctx, w_ada, b_ada, norm1_g, norm2_g, w_in, w_out,
           s5_a_re, s5_a_im, s5_log_dt, s5_b_re, s5_b_im, s5_c_re, s5_c_im, s5_d, s5_w_glu,
           mla_q_norm_g, mla_w_uq, mla_kv_norm_g, mla_w_ukv,
           lru_conv_w, lru_conv_b, lru_w_a, lru_b_a, lru_w_x, lru_b_x, lru_lambda,
           router_w, router_b, moe_w_gate, moe_w_up, moe_w_down, final_norm_g):
    f32, bf = jnp.float32, jnp.bfloat16
    batch, seq_len, d = x_prompt.shape
    dec_batch, dec_seq, _ = x_sample.shape
    past = cache_mla_ckv.shape[2]
    n_ctx, n_lat = batch * seq_len, dec_batch * dec_seq
    depth = w_in.shape[0]

    x = jnp.concatenate([x_prompt.reshape(n_ctx, d), x_sample.reshape(n_lat, d)], axis=0)
    cvec = jnp.zeros((8, d), f32).at[0].set(c_ctx).at[1:1 + dec_batch].set(c)
    mods_all = _ada(cvec, w_ada, b_ada).reshape(depth, 8, N_MOD, d)

    cos64, sin64 = _rope_tables(dec_seq, MLA_ROPE)
    lat_tab = jnp.tile(jnp.concatenate([cos64, sin64], axis=1), (dec_batch, 1))
    ctx_tab = jnp.concatenate([jnp.ones((n_ctx, MLA_ROPE), f32), jnp.zeros((n_ctx, MLA_ROPE), f32)], axis=1)
    mla_tab = jnp.concatenate([ctx_tab, lat_tab], axis=0)
    ret_rope = _rope_tables(dec_seq, RET_DK)
    ret_tables = _ret_tables()
    rw = _router_weights(router_w, router_b)
    gf = final_norm_g[None, :]

    states = []
    for l in range(depth):
        mods = mods_all[l]
        s5_tab = _s5_tables(s5_a_re[l], s5_a_im[l], s5_log_dt[l], s5_b_re[l], s5_b_im[l], s5_c_re[l], s5_c_im[l])
        lru_tab = _lru_tables(lru_conv_w[l], lru_conv_b[l], lru_w_a[l], lru_b_a[l], lru_w_x[l], lru_b_x[l],
                              lru_lambda[l])
        wuq, wukv = _mla_weights(mla_w_uq[l], mla_w_ukv[l])

        p = _inproj(x, mods, norm1_g[l][None, :], _inproj_weights(w_in[l]), n_ctx, dec_seq)

        u_ctx = p[:n_ctx, C_S5:C_S5 + S5_CH].reshape(batch, seq_len, S5_CH)
        u_lat = p[n_ctx:, C_S5:C_S5 + S5_CH].reshape(dec_batch, dec_seq, S5_CH)
        ylin, s5_fins = _s5_core([u_ctx, u_lat], [None, state_s5[:, l]], s5_tab)

        qo, ko, vo, ckv = _mla_prep(p, mla_tab, mla_q_norm_g[l][None, :], mla_kv_norm_g[l][None, :], wuq, wukv)
        ckv_c = cache_mla_ckv[:, l].reshape(dec_batch * past, MLA_KV_RANK)
        kv_c = _matmul(ckv_c, wukv, out_dtype=bf).reshape(dec_batch * past, MLA_HEADS, 2, 128)
        kpe_c = jnp.broadcast_to(cache_mla_kpe[:, l].reshape(dec_batch * past, 1, MLA_ROPE).astype(bf),
                                 (dec_batch * past, MLA_HEADS, MLA_ROPE))
        k_c = jnp.concatenate([kv_c[:, :, 0], kpe_c, jnp.zeros_like(kpe_c)], axis=-1).reshape(
            dec_batch * past, MLA_HEADS * ATT_DQ)
        v_c = kv_c[:, :, 1].reshape(dec_batch * past, MLA_HEADS * MLA_V)
        ymla = (_attention(qo, ko, vo, 0, batch, seq_len),
                _attention(qo, ko, vo, n_ctx, dec_batch, dec_seq, k_c, v_c))

        ylru_c, lru_fin = _lru(p, 0, batch, seq_len, jnp.zeros((batch, 2, LRU_W), f32), lru_tab)
        ylru_l, _ = _lru(p, n_ctx, dec_batch, dec_seq, state_lru[:, l].astype(f32), lru_tab)
        ylru = (ylru_c, ylru_l)

        yret_c, ret_fin = _retention(p, 0, batch, seq_len,
                                     jnp.zeros((batch, 2, RET_HEADS, RET_DK, RET_DV), f32), ret_tables)
        yret_l, _ = _retention(p, n_ctx, dec_batch, dec_seq, state_ret[:, l].astype(f32), ret_tables, ret_rope)
        yret = (yret_c, yret_l)

        x_mid, h2, idx, wgt = _mixout(ylin, p, ymla, ylru, yret, x, mods, s5_d[l][None, :],
                                      s5_w_glu[l].astype(bf), w_out[l].astype(bf), norm2_g[l][None, :], rw,
                                      n_ctx, dec_seq)
        y2 = _moe_rows(h2, l, idx, wgt, moe_w_gate, moe_w_up, moe_w_down)
        x = _resid(x_mid, y2, mods, gf, n_ctx, dec_seq, final=(l == depth - 1))

        states.append((ckv[:n_ctx].reshape(batch, seq_len, MLA_KV_RANK),
                       p[:n_ctx, C_KR:C_KR + MLA_ROPE].astype(f32).reshape(batch, seq_len, MLA_ROPE),
                       s5_fins[0], lru_fin, ret_fin))

    y_prompt = x[0].reshape(batch, seq_len, d)
    y_sample = x[1].reshape(dec_batch, dec_seq, d)
    new_cache_mla_ckv = jnp.stack([st[0] for st in states], axis=1)
    new_cache_mla_kpe = jnp.stack([st[1] for st in states], axis=1)
    new_state_s5 = jnp.stack([st[2] for st in states], axis=1)
    new_state_lru = jnp.stack([st[3] for st in states], axis=1)
    new_state_ret = jnp.stack([st[4] for st in states], axis=1)
    return (y_prompt, y_sample, new_cache_mla_ckv, new_cache_mla_kpe, new_state_s5, new_state_lru, new_state_ret)
```

```python
import functools
import numpy as np
import jax
import jax.numpy as jnp
from jax import lax
from jax.experimental import pallas as pl
from jax.experimental.pallas import tpu as pltpu

D_MODEL = 2048
DEPTH = 2
GRID_W = 64
EPS = 1e-6
ROPE_BASE = 10000.0
N_MOD = 6
GROUP_W = 512
S5_CH = GROUP_W
S5_GROUP_CH = 16
S5_GROUPS = S5_CH // S5_GROUP_CH
S5_STATE = 64
MLA_HEADS = 4
MLA_NOPE = 128
MLA_ROPE = 64
MLA_V = 128
MLA_Q_RANK = GROUP_W
MLA_KV_RANK = GROUP_W // 2
MLA_SCALE = (MLA_NOPE + MLA_ROPE) ** -0.5
LRU_W = GROUP_W
LRU_CONV = 4
LRU_C = 8.0
RET_HEADS = 4
RET_DK = 128
RET_DV = 128
RET_CHUNK = 128
N_EXPERTS = 16
N_EXPERT_GROUPS = 4
EXPERTS_PER_GROUP = N_EXPERTS // N_EXPERT_GROUPS
TOP_K = 2
D_EXPERT = D_MODEL // 4

V7X_VMEM_LIMIT = 48 * 1024 * 1024
V7X_VMEM_LIMIT_BIG = 56 * 1024 * 1024

C_S5, C_MQ, C_LX, C_LG, C_RQ, C_RK, C_RV, C_RG, C_RQP, C_RKP, C_MKV, C_KR = (
    0, 512, 1024, 1536, 2048, 2560, 3072, 3584, 4096, 4608, 5120, 5376)
P_COLS = 5632
R_S5, R_MQ, R_MKV, R_KR, R_LX, R_LG, R_RQ, R_RK, R_RV, R_RG = (0, 512, 1024, 1280, 1344, 1856, 2368, 2880, 3392, 3904)


def _cparams(n_axes, limit=V7X_VMEM_LIMIT):
    return pltpu.CompilerParams(dimension_semantics=("arbitrary",) * n_axes, vmem_limit_bytes=limit)


def _sigmoid(x):
    return 1.0 / (1.0 + jnp.exp(-x))


def _gelu_tanh(x):
    return 0.5 * x * (1.0 + jnp.tanh(0.7978845608028654 * (x + 0.044715 * (x * x * x))))


def _rms(x):
    return x * lax.rsqrt(jnp.mean(x * x, axis=-1, keepdims=True) + EPS)


def _seq_of_tile(i, tm, n_ctx, dec_seq):
    assert n_ctx % tm == 0 and dec_seq % tm == 0, "a row tile must not straddle two sequences"
    r = i * tm
    return jnp.where(r < n_ctx, 0, 1 + (r - n_ctx) // dec_seq)


def _mm_kernel(x_ref, w_ref, o_ref):
    o_ref[...] = jnp.dot(x_ref[...].astype(jnp.bfloat16), w_ref[...].astype(jnp.bfloat16),
                         preferred_element_type=jnp.float32).astype(o_ref.dtype)


def _matmul(x, w, tm=512, tn=512, out_dtype=jnp.float32):
    m, k = x.shape
    _, n = w.shape
    tm, tn = min(tm, m), min(tn, n)
    return pl.pallas_call(
        _mm_kernel,
        grid=(n // tn, m // tm),
        in_specs=[pl.BlockSpec((tm, k), lambda j, i: (i, 0)), pl.BlockSpec((k, tn), lambda j, i: (0, j))],
        out_specs=pl.BlockSpec((tm, tn), lambda j, i: (i, j)),
        out_shape=jax.ShapeDtypeStruct((m, n), out_dtype),
        compiler_params=_cparams(2),
        name="matmul",
    )(x, w)


ADA_TN = 1024


def _ada_kernel(c_ref, w_ref, b_ref, o_ref):
    c = c_ref[...]
    s = (c * _sigmoid(c)).astype(jnp.bfloat16)
    o_ref[0] = jnp.dot(s, w_ref[0].astype(jnp.bfloat16), preferred_element_type=jnp.float32) + b_ref[0]


def _ada(cvec, w_ada, b_ada):
    depth, d, n = w_ada.shape
    return pl.pallas_call(
        _ada_kernel,
        grid=(depth, n // ADA_TN),
        in_specs=[pl.BlockSpec((8, d), lambda l, j: (0, 0)),
                  pl.BlockSpec((1, d, ADA_TN), lambda l, j: (l, 0, j)),
                  pl.BlockSpec((1, 1, ADA_TN), lambda l, j: (l, 0, j))],
        out_specs=pl.BlockSpec((1, 8, ADA_TN), lambda l, j: (l, 0, j)),
        out_shape=jax.ShapeDtypeStruct((depth, 8, n), jnp.float32),
        compiler_params=_cparams(2),
        name="adaln_mod",
    )(cvec, w_ada, b_ada.reshape(depth, 1, n))


IN_TM = 1024
IN_TN = 512


def _inproj_kernel(x_ref, mod_ref, g_ref, w_ref, o_ref, h_s):
    @pl.when(pl.program_id(1) == 0)
    def _():
        h = _rms(x_ref[...]) * g_ref[...]
        h_s[...] = (h * (1.0 + mod_ref[0, 1:2, :]) + mod_ref[0, 0:1, :]).astype(h_s.dtype)

    o_ref[...] = jnp.dot(h_s[...], w_ref[...], preferred_element_type=jnp.float32).astype(o_ref.dtype)


def _inproj(x, mods, g, w, n_ctx, dec_seq):
    n, d = x.shape
    seq = functools.partial(_seq_of_tile, tm=IN_TM, n_ctx=n_ctx, dec_seq=dec_seq)
    return pl.pallas_call(
        _inproj_kernel,
        grid=(n // IN_TM, P_COLS // IN_TN),
        in_specs=[pl.BlockSpec((IN_TM, d), lambda i, j: (i, 0)),
                  pl.BlockSpec((1, N_MOD, d), lambda i, j: (seq(i), 0, 0)),
                  pl.BlockSpec((1, d), lambda i, j: (0, 0)),
                  pl.BlockSpec((d, IN_TN), lambda i, j: (0, j))],
        out_specs=pl.BlockSpec((IN_TM, IN_TN), lambda i, j: (i, j)),
        out_shape=jax.ShapeDtypeStruct((n, P_COLS), jnp.bfloat16),
        scratch_shapes=[pltpu.VMEM((IN_TM, d), jnp.bfloat16)],
        compiler_params=_cparams(2),
        name="norm_inproj",
    )(x, mods, g, w)


def _rot_partner_cols(w, n_heads, head_dim):
    q = head_dim // 4
    wr = w.reshape(w.shape[0], n_heads, 2, 2, q)
    return jnp.stack([-wr[:, :, :, 1], wr[:, :, :, 0]], axis=3).reshape(w.shape)


def _inproj_weights(w_in):
    seg = lambda off, width: w_in[:, off:off + width]
    kr = seg(R_KR, MLA_ROPE)
    cols = [seg(R_S5, 2 * GROUP_W),
            seg(R_LX, 6 * GROUP_W),
            _rot_partner_cols(seg(R_RQ, 2 * GROUP_W), 2 * RET_HEADS, RET_DK),
            seg(R_MKV, MLA_KV_RANK), kr, _rot_partner_cols(kr, 1, MLA_ROPE),
            jnp.zeros((w_in.shape[0], P_COLS - C_KR - 2 * MLA_ROPE), w_in.dtype)]
    return jnp.concatenate(cols, axis=1).astype(jnp.bfloat16)


def _rope_tables(t_len, rot_dim):
    rows = t_len // GRID_W
    row = jnp.repeat(jnp.arange(rows, dtype=jnp.float32), GRID_W)
    col = jnp.tile(jnp.arange(GRID_W, dtype=jnp.float32), rows)
    n_freq = rot_dim // 4
    inv = ROPE_BASE ** (-jnp.arange(n_freq, dtype=jnp.float32) / n_freq)
    ang = jnp.concatenate([row[:, None] * inv[None]] * 2 + [col[:, None] * inv[None]] * 2, axis=1)
    return jnp.cos(ang), jnp.sin(ang)


S5_L = 32
S5_PAIRS = S5_GROUPS // 2
S5_Q = 4
S5_LANES = S5_GROUPS * S5_STATE
S5_SCAN_LANES = 1024


def _s5_tables(a_re, a_im, log_dt, b_re, b_im, c_re, c_im):
    f32 = jnp.float32
    L, G, P, C = S5_L, S5_GROUPS, S5_STATE, S5_GROUP_CH
    hp = lax.Precision.HIGHEST
    cmul = lambda xr, xi, yr, yi: (xr * yr - xi * yi, xr * yi + xi * yr)
    dt = jnp.exp(log_dt.astype(f32))[..., None]
    zr, zi = a_re * dt, a_im * dt
    ab_r, ab_i = jnp.exp(zr) * jnp.cos(zi), jnp.exp(zr) * jnp.sin(zi)
    den = a_re * a_re + a_im * a_im
    nr, ni = ab_r - 1.0, ab_i
    be_r = (nr * a_re + ni * a_im) / den
    be_i = (ni * a_re - nr * a_im) / den
    bt_r, bt_i = b_re.transpose(0, 2, 1), b_im.transpose(0, 2, 1)
    bp_r, bp_i = cmul(be_r[:, :, None, :], be_i[:, :, None, :], bt_r[None], bt_i[None])
    tau = jnp.arange(L + 1, dtype=f32)[None, None, :, None]
    mag = jnp.exp(zr[:, :, None, :] * tau)
    pw_r, pw_i = mag * jnp.cos(zi[:, :, None, :] * tau), mag * jnp.sin(zi[:, :, None, :] * tau)

    pws = lambda d, rev: tuple(x[d][:, :L][:, ::-1] if rev else x[d][:, :L] for x in (pw_r, pw_i))
    coef = lambda d, rev: cmul(*(x[:, :, None, :] for x in pws(d, rev)), bp_r[d][:, None], bp_i[d][:, None])
    ws = jnp.stack(coef(0, True) + coef(1, False), axis=3).reshape(G, L * C, S5_Q * P)

    ct_r, ct_i = c_re.transpose(0, 2, 1), c_im.transpose(0, 2, 1)
    pt_r, pt_i = pw_r.transpose(0, 1, 3, 2), pw_i.transpose(0, 1, 3, 2)

    def cm(er, ei):
        m_r, m_i = cmul(ct_r[:, :, None, :], ct_i[:, :, None, :], er[..., None], ei[..., None])
        return m_r.reshape(G, P, L * C), m_i.reshape(G, P, L * C)

    mf_r, mf_i = cm(pt_r[0][:, :, 1:L + 1], pt_i[0][:, :, 1:L + 1])
    mb_r, mb_i = cm(pt_r[1][:, :, 1:L + 1][:, :, ::-1], pt_i[1][:, :, 1:L + 1][:, :, ::-1])
    wo = jnp.stack([mf_r, -mf_i, mb_r, -mb_i], axis=1).reshape(G, S5_Q * P, L * C)

    def impulse(d):
        m_r, m_i = cm(pt_r[d][:, :, :L], pt_i[d][:, :, :L])
        kk = (jnp.einsum('gkp,gpx->gkx', bp_r[d], m_r, precision=hp)
              - jnp.einsum('gkp,gpx->gkx', bp_i[d], m_i, precision=hp))
        return kk.reshape(G, C, L, C)

    kf, kb = impulse(0), impulse(1)
    wide = jnp.concatenate([kb[:, :, :0:-1], kf[:, :, :1] + kb[:, :, :1], kf[:, :, 1:]], axis=2)
    wide = jnp.pad(wide.reshape(G, C, (2 * L - 1) * C), ((0, 0), (0, 0), (0, C)))

    a_l = jnp.stack([pw_r[0, :, L], pw_i[0, :, L], pw_r[1, :, L], pw_i[1, :, L]], axis=0).reshape(S5_Q, 1, S5_LANES)
    bf = jnp.bfloat16
    return _s5_toeplitz(wide), ws.astype(bf), wo.astype(bf), a_l


def _s5_toeplitz_kernel(w_ref, o_ref):
    w = w_ref[0]
    for s in range(S5_L):
        off = (S5_L - 1 - s) * S5_GROUP_CH
        o_ref[0, s * S5_GROUP_CH:(s + 1) * S5_GROUP_CH, :] = w[:, off:off + S5_L * S5_GROUP_CH].astype(o_ref.dtype)


def _s5_toeplitz(wide):
    g, c, wl = wide.shape
    n = S5_L * S5_GROUP_CH
    return pl.pallas_call(
        _s5_toeplitz_kernel,
        grid=(g,),
        in_specs=[pl.BlockSpec((1, c, wl), lambda i: (i, 0, 0))],
        out_specs=pl.BlockSpec((1, n, n), lambda i: (i, 0, 0)),
        out_shape=jax.ShapeDtypeStruct((g, n, n), jnp.bfloat16),
        compiler_params=_cparams(1),
        name="s5_toeplitz",
    )(wide)


def _s5a_kernel(x_ref, t_ref, ws_ref, y1_ref, s_ref):
    ss = []
    for j in range(2):
        x = x_ref[j]
        y1_ref[j] = jnp.dot(x, t_ref[j], preferred_element_type=jnp.float32)
        ss.append(jnp.dot(x, ws_ref[j], preferred_element_type=jnp.float32))
    p = S5_STATE
    for q in range(S5_Q):
        s_ref[q] = jnp.concatenate([ss[0][:, q * p:(q + 1) * p], ss[1][:, q * p:(q + 1) * p]], axis=1)


def _s5a(x, toe, ws):
    g, r, w = x.shape
    return pl.pallas_call(
        _s5a_kernel,
        grid=(S5_PAIRS,),
        in_specs=[pl.BlockSpec((2, r, w), lambda i: (i, 0, 0)),
                  pl.BlockSpec((2, w, w), lambda i: (i, 0, 0)),
                  pl.BlockSpec((2, w, S5_Q * S5_STATE), lambda i: (i, 0, 0))],
        out_specs=[pl.BlockSpec((2, r, w), lambda i: (i, 0, 0)),
                   pl.BlockSpec((S5_Q, r, 128), lambda i: (0, 0, i))],
        out_shape=[jax.ShapeDtypeStruct((g, r, w), jnp.float32),
                   jax.ShapeDtypeStruct((S5_Q, r, S5_LANES), jnp.float32)],
        compiler_params=_cparams(1),
        name="s5_chunk_local",
    )(x, toe, ws)


def _s5b_kernel(s_ref, a_ref, h0_ref, hp_ref, fin_ref, *, nc):
    def run(qr, qi, order_fwd):
        ar, ai = a_ref[qr], a_ref[qi]

        def body(i, carry):
            hr, hi = carry
            k = i if order_fwd else nc - 1 - i
            hp_ref[qr, pl.ds(k, 1), :] = hr
            hp_ref[qi, pl.ds(k, 1), :] = hi
            nr = ar * hr - ai * hi + s_ref[qr, pl.ds(k, 1), :]
            ni = ar * hi + ai * hr + s_ref[qi, pl.ds(k, 1), :]
            return nr, ni

        hr, hi = lax.fori_loop(0, nc, body, (h0_ref[qr], h0_ref[qi]))
        fin_ref[qr] = hr
        fin_ref[qi] = hi

    run(0, 1, True)
    run(2, 3, False)


def _s5b(s, a_l, h0):
    _, nc, w = s.shape
    tl = S5_SCAN_LANES
    vec = pl.BlockSpec((S5_Q, 1, tl), lambda i: (0, 0, i))
    seq = pl.BlockSpec((S5_Q, nc, tl), lambda i: (0, 0, i))
    return pl.pallas_call(
        functools.partial(_s5b_kernel, nc=nc),
        grid=(w // tl,),
        in_specs=[seq, vec, vec],
        out_specs=[seq, vec],
        out_shape=[jax.ShapeDtypeStruct((S5_Q, nc, w), jnp.float32),
                   jax.ShapeDtypeStruct((S5_Q, 1, w), jnp.float32)],
        compiler_params=_cparams(1),
        name="s5_chunk_scan",
    )(s, a_l, h0)


def _s5c_kernel(h_ref, wo_ref, y1_ref, y_ref):
    p = S5_STATE
    for j in range(2):
        hcat = jnp.concatenate([h_ref[q][:, j * p:(j + 1) * p] for q in range(S5_Q)], axis=1)
        y2 = jnp.dot(hcat.astype(jnp.bfloat16), wo_ref[j], preferred_element_type=jnp.float32)
        y_ref[j] = (y1_ref[j] + y2).astype(y_ref.dtype)


def _s5c(hprev, wo, y1):
    g, r, w = y1.shape
    return pl.pallas_call(
        _s5c_kernel,
        grid=(S5_PAIRS,),
        in_specs=[pl.BlockSpec((S5_Q, r, 128), lambda i: (0, 0, i)),
                  pl.BlockSpec((2, S5_Q * S5_STATE, w), lambda i: (i, 0, 0)),
                  pl.BlockSpec((2, r, w), lambda i: (i, 0, 0))],
        out_specs=pl.BlockSpec((2, r, w), lambda i: (i, 0, 0)),
        out_shape=jax.ShapeDtypeStruct((g, r, w), jnp.bfloat16),
        compiler_params=_cparams(1),
        name="s5_state_to_out",
    )(hprev, wo, y1)


def _s5_to_chunks(u):
    b, t, _ = u.shape
    nc = t // S5_L
    x = u.reshape(b, nc, S5_L, S5_GROUPS, S5_GROUP_CH).transpose(3, 1, 0, 2, 4)
    return x.reshape(S5_GROUPS, nc * b, S5_L * S5_GROUP_CH)


def _s5_from_chunks(y, b, t):
    nc = t // S5_L
    y = y.reshape(S5_GROUPS, nc, b, S5_L, S5_GROUP_CH).transpose(2, 1, 3, 0, 4)
    return y.reshape(b * t, S5_CH)


def _s5_state_planes(h0):
    b = h0.shape[0]
    return h0.transpose(1, 4, 0, 2, 3).reshape(S5_Q, b, S5_LANES)


def _s5_core(us, h0s, tables):
    toe, ws_pair, wo_pair, a_l = tables
    xs = [_s5_to_chunks(u) for u in us]
    rows = [x.shape[1] for x in xs]
    y1, s = _s5a(jnp.concatenate(xs, axis=1), toe, ws_pair)
    hps, fins = [], []
    off = 0
    for u, h0, r in zip(us, h0s, rows):
        b, t, _ = u.shape
        nc = t // S5_L
        if h0 is None:
            h0p = jnp.zeros((S5_Q, 1, b * S5_LANES), jnp.float32)
        else:
            h0p = _s5_state_planes(h0.astype(jnp.float32)).reshape(S5_Q, 1, b * S5_LANES)
        hp, fin = _s5b(s[:, off:off + r].reshape(S5_Q, nc, b * S5_LANES), jnp.tile(a_l, (1, 1, b)), h0p)
        hps.append(hp.reshape(S5_Q, r, S5_LANES))
        fins.append(fin.reshape(2, 2, b, S5_GROUPS, S5_STATE).transpose(2, 0, 3, 4, 1))
        off += r
    y = _s5c(jnp.concatenate(hps, axis=1), wo_pair, y1)
    outs, off = [], 0
    for u, r in zip(us, rows):
        b, t, _ = u.shape
        outs.append(_s5_from_chunks(y[:, off:off + r], b, t))
        off += r
    return jnp.concatenate(outs, axis=0), fins


LRU_TC = 256
LRU_HALO = 16


def _lru_kernel(x_ref, gate_ref, cw_ref, cb_ref, wg_ref, bg_ref, sp_ref, h0_ref, out_ref, fin_ref,
                a_s, b_s, hf_s, *, t_len, tc):
    f32 = jnp.float32
    nt = t_len // tc
    w = LRU_W
    nb = tc // 8
    row = lax.broadcasted_iota(jnp.int32, (nb, 8, w), 1)

    def gates(c, d):
        r0 = pl.multiple_of(c * tc, tc)
        lo = pl.multiple_of(jnp.maximum(r0 - LRU_HALO, 0), LRU_HALO)
        hi = pl.multiple_of(jnp.minimum(r0 + tc, t_len - LRU_HALO), LRU_HALO)
        prev = jnp.where(c > 0, x_ref[pl.ds(lo, LRU_HALO), :].astype(f32), 0.0)
        nxt = jnp.where(c < nt - 1, x_ref[pl.ds(hi, LRU_HALO), :].astype(f32), 0.0)
        slab = jnp.concatenate([prev, x_ref[pl.ds(r0, tc), :].astype(f32), nxt], axis=0)
        o = LRU_HALO - LRU_CONV // 2
        xc = cb_ref[...] + sum(cw_ref[k:k + 1, :] * slab[o + k:o + k + tc] for k in range(LRU_CONV))
        g = jnp.dot(xc.astype(jnp.bfloat16), wg_ref[d], preferred_element_type=f32) + bg_ref[d]
        r = _sigmoid(g[:, :w])
        i = _sigmoid(g[:, w:])
        log_a = -sp_ref[d] * r
        a = jnp.exp(log_a).reshape(nb, 8, w)
        b = (jnp.sqrt(1.0 - jnp.exp(2.0 * log_a)) * (i * xc)).reshape(nb, 8, w)
        for sh in (1, 2, 4):
            keep, rot = (row >= sh, sh) if d == 0 else (row < 8 - sh, 8 - sh)
            ap = jnp.where(keep, pltpu.roll(a, rot, 1), 1.0)
            bp = jnp.where(keep, pltpu.roll(b, rot, 1), 0.0)
            b = a * bp + b
            a = a * ap
        a_s[...] = a
        b_s[...] = b
        return r0

    def fwd_chunk(c, h):
        r0 = gates(c, 0)

        def block(k, h):
            h8 = a_s[k] * h + b_s[k]
            hf_s[pl.ds(pl.multiple_of(r0 + k * 8, 8), 8), :] = h8
            return h8[7:8, :]

        return lax.fori_loop(0, nb, block, h, unroll=4)

    h = lax.fori_loop(0, nt, fwd_chunk, h0_ref[0, 0:1, :])
    fin_ref[0, 0:1, :] = h

    def bwd_chunk(ci, h):
        r0 = gates(nt - 1 - ci, 1)

        def block(i, h):
            k = nb - 1 - i
            h8 = a_s[k] * h + b_s[k]
            rows = pl.ds(pl.multiple_of(r0 + k * 8, 8), 8)
            hf_s[rows, :] = hf_s[rows, :] + h8
            return h8[0:1, :]

        h = lax.fori_loop(0, nb, block, h, unroll=4)
        sl = pl.ds(r0, tc)
        out_ref[sl, :] = (hf_s[sl, :] * _gelu_tanh(gate_ref[sl, :].astype(f32))).astype(out_ref.dtype)
        return h

    h = lax.fori_loop(0, nt, bwd_chunk, h0_ref[0, 1:2, :])
    fin_ref[0, 1:2, :] = h


def _block_diag(wb):
    n, k, j = wb.shape
    return (wb[:, :, None, :] * jnp.eye(n, dtype=wb.dtype)[:, None, :, None]).reshape(n * k, n * j)


def _lru_tables(conv_w, conv_b, w_a, b_a, w_x, b_x, lam):
    wg = jnp.stack([jnp.concatenate([_block_diag(w_a[d]), _block_diag(w_x[d])], axis=1) for d in range(2)])
    bg = jnp.concatenate([b_a, b_x], axis=-1)[:, None, :]
    sp = (LRU_C * jax.nn.softplus(-lam.astype(jnp.float32)))[:, None, :]
    return conv_w, conv_b[None, :], wg.astype(jnp.bfloat16), bg, sp


def _lru(p, row0, b, t, h0, tables):
    w = LRU_W
    tc = min(LRU_TC, t)
    cw, cb, wg, bg, sp = tables
    full = lambda a: pl.BlockSpec(a.shape, lambda i: (0,) * a.ndim)
    rb = row0 // t
    return pl.pallas_call(
        functools.partial(_lru_kernel, t_len=t, tc=tc),
        grid=(b,),
        in_specs=[pl.BlockSpec((t, w), lambda i: (rb + i, C_LX // w)),
                  pl.BlockSpec((t, w), lambda i: (rb + i, C_LG // w)),
                  full(cw), full(cb), full(wg), full(bg), full(sp),
                  pl.BlockSpec((1, 2, w), lambda i: (i, 0, 0))],
        out_specs=[pl.BlockSpec((t, w), lambda i: (i, 0)),
                   pl.BlockSpec((1, 2, w), lambda i: (i, 0, 0))],
        out_shape=[jax.ShapeDtypeStruct((b * t, w), jnp.bfloat16),
                   jax.ShapeDtypeStruct((b, 2, w), jnp.float32)],
        scratch_shapes=[pltpu.VMEM((tc // 8, 8, w), jnp.float32), pltpu.VMEM((tc // 8, 8, w), jnp.float32),
                        pltpu.VMEM((t, w), jnp.float32)],
        compiler_params=_cparams(1),
        name="rglru",
    )(p, p, cw, cb, wg, bg, sp, h0)


MLA_TM = 512
ATT_DQ = 256


def _mla_prep_kernel(q_ref, kv_ref, kr_ref, tab_ref, gq_ref, gkv_ref, wuq_ref, wukv_ref,
                     qo_ref, ko_ref, vo_ref, ckv_ref):
    f32, bf = jnp.float32, jnp.bfloat16
    tab = tab_ref[...]

    def rope(blk):
        prod = blk * tab
        return prod + pltpu.roll(prod, 64, 1)

    qn = (_rms(q_ref[...].astype(f32)) * gq_ref[...]).astype(bf)
    qq = jnp.dot(qn, wuq_ref[...], preferred_element_type=f32)
    ckv = _rms(kv_ref[...].astype(f32)) * gkv_ref[...]
    ckv_ref[...] = ckv
    kk = jnp.dot(ckv.astype(bf), wukv_ref[...], preferred_element_type=f32)
    lane = lax.broadcasted_iota(jnp.int32, tab.shape, 1)
    kpe = jnp.where(lane < MLA_ROPE, rope(kr_ref[...].astype(f32)), 0.0).astype(bf)
    for h in range(MLA_HEADS):
        o = h * ATT_DQ
        qo_ref[:, o:o + 128] = (qq[:, o:o + 128] * MLA_SCALE).astype(bf)
        qo_ref[:, o + 128:o + 256] = (rope(qq[:, o + 128:o + 256]) * MLA_SCALE).astype(bf)
        ko_ref[:, o:o + 128] = kk[:, o:o + 128].astype(bf)
        ko_ref[:, o + 128:o + 256] = kpe
        vo_ref[:, h * MLA_V:(h + 1) * MLA_V] = kk[:, o + 128:o + 256].astype(bf)


def _mla_weights(w_uq, w_ukv):
    wq = w_uq.reshape(MLA_Q_RANK, MLA_HEADS, MLA_NOPE + MLA_ROPE)
    pe = wq[:, :, MLA_NOPE:].reshape(MLA_Q_RANK, MLA_HEADS * MLA_ROPE)
    pep = _rot_partner_cols(pe, MLA_HEADS, MLA_ROPE).reshape(MLA_Q_RANK, MLA_HEADS, MLA_ROPE)
    wq_ext = jnp.concatenate([wq, pep], axis=-1).reshape(MLA_Q_RANK, MLA_HEADS * ATT_DQ)
    return wq_ext.astype(jnp.bfloat16), w_ukv.astype(jnp.bfloat16)


def _mla_prep(p, tab, gq, gkv, wuq, wukv):
    n = p.shape[0]
    tm = MLA_TM
    full = lambda a: pl.BlockSpec(a.shape, lambda i: (0,) * a.ndim)
    row = lambda width, col: pl.BlockSpec((tm, width), lambda i: (i, col // width))
    return pl.pallas_call(
        _mla_prep_kernel,
        grid=(n // tm,),
        in_specs=[row(MLA_Q_RANK, C_MQ), row(MLA_KV_RANK, C_MKV), row(128, C_KR), row(128, 0),
                  full(gq), full(gkv), full(wuq), full(wukv)],
        out_specs=[row(MLA_HEADS * ATT_DQ, 0), row(MLA_HEADS * ATT_DQ, 0), row(MLA_HEADS * MLA_V, 0),
                   row(MLA_KV_RANK, 0)],
        out_shape=[jax.ShapeDtypeStruct((n, MLA_HEADS * ATT_DQ), jnp.bfloat16),
                   jax.ShapeDtypeStruct((n, MLA_HEADS * ATT_DQ), jnp.bfloat16),
                   jax.ShapeDtypeStruct((n, MLA_HEADS * MLA_V), jnp.bfloat16),
                   jax.ShapeDtypeStruct((n, MLA_KV_RANK), jnp.float32)],
        compiler_params=_cparams(1),
        name="mla_prep",
    )(p, p, p, tab, gq, gkv, wuq, wukv)


ATT_TQ = 512
ATT_SUB = 256


def _attn_kernel(*refs, two, n_sub):
    dn = (((1,), (1,)), ((), ()))
    f32 = jnp.float32
    if two:
        q_ref, k_ref, v_ref, k2_ref, v2_ref, o_ref = refs
    else:
        q_ref, k_ref, v_ref, o_ref = refs
    rows = q_ref.shape[0] // n_sub
    for part in range(n_sub):
        sl = slice(part * rows, (part + 1) * rows)
        q = q_ref[sl, :]
        s = lax.dot_general(q, k_ref[...], dn, preferred_element_type=f32)
        m = jnp.max(s, axis=-1, keepdims=True)
        if two:
            s2 = lax.dot_general(q, k2_ref[...], dn, preferred_element_type=f32)
            m = jnp.maximum(m, jnp.max(s2, axis=-1, keepdims=True))
        p = jnp.exp(s - m)
        l = jnp.sum(p, axis=-1, keepdims=True)
        o = jnp.dot(p.astype(jnp.bfloat16), v_ref[...], preferred_element_type=f32)
        if two:
            p2 = jnp.exp(s2 - m)
            l = l + jnp.sum(p2, axis=-1, keepdims=True)
            o = o + jnp.dot(p2.astype(jnp.bfloat16), v2_ref[...], preferred_element_type=f32)
        o_ref[sl, :] = (o / l).astype(o_ref.dtype)


def _attention(q, k, v, row0, b, t, k2=None, v2=None):
    tb = min(ATT_TQ, t)
    nq = t // tb
    qb, kb = row0 // tb, row0 // t
    two = k2 is not None
    in_specs = [pl.BlockSpec((tb, ATT_DQ), lambda bi, h, i: (qb + bi * nq + i, h)),
                pl.BlockSpec((t, ATT_DQ), lambda bi, h, i: (kb + bi, h)),
                pl.BlockSpec((t, MLA_V), lambda bi, h, i: (kb + bi, h))]
    args = [q, k, v]
    if two:
        t2 = k2.shape[0] // b
        in_specs += [pl.BlockSpec((t2, ATT_DQ), lambda bi, h, i: (bi, h)),
                     pl.BlockSpec((t2, MLA_V), lambda bi, h, i: (bi, h))]
        args += [k2, v2]
    return pl.pallas_call(
        functools.partial(_attn_kernel, two=two, n_sub=max(tb // ATT_SUB, 1)),
        grid=(b, MLA_HEADS, nq),
        in_specs=in_specs,
        out_specs=pl.BlockSpec((tb, MLA_V), lambda bi, h, i: (bi * nq + i, h)),
        out_shape=jax.ShapeDtypeStruct((b * t, MLA_HEADS * MLA_V), jnp.bfloat16),
        compiler_params=_cparams(3),
        name="mla_attention",
    )(*args)


def _ret_tables():
    f32 = jnp.float32
    c = RET_CHUNK
    log_g = jnp.log1p(-jnp.exp2(-5.0 - jnp.arange(RET_HEADS, dtype=f32)))[:, None, None]
    idx = jnp.arange(c, dtype=f32)
    dec = jnp.exp(jnp.abs(idx[:, None] - idx[None, :])[None] * log_g)
    row = lambda e: jnp.broadcast_to(jnp.exp(e[None, :, None] * log_g), (RET_HEADS, c, c))
    return jnp.stack([dec, row(idx + 1.0), row(c - idx), row(c - 1.0 - idx), row(idx)], axis=0)


RET_HPS = 2


def _ret_kernel(*refs, nc, rope):
    f32, bf = jnp.float32, jnp.bfloat16
    if rope:
        (q_ref, k_ref, v_ref, g_ref, qp_ref, kp_ref, cos_ref, sin_ref, tab_ref, s0_ref, o_ref, fin_ref,
         acc_s, st_s) = refs
    else:
        q_ref, k_ref, v_ref, g_ref, tab_ref, s0_ref, o_ref, fin_ref, acc_s, st_s = refs
    c, dk = RET_CHUNK, RET_DK
    nt_dims, tn_dims = (((1,), (1,)), ((), ())), (((0,), (0,)), ((), ()))

    def chunk(k0, hh):
        sl = pl.ds(pl.multiple_of(k0 * c, c), c)
        cols = slice(hh * dk, (hh + 1) * dk)
        qc, kc = q_ref[sl, cols].astype(f32), k_ref[sl, cols].astype(f32)
        if rope:
            cos, sin = cos_ref[sl, :], sin_ref[sl, :]
            qc = qc * cos + qp_ref[sl, cols].astype(f32) * sin
            kc = kc * cos + kp_ref[sl, cols].astype(f32) * sin
        return sl, cols, qc.astype(bf), kc * (RET_DK ** -0.5), v_ref[sl, cols]

    st_s[...] = s0_ref[0]
    acc_s[...] = jnp.zeros_like(acc_s)

    def body(i, carry):
        for hh in range(RET_HPS):
            dec, xif, xib, zf, zb = (tab_ref[t, hh] for t in range(5))
            g_chunk = xif[c - 1:c, :]
            sl, cols, qc, kc, vc = chunk(i, hh)
            s = st_s[0, hh]
            sc = lax.dot_general(qc, kc.astype(bf), nt_dims, preferred_element_type=f32) * dec
            acc_s[sl, cols] += (jnp.dot(sc.astype(bf), vc, preferred_element_type=f32)
                                + jnp.dot(qc, s.astype(bf), preferred_element_type=f32) * xif)
            st_s[0, hh] = g_chunk * s + lax.dot_general((kc * zf).astype(bf), vc, tn_dims,
                                                        preferred_element_type=f32)
            sl, cols, qc, kc, vc = chunk(nc - 1 - i, hh)
            s = st_s[1, hh]
            acc_s[sl, cols] += jnp.dot(qc, s.astype(bf), preferred_element_type=f32) * xib
            st_s[1, hh] = g_chunk * s + lax.dot_general((kc * zb).astype(bf), vc, tn_dims,
                                                        preferred_element_type=f32)
        return carry

    lax.fori_loop(0, nc, body, 0)
    fin_ref[0] = st_s[...]

    def finish(i, carry):
        sl = pl.ds(pl.multiple_of(i * c, c), c)
        for hh in range(RET_HPS):
            cols = slice(hh * dk, (hh + 1) * dk)
            g = g_ref[sl, cols].astype(f32)
            o_ref[sl, cols] = (_rms(acc_s[sl, cols]) * (g * _sigmoid(g))).astype(o_ref.dtype)
        return carry

    lax.fori_loop(0, nc, finish, 0)


def _retention(p, row0, b, t, s0, tables, rope_tabs=None):
    nc = t // RET_CHUNK
    rb = row0 // t
    rope = rope_tabs is not None
    wd = RET_HPS * RET_DK
    col = lambda c0: pl.BlockSpec((t, wd), lambda bi, h: (rb + bi, c0 // wd + h))
    st = pl.BlockSpec((1, 2, RET_HPS, RET_DK, RET_DV), lambda bi, h: (bi, 0, h, 0, 0))
    in_specs = [col(C_RQ), col(C_RK), col(C_RV), col(C_RG)]
    args = [p, p, p, p]
    if rope:
        tab = pl.BlockSpec((t, RET_DK), lambda bi, h: (0, 0))
        in_specs += [col(C_RQP), col(C_RKP), tab, tab]
        args += [p, p, rope_tabs[0], rope_tabs[1]]
    in_specs += [pl.BlockSpec((5, RET_HPS, RET_CHUNK, RET_CHUNK), lambda bi, h: (0, h, 0, 0)), st]
    args += [tables, s0]
    return pl.pallas_call(
        functools.partial(_ret_kernel, nc=nc, rope=rope),
        grid=(b, RET_HEADS // RET_HPS),
        in_specs=in_specs,
        out_specs=[pl.BlockSpec((t, wd), lambda bi, h: (bi, h)), st],
        out_shape=[jax.ShapeDtypeStruct((b * t, RET_HEADS * RET_DV), jnp.bfloat16),
                   jax.ShapeDtypeStruct((b, 2, RET_HEADS, RET_DK, RET_DV), jnp.float32)],
        scratch_shapes=[pltpu.VMEM((t, wd), jnp.float32), pltpu.VMEM((2, RET_HPS, RET_DK, RET_DV), jnp.float32)],
        compiler_params=_cparams(2, V7X_VMEM_LIMIT_BIG),
        name="retention",
    )(*args)


OUT_TM = 512
ROUTER_PAD = 128


def _pack_halves(x):
    w = x.shape[1] // 2
    bits = lambda a: lax.bitcast_convert_type(a.astype(jnp.bfloat16).astype(jnp.float32), jnp.int32)
    return bits(x[:, w:]) | lax.shift_right_logical(bits(x[:, :w]), 16)


def _unpack_halves(u):
    lo = lax.bitcast_convert_type(lax.shift_left(u, 16), jnp.float32)
    hi = lax.bitcast_convert_type(u & jnp.int32(-65536), jnp.float32)
    return jnp.concatenate([lo, hi], axis=1)


def _split_bf16(x):
    hi = x.astype(jnp.bfloat16)
    return hi, (x - hi.astype(jnp.float32)).astype(jnp.bfloat16)


def _route(h, whi_ref, wlo_ref, rb_ref):
    f32 = jnp.float32
    hi, lo = _split_bf16(h)
    dn = (((1,), (1,)), ((), ()))
    lt = (lax.dot_general(whi_ref[...], hi, dn, preferred_element_type=f32)
          + lax.dot_general(whi_ref[...], lo, dn, preferred_element_type=f32)
          + lax.dot_general(wlo_ref[...], hi, dn, preferred_element_type=f32))[:N_EXPERTS]
    m = jnp.max(lt, axis=0, keepdims=True)
    e = jnp.exp(lt - m)
    sc = e / jnp.sum(e, axis=0, keepdims=True)
    sel = sc + rb_ref[...][:N_EXPERTS, 0:1]
    rows = lambda a: [a[j:j + 1, :] for j in range(N_EXPERTS)]
    sel_r, sc_r = rows(sel), rows(sc)
    epg = EXPERTS_PER_GROUP

    def top2sum(a, b, c, d):
        h1, l1, h2, l2 = jnp.maximum(a, b), jnp.minimum(a, b), jnp.maximum(c, d), jnp.minimum(c, d)
        return jnp.maximum(h1, h2) + jnp.maximum(jnp.minimum(h1, h2), jnp.maximum(l1, l2))

    gs = [top2sum(*sel_r[g * epg:(g + 1) * epg]) for g in range(N_EXPERT_GROUPS)]
    best, gi = gs[0], jnp.zeros_like(gs[0], dtype=jnp.int32)
    for g in range(1, N_EXPERT_GROUPS):
        upd = gs[g] > best
        gi = jnp.where(upd, g, gi)
        best = jnp.where(upd, gs[g], best)

    def pick(r, j):
        out = r[j]
        for g in range(1, N_EXPERT_GROUPS):
            out = jnp.where(gi == g, r[g * epg + j], out)
        return out

    v = [pick(sel_r, j) for j in range(epg)]
    s = [pick(sc_r, j) for j in range(epg)]

    def argmax_first(vals):
        bv, bi = vals[0], jnp.zeros_like(gi)
        for j in range(1, epg):
            upd = vals[j] > bv
            bi = jnp.where(upd, j, bi)
            bv = jnp.where(upd, vals[j], bv)
        return bi

    i1 = argmax_first(v)
    neg = jnp.float32(-jnp.inf)
    i2 = argmax_first([jnp.where(i1 == j, neg, v[j]) for j in range(epg)])
    take = lambda i: sum(jnp.where(i == j, s[j], 0.0) for j in range(epg))
    w1, w2 = take(i1), take(i2)
    tot = w1 + w2
    return (jnp.concatenate([gi * epg + i1, gi * epg + i2], axis=0),
            jnp.concatenate([w1 / tot, w2 / tot], axis=0))


def _mixout_kernel(ylin_ref, u_ref, ymla_c, ymla_l, ylru_c, ylru_l, yret_c, yret_l, x_ref, mod_ref, d_ref,
                   wglu_ref, wout_ref, g2n_ref, whi_ref, wlo_ref, rb_ref, xo_ref, h2_ref, idx_ref, wgt_ref,
                   *, n_ctx_tiles):
    f32, bf = jnp.float32, jnp.bfloat16
    w = GROUP_W
    is_ctx = pl.program_id(0) < n_ctx_tiles
    pick = lambda c_ref, l_ref: jnp.where(is_ctx, c_ref[...], l_ref[...])
    y = _gelu_tanh(ylin_ref[...].astype(f32) + d_ref[...] * u_ref[...].astype(f32))
    y5 = (y * _sigmoid(jnp.dot(y.astype(bf), wglu_ref[...], preferred_element_type=f32))).astype(bf)
    mix = (jnp.dot(y5, wout_ref[0:w, :], preferred_element_type=f32)
           + jnp.dot(pick(ymla_c, ymla_l), wout_ref[w:2 * w, :], preferred_element_type=f32)
           + jnp.dot(pick(ylru_c, ylru_l), wout_ref[2 * w:3 * w, :], preferred_element_type=f32)
           + jnp.dot(pick(yret_c, yret_l), wout_ref[3 * w:4 * w, :], preferred_element_type=f32))
    x = x_ref[...] + mod_ref[0, 2:3, :] * mix
    xo_ref[...] = x
    h2 = _rms(x) * g2n_ref[...] * (1.0 + mod_ref[0, 4:5, :]) + mod_ref[0, 3:4, :]
    h2_ref[...] = _pack_halves(h2)
    idx, wgt = _route(h2, whi_ref, wlo_ref, rb_ref)
    idx_ref[...] = idx
    wgt_ref[...] = wgt


def _router_weights(router_w, router_b):
    d = router_w.shape[0]
    wt = jnp.zeros((ROUTER_PAD, d), jnp.float32).at[:N_EXPERTS].set(router_w.T.astype(jnp.float32))
    whi, wlo = _split_bf16(wt)
    rb = jnp.zeros((ROUTER_PAD, 128), jnp.float32).at[:N_EXPERTS].set(router_b.astype(jnp.float32)[:, None])
    return whi, wlo, rb


def _mixout(ylin, p, ymla, ylru, yret, x, mods, s5_d, wglu, wout, g2n, rw, n_ctx, dec_seq):
    n, d = x.shape
    tm = OUT_TM
    w = GROUP_W
    na = n_ctx // tm
    seq = functools.partial(_seq_of_tile, tm=tm, n_ctx=n_ctx, dec_seq=dec_seq)
    full = lambda a: pl.BlockSpec(a.shape, lambda i: (0,) * a.ndim)
    row = lambda width: pl.BlockSpec((tm, width), lambda i: (i, 0))
    ctx_row = pl.BlockSpec((tm, w), lambda i: (jnp.minimum(i, na - 1), 0))
    lat_row = pl.BlockSpec((tm, w), lambda i: (jnp.maximum(i - na, 0), 0))
    lanes = pl.BlockSpec((TOP_K, tm), lambda i: (0, i))
    whi, wlo, rb = rw
    return pl.pallas_call(
        functools.partial(_mixout_kernel, n_ctx_tiles=na),
        grid=(n // tm,),
        in_specs=[row(w), row(w), ctx_row, lat_row, ctx_row, lat_row, ctx_row, lat_row, row(d),
                  pl.BlockSpec((1, N_MOD, d), lambda i: (seq(i), 0, 0)),
                  full(s5_d), full(wglu), full(wout), full(g2n), full(whi), full(wlo), full(rb)],
        out_specs=[row(d), row(d // 2), lanes, lanes],
        out_shape=[jax.ShapeDtypeStruct((n, d), jnp.float32), jax.ShapeDtypeStruct((n, d // 2), jnp.int32),
                   jax.ShapeDtypeStruct((TOP_K, n), jnp.int32), jax.ShapeDtypeStruct((TOP_K, n), jnp.float32)],
        compiler_params=_cparams(1, V7X_VMEM_LIMIT_BIG),
        name="mix_out_norm_route",
    )(ylin, p, *ymla, *ylru, *yret, x, mods, s5_d, wglu, wout, g2n, whi, wlo, rb)


MOE_TM = 256


def _expert_kernel(te_ref, nt_ref, nv_ref, tok0_ref, tok1_ref, w_ref, dstp_ref, dst_ref, wg_ref, wu_ref, wd_ref,
                   h_hbm, y_hbm, wg_s, wu_s, wd_s, xbuf, xs_s, ybuf, gsem, ssem):
    i = pl.program_id(0)
    bf = jnp.bfloat16
    tm = xs_s.shape[0]
    nt = nt_ref[0]
    slot = i % 2

    def gather_start(tok_ref, sl, unroll):
        def row(r, carry):
            pltpu.make_async_copy(h_hbm.at[pl.ds(tok_ref[0, 0, r], 1), :], xbuf.at[sl, pl.ds(r, 1), :],
                                  gsem.at[sl]).start()
            return carry

        lax.fori_loop(0, tm, row, 0, unroll=unroll)

    def gather_done(sl):
        pltpu.make_async_copy(h_hbm.at[pl.ds(0, tm), :], xbuf.at[sl], gsem.at[sl]).wait()

    def scatter_row(d_ref, sl, r):
        return pltpu.make_async_copy(ybuf.at[sl, pl.ds(r, 1), :], y_hbm.at[pl.ds(d_ref[0, 0, r], 1), :], ssem.at[sl])

    def scatter_done(sl, n_rows):
        n8 = pl.multiple_of((n_rows // 8) * 8, 8)

        @pl.when(n8 > 0)
        def _():
            pltpu.make_async_copy(ybuf.at[sl, pl.ds(0, n8), :], y_hbm.at[pl.ds(0, n8), :], ssem.at[sl]).wait()

        def one(r, carry):
            pltpu.make_async_copy(ybuf.at[sl, pl.ds(0, 1), :], y_hbm.at[pl.ds(0, 1), :], ssem.at[sl]).wait()
            return carry

        lax.fori_loop(0, n_rows - n8, one, 0)

    @pl.when(i < nt)
    def _():
        @pl.when(i == 0)
        def _():
            gather_start(tok0_ref, 0, 8)

        @pl.when((i == 0) | (te_ref[i] != te_ref[jnp.maximum(i - 1, 0)]))
        def _():
            wg_s[...] = wg_ref[0, 0].astype(bf)
            wu_s[...] = wu_ref[0, 0].astype(bf)
            wd_s[...] = wd_ref[0, 0].astype(bf)

        gather_done(slot)
        xs_s[...] = _unpack_halves(xbuf[slot]).astype(bf)
        gather_start(tok1_ref, 1 - slot, True)
        n_prev = jnp.where(i > 0, nv_ref[jnp.maximum(i - 1, 0)], 0)
        for r in range(tm):
            @pl.when(r < n_prev)
            def _():
                scatter_row(dstp_ref, 1 - slot, r).start()

        x = xs_s[...]
        g = jnp.dot(x, wg_s[...], preferred_element_type=jnp.float32)
        u = jnp.dot(x, wu_s[...], preferred_element_type=jnp.float32)
        act = ((g * _sigmoid(g)) * u * w_ref[...]).astype(bf)
        y = jnp.dot(act, wd_s[...], preferred_element_type=jnp.float32)

        @pl.when(i > 1)
        def _():
            scatter_done(slot, nv_ref[jnp.maximum(i - 2, 0)])

        ybuf[slot] = _pack_halves(y)

        @pl.when(i == nt - 1)
        def _():
            def row(r, carry):
                scatter_row(dst_ref, slot, r).start()
                return carry

            lax.fori_loop(0, nv_ref[i], row, 0)

            @pl.when(i > 0)
            def _():
                scatter_done(1 - slot, nv_ref[jnp.maximum(i - 1, 0)])

            scatter_done(slot, nv_ref[i])
            gather_done(1 - slot)


def _experts(h2p, layer, ws, tok, dst, tile_expert, n_tiles_used, n_valid, wg, wu, wd):
    n, dh = h2p.shape
    d = 2 * dh
    tm = MOE_TM
    n_tiles = tok.shape[0]
    f = wg.shape[-1]
    smem_row = lambda delta: pl.BlockSpec(
        (1, 1, tm), lambda i, te, nt, nv: (jnp.clip(i + delta, 0, n_tiles - 1), 0, 0), memory_space=pltpu.SMEM)
    wspec = lambda a, b: pl.BlockSpec((1, 1, a, b), lambda i, te, nt, nv: (layer, te[i], 0, 0))
    return pl.pallas_call(
        _expert_kernel,
        grid_spec=pltpu.PrefetchScalarGridSpec(
            num_scalar_prefetch=3,
            grid=(n_tiles,),
            in_specs=[smem_row(0), smem_row(1),
                      pl.BlockSpec((tm, 1), lambda i, te, nt, nv: (i, 0)),
                      smem_row(-1), smem_row(0),
                      wspec(d, f), wspec(d, f), wspec(f, d),
                      pl.BlockSpec(memory_space=pl.ANY)],
            out_specs=pl.BlockSpec(memory_space=pl.ANY),
            scratch_shapes=[pltpu.VMEM((d, f), jnp.bfloat16), pltpu.VMEM((d, f), jnp.bfloat16),
                            pltpu.VMEM((f, d), jnp.bfloat16), pltpu.VMEM((2, tm, dh), jnp.int32),
                            pltpu.VMEM((tm, d), jnp.bfloat16), pltpu.VMEM((2, tm, dh), jnp.int32),
                            pltpu.SemaphoreType.DMA((2,)), pltpu.SemaphoreType.DMA((2,))]),
        out_shape=jax.ShapeDtypeStruct((TOP_K * n, dh), jnp.int32),
        compiler_params=pltpu.CompilerParams(dimension_semantics=("arbitrary",),
                                             vmem_limit_bytes=V7X_VMEM_LIMIT_BIG, disable_bounds_checks=True),
        name="moe_experts",
    )(tile_expert, n_tiles_used, n_valid, tok, tok, ws, dst, dst, wg, wu, wd, h2p)


def _moe_dispatch(idx, wgt):
    n = idx.shape[1]
    tm = MOE_TM
    n_pairs = TOP_K * n
    m_pad = n_pairs + N_EXPERTS * tm
    n_tiles = m_pad // tm
    e_flat = idx.reshape(-1)
    order = jnp.argsort(e_flat, stable=True).astype(jnp.int32)
    experts = jnp.arange(N_EXPERTS, dtype=jnp.int32)
    counts = jnp.sum((e_flat[None, :] == experts[:, None]).astype(jnp.int32), axis=1)
    starts_raw = jnp.cumsum(counts) - counts
    padded = ((counts + tm - 1) // tm) * tm
    ends = jnp.cumsum(padded)
    starts_pad = ends - padded
    tile_start = jnp.arange(n_tiles, dtype=jnp.int32) * tm
    tile_expert = jnp.minimum(jnp.sum((tile_start[:, None] >= ends[None, :]).astype(jnp.int32), axis=1),
                              N_EXPERTS - 1)
    r = jnp.arange(m_pad, dtype=jnp.int32)
    te_r = jnp.repeat(tile_expert, tm)
    off = r - jnp.take(starts_pad, te_r)
    valid = (off >= 0) & (off < jnp.take(counts, te_r))
    pair = jnp.take(order, jnp.clip(jnp.take(starts_raw, te_r) + off, 0, n_pairs - 1))
    tok = jnp.where(valid, pair % n, 0).reshape(n_tiles, 1, tm)
    dst = jnp.where(valid, pair, 0).reshape(n_tiles, 1, tm)
    ws = jnp.where(valid, jnp.take(wgt.reshape(-1), pair), 0.0)[:, None]
    n_tiles_used = (ends[-1] // tm).astype(jnp.int32).reshape(1)
    n_valid = jnp.sum(valid.reshape(n_tiles, tm), axis=1).astype(jnp.int32)
    return tok, dst, ws, tile_expert, n_tiles_used, n_valid


def _moe_rows(h2p, layer, idx, wgt, wg, wu, wd):
    tok, dst, ws, tile_expert, n_tiles_used, n_valid = _moe_dispatch(idx, wgt)
    return _experts(h2p, layer, ws, tok, dst, tile_expert, n_tiles_used, n_valid, wg, wu, wd)


RES_TM = 512


def _resid_kernel(x_ref, ya_ref, yb_ref, mod_ref, gf_ref, *o_refs, final, n_ctx_tiles):
    x = x_ref[...] + mod_ref[0, 5:6, :] * (_unpack_halves(ya_ref[...]) + _unpack_halves(yb_ref[...]))
    if not final:
        o_refs[0][...] = x
    else:
        y = _rms(x) * gf_ref[...]
        i = pl.program_id(0)

        @pl.when(i < n_ctx_tiles)
        def _():
            o_refs[0][...] = y

        @pl.when(i >= n_ctx_tiles)
        def _():
            o_refs[1][...] = y


def _resid(x, y2, mods, gf, n_ctx, dec_seq, final):
    n, d = x.shape
    tm = RES_TM
    na = n_ctx // tm
    seq = functools.partial(_seq_of_tile, tm=tm, n_ctx=n_ctx, dec_seq=dec_seq)
    row = pl.BlockSpec((tm, d), lambda i: (i, 0))
    if final:
        out_specs = [pl.BlockSpec((tm, d), lambda i: (jnp.minimum(i, na - 1), 0)),
                     pl.BlockSpec((tm, d), lambda i: (jnp.maximum(i - na, 0), 0))]
        out_shape = [jax.ShapeDtypeStruct((n_ctx, d), jnp.float32), jax.ShapeDtypeStruct((n - n_ctx, d), jnp.float32)]
    else:
        out_specs, out_shape = row, jax.ShapeDtypeStruct((n, d), jnp.float32)
    return pl.pallas_call(
        functools.partial(_resid_kernel, final=final, n_ctx_tiles=na),
        grid=(n // tm,),
        in_specs=[row, pl.BlockSpec((tm, d // 2), lambda i: (i, 0)),
                  pl.BlockSpec((tm, d // 2), lambda i: (n // tm + i, 0)),
                  pl.BlockSpec((1, N_MOD, d), lambda i: (seq(i), 0, 0)),
                  pl.BlockSpec((1, d), lambda i: (0, 0))],
        out_specs=out_specs,
        out_shape=out_shape,
        compiler_params=_cparams(1),
        name="moe_residual_norm",
    )(x, y2, y2, mods, gf)


def kernel(x_prompt, x_sample, c, cache_mla_ckv, cache_mla_kpe, state_s5, state_lru, state_ret,
           c_ctx, w_ada, b_ada, norm1_g, norm2_g, w_in, w_out,
           s5_a_re, s5_a_im, s5_log_dt, s5_b_re, s5_b_im, s5_c_re, s5_c_im, s5_d, s5_w_glu,
           mla_q_norm_g, mla_w_uq, mla_kv_norm_g, mla_w_ukv,
           lru_conv_w, lru_conv_b, lru_w_a, lru_b_a, lru_w_x, lru_b_x, lru_lambda,
           router_w, router_b, moe_w_gate, moe_w_up, moe_w_down, final_norm_g):
    f32, bf = jnp.float32, jnp.bfloat16
    batch, seq_len, d = x_prompt.shape
    dec_batch, dec_seq, _ = x_sample.shape
    past = cache_mla_ckv.shape[2]
    n_ctx, n_lat = batch * seq_len, dec_batch * dec_seq
    depth = w_in.shape[0]

    x = jnp.concatenate([x_prompt.reshape(n_ctx, d), x_sample.reshape(n_lat, d)], axis=0)
    cvec = jnp.zeros((8, d), f32).at[0].set(c_ctx).at[1:1 + dec_batch].set(c)
    mods_all = _ada(cvec, w_ada, b_ada).reshape(depth, 8, N_MOD, d)

    cos64, sin64 = _rope_tables(dec_seq, MLA_ROPE)
    lat_tab = jnp.tile(jnp.concatenate([cos64, sin64], axis=1), (dec_batch, 1))
    ctx_tab = jnp.concatenate([jnp.ones((n_ctx, MLA_ROPE), f32), jnp.zeros((n_ctx, MLA_ROPE), f32)], axis=1)
    mla_tab = jnp.concatenate([ctx_tab, lat_tab], axis=0)
    ret_rope = _rope_tables(dec_seq, RET_DK)
    ret_tables = _ret_tables()
    rw = _router_weights(router_w, router_b)
    gf = final_norm_g[None, :]

    states = []
    for l in range(depth):
        mods = mods_all[l]
        s5_tab = _s5_tables(s5_a_re[l], s5_a_im[l], s5_log_dt[l], s5_b_re[l], s5_b_im[l], s5_c_re[l], s5_c_im[l])
        lru_tab = _lru_tables(lru_conv_w[l], lru_conv_b[l], lru_w_a[l], lru_b_a[l], lru_w_x[l], lru_b_x[l],
                              lru_lambda[l])
        wuq, wukv = _mla_weights(mla_w_uq[l], mla_w_ukv[l])

        p = _inproj(x, mods, norm1_g[l][None, :], _inproj_weights(w_in[l]), n_ctx, dec_seq)

        u_ctx = p[:n_ctx, C_S5:C_S5 + S5_CH].reshape(batch, seq_len, S5_CH)
        u_lat = p[n_ctx:, C_S5:C_S5 + S5_CH].reshape(dec_batch, dec_seq, S5_CH)
        ylin, s5_fins = _s5_core([u_ctx, u_lat], [None, state_s5[:, l]], s5_tab)

        qo, ko, vo, ckv = _mla_prep(p, mla_tab, mla_q_norm_g[l][None, :], mla_kv_norm_g[l][None, :], wuq, wukv)
        ckv_c = cache_mla_ckv[:, l].reshape(dec_batch * past, MLA_KV_RANK)
        kv_c = _matmul(ckv_c, wukv, out_dtype=bf).reshape(dec_batch * past, MLA_HEADS, 2, 128)
        kpe_c = jnp.broadcast_to(cache_mla_kpe[:, l].reshape(dec_batch * past, 1, MLA_ROPE).astype(bf),
                                 (dec_batch * past, MLA_HEADS, MLA_ROPE))
        k_c = jnp.concatenate([kv_c[:, :, 0], kpe_c, jnp.zeros_like(kpe_c)], axis=-1).reshape(
            dec_batch * past, MLA_HEADS * ATT_DQ)
        v_c = kv_c[:, :, 1].reshape(dec_batch * past, MLA_HEADS * MLA_V)
        ymla = (_attention(qo, ko, vo, 0, batch, seq_len),
                _attention(qo, ko, vo, n_ctx, dec_batch, dec_seq, k_c, v_c))

        ylru_c, lru_fin = _lru(p, 0, batch, seq_len, jnp.zeros((batch, 2, LRU_W), f32), lru_tab)
        ylru_l, _ = _lru(p, n_ctx, dec_batch, dec_seq, state_lru[:, l].astype(f32), lru_tab)
        ylru = (ylru_c, ylru_l)

        yret_c, ret_fin = _retention(p, 0, batch, seq_len,
                                     jnp.zeros((batch, 2, RET_HEADS, RET_DK, RET_DV), f32), ret_tables)
        yret_l, _ = _retention(p, n_ctx, dec_batch, dec_seq, state_ret[:, l].astype(f32), ret_tables, ret_rope)
        yret = (yret_c, yret_l)

        x_mid, h2, idx, wgt = _mixout(ylin, p, ymla, ylru, yret, x, mods, s5_d[l][None, :],
                                      s5_w_glu[l].astype(bf), w_out[l].astype(bf), norm2_g[l][None, :], rw,
                                      n_ctx, dec_seq)
        y2 = _moe_rows(h2, l, idx, wgt, moe_w_gate, moe_w_up, moe_w_down)
        x = _resid(x_mid, y2, mods, gf, n_ctx, dec_seq, final=(l == depth - 1))

        states.append((ckv[:n_ctx].reshape(batch, seq_len, MLA_KV_RANK),
                       p[:n_ctx, C_KR:C_KR + MLA_ROPE].astype(f32).reshape(batch, seq_len, MLA_ROPE),
                       s5_fins[0], lru_fin, ret_fin))

    y_prompt = x[0].reshape(batch, seq_len, d)
    y_sample = x[1].reshape(dec_batch, dec_seq, d)
    new_cache_mla_ckv = jnp.stack([st[0] for st in states], axis=1)
    new_cache_mla_kpe = jnp.stack([st[1] for st in states], axis=1)
    new_state_s5 = jnp.stack([st[2] for st in states], axis=1)
    new_state_lru = jnp.stack([st[3] for st in states], axis=1)
    new_state_ret = jnp.stack([st[4] for st in states], axis=1)
    return (y_prompt, y_sample, new_cache_mla_ckv, new_cache_mla_kpe, new_state_s5, new_state_lru, new_state_ret)
```

```python
import functools
import numpy as np
import jax
import jax.numpy as jnp
from jax import lax
from jax.experimental import pallas as pl
from jax.experimental.pallas import tpu as pltpu

D_MODEL = 2048
DEPTH = 2
GRID_W = 64
EPS = 1e-6
ROPE_BASE = 10000.0
N_MOD = 6
GROUP_W = 512
S5_CH = GROUP_W
S5_GROUP_CH = 16
S5_GROUPS = S5_CH // S5_GROUP_CH
S5_STATE = 64
MLA_HEADS = 4
MLA_NOPE = 128
MLA_ROPE = 64
MLA_V = 128
MLA_Q_RANK = GROUP_W
MLA_KV_RANK = GROUP_W // 2
MLA_SCALE = (MLA_NOPE + MLA_ROPE) ** -0.5
LRU_W = GROUP_W
LRU_CONV = 4
LRU_C = 8.0
RET_HEADS = 4
RET_DK = 128
RET_DV = 128
RET_CHUNK = 128
N_EXPERTS = 16
N_EXPERT_GROUPS = 4
EXPERTS_PER_GROUP = N_EXPERTS // N_EXPERT_GROUPS
TOP_K = 2
D_EXPERT = D_MODEL // 4

V7X_VMEM_LIMIT = 48 * 1024 * 1024
V7X_VMEM_LIMIT_BIG = 56 * 1024 * 1024

C_S5, C_MQ, C_LX, C_LG, C_RQ, C_RK, C_RV, C_RG, C_RQP, C_RKP, C_MKV, C_KR = (
    0, 512, 1024, 1536, 2048, 2560, 3072, 3584, 4096, 4608, 5120, 5376)
P_COLS = 5632
R_S5, R_MQ, R_MKV, R_KR, R_LX, R_LG, R_RQ, R_RK, R_RV, R_RG = (0, 512, 1024, 1280, 1344, 1856, 2368, 2880, 3392, 3904)


def _cparams(n_axes, limit=V7X_VMEM_LIMIT):
    return pltpu.CompilerParams(dimension_semantics=("arbitrary",) * n_axes, vmem_limit_bytes=limit)


def _sigmoid(x):
    return 1.0 / (1.0 + jnp.exp(-x))


def _gelu_tanh(x):
    return 0.5 * x * (1.0 + jnp.tanh(0.7978845608028654 * (x + 0.044715 * (x * x * x))))


def _rms(x):
    return x * lax.rsqrt(jnp.mean(x * x, axis=-1, keepdims=True) + EPS)


def _seq_of_tile(i, tm, n_ctx, dec_seq):
    assert n_ctx % tm == 0 and dec_seq % tm == 0, "a row tile must not straddle two sequences"
    r = i * tm
    return jnp.where(r < n_ctx, 0, 1 + (r - n_ctx) // dec_seq)


def _mm_kernel(x_ref, w_ref, o_ref):
    o_ref[...] = jnp.dot(x_ref[...].astype(jnp.bfloat16), w_ref[...].astype(jnp.bfloat16),
                         preferred_element_type=jnp.float32).astype(o_ref.dtype)


def _matmul(x, w, tm=512, tn=512, out_dtype=jnp.float32):
    m, k = x.shape
    _, n = w.shape
    tm, tn = min(tm, m), min(tn, n)
    return pl.pallas_call(
        _mm_kernel,
        grid=(n // tn, m // tm),
        in_specs=[pl.BlockSpec((tm, k), lambda j, i: (i, 0)), pl.BlockSpec((k, tn), lambda j, i: (0, j))],
        out_specs=pl.BlockSpec((tm, tn), lambda j, i: (i, j)),
        out_shape=jax.ShapeDtypeStruct((m, n), out_dtype),
        compiler_params=_cparams(2),
        name="matmul",
    )(x, w)


ADA_TN = 1024


def _ada_kernel(c_ref, w_ref, b_ref, o_ref):
    c = c_ref[...]
    s = (c * _sigmoid(c)).astype(jnp.bfloat16)
    o_ref[0] = jnp.dot(s, w_ref[0].astype(jnp.bfloat16), preferred_element_type=jnp.float32) + b_ref[0]


def _ada(cvec, w_ada, b_ada):
    depth, d, n = w_ada.shape
    return pl.pallas_call(
        _ada_kernel,
        grid=(depth, n // ADA_TN),
        in_specs=[pl.BlockSpec((8, d), lambda l, j: (0, 0)),
                  pl.BlockSpec((1, d, ADA_TN), lambda l, j: (l, 0, j)),
                  pl.BlockSpec((1, 1, ADA_TN), lambda l, j: (l, 0, j))],
        out_specs=pl.BlockSpec((1, 8, ADA_TN), lambda l, j: (l, 0, j)),
        out_shape=jax.ShapeDtypeStruct((depth, 8, n), jnp.float32),
        compiler_params=_cparams(2),
        name="adaln_mod",
    )(cvec, w_ada, b_ada.reshape(depth, 1, n))


IN_TM = 1024
IN_TN = 512


def _inproj_kernel(x_ref, mod_ref, g_ref, w_ref, o_ref, h_s):
    @pl.when(pl.program_id(1) == 0)
    def _():
        h = _rms(x_ref[...]) * g_ref[...]
        h_s[...] = (h * (1.0 + mod_ref[0, 1:2, :]) + mod_ref[0, 0:1, :]).astype(h_s.dtype)

    o_ref[...] = jnp.dot(h_s[...], w_ref[...], preferred_element_type=jnp.float32).astype(o_ref.dtype)


def _inproj(x, mods, g, w, n_ctx, dec_seq):
    n, d = x.shape
    seq = functools.partial(_seq_of_tile, tm=IN_TM, n_ctx=n_ctx, dec_seq=dec_seq)
    return pl.pallas_call(
        _inproj_kernel,
        grid=(n // IN_TM, P_COLS // IN_TN),
        in_specs=[pl.BlockSpec((IN_TM, d), lambda i, j: (i, 0)),
                  pl.BlockSpec((1, N_MOD, d), lambda i, j: (seq(i), 0, 0)),
                  pl.BlockSpec((1, d), lambda i, j: (0, 0)),
                  pl.BlockSpec((d, IN_TN), lambda i, j: (0, j))],
        out_specs=pl.BlockSpec((IN_TM, IN_TN), lambda i, j: (i, j)),
        out_shape=jax.ShapeDtypeStruct((n, P_COLS), jnp.bfloat16),
        scratch_shapes=[pltpu.VMEM((IN_TM, d), jnp.bfloat16)],
        compiler_params=_cparams(2),
        name="norm_inproj",
    )(x, mods, g, w)


def _rot_partner_cols(w, n_heads, head_dim):
    q = head_dim // 4
    wr = w.reshape(w.shape[0], n_heads, 2, 2, q)
    return jnp.stack([-wr[:, :, :, 1], wr[:, :, :, 0]], axis=3).reshape(w.shape)


def _inproj_weights(w_in):
    seg = lambda off, width: w_in[:, off:off + width]
    kr = seg(R_KR, MLA_ROPE)
    cols = [seg(R_S5, 2 * GROUP_W),
            seg(R_LX, 6 * GROUP_W),
            _rot_partner_cols(seg(R_RQ, 2 * GROUP_W), 2 * RET_HEADS, RET_DK),
            seg(R_MKV, MLA_KV_RANK), kr, _rot_partner_cols(kr, 1, MLA_ROPE),
            jnp.zeros((w_in.shape[0], P_COLS - C_KR - 2 * MLA_ROPE), w_in.dtype)]
    return jnp.concatenate(cols, axis=1).astype(jnp.bfloat16)


def _rope_tables(t_len, rot_dim):
    rows = t_len // GRID_W
    row = jnp.repeat(jnp.arange(rows, dtype=jnp.float32), GRID_W)
    col = jnp.tile(jnp.arange(GRID_W, dtype=jnp.float32), rows)
    n_freq = rot_dim // 4
    inv = ROPE_BASE ** (-jnp.arange(n_freq, dtype=jnp.float32) / n_freq)
    ang = jnp.concatenate([row[:, None] * inv[None]] * 2 + [col[:, None] * inv[None]] * 2, axis=1)
    return jnp.cos(ang), jnp.sin(ang)


S5_L = 32
S5_PAIRS = S5_GROUPS // 2
S5_Q = 4
S5_LANES = S5_GROUPS * S5_STATE
S5_SCAN_LANES = 1024


def _s5_tables(a_re, a_im, log_dt, b_re, b_im, c_re, c_im):
    f32 = jnp.float32
    L, G, P, C = S5_L, S5_GROUPS, S5_STATE, S5_GROUP_CH
    hp = lax.Precision.HIGHEST
    cmul = lambda xr, xi, yr, yi: (xr * yr - xi * yi, xr * yi + xi * yr)
    dt = jnp.exp(log_dt.astype(f32))[..., None]
    zr, zi = a_re * dt, a_im * dt
    ab_r, ab_i = jnp.exp(zr) * jnp.cos(zi), jnp.exp(zr) * jnp.sin(zi)
    den = a_re * a_re + a_im * a_im
    nr, ni = ab_r - 1.0, ab_i
    be_r = (nr * a_re + ni * a_im) / den
    be_i = (ni * a_re - nr * a_im) / den
    bt_r, bt_i = b_re.transpose(0, 2, 1), b_im.transpose(0, 2, 1)
    bp_r, bp_i = cmul(be_r[:, :, None, :], be_i[:, :, None, :], bt_r[None], bt_i[None])
    tau = jnp.arange(L + 1, dtype=f32)[None, None, :, None]
    mag = jnp.exp(zr[:, :, None, :] * tau)
    pw_r, pw_i = mag * jnp.cos(zi[:, :, None, :] * tau), mag * jnp.sin(zi[:, :, None, :] * tau)

    pws = lambda d, rev: tuple(x[d][:, :L][:, ::-1] if rev else x[d][:, :L] for x in (pw_r, pw_i))
    coef = lambda d, rev: cmul(*(x[:, :, None, :] for x in pws(d, rev)), bp_r[d][:, None], bp_i[d][:, None])
    ws = jnp.stack(coef(0, True) + coef(1, False), axis=3).reshape(G, L * C, S5_Q * P)

    ct_r, ct_i = c_re.transpose(0, 2, 1), c_im.transpose(0, 2, 1)
    pt_r, pt_i = pw_r.transpose(0, 1, 3, 2), pw_i.transpose(0, 1, 3, 2)

    def cm(er, ei):
        m_r, m_i = cmul(ct_r[:, :, None, :], ct_i[:, :, None, :], er[..., None], ei[..., None])
        return m_r.reshape(G, P, L * C), m_i.reshape(G, P, L * C)

    mf_r, mf_i = cm(pt_r[0][:, :, 1:L + 1], pt_i[0][:, :, 1:L + 1])
    mb_r, mb_i = cm(pt_r[1][:, :, 1:L + 1][:, :, ::-1], pt_i[1][:, :, 1:L + 1][:, :, ::-1])
    wo = jnp.stack([mf_r, -mf_i, mb_r, -mb_i], axis=1).reshape(G, S5_Q * P, L * C)

    def impulse(d):
        m_r, m_i = cm(pt_r[d][:, :, :L], pt_i[d][:, :, :L])
        kk = (jnp.einsum('gkp,gpx->gkx', bp_r[d], m_r, precision=hp)
              - jnp.einsum('gkp,gpx->gkx', bp_i[d], m_i, precision=hp))
        return kk.reshape(G, C, L, C)

    kf, kb = impulse(0), impulse(1)
    wide = jnp.concatenate([kb[:, :, :0:-1], kf[:, :, :1] + kb[:, :, :1], kf[:, :, 1:]], axis=2)
    wide = jnp.pad(wide.reshape(G, C, (2 * L - 1) * C), ((0, 0), (0, 0), (0, C)))

    a_l = jnp.stack([pw_r[0, :, L], pw_i[0, :, L], pw_r[1, :, L], pw_i[1, :, L]], axis=0).reshape(S5_Q, 1, S5_LANES)
    bf = jnp.bfloat16
    return _s5_toeplitz(wide), ws.astype(bf), wo.astype(bf), a_l


def _s5_toeplitz_kernel(w_ref, o_ref):
    w = w_ref[0]
    for s in range(S5_L):
        off = (S5_L - 1 - s) * S5_GROUP_CH
        o_ref[0, s * S5_GROUP_CH:(s + 1) * S5_GROUP_CH, :] = w[:, off:off + S5_L * S5_GROUP_CH].astype(o_ref.dtype)


def _s5_toeplitz(wide):
    g, c, wl = wide.shape
    n = S5_L * S5_GROUP_CH
    return pl.pallas_call(
        _s5_toeplitz_kernel,
        grid=(g,),
        in_specs=[pl.BlockSpec((1, c, wl), lambda i: (i, 0, 0))],
        out_specs=pl.BlockSpec((1, n, n), lambda i: (i, 0, 0)),
        out_shape=jax.ShapeDtypeStruct((g, n, n), jnp.bfloat16),
        compiler_params=_cparams(1),
        name="s5_toeplitz",
    )(wide)


def _s5a_kernel(x_ref, t_ref, ws_ref, y1_ref, s_ref):
    ss = []
    for j in range(2):
        x = x_ref[j]
        y1_ref[j] = jnp.dot(x, t_ref[j], preferred_element_type=jnp.float32)
        ss.append(jnp.dot(x, ws_ref[j], preferred_element_type=jnp.float32))
    p = S5_STATE
    for q in range(S5_Q):
        s_ref[q] = jnp.concatenate([ss[0][:, q * p:(q + 1) * p], ss[1][:, q * p:(q + 1) * p]], axis=1)


def _s5a(x, toe, ws):
    g, r, w = x.shape
    return pl.pallas_call(
        _s5a_kernel,
        grid=(S5_PAIRS,),
        in_specs=[pl.BlockSpec((2, r, w), lambda i: (i, 0, 0)),
                  pl.BlockSpec((2, w, w), lambda i: (i, 0, 0)),
                  pl.BlockSpec((2, w, S5_Q * S5_STATE), lambda i: (i, 0, 0))],
        out_specs=[pl.BlockSpec((2, r, w), lambda i: (i, 0, 0)),
                   pl.BlockSpec((S5_Q, r, 128), lambda i: (0, 0, i))],
        out_shape=[jax.ShapeDtypeStruct((g, r, w), jnp.float32),
                   jax.ShapeDtypeStruct((S5_Q, r, S5_LANES), jnp.float32)],
        compiler_params=_cparams(1),
        name="s5_chunk_local",
    )(x, toe, ws)


def _s5b_kernel(s_ref, a_ref, h0_ref, hp_ref, fin_ref, *, nc):
    def run(qr, qi, order_fwd):
        ar, ai = a_ref[qr], a_ref[qi]

        def body(i, carry):
            hr, hi = carry
            k = i if order_fwd else nc - 1 - i
            hp_ref[qr, pl.ds(k, 1), :] = hr
            hp_ref[qi, pl.ds(k, 1), :] = hi
            nr = ar * hr - ai * hi + s_ref[qr, pl.ds(k, 1), :]
            ni = ar * hi + ai * hr + s_ref[qi, pl.ds(k, 1), :]
            return nr, ni

        hr, hi = lax.fori_loop(0, nc, body, (h0_ref[qr], h0_ref[qi]))
        fin_ref[qr] = hr
        fin_ref[qi] = hi

    run(0, 1, True)
    run(2, 3, False)


def _s5b(s, a_l, h0):
    _, nc, w = s.shape
    tl = S5_SCAN_LANES
    vec = pl.BlockSpec((S5_Q, 1, tl), lambda i: (0, 0, i))
    seq = pl.BlockSpec((S5_Q, nc, tl), lambda i: (0, 0, i))
    return pl.pallas_call(
        functools.partial(_s5b_kernel, nc=nc),
        grid=(w // tl,),
        in_specs=[seq, vec, vec],
        out_specs=[seq, vec],
        out_shape=[jax.ShapeDtypeStruct((S5_Q, nc, w), jnp.float32),
                   jax.ShapeDtypeStruct((S5_Q, 1, w), jnp.float32)],
        compiler_params=_cparams(1),
        name="s5_chunk_scan",
    )(s, a_l, h0)


def _s5c_kernel(h_ref, wo_ref, y1_ref, y_ref):
    p = S5_STATE
    for j in range(2):
        hcat = jnp.concatenate([h_ref[q][:, j * p:(j + 1) * p] for q in range(S5_Q)], axis=1)
        y2 = jnp.dot(hcat.astype(jnp.bfloat16), wo_ref[j], preferred_element_type=jnp.float32)
        y_ref[j] = (y1_ref[j] + y2).astype(y_ref.dtype)


def _s5c(hprev, wo, y1):
    g, r, w = y1.shape
    return pl.pallas_call(
        _s5c_kernel,
        grid=(S5_PAIRS,),
        in_specs=[pl.BlockSpec((S5_Q, r, 128), lambda i: (0, 0, i)),
                  pl.BlockSpec((2, S5_Q * S5_STATE, w), lambda i: (i, 0, 0)),
                  pl.BlockSpec((2, r, w), lambda i: (i, 0, 0))],
        out_specs=pl.BlockSpec((2, r, w), lambda i: (i, 0, 0)),
        out_shape=jax.ShapeDtypeStruct((g, r, w), jnp.bfloat16),
        compiler_params=_cparams(1),
        name="s5_state_to_out",
    )(hprev, wo, y1)


def _s5_to_chunks(u):
    b, t, _ = u.shape
    nc = t // S5_L
    x = u.reshape(b, nc, S5_L, S5_GROUPS, S5_GROUP_CH).transpose(3, 1, 0, 2, 4)
    return x.reshape(S5_GROUPS, nc * b, S5_L * S5_GROUP_CH)


def _s5_from_chunks(y, b, t):
    nc = t // S5_L
    y = y.reshape(S5_GROUPS, nc, b, S5_L, S5_GROUP_CH).transpose(2, 1, 3, 0, 4)
    return y.reshape(b * t, S5_CH)


def _s5_state_planes(h0):
    b = h0.shape[0]
    return h0.transpose(1, 4, 0, 2, 3).reshape(S5_Q, b, S5_LANES)


def _s5_core(us, h0s, tables):
    toe, ws_pair, wo_pair, a_l = tables
    xs = [_s5_to_chunks(u) for u in us]
    rows = [x.shape[1] for x in xs]
    y1, s = _s5a(jnp.concatenate(xs, axis=1), toe, ws_pair)
    hps, fins = [], []
    off = 0
    for u, h0, r in zip(us, h0s, rows):
        b, t, _ = u.shape
        nc = t // S5_L
        if h0 is None:
            h0p = jnp.zeros((S5_Q, 1, b * S5_LANES), jnp.float32)
        else:
            h0p = _s5_state_planes(h0.astype(jnp.float32)).reshape(S5_Q, 1, b * S5_LANES)
        hp, fin = _s5b(s[:, off:off + r].reshape(S5_Q, nc, b * S5_LANES), jnp.tile(a_l, (1, 1, b)), h0p)
        hps.append(hp.reshape(S5_Q, r, S5_LANES))
        fins.append(fin.reshape(2, 2, b, S5_GROUPS, S5_STATE).transpose(2, 0, 3, 4, 1))
        off += r
    y = _s5c(jnp.concatenate(hps, axis=1), wo_pair, y1)
    outs, off = [], 0
    for u, r in zip(us, rows):
        b, t, _ = u.shape
        outs.append(_s5_from_chunks(y[:, off:off + r], b, t))
        off += r
    return jnp.concatenate(outs, axis=0), fins


LRU_TC = 256
LRU_HALO = 16


def _lru_kernel(x_ref, gate_ref, cw_ref, cb_ref, wg_ref, bg_ref, sp_ref, h0_ref, out_ref, fin_ref,
                a_s, b_s, hf_s, *, t_len, tc):
    f32 = jnp.float32
    nt = t_len // tc
    w = LRU_W
    nb = tc // 8
    row = lax.broadcasted_iota(jnp.int32, (nb, 8, w), 1)

    def gates(c, d):
        r0 = pl.multiple_of(c * tc, tc)
        lo = pl.multiple_of(jnp.maximum(r0 - LRU_HALO, 0), LRU_HALO)
        hi = pl.multiple_of(jnp.minimum(r0 + tc, t_len - LRU_HALO), LRU_HALO)
        prev = jnp.where(c > 0, x_ref[pl.ds(lo, LRU_HALO), :].astype(f32), 0.0)
        nxt = jnp.where(c < nt - 1, x_ref[pl.ds(hi, LRU_HALO), :].astype(f32), 0.0)
        slab = jnp.concatenate([prev, x_ref[pl.ds(r0, tc), :].astype(f32), nxt], axis=0)
        o = LRU_HALO - LRU_CONV // 2
        xc = cb_ref[...] + sum(cw_ref[k:k + 1, :] * slab[o + k:o + k + tc] for k in range(LRU_CONV))
        g = jnp.dot(xc.astype(jnp.bfloat16), wg_ref[d], preferred_element_type=f32) + bg_ref[d]
        r = _sigmoid(g[:, :w])
        i = _sigmoid(g[:, w:])
        log_a = -sp_ref[d] * r
        a = jnp.exp(log_a).reshape(nb, 8, w)
        b = (jnp.sqrt(1.0 - jnp.exp(2.0 * log_a)) * (i * xc)).reshape(nb, 8, w)
        for sh in (1, 2, 4):
            keep, rot = (row >= sh, sh) if d == 0 else (row < 8 - sh, 8 - sh)
            ap = jnp.where(keep, pltpu.roll(a, rot, 1), 1.0)
            bp = jnp.where(keep, pltpu.roll(b, rot, 1), 0.0)
            b = a * bp + b
            a = a * ap
        a_s[...] = a
        b_s[...] = b
        return r0

    def fwd_chunk(c, h):
        r0 = gates(c, 0)

        def block(k, h):
            h8 = a_s[k] * h + b_s[k]
            hf_s[pl.ds(pl.multiple_of(r0 + k * 8, 8), 8), :] = h8
            return h8[7:8, :]

        return lax.fori_loop(0, nb, block, h, unroll=4)

    h = lax.fori_loop(0, nt, fwd_chunk, h0_ref[0, 0:1, :])
    fin_ref[0, 0:1, :] = h

    def bwd_chunk(ci, h):
        r0 = gates(nt - 1 - ci, 1)

        def block(i, h):
            k = nb - 1 - i
            h8 = a_s[k] * h + b_s[k]
            rows = pl.ds(pl.multiple_of(r0 + k * 8, 8), 8)
            hf_s[rows, :] = hf_s[rows, :] + h8
            return h8[0:1, :]

        h = lax.fori_loop(0, nb, block, h, unroll=4)
        sl = pl.ds(r0, tc)
        out_ref[sl, :] = (hf_s[sl, :] * _gelu_tanh(gate_ref[sl, :].astype(f32))).astype(out_ref.dtype)
        return h

    h = lax.fori_loop(0, nt, bwd_chunk, h0_ref[0, 1:2, :])
    fin_ref[0, 1:2, :] = h


def _block_diag(wb):
    n, k, j = wb.shape
    return (wb[:, :, None, :] * jnp.eye(n, dtype=wb.dtype)[:, None, :, None]).reshape(n * k, n * j)


def _lru_tables(conv_w, conv_b, w_a, b_a, w_x, b_x, lam):
    wg = jnp.stack([jnp.concatenate([_block_diag(w_a[d]), _block_diag(w_x[d])], axis=1) for d in range(2)])
    bg = jnp.concatenate([b_a, b_x], axis=-1)[:, None, :]
    sp = (LRU_C * jax.nn.softplus(-lam.astype(jnp.float32)))[:, None, :]
    return conv_w, conv_b[None, :], wg.astype(jnp.bfloat16), bg, sp


def _lru(p, row0, b, t, h0, tables):
    w = LRU_W
    tc = min(LRU_TC, t)
    cw, cb, wg, bg, sp = tables
    full = lambda a: pl.BlockSpec(a.shape, lambda i: (0,) * a.ndim)
    rb = row0 // t
    return pl.pallas_call(
        functools.partial(_lru_kernel, t_len=t, tc=tc),
        grid=(b,),
        in_specs=[pl.BlockSpec((t, w), lambda i: (rb + i, C_LX // w)),
                  pl.BlockSpec((t, w), lambda i: (rb + i, C_LG // w)),
                  full(cw), full(cb), full(wg), full(bg), full(sp),
                  pl.BlockSpec((1, 2, w), lambda i: (i, 0, 0))],
        out_specs=[pl.BlockSpec((t, w), lambda i: (i, 0)),
                   pl.BlockSpec((1, 2, w), lambda i: (i, 0, 0))],
        out_shape=[jax.ShapeDtypeStruct((b * t, w), jnp.bfloat16),
                   jax.ShapeDtypeStruct((b, 2, w), jnp.float32)],
        scratch_shapes=[pltpu.VMEM((tc // 8, 8, w), jnp.float32), pltpu.VMEM((tc // 8, 8, w), jnp.float32),
                        pltpu.VMEM((t, w), jnp.float32)],
        compiler_params=_cparams(1),
        name="rglru",
    )(p, p, cw, cb, wg, bg, sp, h0)


MLA_TM = 512
ATT_DQ = 256


def _mla_prep_kernel(q_ref, kv_ref, kr_ref, tab_ref, gq_ref, gkv_ref, wuq_ref, wukv_ref,
                     qo_ref, ko_ref, vo_ref, ckv_ref):
    f32, bf = jnp.float32, jnp.bfloat16
    tab = tab_ref[...]

    def rope(blk):
        prod = blk * tab
        return prod + pltpu.roll(prod, 64, 1)

    qn = (_rms(q_ref[...].astype(f32)) * gq_ref[...]).astype(bf)
    qq = jnp.dot(qn, wuq_ref[...], preferred_element_type=f32)
    ckv = _rms(kv_ref[...].astype(f32)) * gkv_ref[...]
    ckv_ref[...] = ckv
    kk = jnp.dot(ckv.astype(bf), wukv_ref[...], preferred_element_type=f32)
    lane = lax.broadcasted_iota(jnp.int32, tab.shape, 1)
    kpe = jnp.where(lane < MLA_ROPE, rope(kr_ref[...].astype(f32)), 0.0).astype(bf)
    for h in range(MLA_HEADS):
        o = h * ATT_DQ
        qo_ref[:, o:o + 128] = (qq[:, o:o + 128] * MLA_SCALE).astype(bf)
        qo_ref[:, o + 128:o + 256] = (rope(qq[:, o + 128:o + 256]) * MLA_SCALE).astype(bf)
        ko_ref[:, o:o + 128] = kk[:, o:o + 128].astype(bf)
        ko_ref[:, o + 128:o + 256] = kpe
        vo_ref[:, h * MLA_V:(h + 1) * MLA_V] = kk[:, o + 128:o + 256].astype(bf)


def _mla_weights(w_uq, w_ukv):
    wq = w_uq.reshape(MLA_Q_RANK, MLA_HEADS, MLA_NOPE + MLA_ROPE)
    pe = wq[:, :, MLA_NOPE:].reshape(MLA_Q_RANK, MLA_HEADS * MLA_ROPE)
    pep = _rot_partner_cols(pe, MLA_HEADS, MLA_ROPE).reshape(MLA_Q_RANK, MLA_HEADS, MLA_ROPE)
    wq_ext = jnp.concatenate([wq, pep], axis=-1).reshape(MLA_Q_RANK, MLA_HEADS * ATT_DQ)
    return wq_ext.astype(jnp.bfloat16), w_ukv.astype(jnp.bfloat16)


def _mla_prep(p, tab, gq, gkv, wuq, wukv):
    n = p.shape[0]
    tm = MLA_TM
    full = lambda a: pl.BlockSpec(a.shape, lambda i: (0,) * a.ndim)
    row = lambda width, col: pl.BlockSpec((tm, width), lambda i: (i, col // width))
    return pl.pallas_call(
        _mla_prep_kernel,
        grid=(n // tm,),
        in_specs=[row(MLA_Q_RANK, C_MQ), row(MLA_KV_RANK, C_MKV), row(128, C_KR), row(128, 0),
                  full(gq), full(gkv), full(wuq), full(wukv)],
        out_specs=[row(MLA_HEADS * ATT_DQ, 0), row(MLA_HEADS * ATT_DQ, 0), row(MLA_HEADS * MLA_V, 0),
                   row(MLA_KV_RANK, 0)],
        out_shape=[jax.ShapeDtypeStruct((n, MLA_HEADS * ATT_DQ), jnp.bfloat16),
                   jax.ShapeDtypeStruct((n, MLA_HEADS * ATT_DQ), jnp.bfloat16),
                   jax.ShapeDtypeStruct((n, MLA_HEADS * MLA_V), jnp.bfloat16),
                   jax.ShapeDtypeStruct((n, MLA_KV_RANK), jnp.float32)],
        compiler_params=_cparams(1),
        name="mla_prep",
    )(p, p, p, tab, gq, gkv, wuq, wukv)


ATT_TQ = 512
ATT_SUB = 256


def _attn_kernel(*refs, two, n_sub):
    dn = (((1,), (1,)), ((), ()))
    f32 = jnp.float32
    if two:
        q_ref, k_ref, v_ref, k2_ref, v2_ref, o_ref = refs
    else:
        q_ref, k_ref, v_ref, o_ref = refs
    rows = q_ref.shape[0] // n_sub
    for part in range(n_sub):
        sl = slice(part * rows, (part + 1) * rows)
        q = q_ref[sl, :]
        s = lax.dot_general(q, k_ref[...], dn, preferred_element_type=f32)
        m = jnp.max(s, axis=-1, keepdims=True)
        if two:
            s2 = lax.dot_general(q, k2_ref[...], dn, preferred_element_type=f32)
            m = jnp.maximum(m, jnp.max(s2, axis=-1, keepdims=True))
        p = jnp.exp(s - m)
        l = jnp.sum(p, axis=-1, keepdims=True)
        o = jnp.dot(p.astype(jnp.bfloat16), v_ref[...], preferred_element_type=f32)
        if two:
            p2 = jnp.exp(s2 - m)
            l = l + jnp.sum(p2, axis=-1, keepdims=True)
            o = o + jnp.dot(p2.astype(jnp.bfloat16), v2_ref[...], preferred_element_type=f32)
        o_ref[sl, :] = (o / l).astype(o_ref.dtype)


def _attention(q, k, v, row0, b, t, k2=None, v2=None):
    tb = min(ATT_TQ, t)
    nq = t // tb
    qb, kb = row0 // tb, row0 // t
    two = k2 is not None
    in_specs = [pl.BlockSpec((tb, ATT_DQ), lambda bi, h, i: (qb + bi * nq + i, h)),
                pl.BlockSpec((t, ATT_DQ), lambda bi, h, i: (kb + bi, h)),
                pl.BlockSpec((t, MLA_V), lambda bi, h, i: (kb + bi, h))]
    args = [q, k, v]
    if two:
        t2 = k2.shape[0] // b
        in_specs += [pl.BlockSpec((t2, ATT_DQ), lambda bi, h, i: (bi, h)),
                     pl.BlockSpec((t2, MLA_V), lambda bi, h, i: (bi, h))]
        args += [k2, v2]
    return pl.pallas_call(
        functools.partial(_attn_kernel, two=two, n_sub=max(tb // ATT_SUB, 1)),
        grid=(b, MLA_HEADS, nq),
        in_specs=in_specs,
        out_specs=pl.BlockSpec((tb, MLA_V), lambda bi, h, i: (bi * nq + i, h)),
        out_shape=jax.ShapeDtypeStruct((b * t, MLA_HEADS * MLA_V), jnp.bfloat16),
        compiler_params=_cparams(3),
        name="mla_attention",
    )(*args)


def _ret_tables():
    f32 = jnp.float32
    c = RET_CHUNK
    log_g = jnp.log1p(-jnp.exp2(-5.0 - jnp.arange(RET_HEADS, dtype=f32)))[:, None, None]
    idx = jnp.arange(c, dtype=f32)
    dec = jnp.exp(jnp.abs(idx[:, None] - idx[None, :])[None] * log_g)
    row = lambda e: jnp.broadcast_to(jnp.exp(e[None, :, None] * log_g), (RET_HEADS, c, c))
    return jnp.stack([dec, row(idx + 1.0), row(c - idx), row(c - 1.0 - idx), row(idx)], axis=0)


RET_HPS = 2


def _ret_kernel(*refs, nc, rope):
    f32, bf = jnp.float32, jnp.bfloat16
    if rope:
        (q_ref, k_ref, v_ref, g_ref, qp_ref, kp_ref, cos_ref, sin_ref, tab_ref, s0_ref, o_ref, fin_ref,
         acc_s, st_s) = refs
    else:
        q_ref, k_ref, v_ref, g_ref, tab_ref, s0_ref, o_ref, fin_ref, acc_s, st_s = refs
    c, dk = RET_CHUNK, RET_DK
    nt_dims, tn_dims = (((1,), (1,)), ((), ())), (((0,), (0,)), ((), ()))

    def chunk(k0, hh):
        sl = pl.ds(pl.multiple_of(k0 * c, c), c)
        cols = slice(hh * dk, (hh + 1) * dk)
        qc, kc = q_ref[sl, cols].astype(f32), k_ref[sl, cols].astype(f32)
        if rope:
            cos, sin = cos_ref[sl, :], sin_ref[sl, :]
            qc = qc * cos + qp_ref[sl, cols].astype(f32) * sin
            kc = kc * cos + kp_ref[sl, cols].astype(f32) * sin
        return sl, cols, qc.astype(bf), kc * (RET_DK ** -0.5), v_ref[sl, cols]

    st_s[...] = s0_ref[0]
    acc_s[...] = jnp.zeros_like(acc_s)

    def body(i, carry):
        for hh in range(RET_HPS):
            dec, xif, xib, zf, zb = (tab_ref[t, hh] for t in range(5))
            g_chunk = xif[c - 1:c, :]
            sl, cols, qc, kc, vc = chunk(i, hh)
            s = st_s[0, hh]
            sc = lax.dot_general(qc, kc.astype(bf), nt_dims, preferred_element_type=f32) * dec
            acc_s[sl, cols] += (jnp.dot(sc.astype(bf), vc, preferred_element_type=f32)
                                + jnp.dot(qc, s.astype(bf), preferred_element_type=f32) * xif)
            st_s[0, hh] = g_chunk * s + lax.dot_general((kc * zf).astype(bf), vc, tn_dims,
                                                        preferred_element_type=f32)
            sl, cols, qc, kc, vc = chunk(nc - 1 - i, hh)
            s = st_s[1, hh]
            acc_s[sl, cols] += jnp.dot(qc, s.astype(bf), preferred_element_type=f32) * xib
            st_s[1, hh] = g_chunk * s + lax.dot_general((kc * zb).astype(bf), vc, tn_dims,
                                                        preferred_element_type=f32)
        return carry

    lax.fori_loop(0, nc, body, 0)
    fin_ref[0] = st_s[...]

    def finish(i, carry):
        sl = pl.ds(pl.multiple_of(i * c, c), c)
        for hh in range(RET_HPS):
            cols = slice(hh * dk, (hh + 1) * dk)
            g = g_ref[sl, cols].astype(f32)
            o_ref[sl, cols] = (_rms(acc_s[sl, cols]) * (g * _sigmoid(g))).astype(o_ref.dtype)
        return carry

    lax.fori_loop(0, nc, finish, 0)


def _retention(p, row0, b, t, s0, tables, rope_tabs=None):
    nc = t // RET_CHUNK
    rb = row0 // t
    rope = rope_tabs is not None
    wd = RET_HPS * RET_DK
    col = lambda c0: pl.BlockSpec((t, wd), lambda bi, h: (rb + bi, c0 // wd + h))
    st = pl.BlockSpec((1, 2, RET_HPS, RET_DK, RET_DV), lambda bi, h: (bi, 0, h, 0, 0))
    in_specs = [col(C_RQ), col(C_RK), col(C_RV), col(C_RG)]
    args = [p, p, p, p]
    if rope:
        tab = pl.BlockSpec((t, RET_DK), lambda bi, h: (0, 0))
        in_specs += [col(C_RQP), col(C_RKP), tab, tab]
        args += [p, p, rope_tabs[0], rope_tabs[1]]
    in_specs += [pl.BlockSpec((5, RET_HPS, RET_CHUNK, RET_CHUNK), lambda bi, h: (0, h, 0, 0)), st]
    args += [tables, s0]
    return pl.pallas_call(
        functools.partial(_ret_kernel, nc=nc, rope=rope),
        grid=(b, RET_HEADS // RET_HPS),
        in_specs=in_specs,
        out_specs=[pl.BlockSpec((t, wd), lambda bi, h: (bi, h)), st],
        out_shape=[jax.ShapeDtypeStruct((b * t, RET_HEADS * RET_DV), jnp.bfloat16),
                   jax.ShapeDtypeStruct((b, 2, RET_HEADS, RET_DK, RET_DV), jnp.float32)],
        scratch_shapes=[pltpu.VMEM((t, wd), jnp.float32), pltpu.VMEM((2, RET_HPS, RET_DK, RET_DV), jnp.float32)],
        compiler_params=_cparams(2, V7X_VMEM_LIMIT_BIG),
        name="retention",
    )(*args)


OUT_TM = 512
ROUTER_PAD = 128


ROW_TILE = (8, D_MODEL // 16)


def _pack_halves(x):
    w = x.shape[1] // 2
    bits = lambda a: lax.bitcast_convert_type(a.astype(jnp.bfloat16).astype(jnp.float32), jnp.int32)
    return bits(x[:, w:]) | lax.shift_right_logical(bits(x[:, :w]), 16)


def _unpack_halves(u):
    lo = lax.bitcast_convert_type(lax.shift_left(u, 16), jnp.float32)
    hi = lax.bitcast_convert_type(u & jnp.int32(-65536), jnp.float32)
    return jnp.concatenate([lo, hi], axis=1)


def _split_bf16(x):
    hi = x.astype(jnp.bfloat16)
    return hi, (x - hi.astype(jnp.float32)).astype(jnp.bfloat16)


def _route(h, whi_ref, wlo_ref, rb_ref):
    f32 = jnp.float32
    hi, lo = _split_bf16(h)
    dn = (((1,), (1,)), ((), ()))
    lt = (lax.dot_general(whi_ref[...], hi, dn, preferred_element_type=f32)
          + lax.dot_general(whi_ref[...], lo, dn, preferred_element_type=f32)
          + lax.dot_general(wlo_ref[...], hi, dn, preferred_element_type=f32))[:N_EXPERTS]
    m = jnp.max(lt, axis=0, keepdims=True)
    e = jnp.exp(lt - m)
    sc = e / jnp.sum(e, axis=0, keepdims=True)
    sel = sc + rb_ref[...][:N_EXPERTS, 0:1]
    rows = lambda a: [a[j:j + 1, :] for j in range(N_EXPERTS)]
    sel_r, sc_r = rows(sel), rows(sc)
    epg = EXPERTS_PER_GROUP

    def top2sum(a, b, c, d):
        h1, l1, h2, l2 = jnp.maximum(a, b), jnp.minimum(a, b), jnp.maximum(c, d), jnp.minimum(c, d)
        return jnp.maximum(h1, h2) + jnp.maximum(jnp.minimum(h1, h2), jnp.maximum(l1, l2))

    gs = [top2sum(*sel_r[g * epg:(g + 1) * epg]) for g in range(N_EXPERT_GROUPS)]
    best, gi = gs[0], jnp.zeros_like(gs[0], dtype=jnp.int32)
    for g in range(1, N_EXPERT_GROUPS):
        upd = gs[g] > best
        gi = jnp.where(upd, g, gi)
        best = jnp.where(upd, gs[g], best)

    def pick(r, j):
        out = r[j]
        for g in range(1, N_EXPERT_GROUPS):
            out = jnp.where(gi == g, r[g * epg + j], out)
        return out

    v = [pick(sel_r, j) for j in range(epg)]
    s = [pick(sc_r, j) for j in range(epg)]

    def argmax_first(vals):
        bv, bi = vals[0], jnp.zeros_like(gi)
        for j in range(1, epg):
            upd = vals[j] > bv
            bi = jnp.where(upd, j, bi)
            bv = jnp.where(upd, vals[j], bv)
        return bi

    i1 = argmax_first(v)
    neg = jnp.float32(-jnp.inf)
    i2 = argmax_first([jnp.where(i1 == j, neg, v[j]) for j in range(epg)])
    take = lambda i: sum(jnp.where(i == j, s[j], 0.0) for j in range(epg))
    w1, w2 = take(i1), take(i2)
    tot = w1 + w2
    return (jnp.concatenate([gi * epg + i1, gi * epg + i2], axis=0),
            jnp.concatenate([w1 / tot, w2 / tot], axis=0))


def _mixout_kernel(ylin_ref, u_ref, ymla_c, ymla_l, ylru_c, ylru_l, yret_c, yret_l, x_ref, mod_ref, d_ref,
                   wglu_ref, wout_ref, g2n_ref, whi_ref, wlo_ref, rb_ref, xo_ref, h2_ref, idx_ref, wgt_ref,
                   *, n_ctx_tiles):
    f32, bf = jnp.float32, jnp.bfloat16
    w = GROUP_W
    is_ctx = pl.program_id(0) < n_ctx_tiles
    pick = lambda c_ref, l_ref: jnp.where(is_ctx, c_ref[...], l_ref[...])
    y = _gelu_tanh(ylin_ref[...].astype(f32) + d_ref[...] * u_ref[...].astype(f32))
    y5 = (y * _sigmoid(jnp.dot(y.astype(bf), wglu_ref[...], preferred_element_type=f32))).astype(bf)
    mix = (jnp.dot(y5, wout_ref[0:w, :], preferred_element_type=f32)
           + jnp.dot(pick(ymla_c, ymla_l), wout_ref[w:2 * w, :], preferred_element_type=f32)
           + jnp.dot(pick(ylru_c, ylru_l), wout_ref[2 * w:3 * w, :], preferred_element_type=f32)
           + jnp.dot(pick(yret_c, yret_l), wout_ref[3 * w:4 * w, :], preferred_element_type=f32))
    x = x_ref[...] + mod_ref[0, 2:3, :] * mix
    xo_ref[...] = x
    h2 = _rms(x) * g2n_ref[...] * (1.0 + mod_ref[0, 4:5, :]) + mod_ref[0, 3:4, :]
    h2_ref[...] = _pack_halves(h2).reshape(h2_ref.shape)
    idx, wgt = _route(h2, whi_ref, wlo_ref, rb_ref)
    idx_ref[...] = idx
    wgt_ref[...] = wgt


def _router_weights(router_w, router_b):
    d = router_w.shape[0]
    wt = jnp.zeros((ROUTER_PAD, d), jnp.float32).at[:N_EXPERTS].set(router_w.T.astype(jnp.float32))
    whi, wlo = _split_bf16(wt)
    rb = jnp.zeros((ROUTER_PAD, 128), jnp.float32).at[:N_EXPERTS].set(router_b.astype(jnp.float32)[:, None])
    return whi, wlo, rb


def _mixout(ylin, p, ymla, ylru, yret, x, mods, s5_d, wglu, wout, g2n, rw, n_ctx, dec_seq):
    n, d = x.shape
    tm = OUT_TM
    w = GROUP_W
    na = n_ctx // tm
    seq = functools.partial(_seq_of_tile, tm=tm, n_ctx=n_ctx, dec_seq=dec_seq)
    full = lambda a: pl.BlockSpec(a.shape, lambda i: (0,) * a.ndim)
    row = lambda width: pl.BlockSpec((tm, width), lambda i: (i, 0))
    ctx_row = pl.BlockSpec((tm, w), lambda i: (jnp.minimum(i, na - 1), 0))
    lat_row = pl.BlockSpec((tm, w), lambda i: (jnp.maximum(i - na, 0), 0))
    lanes = pl.BlockSpec((TOP_K, tm), lambda i: (0, i))
    whi, wlo, rb = rw
    return pl.pallas_call(
        functools.partial(_mixout_kernel, n_ctx_tiles=na),
        grid=(n // tm,),
        in_specs=[row(w), row(w), ctx_row, lat_row, ctx_row, lat_row, ctx_row, lat_row, row(d),
                  pl.BlockSpec((1, N_MOD, d), lambda i: (seq(i), 0, 0)),
                  full(s5_d), full(wglu), full(wout), full(g2n), full(whi), full(wlo), full(rb)],
        out_specs=[row(d), pl.BlockSpec((tm,) + ROW_TILE, lambda i: (i, 0, 0)), lanes, lanes],
        out_shape=[jax.ShapeDtypeStruct((n, d), jnp.float32), jax.ShapeDtypeStruct((n,) + ROW_TILE, jnp.int32),
                   jax.ShapeDtypeStruct((TOP_K, n), jnp.int32), jax.ShapeDtypeStruct((TOP_K, n), jnp.float32)],
        compiler_params=_cparams(1, V7X_VMEM_LIMIT_BIG),
        name="mix_out_norm_route",
    )(ylin, p, *ymla, *ylru, *yret, x, mods, s5_d, wglu, wout, g2n, whi, wlo, rb)


MOE_TM = 256
MOE_NB = 256


def _expert_kernel(te_ref, nt_ref, nv_ref, tok0_ref, tok1_ref, w_ref, dstp_ref, dst_ref, wg_ref, wu_ref, wd_ref,
                   h_hbm, y_hbm, wg_s, wu_s, wd_s, xbuf, xs_s, ybuf, gsem, ssem):
    i = pl.program_id(0)
    bf = jnp.bfloat16
    tm = xs_s.shape[0]
    nt = nt_ref[0]

    def gather_start(tok_ref, sl, unroll):
        def row(r, carry):
            pltpu.make_async_copy(h_hbm.at[tok_ref[0, 0, r]], xbuf.at[sl, r], gsem.at[sl]).start()
            return carry

        lax.fori_loop(0, tm, row, 0, unroll=unroll)

    def gather_done(sl):
        pltpu.make_async_copy(h_hbm.at[pl.ds(0, tm)], xbuf.at[sl], gsem.at[sl]).wait()

    def scatter_row(d_ref, sl, r):
        return pltpu.make_async_copy(ybuf.at[sl, r], y_hbm.at[d_ref[0, 0, r]], ssem.at[sl])

    def scatter_done(sl, n_rows):
        pltpu.make_async_copy(ybuf.at[sl, pl.ds(0, n_rows)], y_hbm.at[pl.ds(0, n_rows)], ssem.at[sl]).wait()

    def tile(slot):
        n_prev = jnp.where(i > 0, nv_ref[jnp.maximum(i - 1, 0)], 0)
        f = wg_s.shape[1]
        d = wd_s.shape[1]
        pieces = 2 * (f // MOE_NB) + d // MOE_NB
        per = tm // pieces
        done = [0]

        def issue(last=False):
            lo = done[0]
            hi = tm if last else lo + per
            done[0] = hi
            for r in range(lo, hi):
                pltpu.make_async_copy(h_hbm.at[tok1_ref[0, 0, r]], xbuf.at[1 - slot, r], gsem.at[1 - slot]).start()

                @pl.when(r < n_prev)
                def _():
                    scatter_row(dstp_ref, 1 - slot, r).start()

        gather_done(slot)
        xs_s[...] = _unpack_halves(xbuf[slot].reshape(tm, -1)).astype(bf)
        x = xs_s[...]
        gs, us = [], []
        for c in range(f // MOE_NB):
            cols = slice(c * MOE_NB, (c + 1) * MOE_NB)
            gs.append(jnp.dot(x, wg_s[:, cols], preferred_element_type=jnp.float32))
            issue()
            us.append(jnp.dot(x, wu_s[:, cols], preferred_element_type=jnp.float32))
            issue()
        g = jnp.concatenate(gs, axis=1)
        u = jnp.concatenate(us, axis=1)
        act = ((g * _sigmoid(g)) * u * w_ref[...]).astype(bf)
        ys = []
        for c in range(d // MOE_NB):
            ys.append(jnp.dot(act, wd_s[:, c * MOE_NB:(c + 1) * MOE_NB], preferred_element_type=jnp.float32))
            issue(last=(c == d // MOE_NB - 1))
        y = jnp.concatenate(ys, axis=1)

        @pl.when(i > 1)
        def _():
            scatter_done(slot, nv_ref[jnp.maximum(i - 2, 0)])

        ybuf[slot] = _pack_halves(y).reshape(ybuf.shape[1:])

        @pl.when(i == nt - 1)
        def _():
            def row(r, carry):
                scatter_row(dst_ref, slot, r).start()
                return carry

            lax.fori_loop(0, nv_ref[i], row, 0)

            @pl.when(i > 0)
            def _():
                scatter_done(1 - slot, nv_ref[jnp.maximum(i - 1, 0)])

            scatter_done(slot, nv_ref[i])
            gather_done(1 - slot)

    @pl.when(i < nt)
    def _():
        @pl.when(i == 0)
        def _():
            gather_start(tok0_ref, 0, 8)

        @pl.when((i == 0) | (te_ref[i] != te_ref[jnp.maximum(i - 1, 0)]))
        def _():
            wg_s[...] = wg_ref[0, 0].astype(bf)
            wu_s[...] = wu_ref[0, 0].astype(bf)
            wd_s[...] = wd_ref[0, 0].astype(bf)

        @pl.when(i % 2 == 0)
        def _():
            tile(0)

        @pl.when(i % 2 == 1)
        def _():
            tile(1)


def _experts(h2p, layer, ws, tok, dst, tile_expert, n_tiles_used, n_valid, wg, wu, wd):
    n = h2p.shape[0]
    d = 2 * ROW_TILE[0] * ROW_TILE[1]
    tm = MOE_TM
    n_tiles = tok.shape[0]
    f = wg.shape[-1]
    smem_row = lambda delta: pl.BlockSpec(
        (1, 1, tm), lambda i, te, nt, nv: (jnp.clip(i + delta, 0, n_tiles - 1), 0, 0), memory_space=pltpu.SMEM)
    wspec = lambda a, b: pl.BlockSpec((1, 1, a, b), lambda i, te, nt, nv: (layer, te[i], 0, 0))
    return pl.pallas_call(
        _expert_kernel,
        grid_spec=pltpu.PrefetchScalarGridSpec(
            num_scalar_prefetch=3,
            grid=(n_tiles,),
            in_specs=[smem_row(0), smem_row(1),
                      pl.BlockSpec((tm, 1), lambda i, te, nt, nv: (i, 0)),
                      smem_row(-1), smem_row(0),
                      wspec(d, f), wspec(d, f), wspec(f, d),
                      pl.BlockSpec(memory_space=pl.ANY)],
            out_specs=pl.BlockSpec(memory_space=pl.ANY),
            scratch_shapes=[pltpu.VMEM((d, f), jnp.bfloat16), pltpu.VMEM((d, f), jnp.bfloat16),
                            pltpu.VMEM((f, d), jnp.bfloat16), pltpu.VMEM((2, tm) + ROW_TILE, jnp.int32),
                            pltpu.VMEM((tm, d), jnp.bfloat16), pltpu.VMEM((2, tm) + ROW_TILE, jnp.int32),
                            pltpu.SemaphoreType.DMA((2,)), pltpu.SemaphoreType.DMA((2,))]),
        out_shape=jax.ShapeDtypeStruct((TOP_K * n,) + ROW_TILE, jnp.int32),
        compiler_params=pltpu.CompilerParams(dimension_semantics=("arbitrary",),
                                             vmem_limit_bytes=V7X_VMEM_LIMIT_BIG, disable_bounds_checks=True),
        name="moe_experts",
    )(tile_expert, n_tiles_used, n_valid, tok, tok, ws, dst, dst, wg, wu, wd, h2p)


def _moe_dispatch(idx, wgt):
    n = idx.shape[1]
    tm = MOE_TM
    n_pairs = TOP_K * n
    m_pad = n_pairs + N_EXPERTS * tm
    n_tiles = m_pad // tm
    e_flat = idx.reshape(-1)
    order = jnp.argsort(e_flat, stable=True).astype(jnp.int32)
    experts = jnp.arange(N_EXPERTS, dtype=jnp.int32)
    counts = jnp.sum((e_flat[None, :] == experts[:, None]).astype(jnp.int32), axis=1)
    starts_raw = jnp.cumsum(counts) - counts
    padded = ((counts + tm - 1) // tm) * tm
    ends = jnp.cumsum(padded)
    starts_pad = ends - padded
    tile_start = jnp.arange(n_tiles, dtype=jnp.int32) * tm
    tile_expert = jnp.minimum(jnp.sum((tile_start[:, None] >= ends[None, :]).astype(jnp.int32), axis=1),
                              N_EXPERTS - 1)
    r = jnp.arange(m_pad, dtype=jnp.int32)
    te_r = jnp.repeat(tile_expert, tm)
    off = r - jnp.take(starts_pad, te_r)
    valid = (off >= 0) & (off < jnp.take(counts, te_r))
    pair = jnp.take(order, jnp.clip(jnp.take(starts_raw, te_r) + off, 0, n_pairs - 1))
    tok = jnp.where(valid, pair % n, 0).reshape(n_tiles, 1, tm)
    dst = jnp.where(valid, pair, 0).reshape(n_tiles, 1, tm)
    ws = jnp.where(valid, jnp.take(wgt.reshape(-1), pair), 0.0)[:, None]
    n_tiles_used = (ends[-1] // tm).astype(jnp.int32).reshape(1)
    n_valid = jnp.sum(valid.reshape(n_tiles, tm), axis=1).astype(jnp.int32)
    return tok, dst, ws, tile_expert, n_tiles_used, n_valid


def _moe_rows(h2p, layer, idx, wgt, wg, wu, wd):
    tok, dst, ws, tile_expert, n_tiles_used, n_valid = _moe_dispatch(idx, wgt)
    return _experts(h2p, layer, ws, tok, dst, tile_expert, n_tiles_used, n_valid, wg, wu, wd)


RES_TM = 512


def _resid_kernel(x_ref, ya_ref, yb_ref, mod_ref, gf_ref, *o_refs, final, n_ctx_tiles):
    rows = lambda r: _unpack_halves(r[...].reshape(r.shape[0], -1))
    x = x_ref[...] + mod_ref[0, 5:6, :] * (rows(ya_ref) + rows(yb_ref))
    if not final:
        o_refs[0][...] = x
    else:
        y = _rms(x) * gf_ref[...]
        i = pl.program_id(0)

        @pl.when(i < n_ctx_tiles)
        def _():
            o_refs[0][...] = y

        @pl.when(i >= n_ctx_tiles)
        def _():
            o_refs[1][...] = y


def _resid(x, y2, mods, gf, n_ctx, dec_seq, final):
    n, d = x.shape
    tm = RES_TM
    na = n_ctx // tm
    seq = functools.partial(_seq_of_tile, tm=tm, n_ctx=n_ctx, dec_seq=dec_seq)
    row = pl.BlockSpec((tm, d), lambda i: (i, 0))
    if final:
        out_specs = [pl.BlockSpec((tm, d), lambda i: (jnp.minimum(i, na - 1), 0)),
                     pl.BlockSpec((tm, d), lambda i: (jnp.maximum(i - na, 0), 0))]
        out_shape = [jax.ShapeDtypeStruct((n_ctx, d), jnp.float32), jax.ShapeDtypeStruct((n - n_ctx, d), jnp.float32)]
    else:
        out_specs, out_shape = row, jax.ShapeDtypeStruct((n, d), jnp.float32)
    return pl.pallas_call(
        functools.partial(_resid_kernel, final=final, n_ctx_tiles=na),
        grid=(n // tm,),
        in_specs=[row, pl.BlockSpec((tm,) + ROW_TILE, lambda i: (i, 0, 0)),
                  pl.BlockSpec((tm,) + ROW_TILE, lambda i: (n // tm + i, 0, 0)),
                  pl.BlockSpec((1, N_MOD, d), lambda i: (seq(i), 0, 0)),
                  pl.BlockSpec((1, d), lambda i: (0, 0))],
        out_specs=out_specs,
        out_shape=out_shape,
        compiler_params=_cparams(1),
        name="moe_residual_norm",
    )(x, y2, y2, mods, gf)


def kernel(x_prompt, x_sample, c, cache_mla_ckv, cache_mla_kpe, state_s5, state_lru, state_ret,
           c_ctx, w_ada, b_ada, norm1_g, norm2_g, w_in, w_out,
           s5_a_re, s5_a_im, s5_log_dt, s5_b_re, s5_b_im, s5_c_re, s5_c_im, s5_d, s5_w_glu,
           mla_q_norm_g, mla_w_uq, mla_kv_norm_g, mla_w_ukv,
           lru_conv_w, lru_conv_b, lru_w_a, lru_b_a, lru_w_x, lru_b_x, lru_lambda,
           router_w, router_b, moe_w_gate, moe_w_up, moe_w_down, final_norm_g):
    f32, bf = jnp.float32, jnp.bfloat16
    batch, seq_len, d = x_prompt.shape
    dec_batch, dec_seq, _ = x_sample.shape
    past = cache_mla_ckv.shape[2]
    n_ctx, n_lat = batch * seq_len, dec_batch * dec_seq
    depth = w_in.shape[0]

    x = jnp.concatenate([x_prompt.reshape(n_ctx, d), x_sample.reshape(n_lat, d)], axis=0)
    cvec = jnp.zeros((8, d), f32).at[0].set(c_ctx).at[1:1 + dec_batch].set(c)
    mods_all = _ada(cvec, w_ada, b_ada).reshape(depth, 8, N_MOD, d)

    cos64, sin64 = _rope_tables(dec_seq, MLA_ROPE)
    lat_tab = jnp.tile(jnp.concatenate([cos64, sin64], axis=1), (dec_batch, 1))
    ctx_tab = jnp.concatenate([jnp.ones((n_ctx, MLA_ROPE), f32), jnp.zeros((n_ctx, MLA_ROPE), f32)], axis=1)
    mla_tab = jnp.concatenate([ctx_tab, lat_tab], axis=0)
    ret_rope = _rope_tables(dec_seq, RET_DK)
    ret_tables = _ret_tables()
    rw = _router_weights(router_w, router_b)
    gf = final_norm_g[None, :]

    states = []
    for l in range(depth):
        mods = mods_all[l]
        s5_tab = _s5_tables(s5_a_re[l], s5_a_im[l], s5_log_dt[l], s5_b_re[l], s5_b_im[l], s5_c_re[l], s5_c_im[l])
        lru_tab = _lru_tables(lru_conv_w[l], lru_conv_b[l], lru_w_a[l], lru_b_a[l], lru_w_x[l], lru_b_x[l],
                              lru_lambda[l])
        wuq, wukv = _mla_weights(mla_w_uq[l], mla_w_ukv[l])

        p = _inproj(x, mods, norm1_g[l][None, :], _inproj_weights(w_in[l]), n_ctx, dec_seq)

        u_ctx = p[:n_ctx, C_S5:C_S5 + S5_CH].reshape(batch, seq_len, S5_CH)
        u_lat = p[n_ctx:, C_S5:C_S5 + S5_CH].reshape(dec_batch, dec_seq, S5_CH)
        ylin, s5_fins = _s5_core([u_ctx, u_lat], [None, state_s5[:, l]], s5_tab)

        qo, ko, vo, ckv = _mla_prep(p, mla_tab, mla_q_norm_g[l][None, :], mla_kv_norm_g[l][None, :], wuq, wukv)
        ckv_c = cache_mla_ckv[:, l].reshape(dec_batch * past, MLA_KV_RANK)
        kv_c = _matmul(ckv_c, wukv, out_dtype=bf).reshape(dec_batch * past, MLA_HEADS, 2, 128)
        kpe_c = jnp.broadcast_to(cache_mla_kpe[:, l].reshape(dec_batch * past, 1, MLA_ROPE).astype(bf),
                                 (dec_batch * past, MLA_HEADS, MLA_ROPE))
        k_c = jnp.concatenate([kv_c[:, :, 0], kpe_c, jnp.zeros_like(kpe_c)], axis=-1).reshape(
            dec_batch * past, MLA_HEADS * ATT_DQ)
        v_c = kv_c[:, :, 1].reshape(dec_batch * past, MLA_HEADS * MLA_V)
        ymla = (_attention(qo, ko, vo, 0, batch, seq_len),
                _attention(qo, ko, vo, n_ctx, dec_batch, dec_seq, k_c, v_c))

        ylru_c, lru_fin = _lru(p, 0, batch, seq_len, jnp.zeros((batch, 2, LRU_W), f32), lru_tab)
        ylru_l, _ = _lru(p, n_ctx, dec_batch, dec_seq, state_lru[:, l].astype(f32), lru_tab)
        ylru = (ylru_c, ylru_l)

        yret_c, ret_fin = _retention(p, 0, batch, seq_len,
                                     jnp.zeros((batch, 2, RET_HEADS, RET_DK, RET_DV), f32), ret_tables)
        yret_l, _ = _retention(p, n_ctx, dec_batch, dec_seq, state_ret[:, l].astype(f32), ret_tables, ret_rope)
        yret = (yret_c, yret_l)

        x_mid, h2, idx, wgt = _mixout(ylin, p, ymla, ylru, yret, x, mods, s5_d[l][None, :],
                                      s5_w_glu[l].astype(bf), w_out[l].astype(bf), norm2_g[l][None, :], rw,
                                      n_ctx, dec_seq)
        y2 = _moe_rows(h2, l, idx, wgt, moe_w_gate, moe_w_up, moe_w_down)
        x = _resid(x_mid, y2, mods, gf, n_ctx, dec_seq, final=(l == depth - 1))

        states.append((ckv[:n_ctx].reshape(batch, seq_len, MLA_KV_RANK),
                       p[:n_ctx, C_KR:C_KR + MLA_ROPE].astype(f32).reshape(batch, seq_len, MLA_ROPE),
                       s5_fins[0], lru_fin, ret_fin))

    y_prompt = x[0].reshape(batch, seq_len, d)
    y_sample = x[1].reshape(dec_batch, dec_seq, d)
    new_cache_mla_ckv = jnp.stack([st[0] for st in states], axis=1)
    new_cache_mla_kpe = jnp.stack([st[1] for st in states], axis=1)
    new_state_s5 = jnp.stack([st[2] for st in states], axis=1)
    new_state_lru = jnp.stack([st[3] for st in states], axis=1)
    new_state_ret = jnp.stack([st[4] for st in states], axis=1)
    return (y_prompt, y_sample, new_cache_mla_ckv, new_cache_mla_kpe, new_state_s5, new_state_lru, new_state_ret)
```

```python
import functools
import numpy as np
import jax
import jax.numpy as jnp
from jax import lax
from jax.experimental import pallas as pl
from jax.experimental.pallas import tpu as pltpu

D_MODEL = 2048
DEPTH = 2
GRID_W = 64
EPS = 1e-6
ROPE_BASE = 10000.0
N_MOD = 6
GROUP_W = 512
S5_CH = GROUP_W
S5_GROUP_CH = 16
S5_GROUPS = S5_CH // S5_GROUP_CH
S5_STATE = 64
MLA_HEADS = 4
MLA_NOPE = 128
MLA_ROPE = 64
MLA_V = 128
MLA_Q_RANK = GROUP_W
MLA_KV_RANK = GROUP_W // 2
MLA_SCALE = (MLA_NOPE + MLA_ROPE) ** -0.5
LRU_W = GROUP_W
LRU_CONV = 4
LRU_C = 8.0
RET_HEADS = 4
RET_DK = 128
RET_DV = 128
RET_CHUNK = 128
N_EXPERTS = 16
N_EXPERT_GROUPS = 4
EXPERTS_PER_GROUP = N_EXPERTS // N_EXPERT_GROUPS
TOP_K = 2
D_EXPERT = D_MODEL // 4

V7X_VMEM_LIMIT = 48 * 1024 * 1024
V7X_VMEM_LIMIT_BIG = 56 * 1024 * 1024

C_S5, C_MQ, C_LX, C_LG, C_RQ, C_RK, C_RV, C_RG, C_RQP, C_RKP, C_MKV, C_KR = (
    0, 512, 1024, 1536, 2048, 2560, 3072, 3584, 4096, 4608, 5120, 5376)
P_COLS = 5632
R_S5, R_MQ, R_MKV, R_KR, R_LX, R_LG, R_RQ, R_RK, R_RV, R_RG = (0, 512, 1024, 1280, 1344, 1856, 2368, 2880, 3392, 3904)


def _cparams(n_axes, limit=V7X_VMEM_LIMIT):
    return pltpu.CompilerParams(dimension_semantics=("arbitrary",) * n_axes, vmem_limit_bytes=limit)


def _sigmoid(x):
    return 1.0 / (1.0 + jnp.exp(-x))


def _gelu_tanh(x):
    return 0.5 * x * (1.0 + jnp.tanh(0.7978845608028654 * (x + 0.044715 * (x * x * x))))


def _rms(x):
    return x * lax.rsqrt(jnp.mean(x * x, axis=-1, keepdims=True) + EPS)


def _seq_of_tile(i, tm, n_ctx, dec_seq):
    assert n_ctx % tm == 0 and dec_seq % tm == 0, "a row tile must not straddle two sequences"
    r = i * tm
    return jnp.where(r < n_ctx, 0, 1 + (r - n_ctx) // dec_seq)


def _mm_kernel(x_ref, w_ref, o_ref):
    o_ref[...] = jnp.dot(x_ref[...].astype(jnp.bfloat16), w_ref[...].astype(jnp.bfloat16),
                         preferred_element_type=jnp.float32).astype(o_ref.dtype)


def _matmul(x, w, tm=512, tn=512, out_dtype=jnp.float32):
    m, k = x.shape
    _, n = w.shape
    tm, tn = min(tm, m), min(tn, n)
    return pl.pallas_call(
        _mm_kernel,
        grid=(n // tn, m // tm),
        in_specs=[pl.BlockSpec((tm, k), lambda j, i: (i, 0)), pl.BlockSpec((k, tn), lambda j, i: (0, j))],
        out_specs=pl.BlockSpec((tm, tn), lambda j, i: (i, j)),
        out_shape=jax.ShapeDtypeStruct((m, n), out_dtype),
        compiler_params=_cparams(2),
        name="matmul",
    )(x, w)


ADA_TN = 1024


def _ada_kernel(c_ref, w_ref, b_ref, o_ref):
    c = c_ref[...]
    s = (c * _sigmoid(c)).astype(jnp.bfloat16)
    o_ref[0] = jnp.dot(s, w_ref[0].astype(jnp.bfloat16), preferred_element_type=jnp.float32) + b_ref[0]


def _ada(cvec, w_ada, b_ada):
    depth, d, n = w_ada.shape
    return pl.pallas_call(
        _ada_kernel,
        grid=(depth, n // ADA_TN),
        in_specs=[pl.BlockSpec((8, d), lambda l, j: (0, 0)),
                  pl.BlockSpec((1, d, ADA_TN), lambda l, j: (l, 0, j)),
                  pl.BlockSpec((1, 1, ADA_TN), lambda l, j: (l, 0, j))],
        out_specs=pl.BlockSpec((1, 8, ADA_TN), lambda l, j: (l, 0, j)),
        out_shape=jax.ShapeDtypeStruct((depth, 8, n), jnp.float32),
        compiler_params=_cparams(2),
        name="adaln_mod",
    )(cvec, w_ada, b_ada.reshape(depth, 1, n))


IN_TM = 1024
IN_TN = 512


def _inproj_kernel(x_ref, mod_ref, g_ref, w_ref, o_ref, h_s):
    @pl.when(pl.program_id(1) == 0)
    def _():
        h = _rms(x_ref[...]) * g_ref[...]
        h_s[...] = (h * (1.0 + mod_ref[0, 1:2, :]) + mod_ref[0, 0:1, :]).astype(h_s.dtype)

    o_ref[...] = jnp.dot(h_s[...], w_ref[...], preferred_element_type=jnp.float32).astype(o_ref.dtype)


def _inproj(x, mods, g, w, n_ctx, dec_seq):
    n, d = x.shape
    seq = functools.partial(_seq_of_tile, tm=IN_TM, n_ctx=n_ctx, dec_seq=dec_seq)
    return pl.pallas_call(
        _inproj_kernel,
        grid=(n // IN_TM, P_COLS // IN_TN),
        in_specs=[pl.BlockSpec((IN_TM, d), lambda i, j: (i, 0)),
                  pl.BlockSpec((1, N_MOD, d), lambda i, j: (seq(i), 0, 0)),
                  pl.BlockSpec((1, d), lambda i, j: (0, 0)),
                  pl.BlockSpec((d, IN_TN), lambda i, j: (0, j))],
        out_specs=pl.BlockSpec((IN_TM, IN_TN), lambda i, j: (i, j)),
        out_shape=jax.ShapeDtypeStruct((n, P_COLS), jnp.bfloat16),
        scratch_shapes=[pltpu.VMEM((IN_TM, d), jnp.bfloat16)],
        compiler_params=_cparams(2),
        name="norm_inproj",
    )(x, mods, g, w)


def _rot_partner_cols(w, n_heads, head_dim):
    q = head_dim // 4
    wr = w.reshape(w.shape[0], n_heads, 2, 2, q)
    return jnp.stack([-wr[:, :, :, 1], wr[:, :, :, 0]], axis=3).reshape(w.shape)


def _inproj_weights(w_in):
    seg = lambda off, width: w_in[:, off:off + width]
    kr = seg(R_KR, MLA_ROPE)
    cols = [seg(R_S5, 2 * GROUP_W),
            seg(R_LX, 6 * GROUP_W),
            _rot_partner_cols(seg(R_RQ, 2 * GROUP_W), 2 * RET_HEADS, RET_DK),
            seg(R_MKV, MLA_KV_RANK), kr, _rot_partner_cols(kr, 1, MLA_ROPE),
            jnp.zeros((w_in.shape[0], P_COLS - C_KR - 2 * MLA_ROPE), w_in.dtype)]
    return jnp.concatenate(cols, axis=1).astype(jnp.bfloat16)


def _rope_tables(t_len, rot_dim):
    rows = t_len // GRID_W
    row = jnp.repeat(jnp.arange(rows, dtype=jnp.float32), GRID_W)
    col = jnp.tile(jnp.arange(GRID_W, dtype=jnp.float32), rows)
    n_freq = rot_dim // 4
    inv = ROPE_BASE ** (-jnp.arange(n_freq, dtype=jnp.float32) / n_freq)
    ang = jnp.concatenate([row[:, None] * inv[None]] * 2 + [col[:, None] * inv[None]] * 2, axis=1)
    return jnp.cos(ang), jnp.sin(ang)


S5_L = 32
S5_PAIRS = S5_GROUPS // 2
S5_Q = 4
S5_LANES = S5_GROUPS * S5_STATE
S5_SCAN_LANES = 1024


def _s5_tables(a_re, a_im, log_dt, b_re, b_im, c_re, c_im):
    f32 = jnp.float32
    L, G, P, C = S5_L, S5_GROUPS, S5_STATE, S5_GROUP_CH
    hp = lax.Precision.HIGHEST
    cmul = lambda xr, xi, yr, yi: (xr * yr - xi * yi, xr * yi + xi * yr)
    dt = jnp.exp(log_dt.astype(f32))[..., None]
    zr, zi = a_re * dt, a_im * dt
    ab_r, ab_i = jnp.exp(zr) * jnp.cos(zi), jnp.exp(zr) * jnp.sin(zi)
    den = a_re * a_re + a_im * a_im
    nr, ni = ab_r - 1.0, ab_i
    be_r = (nr * a_re + ni * a_im) / den
    be_i = (ni * a_re - nr * a_im) / den
    bt_r, bt_i = b_re.transpose(0, 2, 1), b_im.transpose(0, 2, 1)
    bp_r, bp_i = cmul(be_r[:, :, None, :], be_i[:, :, None, :], bt_r[None], bt_i[None])
    tau = jnp.arange(L + 1, dtype=f32)[None, None, :, None]
    mag = jnp.exp(zr[:, :, None, :] * tau)
    pw_r, pw_i = mag * jnp.cos(zi[:, :, None, :] * tau), mag * jnp.sin(zi[:, :, None, :] * tau)

    cat = lambda *xs: jnp.concatenate(xs, axis=-1)
    pf_r, pf_i = pw_r[0][:, :L][:, ::-1], pw_i[0][:, :L][:, ::-1]
    pb_r, pb_i = pw_r[1][:, :L], pw_i[1][:, :L]
    w1 = cat(pf_r, pf_i, pb_r, pb_i)[:, :, None, :]
    w2 = cat(pf_i, pf_r, pb_i, pb_r)[:, :, None, :]
    b1 = cat(bp_r[0], bp_r[0], bp_r[1], bp_r[1])[:, None]
    b2 = cat(-bp_i[0], bp_i[0], -bp_i[1], bp_i[1])[:, None]
    ws = (w1 * b1 + w2 * b2).reshape(G, L * C, S5_Q * P)

    ct_r, ct_i = c_re.transpose(0, 2, 1), c_im.transpose(0, 2, 1)
    pt_r, pt_i = pw_r.transpose(0, 1, 3, 2), pw_i.transpose(0, 1, 3, 2)

    def cm(er, ei):
        m_r, m_i = cmul(ct_r[:, :, None, :], ct_i[:, :, None, :], er[..., None], ei[..., None])
        return m_r.reshape(G, P, L * C), m_i.reshape(G, P, L * C)

    mf_r, mf_i = cm(pt_r[0][:, :, 1:L + 1], pt_i[0][:, :, 1:L + 1])
    mb_r, mb_i = cm(pt_r[1][:, :, 1:L + 1][:, :, ::-1], pt_i[1][:, :, 1:L + 1][:, :, ::-1])
    wo = jnp.stack([mf_r, -mf_i, mb_r, -mb_i], axis=1).reshape(G, S5_Q * P, L * C)

    def impulse(d):
        m_r, m_i = cm(pt_r[d][:, :, :L], pt_i[d][:, :, :L])
        kk = (jnp.einsum('gkp,gpx->gkx', bp_r[d], m_r, precision=hp)
              - jnp.einsum('gkp,gpx->gkx', bp_i[d], m_i, precision=hp))
        return kk.reshape(G, C, L, C)

    kf, kb = impulse(0), impulse(1)
    wide = jnp.concatenate([kb[:, :, :0:-1], kf[:, :, :1] + kb[:, :, :1], kf[:, :, 1:]], axis=2)
    wide = jnp.pad(wide.reshape(G, C, (2 * L - 1) * C), ((0, 0), (0, 0), (0, C)))

    a_l = jnp.stack([pw_r[0, :, L], pw_i[0, :, L], pw_r[1, :, L], pw_i[1, :, L]], axis=0).reshape(S5_Q, 1, S5_LANES)
    bf = jnp.bfloat16
    return _s5_toeplitz(wide), ws.astype(bf), wo.astype(bf), a_l


def _s5_toeplitz_kernel(w_ref, o_ref):
    w = w_ref[0]
    for s in range(S5_L):
        off = (S5_L - 1 - s) * S5_GROUP_CH
        o_ref[0, s * S5_GROUP_CH:(s + 1) * S5_GROUP_CH, :] = w[:, off:off + S5_L * S5_GROUP_CH].astype(o_ref.dtype)


def _s5_toeplitz(wide):
    g, c, wl = wide.shape
    n = S5_L * S5_GROUP_CH
    return pl.pallas_call(
        _s5_toeplitz_kernel,
        grid=(g,),
        in_specs=[pl.BlockSpec((1, c, wl), lambda i: (i, 0, 0))],
        out_specs=pl.BlockSpec((1, n, n), lambda i: (i, 0, 0)),
        out_shape=jax.ShapeDtypeStruct((g, n, n), jnp.bfloat16),
        compiler_params=_cparams(1),
        name="s5_toeplitz",
    )(wide)


def _s5a_kernel(x_ref, t_ref, ws_ref, y1_ref, s_ref):
    ss = []
    for j in range(2):
        x = x_ref[j]
        y1_ref[j] = jnp.dot(x, t_ref[j], preferred_element_type=jnp.float32)
        ss.append(jnp.dot(x, ws_ref[j], preferred_element_type=jnp.float32))
    p = S5_STATE
    for q in range(S5_Q):
        s_ref[q] = jnp.concatenate([ss[0][:, q * p:(q + 1) * p], ss[1][:, q * p:(q + 1) * p]], axis=1)


def _s5a(x, toe, ws):
    g, r, w = x.shape
    return pl.pallas_call(
        _s5a_kernel,
        grid=(S5_PAIRS,),
        in_specs=[pl.BlockSpec((2, r, w), lambda i: (i, 0, 0)),
                  pl.BlockSpec((2, w, w), lambda i: (i, 0, 0)),
                  pl.BlockSpec((2, w, S5_Q * S5_STATE), lambda i: (i, 0, 0))],
        out_specs=[pl.BlockSpec((2, r, w), lambda i: (i, 0, 0)),
                   pl.BlockSpec((S5_Q, r, 128), lambda i: (0, 0, i))],
        out_shape=[jax.ShapeDtypeStruct((g, r, w), jnp.float32),
                   jax.ShapeDtypeStruct((S5_Q, r, S5_LANES), jnp.float32)],
        compiler_params=_cparams(1),
        name="s5_chunk_local",
    )(x, toe, ws)


def _s5b_kernel(s_ref, a_ref, h0_ref, hp_ref, fin_ref, *, nc):
    def run(qr, qi, order_fwd):
        ar, ai = a_ref[qr], a_ref[qi]

        def body(i, carry):
            hr, hi = carry
            k = i if order_fwd else nc - 1 - i
            hp_ref[qr, pl.ds(k, 1), :] = hr
            hp_ref[qi, pl.ds(k, 1), :] = hi
            nr = ar * hr - ai * hi + s_ref[qr, pl.ds(k, 1), :]
            ni = ar * hi + ai * hr + s_ref[qi, pl.ds(k, 1), :]
            return nr, ni

        hr, hi = lax.fori_loop(0, nc, body, (h0_ref[qr], h0_ref[qi]))
        fin_ref[qr] = hr
        fin_ref[qi] = hi

    run(0, 1, True)
    run(2, 3, False)


def _s5b(s, a_l, h0):
    _, nc, w = s.shape
    tl = S5_SCAN_LANES
    vec = pl.BlockSpec((S5_Q, 1, tl), lambda i: (0, 0, i))
    seq = pl.BlockSpec((S5_Q, nc, tl), lambda i: (0, 0, i))
    return pl.pallas_call(
        functools.partial(_s5b_kernel, nc=nc),
        grid=(w // tl,),
        in_specs=[seq, vec, vec],
        out_specs=[seq, vec],
        out_shape=[jax.ShapeDtypeStruct((S5_Q, nc, w), jnp.float32),
                   jax.ShapeDtypeStruct((S5_Q, 1, w), jnp.float32)],
        compiler_params=_cparams(1),
        name="s5_chunk_scan",
    )(s, a_l, h0)


def _s5c_kernel(h_ref, wo_ref, y1_ref, y_ref):
    p = S5_STATE
    for j in range(2):
        hcat = jnp.concatenate([h_ref[q][:, j * p:(j + 1) * p] for q in range(S5_Q)], axis=1)
        y2 = jnp.dot(hcat.astype(jnp.bfloat16), wo_ref[j], preferred_element_type=jnp.float32)
        y_ref[j] = (y1_ref[j] + y2).astype(y_ref.dtype)


def _s5c(hprev, wo, y1):
    g, r, w = y1.shape
    return pl.pallas_call(
        _s5c_kernel,
        grid=(S5_PAIRS,),
        in_specs=[pl.BlockSpec((S5_Q, r, 128), lambda i: (0, 0, i)),
                  pl.BlockSpec((2, S5_Q * S5_STATE, w), lambda i: (i, 0, 0)),
                  pl.BlockSpec((2, r, w), lambda i: (i, 0, 0))],
        out_specs=pl.BlockSpec((2, r, w), lambda i: (i, 0, 0)),
        out_shape=jax.ShapeDtypeStruct((g, r, w), jnp.bfloat16),
        compiler_params=_cparams(1),
        name="s5_state_to_out",
    )(hprev, wo, y1)


def _s5_to_chunks(u):
    b, t, _ = u.shape
    nc = t // S5_L
    x = u.reshape(b, nc, S5_L, S5_GROUPS, S5_GROUP_CH).transpose(3, 1, 0, 2, 4)
    return x.reshape(S5_GROUPS, nc * b, S5_L * S5_GROUP_CH)


def _s5_from_chunks(y, b, t):
    nc = t // S5_L
    y = y.reshape(S5_GROUPS, nc, b, S5_L, S5_GROUP_CH).transpose(2, 1, 3, 0, 4)
    return y.reshape(b * t, S5_CH)


def _s5_state_planes(h0):
    b = h0.shape[0]
    return h0.transpose(1, 4, 0, 2, 3).reshape(S5_Q, b, S5_LANES)


def _s5_core(us, h0s, tables):
    toe, ws_pair, wo_pair, a_l = tables
    xs = [_s5_to_chunks(u) for u in us]
    rows = [x.shape[1] for x in xs]
    y1, s = _s5a(jnp.concatenate(xs, axis=1), toe, ws_pair)
    hps, fins = [], []
    off = 0
    for u, h0, r in zip(us, h0s, rows):
        b, t, _ = u.shape
        nc = t // S5_L
        if h0 is None:
            h0p = jnp.zeros((S5_Q, 1, b * S5_LANES), jnp.float32)
        else:
            h0p = _s5_state_planes(h0.astype(jnp.float32)).reshape(S5_Q, 1, b * S5_LANES)
        hp, fin = _s5b(s[:, off:off + r].reshape(S5_Q, nc, b * S5_LANES), jnp.tile(a_l, (1, 1, b)), h0p)
        hps.append(hp.reshape(S5_Q, r, S5_LANES))
        fins.append(fin.reshape(2, 2, b, S5_GROUPS, S5_STATE).transpose(2, 0, 3, 4, 1))
        off += r
    y = _s5c(jnp.concatenate(hps, axis=1), wo_pair, y1)
    outs, off = [], 0
    for u, r in zip(us, rows):
        b, t, _ = u.shape
        outs.append(_s5_from_chunks(y[:, off:off + r], b, t))
        off += r
    return jnp.concatenate(outs, axis=0), fins


LRU_TC = 256
LRU_HALO = 16


def _lru_kernel(x_ref, gate_ref, cw_ref, cb_ref, wg_ref, bg_ref, sp_ref, h0_ref, out_ref, fin_ref,
                a_s, b_s, hf_s, *, t_len, tc):
    f32 = jnp.float32
    nt = t_len // tc
    w = LRU_W
    nb = tc // 8
    row = lax.broadcasted_iota(jnp.int32, (nb, 8, w), 1)

    def gates(c, d):
        r0 = pl.multiple_of(c * tc, tc)
        lo = pl.multiple_of(jnp.maximum(r0 - LRU_HALO, 0), LRU_HALO)
        hi = pl.multiple_of(jnp.minimum(r0 + tc, t_len - LRU_HALO), LRU_HALO)
        prev = jnp.where(c > 0, x_ref[pl.ds(lo, LRU_HALO), :].astype(f32), 0.0)
        nxt = jnp.where(c < nt - 1, x_ref[pl.ds(hi, LRU_HALO), :].astype(f32), 0.0)
        slab = jnp.concatenate([prev, x_ref[pl.ds(r0, tc), :].astype(f32), nxt], axis=0)
        o = LRU_HALO - LRU_CONV // 2
        xc = cb_ref[...] + sum(cw_ref[k:k + 1, :] * slab[o + k:o + k + tc] for k in range(LRU_CONV))
        g = jnp.dot(xc.astype(jnp.bfloat16), wg_ref[d], preferred_element_type=f32) + bg_ref[d]
        r = _sigmoid(g[:, :w])
        i = _sigmoid(g[:, w:])
        log_a = -sp_ref[d] * r
        a = jnp.exp(log_a)
        b = (jnp.sqrt(1.0 - a * a) * (i * xc)).reshape(nb, 8, w)
        a = a.reshape(nb, 8, w)
        for sh in (1, 2, 4):
            keep, rot = (row >= sh, sh) if d == 0 else (row < 8 - sh, 8 - sh)
            ap = jnp.where(keep, pltpu.roll(a, rot, 1), 1.0)
            bp = jnp.where(keep, pltpu.roll(b, rot, 1), 0.0)
            b = a * bp + b
            a = a * ap
        a_s[...] = a
        b_s[...] = b
        return r0

    def fwd_chunk(c, h):
        r0 = gates(c, 0)

        def block(k, h):
            h8 = a_s[k] * h + b_s[k]
            hf_s[pl.ds(pl.multiple_of(r0 + k * 8, 8), 8), :] = h8
            return h8[7:8, :]

        return lax.fori_loop(0, nb, block, h, unroll=4)

    h = lax.fori_loop(0, nt, fwd_chunk, h0_ref[0, 0:1, :])
    fin_ref[0, 0:1, :] = h

    def bwd_chunk(ci, h):
        r0 = gates(nt - 1 - ci, 1)

        def block(i, h):
            k = nb - 1 - i
            h8 = a_s[k] * h + b_s[k]
            rows = pl.ds(pl.multiple_of(r0 + k * 8, 8), 8)
            hf_s[rows, :] = hf_s[rows, :] + h8
            return h8[0:1, :]

        h = lax.fori_loop(0, nb, block, h, unroll=4)
        sl = pl.ds(r0, tc)
        out_ref[sl, :] = (hf_s[sl, :] * _gelu_tanh(gate_ref[sl, :].astype(f32))).astype(out_ref.dtype)
        return h

    h = lax.fori_loop(0, nt, bwd_chunk, h0_ref[0, 1:2, :])
    fin_ref[0, 1:2, :] = h


def _block_diag(wb):
    n, k, j = wb.shape
    return (wb[:, :, None, :] * jnp.eye(n, dtype=wb.dtype)[:, None, :, None]).reshape(n * k, n * j)


def _lru_tables(conv_w, conv_b, w_a, b_a, w_x, b_x, lam):
    wg = jnp.stack([jnp.concatenate([_block_diag(w_a[d]), _block_diag(w_x[d])], axis=1) for d in range(2)])
    bg = jnp.concatenate([b_a, b_x], axis=-1)[:, None, :]
    sp = (LRU_C * jax.nn.softplus(-lam.astype(jnp.float32)))[:, None, :]
    return conv_w, conv_b[None, :], wg.astype(jnp.bfloat16), bg, sp


def _lru(p, row0, b, t, h0, tables):
    w = LRU_W
    tc = min(LRU_TC, t)
    cw, cb, wg, bg, sp = tables
    full = lambda a: pl.BlockSpec(a.shape, lambda i: (0,) * a.ndim)
    rb = row0 // t
    return pl.pallas_call(
        functools.partial(_lru_kernel, t_len=t, tc=tc),
        grid=(b,),
        in_specs=[pl.BlockSpec((t, w), lambda i: (rb + i, C_LX // w)),
                  pl.BlockSpec((t, w), lambda i: (rb + i, C_LG // w)),
                  full(cw), full(cb), full(wg), full(bg), full(sp),
                  pl.BlockSpec((1, 2, w), lambda i: (i, 0, 0))],
        out_specs=[pl.BlockSpec((t, w), lambda i: (i, 0)),
                   pl.BlockSpec((1, 2, w), lambda i: (i, 0, 0))],
        out_shape=[jax.ShapeDtypeStruct((b * t, w), jnp.bfloat16),
                   jax.ShapeDtypeStruct((b, 2, w), jnp.float32)],
        scratch_shapes=[pltpu.VMEM((tc // 8, 8, w), jnp.float32), pltpu.VMEM((tc // 8, 8, w), jnp.float32),
                        pltpu.VMEM((t, w), jnp.float32)],
        compiler_params=_cparams(1),
        name="rglru",
    )(p, p, cw, cb, wg, bg, sp, h0)


MLA_TM = 512
ATT_DQ = 256


def _mla_prep_kernel(q_ref, kv_ref, kr_ref, tab_ref, gq_ref, gkv_ref, wuq_ref, wukv_ref,
                     qo_ref, ko_ref, vo_ref, ckv_ref):
    f32, bf = jnp.float32, jnp.bfloat16
    tab = tab_ref[...]

    def rope(blk):
        prod = blk * tab
        return prod + pltpu.roll(prod, 64, 1)

    qn = (_rms(q_ref[...].astype(f32)) * gq_ref[...]).astype(bf)
    qq = jnp.dot(qn, wuq_ref[...], preferred_element_type=f32)
    ckv = _rms(kv_ref[...].astype(f32)) * gkv_ref[...]
    ckv_ref[...] = ckv
    kk = jnp.dot(ckv.astype(bf), wukv_ref[...], preferred_element_type=f32)
    lane = lax.broadcasted_iota(jnp.int32, tab.shape, 1)
    kpe = jnp.where(lane < MLA_ROPE, rope(kr_ref[...].astype(f32)), 0.0).astype(bf)
    for h in range(MLA_HEADS):
        o = h * ATT_DQ
        qo_ref[:, o:o + 128] = (qq[:, o:o + 128] * MLA_SCALE).astype(bf)
        qo_ref[:, o + 128:o + 256] = (rope(qq[:, o + 128:o + 256]) * MLA_SCALE).astype(bf)
        ko_ref[:, o:o + 128] = kk[:, o:o + 128].astype(bf)
        ko_ref[:, o + 128:o + 256] = kpe
        vo_ref[:, h * MLA_V:(h + 1) * MLA_V] = kk[:, o + 128:o + 256].astype(bf)


def _mla_weights(w_uq, w_ukv):
    wq = w_uq.reshape(MLA_Q_RANK, MLA_HEADS, MLA_NOPE + MLA_ROPE)
    pe = wq[:, :, MLA_NOPE:].reshape(MLA_Q_RANK, MLA_HEADS * MLA_ROPE)
    pep = _rot_partner_cols(pe, MLA_HEADS, MLA_ROPE).reshape(MLA_Q_RANK, MLA_HEADS, MLA_ROPE)
    wq_ext = jnp.concatenate([wq, pep], axis=-1).reshape(MLA_Q_RANK, MLA_HEADS * ATT_DQ)
    return wq_ext.astype(jnp.bfloat16), w_ukv.astype(jnp.bfloat16)


def _mla_prep(p, tab, gq, gkv, wuq, wukv):
    n = p.shape[0]
    tm = MLA_TM
    full = lambda a: pl.BlockSpec(a.shape, lambda i: (0,) * a.ndim)
    row = lambda width, col: pl.BlockSpec((tm, width), lambda i: (i, col // width))
    return pl.pallas_call(
        _mla_prep_kernel,
        grid=(n // tm,),
        in_specs=[row(MLA_Q_RANK, C_MQ), row(MLA_KV_RANK, C_MKV), row(128, C_KR), row(128, 0),
                  full(gq), full(gkv), full(wuq), full(wukv)],
        out_specs=[row(MLA_HEADS * ATT_DQ, 0), row(MLA_HEADS * ATT_DQ, 0), row(MLA_HEADS * MLA_V, 0),
                   row(MLA_KV_RANK, 0)],
        out_shape=[jax.ShapeDtypeStruct((n, MLA_HEADS * ATT_DQ), jnp.bfloat16),
                   jax.ShapeDtypeStruct((n, MLA_HEADS * ATT_DQ), jnp.bfloat16),
                   jax.ShapeDtypeStruct((n, MLA_HEADS * MLA_V), jnp.bfloat16),
                   jax.ShapeDtypeStruct((n, MLA_KV_RANK), jnp.float32)],
        compiler_params=_cparams(1),
        name="mla_prep",
    )(p, p, p, tab, gq, gkv, wuq, wukv)


ATT_TQ = 512
ATT_SUB = 256


def _attn_kernel(*refs, two, n_sub):
    dn = (((1,), (1,)), ((), ()))
    f32 = jnp.float32
    if two:
        q_ref, k_ref, v_ref, k2_ref, v2_ref, o_ref = refs
    else:
        q_ref, k_ref, v_ref, o_ref = refs
    rows = q_ref.shape[0] // n_sub
    for part in range(n_sub):
        sl = slice(part * rows, (part + 1) * rows)
        q = q_ref[sl, :]
        s = lax.dot_general(q, k_ref[...], dn, preferred_element_type=f32)
        m = jnp.max(s, axis=-1, keepdims=True)
        if two:
            s2 = lax.dot_general(q, k2_ref[...], dn, preferred_element_type=f32)
            m = jnp.maximum(m, jnp.max(s2, axis=-1, keepdims=True))
        p = jnp.exp(s - m)
        l = jnp.sum(p, axis=-1, keepdims=True)
        o = jnp.dot(p.astype(jnp.bfloat16), v_ref[...], preferred_element_type=f32)
        if two:
            p2 = jnp.exp(s2 - m)
            l = l + jnp.sum(p2, axis=-1, keepdims=True)
            o = o + jnp.dot(p2.astype(jnp.bfloat16), v2_ref[...], preferred_element_type=f32)
        o_ref[sl, :] = (o / l).astype(o_ref.dtype)


def _attention(q, k, v, row0, b, t, k2=None, v2=None):
    tb = min(ATT_TQ, t)
    nq = t // tb
    qb, kb = row0 // tb, row0 // t
    two = k2 is not None
    in_specs = [pl.BlockSpec((tb, ATT_DQ), lambda bi, h, i: (qb + bi * nq + i, h)),
                pl.BlockSpec((t, ATT_DQ), lambda bi, h, i: (kb + bi, h)),
                pl.BlockSpec((t, MLA_V), lambda bi, h, i: (kb + bi, h))]
    args = [q, k, v]
    if two:
        t2 = k2.shape[0] // b
        in_specs += [pl.BlockSpec((t2, ATT_DQ), lambda bi, h, i: (bi, h)),
                     pl.BlockSpec((t2, MLA_V), lambda bi, h, i: (bi, h))]
        args += [k2, v2]
    return pl.pallas_call(
        functools.partial(_attn_kernel, two=two, n_sub=max(tb // ATT_SUB, 1)),
        grid=(b, MLA_HEADS, nq),
        in_specs=in_specs,
        out_specs=pl.BlockSpec((tb, MLA_V), lambda bi, h, i: (bi * nq + i, h)),
        out_shape=jax.ShapeDtypeStruct((b * t, MLA_HEADS * MLA_V), jnp.bfloat16),
        compiler_params=_cparams(3),
        name="mla_attention",
    )(*args)


def _ret_tables():
    f32 = jnp.float32
    c = RET_CHUNK
    log_g = jnp.log1p(-jnp.exp2(-5.0 - jnp.arange(RET_HEADS, dtype=f32)))[:, None, None]
    idx = jnp.arange(c, dtype=f32)
    dec = jnp.exp(jnp.abs(idx[:, None] - idx[None, :])[None] * log_g)
    row = lambda e: jnp.broadcast_to(jnp.exp(e[None, :, None] * log_g), (RET_HEADS, c, c))
    return jnp.stack([dec, row(idx + 1.0), row(c - idx), row(c - 1.0 - idx), row(idx)], axis=0)


RET_HPS = 2


def _ret_kernel(*refs, nc, rope):
    f32, bf = jnp.float32, jnp.bfloat16
    if rope:
        (q_ref, k_ref, v_ref, g_ref, qp_ref, kp_ref, cos_ref, sin_ref, tab_ref, s0_ref, o_ref, fin_ref,
         acc_s, st_s) = refs
    else:
        q_ref, k_ref, v_ref, g_ref, tab_ref, s0_ref, o_ref, fin_ref, acc_s, st_s = refs
    c, dk = RET_CHUNK, RET_DK
    nt_dims, tn_dims = (((1,), (1,)), ((), ())), (((0,), (0,)), ((), ()))

    def chunk(k0, hh):
        sl = pl.ds(pl.multiple_of(k0 * c, c), c)
        cols = slice(hh * dk, (hh + 1) * dk)
        qc, kc = q_ref[sl, cols].astype(f32), k_ref[sl, cols].astype(f32)
        if rope:
            cos, sin = cos_ref[sl, :], sin_ref[sl, :]
            qc = qc * cos + qp_ref[sl, cols].astype(f32) * sin
            kc = kc * cos + kp_ref[sl, cols].astype(f32) * sin
        return sl, cols, qc.astype(bf), kc * (RET_DK ** -0.5), v_ref[sl, cols]

    st_s[...] = s0_ref[0]
    acc_s[...] = jnp.zeros_like(acc_s)

    def body(i, carry):
        for hh in range(RET_HPS):
            dec, xif, xib, zf, zb = (tab_ref[t, hh] for t in range(5))
            g_chunk = xif[c - 1:c, :]
            sl, cols, qc, kc, vc = chunk(i, hh)
            s = st_s[0, hh]
            sc = lax.dot_general(qc, kc.astype(bf), nt_dims, preferred_element_type=f32) * dec
            acc_s[sl, cols] += (jnp.dot(sc.astype(bf), vc, preferred_element_type=f32)
                                + jnp.dot(qc, s.astype(bf), preferred_element_type=f32) * xif)
            st_s[0, hh] = g_chunk * s + lax.dot_general((kc * zf).astype(bf), vc, tn_dims,
                                                        preferred_element_type=f32)
            sl, cols, qc, kc, vc = chunk(nc - 1 - i, hh)
            s = st_s[1, hh]
            acc_s[sl, cols] += jnp.dot(qc, s.astype(bf), preferred_element_type=f32) * xib
            st_s[1, hh] = g_chunk * s + lax.dot_general((kc * zb).astype(bf), vc, tn_dims,
                                                        preferred_element_type=f32)
        return carry

    lax.fori_loop(0, nc, body, 0)
    fin_ref[0] = st_s[...]

    def finish(i, carry):
        sl = pl.ds(pl.multiple_of(i * c, c), c)
        for hh in range(RET_HPS):
            cols = slice(hh * dk, (hh + 1) * dk)
            g = g_ref[sl, cols].astype(f32)
            o_ref[sl, cols] = (_rms(acc_s[sl, cols]) * (g * _sigmoid(g))).astype(o_ref.dtype)
        return carry

    lax.fori_loop(0, nc, finish, 0)


def _retention(p, row0, b, t, s0, tables, rope_tabs=None):
    nc = t // RET_CHUNK
    rb = row0 // t
    rope = rope_tabs is not None
    wd = RET_HPS * RET_DK
    col = lambda c0: pl.BlockSpec((t, wd), lambda bi, h: (rb + bi, c0 // wd + h))
    st = pl.BlockSpec((1, 2, RET_HPS, RET_DK, RET_DV), lambda bi, h: (bi, 0, h, 0, 0))
    in_specs = [col(C_RQ), col(C_RK), col(C_RV), col(C_RG)]
    args = [p, p, p, p]
    if rope:
        tab = pl.BlockSpec((t, RET_DK), lambda bi, h: (0, 0))
        in_specs += [col(C_RQP), col(C_RKP), tab, tab]
        args += [p, p, rope_tabs[0], rope_tabs[1]]
    in_specs += [pl.BlockSpec((5, RET_HPS, RET_CHUNK, RET_CHUNK), lambda bi, h: (0, h, 0, 0)), st]
    args += [tables, s0]
    return pl.pallas_call(
        functools.partial(_ret_kernel, nc=nc, rope=rope),
        grid=(b, RET_HEADS // RET_HPS),
        in_specs=in_specs,
        out_specs=[pl.BlockSpec((t, wd), lambda bi, h: (bi, h)), st],
        out_shape=[jax.ShapeDtypeStruct((b * t, RET_HEADS * RET_DV), jnp.bfloat16),
                   jax.ShapeDtypeStruct((b, 2, RET_HEADS, RET_DK, RET_DV), jnp.float32)],
        scratch_shapes=[pltpu.VMEM((t, wd), jnp.float32), pltpu.VMEM((2, RET_HPS, RET_DK, RET_DV), jnp.float32)],
        compiler_params=_cparams(2, V7X_VMEM_LIMIT_BIG),
        name="retention",
    )(*args)


OUT_TM = 512
ROUTER_PAD = 128


ROW_TILE = (8, D_MODEL // 16)


def _pack_halves(x):
    w = x.shape[1] // 2
    bits = lambda a: lax.bitcast_convert_type(a.astype(jnp.bfloat16).astype(jnp.float32), jnp.int32)
    return bits(x[:, w:]) | lax.shift_right_logical(bits(x[:, :w]), 16)


def _unpack_halves(u):
    lo = lax.bitcast_convert_type(lax.shift_left(u, 16), jnp.float32)
    hi = lax.bitcast_convert_type(u & jnp.int32(-65536), jnp.float32)
    return jnp.concatenate([lo, hi], axis=1)


def _split_bf16(x):
    hi = x.astype(jnp.bfloat16)
    return hi, (x - hi.astype(jnp.float32)).astype(jnp.bfloat16)


def _route(h, whi_ref, wlo_ref, rb_ref):
    f32 = jnp.float32
    hi, lo = _split_bf16(h)
    dn = (((1,), (1,)), ((), ()))
    lt = (lax.dot_general(whi_ref[...], hi, dn, preferred_element_type=f32)
          + lax.dot_general(whi_ref[...], lo, dn, preferred_element_type=f32)
          + lax.dot_general(wlo_ref[...], hi, dn, preferred_element_type=f32))[:N_EXPERTS]
    m = jnp.max(lt, axis=0, keepdims=True)
    e = jnp.exp(lt - m)
    sc = e / jnp.sum(e, axis=0, keepdims=True)
    sel = sc + rb_ref[...][:N_EXPERTS, 0:1]
    rows = lambda a: [a[j:j + 1, :] for j in range(N_EXPERTS)]
    sel_r, sc_r = rows(sel), rows(sc)
    epg = EXPERTS_PER_GROUP

    def top2sum(a, b, c, d):
        h1, l1, h2, l2 = jnp.maximum(a, b), jnp.minimum(a, b), jnp.maximum(c, d), jnp.minimum(c, d)
        return jnp.maximum(h1, h2) + jnp.maximum(jnp.minimum(h1, h2), jnp.maximum(l1, l2))

    gs = [top2sum(*sel_r[g * epg:(g + 1) * epg]) for g in range(N_EXPERT_GROUPS)]
    best, gi = gs[0], jnp.zeros_like(gs[0], dtype=jnp.int32)
    for g in range(1, N_EXPERT_GROUPS):
        upd = gs[g] > best
        gi = jnp.where(upd, g, gi)
        best = jnp.where(upd, gs[g], best)

    def pick(r, j):
        out = r[j]
        for g in range(1, N_EXPERT_GROUPS):
            out = jnp.where(gi == g, r[g * epg + j], out)
        return out

    v = [pick(sel_r, j) for j in range(epg)]
    s = [pick(sc_r, j) for j in range(epg)]

    def argmax_first(vals):
        bv, bi = vals[0], jnp.zeros_like(gi)
        for j in range(1, epg):
            upd = vals[j] > bv
            bi = jnp.where(upd, j, bi)
            bv = jnp.where(upd, vals[j], bv)
        return bi

    i1 = argmax_first(v)
    neg = jnp.float32(-jnp.inf)
    i2 = argmax_first([jnp.where(i1 == j, neg, v[j]) for j in range(epg)])
    take = lambda i: sum(jnp.where(i == j, s[j], 0.0) for j in range(epg))
    w1, w2 = take(i1), take(i2)
    tot = w1 + w2
    return (jnp.concatenate([gi * epg + i1, gi * epg + i2], axis=0),
            jnp.concatenate([w1 / tot, w2 / tot], axis=0))


def _mixout_kernel(ylin_ref, u_ref, ymla_c, ymla_l, ylru_c, ylru_l, yret_c, yret_l, x_ref, mod_ref, d_ref,
                   wglu_ref, wout_ref, g2n_ref, whi_ref, wlo_ref, rb_ref, xo_ref, h2_ref, idx_ref, wgt_ref,
                   *, n_ctx_tiles):
    f32, bf = jnp.float32, jnp.bfloat16
    w = GROUP_W
    is_ctx = pl.program_id(0) < n_ctx_tiles
    pick = lambda c_ref, l_ref: jnp.where(is_ctx, c_ref[...], l_ref[...])
    y = _gelu_tanh(ylin_ref[...].astype(f32) + d_ref[...] * u_ref[...].astype(f32))
    y5 = (y * _sigmoid(jnp.dot(y.astype(bf), wglu_ref[...], preferred_element_type=f32))).astype(bf)
    mix = (jnp.dot(y5, wout_ref[0:w, :], preferred_element_type=f32)
           + jnp.dot(pick(ymla_c, ymla_l), wout_ref[w:2 * w, :], preferred_element_type=f32)
           + jnp.dot(pick(ylru_c, ylru_l), wout_ref[2 * w:3 * w, :], preferred_element_type=f32)
           + jnp.dot(pick(yret_c, yret_l), wout_ref[3 * w:4 * w, :], preferred_element_type=f32))
    x = x_ref[...] + mod_ref[0, 2:3, :] * mix
    xo_ref[...] = x
    h2 = _rms(x) * g2n_ref[...] * (1.0 + mod_ref[0, 4:5, :]) + mod_ref[0, 3:4, :]
    h2_ref[...] = _pack_halves(h2).reshape(h2_ref.shape)
    idx, wgt = _route(h2, whi_ref, wlo_ref, rb_ref)
    idx_ref[...] = idx
    wgt_ref[...] = wgt


def _router_weights(router_w, router_b):
    d = router_w.shape[0]
    wt = jnp.zeros((ROUTER_PAD, d), jnp.float32).at[:N_EXPERTS].set(router_w.T.astype(jnp.float32))
    whi, wlo = _split_bf16(wt)
    rb = jnp.zeros((ROUTER_PAD, 128), jnp.float32).at[:N_EXPERTS].set(router_b.astype(jnp.float32)[:, None])
    return whi, wlo, rb


def _mixout(ylin, p, ymla, ylru, yret, x, mods, s5_d, wglu, wout, g2n, rw, n_ctx, dec_seq):
    n, d = x.shape
    tm = OUT_TM
    w = GROUP_W
    na = n_ctx // tm
    seq = functools.partial(_seq_of_tile, tm=tm, n_ctx=n_ctx, dec_seq=dec_seq)
    full = lambda a: pl.BlockSpec(a.shape, lambda i: (0,) * a.ndim)
    row = lambda width: pl.BlockSpec((tm, width), lambda i: (i, 0))
    ctx_row = pl.BlockSpec((tm, w), lambda i: (jnp.minimum(i, na - 1), 0))
    lat_row = pl.BlockSpec((tm, w), lambda i: (jnp.maximum(i - na, 0), 0))
    lanes = pl.BlockSpec((TOP_K, tm), lambda i: (0, i))
    whi, wlo, rb = rw
    return pl.pallas_call(
        functools.partial(_mixout_kernel, n_ctx_tiles=na),
        grid=(n // tm,),
        in_specs=[row(w), row(w), ctx_row, lat_row, ctx_row, lat_row, ctx_row, lat_row, row(d),
                  pl.BlockSpec((1, N_MOD, d), lambda i: (seq(i), 0, 0)),
                  full(s5_d), full(wglu), full(wout), full(g2n), full(whi), full(wlo), full(rb)],
        out_specs=[row(d), pl.BlockSpec((tm,) + ROW_TILE, lambda i: (i, 0, 0)), lanes, lanes],
        out_shape=[jax.ShapeDtypeStruct((n, d), jnp.float32), jax.ShapeDtypeStruct((n,) + ROW_TILE, jnp.int32),
                   jax.ShapeDtypeStruct((TOP_K, n), jnp.int32), jax.ShapeDtypeStruct((TOP_K, n), jnp.float32)],
        compiler_params=_cparams(1, V7X_VMEM_LIMIT_BIG),
        name="mix_out_norm_route",
    )(ylin, p, *ymla, *ylru, *yret, x, mods, s5_d, wglu, wout, g2n, whi, wlo, rb)


MOE_TM = 256
MOE_NB = 256


def _expert_kernel(te_ref, nt_ref, nv_ref, tok0_ref, tok1_ref, w_ref, dstp_ref, dst_ref, wg_ref, wu_ref, wd_ref,
                   h_hbm, y_hbm, wg_s, wu_s, wd_s, xbuf, xs_s, act_s, ybuf, gsem, ssem):
    i = pl.program_id(0)
    bf = jnp.bfloat16
    tm = xs_s.shape[0]
    nt = nt_ref[0]

    def gather_start(tok_ref, sl, unroll):
        def row(r, carry):
            pltpu.make_async_copy(h_hbm.at[tok_ref[0, 0, r]], xbuf.at[sl, r], gsem.at[sl]).start()
            return carry

        lax.fori_loop(0, tm, row, 0, unroll=unroll)

    def gather_done(sl):
        pltpu.make_async_copy(h_hbm.at[pl.ds(0, tm)], xbuf.at[sl], gsem.at[sl]).wait()

    def scatter_row(d_ref, sl, r):
        return pltpu.make_async_copy(ybuf.at[sl, r], y_hbm.at[d_ref[0, 0, r]], ssem.at[sl])

    def scatter_done(sl, n_rows):
        pltpu.make_async_copy(ybuf.at[sl, pl.ds(0, n_rows)], y_hbm.at[pl.ds(0, n_rows)], ssem.at[sl]).wait()

    def tile(slot):
        n_prev = jnp.where(i > 0, nv_ref[jnp.maximum(i - 1, 0)], 0)
        f = wg_s.shape[1]
        d = wd_s.shape[1]
        up_pieces, down_pieces = 2 * (f // MOE_NB), d // MOE_NB

        def gather_rows(piece):
            for r in range(piece * tm // up_pieces, (piece + 1) * tm // up_pieces):
                pltpu.make_async_copy(h_hbm.at[tok1_ref[0, 0, r]], xbuf.at[1 - slot, r], gsem.at[1 - slot]).start()

        def scatter_rows(piece):
            for r in range(piece * tm // down_pieces, (piece + 1) * tm // down_pieces):
                @pl.when(r < n_prev)
                def _():
                    scatter_row(dstp_ref, 1 - slot, r).start()

        gather_done(slot)
        xs_s[...] = _unpack_halves(xbuf[slot].reshape(tm, -1)).astype(bf)
        x = xs_s[...]
        gs, us = [], []
        for c in range(f // MOE_NB):
            cols = slice(c * MOE_NB, (c + 1) * MOE_NB)
            gs.append(jnp.dot(x, wg_s[:, cols], preferred_element_type=jnp.float32))
            gather_rows(2 * c)
            us.append(jnp.dot(x, wu_s[:, cols], preferred_element_type=jnp.float32))
            gather_rows(2 * c + 1)
        g = jnp.concatenate(gs, axis=1)
        u = jnp.concatenate(us, axis=1)
        act_s[...] = ((g * _sigmoid(g)) * u * w_ref[...]).astype(bf)

        @pl.when(i > 1)
        def _():
            scatter_done(slot, nv_ref[jnp.maximum(i - 2, 0)])

        act = act_s[...]
        ys = []
        for c in range(down_pieces):
            ys.append(jnp.dot(act, wd_s[:, c * MOE_NB:(c + 1) * MOE_NB], preferred_element_type=jnp.float32))
            scatter_rows(c)
        ybuf[slot] = _pack_halves(jnp.concatenate(ys, axis=1)).reshape(ybuf.shape[1:])

        @pl.when(i == nt - 1)
        def _():
            def row(r, carry):
                scatter_row(dst_ref, slot, r).start()
                return carry

            lax.fori_loop(0, nv_ref[i], row, 0)

            @pl.when(i > 0)
            def _():
                scatter_done(1 - slot, nv_ref[jnp.maximum(i - 1, 0)])

            scatter_done(slot, nv_ref[i])
            gather_done(1 - slot)

    @pl.when(i < nt)
    def _():
        @pl.when(i == 0)
        def _():
            gather_start(tok0_ref, 0, 8)

        @pl.when((i == 0) | (te_ref[i] != te_ref[jnp.maximum(i - 1, 0)]))
        def _():
            wg_s[...] = wg_ref[0, 0].astype(bf)
            wu_s[...] = wu_ref[0, 0].astype(bf)
            wd_s[...] = wd_ref[0, 0].astype(bf)

        @pl.when(i % 2 == 0)
        def _():
            tile(0)

        @pl.when(i % 2 == 1)
        def _():
            tile(1)


def _experts(h2p, layer, ws, tok, dst, tile_expert, n_tiles_used, n_valid, wg, wu, wd):
    n = h2p.shape[0]
    d = 2 * ROW_TILE[0] * ROW_TILE[1]
    tm = MOE_TM
    n_tiles = tok.shape[0]
    f = wg.shape[-1]
    smem_row = lambda delta: pl.BlockSpec(
        (1, 1, tm), lambda i, te, nt, nv: (jnp.clip(i + delta, 0, n_tiles - 1), 0, 0), memory_space=pltpu.SMEM)
    wspec = lambda a, b: pl.BlockSpec((1, 1, a, b), lambda i, te, nt, nv: (layer, te[i], 0, 0))
    return pl.pallas_call(
        _expert_kernel,
        grid_spec=pltpu.PrefetchScalarGridSpec(
            num_scalar_prefetch=3,
            grid=(n_tiles,),
            in_specs=[smem_row(0), smem_row(1),
                      pl.BlockSpec((tm, 1), lambda i, te, nt, nv: (i, 0)),
                      smem_row(-1), smem_row(0),
                      wspec(d, f), wspec(d, f), wspec(f, d),
                      pl.BlockSpec(memory_space=pl.ANY)],
            out_specs=pl.BlockSpec(memory_space=pl.ANY),
            scratch_shapes=[pltpu.VMEM((d, f), jnp.bfloat16), pltpu.VMEM((d, f), jnp.bfloat16),
                            pltpu.VMEM((f, d), jnp.bfloat16), pltpu.VMEM((2, tm) + ROW_TILE, jnp.int32),
                            pltpu.VMEM((tm, d), jnp.bfloat16), pltpu.VMEM((tm, f), jnp.bfloat16),
                            pltpu.VMEM((2, tm) + ROW_TILE, jnp.int32),
                            pltpu.SemaphoreType.DMA((2,)), pltpu.SemaphoreType.DMA((2,))]),
        out_shape=jax.ShapeDtypeStruct((TOP_K * n,) + ROW_TILE, jnp.int32),
        compiler_params=pltpu.CompilerParams(dimension_semantics=("arbitrary",),
                                             vmem_limit_bytes=V7X_VMEM_LIMIT_BIG, disable_bounds_checks=True),
        name="moe_experts",
    )(tile_expert, n_tiles_used, n_valid, tok, tok, ws, dst, dst, wg, wu, wd, h2p)


def _moe_dispatch(idx, wgt):
    n = idx.shape[1]
    tm = MOE_TM
    n_pairs = TOP_K * n
    m_pad = n_pairs + N_EXPERTS * tm
    n_tiles = m_pad // tm
    e_flat = idx.reshape(-1)
    order = jnp.argsort(e_flat, stable=True).astype(jnp.int32)
    experts = jnp.arange(N_EXPERTS, dtype=jnp.int32)
    counts = jnp.sum((e_flat[None, :] == experts[:, None]).astype(jnp.int32), axis=1)
    starts_raw = jnp.cumsum(counts) - counts
    padded = ((counts + tm - 1) // tm) * tm
    ends = jnp.cumsum(padded)
    starts_pad = ends - padded
    tile_start = jnp.arange(n_tiles, dtype=jnp.int32) * tm
    tile_expert = jnp.minimum(jnp.sum((tile_start[:, None] >= ends[None, :]).astype(jnp.int32), axis=1),
                              N_EXPERTS - 1)
    r = jnp.arange(m_pad, dtype=jnp.int32)
    te_r = jnp.repeat(tile_expert, tm)
    off = r - jnp.take(starts_pad, te_r)
    valid = (off >= 0) & (off < jnp.take(counts, te_r))
    pair = jnp.take(order, jnp.clip(jnp.take(starts_raw, te_r) + off, 0, n_pairs - 1))
    tok = jnp.where(valid, pair % n, 0).reshape(n_tiles, 1, tm)
    dst = jnp.where(valid, pair, 0).reshape(n_tiles, 1, tm)
    ws = jnp.where(valid, jnp.take(wgt.reshape(-1), pair), 0.0)[:, None]
    n_tiles_used = (ends[-1] // tm).astype(jnp.int32).reshape(1)
    n_valid = jnp.sum(valid.reshape(n_tiles, tm), axis=1).astype(jnp.int32)
    return tok, dst, ws, tile_expert, n_tiles_used, n_valid


def _moe_rows(h2p, layer, idx, wgt, wg, wu, wd):
    tok, dst, ws, tile_expert, n_tiles_used, n_valid = _moe_dispatch(idx, wgt)
    return _experts(h2p, layer, ws, tok, dst, tile_expert, n_tiles_used, n_valid, wg, wu, wd)


RES_TM = 512


def _resid_kernel(x_ref, ya_ref, yb_ref, mod_ref, gf_ref, *o_refs, final, n_ctx_tiles):
    rows = lambda r: _unpack_halves(r[...].reshape(r.shape[0], -1))
    x = x_ref[...] + mod_ref[0, 5:6, :] * (rows(ya_ref) + rows(yb_ref))
    if not final:
        o_refs[0][...] = x
    else:
        y = _rms(x) * gf_ref[...]
        i = pl.program_id(0)

        @pl.when(i < n_ctx_tiles)
        def _():
            o_refs[0][...] = y

        @pl.when(i >= n_ctx_tiles)
        def _():
            o_refs[1][...] = y


def _resid(x, y2, mods, gf, n_ctx, dec_seq, final):
    n, d = x.shape
    tm = RES_TM
    na = n_ctx // tm
    seq = functools.partial(_seq_of_tile, tm=tm, n_ctx=n_ctx, dec_seq=dec_seq)
    row = pl.BlockSpec((tm, d), lambda i: (i, 0))
    if final:
        out_specs = [pl.BlockSpec((tm, d), lambda i: (jnp.minimum(i, na - 1), 0)),
                     pl.BlockSpec((tm, d), lambda i: (jnp.maximum(i - na, 0), 0))]
        out_shape = [jax.ShapeDtypeStruct((n_ctx, d), jnp.float32), jax.ShapeDtypeStruct((n - n_ctx, d), jnp.float32)]
    else:
        out_specs, out_shape = row, jax.ShapeDtypeStruct((n, d), jnp.float32)
    return pl.pallas_call(
        functools.partial(_resid_kernel, final=final, n_ctx_tiles=na),
        grid=(n // tm,),
        in_specs=[row, pl.BlockSpec((tm,) + ROW_TILE, lambda i: (i, 0, 0)),
                  pl.BlockSpec((tm,) + ROW_TILE, lambda i: (n // tm + i, 0, 0)),
                  pl.BlockSpec((1, N_MOD, d), lambda i: (seq(i), 0, 0)),
                  pl.BlockSpec((1, d), lambda i: (0, 0))],
        out_specs=out_specs,
        out_shape=out_shape,
        compiler_params=_cparams(1),
        name="moe_residual_norm",
    )(x, y2, y2, mods, gf)


def kernel(x_prompt, x_sample, c, cache_mla_ckv, cache_mla_kpe, state_s5, state_lru, state_ret,
           c_ctx, w_ada, b_ada, norm1_g, norm2_g, w_in, w_out,
           s5_a_re, s5_a_im, s5_log_dt, s5_b_re, s5_b_im, s5_c_re, s5_c_im, s5_d, s5_w_glu,
           mla_q_norm_g, mla_w_uq, mla_kv_norm_g, mla_w_ukv,
           lru_conv_w, lru_conv_b, lru_w_a, lru_b_a, lru_w_x, lru_b_x, lru_lambda,
           router_w, router_b, moe_w_gate, moe_w_up, moe_w_down, final_norm_g):
    f32, bf = jnp.float32, jnp.bfloat16
    batch, seq_len, d = x_prompt.shape
    dec_batch, dec_seq, _ = x_sample.shape
    past = cache_mla_ckv.shape[2]
    n_ctx, n_lat = batch * seq_len, dec_batch * dec_seq
    depth = w_in.shape[0]

    x = jnp.concatenate([x_prompt.reshape(n_ctx, d), x_sample.reshape(n_lat, d)], axis=0)
    cvec = jnp.zeros((8, d), f32).at[0].set(c_ctx).at[1:1 + dec_batch].set(c)
    mods_all = _ada(cvec, w_ada, b_ada).reshape(depth, 8, N_MOD, d)

    cos64, sin64 = _rope_tables(dec_seq, MLA_ROPE)
    lat_tab = jnp.tile(jnp.concatenate([cos64, sin64], axis=1), (dec_batch, 1))
    ctx_tab = jnp.concatenate([jnp.ones((n_ctx, MLA_ROPE), f32), jnp.zeros((n_ctx, MLA_ROPE), f32)], axis=1)
    mla_tab = jnp.concatenate([ctx_tab, lat_tab], axis=0)
    ret_rope = _rope_tables(dec_seq, RET_DK)
    ret_tables = _ret_tables()
    rw = _router_weights(router_w, router_b)
    gf = final_norm_g[None, :]

    states = []
    for l in range(depth):
        mods = mods_all[l]
        s5_tab = _s5_tables(s5_a_re[l], s5_a_im[l], s5_log_dt[l], s5_b_re[l], s5_b_im[l], s5_c_re[l], s5_c_im[l])
        lru_tab = _lru_tables(lru_conv_w[l], lru_conv_b[l], lru_w_a[l], lru_b_a[l], lru_w_x[l], lru_b_x[l],
                              lru_lambda[l])
        wuq, wukv = _mla_weights(mla_w_uq[l], mla_w_ukv[l])

        p = _inproj(x, mods, norm1_g[l][None, :], _inproj_weights(w_in[l]), n_ctx, dec_seq)

        u_ctx = p[:n_ctx, C_S5:C_S5 + S5_CH].reshape(batch, seq_len, S5_CH)
        u_lat = p[n_ctx:, C_S5:C_S5 + S5_CH].reshape(dec_batch, dec_seq, S5_CH)
        ylin, s5_fins = _s5_core([u_ctx, u_lat], [None, state_s5[:, l]], s5_tab)

        qo, ko, vo, ckv = _mla_prep(p, mla_tab, mla_q_norm_g[l][None, :], mla_kv_norm_g[l][None, :], wuq, wukv)
        ckv_c = cache_mla_ckv[:, l].reshape(dec_batch * past, MLA_KV_RANK)
        kv_c = _matmul(ckv_c, wukv, out_dtype=bf).reshape(dec_batch * past, MLA_HEADS, 2, 128)
        kpe_c = jnp.broadcast_to(cache_mla_kpe[:, l].reshape(dec_batch * past, 1, MLA_ROPE).astype(bf),
                                 (dec_batch * past, MLA_HEADS, MLA_ROPE))
        k_c = jnp.concatenate([kv_c[:, :, 0], kpe_c, jnp.zeros_like(kpe_c)], axis=-1).reshape(
            dec_batch * past, MLA_HEADS * ATT_DQ)
        v_c = kv_c[:, :, 1].reshape(dec_batch * past, MLA_HEADS * MLA_V)
        ymla = (_attention(qo, ko, vo, 0, batch, seq_len),
                _attention(qo, ko, vo, n_ctx, dec_batch, dec_seq, k_c, v_c))

        ylru_c, lru_fin = _lru(p, 0, batch, seq_len, jnp.zeros((batch, 2, LRU_W), f32), lru_tab)
        ylru_l, _ = _lru(p, n_ctx, dec_batch, dec_seq, state_lru[:, l].astype(f32), lru_tab)
        ylru = (ylru_c, ylru_l)

        yret_c, ret_fin = _retention(p, 0, batch, seq_len,
                                     jnp.zeros((batch, 2, RET_HEADS, RET_DK, RET_DV), f32), ret_tables)
        yret_l, _ = _retention(p, n_ctx, dec_batch, dec_seq, state_ret[:, l].astype(f32), ret_tables, ret_rope)
        yret = (yret_c, yret_l)

        x_mid, h2, idx, wgt = _mixout(ylin, p, ymla, ylru, yret, x, mods, s5_d[l][None, :],
                                      s5_w_glu[l].astype(bf), w_out[l].astype(bf), norm2_g[l][None, :], rw,
                                      n_ctx, dec_seq)
        y2 = _moe_rows(h2, l, idx, wgt, moe_w_gate, moe_w_up, moe_w_down)
        x = _resid(x_mid, y2, mods, gf, n_ctx, dec_seq, final=(l == depth - 1))

        states.append((ckv[:n_ctx].reshape(batch, seq_len, MLA_KV_RANK),
                       p[:n_ctx, C_KR:C_KR + MLA_ROPE].astype(f32).reshape(batch, seq_len, MLA_ROPE),
                       s5_fins[0], lru_fin, ret_fin))

    y_prompt = x[0].reshape(batch, seq_len, d)
    y_sample = x[1].reshape(dec_batch, dec_seq, d)
    new_cache_mla_ckv = jnp.stack([st[0] for st in states], axis=1)
    new_cache_mla_kpe = jnp.stack([st[1] for st in states], axis=1)
    new_state_s5 = jnp.stack([st[2] for st in states], axis=1)
    new_state_lru = jnp.stack([st[3] for st in states], axis=1)
    new_state_ret = jnp.stack([st[4] for st in states], axis=1)
    return (y_prompt, y_sample, new_cache_mla_ckv, new_cache_mla_kpe, new_state_s5, new_state_lru, new_state_ret)
```

```python
import functools
import numpy as np
import jax
import jax.numpy as jnp
from jax import lax
from jax.experimental import pallas as pl
from jax.experimental.pallas import tpu as pltpu

D_MODEL = 2048
DEPTH = 2
GRID_W = 64
EPS = 1e-6
ROPE_BASE = 10000.0
N_MOD = 6
GROUP_W = 512
S5_CH = GROUP_W
S5_GROUP_CH = 16
S5_GROUPS = S5_CH // S5_GROUP_CH
S5_STATE = 64
MLA_HEADS = 4
MLA_NOPE = 128
MLA_ROPE = 64
MLA_V = 128
MLA_Q_RANK = GROUP_W
MLA_KV_RANK = GROUP_W // 2
MLA_SCALE = (MLA_NOPE + MLA_ROPE) ** -0.5
LRU_W = GROUP_W
LRU_CONV = 4
LRU_C = 8.0
RET_HEADS = 4
RET_DK = 128
RET_DV = 128
RET_CHUNK = 128
N_EXPERTS = 16
N_EXPERT_GROUPS = 4
EXPERTS_PER_GROUP = N_EXPERTS // N_EXPERT_GROUPS
TOP_K = 2
D_EXPERT = D_MODEL // 4

V7X_VMEM_LIMIT = 48 * 1024 * 1024
V7X_VMEM_LIMIT_BIG = 56 * 1024 * 1024

C_S5, C_MQ, C_LX, C_LG, C_RQ, C_RK, C_RV, C_RG, C_RQP, C_RKP, C_MKV, C_KR = (
    0, 512, 1024, 1536, 2048, 2560, 3072, 3584, 4096, 4608, 5120, 5376)
P_COLS = 5632
R_S5, R_MQ, R_MKV, R_KR, R_LX, R_LG, R_RQ, R_RK, R_RV, R_RG = (0, 512, 1024, 1280, 1344, 1856, 2368, 2880, 3392, 3904)


def _cparams(n_axes, limit=V7X_VMEM_LIMIT):
    return pltpu.CompilerParams(dimension_semantics=("arbitrary",) * n_axes, vmem_limit_bytes=limit)


def _sigmoid(x):
    return 1.0 / (1.0 + jnp.exp(-x))


def _gelu_tanh(x):
    return 0.5 * x * (1.0 + jnp.tanh(0.7978845608028654 * (x + 0.044715 * (x * x * x))))


def _rms(x):
    return x * lax.rsqrt(jnp.mean(x * x, axis=-1, keepdims=True) + EPS)


def _seq_of_tile(i, tm, n_ctx, dec_seq):
    assert n_ctx % tm == 0 and dec_seq % tm == 0, "a row tile must not straddle two sequences"
    r = i * tm
    return jnp.where(r < n_ctx, 0, 1 + (r - n_ctx) // dec_seq)


def _mm_kernel(x_ref, w_ref, o_ref):
    o_ref[...] = jnp.dot(x_ref[...].astype(jnp.bfloat16), w_ref[...].astype(jnp.bfloat16),
                         preferred_element_type=jnp.float32).astype(o_ref.dtype)


def _matmul(x, w, tm=512, tn=512, out_dtype=jnp.float32):
    m, k = x.shape
    _, n = w.shape
    tm, tn = min(tm, m), min(tn, n)
    return pl.pallas_call(
        _mm_kernel,
        grid=(n // tn, m // tm),
        in_specs=[pl.BlockSpec((tm, k), lambda j, i: (i, 0)), pl.BlockSpec((k, tn), lambda j, i: (0, j))],
        out_specs=pl.BlockSpec((tm, tn), lambda j, i: (i, j)),
        out_shape=jax.ShapeDtypeStruct((m, n), out_dtype),
        compiler_params=_cparams(2),
        name="matmul",
    )(x, w)


ADA_TN = 1024


def _ada_kernel(c_ref, w_ref, b_ref, o_ref):
    c = c_ref[...]
    s = (c * _sigmoid(c)).astype(jnp.bfloat16)
    o_ref[0] = jnp.dot(s, w_ref[0].astype(jnp.bfloat16), preferred_element_type=jnp.float32) + b_ref[0]


def _ada(cvec, w_ada, b_ada):
    depth, d, n = w_ada.shape
    return pl.pallas_call(
        _ada_kernel,
        grid=(depth, n // ADA_TN),
        in_specs=[pl.BlockSpec((8, d), lambda l, j: (0, 0)),
                  pl.BlockSpec((1, d, ADA_TN), lambda l, j: (l, 0, j)),
                  pl.BlockSpec((1, 1, ADA_TN), lambda l, j: (l, 0, j))],
        out_specs=pl.BlockSpec((1, 8, ADA_TN), lambda l, j: (l, 0, j)),
        out_shape=jax.ShapeDtypeStruct((depth, 8, n), jnp.float32),
        compiler_params=_cparams(2),
        name="adaln_mod",
    )(cvec, w_ada, b_ada.reshape(depth, 1, n))


IN_TM = 1024
IN_TN = 512


def _inproj_kernel(xc_ref, xl_ref, mod_ref, g_ref, w_ref, o_ref, h_s, *, n_ctx_tiles):
    @pl.when(pl.program_id(1) == 0)
    def _():
        x = jnp.where(pl.program_id(0) < n_ctx_tiles, xc_ref[...], xl_ref[...])
        h = _rms(x) * g_ref[...]
        h_s[...] = (h * (1.0 + mod_ref[0, 1:2, :]) + mod_ref[0, 0:1, :]).astype(h_s.dtype)

    o_ref[...] = jnp.dot(h_s[...], w_ref[...], preferred_element_type=jnp.float32).astype(o_ref.dtype)


def _two_part_rows(tm, width, n_ctx_tiles):
    return (pl.BlockSpec((tm, width), lambda i, *_: (jnp.minimum(i, n_ctx_tiles - 1), 0)),
            pl.BlockSpec((tm, width), lambda i, *_: (jnp.maximum(i - n_ctx_tiles, 0), 0)))


def _inproj(x, mods, g, w, n_ctx, dec_seq):
    n, d = x[0].shape[0] + x[1].shape[0], x[0].shape[1]
    na = n_ctx // IN_TM
    seq = functools.partial(_seq_of_tile, tm=IN_TM, n_ctx=n_ctx, dec_seq=dec_seq)
    return pl.pallas_call(
        functools.partial(_inproj_kernel, n_ctx_tiles=na),
        grid=(n // IN_TM, P_COLS // IN_TN),
        in_specs=[*_two_part_rows(IN_TM, d, na),
                  pl.BlockSpec((1, N_MOD, d), lambda i, j: (seq(i), 0, 0)),
                  pl.BlockSpec((1, d), lambda i, j: (0, 0)),
                  pl.BlockSpec((d, IN_TN), lambda i, j: (0, j))],
        out_specs=pl.BlockSpec((IN_TM, IN_TN), lambda i, j: (i, j)),
        out_shape=jax.ShapeDtypeStruct((n, P_COLS), jnp.bfloat16),
        scratch_shapes=[pltpu.VMEM((IN_TM, d), jnp.bfloat16)],
        compiler_params=_cparams(2, V7X_VMEM_LIMIT_BIG),
        name="norm_inproj",
    )(*x, mods, g, w)


def _rot_partner_cols(w, n_heads, head_dim):
    q = head_dim // 4
    wr = w.reshape(w.shape[0], n_heads, 2, 2, q)
    return jnp.stack([-wr[:, :, :, 1], wr[:, :, :, 0]], axis=3).reshape(w.shape)


def _inproj_weights(w_in):
    seg = lambda off, width: w_in[:, off:off + width]
    kr = seg(R_KR, MLA_ROPE)
    cols = [seg(R_S5, 2 * GROUP_W),
            seg(R_LX, 6 * GROUP_W),
            _rot_partner_cols(seg(R_RQ, 2 * GROUP_W), 2 * RET_HEADS, RET_DK),
            seg(R_MKV, MLA_KV_RANK), kr, _rot_partner_cols(kr, 1, MLA_ROPE),
            jnp.zeros((w_in.shape[0], P_COLS - C_KR - 2 * MLA_ROPE), w_in.dtype)]
    return jnp.concatenate(cols, axis=1).astype(jnp.bfloat16)


def _rope_tables(t_len, rot_dim):
    rows = t_len // GRID_W
    row = jnp.repeat(jnp.arange(rows, dtype=jnp.float32), GRID_W)
    col = jnp.tile(jnp.arange(GRID_W, dtype=jnp.float32), rows)
    n_freq = rot_dim // 4
    inv = ROPE_BASE ** (-jnp.arange(n_freq, dtype=jnp.float32) / n_freq)
    ang = jnp.concatenate([row[:, None] * inv[None]] * 2 + [col[:, None] * inv[None]] * 2, axis=1)
    return jnp.cos(ang), jnp.sin(ang)


S5_L = 32
S5_PAIRS = S5_GROUPS // 2
S5_Q = 4
S5_LANES = S5_GROUPS * S5_STATE
S5_SCAN_LANES = 1024


def _s5_tables(a_re, a_im, log_dt, b_re, b_im, c_re, c_im):
    f32 = jnp.float32
    L, G, P, C = S5_L, S5_GROUPS, S5_STATE, S5_GROUP_CH
    hp = lax.Precision.HIGHEST
    cmul = lambda xr, xi, yr, yi: (xr * yr - xi * yi, xr * yi + xi * yr)
    dt = jnp.exp(log_dt.astype(f32))[..., None]
    zr, zi = a_re * dt, a_im * dt
    ab_r, ab_i = jnp.exp(zr) * jnp.cos(zi), jnp.exp(zr) * jnp.sin(zi)
    den = a_re * a_re + a_im * a_im
    nr, ni = ab_r - 1.0, ab_i
    be_r = (nr * a_re + ni * a_im) / den
    be_i = (ni * a_re - nr * a_im) / den
    bt_r, bt_i = b_re.transpose(0, 2, 1), b_im.transpose(0, 2, 1)
    bp_r, bp_i = cmul(be_r[:, :, None, :], be_i[:, :, None, :], bt_r[None], bt_i[None])
    tau = jnp.arange(L + 1, dtype=f32)[None, None, :, None]
    mag = jnp.exp(zr[:, :, None, :] * tau)
    pw_r, pw_i = mag * jnp.cos(zi[:, :, None, :] * tau), mag * jnp.sin(zi[:, :, None, :] * tau)

    cat = lambda *xs: jnp.concatenate(xs, axis=-1)
    pf_r, pf_i = pw_r[0][:, :L][:, ::-1], pw_i[0][:, :L][:, ::-1]
    pb_r, pb_i = pw_r[1][:, :L], pw_i[1][:, :L]
    w1 = cat(pf_r, pf_i, pb_r, pb_i)[:, :, None, :]
    w2 = cat(pf_i, pf_r, pb_i, pb_r)[:, :, None, :]
    b1 = cat(bp_r[0], bp_r[0], bp_r[1], bp_r[1])[:, None]
    b2 = cat(-bp_i[0], bp_i[0], -bp_i[1], bp_i[1])[:, None]
    ws = (w1 * b1 + w2 * b2).reshape(G, L * C, S5_Q * P)

    ct_r, ct_i = c_re.transpose(0, 2, 1), c_im.transpose(0, 2, 1)
    pt_r, pt_i = pw_r.transpose(0, 1, 3, 2), pw_i.transpose(0, 1, 3, 2)

    def cm(er, ei):
        m_r, m_i = cmul(ct_r[:, :, None, :], ct_i[:, :, None, :], er[..., None], ei[..., None])
        return m_r.reshape(G, P, L * C), m_i.reshape(G, P, L * C)

    mf_r, mf_i = cm(pt_r[0][:, :, 1:L + 1], pt_i[0][:, :, 1:L + 1])
    mb_r, mb_i = cm(pt_r[1][:, :, 1:L + 1][:, :, ::-1], pt_i[1][:, :, 1:L + 1][:, :, ::-1])
    wo = jnp.stack([mf_r, -mf_i, mb_r, -mb_i], axis=1).reshape(G, S5_Q * P, L * C)

    def impulse(d):
        m_r, m_i = cm(pt_r[d][:, :, :L], pt_i[d][:, :, :L])
        kk = (jnp.einsum('gkp,gpx->gkx', bp_r[d], m_r, precision=hp)
              - jnp.einsum('gkp,gpx->gkx', bp_i[d], m_i, precision=hp))
        return kk.reshape(G, C, L, C)

    kf, kb = impulse(0), impulse(1)
    wide = jnp.concatenate([kb[:, :, :0:-1], kf[:, :, :1] + kb[:, :, :1], kf[:, :, 1:]], axis=2)
    wide = jnp.pad(wide.reshape(G, C, (2 * L - 1) * C), ((0, 0), (0, 0), (0, C)))

    a_l = jnp.stack([pw_r[0, :, L], pw_i[0, :, L], pw_r[1, :, L], pw_i[1, :, L]], axis=0).reshape(S5_Q, 1, S5_LANES)
    bf = jnp.bfloat16
    return _s5_toeplitz(wide), ws.astype(bf), wo.astype(bf), a_l


def _s5_toeplitz_kernel(w_ref, o_ref):
    w = w_ref[0]
    for s in range(S5_L):
        off = (S5_L - 1 - s) * S5_GROUP_CH
        o_ref[0, s * S5_GROUP_CH:(s + 1) * S5_GROUP_CH, :] = w[:, off:off + S5_L * S5_GROUP_CH].astype(o_ref.dtype)


def _s5_toeplitz(wide):
    g, c, wl = wide.shape
    n = S5_L * S5_GROUP_CH
    return pl.pallas_call(
        _s5_toeplitz_kernel,
        grid=(g,),
        in_specs=[pl.BlockSpec((1, c, wl), lambda i: (i, 0, 0))],
        out_specs=pl.BlockSpec((1, n, n), lambda i: (i, 0, 0)),
        out_shape=jax.ShapeDtypeStruct((g, n, n), jnp.bfloat16),
        compiler_params=_cparams(1),
        name="s5_toeplitz",
    )(wide)


def _s5a_kernel(x_ref, t_ref, ws_ref, y1_ref, s_ref):
    ss = []
    for j in range(2):
        x = x_ref[j]
        y1_ref[j] = jnp.dot(x, t_ref[j], preferred_element_type=jnp.float32)
        ss.append(jnp.dot(x, ws_ref[j], preferred_element_type=jnp.float32))
    p = S5_STATE
    for q in range(S5_Q):
        s_ref[q] = jnp.concatenate([ss[0][:, q * p:(q + 1) * p], ss[1][:, q * p:(q + 1) * p]], axis=1)


def _s5a(x, toe, ws):
    g, r, w = x.shape
    return pl.pallas_call(
        _s5a_kernel,
        grid=(S5_PAIRS,),
        in_specs=[pl.BlockSpec((2, r, w), lambda i: (i, 0, 0)),
                  pl.BlockSpec((2, w, w), lambda i: (i, 0, 0)),
                  pl.BlockSpec((2, w, S5_Q * S5_STATE), lambda i: (i, 0, 0))],
        out_specs=[pl.BlockSpec((2, r, w), lambda i: (i, 0, 0)),
                   pl.BlockSpec((S5_Q, r, 128), lambda i: (0, 0, i))],
        out_shape=[jax.ShapeDtypeStruct((g, r, w), jnp.float32),
                   jax.ShapeDtypeStruct((S5_Q, r, S5_LANES), jnp.float32)],
        compiler_params=_cparams(1),
        name="s5_chunk_local",
    )(x, toe, ws)


def _s5b_kernel(s_ref, a_ref, h0_ref, hp_ref, fin_ref, *, nc):
    def run(qr, qi, order_fwd):
        ar, ai = a_ref[qr], a_ref[qi]

        def body(i, carry):
            hr, hi = carry
            k = i if order_fwd else nc - 1 - i
            hp_ref[qr, pl.ds(k, 1), :] = hr
            hp_ref[qi, pl.ds(k, 1), :] = hi
            nr = ar * hr - ai * hi + s_ref[qr, pl.ds(k, 1), :]
            ni = ar * hi + ai * hr + s_ref[qi, pl.ds(k, 1), :]
            return nr, ni

        hr, hi = lax.fori_loop(0, nc, body, (h0_ref[qr], h0_ref[qi]))
        fin_ref[qr] = hr
        fin_ref[qi] = hi

    run(0, 1, True)
    run(2, 3, False)


def _s5b(s, a_l, h0):
    _, nc, w = s.shape
    tl = S5_SCAN_LANES
    vec = pl.BlockSpec((S5_Q, 1, tl), lambda i: (0, 0, i))
    seq = pl.BlockSpec((S5_Q, nc, tl), lambda i: (0, 0, i))
    return pl.pallas_call(
        functools.partial(_s5b_kernel, nc=nc),
        grid=(w // tl,),
        in_specs=[seq, vec, vec],
        out_specs=[seq, vec],
        out_shape=[jax.ShapeDtypeStruct((S5_Q, nc, w), jnp.float32),
                   jax.ShapeDtypeStruct((S5_Q, 1, w), jnp.float32)],
        compiler_params=_cparams(1),
        name="s5_chunk_scan",
    )(s, a_l, h0)


def _s5c_kernel(h_ref, wo_ref, y1_ref, y_ref):
    p = S5_STATE
    for j in range(2):
        hcat = jnp.concatenate([h_ref[q][:, j * p:(j + 1) * p] for q in range(S5_Q)], axis=1)
        y2 = jnp.dot(hcat.astype(jnp.bfloat16), wo_ref[j], preferred_element_type=jnp.float32)
        y_ref[j] = (y1_ref[j] + y2).astype(y_ref.dtype)


def _s5c(hprev, wo, y1):
    g, r, w = y1.shape
    return pl.pallas_call(
        _s5c_kernel,
        grid=(S5_PAIRS,),
        in_specs=[pl.BlockSpec((S5_Q, r, 128), lambda i: (0, 0, i)),
                  pl.BlockSpec((2, S5_Q * S5_STATE, w), lambda i: (i, 0, 0)),
                  pl.BlockSpec((2, r, w), lambda i: (i, 0, 0))],
        out_specs=pl.BlockSpec((2, r, w), lambda i: (i, 0, 0)),
        out_shape=jax.ShapeDtypeStruct((g, r, w), jnp.bfloat16),
        compiler_params=_cparams(1),
        name="s5_state_to_out",
    )(hprev, wo, y1)


def _s5_to_chunks(u):
    b, t, _ = u.shape
    nc = t // S5_L
    x = u.reshape(b, nc, S5_L, S5_GROUPS, S5_GROUP_CH).transpose(3, 1, 0, 2, 4)
    return x.reshape(S5_GROUPS, nc * b, S5_L * S5_GROUP_CH)


def _s5_from_chunks(y, b, t):
    nc = t // S5_L
    y = y.reshape(S5_GROUPS, nc, b, S5_L, S5_GROUP_CH).transpose(2, 1, 3, 0, 4)
    return y.reshape(b * t, S5_CH)


def _s5_state_planes(h0):
    b = h0.shape[0]
    return h0.transpose(1, 4, 0, 2, 3).reshape(S5_Q, b, S5_LANES)


def _s5_core(us, h0s, tables):
    toe, ws_pair, wo_pair, a_l = tables
    xs = [_s5_to_chunks(u) for u in us]
    rows = [x.shape[1] for x in xs]
    y1, s = _s5a(jnp.concatenate(xs, axis=1), toe, ws_pair)
    hps, fins = [], []
    off = 0
    for u, h0, r in zip(us, h0s, rows):
        b, t, _ = u.shape
        nc = t // S5_L
        if h0 is None:
            h0p = jnp.zeros((S5_Q, 1, b * S5_LANES), jnp.float32)
        else:
            h0p = _s5_state_planes(h0.astype(jnp.float32)).reshape(S5_Q, 1, b * S5_LANES)
        hp, fin = _s5b(s[:, off:off + r].reshape(S5_Q, nc, b * S5_LANES), jnp.tile(a_l, (1, 1, b)), h0p)
        hps.append(hp.reshape(S5_Q, r, S5_LANES))
        fins.append(fin.reshape(2, 2, b, S5_GROUPS, S5_STATE).transpose(2, 0, 3, 4, 1))
        off += r
    y = _s5c(jnp.concatenate(hps, axis=1), wo_pair, y1)
    outs, off = [], 0
    for u, r in zip(us, rows):
        b, t, _ = u.shape
        outs.append(_s5_from_chunks(y[:, off:off + r], b, t))
        off += r
    return jnp.concatenate(outs, axis=0), fins


LRU_TC = 256
LRU_HALO = 16


def _lru_kernel(x_ref, gate_ref, cw_ref, cb_ref, wg_ref, bg_ref, sp_ref, h0_ref, out_ref, fin_ref,
                a_s, b_s, hf_s, *, t_len, tc):
    f32 = jnp.float32
    nt = t_len // tc
    w = LRU_W
    nb = tc // 8
    row = lax.broadcasted_iota(jnp.int32, (nb, 8, w), 1)

    def gates(c, d):
        r0 = pl.multiple_of(c * tc, tc)
        lo = pl.multiple_of(jnp.maximum(r0 - LRU_HALO, 0), LRU_HALO)
        hi = pl.multiple_of(jnp.minimum(r0 + tc, t_len - LRU_HALO), LRU_HALO)
        prev = jnp.where(c > 0, x_ref[pl.ds(lo, LRU_HALO), :].astype(f32), 0.0)
        nxt = jnp.where(c < nt - 1, x_ref[pl.ds(hi, LRU_HALO), :].astype(f32), 0.0)
        slab = jnp.concatenate([prev, x_ref[pl.ds(r0, tc), :].astype(f32), nxt], axis=0)
        o = LRU_HALO - LRU_CONV // 2
        xc = cb_ref[...] + sum(cw_ref[k:k + 1, :] * slab[o + k:o + k + tc] for k in range(LRU_CONV))
        g = jnp.dot(xc.astype(jnp.bfloat16), wg_ref[d], preferred_element_type=f32) + bg_ref[d]
        r = _sigmoid(g[:, :w])
        i = _sigmoid(g[:, w:])
        log_a = -sp_ref[d] * r
        a = jnp.exp(log_a)
        b = (jnp.sqrt(1.0 - a * a) * (i * xc)).reshape(nb, 8, w)
        a = a.reshape(nb, 8, w)
        for sh in (1, 2, 4):
            keep, rot = (row >= sh, sh) if d == 0 else (row < 8 - sh, 8 - sh)
            ap = jnp.where(keep, pltpu.roll(a, rot, 1), 1.0)
            bp = jnp.where(keep, pltpu.roll(b, rot, 1), 0.0)
            b = a * bp + b
            a = a * ap
        a_s[...] = a
        b_s[...] = b
        return r0

    def fwd_chunk(c, h):
        r0 = gates(c, 0)

        def block(k, h):
            h8 = a_s[k] * h + b_s[k]
            hf_s[pl.ds(pl.multiple_of(r0 + k * 8, 8), 8), :] = h8
            return h8[7:8, :]

        return lax.fori_loop(0, nb, block, h, unroll=4)

    h = lax.fori_loop(0, nt, fwd_chunk, h0_ref[0, 0:1, :])
    fin_ref[0, 0:1, :] = h

    def bwd_chunk(ci, h):
        r0 = gates(nt - 1 - ci, 1)

        def block(i, h):
            k = nb - 1 - i
            h8 = a_s[k] * h + b_s[k]
            rows = pl.ds(pl.multiple_of(r0 + k * 8, 8), 8)
            hf_s[rows, :] = hf_s[rows, :] + h8
            return h8[0:1, :]

        h = lax.fori_loop(0, nb, block, h, unroll=4)
        sl = pl.ds(r0, tc)
        out_ref[sl, :] = (hf_s[sl, :] * _gelu_tanh(gate_ref[sl, :].astype(f32))).astype(out_ref.dtype)
        return h

    h = lax.fori_loop(0, nt, bwd_chunk, h0_ref[0, 1:2, :])
    fin_ref[0, 1:2, :] = h


def _block_diag(wb):
    n, k, j = wb.shape
    return (wb[:, :, None, :] * jnp.eye(n, dtype=wb.dtype)[:, None, :, None]).reshape(n * k, n * j)


def _lru_tables(conv_w, conv_b, w_a, b_a, w_x, b_x, lam):
    wg = jnp.stack([jnp.concatenate([_block_diag(w_a[d]), _block_diag(w_x[d])], axis=1) for d in range(2)])
    bg = jnp.concatenate([b_a, b_x], axis=-1)[:, None, :]
    sp = (LRU_C * jax.nn.softplus(-lam.astype(jnp.float32)))[:, None, :]
    return conv_w, conv_b[None, :], wg.astype(jnp.bfloat16), bg, sp


def _lru(p, row0, b, t, h0, tables):
    w = LRU_W
    tc = min(LRU_TC, t)
    cw, cb, wg, bg, sp = tables
    full = lambda a: pl.BlockSpec(a.shape, lambda i: (0,) * a.ndim)
    rb = row0 // t
    return pl.pallas_call(
        functools.partial(_lru_kernel, t_len=t, tc=tc),
        grid=(b,),
        in_specs=[pl.BlockSpec((t, w), lambda i: (rb + i, C_LX // w)),
                  pl.BlockSpec((t, w), lambda i: (rb + i, C_LG // w)),
                  full(cw), full(cb), full(wg), full(bg), full(sp),
                  pl.BlockSpec((1, 2, w), lambda i: (i, 0, 0))],
        out_specs=[pl.BlockSpec((t, w), lambda i: (i, 0)),
                   pl.BlockSpec((1, 2, w), lambda i: (i, 0, 0))],
        out_shape=[jax.ShapeDtypeStruct((b * t, w), jnp.bfloat16),
                   jax.ShapeDtypeStruct((b, 2, w), jnp.float32)],
        scratch_shapes=[pltpu.VMEM((tc // 8, 8, w), jnp.float32), pltpu.VMEM((tc // 8, 8, w), jnp.float32),
                        pltpu.VMEM((t, w), jnp.float32)],
        compiler_params=_cparams(1),
        name="rglru",
    )(p, p, cw, cb, wg, bg, sp, h0)


MLA_TM = 512
ATT_DQ = 256


def _mla_prep_kernel(q_ref, kv_ref, kr_ref, tab_ref, gq_ref, gkv_ref, wuq_ref, wukv_ref,
                     qo_ref, ko_ref, vo_ref, ckv_ref):
    f32, bf = jnp.float32, jnp.bfloat16
    tab = tab_ref[...]

    def rope(blk):
        prod = blk * tab
        return prod + pltpu.roll(prod, 64, 1)

    qn = (_rms(q_ref[...].astype(f32)) * gq_ref[...]).astype(bf)
    qq = jnp.dot(qn, wuq_ref[...], preferred_element_type=f32)
    ckv = _rms(kv_ref[...].astype(f32)) * gkv_ref[...]
    ckv_ref[...] = ckv
    kk = jnp.dot(ckv.astype(bf), wukv_ref[...], preferred_element_type=f32)
    lane = lax.broadcasted_iota(jnp.int32, tab.shape, 1)
    kpe = jnp.where(lane < MLA_ROPE, rope(kr_ref[...].astype(f32)), 0.0).astype(bf)
    for h in range(MLA_HEADS):
        o = h * ATT_DQ
        qo_ref[:, o:o + 128] = (qq[:, o:o + 128] * MLA_SCALE).astype(bf)
        qo_ref[:, o + 128:o + 256] = (rope(qq[:, o + 128:o + 256]) * MLA_SCALE).astype(bf)
        ko_ref[:, o:o + 128] = kk[:, o:o + 128].astype(bf)
        ko_ref[:, o + 128:o + 256] = kpe
        vo_ref[:, h * MLA_V:(h + 1) * MLA_V] = kk[:, o + 128:o + 256].astype(bf)


def _mla_weights(w_uq, w_ukv):
    wq = w_uq.reshape(MLA_Q_RANK, MLA_HEADS, MLA_NOPE + MLA_ROPE)
    pe = wq[:, :, MLA_NOPE:].reshape(MLA_Q_RANK, MLA_HEADS * MLA_ROPE)
    pep = _rot_partner_cols(pe, MLA_HEADS, MLA_ROPE).reshape(MLA_Q_RANK, MLA_HEADS, MLA_ROPE)
    wq_ext = jnp.concatenate([wq, pep], axis=-1).reshape(MLA_Q_RANK, MLA_HEADS * ATT_DQ)
    return wq_ext.astype(jnp.bfloat16), w_ukv.astype(jnp.bfloat16)


def _mla_prep(p, tab, gq, gkv, wuq, wukv):
    n = p.shape[0]
    tm = MLA_TM
    full = lambda a: pl.BlockSpec(a.shape, lambda i: (0,) * a.ndim)
    row = lambda width, col: pl.BlockSpec((tm, width), lambda i: (i, col // width))
    return pl.pallas_call(
        _mla_prep_kernel,
        grid=(n // tm,),
        in_specs=[row(MLA_Q_RANK, C_MQ), row(MLA_KV_RANK, C_MKV), row(128, C_KR), row(128, 0),
                  full(gq), full(gkv), full(wuq), full(wukv)],
        out_specs=[row(MLA_HEADS * ATT_DQ, 0), row(MLA_HEADS * ATT_DQ, 0), row(MLA_HEADS * MLA_V, 0),
                   row(MLA_KV_RANK, 0)],
        out_shape=[jax.ShapeDtypeStruct((n, MLA_HEADS * ATT_DQ), jnp.bfloat16),
                   jax.ShapeDtypeStruct((n, MLA_HEADS * ATT_DQ), jnp.bfloat16),
                   jax.ShapeDtypeStruct((n, MLA_HEADS * MLA_V), jnp.bfloat16),
                   jax.ShapeDtypeStruct((n, MLA_KV_RANK), jnp.float32)],
        compiler_params=_cparams(1),
        name="mla_prep",
    )(p, p, p, tab, gq, gkv, wuq, wukv)


ATT_TQ = 512
ATT_SUB = 256


def _attn_kernel(*refs, two, n_sub):
    dn = (((1,), (1,)), ((), ()))
    f32 = jnp.float32
    if two:
        q_ref, k_ref, v_ref, k2_ref, v2_ref, o_ref = refs
    else:
        q_ref, k_ref, v_ref, o_ref = refs
    rows = q_ref.shape[0] // n_sub
    for part in range(n_sub):
        sl = slice(part * rows, (part + 1) * rows)
        q = q_ref[sl, :]
        s = lax.dot_general(q, k_ref[...], dn, preferred_element_type=f32)
        m = jnp.max(s, axis=-1, keepdims=True)
        if two:
            s2 = lax.dot_general(q, k2_ref[...], dn, preferred_element_type=f32)
            m = jnp.maximum(m, jnp.max(s2, axis=-1, keepdims=True))
        p = jnp.exp(s - m)
        l = jnp.sum(p, axis=-1, keepdims=True)
        o = jnp.dot(p.astype(jnp.bfloat16), v_ref[...], preferred_element_type=f32)
        if two:
            p2 = jnp.exp(s2 - m)
            l = l + jnp.sum(p2, axis=-1, keepdims=True)
            o = o + jnp.dot(p2.astype(jnp.bfloat16), v2_ref[...], preferred_element_type=f32)
        o_ref[sl, :] = (o / l).astype(o_ref.dtype)


def _attention(q, k, v, row0, b, t, k2=None, v2=None):
    tb = min(ATT_TQ, t)
    nq = t // tb
    qb, kb = row0 // tb, row0 // t
    two = k2 is not None
    in_specs = [pl.BlockSpec((tb, ATT_DQ), lambda bi, h, i: (qb + bi * nq + i, h)),
                pl.BlockSpec((t, ATT_DQ), lambda bi, h, i: (kb + bi, h)),
                pl.BlockSpec((t, MLA_V), lambda bi, h, i: (kb + bi, h))]
    args = [q, k, v]
    if two:
        t2 = k2.shape[0] // b
        in_specs += [pl.BlockSpec((t2, ATT_DQ), lambda bi, h, i: (bi, h)),
                     pl.BlockSpec((t2, MLA_V), lambda bi, h, i: (bi, h))]
        args += [k2, v2]
    return pl.pallas_call(
        functools.partial(_attn_kernel, two=two, n_sub=max(tb // ATT_SUB, 1)),
        grid=(b, MLA_HEADS, nq),
        in_specs=in_specs,
        out_specs=pl.BlockSpec((tb, MLA_V), lambda bi, h, i: (bi * nq + i, h)),
        out_shape=jax.ShapeDtypeStruct((b * t, MLA_HEADS * MLA_V), jnp.bfloat16),
        compiler_params=_cparams(3),
        name="mla_attention",
    )(*args)


def _ret_tables():
    f32 = jnp.float32
    c = RET_CHUNK
    log_g = jnp.log1p(-jnp.exp2(-5.0 - jnp.arange(RET_HEADS, dtype=f32)))[:, None, None]
    idx = jnp.arange(c, dtype=f32)
    dec = jnp.exp(jnp.abs(idx[:, None] - idx[None, :])[None] * log_g)
    row = lambda e: jnp.broadcast_to(jnp.exp(e[None, :, None] * log_g), (RET_HEADS, c, c))
    return jnp.stack([dec, row(idx + 1.0), row(c - idx), row(c - 1.0 - idx), row(idx)], axis=0)


RET_HPS = 2


def _ret_kernel(*refs, nc, rope):
    f32, bf = jnp.float32, jnp.bfloat16
    if rope:
        (q_ref, k_ref, v_ref, g_ref, qp_ref, kp_ref, cos_ref, sin_ref, tab_ref, s0_ref, o_ref, fin_ref,
         acc_s, st_s) = refs
    else:
        q_ref, k_ref, v_ref, g_ref, tab_ref, s0_ref, o_ref, fin_ref, acc_s, st_s = refs
    c, dk = RET_CHUNK, RET_DK
    nt_dims, tn_dims = (((1,), (1,)), ((), ())), (((0,), (0,)), ((), ()))

    def chunk(k0, hh):
        sl = pl.ds(pl.multiple_of(k0 * c, c), c)
        cols = slice(hh * dk, (hh + 1) * dk)
        qc, kc = q_ref[sl, cols].astype(f32), k_ref[sl, cols].astype(f32)
        if rope:
            cos, sin = cos_ref[sl, :], sin_ref[sl, :]
            qc = qc * cos + qp_ref[sl, cols].astype(f32) * sin
            kc = kc * cos + kp_ref[sl, cols].astype(f32) * sin
        return sl, cols, qc.astype(bf), kc * (RET_DK ** -0.5), v_ref[sl, cols]

    st_s[...] = s0_ref[0]
    acc_s[...] = jnp.zeros_like(acc_s)

    def body(i, carry):
        for hh in range(RET_HPS):
            dec, xif, xib, zf, zb = (tab_ref[t, hh] for t in range(5))
            g_chunk = xif[c - 1:c, :]
            sl, cols, qc, kc, vc = chunk(i, hh)
            s = st_s[0, hh]
            sc = lax.dot_general(qc, kc.astype(bf), nt_dims, preferred_element_type=f32) * dec
            acc_s[sl, cols] += (jnp.dot(sc.astype(bf), vc, preferred_element_type=f32)
                                + jnp.dot(qc, s.astype(bf), preferred_element_type=f32) * xif)
            st_s[0, hh] = g_chunk * s + lax.dot_general((kc * zf).astype(bf), vc, tn_dims,
                                                        preferred_element_type=f32)
            sl, cols, qc, kc, vc = chunk(nc - 1 - i, hh)
            s = st_s[1, hh]
            acc_s[sl, cols] += jnp.dot(qc, s.astype(bf), preferred_element_type=f32) * xib
            st_s[1, hh] = g_chunk * s + lax.dot_general((kc * zb).astype(bf), vc, tn_dims,
                                                        preferred_element_type=f32)
        return carry

    lax.fori_loop(0, nc, body, 0)
    fin_ref[0] = st_s[...]

    def finish(i, carry):
        sl = pl.ds(pl.multiple_of(i * c, c), c)
        for hh in range(RET_HPS):
            cols = slice(hh * dk, (hh + 1) * dk)
            g = g_ref[sl, cols].astype(f32)
            o_ref[sl, cols] = (_rms(acc_s[sl, cols]) * (g * _sigmoid(g))).astype(o_ref.dtype)
        return carry

    lax.fori_loop(0, nc, finish, 0)


def _retention(p, row0, b, t, s0, tables, rope_tabs=None):
    nc = t // RET_CHUNK
    rb = row0 // t
    rope = rope_tabs is not None
    wd = RET_HPS * RET_DK
    col = lambda c0: pl.BlockSpec((t, wd), lambda bi, h: (rb + bi, c0 // wd + h))
    st = pl.BlockSpec((1, 2, RET_HPS, RET_DK, RET_DV), lambda bi, h: (bi, 0, h, 0, 0))
    in_specs = [col(C_RQ), col(C_RK), col(C_RV), col(C_RG)]
    args = [p, p, p, p]
    if rope:
        tab = pl.BlockSpec((t, RET_DK), lambda bi, h: (0, 0))
        in_specs += [col(C_RQP), col(C_RKP), tab, tab]
        args += [p, p, rope_tabs[0], rope_tabs[1]]
    in_specs += [pl.BlockSpec((5, RET_HPS, RET_CHUNK, RET_CHUNK), lambda bi, h: (0, h, 0, 0)), st]
    args += [tables, s0]
    return pl.pallas_call(
        functools.partial(_ret_kernel, nc=nc, rope=rope),
        grid=(b, RET_HEADS // RET_HPS),
        in_specs=in_specs,
        out_specs=[pl.BlockSpec((t, wd), lambda bi, h: (bi, h)), st],
        out_shape=[jax.ShapeDtypeStruct((b * t, RET_HEADS * RET_DV), jnp.bfloat16),
                   jax.ShapeDtypeStruct((b, 2, RET_HEADS, RET_DK, RET_DV), jnp.float32)],
        scratch_shapes=[pltpu.VMEM((t, wd), jnp.float32), pltpu.VMEM((2, RET_HPS, RET_DK, RET_DV), jnp.float32)],
        compiler_params=_cparams(2, V7X_VMEM_LIMIT_BIG),
        name="retention",
    )(*args)


OUT_TM = 512
ROUTER_PAD = 128


ROW_TILE = (8, D_MODEL // 16)


def _pack_halves(x):
    w = x.shape[1] // 2
    bits = lambda a: lax.bitcast_convert_type(a.astype(jnp.bfloat16).astype(jnp.float32), jnp.int32)
    return bits(x[:, w:]) | lax.shift_right_logical(bits(x[:, :w]), 16)


def _unpack_halves(u):
    lo = lax.bitcast_convert_type(lax.shift_left(u, 16), jnp.float32)
    hi = lax.bitcast_convert_type(u & jnp.int32(-65536), jnp.float32)
    return jnp.concatenate([lo, hi], axis=1)


def _split_bf16(x):
    hi = x.astype(jnp.bfloat16)
    return hi, (x - hi.astype(jnp.float32)).astype(jnp.bfloat16)


def _route(h, whi_ref, wlo_ref, rb_ref):
    f32 = jnp.float32
    hi, lo = _split_bf16(h)
    dn = (((1,), (1,)), ((), ()))
    lt = (lax.dot_general(whi_ref[...], hi, dn, preferred_element_type=f32)
          + lax.dot_general(whi_ref[...], lo, dn, preferred_element_type=f32)
          + lax.dot_general(wlo_ref[...], hi, dn, preferred_element_type=f32))[:N_EXPERTS]
    m = jnp.max(lt, axis=0, keepdims=True)
    e = jnp.exp(lt - m)
    sc = e / jnp.sum(e, axis=0, keepdims=True)
    sel = sc + rb_ref[...][:N_EXPERTS, 0:1]
    rows = lambda a: [a[j:j + 1, :] for j in range(N_EXPERTS)]
    sel_r, sc_r = rows(sel), rows(sc)
    epg = EXPERTS_PER_GROUP

    def top2sum(a, b, c, d):
        h1, l1, h2, l2 = jnp.maximum(a, b), jnp.minimum(a, b), jnp.maximum(c, d), jnp.minimum(c, d)
        return jnp.maximum(h1, h2) + jnp.maximum(jnp.minimum(h1, h2), jnp.maximum(l1, l2))

    gs = [top2sum(*sel_r[g * epg:(g + 1) * epg]) for g in range(N_EXPERT_GROUPS)]
    best, gi = gs[0], jnp.zeros_like(gs[0], dtype=jnp.int32)
    for g in range(1, N_EXPERT_GROUPS):
        upd = gs[g] > best
        gi = jnp.where(upd, g, gi)
        best = jnp.where(upd, gs[g], best)

    def pick(r, j):
        out = r[j]
        for g in range(1, N_EXPERT_GROUPS):
            out = jnp.where(gi == g, r[g * epg + j], out)
        return out

    v = [pick(sel_r, j) for j in range(epg)]
    s = [pick(sc_r, j) for j in range(epg)]

    def argmax_first(vals):
        bv, bi = vals[0], jnp.zeros_like(gi)
        for j in range(1, epg):
            upd = vals[j] > bv
            bi = jnp.where(upd, j, bi)
            bv = jnp.where(upd, vals[j], bv)
        return bi

    i1 = argmax_first(v)
    neg = jnp.float32(-jnp.inf)
    i2 = argmax_first([jnp.where(i1 == j, neg, v[j]) for j in range(epg)])
    take = lambda i: sum(jnp.where(i == j, s[j], 0.0) for j in range(epg))
    w1, w2 = take(i1), take(i2)
    tot = w1 + w2
    return (jnp.concatenate([gi * epg + i1, gi * epg + i2], axis=0),
            jnp.concatenate([w1 / tot, w2 / tot], axis=0))


def _mixout_kernel(ylin_ref, u_ref, ymla_c, ymla_l, ylru_c, ylru_l, yret_c, yret_l, x_c, x_l, mod_ref, d_ref,
                   wglu_ref, wout_ref, g2n_ref, whi_ref, wlo_ref, rb_ref, xo_ref, h2_ref, idx_ref, wgt_ref,
                   *, n_ctx_tiles):
    f32, bf = jnp.float32, jnp.bfloat16
    w = GROUP_W
    is_ctx = pl.program_id(0) < n_ctx_tiles
    pick = lambda c_ref, l_ref: jnp.where(is_ctx, c_ref[...], l_ref[...])
    y = _gelu_tanh(ylin_ref[...].astype(f32) + d_ref[...] * u_ref[...].astype(f32))
    y5 = (y * _sigmoid(jnp.dot(y.astype(bf), wglu_ref[...], preferred_element_type=f32))).astype(bf)
    mix = (jnp.dot(y5, wout_ref[0:w, :], preferred_element_type=f32)
           + jnp.dot(pick(ymla_c, ymla_l), wout_ref[w:2 * w, :], preferred_element_type=f32)
           + jnp.dot(pick(ylru_c, ylru_l), wout_ref[2 * w:3 * w, :], preferred_element_type=f32)
           + jnp.dot(pick(yret_c, yret_l), wout_ref[3 * w:4 * w, :], preferred_element_type=f32))
    x = pick(x_c, x_l) + mod_ref[0, 2:3, :] * mix
    xo_ref[...] = x
    h2 = _rms(x) * g2n_ref[...] * (1.0 + mod_ref[0, 4:5, :]) + mod_ref[0, 3:4, :]
    h2_ref[...] = _pack_halves(h2).reshape(h2_ref.shape)
    idx, wgt = _route(h2, whi_ref, wlo_ref, rb_ref)
    idx_ref[...] = idx
    wgt_ref[...] = wgt


def _router_weights(router_w, router_b):
    d = router_w.shape[0]
    wt = jnp.zeros((ROUTER_PAD, d), jnp.float32).at[:N_EXPERTS].set(router_w.T.astype(jnp.float32))
    whi, wlo = _split_bf16(wt)
    rb = jnp.zeros((ROUTER_PAD, 128), jnp.float32).at[:N_EXPERTS].set(router_b.astype(jnp.float32)[:, None])
    return whi, wlo, rb


def _mixout(ylin, p, ymla, ylru, yret, x, mods, s5_d, wglu, wout, g2n, rw, n_ctx, dec_seq):
    n, d = x[0].shape[0] + x[1].shape[0], x[0].shape[1]
    tm = OUT_TM
    w = GROUP_W
    na = n_ctx // tm
    seq = functools.partial(_seq_of_tile, tm=tm, n_ctx=n_ctx, dec_seq=dec_seq)
    full = lambda a: pl.BlockSpec(a.shape, lambda i: (0,) * a.ndim)
    row = lambda width: pl.BlockSpec((tm, width), lambda i: (i, 0))
    ctx_row, lat_row = _two_part_rows(tm, w, na)
    lanes = pl.BlockSpec((TOP_K, tm), lambda i: (0, i))
    whi, wlo, rb = rw
    return pl.pallas_call(
        functools.partial(_mixout_kernel, n_ctx_tiles=na),
        grid=(n // tm,),
        in_specs=[row(w), row(w), ctx_row, lat_row, ctx_row, lat_row, ctx_row, lat_row, *_two_part_rows(tm, d, na),
                  pl.BlockSpec((1, N_MOD, d), lambda i: (seq(i), 0, 0)),
                  full(s5_d), full(wglu), full(wout), full(g2n), full(whi), full(wlo), full(rb)],
        out_specs=[row(d), pl.BlockSpec((tm,) + ROW_TILE, lambda i: (i, 0, 0)), lanes, lanes],
        out_shape=[jax.ShapeDtypeStruct((n, d), jnp.float32), jax.ShapeDtypeStruct((n,) + ROW_TILE, jnp.int32),
                   jax.ShapeDtypeStruct((TOP_K, n), jnp.int32), jax.ShapeDtypeStruct((TOP_K, n), jnp.float32)],
        compiler_params=_cparams(1, V7X_VMEM_LIMIT_BIG),
        name="mix_out_norm_route",
    )(ylin, p, *ymla, *ylru, *yret, *x, mods, s5_d, wglu, wout, g2n, whi, wlo, rb)


MOE_TM = 256
MOE_NB = 256


def _expert_kernel(te_ref, nt_ref, nv_ref, tok0_ref, tok1_ref, w_ref, dstp_ref, dst_ref, wg_ref, wu_ref, wd_ref,
                   h_hbm, y_hbm, wg_s, wu_s, wd_s, xbuf, xs_s, act_s, ybuf, gsem, ssem):
    i = pl.program_id(0)
    bf = jnp.bfloat16
    tm = xs_s.shape[0]
    nt = nt_ref[0]

    def gather_start(tok_ref, sl, unroll):
        def row(r, carry):
            pltpu.make_async_copy(h_hbm.at[tok_ref[0, 0, r]], xbuf.at[sl, r], gsem.at[sl]).start()
            return carry

        lax.fori_loop(0, tm, row, 0, unroll=unroll)

    def gather_done(sl):
        pltpu.make_async_copy(h_hbm.at[pl.ds(0, tm)], xbuf.at[sl], gsem.at[sl]).wait()

    def scatter_row(d_ref, sl, r):
        return pltpu.make_async_copy(ybuf.at[sl, r], y_hbm.at[d_ref[0, 0, r]], ssem.at[sl])

    def scatter_done(sl, n_rows):
        pltpu.make_async_copy(ybuf.at[sl, pl.ds(0, n_rows)], y_hbm.at[pl.ds(0, n_rows)], ssem.at[sl]).wait()

    def tile(slot):
        n_prev = jnp.where(i > 0, nv_ref[jnp.maximum(i - 1, 0)], 0)
        f = wg_s.shape[1]
        d = wd_s.shape[1]
        up_pieces, down_pieces = 2 * (f // MOE_NB), d // MOE_NB

        def gather_rows(piece):
            for r in range(piece * tm // up_pieces, (piece + 1) * tm // up_pieces):
                pltpu.make_async_copy(h_hbm.at[tok1_ref[0, 0, r]], xbuf.at[1 - slot, r],
                                      gsem.at[1 - slot]).start(priority=r % 2)

        def scatter_rows(piece):
            for r in range(piece * tm // down_pieces, (piece + 1) * tm // down_pieces):
                @pl.when(r < n_prev)
                def _():
                    scatter_row(dstp_ref, 1 - slot, r).start(priority=r % 2)

        gather_done(slot)
        xs_s[...] = _unpack_halves(xbuf[slot].reshape(tm, -1)).astype(bf)
        x = xs_s[...]
        gs, us = [], []
        for c in range(f // MOE_NB):
            cols = slice(c * MOE_NB, (c + 1) * MOE_NB)
            gs.append(jnp.dot(x, wg_s[:, cols], preferred_element_type=jnp.float32))
            gather_rows(2 * c)
            us.append(jnp.dot(x, wu_s[:, cols], preferred_element_type=jnp.float32))
            gather_rows(2 * c + 1)
        g = jnp.concatenate(gs, axis=1)
        u = jnp.concatenate(us, axis=1)
        act_s[...] = ((g * _sigmoid(g)) * u * w_ref[...]).astype(bf)

        @pl.when(i > 1)
        def _():
            scatter_done(slot, nv_ref[jnp.maximum(i - 2, 0)])

        act = act_s[...]
        ys = []
        for c in range(down_pieces):
            ys.append(jnp.dot(act, wd_s[:, c * MOE_NB:(c + 1) * MOE_NB], preferred_element_type=jnp.float32))
            scatter_rows(c)
        ybuf[slot] = _pack_halves(jnp.concatenate(ys, axis=1)).reshape(ybuf.shape[1:])

        @pl.when(i == nt - 1)
        def _():
            def row(r, carry):
                scatter_row(dst_ref, slot, r).start()
                return carry

            lax.fori_loop(0, nv_ref[i], row, 0)

            @pl.when(i > 0)
            def _():
                scatter_done(1 - slot, nv_ref[jnp.maximum(i - 1, 0)])

            scatter_done(slot, nv_ref[i])
            gather_done(1 - slot)

    @pl.when(i < nt)
    def _():
        @pl.when(i == 0)
        def _():
            gather_start(tok0_ref, 0, 8)

        @pl.when((i == 0) | (te_ref[i] != te_ref[jnp.maximum(i - 1, 0)]))
        def _():
            wg_s[...] = wg_ref[0, 0].astype(bf)
            wu_s[...] = wu_ref[0, 0].astype(bf)
            wd_s[...] = wd_ref[0, 0].astype(bf)

        @pl.when(i % 2 == 0)
        def _():
            tile(0)

        @pl.when(i % 2 == 1)
        def _():
            tile(1)


def _experts(h2p, layer, ws, tok, dst, tile_expert, n_tiles_used, n_valid, wg, wu, wd):
    n = h2p.shape[0]
    d = 2 * ROW_TILE[0] * ROW_TILE[1]
    tm = MOE_TM
    n_tiles = tok.shape[0]
    f = wg.shape[-1]
    smem_row = lambda delta: pl.BlockSpec(
        (1, 1, tm), lambda i, te, nt, nv: (jnp.clip(i + delta, 0, n_tiles - 1), 0, 0), memory_space=pltpu.SMEM)
    wspec = lambda a, b: pl.BlockSpec((1, 1, a, b), lambda i, te, nt, nv: (layer, te[i], 0, 0))
    return pl.pallas_call(
        _expert_kernel,
        grid_spec=pltpu.PrefetchScalarGridSpec(
            num_scalar_prefetch=3,
            grid=(n_tiles,),
            in_specs=[smem_row(0), smem_row(1),
                      pl.BlockSpec((tm, 1), lambda i, te, nt, nv: (i, 0)),
                      smem_row(-1), smem_row(0),
                      wspec(d, f), wspec(d, f), wspec(f, d),
                      pl.BlockSpec(memory_space=pl.ANY)],
            out_specs=pl.BlockSpec(memory_space=pl.ANY),
            scratch_shapes=[pltpu.VMEM((d, f), jnp.bfloat16), pltpu.VMEM((d, f), jnp.bfloat16),
                            pltpu.VMEM((f, d), jnp.bfloat16), pltpu.VMEM((2, tm) + ROW_TILE, jnp.int32),
                            pltpu.VMEM((tm, d), jnp.bfloat16), pltpu.VMEM((tm, f), jnp.bfloat16),
                            pltpu.VMEM((2, tm) + ROW_TILE, jnp.int32),
                            pltpu.SemaphoreType.DMA((2,)), pltpu.SemaphoreType.DMA((2,))]),
        out_shape=jax.ShapeDtypeStruct((TOP_K * n,) + ROW_TILE, jnp.int32),
        compiler_params=pltpu.CompilerParams(dimension_semantics=("arbitrary",),
                                             vmem_limit_bytes=V7X_VMEM_LIMIT_BIG, disable_bounds_checks=True),
        name="moe_experts",
    )(tile_expert, n_tiles_used, n_valid, tok, tok, ws, dst, dst, wg, wu, wd, h2p)


def _moe_dispatch(idx, wgt):
    n = idx.shape[1]
    tm = MOE_TM
    n_pairs = TOP_K * n
    m_pad = n_pairs + N_EXPERTS * tm
    n_tiles = m_pad // tm
    e_flat = idx.reshape(-1)
    order = jnp.argsort(e_flat, stable=True).astype(jnp.int32)
    experts = jnp.arange(N_EXPERTS, dtype=jnp.int32)
    counts = jnp.sum((e_flat[None, :] == experts[:, None]).astype(jnp.int32), axis=1)
    starts_raw = jnp.cumsum(counts) - counts
    padded = ((counts + tm - 1) // tm) * tm
    ends = jnp.cumsum(padded)
    starts_pad = ends - padded
    tile_start = jnp.arange(n_tiles, dtype=jnp.int32) * tm
    tile_expert = jnp.minimum(jnp.sum((tile_start[:, None] >= ends[None, :]).astype(jnp.int32), axis=1),
                              N_EXPERTS - 1)
    r = jnp.arange(m_pad, dtype=jnp.int32)
    te_r = jnp.repeat(tile_expert, tm)
    off = r - jnp.take(starts_pad, te_r)
    valid = (off >= 0) & (off < jnp.take(counts, te_r))
    pair = jnp.take(order, jnp.clip(jnp.take(starts_raw, te_r) + off, 0, n_pairs - 1))
    tok = jnp.where(valid, pair % n, 0).reshape(n_tiles, 1, tm)
    dst = jnp.where(valid, pair, 0).reshape(n_tiles, 1, tm)
    ws = jnp.where(valid, jnp.take(wgt.reshape(-1), pair), 0.0)[:, None]
    n_tiles_used = (ends[-1] // tm).astype(jnp.int32).reshape(1)
    n_valid = jnp.sum(valid.reshape(n_tiles, tm), axis=1).astype(jnp.int32)
    return tok, dst, ws, tile_expert, n_tiles_used, n_valid


def _moe_rows(h2p, layer, idx, wgt, wg, wu, wd):
    tok, dst, ws, tile_expert, n_tiles_used, n_valid = _moe_dispatch(idx, wgt)
    return _experts(h2p, layer, ws, tok, dst, tile_expert, n_tiles_used, n_valid, wg, wu, wd)


RES_TM = 512


def _resid_kernel(x_ref, ya_ref, yb_ref, mod_ref, gf_ref, oc_ref, ol_ref, *, final, n_ctx_tiles):
    rows = lambda r: _unpack_halves(r[...].reshape(r.shape[0], -1))
    x = x_ref[...] + mod_ref[0, 5:6, :] * (rows(ya_ref) + rows(yb_ref))
    y = _rms(x) * gf_ref[...] if final else x
    i = pl.program_id(0)

    @pl.when(i < n_ctx_tiles)
    def _():
        oc_ref[...] = y

    @pl.when(i >= n_ctx_tiles)
    def _():
        ol_ref[...] = y


def _resid(x, y2, mods, gf, n_ctx, dec_seq, final):
    n, d = x.shape
    tm = RES_TM
    na = n_ctx // tm
    seq = functools.partial(_seq_of_tile, tm=tm, n_ctx=n_ctx, dec_seq=dec_seq)
    row = pl.BlockSpec((tm, d), lambda i: (i, 0))
    return pl.pallas_call(
        functools.partial(_resid_kernel, final=final, n_ctx_tiles=na),
        grid=(n // tm,),
        in_specs=[row, pl.BlockSpec((tm,) + ROW_TILE, lambda i: (i, 0, 0)),
                  pl.BlockSpec((tm,) + ROW_TILE, lambda i: (n // tm + i, 0, 0)),
                  pl.BlockSpec((1, N_MOD, d), lambda i: (seq(i), 0, 0)),
                  pl.BlockSpec((1, d), lambda i: (0, 0))],
        out_specs=list(_two_part_rows(tm, d, na)),
        out_shape=[jax.ShapeDtypeStruct((n_ctx, d), jnp.float32), jax.ShapeDtypeStruct((n - n_ctx, d), jnp.float32)],
        compiler_params=_cparams(1),
        name="moe_residual_norm",
    )(x, y2, y2, mods, gf)


def kernel(x_prompt, x_sample, c, cache_mla_ckv, cache_mla_kpe, state_s5, state_lru, state_ret,
           c_ctx, w_ada, b_ada, norm1_g, norm2_g, w_in, w_out,
           s5_a_re, s5_a_im, s5_log_dt, s5_b_re, s5_b_im, s5_c_re, s5_c_im, s5_d, s5_w_glu,
           mla_q_norm_g, mla_w_uq, mla_kv_norm_g, mla_w_ukv,
           lru_conv_w, lru_conv_b, lru_w_a, lru_b_a, lru_w_x, lru_b_x, lru_lambda,
           router_w, router_b, moe_w_gate, moe_w_up, moe_w_down, final_norm_g):
    f32, bf = jnp.float32, jnp.bfloat16
    batch, seq_len, d = x_prompt.shape
    dec_batch, dec_seq, _ = x_sample.shape
    past = cache_mla_ckv.shape[2]
    n_ctx, n_lat = batch * seq_len, dec_batch * dec_seq
    depth = w_in.shape[0]

    x = (x_prompt.reshape(n_ctx, d), x_sample.reshape(n_lat, d))
    cvec = jnp.zeros((8, d), f32).at[0].set(c_ctx).at[1:1 + dec_batch].set(c)
    mods_all = _ada(cvec, w_ada, b_ada).reshape(depth, 8, N_MOD, d)

    cos64, sin64 = _rope_tables(dec_seq, MLA_ROPE)
    lat_tab = jnp.tile(jnp.concatenate([cos64, sin64], axis=1), (dec_batch, 1))
    ctx_tab = jnp.concatenate([jnp.ones((n_ctx, MLA_ROPE), f32), jnp.zeros((n_ctx, MLA_ROPE), f32)], axis=1)
    mla_tab = jnp.concatenate([ctx_tab, lat_tab], axis=0)
    ret_rope = _rope_tables(dec_seq, RET_DK)
    ret_tables = _ret_tables()
    rw = _router_weights(router_w, router_b)
    gf = final_norm_g[None, :]

    states = []
    for l in range(depth):
        mods = mods_all[l]
        s5_tab = _s5_tables(s5_a_re[l], s5_a_im[l], s5_log_dt[l], s5_b_re[l], s5_b_im[l], s5_c_re[l], s5_c_im[l])
        lru_tab = _lru_tables(lru_conv_w[l], lru_conv_b[l], lru_w_a[l], lru_b_a[l], lru_w_x[l], lru_b_x[l],
                              lru_lambda[l])
        wuq, wukv = _mla_weights(mla_w_uq[l], mla_w_ukv[l])

        p = _inproj(x, mods, norm1_g[l][None, :], _inproj_weights(w_in[l]), n_ctx, dec_seq)

        u_ctx = p[:n_ctx, C_S5:C_S5 + S5_CH].reshape(batch, seq_len, S5_CH)
        u_lat = p[n_ctx:, C_S5:C_S5 + S5_CH].reshape(dec_batch, dec_seq, S5_CH)
        ylin, s5_fins = _s5_core([u_ctx, u_lat], [None, state_s5[:, l]], s5_tab)

        qo, ko, vo, ckv = _mla_prep(p, mla_tab, mla_q_norm_g[l][None, :], mla_kv_norm_g[l][None, :], wuq, wukv)
        ckv_c = cache_mla_ckv[:, l].reshape(dec_batch * past, MLA_KV_RANK)
        kv_c = _matmul(ckv_c, wukv, out_dtype=bf).reshape(dec_batch * past, MLA_HEADS, 2, 128)
        kpe_c = jnp.broadcast_to(cache_mla_kpe[:, l].reshape(dec_batch * past, 1, MLA_ROPE).astype(bf),
                                 (dec_batch * past, MLA_HEADS, MLA_ROPE))
        k_c = jnp.concatenate([kv_c[:, :, 0], kpe_c, jnp.zeros_like(kpe_c)], axis=-1).reshape(
            dec_batch * past, MLA_HEADS * ATT_DQ)
        v_c = kv_c[:, :, 1].reshape(dec_batch * past, MLA_HEADS * MLA_V)
        ymla = (_attention(qo, ko, vo, 0, batch, seq_len),
                _attention(qo, ko, vo, n_ctx, dec_batch, dec_seq, k_c, v_c))

        ylru_c, lru_fin = _lru(p, 0, batch, seq_len, jnp.zeros((batch, 2, LRU_W), f32), lru_tab)
        ylru_l, _ = _lru(p, n_ctx, dec_batch, dec_seq, state_lru[:, l].astype(f32), lru_tab)
        ylru = (ylru_c, ylru_l)

        yret_c, ret_fin = _retention(p, 0, batch, seq_len,
                                     jnp.zeros((batch, 2, RET_HEADS, RET_DK, RET_DV), f32), ret_tables)
        yret_l, _ = _retention(p, n_ctx, dec_batch, dec_seq, state_ret[:, l].astype(f32), ret_tables, ret_rope)
        yret = (yret_c, yret_l)

        x_mid, h2, idx, wgt = _mixout(ylin, p, ymla, ylru, yret, x, mods, s5_d[l][None, :],
                                      s5_w_glu[l].astype(bf), w_out[l].astype(bf), norm2_g[l][None, :], rw,
                                      n_ctx, dec_seq)
        y2 = _moe_rows(h2, l, idx, wgt, moe_w_gate, moe_w_up, moe_w_down)
        x = _resid(x_mid, y2, mods, gf, n_ctx, dec_seq, final=(l == depth - 1))

        states.append((ckv[:n_ctx].reshape(batch, seq_len, MLA_KV_RANK),
                       p[:n_ctx, C_KR:C_KR + MLA_ROPE].astype(f32).reshape(batch, seq_len, MLA_ROPE),
                       s5_fins[0], lru_fin, ret_fin))

    y_prompt = x[0].reshape(batch, seq_len, d)
    y_sample = x[1].reshape(dec_batch, dec_seq, d)
    new_cache_mla_ckv = jnp.stack([st[0] for st in states], axis=1)
    new_cache_mla_kpe = jnp.stack([st[1] for st in states], axis=1)
    new_state_s5 = jnp.stack([st[2] for st in states], axis=1)
    new_state_lru = jnp.stack([st[3] for st in states], axis=1)
    new_state_ret = jnp.stack([st[4] for st in states], axis=1)
    return (y_prompt, y_sample, new_cache_mla_ckv, new_cache_mla_kpe, new_state_s5, new_state_lru, new_state_ret)
```

```python
import functools
import numpy as np
import jax
import jax.numpy as jnp
from jax import lax
from jax.experimental import pallas as pl
from jax.experimental.pallas import tpu as pltpu

D_MODEL = 2048
DEPTH = 2
GRID_W = 64
EPS = 1e-6
ROPE_BASE = 10000.0
N_MOD = 6
GROUP_W = 512
S5_CH = GROUP_W
S5_GROUP_CH = 16
S5_GROUPS = S5_CH // S5_GROUP_CH
S5_STATE = 64
MLA_HEADS = 4
MLA_NOPE = 128
MLA_ROPE = 64
MLA_V = 128
MLA_Q_RANK = GROUP_W
MLA_KV_RANK = GROUP_W // 2
MLA_SCALE = (MLA_NOPE + MLA_ROPE) ** -0.5
LRU_W = GROUP_W
LRU_CONV = 4
LRU_C = 8.0
RET_HEADS = 4
RET_DK = 128
RET_DV = 128
RET_CHUNK = 128
N_EXPERTS = 16
N_EXPERT_GROUPS = 4
EXPERTS_PER_GROUP = N_EXPERTS // N_EXPERT_GROUPS
TOP_K = 2
D_EXPERT = D_MODEL // 4

V7X_VMEM_LIMIT = 48 * 1024 * 1024
V7X_VMEM_LIMIT_BIG = 56 * 1024 * 1024

C_S5, C_MQ, C_LX, C_LG, C_RQ, C_RK, C_RV, C_RG, C_RQP, C_RKP, C_MKV, C_KR = (
    0, 512, 1024, 1536, 2048, 2560, 3072, 3584, 4096, 4608, 5120, 5376)
P_COLS = 5632
R_S5, R_MQ, R_MKV, R_KR, R_LX, R_LG, R_RQ, R_RK, R_RV, R_RG = (0, 512, 1024, 1280, 1344, 1856, 2368, 2880, 3392, 3904)


def _cparams(n_axes, limit=V7X_VMEM_LIMIT):
    return pltpu.CompilerParams(dimension_semantics=("arbitrary",) * n_axes, vmem_limit_bytes=limit)


def _sigmoid(x):
    return 1.0 / (1.0 + jnp.exp(-x))


def _gelu_tanh(x):
    return 0.5 * x * (1.0 + jnp.tanh(0.7978845608028654 * (x + 0.044715 * (x * x * x))))


def _rms(x):
    return x * lax.rsqrt(jnp.mean(x * x, axis=-1, keepdims=True) + EPS)


def _seq_of_tile(i, tm, n_ctx, dec_seq):
    assert n_ctx % tm == 0 and dec_seq % tm == 0, "a row tile must not straddle two sequences"
    r = i * tm
    return jnp.where(r < n_ctx, 0, 1 + (r - n_ctx) // dec_seq)


def _mm_kernel(x_ref, w_ref, o_ref):
    o_ref[...] = jnp.dot(x_ref[...].astype(jnp.bfloat16), w_ref[...].astype(jnp.bfloat16),
                         preferred_element_type=jnp.float32).astype(o_ref.dtype)


def _matmul(x, w, tm=512, tn=512, out_dtype=jnp.float32):
    m, k = x.shape
    _, n = w.shape
    tm, tn = min(tm, m), min(tn, n)
    return pl.pallas_call(
        _mm_kernel,
        grid=(n // tn, m // tm),
        in_specs=[pl.BlockSpec((tm, k), lambda j, i: (i, 0)), pl.BlockSpec((k, tn), lambda j, i: (0, j))],
        out_specs=pl.BlockSpec((tm, tn), lambda j, i: (i, j)),
        out_shape=jax.ShapeDtypeStruct((m, n), out_dtype),
        compiler_params=_cparams(2),
        name="matmul",
    )(x, w)


ADA_TN = 1024


def _ada_kernel(c_ref, w_ref, b_ref, o_ref):
    c = c_ref[...]
    s = (c * _sigmoid(c)).astype(jnp.bfloat16)
    o_ref[0] = jnp.dot(s, w_ref[0].astype(jnp.bfloat16), preferred_element_type=jnp.float32) + b_ref[0]


def _ada(cvec, w_ada, b_ada):
    depth, d, n = w_ada.shape
    return pl.pallas_call(
        _ada_kernel,
        grid=(depth, n // ADA_TN),
        in_specs=[pl.BlockSpec((8, d), lambda l, j: (0, 0)),
                  pl.BlockSpec((1, d, ADA_TN), lambda l, j: (l, 0, j)),
                  pl.BlockSpec((1, 1, ADA_TN), lambda l, j: (l, 0, j))],
        out_specs=pl.BlockSpec((1, 8, ADA_TN), lambda l, j: (l, 0, j)),
        out_shape=jax.ShapeDtypeStruct((depth, 8, n), jnp.float32),
        compiler_params=_cparams(2),
        name="adaln_mod",
    )(cvec, w_ada, b_ada.reshape(depth, 1, n))


IN_TM = 1024
IN_TN = 512


def _inproj_kernel(xc_ref, xl_ref, mod_ref, g_ref, w_ref, o_ref, h_s, *, n_ctx_tiles):
    @pl.when(pl.program_id(1) == 0)
    def _():
        x = jnp.where(pl.program_id(0) < n_ctx_tiles, xc_ref[...], xl_ref[...])
        h = _rms(x) * g_ref[...]
        h_s[...] = (h * (1.0 + mod_ref[0, 1:2, :]) + mod_ref[0, 0:1, :]).astype(h_s.dtype)

    o_ref[...] = jnp.dot(h_s[...], w_ref[...], preferred_element_type=jnp.float32).astype(o_ref.dtype)


def _two_part_rows(tm, width, n_ctx_tiles):
    return (pl.BlockSpec((tm, width), lambda i, *_: (jnp.minimum(i, n_ctx_tiles - 1), 0)),
            pl.BlockSpec((tm, width), lambda i, *_: (jnp.maximum(i - n_ctx_tiles, 0), 0)))


def _inproj(x, mods, g, w, n_ctx, dec_seq):
    n, d = x[0].shape[0] + x[1].shape[0], x[0].shape[1]
    na = n_ctx // IN_TM
    seq = functools.partial(_seq_of_tile, tm=IN_TM, n_ctx=n_ctx, dec_seq=dec_seq)
    return pl.pallas_call(
        functools.partial(_inproj_kernel, n_ctx_tiles=na),
        grid=(n // IN_TM, P_COLS // IN_TN),
        in_specs=[*_two_part_rows(IN_TM, d, na),
                  pl.BlockSpec((1, N_MOD, d), lambda i, j: (seq(i), 0, 0)),
                  pl.BlockSpec((1, d), lambda i, j: (0, 0)),
                  pl.BlockSpec((d, IN_TN), lambda i, j: (0, j))],
        out_specs=pl.BlockSpec((IN_TM, IN_TN), lambda i, j: (i, j)),
        out_shape=jax.ShapeDtypeStruct((n, P_COLS), jnp.bfloat16),
        scratch_shapes=[pltpu.VMEM((IN_TM, d), jnp.bfloat16)],
        compiler_params=_cparams(2, V7X_VMEM_LIMIT_BIG),
        name="norm_inproj",
    )(*x, mods, g, w)


def _rot_partner_cols(w, n_heads, head_dim):
    q = head_dim // 4
    wr = w.reshape(w.shape[0], n_heads, 2, 2, q)
    return jnp.stack([-wr[:, :, :, 1], wr[:, :, :, 0]], axis=3).reshape(w.shape)


def _inproj_weights(w_in):
    seg = lambda off, width: w_in[:, off:off + width]
    kr = seg(R_KR, MLA_ROPE)
    cols = [seg(R_S5, 2 * GROUP_W),
            seg(R_LX, 6 * GROUP_W),
            _rot_partner_cols(seg(R_RQ, 2 * GROUP_W), 2 * RET_HEADS, RET_DK),
            seg(R_MKV, MLA_KV_RANK), kr, _rot_partner_cols(kr, 1, MLA_ROPE),
            jnp.zeros((w_in.shape[0], P_COLS - C_KR - 2 * MLA_ROPE), w_in.dtype)]
    return jnp.concatenate(cols, axis=1).astype(jnp.bfloat16)


def _rope_tables(t_len, rot_dim):
    rows = t_len // GRID_W
    row = jnp.repeat(jnp.arange(rows, dtype=jnp.float32), GRID_W)
    col = jnp.tile(jnp.arange(GRID_W, dtype=jnp.float32), rows)
    n_freq = rot_dim // 4
    inv = ROPE_BASE ** (-jnp.arange(n_freq, dtype=jnp.float32) / n_freq)
    ang = jnp.concatenate([row[:, None] * inv[None]] * 2 + [col[:, None] * inv[None]] * 2, axis=1)
    return jnp.cos(ang), jnp.sin(ang)


S5_L = 32
S5_PAIRS = S5_GROUPS // 2
S5_Q = 4
S5_LANES = S5_GROUPS * S5_STATE
S5_SCAN_LANES = 1024


def _s5_tables(a_re, a_im, log_dt, b_re, b_im, c_re, c_im):
    f32 = jnp.float32
    L, G, P, C = S5_L, S5_GROUPS, S5_STATE, S5_GROUP_CH
    hp = lax.Precision.HIGHEST
    cmul = lambda xr, xi, yr, yi: (xr * yr - xi * yi, xr * yi + xi * yr)
    dt = jnp.exp(log_dt.astype(f32))[..., None]
    zr, zi = a_re * dt, a_im * dt
    ab_r, ab_i = jnp.exp(zr) * jnp.cos(zi), jnp.exp(zr) * jnp.sin(zi)
    den = a_re * a_re + a_im * a_im
    nr, ni = ab_r - 1.0, ab_i
    be_r = (nr * a_re + ni * a_im) / den
    be_i = (ni * a_re - nr * a_im) / den
    bt_r, bt_i = b_re.transpose(0, 2, 1), b_im.transpose(0, 2, 1)
    bp_r, bp_i = cmul(be_r[:, :, None, :], be_i[:, :, None, :], bt_r[None], bt_i[None])
    tau = jnp.arange(L + 1, dtype=f32)[None, None, :, None]
    mag = jnp.exp(zr[:, :, None, :] * tau)
    pw_r, pw_i = mag * jnp.cos(zi[:, :, None, :] * tau), mag * jnp.sin(zi[:, :, None, :] * tau)

    cat = lambda *xs: jnp.concatenate(xs, axis=-1)
    pf_r, pf_i = pw_r[0][:, :L][:, ::-1], pw_i[0][:, :L][:, ::-1]
    pb_r, pb_i = pw_r[1][:, :L], pw_i[1][:, :L]
    w1 = cat(pf_r, pf_i, pb_r, pb_i)[:, :, None, :]
    w2 = cat(pf_i, pf_r, pb_i, pb_r)[:, :, None, :]
    b1 = cat(bp_r[0], bp_r[0], bp_r[1], bp_r[1])[:, None]
    b2 = cat(-bp_i[0], bp_i[0], -bp_i[1], bp_i[1])[:, None]
    ws = (w1 * b1 + w2 * b2).reshape(G, L * C, S5_Q * P)

    ct_r, ct_i = c_re.transpose(0, 2, 1), c_im.transpose(0, 2, 1)
    pt_r, pt_i = pw_r.transpose(0, 1, 3, 2), pw_i.transpose(0, 1, 3, 2)

    def cm(er, ei):
        m_r, m_i = cmul(ct_r[:, :, None, :], ct_i[:, :, None, :], er[..., None], ei[..., None])
        return m_r.reshape(G, P, L * C), m_i.reshape(G, P, L * C)

    mf_r, mf_i = cm(pt_r[0][:, :, 1:L + 1], pt_i[0][:, :, 1:L + 1])
    mb_r, mb_i = cm(pt_r[1][:, :, 1:L + 1][:, :, ::-1], pt_i[1][:, :, 1:L + 1][:, :, ::-1])
    wo = jnp.stack([mf_r, -mf_i, mb_r, -mb_i], axis=1).reshape(G, S5_Q * P, L * C)

    def impulse(d):
        m_r, m_i = cm(pt_r[d][:, :, :L], pt_i[d][:, :, :L])
        kk = (jnp.einsum('gkp,gpx->gkx', bp_r[d], m_r, precision=hp)
              - jnp.einsum('gkp,gpx->gkx', bp_i[d], m_i, precision=hp))
        return kk.reshape(G, C, L, C)

    kf, kb = impulse(0), impulse(1)
    wide = jnp.concatenate([kb[:, :, :0:-1], kf[:, :, :1] + kb[:, :, :1], kf[:, :, 1:]], axis=2)
    wide = jnp.pad(wide.reshape(G, C, (2 * L - 1) * C), ((0, 0), (0, 0), (0, C)))

    a_l = jnp.stack([pw_r[0, :, L], pw_i[0, :, L], pw_r[1, :, L], pw_i[1, :, L]], axis=0).reshape(S5_Q, 1, S5_LANES)
    bf = jnp.bfloat16
    return _s5_toeplitz(wide), ws.astype(bf), wo.astype(bf), a_l


def _s5_toeplitz_kernel(w_ref, o_ref):
    w = w_ref[0]
    for s in range(S5_L):
        off = (S5_L - 1 - s) * S5_GROUP_CH
        o_ref[0, s * S5_GROUP_CH:(s + 1) * S5_GROUP_CH, :] = w[:, off:off + S5_L * S5_GROUP_CH].astype(o_ref.dtype)


def _s5_toeplitz(wide):
    g, c, wl = wide.shape
    n = S5_L * S5_GROUP_CH
    return pl.pallas_call(
        _s5_toeplitz_kernel,
        grid=(g,),
        in_specs=[pl.BlockSpec((1, c, wl), lambda i: (i, 0, 0))],
        out_specs=pl.BlockSpec((1, n, n), lambda i: (i, 0, 0)),
        out_shape=jax.ShapeDtypeStruct((g, n, n), jnp.bfloat16),
        compiler_params=_cparams(1),
        name="s5_toeplitz",
    )(wide)


def _s5a_kernel(x_ref, t_ref, ws_ref, y1_ref, s_ref):
    ss = []
    for j in range(2):
        x = x_ref[j]
        y1_ref[j] = jnp.dot(x, t_ref[j], preferred_element_type=jnp.float32)
        ss.append(jnp.dot(x, ws_ref[j], preferred_element_type=jnp.float32))
    p = S5_STATE
    for q in range(S5_Q):
        s_ref[q] = jnp.concatenate([ss[0][:, q * p:(q + 1) * p], ss[1][:, q * p:(q + 1) * p]], axis=1)


def _s5a(x, toe, ws):
    g, r, w = x.shape
    return pl.pallas_call(
        _s5a_kernel,
        grid=(S5_PAIRS,),
        in_specs=[pl.BlockSpec((2, r, w), lambda i: (i, 0, 0)),
                  pl.BlockSpec((2, w, w), lambda i: (i, 0, 0)),
                  pl.BlockSpec((2, w, S5_Q * S5_STATE), lambda i: (i, 0, 0))],
        out_specs=[pl.BlockSpec((2, r, w), lambda i: (i, 0, 0)),
                   pl.BlockSpec((S5_Q, r, 128), lambda i: (0, 0, i))],
        out_shape=[jax.ShapeDtypeStruct((g, r, w), jnp.float32),
                   jax.ShapeDtypeStruct((S5_Q, r, S5_LANES), jnp.float32)],
        compiler_params=_cparams(1),
        name="s5_chunk_local",
    )(x, toe, ws)


def _s5b_kernel(s_ref, a_ref, h0_ref, hp_ref, fin_ref, *, nc):
    def run(qr, qi, order_fwd):
        ar, ai = a_ref[qr], a_ref[qi]

        def body(i, carry):
            hr, hi = carry
            k = i if order_fwd else nc - 1 - i
            hp_ref[qr, pl.ds(k, 1), :] = hr
            hp_ref[qi, pl.ds(k, 1), :] = hi
            nr = ar * hr - ai * hi + s_ref[qr, pl.ds(k, 1), :]
            ni = ar * hi + ai * hr + s_ref[qi, pl.ds(k, 1), :]
            return nr, ni

        hr, hi = lax.fori_loop(0, nc, body, (h0_ref[qr], h0_ref[qi]))
        fin_ref[qr] = hr
        fin_ref[qi] = hi

    run(0, 1, True)
    run(2, 3, False)


def _s5b(s, a_l, h0):
    _, nc, w = s.shape
    tl = S5_SCAN_LANES
    vec = pl.BlockSpec((S5_Q, 1, tl), lambda i: (0, 0, i))
    seq = pl.BlockSpec((S5_Q, nc, tl), lambda i: (0, 0, i))
    return pl.pallas_call(
        functools.partial(_s5b_kernel, nc=nc),
        grid=(w // tl,),
        in_specs=[seq, vec, vec],
        out_specs=[seq, vec],
        out_shape=[jax.ShapeDtypeStruct((S5_Q, nc, w), jnp.float32),
                   jax.ShapeDtypeStruct((S5_Q, 1, w), jnp.float32)],
        compiler_params=_cparams(1),
        name="s5_chunk_scan",
    )(s, a_l, h0)


def _s5c_kernel(h_ref, wo_ref, y1_ref, y_ref):
    p = S5_STATE
    for j in range(2):
        hcat = jnp.concatenate([h_ref[q][:, j * p:(j + 1) * p] for q in range(S5_Q)], axis=1)
        y2 = jnp.dot(hcat.astype(jnp.bfloat16), wo_ref[j], preferred_element_type=jnp.float32)
        y_ref[j] = (y1_ref[j] + y2).astype(y_ref.dtype)


def _s5c(hprev, wo, y1):
    g, r, w = y1.shape
    return pl.pallas_call(
        _s5c_kernel,
        grid=(S5_PAIRS,),
        in_specs=[pl.BlockSpec((S5_Q, r, 128), lambda i: (0, 0, i)),
                  pl.BlockSpec((2, S5_Q * S5_STATE, w), lambda i: (i, 0, 0)),
                  pl.BlockSpec((2, r, w), lambda i: (i, 0, 0))],
        out_specs=pl.BlockSpec((2, r, w), lambda i: (i, 0, 0)),
        out_shape=jax.ShapeDtypeStruct((g, r, w), jnp.bfloat16),
        compiler_params=_cparams(1),
        name="s5_state_to_out",
    )(hprev, wo, y1)


def _s5_to_chunks(u):
    b, t, _ = u.shape
    nc = t // S5_L
    x = u.reshape(b, nc, S5_L, S5_GROUPS, S5_GROUP_CH).transpose(3, 1, 0, 2, 4)
    return x.reshape(S5_GROUPS, nc * b, S5_L * S5_GROUP_CH)


def _s5_from_chunks(y, b, t):
    nc = t // S5_L
    y = y.reshape(S5_GROUPS, nc, b, S5_L, S5_GROUP_CH).transpose(2, 1, 3, 0, 4)
    return y.reshape(b * t, S5_CH)


def _s5_state_planes(h0):
    b = h0.shape[0]
    return h0.transpose(1, 4, 0, 2, 3).reshape(S5_Q, b, S5_LANES)


def _s5_core(us, h0s, tables):
    toe, ws_pair, wo_pair, a_l = tables
    xs = [_s5_to_chunks(u) for u in us]
    rows = [x.shape[1] for x in xs]
    y1, s = _s5a(jnp.concatenate(xs, axis=1), toe, ws_pair)
    hps, fins = [], []
    off = 0
    for u, h0, r in zip(us, h0s, rows):
        b, t, _ = u.shape
        nc = t // S5_L
        if h0 is None:
            h0p = jnp.zeros((S5_Q, 1, b * S5_LANES), jnp.float32)
        else:
            h0p = _s5_state_planes(h0.astype(jnp.float32)).reshape(S5_Q, 1, b * S5_LANES)
        hp, fin = _s5b(s[:, off:off + r].reshape(S5_Q, nc, b * S5_LANES), jnp.tile(a_l, (1, 1, b)), h0p)
        hps.append(hp.reshape(S5_Q, r, S5_LANES))
        fins.append(fin.reshape(2, 2, b, S5_GROUPS, S5_STATE).transpose(2, 0, 3, 4, 1))
        off += r
    y = _s5c(jnp.concatenate(hps, axis=1), wo_pair, y1)
    outs, off = [], 0
    for u, r in zip(us, rows):
        b, t, _ = u.shape
        outs.append(_s5_from_chunks(y[:, off:off + r], b, t))
        off += r
    return jnp.concatenate(outs, axis=0), fins


LRU_TC = 256
LRU_HALO = 16


def _lru_kernel(x_ref, gate_ref, cw_ref, cb_ref, wg_ref, bg_ref, sp_ref, h0_ref, out_ref, fin_ref,
                a_s, b_s, hf_s, *, t_len, tc):
    f32 = jnp.float32
    nt = t_len // tc
    w = LRU_W
    nb = tc // 8
    row = lax.broadcasted_iota(jnp.int32, (nb, 8, w), 1)

    def gates(c, d):
        r0 = pl.multiple_of(c * tc, tc)
        lo = pl.multiple_of(jnp.maximum(r0 - LRU_HALO, 0), LRU_HALO)
        hi = pl.multiple_of(jnp.minimum(r0 + tc, t_len - LRU_HALO), LRU_HALO)
        prev = jnp.where(c > 0, x_ref[pl.ds(lo, LRU_HALO), :].astype(f32), 0.0)
        nxt = jnp.where(c < nt - 1, x_ref[pl.ds(hi, LRU_HALO), :].astype(f32), 0.0)
        slab = jnp.concatenate([prev, x_ref[pl.ds(r0, tc), :].astype(f32), nxt], axis=0)
        o = LRU_HALO - LRU_CONV // 2
        xc = cb_ref[...] + sum(cw_ref[k:k + 1, :] * slab[o + k:o + k + tc] for k in range(LRU_CONV))
        g = jnp.dot(xc.astype(jnp.bfloat16), wg_ref[d], preferred_element_type=f32) + bg_ref[d]
        r = _sigmoid(g[:, :w])
        i = _sigmoid(g[:, w:])
        log_a = -sp_ref[d] * r
        a = jnp.exp(log_a)
        b = (jnp.sqrt(1.0 - a * a) * (i * xc)).reshape(nb, 8, w)
        a = a.reshape(nb, 8, w)
        for sh in (1, 2, 4):
            keep, rot = (row >= sh, sh) if d == 0 else (row < 8 - sh, 8 - sh)
            ap = jnp.where(keep, pltpu.roll(a, rot, 1), 1.0)
            bp = jnp.where(keep, pltpu.roll(b, rot, 1), 0.0)
            b = a * bp + b
            a = a * ap
        a_s[...] = a
        b_s[...] = b
        return r0

    def fwd_chunk(c, h):
        r0 = gates(c, 0)

        def block(k, h):
            h8 = a_s[k] * h + b_s[k]
            hf_s[pl.ds(pl.multiple_of(r0 + k * 8, 8), 8), :] = h8
            return h8[7:8, :]

        return lax.fori_loop(0, nb, block, h, unroll=4)

    h = lax.fori_loop(0, nt, fwd_chunk, h0_ref[0, 0:1, :])
    fin_ref[0, 0:1, :] = h

    def bwd_chunk(ci, h):
        r0 = gates(nt - 1 - ci, 1)

        def block(i, h):
            k = nb - 1 - i
            h8 = a_s[k] * h + b_s[k]
            rows = pl.ds(pl.multiple_of(r0 + k * 8, 8), 8)
            hf_s[rows, :] = hf_s[rows, :] + h8
            return h8[0:1, :]

        h = lax.fori_loop(0, nb, block, h, unroll=4)
        sl = pl.ds(r0, tc)
        out_ref[sl, :] = (hf_s[sl, :] * _gelu_tanh(gate_ref[sl, :].astype(f32))).astype(out_ref.dtype)
        return h

    h = lax.fori_loop(0, nt, bwd_chunk, h0_ref[0, 1:2, :])
    fin_ref[0, 1:2, :] = h


def _block_diag(wb):
    n, k, j = wb.shape
    return (wb[:, :, None, :] * jnp.eye(n, dtype=wb.dtype)[:, None, :, None]).reshape(n * k, n * j)


def _lru_tables(conv_w, conv_b, w_a, b_a, w_x, b_x, lam):
    wg = jnp.stack([jnp.concatenate([_block_diag(w_a[d]), _block_diag(w_x[d])], axis=1) for d in range(2)])
    bg = jnp.concatenate([b_a, b_x], axis=-1)[:, None, :]
    sp = (LRU_C * jax.nn.softplus(-lam.astype(jnp.float32)))[:, None, :]
    return conv_w, conv_b[None, :], wg.astype(jnp.bfloat16), bg, sp


def _lru(p, row0, b, t, h0, tables):
    w = LRU_W
    tc = min(LRU_TC, t)
    cw, cb, wg, bg, sp = tables
    full = lambda a: pl.BlockSpec(a.shape, lambda i: (0,) * a.ndim)
    rb = row0 // t
    return pl.pallas_call(
        functools.partial(_lru_kernel, t_len=t, tc=tc),
        grid=(b,),
        in_specs=[pl.BlockSpec((t, w), lambda i: (rb + i, C_LX // w)),
                  pl.BlockSpec((t, w), lambda i: (rb + i, C_LG // w)),
                  full(cw), full(cb), full(wg), full(bg), full(sp),
                  pl.BlockSpec((1, 2, w), lambda i: (i, 0, 0))],
        out_specs=[pl.BlockSpec((t, w), lambda i: (i, 0)),
                   pl.BlockSpec((1, 2, w), lambda i: (i, 0, 0))],
        out_shape=[jax.ShapeDtypeStruct((b * t, w), jnp.bfloat16),
                   jax.ShapeDtypeStruct((b, 2, w), jnp.float32)],
        scratch_shapes=[pltpu.VMEM((tc // 8, 8, w), jnp.float32), pltpu.VMEM((tc // 8, 8, w), jnp.float32),
                        pltpu.VMEM((t, w), jnp.float32)],
        compiler_params=_cparams(1),
        name="rglru",
    )(p, p, cw, cb, wg, bg, sp, h0)


MLA_TM = 512
ATT_DQ = 256


def _mla_prep_kernel(q_ref, kv_ref, kr_ref, tab_ref, gq_ref, gkv_ref, wuq_ref, wukv_ref,
                     qo_ref, ko_ref, vo_ref, ckv_ref):
    f32, bf = jnp.float32, jnp.bfloat16
    tab = tab_ref[...]

    def rope(blk):
        prod = blk * tab
        return prod + pltpu.roll(prod, 64, 1)

    qn = (_rms(q_ref[...].astype(f32)) * gq_ref[...]).astype(bf)
    qq = jnp.dot(qn, wuq_ref[...], preferred_element_type=f32)
    ckv = _rms(kv_ref[...].astype(f32)) * gkv_ref[...]
    ckv_ref[...] = ckv
    kk = jnp.dot(ckv.astype(bf), wukv_ref[...], preferred_element_type=f32)
    lane = lax.broadcasted_iota(jnp.int32, tab.shape, 1)
    kpe = jnp.where(lane < MLA_ROPE, rope(kr_ref[...].astype(f32)), 0.0).astype(bf)
    for h in range(MLA_HEADS):
        o = h * ATT_DQ
        qo_ref[:, o:o + 128] = (qq[:, o:o + 128] * MLA_SCALE).astype(bf)
        qo_ref[:, o + 128:o + 256] = (rope(qq[:, o + 128:o + 256]) * MLA_SCALE).astype(bf)
        ko_ref[:, o:o + 128] = kk[:, o:o + 128].astype(bf)
        ko_ref[:, o + 128:o + 256] = kpe
        vo_ref[:, h * MLA_V:(h + 1) * MLA_V] = kk[:, o + 128:o + 256].astype(bf)


def _mla_weights(w_uq, w_ukv):
    wq = w_uq.reshape(MLA_Q_RANK, MLA_HEADS, MLA_NOPE + MLA_ROPE)
    pe = wq[:, :, MLA_NOPE:].reshape(MLA_Q_RANK, MLA_HEADS * MLA_ROPE)
    pep = _rot_partner_cols(pe, MLA_HEADS, MLA_ROPE).reshape(MLA_Q_RANK, MLA_HEADS, MLA_ROPE)
    wq_ext = jnp.concatenate([wq, pep], axis=-1).reshape(MLA_Q_RANK, MLA_HEADS * ATT_DQ)
    return wq_ext.astype(jnp.bfloat16), w_ukv.astype(jnp.bfloat16)


def _mla_prep(p, tab, gq, gkv, wuq, wukv):
    n = p.shape[0]
    tm = MLA_TM
    full = lambda a: pl.BlockSpec(a.shape, lambda i: (0,) * a.ndim)
    row = lambda width, col: pl.BlockSpec((tm, width), lambda i: (i, col // width))
    return pl.pallas_call(
        _mla_prep_kernel,
        grid=(n // tm,),
        in_specs=[row(MLA_Q_RANK, C_MQ), row(MLA_KV_RANK, C_MKV), row(128, C_KR), row(128, 0),
                  full(gq), full(gkv), full(wuq), full(wukv)],
        out_specs=[row(MLA_HEADS * ATT_DQ, 0), row(MLA_HEADS * ATT_DQ, 0), row(MLA_HEADS * MLA_V, 0),
                   row(MLA_KV_RANK, 0)],
        out_shape=[jax.ShapeDtypeStruct((n, MLA_HEADS * ATT_DQ), jnp.bfloat16),
                   jax.ShapeDtypeStruct((n, MLA_HEADS * ATT_DQ), jnp.bfloat16),
                   jax.ShapeDtypeStruct((n, MLA_HEADS * MLA_V), jnp.bfloat16),
                   jax.ShapeDtypeStruct((n, MLA_KV_RANK), jnp.float32)],
        compiler_params=_cparams(1),
        name="mla_prep",
    )(p, p, p, tab, gq, gkv, wuq, wukv)


ATT_TQ = 512
ATT_SUB = 256


def _attn_kernel(*refs, two, n_sub):
    dn = (((1,), (1,)), ((), ()))
    f32 = jnp.float32
    if two:
        q_ref, k_ref, v_ref, k2_ref, v2_ref, o_ref = refs
    else:
        q_ref, k_ref, v_ref, o_ref = refs
    rows = q_ref.shape[0] // n_sub
    for part in range(n_sub):
        sl = slice(part * rows, (part + 1) * rows)
        q = q_ref[sl, :]
        s = lax.dot_general(q, k_ref[...], dn, preferred_element_type=f32)
        m = jnp.max(s, axis=-1, keepdims=True)
        if two:
            s2 = lax.dot_general(q, k2_ref[...], dn, preferred_element_type=f32)
            m = jnp.maximum(m, jnp.max(s2, axis=-1, keepdims=True))
        p = jnp.exp(s - m)
        l = jnp.sum(p, axis=-1, keepdims=True)
        o = jnp.dot(p.astype(jnp.bfloat16), v_ref[...], preferred_element_type=f32)
        if two:
            p2 = jnp.exp(s2 - m)
            l = l + jnp.sum(p2, axis=-1, keepdims=True)
            o = o + jnp.dot(p2.astype(jnp.bfloat16), v2_ref[...], preferred_element_type=f32)
        o_ref[sl, :] = (o / l).astype(o_ref.dtype)


def _attention(q, k, v, row0, b, t, k2=None, v2=None):
    tb = min(ATT_TQ, t)
    nq = t // tb
    qb, kb = row0 // tb, row0 // t
    two = k2 is not None
    in_specs = [pl.BlockSpec((tb, ATT_DQ), lambda bi, h, i: (qb + bi * nq + i, h)),
                pl.BlockSpec((t, ATT_DQ), lambda bi, h, i: (kb + bi, h)),
                pl.BlockSpec((t, MLA_V), lambda bi, h, i: (kb + bi, h))]
    args = [q, k, v]
    if two:
        t2 = k2.shape[0] // b
        in_specs += [pl.BlockSpec((t2, ATT_DQ), lambda bi, h, i: (bi, h)),
                     pl.BlockSpec((t2, MLA_V), lambda bi, h, i: (bi, h))]
        args += [k2, v2]
    return pl.pallas_call(
        functools.partial(_attn_kernel, two=two, n_sub=max(tb // ATT_SUB, 1)),
        grid=(b, MLA_HEADS, nq),
        in_specs=in_specs,
        out_specs=pl.BlockSpec((tb, MLA_V), lambda bi, h, i: (bi * nq + i, h)),
        out_shape=jax.ShapeDtypeStruct((b * t, MLA_HEADS * MLA_V), jnp.bfloat16),
        compiler_params=_cparams(3),
        name="mla_attention",
    )(*args)


def _ret_tables():
    f32 = jnp.float32
    c = RET_CHUNK
    log_g = jnp.log1p(-jnp.exp2(-5.0 - jnp.arange(RET_HEADS, dtype=f32)))[:, None, None]
    idx = jnp.arange(c, dtype=f32)
    dec = jnp.exp(jnp.abs(idx[:, None] - idx[None, :])[None] * log_g)
    row = lambda e: jnp.broadcast_to(jnp.exp(e[None, :, None] * log_g), (RET_HEADS, c, c))
    return jnp.stack([dec, row(idx + 1.0), row(c - idx), row(c - 1.0 - idx), row(idx)], axis=0)


RET_HPS = 2


def _ret_kernel(*refs, nc, rope):
    f32, bf = jnp.float32, jnp.bfloat16
    if rope:
        (q_ref, k_ref, v_ref, g_ref, qp_ref, kp_ref, cos_ref, sin_ref, tab_ref, s0_ref, o_ref, fin_ref,
         acc_s, st_s) = refs
    else:
        q_ref, k_ref, v_ref, g_ref, tab_ref, s0_ref, o_ref, fin_ref, acc_s, st_s = refs
    c, dk = RET_CHUNK, RET_DK
    nt_dims, tn_dims = (((1,), (1,)), ((), ())), (((0,), (0,)), ((), ()))

    def chunk(k0, hh):
        sl = pl.ds(pl.multiple_of(k0 * c, c), c)
        cols = slice(hh * dk, (hh + 1) * dk)
        qc, kc = q_ref[sl, cols].astype(f32), k_ref[sl, cols].astype(f32)
        if rope:
            cos, sin = cos_ref[sl, :], sin_ref[sl, :]
            qc = qc * cos + qp_ref[sl, cols].astype(f32) * sin
            kc = kc * cos + kp_ref[sl, cols].astype(f32) * sin
        return sl, cols, qc.astype(bf), kc * (RET_DK ** -0.5), v_ref[sl, cols]

    st_s[...] = s0_ref[0]
    acc_s[...] = jnp.zeros_like(acc_s)

    def body(i, carry):
        for hh in range(RET_HPS):
            dec, xif, xib, zf, zb = (tab_ref[t, hh] for t in range(5))
            g_chunk = xif[c - 1:c, :]
            sl, cols, qc, kc, vc = chunk(i, hh)
            s = st_s[0, hh]
            sc = lax.dot_general(qc, kc.astype(bf), nt_dims, preferred_element_type=f32) * dec
            acc_s[sl, cols] += (jnp.dot(sc.astype(bf), vc, preferred_element_type=f32)
                                + jnp.dot(qc, s.astype(bf), preferred_element_type=f32) * xif)
            st_s[0, hh] = g_chunk * s + lax.dot_general((kc * zf).astype(bf), vc, tn_dims,
                                                        preferred_element_type=f32)
            sl, cols, qc, kc, vc = chunk(nc - 1 - i, hh)
            s = st_s[1, hh]
            acc_s[sl, cols] += jnp.dot(qc, s.astype(bf), preferred_element_type=f32) * xib
            st_s[1, hh] = g_chunk * s + lax.dot_general((kc * zb).astype(bf), vc, tn_dims,
                                                        preferred_element_type=f32)
        return carry

    lax.fori_loop(0, nc, body, 0)
    fin_ref[0] = st_s[...]

    def finish(i, carry):
        sl = pl.ds(pl.multiple_of(i * c, c), c)
        for hh in range(RET_HPS):
            cols = slice(hh * dk, (hh + 1) * dk)
            g = g_ref[sl, cols].astype(f32)
            o_ref[sl, cols] = (_rms(acc_s[sl, cols]) * (g * _sigmoid(g))).astype(o_ref.dtype)
        return carry

    lax.fori_loop(0, nc, finish, 0)


def _retention(p, row0, b, t, s0, tables, rope_tabs=None):
    nc = t // RET_CHUNK
    rb = row0 // t
    rope = rope_tabs is not None
    wd = RET_HPS * RET_DK
    col = lambda c0: pl.BlockSpec((t, wd), lambda bi, h: (rb + bi, c0 // wd + h))
    st = pl.BlockSpec((1, 2, RET_HPS, RET_DK, RET_DV), lambda bi, h: (bi, 0, h, 0, 0))
    in_specs = [col(C_RQ), col(C_RK), col(C_RV), col(C_RG)]
    args = [p, p, p, p]
    if rope:
        tab = pl.BlockSpec((t, RET_DK), lambda bi, h: (0, 0))
        in_specs += [col(C_RQP), col(C_RKP), tab, tab]
        args += [p, p, rope_tabs[0], rope_tabs[1]]
    in_specs += [pl.BlockSpec((5, RET_HPS, RET_CHUNK, RET_CHUNK), lambda bi, h: (0, h, 0, 0)), st]
    args += [tables, s0]
    return pl.pallas_call(
        functools.partial(_ret_kernel, nc=nc, rope=rope),
        grid=(b, RET_HEADS // RET_HPS),
        in_specs=in_specs,
        out_specs=[pl.BlockSpec((t, wd), lambda bi, h: (bi, h)), st],
        out_shape=[jax.ShapeDtypeStruct((b * t, RET_HEADS * RET_DV), jnp.bfloat16),
                   jax.ShapeDtypeStruct((b, 2, RET_HEADS, RET_DK, RET_DV), jnp.float32)],
        scratch_shapes=[pltpu.VMEM((t, wd), jnp.float32), pltpu.VMEM((2, RET_HPS, RET_DK, RET_DV), jnp.float32)],
        compiler_params=_cparams(2, V7X_VMEM_LIMIT_BIG),
        name="retention",
    )(*args)


OUT_TM = 512
ROUTER_PAD = 128


ROW_TILE = (8, D_MODEL // 16)


def _pack_halves(x):
    w = x.shape[1] // 2
    bits = lambda a: lax.bitcast_convert_type(a.astype(jnp.bfloat16).astype(jnp.float32), jnp.int32)
    return bits(x[:, w:]) | lax.shift_right_logical(bits(x[:, :w]), 16)


def _unpack_halves(u):
    lo = lax.bitcast_convert_type(lax.shift_left(u, 16), jnp.float32)
    hi = lax.bitcast_convert_type(u & jnp.int32(-65536), jnp.float32)
    return jnp.concatenate([lo, hi], axis=1)


def _split_bf16(x):
    hi = x.astype(jnp.bfloat16)
    return hi, (x - hi.astype(jnp.float32)).astype(jnp.bfloat16)


def _route(h, whi_ref, wlo_ref, rb_ref):
    f32 = jnp.float32
    hi, lo = _split_bf16(h)
    dn = (((1,), (1,)), ((), ()))
    lt = (lax.dot_general(whi_ref[...], hi, dn, preferred_element_type=f32)
          + lax.dot_general(whi_ref[...], lo, dn, preferred_element_type=f32)
          + lax.dot_general(wlo_ref[...], hi, dn, preferred_element_type=f32))[:N_EXPERTS]
    m = jnp.max(lt, axis=0, keepdims=True)
    e = jnp.exp(lt - m)
    sc = e / jnp.sum(e, axis=0, keepdims=True)
    sel = sc + rb_ref[...][:N_EXPERTS, 0:1]
    rows = lambda a: [a[j:j + 1, :] for j in range(N_EXPERTS)]
    sel_r, sc_r = rows(sel), rows(sc)
    epg = EXPERTS_PER_GROUP

    def top2sum(a, b, c, d):
        h1, l1, h2, l2 = jnp.maximum(a, b), jnp.minimum(a, b), jnp.maximum(c, d), jnp.minimum(c, d)
        return jnp.maximum(h1, h2) + jnp.maximum(jnp.minimum(h1, h2), jnp.maximum(l1, l2))

    gs = [top2sum(*sel_r[g * epg:(g + 1) * epg]) for g in range(N_EXPERT_GROUPS)]
    best, gi = gs[0], jnp.zeros_like(gs[0], dtype=jnp.int32)
    for g in range(1, N_EXPERT_GROUPS):
        upd = gs[g] > best
        gi = jnp.where(upd, g, gi)
        best = jnp.where(upd, gs[g], best)

    def pick(r, j):
        out = r[j]
        for g in range(1, N_EXPERT_GROUPS):
            out = jnp.where(gi == g, r[g * epg + j], out)
        return out

    v = [pick(sel_r, j) for j in range(epg)]
    s = [pick(sc_r, j) for j in range(epg)]

    def argmax_first(vals):
        bv, bi = vals[0], jnp.zeros_like(gi)
        for j in range(1, epg):
            upd = vals[j] > bv
            bi = jnp.where(upd, j, bi)
            bv = jnp.where(upd, vals[j], bv)
        return bi

    i1 = argmax_first(v)
    neg = jnp.float32(-jnp.inf)
    i2 = argmax_first([jnp.where(i1 == j, neg, v[j]) for j in range(epg)])
    take = lambda i: sum(jnp.where(i == j, s[j], 0.0) for j in range(epg))
    w1, w2 = take(i1), take(i2)
    tot = w1 + w2
    return (jnp.concatenate([gi * epg + i1, gi * epg + i2], axis=0),
            jnp.concatenate([w1 / tot, w2 / tot], axis=0))


def _mixout_kernel(ylin_ref, u_ref, ymla_c, ymla_l, ylru_c, ylru_l, yret_c, yret_l, x_c, x_l, mod_ref, d_ref,
                   wglu_ref, wout_ref, g2n_ref, whi_ref, wlo_ref, rb_ref, xo_ref, h2_ref, idx_ref, wgt_ref,
                   *, n_ctx_tiles):
    f32, bf = jnp.float32, jnp.bfloat16
    w = GROUP_W
    is_ctx = pl.program_id(0) < n_ctx_tiles
    pick = lambda c_ref, l_ref: jnp.where(is_ctx, c_ref[...], l_ref[...])
    y = _gelu_tanh(ylin_ref[...].astype(f32) + d_ref[...] * u_ref[...].astype(f32))
    y5 = (y * _sigmoid(jnp.dot(y.astype(bf), wglu_ref[...], preferred_element_type=f32))).astype(bf)
    mix = (jnp.dot(y5, wout_ref[0:w, :], preferred_element_type=f32)
           + jnp.dot(pick(ymla_c, ymla_l), wout_ref[w:2 * w, :], preferred_element_type=f32)
           + jnp.dot(pick(ylru_c, ylru_l), wout_ref[2 * w:3 * w, :], preferred_element_type=f32)
           + jnp.dot(pick(yret_c, yret_l), wout_ref[3 * w:4 * w, :], preferred_element_type=f32))
    x = pick(x_c, x_l) + mod_ref[0, 2:3, :] * mix
    xo_ref[...] = x
    h2 = _rms(x) * g2n_ref[...] * (1.0 + mod_ref[0, 4:5, :]) + mod_ref[0, 3:4, :]
    h2_ref[...] = _pack_halves(h2).reshape(h2_ref.shape)
    idx, wgt = _route(h2, whi_ref, wlo_ref, rb_ref)
    idx_ref[...] = idx
    wgt_ref[...] = wgt


def _router_weights(router_w, router_b):
    d = router_w.shape[0]
    wt = jnp.zeros((ROUTER_PAD, d), jnp.float32).at[:N_EXPERTS].set(router_w.T.astype(jnp.float32))
    whi, wlo = _split_bf16(wt)
    rb = jnp.zeros((ROUTER_PAD, 128), jnp.float32).at[:N_EXPERTS].set(router_b.astype(jnp.float32)[:, None])
    return whi, wlo, rb


def _mixout(ylin, p, ymla, ylru, yret, x, mods, s5_d, wglu, wout, g2n, rw, n_ctx, dec_seq):
    n, d = x[0].shape[0] + x[1].shape[0], x[0].shape[1]
    tm = OUT_TM
    w = GROUP_W
    na = n_ctx // tm
    seq = functools.partial(_seq_of_tile, tm=tm, n_ctx=n_ctx, dec_seq=dec_seq)
    full = lambda a: pl.BlockSpec(a.shape, lambda i: (0,) * a.ndim)
    row = lambda width: pl.BlockSpec((tm, width), lambda i: (i, 0))
    ctx_row, lat_row = _two_part_rows(tm, w, na)
    lanes = pl.BlockSpec((TOP_K, tm), lambda i: (0, i))
    whi, wlo, rb = rw
    return pl.pallas_call(
        functools.partial(_mixout_kernel, n_ctx_tiles=na),
        grid=(n // tm,),
        in_specs=[row(w), row(w), ctx_row, lat_row, ctx_row, lat_row, ctx_row, lat_row, *_two_part_rows(tm, d, na),
                  pl.BlockSpec((1, N_MOD, d), lambda i: (seq(i), 0, 0)),
                  full(s5_d), full(wglu), full(wout), full(g2n), full(whi), full(wlo), full(rb)],
        out_specs=[row(d), pl.BlockSpec((tm,) + ROW_TILE, lambda i: (i, 0, 0)), lanes, lanes],
        out_shape=[jax.ShapeDtypeStruct((n, d), jnp.float32), jax.ShapeDtypeStruct((n,) + ROW_TILE, jnp.int32),
                   jax.ShapeDtypeStruct((TOP_K, n), jnp.int32), jax.ShapeDtypeStruct((TOP_K, n), jnp.float32)],
        compiler_params=_cparams(1, V7X_VMEM_LIMIT_BIG),
        name="mix_out_norm_route",
    )(ylin, p, *ymla, *ylru, *yret, *x, mods, s5_d, wglu, wout, g2n, whi, wlo, rb)


MOE_TM = 256
MOE_NB = 256


def _expert_kernel(te_ref, nt_ref, nv_ref, tok0_ref, tok1_ref, w_ref, dstp_ref, dst_ref, wg_ref, wu_ref, wd_ref,
                   h_hbm, y_hbm, wg_s, wu_s, wd_s, xbuf, xs_s, act_s, ybuf, gsem, ssem):
    i = pl.program_id(0)
    bf = jnp.bfloat16
    tm = xs_s.shape[0]
    nt = nt_ref[0]

    def gather_start(tok_ref, sl, unroll):
        def row(r, carry):
            pltpu.make_async_copy(h_hbm.at[tok_ref[0, 0, r]], xbuf.at[sl, r], gsem.at[sl]).start()
            return carry

        lax.fori_loop(0, tm, row, 0, unroll=unroll)

    def gather_done(sl):
        pltpu.make_async_copy(h_hbm.at[pl.ds(0, tm)], xbuf.at[sl], gsem.at[sl]).wait()

    def scatter_row(d_ref, sl, r):
        return pltpu.make_async_copy(ybuf.at[sl, r], y_hbm.at[d_ref[0, 0, r]], ssem.at[sl])

    def scatter_done(sl, n_rows):
        pltpu.make_async_copy(ybuf.at[sl, pl.ds(0, n_rows)], y_hbm.at[pl.ds(0, n_rows)], ssem.at[sl]).wait()

    def tile(slot):
        n_prev = jnp.where(i > 0, nv_ref[jnp.maximum(i - 1, 0)], 0)
        f = wg_s.shape[1]
        d = wd_s.shape[1]
        up_pieces, down_pieces = 2 * (f // MOE_NB), d // MOE_NB

        def gather_rows(piece):
            for r in range(piece * tm // up_pieces, (piece + 1) * tm // up_pieces):
                pltpu.make_async_copy(h_hbm.at[tok1_ref[0, 0, r]], xbuf.at[1 - slot, r],
                                      gsem.at[1 - slot]).start(priority=r % 2)

        def scatter_rows(piece):
            for r in range(piece * tm // down_pieces, (piece + 1) * tm // down_pieces):
                @pl.when(r < n_prev)
                def _():
                    scatter_row(dstp_ref, 1 - slot, r).start(priority=r % 2)

        gather_done(slot)
        xs_s[...] = _unpack_halves(xbuf[slot].reshape(tm, -1)).astype(bf)
        x = xs_s[...]
        gs, us = [], []
        for c in range(f // MOE_NB):
            cols = slice(c * MOE_NB, (c + 1) * MOE_NB)
            gs.append(jnp.dot(x, wg_s[:, cols], preferred_element_type=jnp.float32))
            gather_rows(2 * c)
            us.append(jnp.dot(x, wu_s[:, cols], preferred_element_type=jnp.float32))
            gather_rows(2 * c + 1)
        g = jnp.concatenate(gs, axis=1)
        u = jnp.concatenate(us, axis=1)
        act_s[...] = ((g * _sigmoid(g)) * u * w_ref[...]).astype(bf)

        @pl.when(nt > 0)
        def _():
            act = act_s[...]
            ys = []
            for c in range(down_pieces):
                ys.append(jnp.dot(act, wd_s[:, c * MOE_NB:(c + 1) * MOE_NB], preferred_element_type=jnp.float32))
                scatter_rows(c)
            y = _pack_halves(jnp.concatenate(ys, axis=1)).reshape(ybuf.shape[1:])

            @pl.when(i > 1)
            def _():
                scatter_done(slot, nv_ref[jnp.maximum(i - 2, 0)])

            ybuf[slot] = y

        @pl.when(i == nt - 1)
        def _():
            def row(r, carry):
                scatter_row(dst_ref, slot, r).start()
                return carry

            lax.fori_loop(0, nv_ref[i], row, 0)

            @pl.when(i > 0)
            def _():
                scatter_done(1 - slot, nv_ref[jnp.maximum(i - 1, 0)])

            scatter_done(slot, nv_ref[i])
            gather_done(1 - slot)

    @pl.when(i < nt)
    def _():
        @pl.when(i == 0)
        def _():
            gather_start(tok0_ref, 0, 8)

        @pl.when((i == 0) | (te_ref[i] != te_ref[jnp.maximum(i - 1, 0)]))
        def _():
            wg_s[...] = wg_ref[0, 0].astype(bf)
            wu_s[...] = wu_ref[0, 0].astype(bf)
            wd_s[...] = wd_ref[0, 0].astype(bf)

        @pl.when(i % 2 == 0)
        def _():
            tile(0)

        @pl.when(i % 2 == 1)
        def _():
            tile(1)


def _experts(h2p, layer, ws, tok, dst, tile_expert, n_tiles_used, n_valid, wg, wu, wd):
    n = h2p.shape[0]
    d = 2 * ROW_TILE[0] * ROW_TILE[1]
    tm = MOE_TM
    n_tiles = tok.shape[0]
    f = wg.shape[-1]
    smem_row = lambda delta: pl.BlockSpec(
        (1, 1, tm), lambda i, te, nt, nv: (jnp.clip(i + delta, 0, n_tiles - 1), 0, 0), memory_space=pltpu.SMEM)
    wspec = lambda a, b: pl.BlockSpec((1, 1, a, b), lambda i, te, nt, nv: (layer, te[i], 0, 0))
    return pl.pallas_call(
        _expert_kernel,
        grid_spec=pltpu.PrefetchScalarGridSpec(
            num_scalar_prefetch=3,
            grid=(n_tiles,),
            in_specs=[smem_row(0), smem_row(1),
                      pl.BlockSpec((tm, 1), lambda i, te, nt, nv: (i, 0)),
                      smem_row(-1), smem_row(0),
                      wspec(d, f), wspec(d, f), wspec(f, d),
                      pl.BlockSpec(memory_space=pl.ANY)],
            out_specs=pl.BlockSpec(memory_space=pl.ANY),
            scratch_shapes=[pltpu.VMEM((d, f), jnp.bfloat16), pltpu.VMEM((d, f), jnp.bfloat16),
                            pltpu.VMEM((f, d), jnp.bfloat16), pltpu.VMEM((2, tm) + ROW_TILE, jnp.int32),
                            pltpu.VMEM((tm, d), jnp.bfloat16), pltpu.VMEM((tm, f), jnp.bfloat16),
                            pltpu.VMEM((2, tm) + ROW_TILE, jnp.int32),
                            pltpu.SemaphoreType.DMA((2,)), pltpu.SemaphoreType.DMA((2,))]),
        out_shape=jax.ShapeDtypeStruct((TOP_K * n,) + ROW_TILE, jnp.int32),
        compiler_params=pltpu.CompilerParams(dimension_semantics=("arbitrary",),
                                             vmem_limit_bytes=V7X_VMEM_LIMIT_BIG, disable_bounds_checks=True),
        name="moe_experts",
    )(tile_expert, n_tiles_used, n_valid, tok, tok, ws, dst, dst, wg, wu, wd, h2p)


def _moe_dispatch(idx, wgt):
    n = idx.shape[1]
    tm = MOE_TM
    n_pairs = TOP_K * n
    m_pad = n_pairs + N_EXPERTS * tm
    n_tiles = m_pad // tm
    e_flat = idx.reshape(-1)
    order = jnp.argsort(e_flat, stable=True).astype(jnp.int32)
    experts = jnp.arange(N_EXPERTS, dtype=jnp.int32)
    counts = jnp.sum((e_flat[None, :] == experts[:, None]).astype(jnp.int32), axis=1)
    starts_raw = jnp.cumsum(counts) - counts
    padded = ((counts + tm - 1) // tm) * tm
    ends = jnp.cumsum(padded)
    starts_pad = ends - padded
    tile_start = jnp.arange(n_tiles, dtype=jnp.int32) * tm
    tile_expert = jnp.minimum(jnp.sum((tile_start[:, None] >= ends[None, :]).astype(jnp.int32), axis=1),
                              N_EXPERTS - 1)
    r = jnp.arange(m_pad, dtype=jnp.int32)
    te_r = jnp.repeat(tile_expert, tm)
    off = r - jnp.take(starts_pad, te_r)
    valid = (off >= 0) & (off < jnp.take(counts, te_r))
    pair = jnp.take(order, jnp.clip(jnp.take(starts_raw, te_r) + off, 0, n_pairs - 1))
    tok = jnp.where(valid, pair % n, 0).reshape(n_tiles, 1, tm)
    dst = jnp.where(valid, pair, 0).reshape(n_tiles, 1, tm)
    ws = jnp.where(valid, jnp.take(wgt.reshape(-1), pair), 0.0)[:, None]
    n_tiles_used = (ends[-1] // tm).astype(jnp.int32).reshape(1)
    n_valid = jnp.sum(valid.reshape(n_tiles, tm), axis=1).astype(jnp.int32)
    return tok, dst, ws, tile_expert, n_tiles_used, n_valid


def _moe_rows(h2p, layer, idx, wgt, wg, wu, wd):
    tok, dst, ws, tile_expert, n_tiles_used, n_valid = _moe_dispatch(idx, wgt)
    return _experts(h2p, layer, ws, tok, dst, tile_expert, n_tiles_used, n_valid, wg, wu, wd)


RES_TM = 512


def _resid_kernel(x_ref, ya_ref, yb_ref, mod_ref, gf_ref, oc_ref, ol_ref, *, final, n_ctx_tiles):
    rows = lambda r: _unpack_halves(r[...].reshape(r.shape[0], -1))
    x = x_ref[...] + mod_ref[0, 5:6, :] * (rows(ya_ref) + rows(yb_ref))
    y = _rms(x) * gf_ref[...] if final else x
    i = pl.program_id(0)

    @pl.when(i < n_ctx_tiles)
    def _():
        oc_ref[...] = y

    @pl.when(i >= n_ctx_tiles)
    def _():
        ol_ref[...] = y


def _resid(x, y2, mods, gf, n_ctx, dec_seq, final):
    n, d = x.shape
    tm = RES_TM
    na = n_ctx // tm
    seq = functools.partial(_seq_of_tile, tm=tm, n_ctx=n_ctx, dec_seq=dec_seq)
    row = pl.BlockSpec((tm, d), lambda i: (i, 0))
    return pl.pallas_call(
        functools.partial(_resid_kernel, final=final, n_ctx_tiles=na),
        grid=(n // tm,),
        in_specs=[row, pl.BlockSpec((tm,) + ROW_TILE, lambda i: (i, 0, 0)),
                  pl.BlockSpec((tm,) + ROW_TILE, lambda i: (n // tm + i, 0, 0)),
                  pl.BlockSpec((1, N_MOD, d), lambda i: (seq(i), 0, 0)),
                  pl.BlockSpec((1, d), lambda i: (0, 0))],
        out_specs=list(_two_part_rows(tm, d, na)),
        out_shape=[jax.ShapeDtypeStruct((n_ctx, d), jnp.float32), jax.ShapeDtypeStruct((n - n_ctx, d), jnp.float32)],
        compiler_params=_cparams(1),
        name="moe_residual_norm",
    )(x, y2, y2, mods, gf)


def kernel(x_prompt, x_sample, c, cache_mla_ckv, cache_mla_kpe, state_s5, state_lru, state_ret,
           c_ctx, w_ada, b_ada, norm1_g, norm2_g, w_in, w_out,
           s5_a_re, s5_a_im, s5_log_dt, s5_b_re, s5_b_im, s5_c_re, s5_c_im, s5_d, s5_w_glu,
           mla_q_norm_g, mla_w_uq, mla_kv_norm_g, mla_w_ukv,
           lru_conv_w, lru_conv_b, lru_w_a, lru_b_a, lru_w_x, lru_b_x, lru_lambda,
           router_w, router_b, moe_w_gate, moe_w_up, moe_w_down, final_norm_g):
    f32, bf = jnp.float32, jnp.bfloat16
    batch, seq_len, d = x_prompt.shape
    dec_batch, dec_seq, _ = x_sample.shape
    past = cache_mla_ckv.shape[2]
    n_ctx, n_lat = batch * seq_len, dec_batch * dec_seq
    depth = w_in.shape[0]

    x = (x_prompt.reshape(n_ctx, d), x_sample.reshape(n_lat, d))
    cvec = jnp.zeros((8, d), f32).at[0].set(c_ctx).at[1:1 + dec_batch].set(c)
    mods_all = _ada(cvec, w_ada, b_ada).reshape(depth, 8, N_MOD, d)

    cos64, sin64 = _rope_tables(dec_seq, MLA_ROPE)
    lat_tab = jnp.tile(jnp.concatenate([cos64, sin64], axis=1), (dec_batch, 1))
    ctx_tab = jnp.concatenate([jnp.ones((n_ctx, MLA_ROPE), f32), jnp.zeros((n_ctx, MLA_ROPE), f32)], axis=1)
    mla_tab = jnp.concatenate([ctx_tab, lat_tab], axis=0)
    ret_rope = _rope_tables(dec_seq, RET_DK)
    ret_tables = _ret_tables()
    rw = _router_weights(router_w, router_b)
    gf = final_norm_g[None, :]

    states = []
    for l in range(depth):
        mods = mods_all[l]
        s5_tab = _s5_tables(s5_a_re[l], s5_a_im[l], s5_log_dt[l], s5_b_re[l], s5_b_im[l], s5_c_re[l], s5_c_im[l])
        lru_tab = _lru_tables(lru_conv_w[l], lru_conv_b[l], lru_w_a[l], lru_b_a[l], lru_w_x[l], lru_b_x[l],
                              lru_lambda[l])
        wuq, wukv = _mla_weights(mla_w_uq[l], mla_w_ukv[l])

        p = _inproj(x, mods, norm1_g[l][None, :], _inproj_weights(w_in[l]), n_ctx, dec_seq)

        u_ctx = p[:n_ctx, C_S5:C_S5 + S5_CH].reshape(batch, seq_len, S5_CH)
        u_lat = p[n_ctx:, C_S5:C_S5 + S5_CH].reshape(dec_batch, dec_seq, S5_CH)
        ylin, s5_fins = _s5_core([u_ctx, u_lat], [None, state_s5[:, l]], s5_tab)

        qo, ko, vo, ckv = _mla_prep(p, mla_tab, mla_q_norm_g[l][None, :], mla_kv_norm_g[l][None, :], wuq, wukv)
        ckv_c = cache_mla_ckv[:, l].reshape(dec_batch * past, MLA_KV_RANK)
        kv_c = _matmul(ckv_c, wukv, out_dtype=bf).reshape(dec_batch * past, MLA_HEADS, 2, 128)
        kpe_c = jnp.broadcast_to(cache_mla_kpe[:, l].reshape(dec_batch * past, 1, MLA_ROPE).astype(bf),
                                 (dec_batch * past, MLA_HEADS, MLA_ROPE))
        k_c = jnp.concatenate([kv_c[:, :, 0], kpe_c, jnp.zeros_like(kpe_c)], axis=-1).reshape(
            dec_batch * past, MLA_HEADS * ATT_DQ)
        v_c = kv_c[:, :, 1].reshape(dec_batch * past, MLA_HEADS * MLA_V)
        ymla = (_attention(qo, ko, vo, 0, batch, seq_len),
                _attention(qo, ko, vo, n_ctx, dec_batch, dec_seq, k_c, v_c))

        ylru_c, lru_fin = _lru(p, 0, batch, seq_len, jnp.zeros((batch, 2, LRU_W), f32), lru_tab)
        ylru_l, _ = _lru(p, n_ctx, dec_batch, dec_seq, state_lru[:, l].astype(f32), lru_tab)
        ylru = (ylru_c, ylru_l)

        yret_c, ret_fin = _retention(p, 0, batch, seq_len,
                                     jnp.zeros((batch, 2, RET_HEADS, RET_DK, RET_DV), f32), ret_tables)
        yret_l, _ = _retention(p, n_ctx, dec_batch, dec_seq, state_ret[:, l].astype(f32), ret_tables, ret_rope)
        yret = (yret_c, yret_l)

        x_mid, h2, idx, wgt = _mixout(ylin, p, ymla, ylru, yret, x, mods, s5_d[l][None, :],
                                      s5_w_glu[l].astype(bf), w_out[l].astype(bf), norm2_g[l][None, :], rw,
                                      n_ctx, dec_seq)
        y2 = _moe_rows(h2, l, idx, wgt, moe_w_gate, moe_w_up, moe_w_down)
        x = _resid(x_mid, y2, mods, gf, n_ctx, dec_seq, final=(l == depth - 1))

        states.append((ckv[:n_ctx].reshape(batch, seq_len, MLA_KV_RANK),
                       p[:n_ctx, C_KR:C_KR + MLA_ROPE].astype(f32).reshape(batch, seq_len, MLA_ROPE),
                       s5_fins[0], lru_fin, ret_fin))

    y_prompt = x[0].reshape(batch, seq_len, d)
    y_sample = x[1].reshape(dec_batch, dec_seq, d)
    new_cache_mla_ckv = jnp.stack([st[0] for st in states], axis=1)
    new_cache_mla_kpe = jnp.stack([st[1] for st in states], axis=1)
    new_state_s5 = jnp.stack([st[2] for st in states], axis=1)
    new_state_lru = jnp.stack([st[3] for st in states], axis=1)
    new_state_ret = jnp.stack([st[4] for st in states], axis=1)
    return (y_prompt, y_sample, new_cache_mla_ckv, new_cache_mla_kpe, new_state_s5, new_state_lru, new_state_ret)
```

```python
import functools
import numpy as np
import jax
import jax.numpy as jnp
from jax import lax
from jax.experimental import pallas as pl
from jax.experimental.pallas import tpu as pltpu

D_MODEL = 2048
DEPTH = 2
GRID_W = 64
EPS = 1e-6
ROPE_BASE = 10000.0
N_MOD = 6
GROUP_W = 512
S5_CH = GROUP_W
S5_GROUP_CH = 16
S5_GROUPS = S5_CH // S5_GROUP_CH
S5_STATE = 64
MLA_HEADS = 4
MLA_NOPE = 128
MLA_ROPE = 64
MLA_V = 128
MLA_Q_RANK = GROUP_W
MLA_KV_RANK = GROUP_W // 2
MLA_SCALE = (MLA_NOPE + MLA_ROPE) ** -0.5
LRU_W = GROUP_W
LRU_CONV = 4
LRU_C = 8.0
RET_HEADS = 4
RET_DK = 128
RET_DV = 128
RET_CHUNK = 128
N_EXPERTS = 16
N_EXPERT_GROUPS = 4
EXPERTS_PER_GROUP = N_EXPERTS // N_EXPERT_GROUPS
TOP_K = 2
D_EXPERT = D_MODEL // 4

V7X_VMEM_LIMIT = 48 * 1024 * 1024
V7X_VMEM_LIMIT_BIG = 56 * 1024 * 1024

C_S5, C_MQ, C_LX, C_LG, C_RQ, C_RK, C_RV, C_RG, C_RQP, C_RKP, C_MKV, C_KR = (
    0, 512, 1024, 1536, 2048, 2560, 3072, 3584, 4096, 4608, 5120, 5376)
P_COLS = 5632
R_S5, R_MQ, R_MKV, R_KR, R_LX, R_LG, R_RQ, R_RK, R_RV, R_RG = (0, 512, 1024, 1280, 1344, 1856, 2368, 2880, 3392, 3904)


def _cparams(n_axes, limit=V7X_VMEM_LIMIT):
    return pltpu.CompilerParams(dimension_semantics=("arbitrary",) * n_axes, vmem_limit_bytes=limit)


def _sigmoid(x):
    return 1.0 / (1.0 + jnp.exp(-x))


def _gelu_tanh(x):
    return 0.5 * x * (1.0 + jnp.tanh(0.7978845608028654 * (x + 0.044715 * (x * x * x))))


def _rms(x):
    return x * lax.rsqrt(jnp.mean(x * x, axis=-1, keepdims=True) + EPS)


def _seq_of_tile(i, tm, n_ctx, dec_seq):
    assert n_ctx % tm == 0 and dec_seq % tm == 0, "a row tile must not straddle two sequences"
    r = i * tm
    return jnp.where(r < n_ctx, 0, 1 + (r - n_ctx) // dec_seq)


def _mm_kernel(x_ref, w_ref, o_ref):
    o_ref[...] = jnp.dot(x_ref[...].astype(jnp.bfloat16), w_ref[...].astype(jnp.bfloat16),
                         preferred_element_type=jnp.float32).astype(o_ref.dtype)


def _matmul(x, w, tm=512, tn=512, out_dtype=jnp.float32):
    m, k = x.shape
    _, n = w.shape
    tm, tn = min(tm, m), min(tn, n)
    return pl.pallas_call(
        _mm_kernel,
        grid=(n // tn, m // tm),
        in_specs=[pl.BlockSpec((tm, k), lambda j, i: (i, 0)), pl.BlockSpec((k, tn), lambda j, i: (0, j))],
        out_specs=pl.BlockSpec((tm, tn), lambda j, i: (i, j)),
        out_shape=jax.ShapeDtypeStruct((m, n), out_dtype),
        compiler_params=_cparams(2),
        name="matmul",
    )(x, w)


ADA_TN = 1024


def _ada_kernel(c_ref, w_ref, b_ref, o_ref):
    c = c_ref[...]
    s = (c * _sigmoid(c)).astype(jnp.bfloat16)
    o_ref[0] = jnp.dot(s, w_ref[0].astype(jnp.bfloat16), preferred_element_type=jnp.float32) + b_ref[0]


def _ada(cvec, w_ada, b_ada):
    depth, d, n = w_ada.shape
    return pl.pallas_call(
        _ada_kernel,
        grid=(depth, n // ADA_TN),
        in_specs=[pl.BlockSpec((8, d), lambda l, j: (0, 0)),
                  pl.BlockSpec((1, d, ADA_TN), lambda l, j: (l, 0, j)),
                  pl.BlockSpec((1, 1, ADA_TN), lambda l, j: (l, 0, j))],
        out_specs=pl.BlockSpec((1, 8, ADA_TN), lambda l, j: (l, 0, j)),
        out_shape=jax.ShapeDtypeStruct((depth, 8, n), jnp.float32),
        compiler_params=_cparams(2),
        name="adaln_mod",
    )(cvec, w_ada, b_ada.reshape(depth, 1, n))


IN_TM = 1024
IN_TN = 512


def _inproj_kernel(xc_ref, xl_ref, mod_ref, g_ref, w_ref, o_ref, h_s, *, n_ctx_tiles):
    @pl.when(pl.program_id(1) == 0)
    def _():
        x = jnp.where(pl.program_id(0) < n_ctx_tiles, xc_ref[...], xl_ref[...])
        h = _rms(x) * g_ref[...]
        h_s[...] = (h * (1.0 + mod_ref[0, 1:2, :]) + mod_ref[0, 0:1, :]).astype(h_s.dtype)

    o_ref[...] = jnp.dot(h_s[...], w_ref[...], preferred_element_type=jnp.float32).astype(o_ref.dtype)


def _two_part_rows(tm, width, n_ctx_tiles):
    return (pl.BlockSpec((tm, width), lambda i, *_: (jnp.minimum(i, n_ctx_tiles - 1), 0)),
            pl.BlockSpec((tm, width), lambda i, *_: (jnp.maximum(i - n_ctx_tiles, 0), 0)))


def _inproj(x, mods, g, w, n_ctx, dec_seq):
    n, d = x[0].shape[0] + x[1].shape[0], x[0].shape[1]
    na = n_ctx // IN_TM
    seq = functools.partial(_seq_of_tile, tm=IN_TM, n_ctx=n_ctx, dec_seq=dec_seq)
    return pl.pallas_call(
        functools.partial(_inproj_kernel, n_ctx_tiles=na),
        grid=(n // IN_TM, P_COLS // IN_TN),
        in_specs=[*_two_part_rows(IN_TM, d, na),
                  pl.BlockSpec((1, N_MOD, d), lambda i, j: (seq(i), 0, 0)),
                  pl.BlockSpec((1, d), lambda i, j: (0, 0)),
                  pl.BlockSpec((d, IN_TN), lambda i, j: (0, j))],
        out_specs=pl.BlockSpec((IN_TM, IN_TN), lambda i, j: (i, j)),
        out_shape=jax.ShapeDtypeStruct((n, P_COLS), jnp.bfloat16),
        scratch_shapes=[pltpu.VMEM((IN_TM, d), jnp.bfloat16)],
        compiler_params=_cparams(2, V7X_VMEM_LIMIT_BIG),
        name="norm_inproj",
    )(*x, mods, g, w)


def _rot_partner_cols(w, n_heads, head_dim):
    q = head_dim // 4
    wr = w.reshape(w.shape[0], n_heads, 2, 2, q)
    return jnp.stack([-wr[:, :, :, 1], wr[:, :, :, 0]], axis=3).reshape(w.shape)


def _inproj_weights(w_in):
    seg = lambda off, width: w_in[:, off:off + width]
    kr = seg(R_KR, MLA_ROPE)
    cols = [seg(R_S5, 2 * GROUP_W),
            seg(R_LX, 6 * GROUP_W),
            _rot_partner_cols(seg(R_RQ, 2 * GROUP_W), 2 * RET_HEADS, RET_DK),
            seg(R_MKV, MLA_KV_RANK), kr, _rot_partner_cols(kr, 1, MLA_ROPE),
            jnp.zeros((w_in.shape[0], P_COLS - C_KR - 2 * MLA_ROPE), w_in.dtype)]
    return jnp.concatenate(cols, axis=1).astype(jnp.bfloat16)


def _rope_tables(t_len, rot_dim):
    rows = t_len // GRID_W
    row = jnp.repeat(jnp.arange(rows, dtype=jnp.float32), GRID_W)
    col = jnp.tile(jnp.arange(GRID_W, dtype=jnp.float32), rows)
    n_freq = rot_dim // 4
    inv = ROPE_BASE ** (-jnp.arange(n_freq, dtype=jnp.float32) / n_freq)
    ang = jnp.concatenate([row[:, None] * inv[None]] * 2 + [col[:, None] * inv[None]] * 2, axis=1)
    return jnp.cos(ang), jnp.sin(ang)


S5_L = 32
S5_PAIRS = S5_GROUPS // 2
S5_Q = 4
S5_LANES = S5_GROUPS * S5_STATE
S5_SCAN_LANES = 1024


def _s5_tables(a_re, a_im, log_dt, b_re, b_im, c_re, c_im):
    f32 = jnp.float32
    L, G, P, C = S5_L, S5_GROUPS, S5_STATE, S5_GROUP_CH
    hp = lax.Precision.HIGHEST
    cmul = lambda xr, xi, yr, yi: (xr * yr - xi * yi, xr * yi + xi * yr)
    dt = jnp.exp(log_dt.astype(f32))[..., None]
    zr, zi = a_re * dt, a_im * dt
    ab_r, ab_i = jnp.exp(zr) * jnp.cos(zi), jnp.exp(zr) * jnp.sin(zi)
    den = a_re * a_re + a_im * a_im
    nr, ni = ab_r - 1.0, ab_i
    be_r = (nr * a_re + ni * a_im) / den
    be_i = (ni * a_re - nr * a_im) / den
    bt_r, bt_i = b_re.transpose(0, 2, 1), b_im.transpose(0, 2, 1)
    bp_r, bp_i = cmul(be_r[:, :, None, :], be_i[:, :, None, :], bt_r[None], bt_i[None])
    tau = jnp.arange(L + 1, dtype=f32)[None, None, :, None]
    mag = jnp.exp(zr[:, :, None, :] * tau)
    pw_r, pw_i = mag * jnp.cos(zi[:, :, None, :] * tau), mag * jnp.sin(zi[:, :, None, :] * tau)

    cat = lambda *xs: jnp.concatenate(xs, axis=-1)
    pf_r, pf_i = pw_r[0][:, :L][:, ::-1], pw_i[0][:, :L][:, ::-1]
    pb_r, pb_i = pw_r[1][:, :L], pw_i[1][:, :L]
    w1 = cat(pf_r, pf_i, pb_r, pb_i)[:, :, None, :]
    w2 = cat(pf_i, pf_r, pb_i, pb_r)[:, :, None, :]
    b1 = cat(bp_r[0], bp_r[0], bp_r[1], bp_r[1])[:, None]
    b2 = cat(-bp_i[0], bp_i[0], -bp_i[1], bp_i[1])[:, None]
    ws = (w1 * b1 + w2 * b2).reshape(G, L * C, S5_Q * P)

    ct_r, ct_i = c_re.transpose(0, 2, 1), c_im.transpose(0, 2, 1)
    pt_r, pt_i = pw_r.transpose(0, 1, 3, 2), pw_i.transpose(0, 1, 3, 2)

    def cm(er, ei):
        m_r, m_i = cmul(ct_r[:, :, None, :], ct_i[:, :, None, :], er[..., None], ei[..., None])
        return m_r.reshape(G, P, L * C), m_i.reshape(G, P, L * C)

    mf_r, mf_i = cm(pt_r[0][:, :, 1:L + 1], pt_i[0][:, :, 1:L + 1])
    mb_r, mb_i = cm(pt_r[1][:, :, 1:L + 1][:, :, ::-1], pt_i[1][:, :, 1:L + 1][:, :, ::-1])
    wo = jnp.stack([mf_r, -mf_i, mb_r, -mb_i], axis=1).reshape(G, S5_Q * P, L * C)

    def impulse(d):
        m_r, m_i = cm(pt_r[d][:, :, :L], pt_i[d][:, :, :L])
        kk = (jnp.einsum('gkp,gpx->gkx', bp_r[d], m_r, precision=hp)
              - jnp.einsum('gkp,gpx->gkx', bp_i[d], m_i, precision=hp))
        return kk.reshape(G, C, L, C)

    kf, kb = impulse(0), impulse(1)
    wide = jnp.concatenate([kb[:, :, :0:-1], kf[:, :, :1] + kb[:, :, :1], kf[:, :, 1:]], axis=2)
    wide = jnp.pad(wide.reshape(G, C, (2 * L - 1) * C), ((0, 0), (0, 0), (0, C)))

    a_l = jnp.stack([pw_r[0, :, L], pw_i[0, :, L], pw_r[1, :, L], pw_i[1, :, L]], axis=0).reshape(S5_Q, 1, S5_LANES)
    bf = jnp.bfloat16
    return _s5_toeplitz(wide), ws.astype(bf), wo.astype(bf), a_l


def _s5_toeplitz_kernel(w_ref, o_ref):
    w = w_ref[0]
    for s in range(S5_L):
        off = (S5_L - 1 - s) * S5_GROUP_CH
        o_ref[0, s * S5_GROUP_CH:(s + 1) * S5_GROUP_CH, :] = w[:, off:off + S5_L * S5_GROUP_CH].astype(o_ref.dtype)


def _s5_toeplitz(wide):
    g, c, wl = wide.shape
    n = S5_L * S5_GROUP_CH
    return pl.pallas_call(
        _s5_toeplitz_kernel,
        grid=(g,),
        in_specs=[pl.BlockSpec((1, c, wl), lambda i: (i, 0, 0))],
        out_specs=pl.BlockSpec((1, n, n), lambda i: (i, 0, 0)),
        out_shape=jax.ShapeDtypeStruct((g, n, n), jnp.bfloat16),
        compiler_params=_cparams(1),
        name="s5_toeplitz",
    )(wide)


def _s5a_kernel(x_ref, t_ref, ws_ref, y1_ref, s_ref):
    ss = []
    for j in range(2):
        x = x_ref[j]
        y1_ref[j] = jnp.dot(x, t_ref[j], preferred_element_type=jnp.float32)
        ss.append(jnp.dot(x, ws_ref[j], preferred_element_type=jnp.float32))
    p = S5_STATE
    for q in range(S5_Q):
        s_ref[q] = jnp.concatenate([ss[0][:, q * p:(q + 1) * p], ss[1][:, q * p:(q + 1) * p]], axis=1)


def _s5a(x, toe, ws):
    g, r, w = x.shape
    return pl.pallas_call(
        _s5a_kernel,
        grid=(S5_PAIRS,),
        in_specs=[pl.BlockSpec((2, r, w), lambda i: (i, 0, 0)),
                  pl.BlockSpec((2, w, w), lambda i: (i, 0, 0)),
                  pl.BlockSpec((2, w, S5_Q * S5_STATE), lambda i: (i, 0, 0))],
        out_specs=[pl.BlockSpec((2, r, w), lambda i: (i, 0, 0)),
                   pl.BlockSpec((S5_Q, r, 128), lambda i: (0, 0, i))],
        out_shape=[jax.ShapeDtypeStruct((g, r, w), jnp.float32),
                   jax.ShapeDtypeStruct((S5_Q, r, S5_LANES), jnp.float32)],
        compiler_params=_cparams(1),
        name="s5_chunk_local",
    )(x, toe, ws)


def _s5b_kernel(s_ref, a_ref, h0_ref, hp_ref, fin_ref, *, nc):
    def run(qr, qi, order_fwd):
        ar, ai = a_ref[qr], a_ref[qi]

        def body(i, carry):
            hr, hi = carry
            k = i if order_fwd else nc - 1 - i
            hp_ref[qr, pl.ds(k, 1), :] = hr
            hp_ref[qi, pl.ds(k, 1), :] = hi
            nr = ar * hr - ai * hi + s_ref[qr, pl.ds(k, 1), :]
            ni = ar * hi + ai * hr + s_ref[qi, pl.ds(k, 1), :]
            return nr, ni

        hr, hi = lax.fori_loop(0, nc, body, (h0_ref[qr], h0_ref[qi]))
        fin_ref[qr] = hr
        fin_ref[qi] = hi

    run(0, 1, True)
    run(2, 3, False)


def _s5b(s, a_l, h0):
    _, nc, w = s.shape
    tl = S5_SCAN_LANES
    vec = pl.BlockSpec((S5_Q, 1, tl), lambda i: (0, 0, i))
    seq = pl.BlockSpec((S5_Q, nc, tl), lambda i: (0, 0, i))
    return pl.pallas_call(
        functools.partial(_s5b_kernel, nc=nc),
        grid=(w // tl,),
        in_specs=[seq, vec, vec],
        out_specs=[seq, vec],
        out_shape=[jax.ShapeDtypeStruct((S5_Q, nc, w), jnp.float32),
                   jax.ShapeDtypeStruct((S5_Q, 1, w), jnp.float32)],
        compiler_params=_cparams(1),
        name="s5_chunk_scan",
    )(s, a_l, h0)


def _s5c_kernel(h_ref, wo_ref, y1_ref, y_ref):
    p = S5_STATE
    for j in range(2):
        hcat = jnp.concatenate([h_ref[q][:, j * p:(j + 1) * p] for q in range(S5_Q)], axis=1)
        y2 = jnp.dot(hcat.astype(jnp.bfloat16), wo_ref[j], preferred_element_type=jnp.float32)
        y_ref[j] = (y1_ref[j] + y2).astype(y_ref.dtype)


def _s5c(hprev, wo, y1):
    g, r, w = y1.shape
    return pl.pallas_call(
        _s5c_kernel,
        grid=(S5_PAIRS,),
        in_specs=[pl.BlockSpec((S5_Q, r, 128), lambda i: (0, 0, i)),
                  pl.BlockSpec((2, S5_Q * S5_STATE, w), lambda i: (i, 0, 0)),
                  pl.BlockSpec((2, r, w), lambda i: (i, 0, 0))],
        out_specs=pl.BlockSpec((2, r, w), lambda i: (i, 0, 0)),
        out_shape=jax.ShapeDtypeStruct((g, r, w), jnp.bfloat16),
        compiler_params=_cparams(1),
        name="s5_state_to_out",
    )(hprev, wo, y1)


def _s5_to_chunks(u):
    b, t, _ = u.shape
    nc = t // S5_L
    x = u.reshape(b, nc, S5_L, S5_GROUPS, S5_GROUP_CH).transpose(3, 1, 0, 2, 4)
    return x.reshape(S5_GROUPS, nc * b, S5_L * S5_GROUP_CH)


def _s5_from_chunks(y, b, t):
    nc = t // S5_L
    y = y.reshape(S5_GROUPS, nc, b, S5_L, S5_GROUP_CH).transpose(2, 1, 3, 0, 4)
    return y.reshape(b * t, S5_CH)


def _s5_state_planes(h0):
    b = h0.shape[0]
    return h0.transpose(1, 4, 0, 2, 3).reshape(S5_Q, b, S5_LANES)


def _s5_core(us, h0s, tables):
    toe, ws_pair, wo_pair, a_l = tables
    xs = [_s5_to_chunks(u) for u in us]
    rows = [x.shape[1] for x in xs]
    y1, s = _s5a(jnp.concatenate(xs, axis=1), toe, ws_pair)
    hps, fins = [], []
    off = 0
    for u, h0, r in zip(us, h0s, rows):
        b, t, _ = u.shape
        nc = t // S5_L
        if h0 is None:
            h0p = jnp.zeros((S5_Q, 1, b * S5_LANES), jnp.float32)
        else:
            h0p = _s5_state_planes(h0.astype(jnp.float32)).reshape(S5_Q, 1, b * S5_LANES)
        hp, fin = _s5b(s[:, off:off + r].reshape(S5_Q, nc, b * S5_LANES), jnp.tile(a_l, (1, 1, b)), h0p)
        hps.append(hp.reshape(S5_Q, r, S5_LANES))
        fins.append(fin.reshape(2, 2, b, S5_GROUPS, S5_STATE).transpose(2, 0, 3, 4, 1))
        off += r
    y = _s5c(jnp.concatenate(hps, axis=1), wo_pair, y1)
    outs, off = [], 0
    for u, r in zip(us, rows):
        b, t, _ = u.shape
        outs.append(_s5_from_chunks(y[:, off:off + r], b, t))
        off += r
    return jnp.concatenate(outs, axis=0), fins


LRU_TC = 256
LRU_HALO = 16


def _lru_kernel(x_ref, gate_ref, cw_ref, cb_ref, wg_ref, bg_ref, sp_ref, h0_ref, out_ref, fin_ref,
                a_s, b_s, hf_s, *, t_len, tc):
    f32 = jnp.float32
    nt = t_len // tc
    w = LRU_W
    nb = tc // 8
    row = lax.broadcasted_iota(jnp.int32, (nb, 8, w), 1)

    def gates(c, d):
        r0 = pl.multiple_of(c * tc, tc)
        lo = pl.multiple_of(jnp.maximum(r0 - LRU_HALO, 0), LRU_HALO)
        hi = pl.multiple_of(jnp.minimum(r0 + tc, t_len - LRU_HALO), LRU_HALO)
        prev = jnp.where(c > 0, x_ref[pl.ds(lo, LRU_HALO), :].astype(f32), 0.0)
        nxt = jnp.where(c < nt - 1, x_ref[pl.ds(hi, LRU_HALO), :].astype(f32), 0.0)
        slab = jnp.concatenate([prev, x_ref[pl.ds(r0, tc), :].astype(f32), nxt], axis=0)
        o = LRU_HALO - LRU_CONV // 2
        xc = cb_ref[...] + sum(cw_ref[k:k + 1, :] * slab[o + k:o + k + tc] for k in range(LRU_CONV))
        g = jnp.dot(xc.astype(jnp.bfloat16), wg_ref[d], preferred_element_type=f32) + bg_ref[d]
        r = _sigmoid(g[:, :w])
        i = _sigmoid(g[:, w:])
        log_a = -sp_ref[d] * r
        a = jnp.exp(log_a)
        b = (jnp.sqrt(1.0 - a * a) * (i * xc)).reshape(nb, 8, w)
        a = a.reshape(nb, 8, w)
        for sh in (1, 2, 4):
            keep, rot = (row >= sh, sh) if d == 0 else (row < 8 - sh, 8 - sh)
            ap = jnp.where(keep, pltpu.roll(a, rot, 1), 1.0)
            bp = jnp.where(keep, pltpu.roll(b, rot, 1), 0.0)
            b = a * bp + b
            a = a * ap
        a_s[...] = a
        b_s[...] = b
        return r0

    def fwd_chunk(c, h):
        r0 = gates(c, 0)

        def block(k, h):
            h8 = a_s[k] * h + b_s[k]
            hf_s[pl.ds(pl.multiple_of(r0 + k * 8, 8), 8), :] = h8
            return h8[7:8, :]

        return lax.fori_loop(0, nb, block, h, unroll=4)

    h = lax.fori_loop(0, nt, fwd_chunk, h0_ref[0, 0:1, :])
    fin_ref[0, 0:1, :] = h

    def bwd_chunk(ci, h):
        r0 = gates(nt - 1 - ci, 1)

        def block(i, h):
            k = nb - 1 - i
            h8 = a_s[k] * h + b_s[k]
            rows = pl.ds(pl.multiple_of(r0 + k * 8, 8), 8)
            hf_s[rows, :] = hf_s[rows, :] + h8
            return h8[0:1, :]

        h = lax.fori_loop(0, nb, block, h, unroll=4)
        sl = pl.ds(r0, tc)
        out_ref[sl, :] = (hf_s[sl, :] * _gelu_tanh(gate_ref[sl, :].astype(f32))).astype(out_ref.dtype)
        return h

    h = lax.fori_loop(0, nt, bwd_chunk, h0_ref[0, 1:2, :])
    fin_ref[0, 1:2, :] = h


def _block_diag(wb):
    n, k, j = wb.shape
    return (wb[:, :, None, :] * jnp.eye(n, dtype=wb.dtype)[:, None, :, None]).reshape(n * k, n * j)


def _lru_tables(conv_w, conv_b, w_a, b_a, w_x, b_x, lam):
    wg = jnp.stack([jnp.concatenate([_block_diag(w_a[d]), _block_diag(w_x[d])], axis=1) for d in range(2)])
    bg = jnp.concatenate([b_a, b_x], axis=-1)[:, None, :]
    sp = (LRU_C * jax.nn.softplus(-lam.astype(jnp.float32)))[:, None, :]
    return conv_w, conv_b[None, :], wg.astype(jnp.bfloat16), bg, sp


def _lru(p, row0, b, t, h0, tables):
    w = LRU_W
    tc = min(LRU_TC, t)
    cw, cb, wg, bg, sp = tables
    full = lambda a: pl.BlockSpec(a.shape, lambda i: (0,) * a.ndim)
    rb = row0 // t
    return pl.pallas_call(
        functools.partial(_lru_kernel, t_len=t, tc=tc),
        grid=(b,),
        in_specs=[pl.BlockSpec((t, w), lambda i: (rb + i, C_LX // w)),
                  pl.BlockSpec((t, w), lambda i: (rb + i, C_LG // w)),
                  full(cw), full(cb), full(wg), full(bg), full(sp),
                  pl.BlockSpec((1, 2, w), lambda i: (i, 0, 0))],
        out_specs=[pl.BlockSpec((t, w), lambda i: (i, 0)),
                   pl.BlockSpec((1, 2, w), lambda i: (i, 0, 0))],
        out_shape=[jax.ShapeDtypeStruct((b * t, w), jnp.bfloat16),
                   jax.ShapeDtypeStruct((b, 2, w), jnp.float32)],
        scratch_shapes=[pltpu.VMEM((tc // 8, 8, w), jnp.float32), pltpu.VMEM((tc // 8, 8, w), jnp.float32),
                        pltpu.VMEM((t, w), jnp.float32)],
        compiler_params=_cparams(1),
        name="rglru",
    )(p, p, cw, cb, wg, bg, sp, h0)


MLA_TM = 512
ATT_DQ = 256


def _mla_prep_kernel(q_ref, kv_ref, kr_ref, tab_ref, gq_ref, gkv_ref, wuq_ref, wukv_ref,
                     qo_ref, ko_ref, vo_ref, ckv_ref):
    f32, bf = jnp.float32, jnp.bfloat16
    tab = tab_ref[...]

    def rope(blk):
        prod = blk * tab
        return prod + pltpu.roll(prod, 64, 1)

    qn = (_rms(q_ref[...].astype(f32)) * gq_ref[...]).astype(bf)
    qq = jnp.dot(qn, wuq_ref[...], preferred_element_type=f32)
    ckv = _rms(kv_ref[...].astype(f32)) * gkv_ref[...]
    ckv_ref[...] = ckv
    kk = jnp.dot(ckv.astype(bf), wukv_ref[...], preferred_element_type=f32)
    lane = lax.broadcasted_iota(jnp.int32, tab.shape, 1)
    kpe = jnp.where(lane < MLA_ROPE, rope(kr_ref[...].astype(f32)), 0.0).astype(bf)
    for h in range(MLA_HEADS):
        o = h * ATT_DQ
        qo_ref[:, o:o + 128] = (qq[:, o:o + 128] * MLA_SCALE).astype(bf)
        qo_ref[:, o + 128:o + 256] = (rope(qq[:, o + 128:o + 256]) * MLA_SCALE).astype(bf)
        ko_ref[:, o:o + 128] = kk[:, o:o + 128].astype(bf)
        ko_ref[:, o + 128:o + 256] = kpe
        vo_ref[:, h * MLA_V:(h + 1) * MLA_V] = kk[:, o + 128:o + 256].astype(bf)


def _mla_weights(w_uq, w_ukv):
    wq = w_uq.reshape(MLA_Q_RANK, MLA_HEADS, MLA_NOPE + MLA_ROPE)
    pe = wq[:, :, MLA_NOPE:].reshape(MLA_Q_RANK, MLA_HEADS * MLA_ROPE)
    pep = _rot_partner_cols(pe, MLA_HEADS, MLA_ROPE).reshape(MLA_Q_RANK, MLA_HEADS, MLA_ROPE)
    wq_ext = jnp.concatenate([wq, pep], axis=-1).reshape(MLA_Q_RANK, MLA_HEADS * ATT_DQ)
    return wq_ext.astype(jnp.bfloat16), w_ukv.astype(jnp.bfloat16)


def _mla_prep(p, tab, gq, gkv, wuq, wukv):
    n = p.shape[0]
    tm = MLA_TM
    full = lambda a: pl.BlockSpec(a.shape, lambda i: (0,) * a.ndim)
    row = lambda width, col: pl.BlockSpec((tm, width), lambda i: (i, col // width))
    return pl.pallas_call(
        _mla_prep_kernel,
        grid=(n // tm,),
        in_specs=[row(MLA_Q_RANK, C_MQ), row(MLA_KV_RANK, C_MKV), row(128, C_KR), row(128, 0),
                  full(gq), full(gkv), full(wuq), full(wukv)],
        out_specs=[row(MLA_HEADS * ATT_DQ, 0), row(MLA_HEADS * ATT_DQ, 0), row(MLA_HEADS * MLA_V, 0),
                   row(MLA_KV_RANK, 0)],
        out_shape=[jax.ShapeDtypeStruct((n, MLA_HEADS * ATT_DQ), jnp.bfloat16),
                   jax.ShapeDtypeStruct((n, MLA_HEADS * ATT_DQ), jnp.bfloat16),
                   jax.ShapeDtypeStruct((n, MLA_HEADS * MLA_V), jnp.bfloat16),
                   jax.ShapeDtypeStruct((n, MLA_KV_RANK), jnp.float32)],
        compiler_params=_cparams(1),
        name="mla_prep",
    )(p, p, p, tab, gq, gkv, wuq, wukv)


ATT_TQ = 512
ATT_SUB = 256


def _attn_kernel(*refs, two, n_sub):
    dn = (((1,), (1,)), ((), ()))
    f32 = jnp.float32
    if two:
        q_ref, k_ref, v_ref, k2_ref, v2_ref, o_ref = refs
    else:
        q_ref, k_ref, v_ref, o_ref = refs
    rows = q_ref.shape[0] // n_sub
    for part in range(n_sub):
        sl = slice(part * rows, (part + 1) * rows)
        q = q_ref[sl, :]
        s = lax.dot_general(q, k_ref[...], dn, preferred_element_type=f32)
        m = jnp.max(s, axis=-1, keepdims=True)
        if two:
            s2 = lax.dot_general(q, k2_ref[...], dn, preferred_element_type=f32)
            m = jnp.maximum(m, jnp.max(s2, axis=-1, keepdims=True))
        p = jnp.exp(s - m)
        l = jnp.sum(p, axis=-1, keepdims=True)
        o = jnp.dot(p.astype(jnp.bfloat16), v_ref[...], preferred_element_type=f32)
        if two:
            p2 = jnp.exp(s2 - m)
            l = l + jnp.sum(p2, axis=-1, keepdims=True)
            o = o + jnp.dot(p2.astype(jnp.bfloat16), v2_ref[...], preferred_element_type=f32)
        o_ref[sl, :] = (o / l).astype(o_ref.dtype)


def _attention(q, k, v, row0, b, t, k2=None, v2=None):
    tb = min(ATT_TQ, t)
    nq = t // tb
    qb, kb = row0 // tb, row0 // t
    two = k2 is not None
    in_specs = [pl.BlockSpec((tb, ATT_DQ), lambda bi, h, i: (qb + bi * nq + i, h)),
                pl.BlockSpec((t, ATT_DQ), lambda bi, h, i: (kb + bi, h)),
                pl.BlockSpec((t, MLA_V), lambda bi, h, i: (kb + bi, h))]
    args = [q, k, v]
    if two:
        t2 = k2.shape[0] // b
        in_specs += [pl.BlockSpec((t2, ATT_DQ), lambda bi, h, i: (bi, h)),
                     pl.BlockSpec((t2, MLA_V), lambda bi, h, i: (bi, h))]
        args += [k2, v2]
    return pl.pallas_call(
        functools.partial(_attn_kernel, two=two, n_sub=max(tb // ATT_SUB, 1)),
        grid=(b, MLA_HEADS, nq),
        in_specs=in_specs,
        out_specs=pl.BlockSpec((tb, MLA_V), lambda bi, h, i: (bi * nq + i, h)),
        out_shape=jax.ShapeDtypeStruct((b * t, MLA_HEADS * MLA_V), jnp.bfloat16),
        compiler_params=_cparams(3),
        name="mla_attention",
    )(*args)


def _ret_tables():
    f32 = jnp.float32
    c = RET_CHUNK
    log_g = jnp.log1p(-jnp.exp2(-5.0 - jnp.arange(RET_HEADS, dtype=f32)))[:, None, None]
    idx = jnp.arange(c, dtype=f32)
    dec = jnp.exp(jnp.abs(idx[:, None] - idx[None, :])[None] * log_g)
    row = lambda e: jnp.broadcast_to(jnp.exp(e[None, :, None] * log_g), (RET_HEADS, c, c))
    return jnp.stack([dec, row(idx + 1.0), row(c - idx), row(c - 1.0 - idx), row(idx)], axis=0)


RET_HPS = 2


def _ret_kernel(*refs, nc, rope):
    f32, bf = jnp.float32, jnp.bfloat16
    if rope:
        (q_ref, k_ref, v_ref, g_ref, qp_ref, kp_ref, cos_ref, sin_ref, tab_ref, s0_ref, o_ref, fin_ref,
         acc_s, st_s) = refs
    else:
        q_ref, k_ref, v_ref, g_ref, tab_ref, s0_ref, o_ref, fin_ref, acc_s, st_s = refs
    c, dk = RET_CHUNK, RET_DK
    nt_dims, tn_dims = (((1,), (1,)), ((), ())), (((0,), (0,)), ((), ()))

    def chunk(k0, hh):
        sl = pl.ds(pl.multiple_of(k0 * c, c), c)
        cols = slice(hh * dk, (hh + 1) * dk)
        qc, kc = q_ref[sl, cols].astype(f32), k_ref[sl, cols].astype(f32)
        if rope:
            cos, sin = cos_ref[sl, :], sin_ref[sl, :]
            qc = qc * cos + qp_ref[sl, cols].astype(f32) * sin
            kc = kc * cos + kp_ref[sl, cols].astype(f32) * sin
        return sl, cols, qc.astype(bf), kc * (RET_DK ** -0.5), v_ref[sl, cols]

    st_s[...] = s0_ref[0]
    acc_s[...] = jnp.zeros_like(acc_s)

    def body(i, carry):
        for hh in range(RET_HPS):
            dec, xif, xib, zf, zb = (tab_ref[t, hh] for t in range(5))
            g_chunk = xif[c - 1:c, :]
            sl, cols, qc, kc, vc = chunk(i, hh)
            s = st_s[0, hh]
            sc = lax.dot_general(qc, kc.astype(bf), nt_dims, preferred_element_type=f32) * dec
            acc_s[sl, cols] += (jnp.dot(sc.astype(bf), vc, preferred_element_type=f32)
                                + jnp.dot(qc, s.astype(bf), preferred_element_type=f32) * xif)
            st_s[0, hh] = g_chunk * s + lax.dot_general((kc * zf).astype(bf), vc, tn_dims,
                                                        preferred_element_type=f32)
            sl, cols, qc, kc, vc = chunk(nc - 1 - i, hh)
            s = st_s[1, hh]
            acc_s[sl, cols] += jnp.dot(qc, s.astype(bf), preferred_element_type=f32) * xib
            st_s[1, hh] = g_chunk * s + lax.dot_general((kc * zb).astype(bf), vc, tn_dims,
                                                        preferred_element_type=f32)
        return carry

    lax.fori_loop(0, nc, body, 0)
    fin_ref[0] = st_s[...]

    def finish(i, carry):
        sl = pl.ds(pl.multiple_of(i * c, c), c)
        for hh in range(RET_HPS):
            cols = slice(hh * dk, (hh + 1) * dk)
            g = g_ref[sl, cols].astype(f32)
            o_ref[sl, cols] = (_rms(acc_s[sl, cols]) * (g * _sigmoid(g))).astype(o_ref.dtype)
        return carry

    lax.fori_loop(0, nc, finish, 0)


def _retention(p, row0, b, t, s0, tables, rope_tabs=None):
    nc = t // RET_CHUNK
    rb = row0 // t
    rope = rope_tabs is not None
    wd = RET_HPS * RET_DK
    col = lambda c0: pl.BlockSpec((t, wd), lambda bi, h: (rb + bi, c0 // wd + h))
    st = pl.BlockSpec((1, 2, RET_HPS, RET_DK, RET_DV), lambda bi, h: (bi, 0, h, 0, 0))
    in_specs = [col(C_RQ), col(C_RK), col(C_RV), col(C_RG)]
    args = [p, p, p, p]
    if rope:
        tab = pl.BlockSpec((t, RET_DK), lambda bi, h: (0, 0))
        in_specs += [col(C_RQP), col(C_RKP), tab, tab]
        args += [p, p, rope_tabs[0], rope_tabs[1]]
    in_specs += [pl.BlockSpec((5, RET_HPS, RET_CHUNK, RET_CHUNK), lambda bi, h: (0, h, 0, 0)), st]
    args += [tables, s0]
    return pl.pallas_call(
        functools.partial(_ret_kernel, nc=nc, rope=rope),
        grid=(b, RET_HEADS // RET_HPS),
        in_specs=in_specs,
        out_specs=[pl.BlockSpec((t, wd), lambda bi, h: (bi, h)), st],
        out_shape=[jax.ShapeDtypeStruct((b * t, RET_HEADS * RET_DV), jnp.bfloat16),
                   jax.ShapeDtypeStruct((b, 2, RET_HEADS, RET_DK, RET_DV), jnp.float32)],
        scratch_shapes=[pltpu.VMEM((t, wd), jnp.float32), pltpu.VMEM((2, RET_HPS, RET_DK, RET_DV), jnp.float32)],
        compiler_params=_cparams(2, V7X_VMEM_LIMIT_BIG),
        name="retention",
    )(*args)


OUT_TM = 512
ROUTER_PAD = 128


ROW_TILE = (8, D_MODEL // 16)


def _pack_halves(x):
    w = x.shape[1] // 2
    bits = lambda a: lax.bitcast_convert_type(a.astype(jnp.bfloat16).astype(jnp.float32), jnp.int32)
    return bits(x[:, w:]) | lax.shift_right_logical(bits(x[:, :w]), 16)


def _unpack_halves(u):
    lo = lax.bitcast_convert_type(lax.shift_left(u, 16), jnp.float32)
    hi = lax.bitcast_convert_type(u & jnp.int32(-65536), jnp.float32)
    return jnp.concatenate([lo, hi], axis=1)


def _split_bf16(x):
    hi = x.astype(jnp.bfloat16)
    return hi, (x - hi.astype(jnp.float32)).astype(jnp.bfloat16)


def _route(h, whi_ref, wlo_ref, rb_ref):
    f32 = jnp.float32
    hi, lo = _split_bf16(h)
    dn = (((1,), (1,)), ((), ()))
    lt = (lax.dot_general(whi_ref[...], hi, dn, preferred_element_type=f32)
          + lax.dot_general(whi_ref[...], lo, dn, preferred_element_type=f32)
          + lax.dot_general(wlo_ref[...], hi, dn, preferred_element_type=f32))[:N_EXPERTS]
    m = jnp.max(lt, axis=0, keepdims=True)
    e = jnp.exp(lt - m)
    sc = e / jnp.sum(e, axis=0, keepdims=True)
    sel = sc + rb_ref[...][:N_EXPERTS, 0:1]
    rows = lambda a: [a[j:j + 1, :] for j in range(N_EXPERTS)]
    sel_r, sc_r = rows(sel), rows(sc)
    epg = EXPERTS_PER_GROUP

    def top2sum(a, b, c, d):
        h1, l1, h2, l2 = jnp.maximum(a, b), jnp.minimum(a, b), jnp.maximum(c, d), jnp.minimum(c, d)
        return jnp.maximum(h1, h2) + jnp.maximum(jnp.minimum(h1, h2), jnp.maximum(l1, l2))

    gs = [top2sum(*sel_r[g * epg:(g + 1) * epg]) for g in range(N_EXPERT_GROUPS)]
    best, gi = gs[0], jnp.zeros_like(gs[0], dtype=jnp.int32)
    for g in range(1, N_EXPERT_GROUPS):
        upd = gs[g] > best
        gi = jnp.where(upd, g, gi)
        best = jnp.where(upd, gs[g], best)

    def pick(r, j):
        out = r[j]
        for g in range(1, N_EXPERT_GROUPS):
            out = jnp.where(gi == g, r[g * epg + j], out)
        return out

    v = [pick(sel_r, j) for j in range(epg)]
    s = [pick(sc_r, j) for j in range(epg)]

    def argmax_first(vals):
        bv, bi = vals[0], jnp.zeros_like(gi)
        for j in range(1, epg):
            upd = vals[j] > bv
            bi = jnp.where(upd, j, bi)
            bv = jnp.where(upd, vals[j], bv)
        return bi

    i1 = argmax_first(v)
    neg = jnp.float32(-jnp.inf)
    i2 = argmax_first([jnp.where(i1 == j, neg, v[j]) for j in range(epg)])
    take = lambda i: sum(jnp.where(i == j, s[j], 0.0) for j in range(epg))
    w1, w2 = take(i1), take(i2)
    tot = w1 + w2
    return (jnp.concatenate([gi * epg + i1, gi * epg + i2], axis=0),
            jnp.concatenate([w1 / tot, w2 / tot], axis=0))


def _mixout_kernel(ylin_ref, u_ref, ymla_c, ymla_l, ylru_c, ylru_l, yret_c, yret_l, x_c, x_l, mod_ref, d_ref,
                   wglu_ref, wout_ref, g2n_ref, whi_ref, wlo_ref, rb_ref, xo_ref, h2_ref, idx_ref, wgt_ref,
                   *, n_ctx_tiles):
    f32, bf = jnp.float32, jnp.bfloat16
    w = GROUP_W
    is_ctx = pl.program_id(0) < n_ctx_tiles
    pick = lambda c_ref, l_ref: jnp.where(is_ctx, c_ref[...], l_ref[...])
    y = _gelu_tanh(ylin_ref[...].astype(f32) + d_ref[...] * u_ref[...].astype(f32))
    y5 = (y * _sigmoid(jnp.dot(y.astype(bf), wglu_ref[...], preferred_element_type=f32))).astype(bf)
    mix = (jnp.dot(y5, wout_ref[0:w, :], preferred_element_type=f32)
           + jnp.dot(pick(ymla_c, ymla_l), wout_ref[w:2 * w, :], preferred_element_type=f32)
           + jnp.dot(pick(ylru_c, ylru_l), wout_ref[2 * w:3 * w, :], preferred_element_type=f32)
           + jnp.dot(pick(yret_c, yret_l), wout_ref[3 * w:4 * w, :], preferred_element_type=f32))
    x = pick(x_c, x_l) + mod_ref[0, 2:3, :] * mix
    xo_ref[...] = x
    h2 = _rms(x) * g2n_ref[...] * (1.0 + mod_ref[0, 4:5, :]) + mod_ref[0, 3:4, :]
    h2_ref[...] = _pack_halves(h2).reshape(h2_ref.shape)
    idx, wgt = _route(h2, whi_ref, wlo_ref, rb_ref)
    idx_ref[...] = idx
    wgt_ref[...] = wgt


def _router_weights(router_w, router_b):
    d = router_w.shape[0]
    wt = jnp.zeros((ROUTER_PAD, d), jnp.float32).at[:N_EXPERTS].set(router_w.T.astype(jnp.float32))
    whi, wlo = _split_bf16(wt)
    rb = jnp.zeros((ROUTER_PAD, 128), jnp.float32).at[:N_EXPERTS].set(router_b.astype(jnp.float32)[:, None])
    return whi, wlo, rb


def _mixout(ylin, p, ymla, ylru, yret, x, mods, s5_d, wglu, wout, g2n, rw, n_ctx, dec_seq):
    n, d = x[0].shape[0] + x[1].shape[0], x[0].shape[1]
    tm = OUT_TM
    w = GROUP_W
    na = n_ctx // tm
    seq = functools.partial(_seq_of_tile, tm=tm, n_ctx=n_ctx, dec_seq=dec_seq)
    full = lambda a: pl.BlockSpec(a.shape, lambda i: (0,) * a.ndim)
    row = lambda width: pl.BlockSpec((tm, width), lambda i: (i, 0))
    ctx_row, lat_row = _two_part_rows(tm, w, na)
    lanes = pl.BlockSpec((TOP_K, tm), lambda i: (0, i))
    whi, wlo, rb = rw
    return pl.pallas_call(
        functools.partial(_mixout_kernel, n_ctx_tiles=na),
        grid=(n // tm,),
        in_specs=[row(w), row(w), ctx_row, lat_row, ctx_row, lat_row, ctx_row, lat_row, *_two_part_rows(tm, d, na),
                  pl.BlockSpec((1, N_MOD, d), lambda i: (seq(i), 0, 0)),
                  full(s5_d), full(wglu), full(wout), full(g2n), full(whi), full(wlo), full(rb)],
        out_specs=[row(d), pl.BlockSpec((tm,) + ROW_TILE, lambda i: (i, 0, 0)), lanes, lanes],
        out_shape=[jax.ShapeDtypeStruct((n, d), jnp.float32), jax.ShapeDtypeStruct((n,) + ROW_TILE, jnp.int32),
                   jax.ShapeDtypeStruct((TOP_K, n), jnp.int32), jax.ShapeDtypeStruct((TOP_K, n), jnp.float32)],
        compiler_params=_cparams(1, V7X_VMEM_LIMIT_BIG),
        name="mix_out_norm_route",
    )(ylin, p, *ymla, *ylru, *yret, *x, mods, s5_d, wglu, wout, g2n, whi, wlo, rb)


MOE_TM = 512
MOE_NB = 256


def _expert_kernel(te_ref, nt_ref, nv_ref, tok0_ref, tok1_ref, w_ref, dstp_ref, dst_ref, wg_ref, wu_ref, wd_ref,
                   h_hbm, y_hbm, wg_s, wu_s, wd_s, xbuf, xs_s, act_s, ybuf, gsem, ssem):
    i = pl.program_id(0)
    bf = jnp.bfloat16
    tm = xs_s.shape[0]
    nt = nt_ref[0]

    def gather_start(tok_ref, sl, unroll):
        def row(r, carry):
            pltpu.make_async_copy(h_hbm.at[tok_ref[0, 0, r]], xbuf.at[sl, r], gsem.at[sl]).start()
            return carry

        lax.fori_loop(0, tm, row, 0, unroll=unroll)

    def gather_done(sl):
        pltpu.make_async_copy(h_hbm.at[pl.ds(0, tm)], xbuf.at[sl], gsem.at[sl]).wait()

    def scatter_row(d_ref, sl, r):
        return pltpu.make_async_copy(ybuf.at[sl, r], y_hbm.at[d_ref[0, 0, r]], ssem.at[sl])

    def scatter_done(sl, n_rows):
        pltpu.make_async_copy(ybuf.at[sl, pl.ds(0, n_rows)], y_hbm.at[pl.ds(0, n_rows)], ssem.at[sl]).wait()

    def tile(slot):
        n_prev = jnp.where(i > 0, nv_ref[jnp.maximum(i - 1, 0)], 0)
        f = wg_s.shape[1]
        d = wd_s.shape[1]
        up_pieces, down_pieces = 2 * (f // MOE_NB), d // MOE_NB

        def gather_rows(piece):
            for r in range(piece * tm // up_pieces, (piece + 1) * tm // up_pieces):
                pltpu.make_async_copy(h_hbm.at[tok1_ref[0, 0, r]], xbuf.at[1 - slot, r],
                                      gsem.at[1 - slot]).start(priority=r % 2)

        def scatter_rows(piece):
            for r in range(piece * tm // down_pieces, (piece + 1) * tm // down_pieces):
                @pl.when(r < n_prev)
                def _():
                    scatter_row(dstp_ref, 1 - slot, r).start(priority=r % 2)

        gather_done(slot)
        xs_s[...] = _unpack_halves(xbuf[slot].reshape(tm, -1)).astype(bf)
        x = xs_s[...]
        gs, us = [], []
        for c in range(f // MOE_NB):
            cols = slice(c * MOE_NB, (c + 1) * MOE_NB)
            gs.append(jnp.dot(x, wg_s[:, cols], preferred_element_type=jnp.float32))
            gather_rows(2 * c)
            us.append(jnp.dot(x, wu_s[:, cols], preferred_element_type=jnp.float32))
            gather_rows(2 * c + 1)
        g = jnp.concatenate(gs, axis=1)
        u = jnp.concatenate(us, axis=1)
        act_s[...] = ((g * _sigmoid(g)) * u * w_ref[...]).astype(bf)

        @pl.when(nt > 0)
        def _():
            act = act_s[...]
            ys = []
            for c in range(down_pieces):
                ys.append(jnp.dot(act, wd_s[:, c * MOE_NB:(c + 1) * MOE_NB], preferred_element_type=jnp.float32))
                scatter_rows(c)
            y = _pack_halves(jnp.concatenate(ys, axis=1)).reshape(ybuf.shape[1:])

            @pl.when(i > 1)
            def _():
                scatter_done(slot, nv_ref[jnp.maximum(i - 2, 0)])

            ybuf[slot] = y

        @pl.when(i == nt - 1)
        def _():
            def row(r, carry):
                scatter_row(dst_ref, slot, r).start()
                return carry

            lax.fori_loop(0, nv_ref[i], row, 0)

            @pl.when(i > 0)
            def _():
                scatter_done(1 - slot, nv_ref[jnp.maximum(i - 1, 0)])

            scatter_done(slot, nv_ref[i])
            gather_done(1 - slot)

    @pl.when(i < nt)
    def _():
        @pl.when(i == 0)
        def _():
            gather_start(tok0_ref, 0, 8)

        @pl.when((i == 0) | (te_ref[i] != te_ref[jnp.maximum(i - 1, 0)]))
        def _():
            wg_s[...] = wg_ref[0, 0].astype(bf)
            wu_s[...] = wu_ref[0, 0].astype(bf)
            wd_s[...] = wd_ref[0, 0].astype(bf)

        @pl.when(i % 2 == 0)
        def _():
            tile(0)

        @pl.when(i % 2 == 1)
        def _():
            tile(1)


def _experts(h2p, layer, ws, tok, dst, tile_expert, n_tiles_used, n_valid, wg, wu, wd):
    n = h2p.shape[0]
    d = 2 * ROW_TILE[0] * ROW_TILE[1]
    tm = MOE_TM
    n_tiles = tok.shape[0]
    f = wg.shape[-1]
    smem_row = lambda delta: pl.BlockSpec(
        (1, 1, tm), lambda i, te, nt, nv: (jnp.clip(i + delta, 0, n_tiles - 1), 0, 0), memory_space=pltpu.SMEM)
    wspec = lambda a, b: pl.BlockSpec((1, 1, a, b), lambda i, te, nt, nv: (layer, te[i], 0, 0))
    return pl.pallas_call(
        _expert_kernel,
        grid_spec=pltpu.PrefetchScalarGridSpec(
            num_scalar_prefetch=3,
            grid=(n_tiles,),
            in_specs=[smem_row(0), smem_row(1),
                      pl.BlockSpec((tm, 1), lambda i, te, nt, nv: (i, 0)),
                      smem_row(-1), smem_row(0),
                      wspec(d, f), wspec(d, f), wspec(f, d),
                      pl.BlockSpec(memory_space=pl.ANY)],
            out_specs=pl.BlockSpec(memory_space=pl.ANY),
            scratch_shapes=[pltpu.VMEM((d, f), jnp.bfloat16), pltpu.VMEM((d, f), jnp.bfloat16),
                            pltpu.VMEM((f, d), jnp.bfloat16), pltpu.VMEM((2, tm) + ROW_TILE, jnp.int32),
                            pltpu.VMEM((tm, d), jnp.bfloat16), pltpu.VMEM((tm, f), jnp.bfloat16),
                            pltpu.VMEM((2, tm) + ROW_TILE, jnp.int32),
                            pltpu.SemaphoreType.DMA((2,)), pltpu.SemaphoreType.DMA((2,))]),
        out_shape=jax.ShapeDtypeStruct((TOP_K * n,) + ROW_TILE, jnp.int32),
        compiler_params=pltpu.CompilerParams(dimension_semantics=("arbitrary",),
                                             vmem_limit_bytes=V7X_VMEM_LIMIT_BIG, disable_bounds_checks=True),
        name="moe_experts",
    )(tile_expert, n_tiles_used, n_valid, tok, tok, ws, dst, dst, wg, wu, wd, h2p)


def _moe_dispatch(idx, wgt):
    n = idx.shape[1]
    tm = MOE_TM
    n_pairs = TOP_K * n
    m_pad = n_pairs + N_EXPERTS * tm
    n_tiles = m_pad // tm
    e_flat = idx.reshape(-1)
    order = jnp.argsort(e_flat, stable=True).astype(jnp.int32)
    experts = jnp.arange(N_EXPERTS, dtype=jnp.int32)
    counts = jnp.sum((e_flat[None, :] == experts[:, None]).astype(jnp.int32), axis=1)
    starts_raw = jnp.cumsum(counts) - counts
    padded = ((counts + tm - 1) // tm) * tm
    ends = jnp.cumsum(padded)
    starts_pad = ends - padded
    tile_start = jnp.arange(n_tiles, dtype=jnp.int32) * tm
    tile_expert = jnp.minimum(jnp.sum((tile_start[:, None] >= ends[None, :]).astype(jnp.int32), axis=1),
                              N_EXPERTS - 1)
    r = jnp.arange(m_pad, dtype=jnp.int32)
    te_r = jnp.repeat(tile_expert, tm)
    off = r - jnp.take(starts_pad, te_r)
    valid = (off >= 0) & (off < jnp.take(counts, te_r))
    pair = jnp.take(order, jnp.clip(jnp.take(starts_raw, te_r) + off, 0, n_pairs - 1))
    tok = jnp.where(valid, pair % n, 0).reshape(n_tiles, 1, tm)
    dst = jnp.where(valid, pair, 0).reshape(n_tiles, 1, tm)
    ws = jnp.where(valid, jnp.take(wgt.reshape(-1), pair), 0.0)[:, None]
    n_tiles_used = (ends[-1] // tm).astype(jnp.int32).reshape(1)
    n_valid = jnp.sum(valid.reshape(n_tiles, tm), axis=1).astype(jnp.int32)
    return tok, dst, ws, tile_expert, n_tiles_used, n_valid


def _moe_rows(h2p, layer, idx, wgt, wg, wu, wd):
    tok, dst, ws, tile_expert, n_tiles_used, n_valid = _moe_dispatch(idx, wgt)
    return _experts(h2p, layer, ws, tok, dst, tile_expert, n_tiles_used, n_valid, wg, wu, wd)


RES_TM = 512


def _resid_kernel(x_ref, ya_ref, yb_ref, mod_ref, gf_ref, oc_ref, ol_ref, *, final, n_ctx_tiles):
    rows = lambda r: _unpack_halves(r[...].reshape(r.shape[0], -1))
    x = x_ref[...] + mod_ref[0, 5:6, :] * (rows(ya_ref) + rows(yb_ref))
    y = _rms(x) * gf_ref[...] if final else x
    i = pl.program_id(0)

    @pl.when(i < n_ctx_tiles)
    def _():
        oc_ref[...] = y

    @pl.when(i >= n_ctx_tiles)
    def _():
        ol_ref[...] = y


def _resid(x, y2, mods, gf, n_ctx, dec_seq, final):
    n, d = x.shape
    tm = RES_TM
    na = n_ctx // tm
    seq = functools.partial(_seq_of_tile, tm=tm, n_ctx=n_ctx, dec_seq=dec_seq)
    row = pl.BlockSpec((tm, d), lambda i: (i, 0))
    return pl.pallas_call(
        functools.partial(_resid_kernel, final=final, n_ctx_tiles=na),
        grid=(n // tm,),
        in_specs=[row, pl.BlockSpec((tm,) + ROW_TILE, lambda i: (i, 0, 0)),
                  pl.BlockSpec((tm,) + ROW_TILE, lambda i: (n // tm + i, 0, 0)),
                  pl.BlockSpec((1, N_MOD, d), lambda i: (seq(i), 0, 0)),
                  pl.BlockSpec((1, d), lambda i: (0, 0))],
        out_specs=list(_two_part_rows(tm, d, na)),
        out_shape=[jax.ShapeDtypeStruct((n_ctx, d), jnp.float32), jax.ShapeDtypeStruct((n - n_ctx, d), jnp.float32)],
        compiler_params=_cparams(1),
        name="moe_residual_norm",
    )(x, y2, y2, mods, gf)


def kernel(x_prompt, x_sample, c, cache_mla_ckv, cache_mla_kpe, state_s5, state_lru, state_ret,
           c_ctx, w_ada, b_ada, norm1_g, norm2_g, w_in, w_out,
           s5_a_re, s5_a_im, s5_log_dt, s5_b_re, s5_b_im, s5_c_re, s5_c_im, s5_d, s5_w_glu,
           mla_q_norm_g, mla_w_uq, mla_kv_norm_g, mla_w_ukv,
           lru_conv_w, lru_conv_b, lru_w_a, lru_b_a, lru_w_x, lru_b_x, lru_lambda,
           router_w, router_b, moe_w_gate, moe_w_up, moe_w_down, final_norm_g):
    f32, bf = jnp.float32, jnp.bfloat16
    batch, seq_len, d = x_prompt.shape
    dec_batch, dec_seq, _ = x_sample.shape
    past = cache_mla_ckv.shape[2]
    n_ctx, n_lat = batch * seq_len, dec_batch * dec_seq
    depth = w_in.shape[0]

    x = (x_prompt.reshape(n_ctx, d), x_sample.reshape(n_lat, d))
    cvec = jnp.zeros((8, d), f32).at[0].set(c_ctx).at[1:1 + dec_batch].set(c)
    mods_all = _ada(cvec, w_ada, b_ada).reshape(depth, 8, N_MOD, d)

    cos64, sin64 = _rope_tables(dec_seq, MLA_ROPE)
    lat_tab = jnp.tile(jnp.concatenate([cos64, sin64], axis=1), (dec_batch, 1))
    ctx_tab = jnp.concatenate([jnp.ones((n_ctx, MLA_ROPE), f32), jnp.zeros((n_ctx, MLA_ROPE), f32)], axis=1)
    mla_tab = jnp.concatenate([ctx_tab, lat_tab], axis=0)
    ret_rope = _rope_tables(dec_seq, RET_DK)
    ret_tables = _ret_tables()
    rw = _router_weights(router_w, router_b)
    gf = final_norm_g[None, :]

    states = []
    for l in range(depth):
        mods = mods_all[l]
        s5_tab = _s5_tables(s5_a_re[l], s5_a_im[l], s5_log_dt[l], s5_b_re[l], s5_b_im[l], s5_c_re[l], s5_c_im[l])
        lru_tab = _lru_tables(lru_conv_w[l], lru_conv_b[l], lru_w_a[l], lru_b_a[l], lru_w_x[l], lru_b_x[l],
                              lru_lambda[l])
        wuq, wukv = _mla_weights(mla_w_uq[l], mla_w_ukv[l])

        p = _inproj(x, mods, norm1_g[l][None, :], _inproj_weights(w_in[l]), n_ctx, dec_seq)

        u_ctx = p[:n_ctx, C_S5:C_S5 + S5_CH].reshape(batch, seq_len, S5_CH)
        u_lat = p[n_ctx:, C_S5:C_S5 + S5_CH].reshape(dec_batch, dec_seq, S5_CH)
        ylin, s5_fins = _s5_core([u_ctx, u_lat], [None, state_s5[:, l]], s5_tab)

        qo, ko, vo, ckv = _mla_prep(p, mla_tab, mla_q_norm_g[l][None, :], mla_kv_norm_g[l][None, :], wuq, wukv)
        ckv_c = cache_mla_ckv[:, l].reshape(dec_batch * past, MLA_KV_RANK)
        kv_c = _matmul(ckv_c, wukv, out_dtype=bf).reshape(dec_batch * past, MLA_HEADS, 2, 128)
        kpe_c = jnp.broadcast_to(cache_mla_kpe[:, l].reshape(dec_batch * past, 1, MLA_ROPE).astype(bf),
                                 (dec_batch * past, MLA_HEADS, MLA_ROPE))
        k_c = jnp.concatenate([kv_c[:, :, 0], kpe_c, jnp.zeros_like(kpe_c)], axis=-1).reshape(
            dec_batch * past, MLA_HEADS * ATT_DQ)
        v_c = kv_c[:, :, 1].reshape(dec_batch * past, MLA_HEADS * MLA_V)
        ymla = (_attention(qo, ko, vo, 0, batch, seq_len),
                _attention(qo, ko, vo, n_ctx, dec_batch, dec_seq, k_c, v_c))

        ylru_c, lru_fin = _lru(p, 0, batch, seq_len, jnp.zeros((batch, 2, LRU_W), f32), lru_tab)
        ylru_l, _ = _lru(p, n_ctx, dec_batch, dec_seq, state_lru[:, l].astype(f32), lru_tab)
        ylru = (ylru_c, ylru_l)

        yret_c, ret_fin = _retention(p, 0, batch, seq_len,
                                     jnp.zeros((batch, 2, RET_HEADS, RET_DK, RET_DV), f32), ret_tables)
        yret_l, _ = _retention(p, n_ctx, dec_batch, dec_seq, state_ret[:, l].astype(f32), ret_tables, ret_rope)
        yret = (yret_c, yret_l)

        x_mid, h2, idx, wgt = _mixout(ylin, p, ymla, ylru, yret, x, mods, s5_d[l][None, :],
                                      s5_w_glu[l].astype(bf), w_out[l].astype(bf), norm2_g[l][None, :], rw,
                                      n_ctx, dec_seq)
        y2 = _moe_rows(h2, l, idx, wgt, moe_w_gate, moe_w_up, moe_w_down)
        x = _resid(x_mid, y2, mods, gf, n_ctx, dec_seq, final=(l == depth - 1))

        states.append((ckv[:n_ctx].reshape(batch, seq_len, MLA_KV_RANK),
                       p[:n_ctx, C_KR:C_KR + MLA_ROPE].astype(f32).reshape(batch, seq_len, MLA_ROPE),
                       s5_fins[0], lru_fin, ret_fin))

    y_prompt = x[0].reshape(batch, seq_len, d)
    y_sample = x[1].reshape(dec_batch, dec_seq, d)
    new_cache_mla_ckv = jnp.stack([st[0] for st in states], axis=1)
    new_cache_mla_kpe = jnp.stack([st[1] for st in states], axis=1)
    new_state_s5 = jnp.stack([st[2] for st in states], axis=1)
    new_state_lru = jnp.stack([st[3] for st in states], axis=1)
    new_state_ret = jnp.stack([st[4] for st in states], axis=1)
    return (y_prompt, y_sample, new_cache_mla_ckv, new_cache_mla_kpe, new_state_s5, new_state_lru, new_state_ret)
```

```python
import functools
import numpy as np
import jax
import jax.numpy as jnp
from jax import lax
from jax.experimental import pallas as pl
from jax.experimental.pallas import tpu as pltpu

D_MODEL = 2048
DEPTH = 2
GRID_W = 64
EPS = 1e-6
ROPE_BASE = 10000.0
N_MOD = 6
GROUP_W = 512
S5_CH = GROUP_W
S5_GROUP_CH = 16
S5_GROUPS = S5_CH // S5_GROUP_CH
S5_STATE = 64
MLA_HEADS = 4
MLA_NOPE = 128
MLA_ROPE = 64
MLA_V = 128
MLA_Q_RANK = GROUP_W
MLA_KV_RANK = GROUP_W // 2
MLA_SCALE = (MLA_NOPE + MLA_ROPE) ** -0.5
LRU_W = GROUP_W
LRU_CONV = 4
LRU_C = 8.0
RET_HEADS = 4
RET_DK = 128
RET_DV = 128
RET_CHUNK = 128
N_EXPERTS = 16
N_EXPERT_GROUPS = 4
EXPERTS_PER_GROUP = N_EXPERTS // N_EXPERT_GROUPS
TOP_K = 2
D_EXPERT = D_MODEL // 4

V7X_VMEM_LIMIT = 48 * 1024 * 1024
V7X_VMEM_LIMIT_BIG = 56 * 1024 * 1024

C_S5, C_MQ, C_LX, C_LG, C_RQ, C_RK, C_RV, C_RG, C_RQP, C_RKP, C_MKV, C_KR = (
    0, 512, 1024, 1536, 2048, 2560, 3072, 3584, 4096, 4608, 5120, 5376)
P_COLS = 5632
R_S5, R_MQ, R_MKV, R_KR, R_LX, R_LG, R_RQ, R_RK, R_RV, R_RG = (0, 512, 1024, 1280, 1344, 1856, 2368, 2880, 3392, 3904)


def _cparams(n_axes, limit=V7X_VMEM_LIMIT):
    return pltpu.CompilerParams(dimension_semantics=("arbitrary",) * n_axes, vmem_limit_bytes=limit)


def _sigmoid(x):
    return 1.0 / (1.0 + jnp.exp(-x))


def _gelu_tanh(x):
    return 0.5 * x * (1.0 + jnp.tanh(0.7978845608028654 * (x + 0.044715 * (x * x * x))))


def _rms(x):
    return x * lax.rsqrt(jnp.mean(x * x, axis=-1, keepdims=True) + EPS)


def _seq_of_tile(i, tm, n_ctx, dec_seq):
    assert n_ctx % tm == 0 and dec_seq % tm == 0, "a row tile must not straddle two sequences"
    r = i * tm
    return jnp.where(r < n_ctx, 0, 1 + (r - n_ctx) // dec_seq)


def _mm_kernel(x_ref, w_ref, o_ref):
    o_ref[...] = jnp.dot(x_ref[...].astype(jnp.bfloat16), w_ref[...].astype(jnp.bfloat16),
                         preferred_element_type=jnp.float32).astype(o_ref.dtype)


def _matmul(x, w, tm=512, tn=512, out_dtype=jnp.float32):
    m, k = x.shape
    _, n = w.shape
    tm, tn = min(tm, m), min(tn, n)
    return pl.pallas_call(
        _mm_kernel,
        grid=(n // tn, m // tm),
        in_specs=[pl.BlockSpec((tm, k), lambda j, i: (i, 0)), pl.BlockSpec((k, tn), lambda j, i: (0, j))],
        out_specs=pl.BlockSpec((tm, tn), lambda j, i: (i, j)),
        out_shape=jax.ShapeDtypeStruct((m, n), out_dtype),
        compiler_params=_cparams(2),
        name="matmul",
    )(x, w)


ADA_TN = 1024


def _ada_kernel(c_ref, w_ref, b_ref, o_ref):
    c = c_ref[...]
    s = (c * _sigmoid(c)).astype(jnp.bfloat16)
    o_ref[0] = jnp.dot(s, w_ref[0].astype(jnp.bfloat16), preferred_element_type=jnp.float32) + b_ref[0]


def _ada(cvec, w_ada, b_ada):
    depth, d, n = w_ada.shape
    return pl.pallas_call(
        _ada_kernel,
        grid=(depth, n // ADA_TN),
        in_specs=[pl.BlockSpec((8, d), lambda l, j: (0, 0)),
                  pl.BlockSpec((1, d, ADA_TN), lambda l, j: (l, 0, j)),
                  pl.BlockSpec((1, 1, ADA_TN), lambda l, j: (l, 0, j))],
        out_specs=pl.BlockSpec((1, 8, ADA_TN), lambda l, j: (l, 0, j)),
        out_shape=jax.ShapeDtypeStruct((depth, 8, n), jnp.float32),
        compiler_params=_cparams(2),
        name="adaln_mod",
    )(cvec, w_ada, b_ada.reshape(depth, 1, n))


IN_TM = 1024
IN_TN = 512


def _inproj_kernel(xc_ref, xl_ref, mod_ref, g_ref, w_ref, o_ref, h_s, *, n_ctx_tiles):
    @pl.when(pl.program_id(1) == 0)
    def _():
        x = jnp.where(pl.program_id(0) < n_ctx_tiles, xc_ref[...], xl_ref[...])
        h = _rms(x) * g_ref[...]
        h_s[...] = (h * (1.0 + mod_ref[0, 1:2, :]) + mod_ref[0, 0:1, :]).astype(h_s.dtype)

    j = pl.program_id(1)
    unused = (pl.program_id(0) < n_ctx_tiles) & (j >= C_RQP // IN_TN) & (j < C_MKV // IN_TN)

    @pl.when(unused)
    def _():
        o_ref[...] = jnp.zeros_like(o_ref)

    @pl.when(jnp.logical_not(unused))
    def _():
        o_ref[...] = jnp.dot(h_s[...], w_ref[...], preferred_element_type=jnp.float32).astype(o_ref.dtype)


def _two_part_rows(tm, width, n_ctx_tiles):
    return (pl.BlockSpec((tm, width), lambda i, *_: (jnp.minimum(i, n_ctx_tiles - 1), 0)),
            pl.BlockSpec((tm, width), lambda i, *_: (jnp.maximum(i - n_ctx_tiles, 0), 0)))


def _inproj(x, mods, g, w, n_ctx, dec_seq):
    n, d = x[0].shape[0] + x[1].shape[0], x[0].shape[1]
    na = n_ctx // IN_TM
    seq = functools.partial(_seq_of_tile, tm=IN_TM, n_ctx=n_ctx, dec_seq=dec_seq)
    return pl.pallas_call(
        functools.partial(_inproj_kernel, n_ctx_tiles=na),
        grid=(n // IN_TM, P_COLS // IN_TN),
        in_specs=[*_two_part_rows(IN_TM, d, na),
                  pl.BlockSpec((1, N_MOD, d), lambda i, j: (seq(i), 0, 0)),
                  pl.BlockSpec((1, d), lambda i, j: (0, 0)),
                  pl.BlockSpec((d, IN_TN), lambda i, j: (0, j))],
        out_specs=pl.BlockSpec((IN_TM, IN_TN), lambda i, j: (i, j)),
        out_shape=jax.ShapeDtypeStruct((n, P_COLS), jnp.bfloat16),
        scratch_shapes=[pltpu.VMEM((IN_TM, d), jnp.bfloat16)],
        compiler_params=_cparams(2, V7X_VMEM_LIMIT_BIG),
        name="norm_inproj",
    )(*x, mods, g, w)


def _rot_partner_cols(w, n_heads, head_dim):
    q = head_dim // 4
    wr = w.reshape(w.shape[0], n_heads, 2, 2, q)
    return jnp.stack([-wr[:, :, :, 1], wr[:, :, :, 0]], axis=3).reshape(w.shape)


def _inproj_weights(w_in):
    seg = lambda off, width: w_in[:, off:off + width]
    kr = seg(R_KR, MLA_ROPE)
    cols = [seg(R_S5, 2 * GROUP_W),
            seg(R_LX, 6 * GROUP_W),
            _rot_partner_cols(seg(R_RQ, 2 * GROUP_W), 2 * RET_HEADS, RET_DK),
            seg(R_MKV, MLA_KV_RANK), kr, _rot_partner_cols(kr, 1, MLA_ROPE),
            jnp.zeros((w_in.shape[0], P_COLS - C_KR - 2 * MLA_ROPE), w_in.dtype)]
    return jnp.concatenate(cols, axis=1).astype(jnp.bfloat16)


def _rope_tables(t_len, rot_dim):
    rows = t_len // GRID_W
    row = jnp.repeat(jnp.arange(rows, dtype=jnp.float32), GRID_W)
    col = jnp.tile(jnp.arange(GRID_W, dtype=jnp.float32), rows)
    n_freq = rot_dim // 4
    inv = ROPE_BASE ** (-jnp.arange(n_freq, dtype=jnp.float32) / n_freq)
    ang = jnp.concatenate([row[:, None] * inv[None]] * 2 + [col[:, None] * inv[None]] * 2, axis=1)
    return jnp.cos(ang), jnp.sin(ang)


S5_L = 32
S5_PAIRS = S5_GROUPS // 2
S5_Q = 4
S5_LANES = S5_GROUPS * S5_STATE
S5_SCAN_LANES = 1024


def _s5_tables(a_re, a_im, log_dt, b_re, b_im, c_re, c_im):
    f32 = jnp.float32
    L, G, P, C = S5_L, S5_GROUPS, S5_STATE, S5_GROUP_CH
    hp = lax.Precision.HIGHEST
    cmul = lambda xr, xi, yr, yi: (xr * yr - xi * yi, xr * yi + xi * yr)
    dt = jnp.exp(log_dt.astype(f32))[..., None]
    zr, zi = a_re * dt, a_im * dt
    ab_r, ab_i = jnp.exp(zr) * jnp.cos(zi), jnp.exp(zr) * jnp.sin(zi)
    den = a_re * a_re + a_im * a_im
    nr, ni = ab_r - 1.0, ab_i
    be_r = (nr * a_re + ni * a_im) / den
    be_i = (ni * a_re - nr * a_im) / den
    bt_r, bt_i = b_re.transpose(0, 2, 1), b_im.transpose(0, 2, 1)
    bp_r, bp_i = cmul(be_r[:, :, None, :], be_i[:, :, None, :], bt_r[None], bt_i[None])
    tau = jnp.arange(L + 1, dtype=f32)[None, None, :, None]
    mag = jnp.exp(zr[:, :, None, :] * tau)
    pw_r, pw_i = mag * jnp.cos(zi[:, :, None, :] * tau), mag * jnp.sin(zi[:, :, None, :] * tau)

    cat = lambda *xs: jnp.concatenate(xs, axis=-1)
    pf_r, pf_i = pw_r[0][:, :L][:, ::-1], pw_i[0][:, :L][:, ::-1]
    pb_r, pb_i = pw_r[1][:, :L], pw_i[1][:, :L]
    w1 = cat(pf_r, pf_i, pb_r, pb_i)[:, :, None, :]
    w2 = cat(pf_i, pf_r, pb_i, pb_r)[:, :, None, :]
    b1 = cat(bp_r[0], bp_r[0], bp_r[1], bp_r[1])[:, None]
    b2 = cat(-bp_i[0], bp_i[0], -bp_i[1], bp_i[1])[:, None]
    ws = (w1 * b1 + w2 * b2).reshape(G, L * C, S5_Q * P)

    ct_r, ct_i = c_re.transpose(0, 2, 1), c_im.transpose(0, 2, 1)
    pt_r, pt_i = pw_r.transpose(0, 1, 3, 2), pw_i.transpose(0, 1, 3, 2)

    def cm(er, ei):
        m_r, m_i = cmul(ct_r[:, :, None, :], ct_i[:, :, None, :], er[..., None], ei[..., None])
        return m_r.reshape(G, P, L * C), m_i.reshape(G, P, L * C)

    mf_r, mf_i = cm(pt_r[0][:, :, 1:L + 1], pt_i[0][:, :, 1:L + 1])
    mb_r, mb_i = cm(pt_r[1][:, :, 1:L + 1][:, :, ::-1], pt_i[1][:, :, 1:L + 1][:, :, ::-1])
    wo = jnp.stack([mf_r, -mf_i, mb_r, -mb_i], axis=1).reshape(G, S5_Q * P, L * C)

    def impulse(d):
        m_r, m_i = cm(pt_r[d][:, :, :L], pt_i[d][:, :, :L])
        kk = (jnp.einsum('gkp,gpx->gkx', bp_r[d], m_r, precision=hp)
              - jnp.einsum('gkp,gpx->gkx', bp_i[d], m_i, precision=hp))
        return kk.reshape(G, C, L, C)

    kf, kb = impulse(0), impulse(1)
    wide = jnp.concatenate([kb[:, :, :0:-1], kf[:, :, :1] + kb[:, :, :1], kf[:, :, 1:]], axis=2)
    wide = jnp.pad(wide.reshape(G, C, (2 * L - 1) * C), ((0, 0), (0, 0), (0, C)))

    a_l = jnp.stack([pw_r[0, :, L], pw_i[0, :, L], pw_r[1, :, L], pw_i[1, :, L]], axis=0).reshape(S5_Q, 1, S5_LANES)
    bf = jnp.bfloat16
    return _s5_toeplitz(wide), ws.astype(bf), wo.astype(bf), a_l


def _s5_toeplitz_kernel(w_ref, o_ref):
    w = w_ref[0]
    for s in range(S5_L):
        off = (S5_L - 1 - s) * S5_GROUP_CH
        o_ref[0, s * S5_GROUP_CH:(s + 1) * S5_GROUP_CH, :] = w[:, off:off + S5_L * S5_GROUP_CH].astype(o_ref.dtype)


def _s5_toeplitz(wide):
    g, c, wl = wide.shape
    n = S5_L * S5_GROUP_CH
    return pl.pallas_call(
        _s5_toeplitz_kernel,
        grid=(g,),
        in_specs=[pl.BlockSpec((1, c, wl), lambda i: (i, 0, 0))],
        out_specs=pl.BlockSpec((1, n, n), lambda i: (i, 0, 0)),
        out_shape=jax.ShapeDtypeStruct((g, n, n), jnp.bfloat16),
        compiler_params=_cparams(1),
        name="s5_toeplitz",
    )(wide)


def _s5a_kernel(x_ref, t_ref, ws_ref, y1_ref, s_ref):
    ss = []
    for j in range(2):
        x = x_ref[j]
        y1_ref[j] = jnp.dot(x, t_ref[j], preferred_element_type=jnp.float32)
        ss.append(jnp.dot(x, ws_ref[j], preferred_element_type=jnp.float32))
    p = S5_STATE
    for q in range(S5_Q):
        s_ref[q] = jnp.concatenate([ss[0][:, q * p:(q + 1) * p], ss[1][:, q * p:(q + 1) * p]], axis=1)


def _s5a(x, toe, ws):
    g, r, w = x.shape
    return pl.pallas_call(
        _s5a_kernel,
        grid=(S5_PAIRS,),
        in_specs=[pl.BlockSpec((2, r, w), lambda i: (i, 0, 0)),
                  pl.BlockSpec((2, w, w), lambda i: (i, 0, 0)),
                  pl.BlockSpec((2, w, S5_Q * S5_STATE), lambda i: (i, 0, 0))],
        out_specs=[pl.BlockSpec((2, r, w), lambda i: (i, 0, 0)),
                   pl.BlockSpec((S5_Q, r, 128), lambda i: (0, 0, i))],
        out_shape=[jax.ShapeDtypeStruct((g, r, w), jnp.float32),
                   jax.ShapeDtypeStruct((S5_Q, r, S5_LANES), jnp.float32)],
        compiler_params=_cparams(1),
        name="s5_chunk_local",
    )(x, toe, ws)


def _s5b_kernel(s_ref, a_ref, h0_ref, hp_ref, fin_ref, *, nc):
    def run(qr, qi, order_fwd):
        ar, ai = a_ref[qr], a_ref[qi]

        def body(i, carry):
            hr, hi = carry
            k = i if order_fwd else nc - 1 - i
            hp_ref[qr, pl.ds(k, 1), :] = hr
            hp_ref[qi, pl.ds(k, 1), :] = hi
            nr = ar * hr - ai * hi + s_ref[qr, pl.ds(k, 1), :]
            ni = ar * hi + ai * hr + s_ref[qi, pl.ds(k, 1), :]
            return nr, ni

        hr, hi = lax.fori_loop(0, nc, body, (h0_ref[qr], h0_ref[qi]))
        fin_ref[qr] = hr
        fin_ref[qi] = hi

    run(0, 1, True)
    run(2, 3, False)


def _s5b(s, a_l, h0):
    _, nc, w = s.shape
    tl = S5_SCAN_LANES
    vec = pl.BlockSpec((S5_Q, 1, tl), lambda i: (0, 0, i))
    seq = pl.BlockSpec((S5_Q, nc, tl), lambda i: (0, 0, i))
    return pl.pallas_call(
        functools.partial(_s5b_kernel, nc=nc),
        grid=(w // tl,),
        in_specs=[seq, vec, vec],
        out_specs=[seq, vec],
        out_shape=[jax.ShapeDtypeStruct((S5_Q, nc, w), jnp.float32),
                   jax.ShapeDtypeStruct((S5_Q, 1, w), jnp.float32)],
        compiler_params=_cparams(1),
        name="s5_chunk_scan",
    )(s, a_l, h0)


def _s5c_kernel(h_ref, wo_ref, y1_ref, y_ref):
    p = S5_STATE
    for j in range(2):
        hcat = jnp.concatenate([h_ref[q][:, j * p:(j + 1) * p] for q in range(S5_Q)], axis=1)
        y2 = jnp.dot(hcat.astype(jnp.bfloat16), wo_ref[j], preferred_element_type=jnp.float32)
        y_ref[j] = (y1_ref[j] + y2).astype(y_ref.dtype)


def _s5c(hprev, wo, y1):
    g, r, w = y1.shape
    return pl.pallas_call(
        _s5c_kernel,
        grid=(S5_PAIRS,),
        in_specs=[pl.BlockSpec((S5_Q, r, 128), lambda i: (0, 0, i)),
                  pl.BlockSpec((2, S5_Q * S5_STATE, w), lambda i: (i, 0, 0)),
                  pl.BlockSpec((2, r, w), lambda i: (i, 0, 0))],
        out_specs=pl.BlockSpec((2, r, w), lambda i: (i, 0, 0)),
        out_shape=jax.ShapeDtypeStruct((g, r, w), jnp.bfloat16),
        compiler_params=_cparams(1),
        name="s5_state_to_out",
    )(hprev, wo, y1)


def _s5_to_chunks(u):
    b, t, _ = u.shape
    nc = t // S5_L
    x = u.reshape(b, nc, S5_L, S5_GROUPS, S5_GROUP_CH).transpose(3, 1, 0, 2, 4)
    return x.reshape(S5_GROUPS, nc * b, S5_L * S5_GROUP_CH)


def _s5_from_chunks(y, b, t):
    nc = t // S5_L
    y = y.reshape(S5_GROUPS, nc, b, S5_L, S5_GROUP_CH).transpose(2, 1, 3, 0, 4)
    return y.reshape(b * t, S5_CH)


def _s5_state_planes(h0):
    b = h0.shape[0]
    return h0.transpose(1, 4, 0, 2, 3).reshape(S5_Q, b, S5_LANES)


def _s5_core(us, h0s, tables):
    toe, ws_pair, wo_pair, a_l = tables
    xs = [_s5_to_chunks(u) for u in us]
    rows = [x.shape[1] for x in xs]
    y1, s = _s5a(jnp.concatenate(xs, axis=1), toe, ws_pair)
    hps, fins = [], []
    off = 0
    for u, h0, r in zip(us, h0s, rows):
        b, t, _ = u.shape
        nc = t // S5_L
        if h0 is None:
            h0p = jnp.zeros((S5_Q, 1, b * S5_LANES), jnp.float32)
        else:
            h0p = _s5_state_planes(h0.astype(jnp.float32)).reshape(S5_Q, 1, b * S5_LANES)
        hp, fin = _s5b(s[:, off:off + r].reshape(S5_Q, nc, b * S5_LANES), jnp.tile(a_l, (1, 1, b)), h0p)
        hps.append(hp.reshape(S5_Q, r, S5_LANES))
        fins.append(fin.reshape(2, 2, b, S5_GROUPS, S5_STATE).transpose(2, 0, 3, 4, 1))
        off += r
    y = _s5c(jnp.concatenate(hps, axis=1), wo_pair, y1)
    outs, off = [], 0
    for u, r in zip(us, rows):
        b, t, _ = u.shape
        outs.append(_s5_from_chunks(y[:, off:off + r], b, t))
        off += r
    return jnp.concatenate(outs, axis=0), fins


LRU_TC = 256
LRU_HALO = 16


def _lru_kernel(x_ref, gate_ref, cw_ref, cb_ref, wg_ref, bg_ref, sp_ref, h0_ref, out_ref, fin_ref,
                a_s, b_s, hf_s, *, t_len, tc):
    f32 = jnp.float32
    nt = t_len // tc
    w = LRU_W
    nb = tc // 8
    row = lax.broadcasted_iota(jnp.int32, (nb, 8, w), 1)

    def gates(c, d):
        r0 = pl.multiple_of(c * tc, tc)
        lo = pl.multiple_of(jnp.maximum(r0 - LRU_HALO, 0), LRU_HALO)
        hi = pl.multiple_of(jnp.minimum(r0 + tc, t_len - LRU_HALO), LRU_HALO)
        prev = jnp.where(c > 0, x_ref[pl.ds(lo, LRU_HALO), :].astype(f32), 0.0)
        nxt = jnp.where(c < nt - 1, x_ref[pl.ds(hi, LRU_HALO), :].astype(f32), 0.0)
        slab = jnp.concatenate([prev, x_ref[pl.ds(r0, tc), :].astype(f32), nxt], axis=0)
        o = LRU_HALO - LRU_CONV // 2
        xc = cb_ref[...] + sum(cw_ref[k:k + 1, :] * slab[o + k:o + k + tc] for k in range(LRU_CONV))
        g = jnp.dot(xc.astype(jnp.bfloat16), wg_ref[d], preferred_element_type=f32) + bg_ref[d]
        r = _sigmoid(g[:, :w])
        i = _sigmoid(g[:, w:])
        log_a = -sp_ref[d] * r
        a = jnp.exp(log_a)
        b = (jnp.sqrt(1.0 - a * a) * (i * xc)).reshape(nb, 8, w)
        a = a.reshape(nb, 8, w)
        for sh in (1, 2, 4):
            keep, rot = (row >= sh, sh) if d == 0 else (row < 8 - sh, 8 - sh)
            ap = jnp.where(keep, pltpu.roll(a, rot, 1), 1.0)
            bp = jnp.where(keep, pltpu.roll(b, rot, 1), 0.0)
            b = a * bp + b
            a = a * ap
        a_s[...] = a
        b_s[...] = b
        return r0

    def fwd_chunk(c, h):
        r0 = gates(c, 0)

        def block(k, h):
            h8 = a_s[k] * h + b_s[k]
            hf_s[pl.ds(pl.multiple_of(r0 + k * 8, 8), 8), :] = h8
            return h8[7:8, :]

        return lax.fori_loop(0, nb, block, h, unroll=4)

    h = lax.fori_loop(0, nt, fwd_chunk, h0_ref[0, 0:1, :])
    fin_ref[0, 0:1, :] = h

    def bwd_chunk(ci, h):
        r0 = gates(nt - 1 - ci, 1)

        def block(i, h):
            k = nb - 1 - i
            h8 = a_s[k] * h + b_s[k]
            rows = pl.ds(pl.multiple_of(r0 + k * 8, 8), 8)
            hf_s[rows, :] = hf_s[rows, :] + h8
            return h8[0:1, :]

        h = lax.fori_loop(0, nb, block, h, unroll=4)
        sl = pl.ds(r0, tc)
        out_ref[sl, :] = (hf_s[sl, :] * _gelu_tanh(gate_ref[sl, :].astype(f32))).astype(out_ref.dtype)
        return h

    h = lax.fori_loop(0, nt, bwd_chunk, h0_ref[0, 1:2, :])
    fin_ref[0, 1:2, :] = h


def _block_diag(wb):
    n, k, j = wb.shape
    return (wb[:, :, None, :] * jnp.eye(n, dtype=wb.dtype)[:, None, :, None]).reshape(n * k, n * j)


def _lru_tables(conv_w, conv_b, w_a, b_a, w_x, b_x, lam):
    wg = jnp.stack([jnp.concatenate([_block_diag(w_a[d]), _block_diag(w_x[d])], axis=1) for d in range(2)])
    bg = jnp.concatenate([b_a, b_x], axis=-1)[:, None, :]
    sp = (LRU_C * jax.nn.softplus(-lam.astype(jnp.float32)))[:, None, :]
    return conv_w, conv_b[None, :], wg.astype(jnp.bfloat16), bg, sp


def _lru(p, row0, b, t, h0, tables):
    w = LRU_W
    tc = min(LRU_TC, t)
    cw, cb, wg, bg, sp = tables
    full = lambda a: pl.BlockSpec(a.shape, lambda i: (0,) * a.ndim)
    rb = row0 // t
    return pl.pallas_call(
        functools.partial(_lru_kernel, t_len=t, tc=tc),
        grid=(b,),
        in_specs=[pl.BlockSpec((t, w), lambda i: (rb + i, C_LX // w)),
                  pl.BlockSpec((t, w), lambda i: (rb + i, C_LG // w)),
                  full(cw), full(cb), full(wg), full(bg), full(sp),
                  pl.BlockSpec((1, 2, w), lambda i: (i, 0, 0))],
        out_specs=[pl.BlockSpec((t, w), lambda i: (i, 0)),
                   pl.BlockSpec((1, 2, w), lambda i: (i, 0, 0))],
        out_shape=[jax.ShapeDtypeStruct((b * t, w), jnp.bfloat16),
                   jax.ShapeDtypeStruct((b, 2, w), jnp.float32)],
        scratch_shapes=[pltpu.VMEM((tc // 8, 8, w), jnp.float32), pltpu.VMEM((tc // 8, 8, w), jnp.float32),
                        pltpu.VMEM((t, w), jnp.float32)],
        compiler_params=_cparams(1),
        name="rglru",
    )(p, p, cw, cb, wg, bg, sp, h0)


MLA_TM = 512
ATT_DQ = 256


def _mla_prep_kernel(q_ref, kv_ref, kr_ref, tab_ref, gq_ref, gkv_ref, wuq_ref, wukv_ref,
                     qo_ref, ko_ref, vo_ref, ckv_ref):
    f32, bf = jnp.float32, jnp.bfloat16
    tab = tab_ref[...]

    def rope(blk):
        prod = blk * tab
        return prod + pltpu.roll(prod, 64, 1)

    qn = (_rms(q_ref[...].astype(f32)) * gq_ref[...]).astype(bf)
    qq = jnp.dot(qn, wuq_ref[...], preferred_element_type=f32)
    ckv = _rms(kv_ref[...].astype(f32)) * gkv_ref[...]
    ckv_ref[...] = ckv
    kk = jnp.dot(ckv.astype(bf), wukv_ref[...], preferred_element_type=f32)
    lane = lax.broadcasted_iota(jnp.int32, tab.shape, 1)
    kpe = jnp.where(lane < MLA_ROPE, rope(kr_ref[...].astype(f32)), 0.0).astype(bf)
    for h in range(MLA_HEADS):
        o = h * ATT_DQ
        qo_ref[:, o:o + 128] = (qq[:, o:o + 128] * MLA_SCALE).astype(bf)
        qo_ref[:, o + 128:o + 256] = (rope(qq[:, o + 128:o + 256]) * MLA_SCALE).astype(bf)
        ko_ref[:, o:o + 128] = kk[:, o:o + 128].astype(bf)
        ko_ref[:, o + 128:o + 256] = kpe
        vo_ref[:, h * MLA_V:(h + 1) * MLA_V] = kk[:, o + 128:o + 256].astype(bf)


def _mla_weights(w_uq, w_ukv):
    wq = w_uq.reshape(MLA_Q_RANK, MLA_HEADS, MLA_NOPE + MLA_ROPE)
    pe = wq[:, :, MLA_NOPE:].reshape(MLA_Q_RANK, MLA_HEADS * MLA_ROPE)
    pep = _rot_partner_cols(pe, MLA_HEADS, MLA_ROPE).reshape(MLA_Q_RANK, MLA_HEADS, MLA_ROPE)
    wq_ext = jnp.concatenate([wq, pep], axis=-1).reshape(MLA_Q_RANK, MLA_HEADS * ATT_DQ)
    return wq_ext.astype(jnp.bfloat16), w_ukv.astype(jnp.bfloat16)


def _mla_prep(p, tab, gq, gkv, wuq, wukv):
    n = p.shape[0]
    tm = MLA_TM
    full = lambda a: pl.BlockSpec(a.shape, lambda i: (0,) * a.ndim)
    row = lambda width, col: pl.BlockSpec((tm, width), lambda i: (i, col // width))
    return pl.pallas_call(
        _mla_prep_kernel,
        grid=(n // tm,),
        in_specs=[row(MLA_Q_RANK, C_MQ), row(MLA_KV_RANK, C_MKV), row(128, C_KR), row(128, 0),
                  full(gq), full(gkv), full(wuq), full(wukv)],
        out_specs=[row(MLA_HEADS * ATT_DQ, 0), row(MLA_HEADS * ATT_DQ, 0), row(MLA_HEADS * MLA_V, 0),
                   row(MLA_KV_RANK, 0)],
        out_shape=[jax.ShapeDtypeStruct((n, MLA_HEADS * ATT_DQ), jnp.bfloat16),
                   jax.ShapeDtypeStruct((n, MLA_HEADS * ATT_DQ), jnp.bfloat16),
                   jax.ShapeDtypeStruct((n, MLA_HEADS * MLA_V), jnp.bfloat16),
                   jax.ShapeDtypeStruct((n, MLA_KV_RANK), jnp.float32)],
        compiler_params=_cparams(1),
        name="mla_prep",
    )(p, p, p, tab, gq, gkv, wuq, wukv)


ATT_TQ = 512
ATT_SUB = 256


def _attn_kernel(*refs, two, n_sub):
    dn = (((1,), (1,)), ((), ()))
    f32 = jnp.float32
    if two:
        q_ref, k_ref, v_ref, k2_ref, v2_ref, o_ref = refs
    else:
        q_ref, k_ref, v_ref, o_ref = refs
    rows = q_ref.shape[0] // n_sub
    for part in range(n_sub):
        sl = slice(part * rows, (part + 1) * rows)
        q = q_ref[sl, :]
        s = lax.dot_general(q, k_ref[...], dn, preferred_element_type=f32)
        m = jnp.max(s, axis=-1, keepdims=True)
        if two:
            s2 = lax.dot_general(q, k2_ref[...], dn, preferred_element_type=f32)
            m = jnp.maximum(m, jnp.max(s2, axis=-1, keepdims=True))
        p = jnp.exp(s - m)
        l = jnp.sum(p, axis=-1, keepdims=True)
        o = jnp.dot(p.astype(jnp.bfloat16), v_ref[...], preferred_element_type=f32)
        if two:
            p2 = jnp.exp(s2 - m)
            l = l + jnp.sum(p2, axis=-1, keepdims=True)
            o = o + jnp.dot(p2.astype(jnp.bfloat16), v2_ref[...], preferred_element_type=f32)
        o_ref[sl, :] = (o / l).astype(o_ref.dtype)


def _attention(q, k, v, row0, b, t, k2=None, v2=None):
    tb = min(ATT_TQ, t)
    nq = t // tb
    qb, kb = row0 // tb, row0 // t
    two = k2 is not None
    in_specs = [pl.BlockSpec((tb, ATT_DQ), lambda bi, h, i: (qb + bi * nq + i, h)),
                pl.BlockSpec((t, ATT_DQ), lambda bi, h, i: (kb + bi, h)),
                pl.BlockSpec((t, MLA_V), lambda bi, h, i: (kb + bi, h))]
    args = [q, k, v]
    if two:
        t2 = k2.shape[0] // b
        in_specs += [pl.BlockSpec((t2, ATT_DQ), lambda bi, h, i: (bi, h)),
                     pl.BlockSpec((t2, MLA_V), lambda bi, h, i: (bi, h))]
        args += [k2, v2]
    return pl.pallas_call(
        functools.partial(_attn_kernel, two=two, n_sub=max(tb // ATT_SUB, 1)),
        grid=(b, MLA_HEADS, nq),
        in_specs=in_specs,
        out_specs=pl.BlockSpec((tb, MLA_V), lambda bi, h, i: (bi * nq + i, h)),
        out_shape=jax.ShapeDtypeStruct((b * t, MLA_HEADS * MLA_V), jnp.bfloat16),
        compiler_params=_cparams(3),
        name="mla_attention",
    )(*args)


def _ret_tables():
    f32 = jnp.float32
    c = RET_CHUNK
    log_g = jnp.log1p(-jnp.exp2(-5.0 - jnp.arange(RET_HEADS, dtype=f32)))[:, None, None]
    idx = jnp.arange(c, dtype=f32)
    dec = jnp.exp(jnp.abs(idx[:, None] - idx[None, :])[None] * log_g)
    row = lambda e: jnp.broadcast_to(jnp.exp(e[None, :, None] * log_g), (RET_HEADS, c, c))
    return jnp.stack([dec, row(idx + 1.0), row(c - idx), row(c - 1.0 - idx), row(idx)], axis=0)


RET_HPS = 2


def _ret_kernel(*refs, nc, rope):
    f32, bf = jnp.float32, jnp.bfloat16
    if rope:
        (q_ref, k_ref, v_ref, g_ref, qp_ref, kp_ref, cos_ref, sin_ref, tab_ref, s0_ref, o_ref, fin_ref,
         acc_s, st_s) = refs
    else:
        q_ref, k_ref, v_ref, g_ref, tab_ref, s0_ref, o_ref, fin_ref, acc_s, st_s = refs
    c, dk = RET_CHUNK, RET_DK
    nt_dims, tn_dims = (((1,), (1,)), ((), ())), (((0,), (0,)), ((), ()))

    def chunk(k0, hh):
        sl = pl.ds(pl.multiple_of(k0 * c, c), c)
        cols = slice(hh * dk, (hh + 1) * dk)
        qc, kc = q_ref[sl, cols].astype(f32), k_ref[sl, cols].astype(f32)
        if rope:
            cos, sin = cos_ref[sl, :], sin_ref[sl, :]
            qc = qc * cos + qp_ref[sl, cols].astype(f32) * sin
            kc = kc * cos + kp_ref[sl, cols].astype(f32) * sin
        return sl, cols, qc.astype(bf), kc * (RET_DK ** -0.5), v_ref[sl, cols]

    st_s[...] = s0_ref[0]
    acc_s[...] = jnp.zeros_like(acc_s)

    def body(i, carry):
        for hh in range(RET_HPS):
            dec, xif, xib, zf, zb = (tab_ref[t, hh] for t in range(5))
            g_chunk = xif[c - 1:c, :]
            sl, cols, qc, kc, vc = chunk(i, hh)
            s = st_s[0, hh]
            sc = lax.dot_general(qc, kc.astype(bf), nt_dims, preferred_element_type=f32) * dec
            acc_s[sl, cols] += (jnp.dot(sc.astype(bf), vc, preferred_element_type=f32)
                                + jnp.dot(qc, s.astype(bf), preferred_element_type=f32) * xif)
            st_s[0, hh] = g_chunk * s + lax.dot_general((kc * zf).astype(bf), vc, tn_dims,
                                                        preferred_element_type=f32)
            sl, cols, qc, kc, vc = chunk(nc - 1 - i, hh)
            s = st_s[1, hh]
            acc_s[sl, cols] += jnp.dot(qc, s.astype(bf), preferred_element_type=f32) * xib
            st_s[1, hh] = g_chunk * s + lax.dot_general((kc * zb).astype(bf), vc, tn_dims,
                                                        preferred_element_type=f32)
        return carry

    lax.fori_loop(0, nc, body, 0)
    fin_ref[0] = st_s[...]

    def finish(i, carry):
        sl = pl.ds(pl.multiple_of(i * c, c), c)
        for hh in range(RET_HPS):
            cols = slice(hh * dk, (hh + 1) * dk)
            g = g_ref[sl, cols].astype(f32)
            o_ref[sl, cols] = (_rms(acc_s[sl, cols]) * (g * _sigmoid(g))).astype(o_ref.dtype)
        return carry

    lax.fori_loop(0, nc, finish, 0)


def _retention(p, row0, b, t, s0, tables, rope_tabs=None):
    nc = t // RET_CHUNK
    rb = row0 // t
    rope = rope_tabs is not None
    wd = RET_HPS * RET_DK
    col = lambda c0: pl.BlockSpec((t, wd), lambda bi, h: (rb + bi, c0 // wd + h))
    st = pl.BlockSpec((1, 2, RET_HPS, RET_DK, RET_DV), lambda bi, h: (bi, 0, h, 0, 0))
    in_specs = [col(C_RQ), col(C_RK), col(C_RV), col(C_RG)]
    args = [p, p, p, p]
    if rope:
        tab = pl.BlockSpec((t, RET_DK), lambda bi, h: (0, 0))
        in_specs += [col(C_RQP), col(C_RKP), tab, tab]
        args += [p, p, rope_tabs[0], rope_tabs[1]]
    in_specs += [pl.BlockSpec((5, RET_HPS, RET_CHUNK, RET_CHUNK), lambda bi, h: (0, h, 0, 0)), st]
    args += [tables, s0]
    return pl.pallas_call(
        functools.partial(_ret_kernel, nc=nc, rope=rope),
        grid=(b, RET_HEADS // RET_HPS),
        in_specs=in_specs,
        out_specs=[pl.BlockSpec((t, wd), lambda bi, h: (bi, h)), st],
        out_shape=[jax.ShapeDtypeStruct((b * t, RET_HEADS * RET_DV), jnp.bfloat16),
                   jax.ShapeDtypeStruct((b, 2, RET_HEADS, RET_DK, RET_DV), jnp.float32)],
        scratch_shapes=[pltpu.VMEM((t, wd), jnp.float32), pltpu.VMEM((2, RET_HPS, RET_DK, RET_DV), jnp.float32)],
        compiler_params=_cparams(2, V7X_VMEM_LIMIT_BIG),
        name="retention",
    )(*args)


OUT_TM = 512
ROUTER_PAD = 128


ROW_TILE = (8, D_MODEL // 16)


def _pack_halves(x):
    w = x.shape[1] // 2
    bits = lambda a: lax.bitcast_convert_type(a.astype(jnp.bfloat16).astype(jnp.float32), jnp.int32)
    return bits(x[:, w:]) | lax.shift_right_logical(bits(x[:, :w]), 16)


def _unpack_halves(u):
    lo = lax.bitcast_convert_type(lax.shift_left(u, 16), jnp.float32)
    hi = lax.bitcast_convert_type(u & jnp.int32(-65536), jnp.float32)
    return jnp.concatenate([lo, hi], axis=1)


def _split_bf16(x):
    hi = x.astype(jnp.bfloat16)
    return hi, (x - hi.astype(jnp.float32)).astype(jnp.bfloat16)


def _route(h, whi_ref, wlo_ref, rb_ref):
    f32 = jnp.float32
    hi, lo = _split_bf16(h)
    dn = (((1,), (1,)), ((), ()))
    lt = (lax.dot_general(whi_ref[...], hi, dn, preferred_element_type=f32)
          + lax.dot_general(whi_ref[...], lo, dn, preferred_element_type=f32)
          + lax.dot_general(wlo_ref[...], hi, dn, preferred_element_type=f32))[:N_EXPERTS]
    m = jnp.max(lt, axis=0, keepdims=True)
    e = jnp.exp(lt - m)
    sc = e / jnp.sum(e, axis=0, keepdims=True)
    sel = sc + rb_ref[...][:N_EXPERTS, 0:1]
    rows = lambda a: [a[j:j + 1, :] for j in range(N_EXPERTS)]
    sel_r, sc_r = rows(sel), rows(sc)
    epg = EXPERTS_PER_GROUP

    def top2sum(a, b, c, d):
        h1, l1, h2, l2 = jnp.maximum(a, b), jnp.minimum(a, b), jnp.maximum(c, d), jnp.minimum(c, d)
        return jnp.maximum(h1, h2) + jnp.maximum(jnp.minimum(h1, h2), jnp.maximum(l1, l2))

    gs = [top2sum(*sel_r[g * epg:(g + 1) * epg]) for g in range(N_EXPERT_GROUPS)]
    best, gi = gs[0], jnp.zeros_like(gs[0], dtype=jnp.int32)
    for g in range(1, N_EXPERT_GROUPS):
        upd = gs[g] > best
        gi = jnp.where(upd, g, gi)
        best = jnp.where(upd, gs[g], best)

    def pick(r, j):
        out = r[j]
        for g in range(1, N_EXPERT_GROUPS):
            out = jnp.where(gi == g, r[g * epg + j], out)
        return out

    v = [pick(sel_r, j) for j in range(epg)]
    s = [pick(sc_r, j) for j in range(epg)]

    def argmax_first(vals):
        bv, bi = vals[0], jnp.zeros_like(gi)
        for j in range(1, epg):
            upd = vals[j] > bv
            bi = jnp.where(upd, j, bi)
            bv = jnp.where(upd, vals[j], bv)
        return bi

    i1 = argmax_first(v)
    neg = jnp.float32(-jnp.inf)
    i2 = argmax_first([jnp.where(i1 == j, neg, v[j]) for j in range(epg)])
    take = lambda i: sum(jnp.where(i == j, s[j], 0.0) for j in range(epg))
    w1, w2 = take(i1), take(i2)
    tot = w1 + w2
    return (jnp.concatenate([gi * epg + i1, gi * epg + i2], axis=0),
            jnp.concatenate([w1 / tot, w2 / tot], axis=0))


def _mixout_kernel(ylin_ref, u_ref, ymla_c, ymla_l, ylru_c, ylru_l, yret_c, yret_l, x_c, x_l, mod_ref, d_ref,
                   wglu_ref, wout_ref, g2n_ref, whi_ref, wlo_ref, rb_ref, xo_ref, h2_ref, idx_ref, wgt_ref,
                   *, n_ctx_tiles):
    f32, bf = jnp.float32, jnp.bfloat16
    w = GROUP_W
    is_ctx = pl.program_id(0) < n_ctx_tiles
    pick = lambda c_ref, l_ref: jnp.where(is_ctx, c_ref[...], l_ref[...])
    y = _gelu_tanh(ylin_ref[...].astype(f32) + d_ref[...] * u_ref[...].astype(f32))
    y5 = (y * _sigmoid(jnp.dot(y.astype(bf), wglu_ref[...], preferred_element_type=f32))).astype(bf)
    mix = (jnp.dot(y5, wout_ref[0:w, :], preferred_element_type=f32)
           + jnp.dot(pick(ymla_c, ymla_l), wout_ref[w:2 * w, :], preferred_element_type=f32)
           + jnp.dot(pick(ylru_c, ylru_l), wout_ref[2 * w:3 * w, :], preferred_element_type=f32)
           + jnp.dot(pick(yret_c, yret_l), wout_ref[3 * w:4 * w, :], preferred_element_type=f32))
    x = pick(x_c, x_l) + mod_ref[0, 2:3, :] * mix
    xo_ref[...] = x
    h2 = _rms(x) * g2n_ref[...] * (1.0 + mod_ref[0, 4:5, :]) + mod_ref[0, 3:4, :]
    h2_ref[...] = _pack_halves(h2).reshape(h2_ref.shape)
    idx, wgt = _route(h2, whi_ref, wlo_ref, rb_ref)
    idx_ref[...] = idx
    wgt_ref[...] = wgt


def _router_weights(router_w, router_b):
    d = router_w.shape[0]
    wt = jnp.zeros((ROUTER_PAD, d), jnp.float32).at[:N_EXPERTS].set(router_w.T.astype(jnp.float32))
    whi, wlo = _split_bf16(wt)
    rb = jnp.zeros((ROUTER_PAD, 128), jnp.float32).at[:N_EXPERTS].set(router_b.astype(jnp.float32)[:, None])
    return whi, wlo, rb


def _mixout(ylin, p, ymla, ylru, yret, x, mods, s5_d, wglu, wout, g2n, rw, n_ctx, dec_seq):
    n, d = x[0].shape[0] + x[1].shape[0], x[0].shape[1]
    tm = OUT_TM
    w = GROUP_W
    na = n_ctx // tm
    seq = functools.partial(_seq_of_tile, tm=tm, n_ctx=n_ctx, dec_seq=dec_seq)
    full = lambda a: pl.BlockSpec(a.shape, lambda i: (0,) * a.ndim)
    row = lambda width: pl.BlockSpec((tm, width), lambda i: (i, 0))
    ctx_row, lat_row = _two_part_rows(tm, w, na)
    lanes = pl.BlockSpec((TOP_K, tm), lambda i: (0, i))
    whi, wlo, rb = rw
    return pl.pallas_call(
        functools.partial(_mixout_kernel, n_ctx_tiles=na),
        grid=(n // tm,),
        in_specs=[row(w), row(w), ctx_row, lat_row, ctx_row, lat_row, ctx_row, lat_row, *_two_part_rows(tm, d, na),
                  pl.BlockSpec((1, N_MOD, d), lambda i: (seq(i), 0, 0)),
                  full(s5_d), full(wglu), full(wout), full(g2n), full(whi), full(wlo), full(rb)],
        out_specs=[row(d), pl.BlockSpec((tm,) + ROW_TILE, lambda i: (i, 0, 0)), lanes, lanes],
        out_shape=[jax.ShapeDtypeStruct((n, d), jnp.float32), jax.ShapeDtypeStruct((n,) + ROW_TILE, jnp.int32),
                   jax.ShapeDtypeStruct((TOP_K, n), jnp.int32), jax.ShapeDtypeStruct((TOP_K, n), jnp.float32)],
        compiler_params=_cparams(1, V7X_VMEM_LIMIT_BIG),
        name="mix_out_norm_route",
    )(ylin, p, *ymla, *ylru, *yret, *x, mods, s5_d, wglu, wout, g2n, whi, wlo, rb)


MOE_TM = 256
MOE_NB = 256


def _expert_kernel(te_ref, nt_ref, nv_ref, tok0_ref, tok1_ref, w_ref, dstp_ref, dst_ref, wg_ref, wu_ref, wd_ref,
                   h_hbm, y_hbm, wg_s, wu_s, wd_s, xbuf, xs_s, act_s, ybuf, gsem, ssem):
    i = pl.program_id(0)
    bf = jnp.bfloat16
    tm = xs_s.shape[0]
    nt = nt_ref[0]

    def gather_start(tok_ref, sl, unroll):
        def row(r, carry):
            pltpu.make_async_copy(h_hbm.at[tok_ref[0, 0, r]], xbuf.at[sl, r], gsem.at[sl]).start()
            return carry

        lax.fori_loop(0, tm, row, 0, unroll=unroll)

    def gather_done(sl):
        pltpu.make_async_copy(h_hbm.at[pl.ds(0, tm)], xbuf.at[sl], gsem.at[sl]).wait()

    def scatter_row(d_ref, sl, r):
        return pltpu.make_async_copy(ybuf.at[sl, r], y_hbm.at[d_ref[0, 0, r]], ssem.at[sl])

    def scatter_done(sl, n_rows):
        pltpu.make_async_copy(ybuf.at[sl, pl.ds(0, n_rows)], y_hbm.at[pl.ds(0, n_rows)], ssem.at[sl]).wait()

    def tile(slot):
        n_prev = jnp.where(i > 0, nv_ref[jnp.maximum(i - 1, 0)], 0)
        f = wg_s.shape[1]
        d = wd_s.shape[1]
        up_pieces, down_pieces = 2 * (f // MOE_NB), d // MOE_NB

        def gather_rows(piece):
            for r in range(piece * tm // up_pieces, (piece + 1) * tm // up_pieces):
                pltpu.make_async_copy(h_hbm.at[tok1_ref[0, 0, r]], xbuf.at[1 - slot, r],
                                      gsem.at[1 - slot]).start(priority=r % 2)

        def scatter_rows(piece):
            for r in range(piece * tm // down_pieces, (piece + 1) * tm // down_pieces):
                @pl.when(r < n_prev)
                def _():
                    scatter_row(dstp_ref, 1 - slot, r).start(priority=r % 2)

        gather_done(slot)
        xs_s[...] = _unpack_halves(xbuf[slot].reshape(tm, -1)).astype(bf)
        x = xs_s[...]
        gs, us = [], []
        for c in range(f // MOE_NB):
            cols = slice(c * MOE_NB, (c + 1) * MOE_NB)
            gs.append(jnp.dot(x, wg_s[:, cols], preferred_element_type=jnp.float32))
            gather_rows(2 * c)
            us.append(jnp.dot(x, wu_s[:, cols], preferred_element_type=jnp.float32))
            gather_rows(2 * c + 1)
        g = jnp.concatenate(gs, axis=1)
        u = jnp.concatenate(us, axis=1)
        act_s[...] = ((g * _sigmoid(g)) * u * w_ref[...]).astype(bf)

        @pl.when(nt > 0)
        def _():
            act = act_s[...]
            ys = []
            for c in range(down_pieces):
                ys.append(jnp.dot(act, wd_s[:, c * MOE_NB:(c + 1) * MOE_NB], preferred_element_type=jnp.float32))
                scatter_rows(c)
            y = _pack_halves(jnp.concatenate(ys, axis=1)).reshape(ybuf.shape[1:])

            @pl.when(i > 1)
            def _():
                scatter_done(slot, nv_ref[jnp.maximum(i - 2, 0)])

            ybuf[slot] = y

        @pl.when(i == nt - 1)
        def _():
            def row(r, carry):
                scatter_row(dst_ref, slot, r).start()
                return carry

            lax.fori_loop(0, nv_ref[i], row, 0)

            @pl.when(i > 0)
            def _():
                scatter_done(1 - slot, nv_ref[jnp.maximum(i - 1, 0)])

            scatter_done(slot, nv_ref[i])
            gather_done(1 - slot)

    @pl.when(i < nt)
    def _():
        @pl.when(i == 0)
        def _():
            gather_start(tok0_ref, 0, 8)

        @pl.when((i == 0) | (te_ref[i] != te_ref[jnp.maximum(i - 1, 0)]))
        def _():
            wg_s[...] = wg_ref[0, 0].astype(bf)
            wu_s[...] = wu_ref[0, 0].astype(bf)
            wd_s[...] = wd_ref[0, 0].astype(bf)

        @pl.when(i % 2 == 0)
        def _():
            tile(0)

        @pl.when(i % 2 == 1)
        def _():
            tile(1)


def _experts(h2p, layer, ws, tok, dst, tile_expert, n_tiles_used, n_valid, wg, wu, wd):
    n = h2p.shape[0]
    d = 2 * ROW_TILE[0] * ROW_TILE[1]
    tm = MOE_TM
    n_tiles = tok.shape[0]
    f = wg.shape[-1]
    smem_row = lambda delta: pl.BlockSpec(
        (1, 1, tm), lambda i, te, nt, nv: (jnp.clip(i + delta, 0, n_tiles - 1), 0, 0), memory_space=pltpu.SMEM)
    wspec = lambda a, b: pl.BlockSpec((1, 1, a, b), lambda i, te, nt, nv: (layer, te[i], 0, 0))
    return pl.pallas_call(
        _expert_kernel,
        grid_spec=pltpu.PrefetchScalarGridSpec(
            num_scalar_prefetch=3,
            grid=(n_tiles,),
            in_specs=[smem_row(0), smem_row(1),
                      pl.BlockSpec((tm, 1), lambda i, te, nt, nv: (i, 0)),
                      smem_row(-1), smem_row(0),
                      wspec(d, f), wspec(d, f), wspec(f, d),
                      pl.BlockSpec(memory_space=pl.ANY)],
            out_specs=pl.BlockSpec(memory_space=pl.ANY),
            scratch_shapes=[pltpu.VMEM((d, f), jnp.bfloat16), pltpu.VMEM((d, f), jnp.bfloat16),
                            pltpu.VMEM((f, d), jnp.bfloat16), pltpu.VMEM((2, tm) + ROW_TILE, jnp.int32),
                            pltpu.VMEM((tm, d), jnp.bfloat16), pltpu.VMEM((tm, f), jnp.bfloat16),
                            pltpu.VMEM((2, tm) + ROW_TILE, jnp.int32),
                            pltpu.SemaphoreType.DMA((2,)), pltpu.SemaphoreType.DMA((2,))]),
        out_shape=jax.ShapeDtypeStruct((TOP_K * n,) + ROW_TILE, jnp.int32),
        compiler_params=pltpu.CompilerParams(dimension_semantics=("arbitrary",),
                                             vmem_limit_bytes=V7X_VMEM_LIMIT_BIG, disable_bounds_checks=True),
        name="moe_experts",
    )(tile_expert, n_tiles_used, n_valid, tok, tok, ws, dst, dst, wg, wu, wd, h2p)


def _moe_dispatch(idx, wgt):
    n = idx.shape[1]
    tm = MOE_TM
    n_pairs = TOP_K * n
    m_pad = n_pairs + N_EXPERTS * tm
    n_tiles = m_pad // tm
    e_flat = idx.reshape(-1)
    order = jnp.argsort(e_flat, stable=True).astype(jnp.int32)
    experts = jnp.arange(N_EXPERTS, dtype=jnp.int32)
    counts = jnp.sum((e_flat[None, :] == experts[:, None]).astype(jnp.int32), axis=1)
    starts_raw = jnp.cumsum(counts) - counts
    padded = ((counts + tm - 1) // tm) * tm
    ends = jnp.cumsum(padded)
    starts_pad = ends - padded
    tile_start = jnp.arange(n_tiles, dtype=jnp.int32) * tm
    tile_expert = jnp.minimum(jnp.sum((tile_start[:, None] >= ends[None, :]).astype(jnp.int32), axis=1),
                              N_EXPERTS - 1)
    r = jnp.arange(m_pad, dtype=jnp.int32)
    te_r = jnp.repeat(tile_expert, tm)
    off = r - jnp.take(starts_pad, te_r)
    valid = (off >= 0) & (off < jnp.take(counts, te_r))
    pair = jnp.take(order, jnp.clip(jnp.take(starts_raw, te_r) + off, 0, n_pairs - 1))
    tok = jnp.where(valid, pair % n, 0).reshape(n_tiles, 1, tm)
    dst = jnp.where(valid, pair, 0).reshape(n_tiles, 1, tm)
    ws = jnp.where(valid, jnp.take(wgt.reshape(-1), pair), 0.0)[:, None]
    n_tiles_used = (ends[-1] // tm).astype(jnp.int32).reshape(1)
    n_valid = jnp.sum(valid.reshape(n_tiles, tm), axis=1).astype(jnp.int32)
    return tok, dst, ws, tile_expert, n_tiles_used, n_valid


def _moe_rows(h2p, layer, idx, wgt, wg, wu, wd):
    tok, dst, ws, tile_expert, n_tiles_used, n_valid = _moe_dispatch(idx, wgt)
    return _experts(h2p, layer, ws, tok, dst, tile_expert, n_tiles_used, n_valid, wg, wu, wd)


RES_TM = 512


def _resid_kernel(x_ref, ya_ref, yb_ref, mod_ref, gf_ref, oc_ref, ol_ref, *, final, n_ctx_tiles):
    rows = lambda r: _unpack_halves(r[...].reshape(r.shape[0], -1))
    x = x_ref[...] + mod_ref[0, 5:6, :] * (rows(ya_ref) + rows(yb_ref))
    y = _rms(x) * gf_ref[...] if final else x
    i = pl.program_id(0)

    @pl.when(i < n_ctx_tiles)
    def _():
        oc_ref[...] = y

    @pl.when(i >= n_ctx_tiles)
    def _():
        ol_ref[...] = y


def _resid(x, y2, mods, gf, n_ctx, dec_seq, final):
    n, d = x.shape
    tm = RES_TM
    na = n_ctx // tm
    seq = functools.partial(_seq_of_tile, tm=tm, n_ctx=n_ctx, dec_seq=dec_seq)
    row = pl.BlockSpec((tm, d), lambda i: (i, 0))
    return pl.pallas_call(
        functools.partial(_resid_kernel, final=final, n_ctx_tiles=na),
        grid=(n // tm,),
        in_specs=[row, pl.BlockSpec((tm,) + ROW_TILE, lambda i: (i, 0, 0)),
                  pl.BlockSpec((tm,) + ROW_TILE, lambda i: (n // tm + i, 0, 0)),
                  pl.BlockSpec((1, N_MOD, d), lambda i: (seq(i), 0, 0)),
                  pl.BlockSpec((1, d), lambda i: (0, 0))],
        out_specs=list(_two_part_rows(tm, d, na)),
        out_shape=[jax.ShapeDtypeStruct((n_ctx, d), jnp.float32), jax.ShapeDtypeStruct((n - n_ctx, d), jnp.float32)],
        compiler_params=_cparams(1),
        name="moe_residual_norm",
    )(x, y2, y2, mods, gf)


def kernel(x_prompt, x_sample, c, cache_mla_ckv, cache_mla_kpe, state_s5, state_lru, state_ret,
           c_ctx, w_ada, b_ada, norm1_g, norm2_g, w_in, w_out,
           s5_a_re, s5_a_im, s5_log_dt, s5_b_re, s5_b_im, s5_c_re, s5_c_im, s5_d, s5_w_glu,
           mla_q_norm_g, mla_w_uq, mla_kv_norm_g, mla_w_ukv,
           lru_conv_w, lru_conv_b, lru_w_a, lru_b_a, lru_w_x, lru_b_x, lru_lambda,
           router_w, router_b, moe_w_gate, moe_w_up, moe_w_down, final_norm_g):
    f32, bf = jnp.float32, jnp.bfloat16
    batch, seq_len, d = x_prompt.shape
    dec_batch, dec_seq, _ = x_sample.shape
    past = cache_mla_ckv.shape[2]
    n_ctx, n_lat = batch * seq_len, dec_batch * dec_seq
    depth = w_in.shape[0]

    x = (x_prompt.reshape(n_ctx, d), x_sample.reshape(n_lat, d))
    cvec = jnp.zeros((8, d), f32).at[0].set(c_ctx).at[1:1 + dec_batch].set(c)
    mods_all = _ada(cvec, w_ada, b_ada).reshape(depth, 8, N_MOD, d)

    cos64, sin64 = _rope_tables(dec_seq, MLA_ROPE)
    lat_tab = jnp.tile(jnp.concatenate([cos64, sin64], axis=1), (dec_batch, 1))
    ctx_tab = jnp.concatenate([jnp.ones((n_ctx, MLA_ROPE), f32), jnp.zeros((n_ctx, MLA_ROPE), f32)], axis=1)
    mla_tab = jnp.concatenate([ctx_tab, lat_tab], axis=0)
    ret_rope = _rope_tables(dec_seq, RET_DK)
    ret_tables = _ret_tables()
    rw = _router_weights(router_w, router_b)
    gf = final_norm_g[None, :]

    states = []
    for l in range(depth):
        mods = mods_all[l]
        s5_tab = _s5_tables(s5_a_re[l], s5_a_im[l], s5_log_dt[l], s5_b_re[l], s5_b_im[l], s5_c_re[l], s5_c_im[l])
        lru_tab = _lru_tables(lru_conv_w[l], lru_conv_b[l], lru_w_a[l], lru_b_a[l], lru_w_x[l], lru_b_x[l],
                              lru_lambda[l])
        wuq, wukv = _mla_weights(mla_w_uq[l], mla_w_ukv[l])

        p = _inproj(x, mods, norm1_g[l][None, :], _inproj_weights(w_in[l]), n_ctx, dec_seq)

        u_ctx = p[:n_ctx, C_S5:C_S5 + S5_CH].reshape(batch, seq_len, S5_CH)
        u_lat = p[n_ctx:, C_S5:C_S5 + S5_CH].reshape(dec_batch, dec_seq, S5_CH)
        ylin, s5_fins = _s5_core([u_ctx, u_lat], [None, state_s5[:, l]], s5_tab)

        qo, ko, vo, ckv = _mla_prep(p, mla_tab, mla_q_norm_g[l][None, :], mla_kv_norm_g[l][None, :], wuq, wukv)
        ckv_c = cache_mla_ckv[:, l].reshape(dec_batch * past, MLA_KV_RANK)
        kv_c = _matmul(ckv_c, wukv, out_dtype=bf).reshape(dec_batch * past, MLA_HEADS, 2, 128)
        kpe_c = jnp.broadcast_to(cache_mla_kpe[:, l].reshape(dec_batch * past, 1, MLA_ROPE).astype(bf),
                                 (dec_batch * past, MLA_HEADS, MLA_ROPE))
        k_c = jnp.concatenate([kv_c[:, :, 0], kpe_c, jnp.zeros_like(kpe_c)], axis=-1).reshape(
            dec_batch * past, MLA_HEADS * ATT_DQ)
        v_c = kv_c[:, :, 1].reshape(dec_batch * past, MLA_HEADS * MLA_V)
        ymla = (_attention(qo, ko, vo, 0, batch, seq_len),
                _attention(qo, ko, vo, n_ctx, dec_batch, dec_seq, k_c, v_c))

        ylru_c, lru_fin = _lru(p, 0, batch, seq_len, jnp.zeros((batch, 2, LRU_W), f32), lru_tab)
        ylru_l, _ = _lru(p, n_ctx, dec_batch, dec_seq, state_lru[:, l].astype(f32), lru_tab)
        ylru = (ylru_c, ylru_l)

        yret_c, ret_fin = _retention(p, 0, batch, seq_len,
                                     jnp.zeros((batch, 2, RET_HEADS, RET_DK, RET_DV), f32), ret_tables)
        yret_l, _ = _retention(p, n_ctx, dec_batch, dec_seq, state_ret[:, l].astype(f32), ret_tables, ret_rope)
        yret = (yret_c, yret_l)

        x_mid, h2, idx, wgt = _mixout(ylin, p, ymla, ylru, yret, x, mods, s5_d[l][None, :],
                                      s5_w_glu[l].astype(bf), w_out[l].astype(bf), norm2_g[l][None, :], rw,
                                      n_ctx, dec_seq)
        y2 = _moe_rows(h2, l, idx, wgt, moe_w_gate, moe_w_up, moe_w_down)
        x = _resid(x_mid, y2, mods, gf, n_ctx, dec_seq, final=(l == depth - 1))

        states.append((ckv[:n_ctx].reshape(batch, seq_len, MLA_KV_RANK),
                       p[:n_ctx, C_KR:C_KR + MLA_ROPE].astype(f32).reshape(batch, seq_len, MLA_ROPE),
                       s5_fins[0], lru_fin, ret_fin))

    y_prompt = x[0].reshape(batch, seq_len, d)
    y_sample = x[1].reshape(dec_batch, dec_seq, d)
    new_cache_mla_ckv = jnp.stack([st[0] for st in states], axis=1)
    new_cache_mla_kpe = jnp.stack([st[1] for st in states], axis=1)
    new_state_s5 = jnp.stack([st[2] for st in states], axis=1)
    new_state_lru = jnp.stack([st[3] for st in states], axis=1)
    new_state_ret = jnp.stack([st[4] for st in states], axis=1)
    return (y_prompt, y_sample, new_cache_mla_ckv, new_cache_mla_kpe, new_state_s5, new_state_lru, new_state_ret)
```

```python
import functools
import numpy as np
import jax
import jax.numpy as jnp
from jax import lax
from jax.experimental import pallas as pl
from jax.experimental.pallas import tpu as pltpu

D_MODEL = 2048
DEPTH = 2
GRID_W = 64
EPS = 1e-6
ROPE_BASE = 10000.0
N_MOD = 6
GROUP_W = 512
S5_CH = GROUP_W
S5_GROUP_CH = 16
S5_GROUPS = S5_CH // S5_GROUP_CH
S5_STATE = 64
MLA_HEADS = 4
MLA_NOPE = 128
MLA_ROPE = 64
MLA_V = 128
MLA_Q_RANK = GROUP_W
MLA_KV_RANK = GROUP_W // 2
MLA_SCALE = (MLA_NOPE + MLA_ROPE) ** -0.5
LRU_W = GROUP_W
LRU_CONV = 4
LRU_C = 8.0
RET_HEADS = 4
RET_DK = 128
RET_DV = 128
RET_CHUNK = 128
N_EXPERTS = 16
N_EXPERT_GROUPS = 4
EXPERTS_PER_GROUP = N_EXPERTS // N_EXPERT_GROUPS
TOP_K = 2
D_EXPERT = D_MODEL // 4

V7X_VMEM_LIMIT = 48 * 1024 * 1024
V7X_VMEM_LIMIT_BIG = 56 * 1024 * 1024

C_S5, C_MQ, C_LX, C_LG, C_RQ, C_RK, C_RV, C_RG, C_RQP, C_RKP, C_MKV, C_KR = (
    0, 512, 1024, 1536, 2048, 2560, 3072, 3584, 4096, 4608, 5120, 5376)
P_COLS = 5632
R_S5, R_MQ, R_MKV, R_KR, R_LX, R_LG, R_RQ, R_RK, R_RV, R_RG = (0, 512, 1024, 1280, 1344, 1856, 2368, 2880, 3392, 3904)


def _cparams(n_axes, limit=V7X_VMEM_LIMIT):
    return pltpu.CompilerParams(dimension_semantics=("arbitrary",) * n_axes, vmem_limit_bytes=limit)


def _sigmoid(x):
    return 1.0 / (1.0 + jnp.exp(-x))


def _gelu_tanh(x):
    return 0.5 * x * (1.0 + jnp.tanh(0.7978845608028654 * (x + 0.044715 * (x * x * x))))


def _rms(x):
    return x * lax.rsqrt(jnp.mean(x * x, axis=-1, keepdims=True) + EPS)


def _seq_of_tile(i, tm, n_ctx, dec_seq):
    assert n_ctx % tm == 0 and dec_seq % tm == 0, "a row tile must not straddle two sequences"
    r = i * tm
    return jnp.where(r < n_ctx, 0, 1 + (r - n_ctx) // dec_seq)


def _mm_kernel(x_ref, w_ref, o_ref):
    o_ref[...] = jnp.dot(x_ref[...].astype(jnp.bfloat16), w_ref[...].astype(jnp.bfloat16),
                         preferred_element_type=jnp.float32).astype(o_ref.dtype)


def _matmul(x, w, tm=512, tn=512, out_dtype=jnp.float32):
    m, k = x.shape
    _, n = w.shape
    tm, tn = min(tm, m), min(tn, n)
    return pl.pallas_call(
        _mm_kernel,
        grid=(n // tn, m // tm),
        in_specs=[pl.BlockSpec((tm, k), lambda j, i: (i, 0)), pl.BlockSpec((k, tn), lambda j, i: (0, j))],
        out_specs=pl.BlockSpec((tm, tn), lambda j, i: (i, j)),
        out_shape=jax.ShapeDtypeStruct((m, n), out_dtype),
        compiler_params=_cparams(2),
        name="matmul",
    )(x, w)


ADA_TN = 1024


def _ada_kernel(c_ref, w_ref, b_ref, o_ref):
    c = c_ref[...]
    s = (c * _sigmoid(c)).astype(jnp.bfloat16)
    o_ref[0] = jnp.dot(s, w_ref[0].astype(jnp.bfloat16), preferred_element_type=jnp.float32) + b_ref[0]


def _ada(cvec, w_ada, b_ada):
    depth, d, n = w_ada.shape
    return pl.pallas_call(
        _ada_kernel,
        grid=(depth, n // ADA_TN),
        in_specs=[pl.BlockSpec((8, d), lambda l, j: (0, 0)),
                  pl.BlockSpec((1, d, ADA_TN), lambda l, j: (l, 0, j)),
                  pl.BlockSpec((1, 1, ADA_TN), lambda l, j: (l, 0, j))],
        out_specs=pl.BlockSpec((1, 8, ADA_TN), lambda l, j: (l, 0, j)),
        out_shape=jax.ShapeDtypeStruct((depth, 8, n), jnp.float32),
        compiler_params=_cparams(2),
        name="adaln_mod",
    )(cvec, w_ada, b_ada.reshape(depth, 1, n))


IN_TM = 1024
IN_TN = 512


def _inproj_kernel(xc_ref, xl_ref, mod_ref, g_ref, w_ref, o_ref, h_s, *, n_ctx_tiles):
    @pl.when(pl.program_id(1) == 0)
    def _():
        x = jnp.where(pl.program_id(0) < n_ctx_tiles, xc_ref[...], xl_ref[...])
        h = _rms(x) * g_ref[...]
        h_s[...] = (h * (1.0 + mod_ref[0, 1:2, :]) + mod_ref[0, 0:1, :]).astype(h_s.dtype)

    j = pl.program_id(1)
    unused = (pl.program_id(0) < n_ctx_tiles) & (j >= C_RQP // IN_TN) & (j < C_MKV // IN_TN)

    @pl.when(unused)
    def _():
        o_ref[...] = jnp.zeros_like(o_ref)

    @pl.when(jnp.logical_not(unused))
    def _():
        o_ref[...] = jnp.dot(h_s[...], w_ref[...], preferred_element_type=jnp.float32).astype(o_ref.dtype)


def _two_part_rows(tm, width, n_ctx_tiles):
    return (pl.BlockSpec((tm, width), lambda i, *_: (jnp.minimum(i, n_ctx_tiles - 1), 0)),
            pl.BlockSpec((tm, width), lambda i, *_: (jnp.maximum(i - n_ctx_tiles, 0), 0)))


def _inproj(x, mods, g, w, n_ctx, dec_seq):
    n, d = x[0].shape[0] + x[1].shape[0], x[0].shape[1]
    na = n_ctx // IN_TM
    seq = functools.partial(_seq_of_tile, tm=IN_TM, n_ctx=n_ctx, dec_seq=dec_seq)
    return pl.pallas_call(
        functools.partial(_inproj_kernel, n_ctx_tiles=na),
        grid=(n // IN_TM, P_COLS // IN_TN),
        in_specs=[*_two_part_rows(IN_TM, d, na),
                  pl.BlockSpec((1, N_MOD, d), lambda i, j: (seq(i), 0, 0)),
                  pl.BlockSpec((1, d), lambda i, j: (0, 0)),
                  pl.BlockSpec((d, IN_TN), lambda i, j: (0, j))],
        out_specs=pl.BlockSpec((IN_TM, IN_TN), lambda i, j: (i, j)),
        out_shape=jax.ShapeDtypeStruct((n, P_COLS), jnp.bfloat16),
        scratch_shapes=[pltpu.VMEM((IN_TM, d), jnp.bfloat16)],
        compiler_params=_cparams(2, V7X_VMEM_LIMIT_BIG),
        name="norm_inproj",
    )(*x, mods, g, w)


def _rot_partner_cols(w, n_heads, head_dim):
    q = head_dim // 4
    wr = w.reshape(w.shape[0], n_heads, 2, 2, q)
    return jnp.stack([-wr[:, :, :, 1], wr[:, :, :, 0]], axis=3).reshape(w.shape)


def _inproj_weights(w_in):
    seg = lambda off, width: w_in[:, off:off + width]
    kr = seg(R_KR, MLA_ROPE)
    cols = [seg(R_S5, 2 * GROUP_W),
            seg(R_LX, 6 * GROUP_W),
            _rot_partner_cols(seg(R_RQ, 2 * GROUP_W), 2 * RET_HEADS, RET_DK),
            seg(R_MKV, MLA_KV_RANK), kr, _rot_partner_cols(kr, 1, MLA_ROPE),
            jnp.zeros((w_in.shape[0], P_COLS - C_KR - 2 * MLA_ROPE), w_in.dtype)]
    return jnp.concatenate(cols, axis=1).astype(jnp.bfloat16)


def _rope_tables(t_len, rot_dim):
    rows = t_len // GRID_W
    row = jnp.repeat(jnp.arange(rows, dtype=jnp.float32), GRID_W)
    col = jnp.tile(jnp.arange(GRID_W, dtype=jnp.float32), rows)
    n_freq = rot_dim // 4
    inv = ROPE_BASE ** (-jnp.arange(n_freq, dtype=jnp.float32) / n_freq)
    ang = jnp.concatenate([row[:, None] * inv[None]] * 2 + [col[:, None] * inv[None]] * 2, axis=1)
    return jnp.cos(ang), jnp.sin(ang)


S5_L = 32
S5_PAIRS = S5_GROUPS // 2
S5_Q = 4
S5_LANES = S5_GROUPS * S5_STATE
S5_SCAN_LANES = 1024


def _s5_tables(a_re, a_im, log_dt, b_re, b_im, c_re, c_im):
    f32 = jnp.float32
    L, G, P, C = S5_L, S5_GROUPS, S5_STATE, S5_GROUP_CH
    hp = lax.Precision.HIGHEST
    cmul = lambda xr, xi, yr, yi: (xr * yr - xi * yi, xr * yi + xi * yr)
    dt = jnp.exp(log_dt.astype(f32))[..., None]
    zr, zi = a_re * dt, a_im * dt
    ab_r, ab_i = jnp.exp(zr) * jnp.cos(zi), jnp.exp(zr) * jnp.sin(zi)
    den = a_re * a_re + a_im * a_im
    nr, ni = ab_r - 1.0, ab_i
    be_r = (nr * a_re + ni * a_im) / den
    be_i = (ni * a_re - nr * a_im) / den
    bt_r, bt_i = b_re.transpose(0, 2, 1), b_im.transpose(0, 2, 1)
    bp_r, bp_i = cmul(be_r[:, :, None, :], be_i[:, :, None, :], bt_r[None], bt_i[None])
    tau = jnp.arange(L + 1, dtype=f32)[None, None, :, None]
    mag = jnp.exp(zr[:, :, None, :] * tau)
    pw_r, pw_i = mag * jnp.cos(zi[:, :, None, :] * tau), mag * jnp.sin(zi[:, :, None, :] * tau)

    cat = lambda *xs: jnp.concatenate(xs, axis=-1)
    pf_r, pf_i = pw_r[0][:, :L][:, ::-1], pw_i[0][:, :L][:, ::-1]
    pb_r, pb_i = pw_r[1][:, :L], pw_i[1][:, :L]
    w1 = cat(pf_r, pf_i, pb_r, pb_i)[:, :, None, :]
    w2 = cat(pf_i, pf_r, pb_i, pb_r)[:, :, None, :]
    b1 = cat(bp_r[0], bp_r[0], bp_r[1], bp_r[1])[:, None]
    b2 = cat(-bp_i[0], bp_i[0], -bp_i[1], bp_i[1])[:, None]
    ws = (w1 * b1 + w2 * b2).reshape(G, L * C, S5_Q * P)

    ct_r, ct_i = c_re.transpose(0, 2, 1), c_im.transpose(0, 2, 1)
    pt_r, pt_i = pw_r.transpose(0, 1, 3, 2), pw_i.transpose(0, 1, 3, 2)

    def cm(er, ei):
        m_r, m_i = cmul(ct_r[:, :, None, :], ct_i[:, :, None, :], er[..., None], ei[..., None])
        return m_r.reshape(G, P, L * C), m_i.reshape(G, P, L * C)

    mf_r, mf_i = cm(pt_r[0][:, :, 1:L + 1], pt_i[0][:, :, 1:L + 1])
    mb_r, mb_i = cm(pt_r[1][:, :, 1:L + 1][:, :, ::-1], pt_i[1][:, :, 1:L + 1][:, :, ::-1])
    wo = jnp.stack([mf_r, -mf_i, mb_r, -mb_i], axis=1).reshape(G, S5_Q * P, L * C)

    def impulse(d):
        m_r, m_i = cm(pt_r[d][:, :, :L], pt_i[d][:, :, :L])
        kk = (jnp.einsum('gkp,gpx->gkx', bp_r[d], m_r, precision=hp)
              - jnp.einsum('gkp,gpx->gkx', bp_i[d], m_i, precision=hp))
        return kk.reshape(G, C, L, C)

    kf, kb = impulse(0), impulse(1)
    wide = jnp.concatenate([kb[:, :, :0:-1], kf[:, :, :1] + kb[:, :, :1], kf[:, :, 1:]], axis=2)
    wide = jnp.pad(wide.reshape(G, C, (2 * L - 1) * C), ((0, 0), (0, 0), (0, C)))

    a_l = jnp.stack([pw_r[0, :, L], pw_i[0, :, L], pw_r[1, :, L], pw_i[1, :, L]], axis=0).reshape(S5_Q, 1, S5_LANES)
    bf = jnp.bfloat16
    return _s5_toeplitz(wide), ws.astype(bf), wo.astype(bf), a_l


def _s5_toeplitz_kernel(w_ref, o_ref):
    w = w_ref[0]
    for s in range(S5_L):
        off = (S5_L - 1 - s) * S5_GROUP_CH
        o_ref[0, s * S5_GROUP_CH:(s + 1) * S5_GROUP_CH, :] = w[:, off:off + S5_L * S5_GROUP_CH].astype(o_ref.dtype)


def _s5_toeplitz(wide):
    g, c, wl = wide.shape
    n = S5_L * S5_GROUP_CH
    return pl.pallas_call(
        _s5_toeplitz_kernel,
        grid=(g,),
        in_specs=[pl.BlockSpec((1, c, wl), lambda i: (i, 0, 0))],
        out_specs=pl.BlockSpec((1, n, n), lambda i: (i, 0, 0)),
        out_shape=jax.ShapeDtypeStruct((g, n, n), jnp.bfloat16),
        compiler_params=_cparams(1),
        name="s5_toeplitz",
    )(wide)


def _s5a_kernel(x_ref, t_ref, ws_ref, y1_ref, s_ref):
    ss = []
    for j in range(2):
        x = x_ref[j]
        y1_ref[j] = jnp.dot(x, t_ref[j], preferred_element_type=jnp.float32)
        ss.append(jnp.dot(x, ws_ref[j], preferred_element_type=jnp.float32))
    p = S5_STATE
    for q in range(S5_Q):
        s_ref[q] = jnp.concatenate([ss[0][:, q * p:(q + 1) * p], ss[1][:, q * p:(q + 1) * p]], axis=1)


def _s5a(x, toe, ws):
    g, r, w = x.shape
    return pl.pallas_call(
        _s5a_kernel,
        grid=(S5_PAIRS,),
        in_specs=[pl.BlockSpec((2, r, w), lambda i: (i, 0, 0)),
                  pl.BlockSpec((2, w, w), lambda i: (i, 0, 0)),
                  pl.BlockSpec((2, w, S5_Q * S5_STATE), lambda i: (i, 0, 0))],
        out_specs=[pl.BlockSpec((2, r, w), lambda i: (i, 0, 0)),
                   pl.BlockSpec((S5_Q, r, 128), lambda i: (0, 0, i))],
        out_shape=[jax.ShapeDtypeStruct((g, r, w), jnp.float32),
                   jax.ShapeDtypeStruct((S5_Q, r, S5_LANES), jnp.float32)],
        compiler_params=_cparams(1),
        name="s5_chunk_local",
    )(x, toe, ws)


def _s5b_kernel(s_ref, a_ref, h0_ref, hp_ref, fin_ref, *, nc):
    def run(qr, qi, order_fwd):
        ar, ai = a_ref[qr], a_ref[qi]

        def body(i, carry):
            hr, hi = carry
            k = i if order_fwd else nc - 1 - i
            hp_ref[qr, pl.ds(k, 1), :] = hr
            hp_ref[qi, pl.ds(k, 1), :] = hi
            nr = ar * hr - ai * hi + s_ref[qr, pl.ds(k, 1), :]
            ni = ar * hi + ai * hr + s_ref[qi, pl.ds(k, 1), :]
            return nr, ni

        hr, hi = lax.fori_loop(0, nc, body, (h0_ref[qr], h0_ref[qi]))
        fin_ref[qr] = hr
        fin_ref[qi] = hi

    run(0, 1, True)
    run(2, 3, False)


def _s5b(s, a_l, h0):
    _, nc, w = s.shape
    tl = S5_SCAN_LANES
    vec = pl.BlockSpec((S5_Q, 1, tl), lambda i: (0, 0, i))
    seq = pl.BlockSpec((S5_Q, nc, tl), lambda i: (0, 0, i))
    return pl.pallas_call(
        functools.partial(_s5b_kernel, nc=nc),
        grid=(w // tl,),
        in_specs=[seq, vec, vec],
        out_specs=[seq, vec],
        out_shape=[jax.ShapeDtypeStruct((S5_Q, nc, w), jnp.float32),
                   jax.ShapeDtypeStruct((S5_Q, 1, w), jnp.float32)],
        compiler_params=_cparams(1),
        name="s5_chunk_scan",
    )(s, a_l, h0)


def _s5c_kernel(h_ref, wo_ref, y1_ref, y_ref):
    p = S5_STATE
    for j in range(2):
        hcat = jnp.concatenate([h_ref[q][:, j * p:(j + 1) * p] for q in range(S5_Q)], axis=1)
        y2 = jnp.dot(hcat.astype(jnp.bfloat16), wo_ref[j], preferred_element_type=jnp.float32)
        y_ref[j] = (y1_ref[j] + y2).astype(y_ref.dtype)


def _s5c(hprev, wo, y1):
    g, r, w = y1.shape
    return pl.pallas_call(
        _s5c_kernel,
        grid=(S5_PAIRS,),
        in_specs=[pl.BlockSpec((S5_Q, r, 128), lambda i: (0, 0, i)),
                  pl.BlockSpec((2, S5_Q * S5_STATE, w), lambda i: (i, 0, 0)),
                  pl.BlockSpec((2, r, w), lambda i: (i, 0, 0))],
        out_specs=pl.BlockSpec((2, r, w), lambda i: (i, 0, 0)),
        out_shape=jax.ShapeDtypeStruct((g, r, w), jnp.bfloat16),
        compiler_params=_cparams(1),
        name="s5_state_to_out",
    )(hprev, wo, y1)


def _s5_to_chunks(u):
    b, t, _ = u.shape
    nc = t // S5_L
    x = u.reshape(b, nc, S5_L, S5_GROUPS, S5_GROUP_CH).transpose(3, 1, 0, 2, 4)
    return x.reshape(S5_GROUPS, nc * b, S5_L * S5_GROUP_CH)


def _s5_from_chunks(y, b, t):
    nc = t // S5_L
    y = y.reshape(S5_GROUPS, nc, b, S5_L, S5_GROUP_CH).transpose(2, 1, 3, 0, 4)
    return y.reshape(b * t, S5_CH)


def _s5_state_planes(h0):
    b = h0.shape[0]
    return h0.transpose(1, 4, 0, 2, 3).reshape(S5_Q, b, S5_LANES)


def _s5_core(us, h0s, tables):
    toe, ws_pair, wo_pair, a_l = tables
    xs = [_s5_to_chunks(u) for u in us]
    rows = [x.shape[1] for x in xs]
    y1, s = _s5a(jnp.concatenate(xs, axis=1), toe, ws_pair)
    hps, fins = [], []
    off = 0
    for u, h0, r in zip(us, h0s, rows):
        b, t, _ = u.shape
        nc = t // S5_L
        if h0 is None:
            h0p = jnp.zeros((S5_Q, 1, b * S5_LANES), jnp.float32)
        else:
            h0p = _s5_state_planes(h0.astype(jnp.float32)).reshape(S5_Q, 1, b * S5_LANES)
        hp, fin = _s5b(s[:, off:off + r].reshape(S5_Q, nc, b * S5_LANES), jnp.tile(a_l, (1, 1, b)), h0p)
        hps.append(hp.reshape(S5_Q, r, S5_LANES))
        fins.append(fin.reshape(2, 2, b, S5_GROUPS, S5_STATE).transpose(2, 0, 3, 4, 1))
        off += r
    y = _s5c(jnp.concatenate(hps, axis=1), wo_pair, y1)
    outs, off = [], 0
    for u, r in zip(us, rows):
        b, t, _ = u.shape
        outs.append(_s5_from_chunks(y[:, off:off + r], b, t))
        off += r
    return jnp.concatenate(outs, axis=0), fins


LRU_TC = 256
LRU_HALO = 16


def _lru_kernel(x_ref, gate_ref, cw_ref, cb_ref, wg_ref, bg_ref, sp_ref, h0_ref, out_ref, fin_ref,
                a_s, b_s, hf_s, *, t_len, tc):
    f32 = jnp.float32
    nt = t_len // tc
    w = LRU_W
    nb = tc // 8
    row = lax.broadcasted_iota(jnp.int32, (nb, 8, w), 1)

    def gates(c, d):
        r0 = pl.multiple_of(c * tc, tc)
        lo = pl.multiple_of(jnp.maximum(r0 - LRU_HALO, 0), LRU_HALO)
        hi = pl.multiple_of(jnp.minimum(r0 + tc, t_len - LRU_HALO), LRU_HALO)
        prev = jnp.where(c > 0, x_ref[pl.ds(lo, LRU_HALO), :].astype(f32), 0.0)
        nxt = jnp.where(c < nt - 1, x_ref[pl.ds(hi, LRU_HALO), :].astype(f32), 0.0)
        slab = jnp.concatenate([prev, x_ref[pl.ds(r0, tc), :].astype(f32), nxt], axis=0)
        o = LRU_HALO - LRU_CONV // 2
        xc = cb_ref[...] + sum(cw_ref[k:k + 1, :] * slab[o + k:o + k + tc] for k in range(LRU_CONV))
        g = jnp.dot(xc.astype(jnp.bfloat16), wg_ref[d], preferred_element_type=f32) + bg_ref[d]
        r = _sigmoid(g[:, :w])
        i = _sigmoid(g[:, w:])
        log_a = -sp_ref[d] * r
        a = jnp.exp(log_a)
        b = (jnp.sqrt(1.0 - a * a) * (i * xc)).reshape(nb, 8, w)
        a = a.reshape(nb, 8, w)
        for sh in (1, 2, 4):
            keep, rot = (row >= sh, sh) if d == 0 else (row < 8 - sh, 8 - sh)
            ap = jnp.where(keep, pltpu.roll(a, rot, 1), 1.0)
            bp = jnp.where(keep, pltpu.roll(b, rot, 1), 0.0)
            b = a * bp + b
            a = a * ap
        a_s[...] = a
        b_s[...] = b
        return r0

    def fwd_chunk(c, h):
        r0 = gates(c, 0)

        def block(k, h):
            h8 = a_s[k] * h + b_s[k]
            hf_s[pl.ds(pl.multiple_of(r0 + k * 8, 8), 8), :] = h8
            return h8[7:8, :]

        return lax.fori_loop(0, nb, block, h, unroll=4)

    h = lax.fori_loop(0, nt, fwd_chunk, h0_ref[0, 0:1, :])
    fin_ref[0, 0:1, :] = h

    def bwd_chunk(ci, h):
        r0 = gates(nt - 1 - ci, 1)

        def block(i, h):
            k = nb - 1 - i
            h8 = a_s[k] * h + b_s[k]
            rows = pl.ds(pl.multiple_of(r0 + k * 8, 8), 8)
            hf_s[rows, :] = hf_s[rows, :] + h8
            return h8[0:1, :]

        h = lax.fori_loop(0, nb, block, h, unroll=4)
        sl = pl.ds(r0, tc)
        out_ref[sl, :] = (hf_s[sl, :] * _gelu_tanh(gate_ref[sl, :].astype(f32))).astype(out_ref.dtype)
        return h

    h = lax.fori_loop(0, nt, bwd_chunk, h0_ref[0, 1:2, :])
    fin_ref[0, 1:2, :] = h


def _block_diag(wb):
    n, k, j = wb.shape
    return (wb[:, :, None, :] * jnp.eye(n, dtype=wb.dtype)[:, None, :, None]).reshape(n * k, n * j)


def _lru_tables(conv_w, conv_b, w_a, b_a, w_x, b_x, lam):
    wg = jnp.stack([jnp.concatenate([_block_diag(w_a[d]), _block_diag(w_x[d])], axis=1) for d in range(2)])
    bg = jnp.concatenate([b_a, b_x], axis=-1)[:, None, :]
    sp = (LRU_C * jax.nn.softplus(-lam.astype(jnp.float32)))[:, None, :]
    return conv_w, conv_b[None, :], wg.astype(jnp.bfloat16), bg, sp


def _lru(p, row0, b, t, h0, tables):
    w = LRU_W
    tc = min(LRU_TC, t)
    cw, cb, wg, bg, sp = tables
    full = lambda a: pl.BlockSpec(a.shape, lambda i: (0,) * a.ndim)
    rb = row0 // t
    return pl.pallas_call(
        functools.partial(_lru_kernel, t_len=t, tc=tc),
        grid=(b,),
        in_specs=[pl.BlockSpec((t, w), lambda i: (rb + i, C_LX // w)),
                  pl.BlockSpec((t, w), lambda i: (rb + i, C_LG // w)),
                  full(cw), full(cb), full(wg), full(bg), full(sp),
                  pl.BlockSpec((1, 2, w), lambda i: (i, 0, 0))],
        out_specs=[pl.BlockSpec((t, w), lambda i: (i, 0)),
                   pl.BlockSpec((1, 2, w), lambda i: (i, 0, 0))],
        out_shape=[jax.ShapeDtypeStruct((b * t, w), jnp.bfloat16),
                   jax.ShapeDtypeStruct((b, 2, w), jnp.float32)],
        scratch_shapes=[pltpu.VMEM((tc // 8, 8, w), jnp.float32), pltpu.VMEM((tc // 8, 8, w), jnp.float32),
                        pltpu.VMEM((t, w), jnp.float32)],
        compiler_params=_cparams(1),
        name="rglru",
    )(p, p, cw, cb, wg, bg, sp, h0)


MLA_TM = 512
ATT_DQ = 256


def _mla_prep_kernel(q_ref, kv_ref, kr_ref, tab_ref, gq_ref, gkv_ref, wuq_ref, wukv_ref,
                     qo_ref, ko_ref, vo_ref, ckv_ref):
    f32, bf = jnp.float32, jnp.bfloat16
    tab = tab_ref[...]

    def rope(blk):
        prod = blk * tab
        return prod + pltpu.roll(prod, 64, 1)

    qn = (_rms(q_ref[...].astype(f32)) * gq_ref[...]).astype(bf)
    qq = jnp.dot(qn, wuq_ref[...], preferred_element_type=f32)
    ckv = _rms(kv_ref[...].astype(f32)) * gkv_ref[...]
    ckv_ref[...] = ckv
    kk = jnp.dot(ckv.astype(bf), wukv_ref[...], preferred_element_type=f32)
    lane = lax.broadcasted_iota(jnp.int32, tab.shape, 1)
    kpe = jnp.where(lane < MLA_ROPE, rope(kr_ref[...].astype(f32)), 0.0).astype(bf)
    for h in range(MLA_HEADS):
        o = h * ATT_DQ
        qo_ref[:, o:o + 128] = (qq[:, o:o + 128] * MLA_SCALE).astype(bf)
        qo_ref[:, o + 128:o + 256] = (rope(qq[:, o + 128:o + 256]) * MLA_SCALE).astype(bf)
        ko_ref[:, o:o + 128] = kk[:, o:o + 128].astype(bf)
        ko_ref[:, o + 128:o + 256] = kpe
        vo_ref[:, h * MLA_V:(h + 1) * MLA_V] = kk[:, o + 128:o + 256].astype(bf)


def _mla_weights(w_uq, w_ukv):
    wq = w_uq.reshape(MLA_Q_RANK, MLA_HEADS, MLA_NOPE + MLA_ROPE)
    pe = wq[:, :, MLA_NOPE:].reshape(MLA_Q_RANK, MLA_HEADS * MLA_ROPE)
    pep = _rot_partner_cols(pe, MLA_HEADS, MLA_ROPE).reshape(MLA_Q_RANK, MLA_HEADS, MLA_ROPE)
    wq_ext = jnp.concatenate([wq, pep], axis=-1).reshape(MLA_Q_RANK, MLA_HEADS * ATT_DQ)
    return wq_ext.astype(jnp.bfloat16), w_ukv.astype(jnp.bfloat16)


def _mla_prep(p, tab, gq, gkv, wuq, wukv):
    n = p.shape[0]
    tm = MLA_TM
    full = lambda a: pl.BlockSpec(a.shape, lambda i: (0,) * a.ndim)
    row = lambda width, col: pl.BlockSpec((tm, width), lambda i: (i, col // width))
    return pl.pallas_call(
        _mla_prep_kernel,
        grid=(n // tm,),
        in_specs=[row(MLA_Q_RANK, C_MQ), row(MLA_KV_RANK, C_MKV), row(128, C_KR), row(128, 0),
                  full(gq), full(gkv), full(wuq), full(wukv)],
        out_specs=[row(MLA_HEADS * ATT_DQ, 0), row(MLA_HEADS * ATT_DQ, 0), row(MLA_HEADS * MLA_V, 0),
                   row(MLA_KV_RANK, 0)],
        out_shape=[jax.ShapeDtypeStruct((n, MLA_HEADS * ATT_DQ), jnp.bfloat16),
                   jax.ShapeDtypeStruct((n, MLA_HEADS * ATT_DQ), jnp.bfloat16),
                   jax.ShapeDtypeStruct((n, MLA_HEADS * MLA_V), jnp.bfloat16),
                   jax.ShapeDtypeStruct((n, MLA_KV_RANK), jnp.float32)],
        compiler_params=_cparams(1),
        name="mla_prep",
    )(p, p, p, tab, gq, gkv, wuq, wukv)


ATT_TQ = 512
ATT_SUB = 256


def _attn_kernel(*refs, two, n_sub):
    dn = (((1,), (1,)), ((), ()))
    f32 = jnp.float32
    if two:
        q_ref, k_ref, v_ref, k2_ref, v2_ref, o_ref = refs
    else:
        q_ref, k_ref, v_ref, o_ref = refs
    rows = q_ref.shape[0] // n_sub
    for part in range(n_sub):
        sl = slice(part * rows, (part + 1) * rows)
        q = q_ref[sl, :]
        s = lax.dot_general(q, k_ref[...], dn, preferred_element_type=f32)
        m = jnp.max(s, axis=-1, keepdims=True)
        if two:
            s2 = lax.dot_general(q, k2_ref[...], dn, preferred_element_type=f32)
            m = jnp.maximum(m, jnp.max(s2, axis=-1, keepdims=True))
        p = jnp.exp(s - m)
        l = jnp.sum(p, axis=-1, keepdims=True)
        o = jnp.dot(p.astype(jnp.bfloat16), v_ref[...], preferred_element_type=f32)
        if two:
            p2 = jnp.exp(s2 - m)
            l = l + jnp.sum(p2, axis=-1, keepdims=True)
            o = o + jnp.dot(p2.astype(jnp.bfloat16), v2_ref[...], preferred_element_type=f32)
        o_ref[sl, :] = (o / l).astype(o_ref.dtype)


def _attention(q, k, v, row0, b, t, k2=None, v2=None):
    tb = min(ATT_TQ, t)
    nq = t // tb
    qb, kb = row0 // tb, row0 // t
    two = k2 is not None
    in_specs = [pl.BlockSpec((tb, ATT_DQ), lambda bi, h, i: (qb + bi * nq + i, h)),
                pl.BlockSpec((t, ATT_DQ), lambda bi, h, i: (kb + bi, h)),
                pl.BlockSpec((t, MLA_V), lambda bi, h, i: (kb + bi, h))]
    args = [q, k, v]
    if two:
        t2 = k2.shape[0] // b
        in_specs += [pl.BlockSpec((t2, ATT_DQ), lambda bi, h, i: (bi, h)),
                     pl.BlockSpec((t2, MLA_V), lambda bi, h, i: (bi, h))]
        args += [k2, v2]
    return pl.pallas_call(
        functools.partial(_attn_kernel, two=two, n_sub=max(tb // ATT_SUB, 1)),
        grid=(b, MLA_HEADS, nq),
        in_specs=in_specs,
        out_specs=pl.BlockSpec((tb, MLA_V), lambda bi, h, i: (bi * nq + i, h)),
        out_shape=jax.ShapeDtypeStruct((b * t, MLA_HEADS * MLA_V), jnp.bfloat16),
        compiler_params=_cparams(3),
        name="mla_attention",
    )(*args)


def _ret_tables():
    f32 = jnp.float32
    c = RET_CHUNK
    log_g = jnp.log1p(-jnp.exp2(-5.0 - jnp.arange(RET_HEADS, dtype=f32)))[:, None, None]
    idx = jnp.arange(c, dtype=f32)
    dec = jnp.exp(jnp.abs(idx[:, None] - idx[None, :])[None] * log_g)
    row = lambda e: jnp.broadcast_to(jnp.exp(e[None, :, None] * log_g), (RET_HEADS, c, c))
    return jnp.stack([dec, row(idx + 1.0), row(c - idx), row(c - 1.0 - idx), row(idx)], axis=0)


RET_HPS = 2


def _ret_kernel(*refs, nc, rope):
    f32, bf = jnp.float32, jnp.bfloat16
    if rope:
        (q_ref, k_ref, v_ref, g_ref, qp_ref, kp_ref, cos_ref, sin_ref, tab_ref, s0_ref, o_ref, fin_ref,
         acc_s, st_s) = refs
    else:
        q_ref, k_ref, v_ref, g_ref, tab_ref, s0_ref, o_ref, fin_ref, acc_s, st_s = refs
    c, dk = RET_CHUNK, RET_DK
    nt_dims, tn_dims = (((1,), (1,)), ((), ())), (((0,), (0,)), ((), ()))

    def chunk(k0, hh):
        sl = pl.ds(pl.multiple_of(k0 * c, c), c)
        cols = slice(hh * dk, (hh + 1) * dk)
        qc, kc = q_ref[sl, cols].astype(f32), k_ref[sl, cols].astype(f32)
        if rope:
            cos, sin = cos_ref[sl, :], sin_ref[sl, :]
            qc = qc * cos + qp_ref[sl, cols].astype(f32) * sin
            kc = kc * cos + kp_ref[sl, cols].astype(f32) * sin
        return sl, cols, qc.astype(bf), kc * (RET_DK ** -0.5), v_ref[sl, cols]

    st_s[...] = s0_ref[0]
    acc_s[...] = jnp.zeros_like(acc_s)

    def body(i, carry):
        for hh in range(RET_HPS):
            dec, xif, xib, zf, zb = (tab_ref[t, hh] for t in range(5))
            g_chunk = xif[c - 1:c, :]
            sl, cols, qc, kc, vc = chunk(i, hh)
            s = st_s[0, hh]
            sc = lax.dot_general(qc, kc.astype(bf), nt_dims, preferred_element_type=f32) * dec
            acc_s[sl, cols] += (jnp.dot(sc.astype(bf), vc, preferred_element_type=f32)
                                + jnp.dot(qc, s.astype(bf), preferred_element_type=f32) * xif)
            st_s[0, hh] = g_chunk * s + lax.dot_general((kc * zf).astype(bf), vc, tn_dims,
                                                        preferred_element_type=f32)
            sl, cols, qc, kc, vc = chunk(nc - 1 - i, hh)
            s = st_s[1, hh]
            acc_s[sl, cols] += jnp.dot(qc, s.astype(bf), preferred_element_type=f32) * xib
            st_s[1, hh] = g_chunk * s + lax.dot_general((kc * zb).astype(bf), vc, tn_dims,
                                                        preferred_element_type=f32)
        return carry

    lax.fori_loop(0, nc, body, 0)
    fin_ref[0] = st_s[...]

    def finish(i, carry):
        sl = pl.ds(pl.multiple_of(i * c, c), c)
        for hh in range(RET_HPS):
            cols = slice(hh * dk, (hh + 1) * dk)
            g = g_ref[sl, cols].astype(f32)
            o_ref[sl, cols] = (_rms(acc_s[sl, cols]) * (g * _sigmoid(g))).astype(o_ref.dtype)
        return carry

    lax.fori_loop(0, nc, finish, 0)


def _retention(p, row0, b, t, s0, tables, rope_tabs=None):
    nc = t // RET_CHUNK
    rb = row0 // t
    rope = rope_tabs is not None
    wd = RET_HPS * RET_DK
    col = lambda c0: pl.BlockSpec((t, wd), lambda bi, h: (rb + bi, c0 // wd + h))
    st = pl.BlockSpec((1, 2, RET_HPS, RET_DK, RET_DV), lambda bi, h: (bi, 0, h, 0, 0))
    in_specs = [col(C_RQ), col(C_RK), col(C_RV), col(C_RG)]
    args = [p, p, p, p]
    if rope:
        tab = pl.BlockSpec((t, RET_DK), lambda bi, h: (0, 0))
        in_specs += [col(C_RQP), col(C_RKP), tab, tab]
        args += [p, p, rope_tabs[0], rope_tabs[1]]
    in_specs += [pl.BlockSpec((5, RET_HPS, RET_CHUNK, RET_CHUNK), lambda bi, h: (0, h, 0, 0)), st]
    args += [tables, s0]
    return pl.pallas_call(
        functools.partial(_ret_kernel, nc=nc, rope=rope),
        grid=(b, RET_HEADS // RET_HPS),
        in_specs=in_specs,
        out_specs=[pl.BlockSpec((t, wd), lambda bi, h: (bi, h)), st],
        out_shape=[jax.ShapeDtypeStruct((b * t, RET_HEADS * RET_DV), jnp.bfloat16),
                   jax.ShapeDtypeStruct((b, 2, RET_HEADS, RET_DK, RET_DV), jnp.float32)],
        scratch_shapes=[pltpu.VMEM((t, wd), jnp.float32), pltpu.VMEM((2, RET_HPS, RET_DK, RET_DV), jnp.float32)],
        compiler_params=_cparams(2, V7X_VMEM_LIMIT_BIG),
        name="retention",
    )(*args)


OUT_TM = 512
ROUTER_PAD = 128


ROW_TILE = (8, D_MODEL // 16)


def _pack_halves(x):
    w = x.shape[1] // 2
    bits = lambda a: lax.bitcast_convert_type(a.astype(jnp.bfloat16).astype(jnp.float32), jnp.int32)
    return bits(x[:, w:]) | lax.shift_right_logical(bits(x[:, :w]), 16)


def _unpack_halves(u):
    lo = lax.bitcast_convert_type(lax.shift_left(u, 16), jnp.float32)
    hi = lax.bitcast_convert_type(u & jnp.int32(-65536), jnp.float32)
    return jnp.concatenate([lo, hi], axis=1)


def _split_bf16(x):
    hi = x.astype(jnp.bfloat16)
    return hi, (x - hi.astype(jnp.float32)).astype(jnp.bfloat16)


def _route(h, whi_ref, wlo_ref, rb_ref):
    f32 = jnp.float32
    hi, lo = _split_bf16(h)
    dn = (((1,), (1,)), ((), ()))
    lt = (lax.dot_general(whi_ref[...], hi, dn, preferred_element_type=f32)
          + lax.dot_general(whi_ref[...], lo, dn, preferred_element_type=f32)
          + lax.dot_general(wlo_ref[...], hi, dn, preferred_element_type=f32))[:N_EXPERTS]
    m = jnp.max(lt, axis=0, keepdims=True)
    e = jnp.exp(lt - m)
    sc = e / jnp.sum(e, axis=0, keepdims=True)
    sel = sc + rb_ref[...][:N_EXPERTS, 0:1]
    rows = lambda a: [a[j:j + 1, :] for j in range(N_EXPERTS)]
    sel_r, sc_r = rows(sel), rows(sc)
    epg = EXPERTS_PER_GROUP

    def top2sum(a, b, c, d):
        h1, l1, h2, l2 = jnp.maximum(a, b), jnp.minimum(a, b), jnp.maximum(c, d), jnp.minimum(c, d)
        return jnp.maximum(h1, h2) + jnp.maximum(jnp.minimum(h1, h2), jnp.maximum(l1, l2))

    gs = [top2sum(*sel_r[g * epg:(g + 1) * epg]) for g in range(N_EXPERT_GROUPS)]
    best, gi = gs[0], jnp.zeros_like(gs[0], dtype=jnp.int32)
    for g in range(1, N_EXPERT_GROUPS):
        upd = gs[g] > best
        gi = jnp.where(upd, g, gi)
        best = jnp.where(upd, gs[g], best)

    def pick(r, j):
        out = r[j]
        for g in range(1, N_EXPERT_GROUPS):
            out = jnp.where(gi == g, r[g * epg + j], out)
        return out

    v = [pick(sel_r, j) for j in range(epg)]
    s = [pick(sc_r, j) for j in range(epg)]

    def argmax_first(vals):
        bv, bi = vals[0], jnp.zeros_like(gi)
        for j in range(1, epg):
            upd = vals[j] > bv
            bi = jnp.where(upd, j, bi)
            bv = jnp.where(upd, vals[j], bv)
        return bi

    i1 = argmax_first(v)
    neg = jnp.float32(-jnp.inf)
    i2 = argmax_first([jnp.where(i1 == j, neg, v[j]) for j in range(epg)])
    take = lambda i: sum(jnp.where(i == j, s[j], 0.0) for j in range(epg))
    w1, w2 = take(i1), take(i2)
    tot = w1 + w2
    return (jnp.concatenate([gi * epg + i1, gi * epg + i2], axis=0),
            jnp.concatenate([w1 / tot, w2 / tot], axis=0))


def _mixout_kernel(ylin_ref, u_ref, ymla_c, ymla_l, ylru_c, ylru_l, yret_c, yret_l, x_c, x_l, mod_ref, d_ref,
                   wglu_ref, wout_ref, g2n_ref, whi_ref, wlo_ref, rb_ref, xo_ref, h2_ref, idx_ref, wgt_ref,
                   *, n_ctx_tiles):
    f32, bf = jnp.float32, jnp.bfloat16
    w = GROUP_W
    is_ctx = pl.program_id(0) < n_ctx_tiles
    pick = lambda c_ref, l_ref: jnp.where(is_ctx, c_ref[...], l_ref[...])
    y = _gelu_tanh(ylin_ref[...].astype(f32) + d_ref[...] * u_ref[...].astype(f32))
    y5 = (y * _sigmoid(jnp.dot(y.astype(bf), wglu_ref[...], preferred_element_type=f32))).astype(bf)
    mix = (jnp.dot(y5, wout_ref[0:w, :], preferred_element_type=f32)
           + jnp.dot(pick(ymla_c, ymla_l), wout_ref[w:2 * w, :], preferred_element_type=f32)
           + jnp.dot(pick(ylru_c, ylru_l), wout_ref[2 * w:3 * w, :], preferred_element_type=f32)
           + jnp.dot(pick(yret_c, yret_l), wout_ref[3 * w:4 * w, :], preferred_element_type=f32))
    x = pick(x_c, x_l) + mod_ref[0, 2:3, :] * mix
    xo_ref[...] = x
    h2 = _rms(x) * g2n_ref[...] * (1.0 + mod_ref[0, 4:5, :]) + mod_ref[0, 3:4, :]
    h2_ref[...] = _pack_halves(h2).reshape(h2_ref.shape)
    idx, wgt = _route(h2, whi_ref, wlo_ref, rb_ref)
    idx_ref[...] = idx
    wgt_ref[...] = wgt


def _router_weights(router_w, router_b):
    d = router_w.shape[0]
    wt = jnp.zeros((ROUTER_PAD, d), jnp.float32).at[:N_EXPERTS].set(router_w.T.astype(jnp.float32))
    whi, wlo = _split_bf16(wt)
    rb = jnp.zeros((ROUTER_PAD, 128), jnp.float32).at[:N_EXPERTS].set(router_b.astype(jnp.float32)[:, None])
    return whi, wlo, rb


def _mixout(ylin, p, ymla, ylru, yret, x, mods, s5_d, wglu, wout, g2n, rw, n_ctx, dec_seq):
    n, d = x[0].shape[0] + x[1].shape[0], x[0].shape[1]
    tm = OUT_TM
    w = GROUP_W
    na = n_ctx // tm
    seq = functools.partial(_seq_of_tile, tm=tm, n_ctx=n_ctx, dec_seq=dec_seq)
    full = lambda a: pl.BlockSpec(a.shape, lambda i: (0,) * a.ndim)
    row = lambda width: pl.BlockSpec((tm, width), lambda i: (i, 0))
    ctx_row, lat_row = _two_part_rows(tm, w, na)
    lanes = pl.BlockSpec((TOP_K, tm), lambda i: (0, i))
    whi, wlo, rb = rw
    return pl.pallas_call(
        functools.partial(_mixout_kernel, n_ctx_tiles=na),
        grid=(n // tm,),
        in_specs=[row(w), row(w), ctx_row, lat_row, ctx_row, lat_row, ctx_row, lat_row, *_two_part_rows(tm, d, na),
                  pl.BlockSpec((1, N_MOD, d), lambda i: (seq(i), 0, 0)),
                  full(s5_d), full(wglu), full(wout), full(g2n), full(whi), full(wlo), full(rb)],
        out_specs=[row(d), pl.BlockSpec((tm,) + ROW_TILE, lambda i: (i, 0, 0)), lanes, lanes],
        out_shape=[jax.ShapeDtypeStruct((n, d), jnp.float32), jax.ShapeDtypeStruct((n,) + ROW_TILE, jnp.int32),
                   jax.ShapeDtypeStruct((TOP_K, n), jnp.int32), jax.ShapeDtypeStruct((TOP_K, n), jnp.float32)],
        compiler_params=_cparams(1, V7X_VMEM_LIMIT_BIG),
        name="mix_out_norm_route",
    )(ylin, p, *ymla, *ylru, *yret, *x, mods, s5_d, wglu, wout, g2n, whi, wlo, rb)


MOE_TM = 256
MOE_NB = 256


def _expert_kernel(te_ref, nt_ref, nv_ref, tok0_ref, tok1_ref, w_ref, dstp_ref, dst_ref, wg_ref, wu_ref, wd_ref,
                   h_hbm, y_hbm, wg_s, wu_s, wd_s, xbuf, xs_s, act_s, ybuf, gsem, ssem):
    i = pl.program_id(0)
    bf = jnp.bfloat16
    tm = xs_s.shape[0]
    nt = nt_ref[0]

    def gather_start(tok_ref, sl, unroll):
        def row(r, carry):
            pltpu.make_async_copy(h_hbm.at[tok_ref[0, 0, r]], xbuf.at[sl, r], gsem.at[sl]).start()
            return carry

        lax.fori_loop(0, tm, row, 0, unroll=unroll)

    def gather_done(sl):
        pltpu.make_async_copy(h_hbm.at[pl.ds(0, tm)], xbuf.at[sl], gsem.at[sl]).wait()

    def scatter_row(d_ref, sl, r):
        return pltpu.make_async_copy(ybuf.at[sl, r], y_hbm.at[d_ref[0, 0, r]], ssem.at[sl])

    def scatter_done(sl, n_rows):
        pltpu.make_async_copy(ybuf.at[sl, pl.ds(0, n_rows)], y_hbm.at[pl.ds(0, n_rows)], ssem.at[sl]).wait()

    def tile(slot):
        n_prev = jnp.where(i > 0, nv_ref[jnp.maximum(i - 1, 0)], 0)
        f = wg_s.shape[1]
        d = wd_s.shape[1]
        up_pieces, down_pieces = 2 * (f // MOE_NB), d // MOE_NB

        def gather_rows(piece):
            for r in range(piece * tm // up_pieces, (piece + 1) * tm // up_pieces):
                pltpu.make_async_copy(h_hbm.at[tok1_ref[0, 0, r]], xbuf.at[1 - slot, r],
                                      gsem.at[1 - slot]).start(priority=1)

        def scatter_rows(piece):
            for r in range(piece * tm // down_pieces, (piece + 1) * tm // down_pieces):
                @pl.when(r < n_prev)
                def _():
                    scatter_row(dstp_ref, 1 - slot, r).start(priority=r % 2)

        gather_done(slot)
        xs_s[...] = _unpack_halves(xbuf[slot].reshape(tm, -1)).astype(bf)
        x = xs_s[...]
        gs, us = [], []
        for c in range(f // MOE_NB):
            cols = slice(c * MOE_NB, (c + 1) * MOE_NB)
            gs.append(jnp.dot(x, wg_s[:, cols], preferred_element_type=jnp.float32))
            gather_rows(2 * c)
            us.append(jnp.dot(x, wu_s[:, cols], preferred_element_type=jnp.float32))
            gather_rows(2 * c + 1)
        g = jnp.concatenate(gs, axis=1)
        u = jnp.concatenate(us, axis=1)
        act_s[...] = ((g * _sigmoid(g)) * u * w_ref[...]).astype(bf)

        @pl.when(nt > 0)
        def _():
            act = act_s[...]
            ys = []
            for c in range(down_pieces):
                ys.append(jnp.dot(act, wd_s[:, c * MOE_NB:(c + 1) * MOE_NB], preferred_element_type=jnp.float32))
                scatter_rows(c)
            y = _pack_halves(jnp.concatenate(ys, axis=1)).reshape(ybuf.shape[1:])

            @pl.when(i > 1)
            def _():
                scatter_done(slot, nv_ref[jnp.maximum(i - 2, 0)])

            ybuf[slot] = y

        @pl.when(i == nt - 1)
        def _():
            def row(r, carry):
                scatter_row(dst_ref, slot, r).start()
                return carry

            lax.fori_loop(0, nv_ref[i], row, 0)

            @pl.when(i > 0)
            def _():
                scatter_done(1 - slot, nv_ref[jnp.maximum(i - 1, 0)])

            scatter_done(slot, nv_ref[i])
            gather_done(1 - slot)

    @pl.when(i < nt)
    def _():
        @pl.when(i == 0)
        def _():
            gather_start(tok0_ref, 0, 8)

        @pl.when((i == 0) | (te_ref[i] != te_ref[jnp.maximum(i - 1, 0)]))
        def _():
            wg_s[...] = wg_ref[0, 0].astype(bf)
            wu_s[...] = wu_ref[0, 0].astype(bf)
            wd_s[...] = wd_ref[0, 0].astype(bf)

        @pl.when(i % 2 == 0)
        def _():
            tile(0)

        @pl.when(i % 2 == 1)
        def _():
            tile(1)


def _experts(h2p, layer, ws, tok, dst, tile_expert, n_tiles_used, n_valid, wg, wu, wd):
    n = h2p.shape[0]
    d = 2 * ROW_TILE[0] * ROW_TILE[1]
    tm = MOE_TM
    n_tiles = tok.shape[0]
    f = wg.shape[-1]
    smem_row = lambda delta: pl.BlockSpec(
        (1, 1, tm), lambda i, te, nt, nv: (jnp.clip(i + delta, 0, n_tiles - 1), 0, 0), memory_space=pltpu.SMEM)
    wspec = lambda a, b: pl.BlockSpec((1, 1, a, b), lambda i, te, nt, nv: (layer, te[i], 0, 0))
    return pl.pallas_call(
        _expert_kernel,
        grid_spec=pltpu.PrefetchScalarGridSpec(
            num_scalar_prefetch=3,
            grid=(n_tiles,),
            in_specs=[smem_row(0), smem_row(1),
                      pl.BlockSpec((tm, 1), lambda i, te, nt, nv: (i, 0)),
                      smem_row(-1), smem_row(0),
                      wspec(d, f), wspec(d, f), wspec(f, d),
                      pl.BlockSpec(memory_space=pl.ANY)],
            out_specs=pl.BlockSpec(memory_space=pl.ANY),
            scratch_shapes=[pltpu.VMEM((d, f), jnp.bfloat16), pltpu.VMEM((d, f), jnp.bfloat16),
                            pltpu.VMEM((f, d), jnp.bfloat16), pltpu.VMEM((2, tm) + ROW_TILE, jnp.int32),
                            pltpu.VMEM((tm, d), jnp.bfloat16), pltpu.VMEM((tm, f), jnp.bfloat16),
                            pltpu.VMEM((2, tm) + ROW_TILE, jnp.int32),
                            pltpu.SemaphoreType.DMA((2,)), pltpu.SemaphoreType.DMA((2,))]),
        out_shape=jax.ShapeDtypeStruct((TOP_K * n,) + ROW_TILE, jnp.int32),
        compiler_params=pltpu.CompilerParams(dimension_semantics=("arbitrary",),
                                             vmem_limit_bytes=V7X_VMEM_LIMIT_BIG, disable_bounds_checks=True),
        name="moe_experts",
    )(tile_expert, n_tiles_used, n_valid, tok, tok, ws, dst, dst, wg, wu, wd, h2p)


def _moe_dispatch(idx, wgt):
    n = idx.shape[1]
    tm = MOE_TM
    n_pairs = TOP_K * n
    m_pad = n_pairs + N_EXPERTS * tm
    n_tiles = m_pad // tm
    e_flat = idx.reshape(-1)
    order = jnp.argsort(e_flat, stable=True).astype(jnp.int32)
    experts = jnp.arange(N_EXPERTS, dtype=jnp.int32)
    counts = jnp.sum((e_flat[None, :] == experts[:, None]).astype(jnp.int32), axis=1)
    starts_raw = jnp.cumsum(counts) - counts
    padded = ((counts + tm - 1) // tm) * tm
    ends = jnp.cumsum(padded)
    starts_pad = ends - padded
    tile_start = jnp.arange(n_tiles, dtype=jnp.int32) * tm
    tile_expert = jnp.minimum(jnp.sum((tile_start[:, None] >= ends[None, :]).astype(jnp.int32), axis=1),
                              N_EXPERTS - 1)
    r = jnp.arange(m_pad, dtype=jnp.int32)
    te_r = jnp.repeat(tile_expert, tm)
    off = r - jnp.take(starts_pad, te_r)
    valid = (off >= 0) & (off < jnp.take(counts, te_r))
    pair = jnp.take(order, jnp.clip(jnp.take(starts_raw, te_r) + off, 0, n_pairs - 1))
    tok = jnp.where(valid, pair % n, 0).reshape(n_tiles, 1, tm)
    dst = jnp.where(valid, pair, 0).reshape(n_tiles, 1, tm)
    ws = jnp.where(valid, jnp.take(wgt.reshape(-1), pair), 0.0)[:, None]
    n_tiles_used = (ends[-1] // tm).astype(jnp.int32).reshape(1)
    n_valid = jnp.sum(valid.reshape(n_tiles, tm), axis=1).astype(jnp.int32)
    return tok, dst, ws, tile_expert, n_tiles_used, n_valid


def _moe_rows(h2p, layer, idx, wgt, wg, wu, wd):
    tok, dst, ws, tile_expert, n_tiles_used, n_valid = _moe_dispatch(idx, wgt)
    return _experts(h2p, layer, ws, tok, dst, tile_expert, n_tiles_used, n_valid, wg, wu, wd)


RES_TM = 512


def _resid_kernel(x_ref, ya_ref, yb_ref, mod_ref, gf_ref, oc_ref, ol_ref, *, final, n_ctx_tiles):
    rows = lambda r: _unpack_halves(r[...].reshape(r.shape[0], -1))
    x = x_ref[...] + mod_ref[0, 5:6, :] * (rows(ya_ref) + rows(yb_ref))
    y = _rms(x) * gf_ref[...] if final else x
    i = pl.program_id(0)

    @pl.when(i < n_ctx_tiles)
    def _():
        oc_ref[...] = y

    @pl.when(i >= n_ctx_tiles)
    def _():
        ol_ref[...] = y


def _resid(x, y2, mods, gf, n_ctx, dec_seq, final):
    n, d = x.shape
    tm = RES_TM
    na = n_ctx // tm
    seq = functools.partial(_seq_of_tile, tm=tm, n_ctx=n_ctx, dec_seq=dec_seq)
    row = pl.BlockSpec((tm, d), lambda i: (i, 0))
    return pl.pallas_call(
        functools.partial(_resid_kernel, final=final, n_ctx_tiles=na),
        grid=(n // tm,),
        in_specs=[row, pl.BlockSpec((tm,) + ROW_TILE, lambda i: (i, 0, 0)),
                  pl.BlockSpec((tm,) + ROW_TILE, lambda i: (n // tm + i, 0, 0)),
                  pl.BlockSpec((1, N_MOD, d), lambda i: (seq(i), 0, 0)),
                  pl.BlockSpec((1, d), lambda i: (0, 0))],
        out_specs=list(_two_part_rows(tm, d, na)),
        out_shape=[jax.ShapeDtypeStruct((n_ctx, d), jnp.float32), jax.ShapeDtypeStruct((n - n_ctx, d), jnp.float32)],
        compiler_params=_cparams(1),
        name="moe_residual_norm",
    )(x, y2, y2, mods, gf)


def kernel(x_prompt, x_sample, c, cache_mla_ckv, cache_mla_kpe, state_s5, state_lru, state_ret,
           c_ctx, w_ada, b_ada, norm1_g, norm2_g, w_in, w_out,
           s5_a_re, s5_a_im, s5_log_dt, s5_b_re, s5_b_im, s5_c_re, s5_c_im, s5_d, s5_w_glu,
           mla_q_norm_g, mla_w_uq, mla_kv_norm_g, mla_w_ukv,
           lru_conv_w, lru_conv_b, lru_w_a, lru_b_a, lru_w_x, lru_b_x, lru_lambda,
           router_w, router_b, moe_w_gate, moe_w_up, moe_w_down, final_norm_g):
    f32, bf = jnp.float32, jnp.bfloat16
    batch, seq_len, d = x_prompt.shape
    dec_batch, dec_seq, _ = x_sample.shape
    past = cache_mla_ckv.shape[2]
    n_ctx, n_lat = batch * seq_len, dec_batch * dec_seq
    depth = w_in.shape[0]

    x = (x_prompt.reshape(n_ctx, d), x_sample.reshape(n_lat, d))
    cvec = jnp.zeros((8, d), f32).at[0].set(c_ctx).at[1:1 + dec_batch].set(c)
    mods_all = _ada(cvec, w_ada, b_ada).reshape(depth, 8, N_MOD, d)

    cos64, sin64 = _rope_tables(dec_seq, MLA_ROPE)
    lat_tab = jnp.tile(jnp.concatenate([cos64, sin64], axis=1), (dec_batch, 1))
    ctx_tab = jnp.concatenate([jnp.ones((n_ctx, MLA_ROPE), f32), jnp.zeros((n_ctx, MLA_ROPE), f32)], axis=1)
    mla_tab = jnp.concatenate([ctx_tab, lat_tab], axis=0)
    ret_rope = _rope_tables(dec_seq, RET_DK)
    ret_tables = _ret_tables()
    rw = _router_weights(router_w, router_b)
    gf = final_norm_g[None, :]

    states = []
    for l in range(depth):
        mods = mods_all[l]
        s5_tab = _s5_tables(s5_a_re[l], s5_a_im[l], s5_log_dt[l], s5_b_re[l], s5_b_im[l], s5_c_re[l], s5_c_im[l])
        lru_tab = _lru_tables(lru_conv_w[l], lru_conv_b[l], lru_w_a[l], lru_b_a[l], lru_w_x[l], lru_b_x[l],
                              lru_lambda[l])
        wuq, wukv = _mla_weights(mla_w_uq[l], mla_w_ukv[l])

        p = _inproj(x, mods, norm1_g[l][None, :], _inproj_weights(w_in[l]), n_ctx, dec_seq)

        u_ctx = p[:n_ctx, C_S5:C_S5 + S5_CH].reshape(batch, seq_len, S5_CH)
        u_lat = p[n_ctx:, C_S5:C_S5 + S5_CH].reshape(dec_batch, dec_seq, S5_CH)
        ylin, s5_fins = _s5_core([u_ctx, u_lat], [None, state_s5[:, l]], s5_tab)

        qo, ko, vo, ckv = _mla_prep(p, mla_tab, mla_q_norm_g[l][None, :], mla_kv_norm_g[l][None, :], wuq, wukv)
        ckv_c = cache_mla_ckv[:, l].reshape(dec_batch * past, MLA_KV_RANK)
        kv_c = _matmul(ckv_c, wukv, out_dtype=bf).reshape(dec_batch * past, MLA_HEADS, 2, 128)
        kpe_c = jnp.broadcast_to(cache_mla_kpe[:, l].reshape(dec_batch * past, 1, MLA_ROPE).astype(bf),
                                 (dec_batch * past, MLA_HEADS, MLA_ROPE))
        k_c = jnp.concatenate([kv_c[:, :, 0], kpe_c, jnp.zeros_like(kpe_c)], axis=-1).reshape(
            dec_batch * past, MLA_HEADS * ATT_DQ)
        v_c = kv_c[:, :, 1].reshape(dec_batch * past, MLA_HEADS * MLA_V)
        ymla = (_attention(qo, ko, vo, 0, batch, seq_len),
                _attention(qo, ko, vo, n_ctx, dec_batch, dec_seq, k_c, v_c))

        ylru_c, lru_fin = _lru(p, 0, batch, seq_len, jnp.zeros((batch, 2, LRU_W), f32), lru_tab)
        ylru_l, _ = _lru(p, n_ctx, dec_batch, dec_seq, state_lru[:, l].astype(f32), lru_tab)
        ylru = (ylru_c, ylru_l)

        yret_c, ret_fin = _retention(p, 0, batch, seq_len,
                                     jnp.zeros((batch, 2, RET_HEADS, RET_DK, RET_DV), f32), ret_tables)
        yret_l, _ = _retention(p, n_ctx, dec_batch, dec_seq, state_ret[:, l].astype(f32), ret_tables, ret_rope)
        yret = (yret_c, yret_l)

        x_mid, h2, idx, wgt = _mixout(ylin, p, ymla, ylru, yret, x, mods, s5_d[l][None, :],
                                      s5_w_glu[l].astype(bf), w_out[l].astype(bf), norm2_g[l][None, :], rw,
                                      n_ctx, dec_seq)
        y2 = _moe_rows(h2, l, idx, wgt, moe_w_gate, moe_w_up, moe_w_down)
        x = _resid(x_mid, y2, mods, gf, n_ctx, dec_seq, final=(l == depth - 1))

        states.append((ckv[:n_ctx].reshape(batch, seq_len, MLA_KV_RANK),
                       p[:n_ctx, C_KR:C_KR + MLA_ROPE].astype(f32).reshape(batch, seq_len, MLA_ROPE),
                       s5_fins[0], lru_fin, ret_fin))

    y_prompt = x[0].reshape(batch, seq_len, d)
    y_sample = x[1].reshape(dec_batch, dec_seq, d)
    new_cache_mla_ckv = jnp.stack([st[0] for st in states], axis=1)
    new_cache_mla_kpe = jnp.stack([st[1] for st in states], axis=1)
    new_state_s5 = jnp.stack([st[2] for st in states], axis=1)
    new_state_lru = jnp.stack([st[3] for st in states], axis=1)
    new_state_ret = jnp.stack([st[4] for st in states], axis=1)
    return (y_prompt, y_sample, new_cache_mla_ckv, new_cache_mla_kpe, new_state_s5, new_state_lru, new_state_ret)
```

```python
import functools
import numpy as np
import jax
import jax.numpy as jnp
from jax import lax
from jax.experimental import pallas as pl
from jax.experimental.pallas import tpu as pltpu

D_MODEL = 2048
DEPTH = 2
GRID_W = 64
EPS = 1e-6
ROPE_BASE = 10000.0
N_MOD = 6
GROUP_W = 512
S5_CH = GROUP_W
S5_GROUP_CH = 16
S5_GROUPS = S5_CH // S5_GROUP_CH
S5_STATE = 64
MLA_HEADS = 4
MLA_NOPE = 128
MLA_ROPE = 64
MLA_V = 128
MLA_Q_RANK = GROUP_W
MLA_KV_RANK = GROUP_W // 2
MLA_SCALE = (MLA_NOPE + MLA_ROPE) ** -0.5
LRU_W = GROUP_W
LRU_CONV = 4
LRU_C = 8.0
RET_HEADS = 4
RET_DK = 128
RET_DV = 128
RET_CHUNK = 128
N_EXPERTS = 16
N_EXPERT_GROUPS = 4
EXPERTS_PER_GROUP = N_EXPERTS // N_EXPERT_GROUPS
TOP_K = 2
D_EXPERT = D_MODEL // 4

V7X_VMEM_LIMIT = 48 * 1024 * 1024
V7X_VMEM_LIMIT_BIG = 56 * 1024 * 1024

C_S5, C_MQ, C_LX, C_LG, C_RQ, C_RK, C_RV, C_RG, C_RQP, C_RKP, C_MKV, C_KR = (
    0, 512, 1024, 1536, 2048, 2560, 3072, 3584, 4096, 4608, 5120, 5376)
P_COLS = 5632
R_S5, R_MQ, R_MKV, R_KR, R_LX, R_LG, R_RQ, R_RK, R_RV, R_RG = (0, 512, 1024, 1280, 1344, 1856, 2368, 2880, 3392, 3904)


def _cparams(n_axes, limit=V7X_VMEM_LIMIT):
    return pltpu.CompilerParams(dimension_semantics=("arbitrary",) * n_axes, vmem_limit_bytes=limit)


def _sigmoid(x):
    return 1.0 / (1.0 + jnp.exp(-x))


def _gelu_tanh(x):
    return 0.5 * x * (1.0 + jnp.tanh(0.7978845608028654 * (x + 0.044715 * (x * x * x))))


def _rms(x):
    return x * lax.rsqrt(jnp.mean(x * x, axis=-1, keepdims=True) + EPS)


def _seq_of_tile(i, tm, n_ctx, dec_seq):
    assert n_ctx % tm == 0 and dec_seq % tm == 0, "a row tile must not straddle two sequences"
    r = i * tm
    return jnp.where(r < n_ctx, 0, 1 + (r - n_ctx) // dec_seq)


def _mm_kernel(x_ref, w_ref, o_ref):
    o_ref[...] = jnp.dot(x_ref[...].astype(jnp.bfloat16), w_ref[...].astype(jnp.bfloat16),
                         preferred_element_type=jnp.float32).astype(o_ref.dtype)


def _matmul(x, w, tm=512, tn=512, out_dtype=jnp.float32):
    m, k = x.shape
    _, n = w.shape
    tm, tn = min(tm, m), min(tn, n)
    return pl.pallas_call(
        _mm_kernel,
        grid=(n // tn, m // tm),
        in_specs=[pl.BlockSpec((tm, k), lambda j, i: (i, 0)), pl.BlockSpec((k, tn), lambda j, i: (0, j))],
        out_specs=pl.BlockSpec((tm, tn), lambda j, i: (i, j)),
        out_shape=jax.ShapeDtypeStruct((m, n), out_dtype),
        compiler_params=_cparams(2),
        name="matmul",
    )(x, w)


ADA_TN = 1024


def _ada_kernel(c_ref, w_ref, b_ref, o_ref):
    c = c_ref[...]
    s = (c * _sigmoid(c)).astype(jnp.bfloat16)
    o_ref[0] = jnp.dot(s, w_ref[0].astype(jnp.bfloat16), preferred_element_type=jnp.float32) + b_ref[0]


def _ada(cvec, w_ada, b_ada):
    depth, d, n = w_ada.shape
    return pl.pallas_call(
        _ada_kernel,
        grid=(depth, n // ADA_TN),
        in_specs=[pl.BlockSpec((8, d), lambda l, j: (0, 0)),
                  pl.BlockSpec((1, d, ADA_TN), lambda l, j: (l, 0, j)),
                  pl.BlockSpec((1, 1, ADA_TN), lambda l, j: (l, 0, j))],
        out_specs=pl.BlockSpec((1, 8, ADA_TN), lambda l, j: (l, 0, j)),
        out_shape=jax.ShapeDtypeStruct((depth, 8, n), jnp.float32),
        compiler_params=_cparams(2),
        name="adaln_mod",
    )(cvec, w_ada, b_ada.reshape(depth, 1, n))


IN_TM = 1024
IN_TN = 512


def _inproj_kernel(xc_ref, xl_ref, mod_ref, g_ref, w_ref, o_ref, h_s, *, n_ctx_tiles):
    @pl.when(pl.program_id(1) == 0)
    def _():
        x = jnp.where(pl.program_id(0) < n_ctx_tiles, xc_ref[...], xl_ref[...])
        h = _rms(x) * g_ref[...]
        h_s[...] = (h * (1.0 + mod_ref[0, 1:2, :]) + mod_ref[0, 0:1, :]).astype(h_s.dtype)

    j = pl.program_id(1)
    unused = (pl.program_id(0) < n_ctx_tiles) & (j >= C_RQP // IN_TN) & (j < C_MKV // IN_TN)

    @pl.when(unused)
    def _():
        o_ref[...] = jnp.zeros_like(o_ref)

    @pl.when(jnp.logical_not(unused))
    def _():
        o_ref[...] = jnp.dot(h_s[...], w_ref[...], preferred_element_type=jnp.float32).astype(o_ref.dtype)


def _two_part_rows(tm, width, n_ctx_tiles):
    return (pl.BlockSpec((tm, width), lambda i, *_: (jnp.minimum(i, n_ctx_tiles - 1), 0)),
            pl.BlockSpec((tm, width), lambda i, *_: (jnp.maximum(i - n_ctx_tiles, 0), 0)))


def _inproj(x, mods, g, w, n_ctx, dec_seq):
    n, d = x[0].shape[0] + x[1].shape[0], x[0].shape[1]
    na = n_ctx // IN_TM
    seq = functools.partial(_seq_of_tile, tm=IN_TM, n_ctx=n_ctx, dec_seq=dec_seq)
    return pl.pallas_call(
        functools.partial(_inproj_kernel, n_ctx_tiles=na),
        grid=(n // IN_TM, P_COLS // IN_TN),
        in_specs=[*_two_part_rows(IN_TM, d, na),
                  pl.BlockSpec((1, N_MOD, d), lambda i, j: (seq(i), 0, 0)),
                  pl.BlockSpec((1, d), lambda i, j: (0, 0)),
                  pl.BlockSpec((d, IN_TN), lambda i, j: (0, j))],
        out_specs=pl.BlockSpec((IN_TM, IN_TN), lambda i, j: (i, j)),
        out_shape=jax.ShapeDtypeStruct((n, P_COLS), jnp.bfloat16),
        scratch_shapes=[pltpu.VMEM((IN_TM, d), jnp.bfloat16)],
        compiler_params=_cparams(2, V7X_VMEM_LIMIT_BIG),
        name="norm_inproj",
    )(*x, mods, g, w)


def _rot_partner_cols(w, n_heads, head_dim):
    q = head_dim // 4
    wr = w.reshape(w.shape[0], n_heads, 2, 2, q)
    return jnp.stack([-wr[:, :, :, 1], wr[:, :, :, 0]], axis=3).reshape(w.shape)


def _inproj_weights(w_in):
    seg = lambda off, width: w_in[:, off:off + width]
    kr = seg(R_KR, MLA_ROPE)
    cols = [seg(R_S5, 2 * GROUP_W),
            seg(R_LX, 6 * GROUP_W),
            _rot_partner_cols(seg(R_RQ, 2 * GROUP_W), 2 * RET_HEADS, RET_DK),
            seg(R_MKV, MLA_KV_RANK), kr, _rot_partner_cols(kr, 1, MLA_ROPE),
            jnp.zeros((w_in.shape[0], P_COLS - C_KR - 2 * MLA_ROPE), w_in.dtype)]
    return jnp.concatenate(cols, axis=1).astype(jnp.bfloat16)


def _rope_tables(t_len, rot_dim):
    rows = t_len // GRID_W
    row = jnp.repeat(jnp.arange(rows, dtype=jnp.float32), GRID_W)
    col = jnp.tile(jnp.arange(GRID_W, dtype=jnp.float32), rows)
    n_freq = rot_dim // 4
    inv = ROPE_BASE ** (-jnp.arange(n_freq, dtype=jnp.float32) / n_freq)
    ang = jnp.concatenate([row[:, None] * inv[None]] * 2 + [col[:, None] * inv[None]] * 2, axis=1)
    return jnp.cos(ang), jnp.sin(ang)


S5_L = 32
S5_PAIRS = S5_GROUPS // 2
S5_Q = 4
S5_LANES = S5_GROUPS * S5_STATE
S5_SCAN_LANES = 1024


def _s5_tables(a_re, a_im, log_dt, b_re, b_im, c_re, c_im):
    f32 = jnp.float32
    L, G, P, C = S5_L, S5_GROUPS, S5_STATE, S5_GROUP_CH
    hp = lax.Precision.HIGHEST
    cmul = lambda xr, xi, yr, yi: (xr * yr - xi * yi, xr * yi + xi * yr)
    dt = jnp.exp(log_dt.astype(f32))[..., None]
    zr, zi = a_re * dt, a_im * dt
    ab_r, ab_i = jnp.exp(zr) * jnp.cos(zi), jnp.exp(zr) * jnp.sin(zi)
    den = a_re * a_re + a_im * a_im
    nr, ni = ab_r - 1.0, ab_i
    be_r = (nr * a_re + ni * a_im) / den
    be_i = (ni * a_re - nr * a_im) / den
    bt_r, bt_i = b_re.transpose(0, 2, 1), b_im.transpose(0, 2, 1)
    bp_r, bp_i = cmul(be_r[:, :, None, :], be_i[:, :, None, :], bt_r[None], bt_i[None])
    tau = jnp.arange(L + 1, dtype=f32)[None, None, :, None]
    mag = jnp.exp(zr[:, :, None, :] * tau)
    pw_r, pw_i = mag * jnp.cos(zi[:, :, None, :] * tau), mag * jnp.sin(zi[:, :, None, :] * tau)

    cat = lambda *xs: jnp.concatenate(xs, axis=-1)
    pf_r, pf_i = pw_r[0][:, :L][:, ::-1], pw_i[0][:, :L][:, ::-1]
    pb_r, pb_i = pw_r[1][:, :L], pw_i[1][:, :L]
    w1 = cat(pf_r, pf_i, pb_r, pb_i)[:, :, None, :]
    w2 = cat(pf_i, pf_r, pb_i, pb_r)[:, :, None, :]
    b1 = cat(bp_r[0], bp_r[0], bp_r[1], bp_r[1])[:, None]
    b2 = cat(-bp_i[0], bp_i[0], -bp_i[1], bp_i[1])[:, None]
    ws = (w1 * b1 + w2 * b2).reshape(G, L * C, S5_Q * P)

    ct_r, ct_i = c_re.transpose(0, 2, 1), c_im.transpose(0, 2, 1)
    pt_r, pt_i = pw_r.transpose(0, 1, 3, 2), pw_i.transpose(0, 1, 3, 2)

    def cm(er, ei):
        m_r, m_i = cmul(ct_r[:, :, None, :], ct_i[:, :, None, :], er[..., None], ei[..., None])
        return m_r.reshape(G, P, L * C), m_i.reshape(G, P, L * C)

    mf_r, mf_i = cm(pt_r[0][:, :, 1:L + 1], pt_i[0][:, :, 1:L + 1])
    mb_r, mb_i = cm(pt_r[1][:, :, 1:L + 1][:, :, ::-1], pt_i[1][:, :, 1:L + 1][:, :, ::-1])
    wo = jnp.stack([mf_r, -mf_i, mb_r, -mb_i], axis=1).reshape(G, S5_Q * P, L * C)

    def impulse(d):
        m_r, m_i = cm(pt_r[d][:, :, :L], pt_i[d][:, :, :L])
        kk = (jnp.einsum('gkp,gpx->gkx', bp_r[d], m_r, precision=hp)
              - jnp.einsum('gkp,gpx->gkx', bp_i[d], m_i, precision=hp))
        return kk.reshape(G, C, L, C)

    kf, kb = impulse(0), impulse(1)
    wide = jnp.concatenate([kb[:, :, :0:-1], kf[:, :, :1] + kb[:, :, :1], kf[:, :, 1:]], axis=2)
    wide = jnp.pad(wide.reshape(G, C, (2 * L - 1) * C), ((0, 0), (0, 0), (0, C)))

    a_l = jnp.stack([pw_r[0, :, L], pw_i[0, :, L], pw_r[1, :, L], pw_i[1, :, L]], axis=0).reshape(S5_Q, 1, S5_LANES)
    bf = jnp.bfloat16
    return _s5_toeplitz(wide), ws.astype(bf), wo.astype(bf), a_l


def _s5_toeplitz_kernel(w_ref, o_ref):
    w = w_ref[0]
    for s in range(S5_L):
        off = (S5_L - 1 - s) * S5_GROUP_CH
        o_ref[0, s * S5_GROUP_CH:(s + 1) * S5_GROUP_CH, :] = w[:, off:off + S5_L * S5_GROUP_CH].astype(o_ref.dtype)


def _s5_toeplitz(wide):
    g, c, wl = wide.shape
    n = S5_L * S5_GROUP_CH
    return pl.pallas_call(
        _s5_toeplitz_kernel,
        grid=(g,),
        in_specs=[pl.BlockSpec((1, c, wl), lambda i: (i, 0, 0))],
        out_specs=pl.BlockSpec((1, n, n), lambda i: (i, 0, 0)),
        out_shape=jax.ShapeDtypeStruct((g, n, n), jnp.bfloat16),
        compiler_params=_cparams(1),
        name="s5_toeplitz",
    )(wide)


def _s5a_kernel(x_ref, t_ref, ws_ref, y1_ref, s_ref):
    ss = []
    for j in range(2):
        x = x_ref[j]
        y1_ref[j] = jnp.dot(x, t_ref[j], preferred_element_type=jnp.float32)
        ss.append(jnp.dot(x, ws_ref[j], preferred_element_type=jnp.float32))
    p = S5_STATE
    for q in range(S5_Q):
        s_ref[q] = jnp.concatenate([ss[0][:, q * p:(q + 1) * p], ss[1][:, q * p:(q + 1) * p]], axis=1)


def _s5a(x, toe, ws):
    g, r, w = x.shape
    return pl.pallas_call(
        _s5a_kernel,
        grid=(S5_PAIRS,),
        in_specs=[pl.BlockSpec((2, r, w), lambda i: (i, 0, 0)),
                  pl.BlockSpec((2, w, w), lambda i: (i, 0, 0)),
                  pl.BlockSpec((2, w, S5_Q * S5_STATE), lambda i: (i, 0, 0))],
        out_specs=[pl.BlockSpec((2, r, w), lambda i: (i, 0, 0)),
                   pl.BlockSpec((S5_Q, r, 128), lambda i: (0, 0, i))],
        out_shape=[jax.ShapeDtypeStruct((g, r, w), jnp.float32),
                   jax.ShapeDtypeStruct((S5_Q, r, S5_LANES), jnp.float32)],
        compiler_params=_cparams(1),
        name="s5_chunk_local",
    )(x, toe, ws)


def _s5b_kernel(s_ref, a_ref, h0_ref, hp_ref, fin_ref, *, nc):
    def run(qr, qi, order_fwd):
        ar, ai = a_ref[qr], a_ref[qi]

        def body(i, carry):
            hr, hi = carry
            k = i if order_fwd else nc - 1 - i
            hp_ref[qr, pl.ds(k, 1), :] = hr
            hp_ref[qi, pl.ds(k, 1), :] = hi
            nr = ar * hr - ai * hi + s_ref[qr, pl.ds(k, 1), :]
            ni = ar * hi + ai * hr + s_ref[qi, pl.ds(k, 1), :]
            return nr, ni

        hr, hi = lax.fori_loop(0, nc, body, (h0_ref[qr], h0_ref[qi]))
        fin_ref[qr] = hr
        fin_ref[qi] = hi

    run(0, 1, True)
    run(2, 3, False)


def _s5b(s, a_l, h0):
    _, nc, w = s.shape
    tl = S5_SCAN_LANES
    vec = pl.BlockSpec((S5_Q, 1, tl), lambda i: (0, 0, i))
    seq = pl.BlockSpec((S5_Q, nc, tl), lambda i: (0, 0, i))
    return pl.pallas_call(
        functools.partial(_s5b_kernel, nc=nc),
        grid=(w // tl,),
        in_specs=[seq, vec, vec],
        out_specs=[seq, vec],
        out_shape=[jax.ShapeDtypeStruct((S5_Q, nc, w), jnp.float32),
                   jax.ShapeDtypeStruct((S5_Q, 1, w), jnp.float32)],
        compiler_params=_cparams(1),
        name="s5_chunk_scan",
    )(s, a_l, h0)


def _s5c_kernel(h_ref, wo_ref, y1_ref, y_ref):
    p = S5_STATE
    for j in range(2):
        hcat = jnp.concatenate([h_ref[q][:, j * p:(j + 1) * p] for q in range(S5_Q)], axis=1)
        y2 = jnp.dot(hcat.astype(jnp.bfloat16), wo_ref[j], preferred_element_type=jnp.float32)
        y_ref[j] = (y1_ref[j] + y2).astype(y_ref.dtype)


def _s5c(hprev, wo, y1):
    g, r, w = y1.shape
    return pl.pallas_call(
        _s5c_kernel,
        grid=(S5_PAIRS,),
        in_specs=[pl.BlockSpec((S5_Q, r, 128), lambda i: (0, 0, i)),
                  pl.BlockSpec((2, S5_Q * S5_STATE, w), lambda i: (i, 0, 0)),
                  pl.BlockSpec((2, r, w), lambda i: (i, 0, 0))],
        out_specs=pl.BlockSpec((2, r, w), lambda i: (i, 0, 0)),
        out_shape=jax.ShapeDtypeStruct((g, r, w), jnp.bfloat16),
        compiler_params=_cparams(1),
        name="s5_state_to_out",
    )(hprev, wo, y1)


def _s5_to_chunks(u):
    b, t, _ = u.shape
    nc = t // S5_L
    x = u.reshape(b, nc, S5_L, S5_GROUPS, S5_GROUP_CH).transpose(3, 1, 0, 2, 4)
    return x.reshape(S5_GROUPS, nc * b, S5_L * S5_GROUP_CH)


def _s5_from_chunks(y, b, t):
    nc = t // S5_L
    y = y.reshape(S5_GROUPS, nc, b, S5_L, S5_GROUP_CH).transpose(2, 1, 3, 0, 4)
    return y.reshape(b * t, S5_CH)


def _s5_state_planes(h0):
    b = h0.shape[0]
    return h0.transpose(1, 4, 0, 2, 3).reshape(S5_Q, b, S5_LANES)


def _s5_core(us, h0s, tables):
    toe, ws_pair, wo_pair, a_l = tables
    xs = [_s5_to_chunks(u) for u in us]
    rows = [x.shape[1] for x in xs]
    y1, s = _s5a(jnp.concatenate(xs, axis=1), toe, ws_pair)
    hps, fins = [], []
    off = 0
    for u, h0, r in zip(us, h0s, rows):
        b, t, _ = u.shape
        nc = t // S5_L
        if h0 is None:
            h0p = jnp.zeros((S5_Q, 1, b * S5_LANES), jnp.float32)
        else:
            h0p = _s5_state_planes(h0.astype(jnp.float32)).reshape(S5_Q, 1, b * S5_LANES)
        hp, fin = _s5b(s[:, off:off + r].reshape(S5_Q, nc, b * S5_LANES), jnp.tile(a_l, (1, 1, b)), h0p)
        hps.append(hp.reshape(S5_Q, r, S5_LANES))
        fins.append(fin.reshape(2, 2, b, S5_GROUPS, S5_STATE).transpose(2, 0, 3, 4, 1))
        off += r
    y = _s5c(jnp.concatenate(hps, axis=1), wo_pair, y1)
    outs, off = [], 0
    for u, r in zip(us, rows):
        b, t, _ = u.shape
        outs.append(_s5_from_chunks(y[:, off:off + r], b, t))
        off += r
    return jnp.concatenate(outs, axis=0), fins


LRU_TC = 256
LRU_HALO = 16


def _lru_kernel(x_ref, gate_ref, cw_ref, cb_ref, wg_ref, bg_ref, sp_ref, h0_ref, out_ref, fin_ref,
                a_s, b_s, hf_s, *, t_len, tc):
    f32 = jnp.float32
    nt = t_len // tc
    w = LRU_W
    nb = tc // 8
    row = lax.broadcasted_iota(jnp.int32, (nb, 8, w), 1)

    def gates(c, d):
        r0 = pl.multiple_of(c * tc, tc)
        lo = pl.multiple_of(jnp.maximum(r0 - LRU_HALO, 0), LRU_HALO)
        hi = pl.multiple_of(jnp.minimum(r0 + tc, t_len - LRU_HALO), LRU_HALO)
        prev = jnp.where(c > 0, x_ref[pl.ds(lo, LRU_HALO), :].astype(f32), 0.0)
        nxt = jnp.where(c < nt - 1, x_ref[pl.ds(hi, LRU_HALO), :].astype(f32), 0.0)
        slab = jnp.concatenate([prev, x_ref[pl.ds(r0, tc), :].astype(f32), nxt], axis=0)
        o = LRU_HALO - LRU_CONV // 2
        xc = cb_ref[...] + sum(cw_ref[k:k + 1, :] * slab[o + k:o + k + tc] for k in range(LRU_CONV))
        g = jnp.dot(xc.astype(jnp.bfloat16), wg_ref[d], preferred_element_type=f32) + bg_ref[d]
        r = _sigmoid(g[:, :w])
        i = _sigmoid(g[:, w:])
        log_a = -sp_ref[d] * r
        a = jnp.exp(log_a)
        b = (jnp.sqrt(1.0 - a * a) * (i * xc)).reshape(nb, 8, w)
        a = a.reshape(nb, 8, w)
        for sh in (1, 2, 4):
            keep, rot = (row >= sh, sh) if d == 0 else (row < 8 - sh, 8 - sh)
            ap = jnp.where(keep, pltpu.roll(a, rot, 1), 1.0)
            bp = jnp.where(keep, pltpu.roll(b, rot, 1), 0.0)
            b = a * bp + b
            a = a * ap
        a_s[...] = a
        b_s[...] = b
        return r0

    def fwd_chunk(c, h):
        r0 = gates(c, 0)

        def block(k, h):
            h8 = a_s[k] * h + b_s[k]
            hf_s[pl.ds(pl.multiple_of(r0 + k * 8, 8), 8), :] = h8
            return h8[7:8, :]

        return lax.fori_loop(0, nb, block, h, unroll=4)

    h = lax.fori_loop(0, nt, fwd_chunk, h0_ref[0, 0:1, :])
    fin_ref[0, 0:1, :] = h

    def bwd_chunk(ci, h):
        r0 = gates(nt - 1 - ci, 1)

        def block(i, h):
            k = nb - 1 - i
            h8 = a_s[k] * h + b_s[k]
            rows = pl.ds(pl.multiple_of(r0 + k * 8, 8), 8)
            hf_s[rows, :] = hf_s[rows, :] + h8
            return h8[0:1, :]

        h = lax.fori_loop(0, nb, block, h, unroll=4)
        sl = pl.ds(r0, tc)
        out_ref[sl, :] = (hf_s[sl, :] * _gelu_tanh(gate_ref[sl, :].astype(f32))).astype(out_ref.dtype)
        return h

    h = lax.fori_loop(0, nt, bwd_chunk, h0_ref[0, 1:2, :])
    fin_ref[0, 1:2, :] = h


def _block_diag(wb):
    n, k, j = wb.shape
    return (wb[:, :, None, :] * jnp.eye(n, dtype=wb.dtype)[:, None, :, None]).reshape(n * k, n * j)


def _lru_tables(conv_w, conv_b, w_a, b_a, w_x, b_x, lam):
    wg = jnp.stack([jnp.concatenate([_block_diag(w_a[d]), _block_diag(w_x[d])], axis=1) for d in range(2)])
    bg = jnp.concatenate([b_a, b_x], axis=-1)[:, None, :]
    sp = (LRU_C * jax.nn.softplus(-lam.astype(jnp.float32)))[:, None, :]
    return conv_w, conv_b[None, :], wg.astype(jnp.bfloat16), bg, sp


def _lru(p, row0, b, t, h0, tables):
    w = LRU_W
    tc = min(LRU_TC, t)
    cw, cb, wg, bg, sp = tables
    full = lambda a: pl.BlockSpec(a.shape, lambda i: (0,) * a.ndim)
    rb = row0 // t
    return pl.pallas_call(
        functools.partial(_lru_kernel, t_len=t, tc=tc),
        grid=(b,),
        in_specs=[pl.BlockSpec((t, w), lambda i: (rb + i, C_LX // w)),
                  pl.BlockSpec((t, w), lambda i: (rb + i, C_LG // w)),
                  full(cw), full(cb), full(wg), full(bg), full(sp),
                  pl.BlockSpec((1, 2, w), lambda i: (i, 0, 0))],
        out_specs=[pl.BlockSpec((t, w), lambda i: (i, 0)),
                   pl.BlockSpec((1, 2, w), lambda i: (i, 0, 0))],
        out_shape=[jax.ShapeDtypeStruct((b * t, w), jnp.bfloat16),
                   jax.ShapeDtypeStruct((b, 2, w), jnp.float32)],
        scratch_shapes=[pltpu.VMEM((tc // 8, 8, w), jnp.float32), pltpu.VMEM((tc // 8, 8, w), jnp.float32),
                        pltpu.VMEM((t, w), jnp.float32)],
        compiler_params=_cparams(1),
        name="rglru",
    )(p, p, cw, cb, wg, bg, sp, h0)


MLA_TM = 512
ATT_DQ = 256


def _mla_prep_kernel(q_ref, kv_ref, kr_ref, tab_ref, gq_ref, gkv_ref, wuq_ref, wukv_ref,
                     qo_ref, ko_ref, vo_ref, ckv_ref):
    f32, bf = jnp.float32, jnp.bfloat16
    tab = tab_ref[...]

    def rope(blk):
        prod = blk * tab
        return prod + pltpu.roll(prod, 64, 1)

    qn = (_rms(q_ref[...].astype(f32)) * gq_ref[...]).astype(bf)
    qq = jnp.dot(qn, wuq_ref[...], preferred_element_type=f32)
    ckv = _rms(kv_ref[...].astype(f32)) * gkv_ref[...]
    ckv_ref[...] = ckv
    kk = jnp.dot(ckv.astype(bf), wukv_ref[...], preferred_element_type=f32)
    lane = lax.broadcasted_iota(jnp.int32, tab.shape, 1)
    kpe = jnp.where(lane < MLA_ROPE, rope(kr_ref[...].astype(f32)), 0.0).astype(bf)
    for h in range(MLA_HEADS):
        o = h * ATT_DQ
        qo_ref[:, o:o + 128] = (qq[:, o:o + 128] * MLA_SCALE).astype(bf)
        qo_ref[:, o + 128:o + 256] = (rope(qq[:, o + 128:o + 256]) * MLA_SCALE).astype(bf)
        ko_ref[:, o:o + 128] = kk[:, o:o + 128].astype(bf)
        ko_ref[:, o + 128:o + 256] = kpe
        vo_ref[:, h * MLA_V:(h + 1) * MLA_V] = kk[:, o + 128:o + 256].astype(bf)


def _mla_weights(w_uq, w_ukv):
    wq = w_uq.reshape(MLA_Q_RANK, MLA_HEADS, MLA_NOPE + MLA_ROPE)
    pe = wq[:, :, MLA_NOPE:].reshape(MLA_Q_RANK, MLA_HEADS * MLA_ROPE)
    pep = _rot_partner_cols(pe, MLA_HEADS, MLA_ROPE).reshape(MLA_Q_RANK, MLA_HEADS, MLA_ROPE)
    wq_ext = jnp.concatenate([wq, pep], axis=-1).reshape(MLA_Q_RANK, MLA_HEADS * ATT_DQ)
    return wq_ext.astype(jnp.bfloat16), w_ukv.astype(jnp.bfloat16)


def _mla_prep(p, tab, gq, gkv, wuq, wukv):
    n = p.shape[0]
    tm = MLA_TM
    full = lambda a: pl.BlockSpec(a.shape, lambda i: (0,) * a.ndim)
    row = lambda width, col: pl.BlockSpec((tm, width), lambda i: (i, col // width))
    return pl.pallas_call(
        _mla_prep_kernel,
        grid=(n // tm,),
        in_specs=[row(MLA_Q_RANK, C_MQ), row(MLA_KV_RANK, C_MKV), row(128, C_KR), row(128, 0),
                  full(gq), full(gkv), full(wuq), full(wukv)],
        out_specs=[row(MLA_HEADS * ATT_DQ, 0), row(MLA_HEADS * ATT_DQ, 0), row(MLA_HEADS * MLA_V, 0),
                   row(MLA_KV_RANK, 0)],
        out_shape=[jax.ShapeDtypeStruct((n, MLA_HEADS * ATT_DQ), jnp.bfloat16),
                   jax.ShapeDtypeStruct((n, MLA_HEADS * ATT_DQ), jnp.bfloat16),
                   jax.ShapeDtypeStruct((n, MLA_HEADS * MLA_V), jnp.bfloat16),
                   jax.ShapeDtypeStruct((n, MLA_KV_RANK), jnp.float32)],
        compiler_params=_cparams(1),
        name="mla_prep",
    )(p, p, p, tab, gq, gkv, wuq, wukv)


ATT_TQ = 1024
ATT_SUB = 256


def _attn_kernel(*refs, two, n_sub):
    dn = (((1,), (1,)), ((), ()))
    f32 = jnp.float32
    if two:
        q_ref, k_ref, v_ref, k2_ref, v2_ref, o_ref = refs
    else:
        q_ref, k_ref, v_ref, o_ref = refs
    rows = q_ref.shape[0] // n_sub
    for part in range(n_sub):
        sl = slice(part * rows, (part + 1) * rows)
        q = q_ref[sl, :]
        s = lax.dot_general(q, k_ref[...], dn, preferred_element_type=f32)
        m = jnp.max(s, axis=-1, keepdims=True)
        if two:
            s2 = lax.dot_general(q, k2_ref[...], dn, preferred_element_type=f32)
            m = jnp.maximum(m, jnp.max(s2, axis=-1, keepdims=True))
        p = jnp.exp(s - m)
        l = jnp.sum(p, axis=-1, keepdims=True)
        o = jnp.dot(p.astype(jnp.bfloat16), v_ref[...], preferred_element_type=f32)
        if two:
            p2 = jnp.exp(s2 - m)
            l = l + jnp.sum(p2, axis=-1, keepdims=True)
            o = o + jnp.dot(p2.astype(jnp.bfloat16), v2_ref[...], preferred_element_type=f32)
        o_ref[sl, :] = (o / l).astype(o_ref.dtype)


def _attention(q, k, v, row0, b, t, k2=None, v2=None):
    tb = min(ATT_TQ, t)
    nq = t // tb
    qb, kb = row0 // tb, row0 // t
    two = k2 is not None
    in_specs = [pl.BlockSpec((tb, ATT_DQ), lambda bi, h, i: (qb + bi * nq + i, h)),
                pl.BlockSpec((t, ATT_DQ), lambda bi, h, i: (kb + bi, h)),
                pl.BlockSpec((t, MLA_V), lambda bi, h, i: (kb + bi, h))]
    args = [q, k, v]
    if two:
        t2 = k2.shape[0] // b
        in_specs += [pl.BlockSpec((t2, ATT_DQ), lambda bi, h, i: (bi, h)),
                     pl.BlockSpec((t2, MLA_V), lambda bi, h, i: (bi, h))]
        args += [k2, v2]
    return pl.pallas_call(
        functools.partial(_attn_kernel, two=two, n_sub=max(tb // ATT_SUB, 1)),
        grid=(b, MLA_HEADS, nq),
        in_specs=in_specs,
        out_specs=pl.BlockSpec((tb, MLA_V), lambda bi, h, i: (bi * nq + i, h)),
        out_shape=jax.ShapeDtypeStruct((b * t, MLA_HEADS * MLA_V), jnp.bfloat16),
        compiler_params=_cparams(3),
        name="mla_attention",
    )(*args)


def _ret_tables():
    f32 = jnp.float32
    c = RET_CHUNK
    log_g = jnp.log1p(-jnp.exp2(-5.0 - jnp.arange(RET_HEADS, dtype=f32)))[:, None, None]
    idx = jnp.arange(c, dtype=f32)
    dec = jnp.exp(jnp.abs(idx[:, None] - idx[None, :])[None] * log_g)
    row = lambda e: jnp.broadcast_to(jnp.exp(e[None, :, None] * log_g), (RET_HEADS, c, c))
    return jnp.stack([dec, row(idx + 1.0), row(c - idx), row(c - 1.0 - idx), row(idx)], axis=0)


RET_HPS = 2


def _ret_kernel(*refs, nc, rope):
    f32, bf = jnp.float32, jnp.bfloat16
    if rope:
        (q_ref, k_ref, v_ref, g_ref, qp_ref, kp_ref, cos_ref, sin_ref, tab_ref, s0_ref, o_ref, fin_ref,
         acc_s, st_s) = refs
    else:
        q_ref, k_ref, v_ref, g_ref, tab_ref, s0_ref, o_ref, fin_ref, acc_s, st_s = refs
    c, dk = RET_CHUNK, RET_DK
    nt_dims, tn_dims = (((1,), (1,)), ((), ())), (((0,), (0,)), ((), ()))

    def chunk(k0, hh):
        sl = pl.ds(pl.multiple_of(k0 * c, c), c)
        cols = slice(hh * dk, (hh + 1) * dk)
        qc, kc = q_ref[sl, cols].astype(f32), k_ref[sl, cols].astype(f32)
        if rope:
            cos, sin = cos_ref[sl, :], sin_ref[sl, :]
            qc = qc * cos + qp_ref[sl, cols].astype(f32) * sin
            kc = kc * cos + kp_ref[sl, cols].astype(f32) * sin
        return sl, cols, qc.astype(bf), kc * (RET_DK ** -0.5), v_ref[sl, cols]

    st_s[...] = s0_ref[0]
    acc_s[...] = jnp.zeros_like(acc_s)

    def body(i, carry):
        for hh in range(RET_HPS):
            dec, xif, xib, zf, zb = (tab_ref[t, hh] for t in range(5))
            g_chunk = xif[c - 1:c, :]
            sl, cols, qc, kc, vc = chunk(i, hh)
            s = st_s[0, hh]
            sc = lax.dot_general(qc, kc.astype(bf), nt_dims, preferred_element_type=f32) * dec
            acc_s[sl, cols] += (jnp.dot(sc.astype(bf), vc, preferred_element_type=f32)
                                + jnp.dot(qc, s.astype(bf), preferred_element_type=f32) * xif)
            st_s[0, hh] = g_chunk * s + lax.dot_general((kc * zf).astype(bf), vc, tn_dims,
                                                        preferred_element_type=f32)
            sl, cols, qc, kc, vc = chunk(nc - 1 - i, hh)
            s = st_s[1, hh]
            acc_s[sl, cols] += jnp.dot(qc, s.astype(bf), preferred_element_type=f32) * xib
            st_s[1, hh] = g_chunk * s + lax.dot_general((kc * zb).astype(bf), vc, tn_dims,
                                                        preferred_element_type=f32)
        return carry

    lax.fori_loop(0, nc, body, 0)
    fin_ref[0] = st_s[...]

    def finish(i, carry):
        sl = pl.ds(pl.multiple_of(i * c, c), c)
        for hh in range(RET_HPS):
            cols = slice(hh * dk, (hh + 1) * dk)
            g = g_ref[sl, cols].astype(f32)
            o_ref[sl, cols] = (_rms(acc_s[sl, cols]) * (g * _sigmoid(g))).astype(o_ref.dtype)
        return carry

    lax.fori_loop(0, nc, finish, 0)


def _retention(p, row0, b, t, s0, tables, rope_tabs=None):
    nc = t // RET_CHUNK
    rb = row0 // t
    rope = rope_tabs is not None
    wd = RET_HPS * RET_DK
    col = lambda c0: pl.BlockSpec((t, wd), lambda bi, h: (rb + bi, c0 // wd + h))
    st = pl.BlockSpec((1, 2, RET_HPS, RET_DK, RET_DV), lambda bi, h: (bi, 0, h, 0, 0))
    in_specs = [col(C_RQ), col(C_RK), col(C_RV), col(C_RG)]
    args = [p, p, p, p]
    if rope:
        tab = pl.BlockSpec((t, RET_DK), lambda bi, h: (0, 0))
        in_specs += [col(C_RQP), col(C_RKP), tab, tab]
        args += [p, p, rope_tabs[0], rope_tabs[1]]
    in_specs += [pl.BlockSpec((5, RET_HPS, RET_CHUNK, RET_CHUNK), lambda bi, h: (0, h, 0, 0)), st]
    args += [tables, s0]
    return pl.pallas_call(
        functools.partial(_ret_kernel, nc=nc, rope=rope),
        grid=(b, RET_HEADS // RET_HPS),
        in_specs=in_specs,
        out_specs=[pl.BlockSpec((t, wd), lambda bi, h: (bi, h)), st],
        out_shape=[jax.ShapeDtypeStruct((b * t, RET_HEADS * RET_DV), jnp.bfloat16),
                   jax.ShapeDtypeStruct((b, 2, RET_HEADS, RET_DK, RET_DV), jnp.float32)],
        scratch_shapes=[pltpu.VMEM((t, wd), jnp.float32), pltpu.VMEM((2, RET_HPS, RET_DK, RET_DV), jnp.float32)],
        compiler_params=_cparams(2, V7X_VMEM_LIMIT_BIG),
        name="retention",
    )(*args)


OUT_TM = 512
ROUTER_PAD = 128


ROW_TILE = (8, D_MODEL // 16)


def _pack_halves(x):
    w = x.shape[1] // 2
    bits = lambda a: lax.bitcast_convert_type(a.astype(jnp.bfloat16).astype(jnp.float32), jnp.int32)
    return bits(x[:, w:]) | lax.shift_right_logical(bits(x[:, :w]), 16)


def _unpack_halves(u):
    lo = lax.bitcast_convert_type(lax.shift_left(u, 16), jnp.float32)
    hi = lax.bitcast_convert_type(u & jnp.int32(-65536), jnp.float32)
    return jnp.concatenate([lo, hi], axis=1)


def _split_bf16(x):
    hi = x.astype(jnp.bfloat16)
    return hi, (x - hi.astype(jnp.float32)).astype(jnp.bfloat16)


def _route(h, whi_ref, wlo_ref, rb_ref):
    f32 = jnp.float32
    hi, lo = _split_bf16(h)
    dn = (((1,), (1,)), ((), ()))
    lt = (lax.dot_general(whi_ref[...], hi, dn, preferred_element_type=f32)
          + lax.dot_general(whi_ref[...], lo, dn, preferred_element_type=f32)
          + lax.dot_general(wlo_ref[...], hi, dn, preferred_element_type=f32))[:N_EXPERTS]
    m = jnp.max(lt, axis=0, keepdims=True)
    e = jnp.exp(lt - m)
    sc = e / jnp.sum(e, axis=0, keepdims=True)
    sel = sc + rb_ref[...][:N_EXPERTS, 0:1]
    rows = lambda a: [a[j:j + 1, :] for j in range(N_EXPERTS)]
    sel_r, sc_r = rows(sel), rows(sc)
    epg = EXPERTS_PER_GROUP

    def top2sum(a, b, c, d):
        h1, l1, h2, l2 = jnp.maximum(a, b), jnp.minimum(a, b), jnp.maximum(c, d), jnp.minimum(c, d)
        return jnp.maximum(h1, h2) + jnp.maximum(jnp.minimum(h1, h2), jnp.maximum(l1, l2))

    gs = [top2sum(*sel_r[g * epg:(g + 1) * epg]) for g in range(N_EXPERT_GROUPS)]
    best, gi = gs[0], jnp.zeros_like(gs[0], dtype=jnp.int32)
    for g in range(1, N_EXPERT_GROUPS):
        upd = gs[g] > best
        gi = jnp.where(upd, g, gi)
        best = jnp.where(upd, gs[g], best)

    def pick(r, j):
        out = r[j]
        for g in range(1, N_EXPERT_GROUPS):
            out = jnp.where(gi == g, r[g * epg + j], out)
        return out

    v = [pick(sel_r, j) for j in range(epg)]
    s = [pick(sc_r, j) for j in range(epg)]

    def argmax_first(vals):
        bv, bi = vals[0], jnp.zeros_like(gi)
        for j in range(1, epg):
            upd = vals[j] > bv
            bi = jnp.where(upd, j, bi)
            bv = jnp.where(upd, vals[j], bv)
        return bi

    i1 = argmax_first(v)
    neg = jnp.float32(-jnp.inf)
    i2 = argmax_first([jnp.where(i1 == j, neg, v[j]) for j in range(epg)])
    take = lambda i: sum(jnp.where(i == j, s[j], 0.0) for j in range(epg))
    w1, w2 = take(i1), take(i2)
    tot = w1 + w2
    return (jnp.concatenate([gi * epg + i1, gi * epg + i2], axis=0),
            jnp.concatenate([w1 / tot, w2 / tot], axis=0))


def _mixout_kernel(ylin_ref, u_ref, ymla_c, ymla_l, ylru_c, ylru_l, yret_c, yret_l, x_c, x_l, mod_ref, d_ref,
                   wglu_ref, wout_ref, g2n_ref, whi_ref, wlo_ref, rb_ref, xo_ref, h2_ref, idx_ref, wgt_ref,
                   *, n_ctx_tiles):
    f32, bf = jnp.float32, jnp.bfloat16
    w = GROUP_W
    is_ctx = pl.program_id(0) < n_ctx_tiles
    pick = lambda c_ref, l_ref: jnp.where(is_ctx, c_ref[...], l_ref[...])
    y = _gelu_tanh(ylin_ref[...].astype(f32) + d_ref[...] * u_ref[...].astype(f32))
    y5 = (y * _sigmoid(jnp.dot(y.astype(bf), wglu_ref[...], preferred_element_type=f32))).astype(bf)
    mix = (jnp.dot(y5, wout_ref[0:w, :], preferred_element_type=f32)
           + jnp.dot(pick(ymla_c, ymla_l), wout_ref[w:2 * w, :], preferred_element_type=f32)
           + jnp.dot(pick(ylru_c, ylru_l), wout_ref[2 * w:3 * w, :], preferred_element_type=f32)
           + jnp.dot(pick(yret_c, yret_l), wout_ref[3 * w:4 * w, :], preferred_element_type=f32))
    x = pick(x_c, x_l) + mod_ref[0, 2:3, :] * mix
    xo_ref[...] = x
    h2 = _rms(x) * g2n_ref[...] * (1.0 + mod_ref[0, 4:5, :]) + mod_ref[0, 3:4, :]
    h2_ref[...] = _pack_halves(h2).reshape(h2_ref.shape)
    idx, wgt = _route(h2, whi_ref, wlo_ref, rb_ref)
    idx_ref[...] = idx
    wgt_ref[...] = wgt


def _router_weights(router_w, router_b):
    d = router_w.shape[0]
    wt = jnp.zeros((ROUTER_PAD, d), jnp.float32).at[:N_EXPERTS].set(router_w.T.astype(jnp.float32))
    whi, wlo = _split_bf16(wt)
    rb = jnp.zeros((ROUTER_PAD, 128), jnp.float32).at[:N_EXPERTS].set(router_b.astype(jnp.float32)[:, None])
    return whi, wlo, rb


def _mixout(ylin, p, ymla, ylru, yret, x, mods, s5_d, wglu, wout, g2n, rw, n_ctx, dec_seq):
    n, d = x[0].shape[0] + x[1].shape[0], x[0].shape[1]
    tm = OUT_TM
    w = GROUP_W
    na = n_ctx // tm
    seq = functools.partial(_seq_of_tile, tm=tm, n_ctx=n_ctx, dec_seq=dec_seq)
    full = lambda a: pl.BlockSpec(a.shape, lambda i: (0,) * a.ndim)
    row = lambda width: pl.BlockSpec((tm, width), lambda i: (i, 0))
    ctx_row, lat_row = _two_part_rows(tm, w, na)
    lanes = pl.BlockSpec((TOP_K, tm), lambda i: (0, i))
    whi, wlo, rb = rw
    return pl.pallas_call(
        functools.partial(_mixout_kernel, n_ctx_tiles=na),
        grid=(n // tm,),
        in_specs=[row(w), row(w), ctx_row, lat_row, ctx_row, lat_row, ctx_row, lat_row, *_two_part_rows(tm, d, na),
                  pl.BlockSpec((1, N_MOD, d), lambda i: (seq(i), 0, 0)),
                  full(s5_d), full(wglu), full(wout), full(g2n), full(whi), full(wlo), full(rb)],
        out_specs=[row(d), pl.BlockSpec((tm,) + ROW_TILE, lambda i: (i, 0, 0)), lanes, lanes],
        out_shape=[jax.ShapeDtypeStruct((n, d), jnp.float32), jax.ShapeDtypeStruct((n,) + ROW_TILE, jnp.int32),
                   jax.ShapeDtypeStruct((TOP_K, n), jnp.int32), jax.ShapeDtypeStruct((TOP_K, n), jnp.float32)],
        compiler_params=_cparams(1, V7X_VMEM_LIMIT_BIG),
        name="mix_out_norm_route",
    )(ylin, p, *ymla, *ylru, *yret, *x, mods, s5_d, wglu, wout, g2n, whi, wlo, rb)


MOE_TM = 256
MOE_NB = 256


def _expert_kernel(te_ref, nt_ref, nv_ref, tok0_ref, tok1_ref, w_ref, dstp_ref, dst_ref, wg_ref, wu_ref, wd_ref,
                   h_hbm, y_hbm, wg_s, wu_s, wd_s, xbuf, xs_s, act_s, ybuf, gsem, ssem):
    i = pl.program_id(0)
    bf = jnp.bfloat16
    tm = xs_s.shape[0]
    nt = nt_ref[0]

    def gather_start(tok_ref, sl, unroll):
        def row(r, carry):
            pltpu.make_async_copy(h_hbm.at[tok_ref[0, 0, r]], xbuf.at[sl, r], gsem.at[sl]).start()
            return carry

        lax.fori_loop(0, tm, row, 0, unroll=unroll)

    def gather_done(sl):
        pltpu.make_async_copy(h_hbm.at[pl.ds(0, tm)], xbuf.at[sl], gsem.at[sl]).wait()

    def scatter_row(d_ref, sl, r):
        return pltpu.make_async_copy(ybuf.at[sl, r], y_hbm.at[d_ref[0, 0, r]], ssem.at[sl])

    def scatter_done(sl, n_rows):
        pltpu.make_async_copy(ybuf.at[sl, pl.ds(0, n_rows)], y_hbm.at[pl.ds(0, n_rows)], ssem.at[sl]).wait()

    def tile(slot):
        n_prev = jnp.where(i > 0, nv_ref[jnp.maximum(i - 1, 0)], 0)
        f = wg_s.shape[1]
        d = wd_s.shape[1]
        up_pieces, down_pieces = 2 * (f // MOE_NB), d // MOE_NB

        def gather_rows(piece):
            for r in range(piece * tm // up_pieces, (piece + 1) * tm // up_pieces):
                pltpu.make_async_copy(h_hbm.at[tok1_ref[0, 0, r]], xbuf.at[1 - slot, r],
                                      gsem.at[1 - slot]).start(priority=1)

        def scatter_rows(piece):
            for r in range(piece * tm // down_pieces, (piece + 1) * tm // down_pieces):
                @pl.when(r < n_prev)
                def _():
                    scatter_row(dstp_ref, 1 - slot, r).start(priority=r % 2)

        gather_done(slot)
        xs_s[...] = _unpack_halves(xbuf[slot].reshape(tm, -1)).astype(bf)
        x = xs_s[...]
        gs, us = [], []
        for c in range(f // MOE_NB):
            cols = slice(c * MOE_NB, (c + 1) * MOE_NB)
            gs.append(jnp.dot(x, wg_s[:, cols], preferred_element_type=jnp.float32))
            gather_rows(2 * c)
            us.append(jnp.dot(x, wu_s[:, cols], preferred_element_type=jnp.float32))
            gather_rows(2 * c + 1)
        g = jnp.concatenate(gs, axis=1)
        u = jnp.concatenate(us, axis=1)
        act_s[...] = ((g * _sigmoid(g)) * u * w_ref[...]).astype(bf)

        @pl.when(nt > 0)
        def _():
            act = act_s[...]
            ys = []
            for c in range(down_pieces):
                ys.append(jnp.dot(act, wd_s[:, c * MOE_NB:(c + 1) * MOE_NB], preferred_element_type=jnp.float32))
                scatter_rows(c)
            y = _pack_halves(jnp.concatenate(ys, axis=1)).reshape(ybuf.shape[1:])

            @pl.when(i > 1)
            def _():
                scatter_done(slot, nv_ref[jnp.maximum(i - 2, 0)])

            ybuf[slot] = y

        @pl.when(i == nt - 1)
        def _():
            def row(r, carry):
                scatter_row(dst_ref, slot, r).start()
                return carry

            lax.fori_loop(0, nv_ref[i], row, 0)

            @pl.when(i > 0)
            def _():
                scatter_done(1 - slot, nv_ref[jnp.maximum(i - 1, 0)])

            scatter_done(slot, nv_ref[i])
            gather_done(1 - slot)

    @pl.when(i < nt)
    def _():
        @pl.when(i == 0)
        def _():
            gather_start(tok0_ref, 0, 8)

        @pl.when((i == 0) | (te_ref[i] != te_ref[jnp.maximum(i - 1, 0)]))
        def _():
            wg_s[...] = wg_ref[0, 0].astype(bf)
            wu_s[...] = wu_ref[0, 0].astype(bf)
            wd_s[...] = wd_ref[0, 0].astype(bf)

        @pl.when(i % 2 == 0)
        def _():
            tile(0)

        @pl.when(i % 2 == 1)
        def _():
            tile(1)


def _experts(h2p, layer, ws, tok, dst, tile_expert, n_tiles_used, n_valid, wg, wu, wd):
    n = h2p.shape[0]
    d = 2 * ROW_TILE[0] * ROW_TILE[1]
    tm = MOE_TM
    n_tiles = tok.shape[0]
    f = wg.shape[-1]
    smem_row = lambda delta: pl.BlockSpec(
        (1, 1, tm), lambda i, te, nt, nv: (jnp.clip(i + delta, 0, n_tiles - 1), 0, 0), memory_space=pltpu.SMEM)
    wspec = lambda a, b: pl.BlockSpec((1, 1, a, b), lambda i, te, nt, nv: (layer, te[i], 0, 0))
    return pl.pallas_call(
        _expert_kernel,
        grid_spec=pltpu.PrefetchScalarGridSpec(
            num_scalar_prefetch=3,
            grid=(n_tiles,),
            in_specs=[smem_row(0), smem_row(1),
                      pl.BlockSpec((tm, 1), lambda i, te, nt, nv: (i, 0)),
                      smem_row(-1), smem_row(0),
                      wspec(d, f), wspec(d, f), wspec(f, d),
                      pl.BlockSpec(memory_space=pl.ANY)],
            out_specs=pl.BlockSpec(memory_space=pl.ANY),
            scratch_shapes=[pltpu.VMEM((d, f), jnp.bfloat16), pltpu.VMEM((d, f), jnp.bfloat16),
                            pltpu.VMEM((f, d), jnp.bfloat16), pltpu.VMEM((2, tm) + ROW_TILE, jnp.int32),
                            pltpu.VMEM((tm, d), jnp.bfloat16), pltpu.VMEM((tm, f), jnp.bfloat16),
                            pltpu.VMEM((2, tm) + ROW_TILE, jnp.int32),
                            pltpu.SemaphoreType.DMA((2,)), pltpu.SemaphoreType.DMA((2,))]),
        out_shape=jax.ShapeDtypeStruct((TOP_K * n,) + ROW_TILE, jnp.int32),
        compiler_params=pltpu.CompilerParams(dimension_semantics=("arbitrary",),
                                             vmem_limit_bytes=V7X_VMEM_LIMIT_BIG, disable_bounds_checks=True),
        name="moe_experts",
    )(tile_expert, n_tiles_used, n_valid, tok, tok, ws, dst, dst, wg, wu, wd, h2p)


def _moe_dispatch(idx, wgt):
    n = idx.shape[1]
    tm = MOE_TM
    n_pairs = TOP_K * n
    m_pad = n_pairs + N_EXPERTS * tm
    n_tiles = m_pad // tm
    e_flat = idx.reshape(-1)
    order = jnp.argsort(e_flat, stable=True).astype(jnp.int32)
    experts = jnp.arange(N_EXPERTS, dtype=jnp.int32)
    counts = jnp.sum((e_flat[None, :] == experts[:, None]).astype(jnp.int32), axis=1)
    starts_raw = jnp.cumsum(counts) - counts
    padded = ((counts + tm - 1) // tm) * tm
    ends = jnp.cumsum(padded)
    starts_pad = ends - padded
    tile_start = jnp.arange(n_tiles, dtype=jnp.int32) * tm
    tile_expert = jnp.minimum(jnp.sum((tile_start[:, None] >= ends[None, :]).astype(jnp.int32), axis=1),
                              N_EXPERTS - 1)
    r = jnp.arange(m_pad, dtype=jnp.int32)
    te_r = jnp.repeat(tile_expert, tm)
    off = r - jnp.take(starts_pad, te_r)
    valid = (off >= 0) & (off < jnp.take(counts, te_r))
    pair = jnp.take(order, jnp.clip(jnp.take(starts_raw, te_r) + off, 0, n_pairs - 1))
    tok = jnp.where(valid, pair % n, 0).reshape(n_tiles, 1, tm)
    dst = jnp.where(valid, pair, 0).reshape(n_tiles, 1, tm)
    ws = jnp.where(valid, jnp.take(wgt.reshape(-1), pair), 0.0)[:, None]
    n_tiles_used = (ends[-1] // tm).astype(jnp.int32).reshape(1)
    n_valid = jnp.sum(valid.reshape(n_tiles, tm), axis=1).astype(jnp.int32)
    return tok, dst, ws, tile_expert, n_tiles_used, n_valid


def _moe_rows(h2p, layer, idx, wgt, wg, wu, wd):
    tok, dst, ws, tile_expert, n_tiles_used, n_valid = _moe_dispatch(idx, wgt)
    return _experts(h2p, layer, ws, tok, dst, tile_expert, n_tiles_used, n_valid, wg, wu, wd)


RES_TM = 512


def _resid_kernel(x_ref, ya_ref, yb_ref, mod_ref, gf_ref, oc_ref, ol_ref, *, final, n_ctx_tiles):
    rows = lambda r: _unpack_halves(r[...].reshape(r.shape[0], -1))
    x = x_ref[...] + mod_ref[0, 5:6, :] * (rows(ya_ref) + rows(yb_ref))
    y = _rms(x) * gf_ref[...] if final else x
    i = pl.program_id(0)

    @pl.when(i < n_ctx_tiles)
    def _():
        oc_ref[...] = y

    @pl.when(i >= n_ctx_tiles)
    def _():
        ol_ref[...] = y


def _resid(x, y2, mods, gf, n_ctx, dec_seq, final):
    n, d = x.shape
    tm = RES_TM
    na = n_ctx // tm
    seq = functools.partial(_seq_of_tile, tm=tm, n_ctx=n_ctx, dec_seq=dec_seq)
    row = pl.BlockSpec((tm, d), lambda i: (i, 0))
    return pl.pallas_call(
        functools.partial(_resid_kernel, final=final, n_ctx_tiles=na),
        grid=(n // tm,),
        in_specs=[row, pl.BlockSpec((tm,) + ROW_TILE, lambda i: (i, 0, 0)),
                  pl.BlockSpec((tm,) + ROW_TILE, lambda i: (n // tm + i, 0, 0)),
                  pl.BlockSpec((1, N_MOD, d), lambda i: (seq(i), 0, 0)),
                  pl.BlockSpec((1, d), lambda i: (0, 0))],
        out_specs=list(_two_part_rows(tm, d, na)),
        out_shape=[jax.ShapeDtypeStruct((n_ctx, d), jnp.float32), jax.ShapeDtypeStruct((n - n_ctx, d), jnp.float32)],
        compiler_params=_cparams(1),
        name="moe_residual_norm",
    )(x, y2, y2, mods, gf)


def kernel(x_prompt, x_sample, c, cache_mla_ckv, cache_mla_kpe, state_s5, state_lru, state_ret,
           c_ctx, w_ada, b_ada, norm1_g, norm2_g, w_in, w_out,
           s5_a_re, s5_a_im, s5_log_dt, s5_b_re, s5_b_im, s5_c_re, s5_c_im, s5_d, s5_w_glu,
           mla_q_norm_g, mla_w_uq, mla_kv_norm_g, mla_w_ukv,
           lru_conv_w, lru_conv_b, lru_w_a, lru_b_a, lru_w_x, lru_b_x, lru_lambda,
           router_w, router_b, moe_w_gate, moe_w_up, moe_w_down, final_norm_g):
    f32, bf = jnp.float32, jnp.bfloat16
    batch, seq_len, d = x_prompt.shape
    dec_batch, dec_seq, _ = x_sample.shape
    past = cache_mla_ckv.shape[2]
    n_ctx, n_lat = batch * seq_len, dec_batch * dec_seq
    depth = w_in.shape[0]

    x = (x_prompt.reshape(n_ctx, d), x_sample.reshape(n_lat, d))
    cvec = jnp.zeros((8, d), f32).at[0].set(c_ctx).at[1:1 + dec_batch].set(c)
    mods_all = _ada(cvec, w_ada, b_ada).reshape(depth, 8, N_MOD, d)

    cos64, sin64 = _rope_tables(dec_seq, MLA_ROPE)
    lat_tab = jnp.tile(jnp.concatenate([cos64, sin64], axis=1), (dec_batch, 1))
    ctx_tab = jnp.concatenate([jnp.ones((n_ctx, MLA_ROPE), f32), jnp.zeros((n_ctx, MLA_ROPE), f32)], axis=1)
    mla_tab = jnp.concatenate([ctx_tab, lat_tab], axis=0)
    ret_rope = _rope_tables(dec_seq, RET_DK)
    ret_tables = _ret_tables()
    rw = _router_weights(router_w, router_b)
    gf = final_norm_g[None, :]

    states = []
    for l in range(depth):
        mods = mods_all[l]
        s5_tab = _s5_tables(s5_a_re[l], s5_a_im[l], s5_log_dt[l], s5_b_re[l], s5_b_im[l], s5_c_re[l], s5_c_im[l])
        lru_tab = _lru_tables(lru_conv_w[l], lru_conv_b[l], lru_w_a[l], lru_b_a[l], lru_w_x[l], lru_b_x[l],
                              lru_lambda[l])
        wuq, wukv = _mla_weights(mla_w_uq[l], mla_w_ukv[l])

        p = _inproj(x, mods, norm1_g[l][None, :], _inproj_weights(w_in[l]), n_ctx, dec_seq)

        u_ctx = p[:n_ctx, C_S5:C_S5 + S5_CH].reshape(batch, seq_len, S5_CH)
        u_lat = p[n_ctx:, C_S5:C_S5 + S5_CH].reshape(dec_batch, dec_seq, S5_CH)
        ylin, s5_fins = _s5_core([u_ctx, u_lat], [None, state_s5[:, l]], s5_tab)

        qo, ko, vo, ckv = _mla_prep(p, mla_tab, mla_q_norm_g[l][None, :], mla_kv_norm_g[l][None, :], wuq, wukv)
        ckv_c = cache_mla_ckv[:, l].reshape(dec_batch * past, MLA_KV_RANK)
        kv_c = _matmul(ckv_c, wukv, out_dtype=bf).reshape(dec_batch * past, MLA_HEADS, 2, 128)
        kpe_c = jnp.broadcast_to(cache_mla_kpe[:, l].reshape(dec_batch * past, 1, MLA_ROPE).astype(bf),
                                 (dec_batch * past, MLA_HEADS, MLA_ROPE))
        k_c = jnp.concatenate([kv_c[:, :, 0], kpe_c, jnp.zeros_like(kpe_c)], axis=-1).reshape(
            dec_batch * past, MLA_HEADS * ATT_DQ)
        v_c = kv_c[:, :, 1].reshape(dec_batch * past, MLA_HEADS * MLA_V)
        ymla = (_attention(qo, ko, vo, 0, batch, seq_len),
                _attention(qo, ko, vo, n_ctx, dec_batch, dec_seq, k_c, v_c))

        ylru_c, lru_fin = _lru(p, 0, batch, seq_len, jnp.zeros((batch, 2, LRU_W), f32), lru_tab)
        ylru_l, _ = _lru(p, n_ctx, dec_batch, dec_seq, state_lru[:, l].astype(f32), lru_tab)
        ylru = (ylru_c, ylru_l)

        yret_c, ret_fin = _retention(p, 0, batch, seq_len,
                                     jnp.zeros((batch, 2, RET_HEADS, RET_DK, RET_DV), f32), ret_tables)
        yret_l, _ = _retention(p, n_ctx, dec_batch, dec_seq, state_ret[:, l].astype(f32), ret_tables, ret_rope)
        yret = (yret_c, yret_l)

        x_mid, h2, idx, wgt = _mixout(ylin, p, ymla, ylru, yret, x, mods, s5_d[l][None, :],
                                      s5_w_glu[l].astype(bf), w_out[l].astype(bf), norm2_g[l][None, :], rw,
                                      n_ctx, dec_seq)
        y2 = _moe_rows(h2, l, idx, wgt, moe_w_gate, moe_w_up, moe_w_down)
        x = _resid(x_mid, y2, mods, gf, n_ctx, dec_seq, final=(l == depth - 1))

        states.append((ckv[:n_ctx].reshape(batch, seq_len, MLA_KV_RANK),
                       p[:n_ctx, C_KR:C_KR + MLA_ROPE].astype(f32).reshape(batch, seq_len, MLA_ROPE),
                       s5_fins[0], lru_fin, ret_fin))

    y_prompt = x[0].reshape(batch, seq_len, d)
    y_sample = x[1].reshape(dec_batch, dec_seq, d)
    new_cache_mla_ckv = jnp.stack([st[0] for st in states], axis=1)
    new_cache_mla_kpe = jnp.stack([st[1] for st in states], axis=1)
    new_state_s5 = jnp.stack([st[2] for st in states], axis=1)
    new_state_lru = jnp.stack([st[3] for st in states], axis=1)
    new_state_ret = jnp.stack([st[4] for st in states], axis=1)
    return (y_prompt, y_sample, new_cache_mla_ckv, new_cache_mla_kpe, new_state_s5, new_state_lru, new_state_ret)
```
